```python
import math, functools
import jax, jax.numpy as jnp
from jax import lax
import numpy as np

D_MODEL = 1024
BATCH = 8
SEQ = 4096
DEPTH = 1
DEC_BATCH = 32
DEC_SEQ = 4
PAST_LEN = 16384
PAGE_SIZE = 128

HEAD_DIM = 64
HEADS_PER_GROUP = 8
DIL_GROUPS = ((128, 1), (512, 4), (2048, 16))
N_DIL = len(DIL_GROUPS)
N_ATTN_HEADS = N_DIL * HEADS_PER_GROUP
ATTN_WIDTH = N_ATTN_HEADS * HEAD_DIM
ATTN_OUT = HEADS_PER_GROUP * HEAD_DIM
ROT_DIM = HEAD_DIM // 4
ROPE_THETA = 500000.0
QUERY_BLOCK = 128
CONV_WIDTH = D_MODEL
CONV_K = 3
N_EXPERT_GROUPS = 4
EXPERTS_PER_GROUP = 4
TOP_K_IN_GROUP = 2
D_EXPERT = 512
RMS_EPS = 1e-6
IN_COLS = 3 * CONV_WIDTH + 3 * ATTN_WIDTH + 2 * D_MODEL

kernel_name = 'hybrid_conv_dilattn_hmoe_step'


def rmsnorm(x, g):
    xf = x.astype(jnp.float32)
    y = xf * lax.rsqrt(jnp.mean(xf * xf, axis=-1, keepdims=True) + RMS_EPS)
    return (y * g.astype(jnp.float32)).astype(x.dtype)


def rope_partial(x, pos):
    half = ROT_DIM // 2
    inv_freq = jnp.power(jnp.float32(ROPE_THETA), -jnp.arange(half, dtype=jnp.float32) / half)
    ang = pos.astype(jnp.float32)[:, None] * inv_freq[None, :]
    cos = jnp.cos(ang)[None, :, None, :]
    sin = jnp.sin(ang)[None, :, None, :]
    xr = x[..., :ROT_DIM].astype(jnp.float32)
    x1, x2 = xr[..., :half], xr[..., half:]
    rot = jnp.concatenate([x1 * cos - x2 * sin, x2 * cos + x1 * sin], axis=-1).astype(x.dtype)
    return jnp.concatenate([rot, x[..., ROT_DIM:]], axis=-1)


def short_conv(u, conv_w, past):
    T = u.shape[1]
    ext = jnp.concatenate([past.astype(u.dtype), u], axis=1)
    y = conv_w[0] * ext[:, 0:T]
    for j in range(1, CONV_K):
        y = y + conv_w[j] * ext[:, j:j + T]
    return y, ext[:, T:]


def dilated_group_attn(q, k_rows, v_rows, q_rows, min_row, window, dilation):
    n_keys = window // dilation + 1
    idx = q_rows[:, None] - dilation * jnp.arange(n_keys, dtype=jnp.int32)[None, :]
    valid = idx >= min_row
    idx = jnp.maximum(idx, 0)
    kg = k_rows[:, idx].astype(jnp.float32)
    vg = v_rows[:, idx].astype(jnp.float32)
    s = jnp.einsum('bqhd,bqjhd->bhqj', q.astype(jnp.float32), kg) * (HEAD_DIM ** -0.5)
    s = jnp.where(valid[None, None], s, -jnp.inf)
    lse = jax.nn.logsumexp(s, axis=-1)
    p = jnp.exp(s - lse[..., None])
    o = jnp.einsum('bhqj,bqjhd->bqhd', p, vg)
    return o, lse


def dilated_mixture(q, ks, vs, rows, mins):
    outs, lses = [], []
    for g, (w, d) in enumerate(DIL_GROUPS):
        o, l = dilated_group_attn(q[:, :, g], ks[g], vs[g], rows[g], mins[g], w, d)
        outs.append(o)
        lses.append(l)
    wts = jax.nn.softmax(jnp.stack(lses), axis=0)
    wts = jnp.transpose(wts, (0, 1, 3, 2))[..., None]
    return jnp.sum(wts * jnp.stack(outs), axis=0)


def prompt_attention(q, k, v):
    B, S = q.shape[0], q.shape[1]
    kps = [jnp.pad(k[:, :, g], ((0, 0), (w, 0), (0, 0), (0, 0))) for g, (w, _) in enumerate(DIL_GROUPS)]
    vps = [jnp.pad(v[:, :, g], ((0, 0), (w, 0), (0, 0), (0, 0))) for g, (w, _) in enumerate(DIL_GROUPS)]
    mins = [w for (w, _) in DIL_GROUPS]

    def block(s0):
        qb = lax.dynamic_slice_in_dim(q, s0, QUERY_BLOCK, axis=1)
        rows = [w + s0 + jnp.arange(QUERY_BLOCK, dtype=jnp.int32) for (w, _) in DIL_GROUPS]
        return dilated_mixture(qb, kps, vps, rows, mins)

    starts = jnp.arange(S // QUERY_BLOCK, dtype=jnp.int32) * QUERY_BLOCK
    ob = lax.map(block, starts)
    o = jnp.moveaxis(ob, 0, 1).reshape(B, S, ATTN_OUT)
    new_kv = []
    for g, (w, _) in enumerate(DIL_GROUPS):
        L = min(w, S)
        new_kv += [k[:, S - L:, g], v[:, S - L:, g]]
    return o, new_kv


def sample_attention(caches, q, k, v):
    B, T = q.shape[0], q.shape[1]
    kcs, vcs, rows, mins, new_kv = [], [], [], [], []
    for g in range(N_DIL):
        kc = jnp.concatenate([caches[2 * g].astype(k.dtype), k[:, :, g]], axis=1)
        vc = jnp.concatenate([caches[2 * g + 1].astype(v.dtype), v[:, :, g]], axis=1)
        L = caches[2 * g].shape[1]
        kcs.append(kc)
        vcs.append(vc)
        rows.append(L + jnp.arange(T, dtype=jnp.int32))
        mins.append(0)
        new_kv += [kc[:, -L:], vc[:, -L:]]
    o = dilated_mixture(q, kcs, vcs, rows, mins)
    return o.reshape(B, T, ATTN_OUT), new_kv


def split_projection(z):
    sizes = (CONV_WIDTH, CONV_WIDTH, CONV_WIDTH, ATTN_WIDTH, ATTN_WIDTH, ATTN_WIDTH, D_MODEL, D_MODEL)
    offs = np.cumsum(sizes)[:-1].tolist()
    return jnp.split(z, offs, axis=-1)


def hier_moe(h, w_rg, b_rg, w_re, b_re, w1, w3, w2):
    B, T, _ = h.shape
    f32 = jnp.float32
    hf = h.astype(f32)
    g_logits = hf @ w_rg.astype(f32) + b_rg.astype(f32)
    g_prob = jax.nn.softmax(g_logits, axis=-1)
    g_onehot = jax.nn.one_hot(jnp.argmax(g_logits, axis=-1), N_EXPERT_GROUPS, dtype=f32)
    g_w = jnp.sum(g_prob * g_onehot, axis=-1, keepdims=True)
    e_logits = (hf @ w_re.astype(f32) + b_re.astype(f32)).reshape(B, T, N_EXPERT_GROUPS, EXPERTS_PER_GROUP)
    e_sel = jnp.sum(e_logits * g_onehot[..., None], axis=2)
    top_v, top_i = lax.top_k(e_sel, TOP_K_IN_GROUP)
    top_w = jax.nn.softmax(top_v, axis=-1) * g_w
    e_w = jnp.sum(jax.nn.one_hot(top_i, EXPERTS_PER_GROUP, dtype=f32) * top_w[..., None], axis=-2)
    comb = (g_onehot[..., None] * e_w[:, :, None, :]).astype(h.dtype)
    out = jnp.zeros_like(h)
    for g in range(N_EXPERT_GROUPS):
        a = jnp.einsum('btd,edf->btef', h, w1[g])
        b = jnp.einsum('btd,edf->btef', h, w3[g])
        hid = jax.nn.silu(a) * b * comb[:, :, g, :, None]
        out = out + jnp.einsum('btef,efd->btd', hid, w2[g])
    return out


def decoder_layer(x, c, pos, conv_past, attn_fn, w_ada, b_ada, norm1_g, norm2_g, w_in, conv_w,
                  w_conv_out, w_attn_out, w_o, w_rg, b_rg, w_re, b_re, w1, w3, w2):
    B, T, _ = x.shape
    mod = (jax.nn.silu(c) @ w_ada + b_ada)[:, None, :]
    sh1, sc1, gt1, sh2, sc2, gt2 = jnp.split(mod, 6, axis=-1)
    h = rmsnorm(x, norm1_g) * (1 + sc1) + sh1
    u_in, gate_c, gate_b, q, k, v, g_conv, g_attn = split_projection(h @ w_in)
    conv_y, conv_state = short_conv(gate_c * u_in, conv_w, conv_past)
    y_conv = (gate_b * conv_y) @ w_conv_out
    q = rope_partial(q.reshape(B, T, N_ATTN_HEADS, HEAD_DIM), pos).reshape(B, T, N_DIL, HEADS_PER_GROUP, HEAD_DIM)
    k = rope_partial(k.reshape(B, T, N_ATTN_HEADS, HEAD_DIM), pos).reshape(B, T, N_DIL, HEADS_PER_GROUP, HEAD_DIM)
    v = v.reshape(B, T, N_DIL, HEADS_PER_GROUP, HEAD_DIM)
    attn_o, kv_state = attn_fn(q, k, v)
    y_attn = attn_o.astype(x.dtype) @ w_attn_out
    mixed = (jax.nn.sigmoid(g_conv) * y_conv + jax.nn.sigmoid(g_attn) * y_attn) @ w_o
    x = x + gt1 * mixed
    h2 = rmsnorm(x, norm2_g) * (1 + sc2) + sh2
    x = x + gt2 * hier_moe(h2, w_rg, b_rg, w_re, b_re, w1, w3, w2)
    return x, kv_state + [conv_state]


def setup_inputs(seed: int = 0) -> dict:
    key = jax.random.key(seed)
    ks = jax.random.split(key, 32)
    f32 = jnp.float32

    def nrm(k, shape, s):
        return jax.random.normal(k, shape, f32) * s

    win = [min(w, PAST_LEN) for (w, _) in DIL_GROUPS]
    G, E = N_EXPERT_GROUPS, EXPERTS_PER_GROUP
    return {
        'x_prompt': nrm(ks[0], (BATCH, SEQ, D_MODEL), 1.0),
        'x_sample': nrm(ks[1], (DEC_BATCH, DEC_SEQ, D_MODEL), 1.0),
        'cache_k1': nrm(ks[2], (DEPTH, DEC_BATCH, win[0], HEADS_PER_GROUP, HEAD_DIM), 1.0),
        'cache_v1': nrm(ks[3], (DEPTH, DEC_BATCH, win[0], HEADS_PER_GROUP, HEAD_DIM), 1.0),
        'cache_k2': nrm(ks[4], (DEPTH, DEC_BATCH, win[1], HEADS_PER_GROUP, HEAD_DIM), 1.0),
        'cache_v2': nrm(ks[5], (DEPTH, DEC_BATCH, win[1], HEADS_PER_GROUP, HEAD_DIM), 1.0),
        'cache_k3': nrm(ks[6], (DEPTH, DEC_BATCH, win[2], HEADS_PER_GROUP, HEAD_DIM), 1.0),
        'cache_v3': nrm(ks[7], (DEPTH, DEC_BATCH, win[2], HEADS_PER_GROUP, HEAD_DIM), 1.0),
        'state_conv': nrm(ks[8], (DEPTH, DEC_BATCH, CONV_K - 1, CONV_WIDTH), 1.0),
        'c_prompt': nrm(ks[9], (BATCH, D_MODEL), 1.0),
        'c_sample': nrm(ks[10], (DEC_BATCH, D_MODEL), 1.0),
        'norm1_g': 1.0 + nrm(ks[11], (DEPTH, D_MODEL), 0.01),
        'norm2_g': 1.0 + nrm(ks[12], (DEPTH, D_MODEL), 0.01),
        'normf_g': 1.0 + nrm(ks[13], (D_MODEL,), 0.01),
        'w_ada': nrm(ks[14], (DEPTH, D_MODEL, 6 * D_MODEL), 0.5 * D_MODEL ** -0.5),
        'b_ada': nrm(ks[15], (DEPTH, 6 * D_MODEL), 0.02),
        'w_in': nrm(ks[16], (DEPTH, D_MODEL, IN_COLS), D_MODEL ** -0.5),
        'conv_w': nrm(ks[17], (DEPTH, CONV_K, CONV_WIDTH), CONV_K ** -0.5),
        'w_conv_out': nrm(ks[18], (DEPTH, CONV_WIDTH, D_MODEL), CONV_WIDTH ** -0.5),
        'w_attn_out': nrm(ks[19], (DEPTH, ATTN_OUT, D_MODEL), ATTN_OUT ** -0.5),
        'w_o': nrm(ks[20], (DEPTH, D_MODEL, D_MODEL), D_MODEL ** -0.5),
        'w_rg': nrm(ks[21], (DEPTH, D_MODEL, G), D_MODEL ** -0.5),
        'b_rg': nrm(ks[22], (DEPTH, G), 0.01),
        'w_re': nrm(ks[23], (DEPTH, D_MODEL, G * E), D_MODEL ** -0.5),
        'b_re': nrm(ks[24], (DEPTH, G * E), 0.01),
        'w1': nrm(ks[25], (DEPTH, G, E, D_MODEL, D_EXPERT), D_MODEL ** -0.5),
        'w3': nrm(ks[26], (DEPTH, G, E, D_MODEL, D_EXPERT), D_MODEL ** -0.5),
        'w2': nrm(ks[27], (DEPTH, G, E, D_EXPERT, D_MODEL), D_EXPERT ** -0.5),
    }


def reference(x_prompt, x_sample, cache_k1, cache_v1, cache_k2, cache_v2, cache_k3, cache_v3, state_conv,
              c_prompt, c_sample, norm1_g, norm2_g, normf_g, w_ada, b_ada, w_in, conv_w, w_conv_out,
              w_attn_out, w_o, w_rg, b_rg, w_re, b_re, w1, w3, w2):
    B, S, _ = x_prompt.shape
    T = x_sample.shape[1]
    pos_p = jnp.arange(S, dtype=jnp.int32)
    pos_s = PAST_LEN + jnp.arange(T, dtype=jnp.int32)
    xp, xs = x_prompt, x_sample
    st_p, st_s = [], []
    for l in range(DEPTH):
        lw = (w_ada[l], b_ada[l], norm1_g[l], norm2_g[l], w_in[l], conv_w[l], w_conv_out[l], w_attn_out[l],
              w_o[l], w_rg[l], b_rg[l], w_re[l], b_re[l], w1[l], w3[l], w2[l])
        conv_zero = jnp.zeros((B, CONV_K - 1, CONV_WIDTH), xp.dtype)
        xp, new_p = decoder_layer(xp, c_prompt, pos_p, conv_zero, prompt_attention, *lw)
        caches_l = (cache_k1[l], cache_v1[l], cache_k2[l], cache_v2[l], cache_k3[l], cache_v3[l])
        xs, new_s = decoder_layer(xs, c_sample, pos_s, state_conv[l],
                                  functools.partial(sample_attention, caches_l), *lw)
        st_p.append(new_p)
        st_s.append(new_s)
    y_prompt = rmsnorm(xp, normf_g)
    y_sample = rmsnorm(xs, normf_g)
    k1_p, v1_p, k2_p, v2_p, k3_p, v3_p, conv_p = [jnp.stack([s[i] for s in st_p]) for i in range(7)]
    k1_s, v1_s, k2_s, v2_s, k3_s, v3_s, conv_s = [jnp.stack([s[i] for s in st_s]) for i in range(7)]
    return (y_prompt, y_sample, k1_p, v1_p, k2_p, v2_p, k3_p, v3_p, conv_p,
            k1_s, v1_s, k2_s, v2_s, k3_s, v3_s, conv_s)
```

```python
import functools

import numpy as np
import jax
import jax.numpy as jnp
from jax import lax
from jax.experimental import pallas as pl
from jax.experimental.pallas import tpu as pltpu

F32 = jnp.float32
BF16 = jnp.bfloat16
HIGHEST = lax.Precision.HIGHEST

D_MODEL = 1024
HEAD_DIM = 64
HEADS_PER_GROUP = 8
GROUP_WIDTH = HEADS_PER_GROUP * HEAD_DIM
DIL_GROUPS = ((128, 1), (512, 4), (2048, 16))
N_DIL = len(DIL_GROUPS)
ATTN_WIDTH = N_DIL * GROUP_WIDTH
ROT_DIM = HEAD_DIM // 4
ROPE_THETA = 500000.0
PAST_LEN = 16384
CONV_K = 3
N_EXPERT_GROUPS = 4
EXPERTS_PER_GROUP = 4
N_EXPERTS = N_EXPERT_GROUPS * EXPERTS_PER_GROUP
D_EXPERT = 512
RMS_EPS = 1e-6
IN_COLS = 3 * D_MODEL + 3 * ATTN_WIDTH + 2 * D_MODEL
OFF_U, OFF_GC, OFF_GB = 0, D_MODEL, 2 * D_MODEL
OFF_Q = 3 * D_MODEL
OFF_K = OFF_Q + ATTN_WIDTH
OFF_V = OFF_K + ATTN_WIDTH
OFF_GCONV = OFF_V + ATTN_WIDTH
OFF_GATTN = OFF_GCONV + D_MODEL

LANES = 128
SUBLANES = 8
CHUNKS_PER_GROUP = GROUP_WIDTH // LANES
KEYS_PER_QUERY = 129
Q_BLOCK = 128
NEG_BIG = -1e30

VMEM_LIMIT = 56 * 1024 * 1024


def _sigmoid(x):
    return 1.0 / (1.0 + jnp.exp(-x))


def _cparams(sem):
    return pltpu.CompilerParams(dimension_semantics=sem, vmem_limit_bytes=VMEM_LIMIT)


def _adaln_kernel(c_ref, w_ref, b_ref, o_ref):
    c = c_ref[...]
    s = c * _sigmoid(c)
    o_ref[...] = jnp.dot(s, w_ref[...], precision=HIGHEST, preferred_element_type=F32) + b_ref[...]


def _adaln(c_all, w_ada, b_ada):
    rows = c_all.shape[0]
    n_col = w_ada.shape[1] // D_MODEL
    return pl.pallas_call(
        _adaln_kernel,
        grid=(n_col,),
        in_specs=[
            pl.BlockSpec((rows, D_MODEL), lambda j: (0, 0)),
            pl.BlockSpec((D_MODEL, D_MODEL), lambda j: (0, j)),
            pl.BlockSpec((1, D_MODEL), lambda j: (0, j)),
        ],
        out_specs=pl.BlockSpec((rows, D_MODEL), lambda j: (0, j)),
        out_shape=jax.ShapeDtypeStruct((rows, w_ada.shape[1]), F32),
        compiler_params=_cparams(("arbitrary",)),
        name="adaln",
    )(c_all, w_ada, b_ada.reshape(1, -1))


def _rope_tables(pos, scale):
    half = ROT_DIM // 2
    inv_freq = jnp.power(jnp.float32(ROPE_THETA), -jnp.arange(half, dtype=F32) / half)
    ang = pos.astype(F32)[:, None] * inv_freq[None, :]
    cos, sin = jnp.cos(ang), jnp.sin(ang)
    lane_in_head = np.arange(LANES) % HEAD_DIM
    freq = lane_in_head % half
    first = lane_in_head < half
    second = (lane_in_head >= half) & (lane_in_head < ROT_DIM)
    a = jnp.where(first | second, cos[:, freq], 1.0)
    bm = jnp.where(first, -sin[:, freq], 0.0)
    bp = jnp.where(second, sin[:, freq], 0.0)
    return jnp.stack([a, bm, bp]) * scale


def _rope_chunk(zc, a, bm, bp):
    return zc * a + pltpu.roll(zc, LANES - ROT_DIM // 2, 1) * bm + pltpu.roll(zc, ROT_DIM // 2, 1) * bp


def _inproj_kernel(x_ref, mod_ref, g1_ref, w_ref, cw_ref, wco_ref, rq_ref, rk_ref, *rest, tm, n_tiles, seq):
    qkv_refs = rest[0:3 * N_DIL]
    pc_ref, sga_ref, cst_ref = rest[3 * N_DIL:3 * N_DIL + 3]
    st_refs = rest[3 * N_DIL + 3:3 * N_DIL + 3 + 2 * N_DIL]
    s_ref, d_ref = rest[-2:]
    i = pl.program_id(1)
    x = x_ref[...]
    var = jnp.mean(x * x, axis=-1, keepdims=True)
    h = (x * lax.rsqrt(var + RMS_EPS)) * (g1_ref[...] * (1.0 + mod_ref[1])) + mod_ref[0]
    hb = h.astype(BF16)

    def proj(lo, width):
        return jnp.dot(hb, w_ref[:, lo:lo + width], preferred_element_type=F32)

    cu = proj(OFF_GC, D_MODEL) * proj(OFF_U, D_MODEL)

    @pl.when(i == 0)
    def _():
        s_ref[0:SUBLANES, :] = jnp.zeros((SUBLANES, D_MODEL), F32)

    s_ref[SUBLANES:SUBLANES + tm, :] = cu
    cw = cw_ref[...]
    conv = (cw[0:1] * s_ref[SUBLANES - 2:SUBLANES - 2 + tm, :]
            + cw[1:2] * s_ref[SUBLANES - 1:SUBLANES - 1 + tm, :]
            + cw[2:3] * cu)
    yc = jnp.dot((proj(OFF_GB, D_MODEL) * conv).astype(BF16), wco_ref[...], preferred_element_type=F32)
    pc_ref[...] = _sigmoid(proj(OFF_GCONV, D_MODEL)) * yc
    sga_ref[...] = _sigmoid(proj(OFF_GATTN, D_MODEL))

    @pl.when(i == n_tiles - 1)
    def _():
        cst_ref[...] = s_ref[tm + SUBLANES - 2:tm + SUBLANES, :]

    s_ref[0:SUBLANES, :] = s_ref[tm:tm + SUBLANES, :]

    zq = proj(OFF_Q, ATTN_WIDTH)
    zk = proj(OFF_K, ATTN_WIDTH)
    zv = proj(OFF_V, ATTN_WIDTH)
    aq, bmq, bpq = rq_ref[0], rq_ref[1], rq_ref[2]
    ak, bmk, bpk = rk_ref[0], rk_ref[1], rk_ref[2]
    n_chunks = ATTN_WIDTH // LANES
    q_chunks, k_chunks, v_chunks = [], [], []
    for c in range(n_chunks):
        sl = slice(c * LANES, (c + 1) * LANES)
        q_chunks.append(_rope_chunk(zq[:, sl], aq, bmq, bpq))
        k_chunks.append(_rope_chunk(zk[:, sl], ak, bmk, bpk))
        v_chunks.append(zv[:, sl])

    for which, chunks in enumerate((q_chunks, k_chunks, v_chunks)):
        for g in range(N_DIL):
            out_ref = qkv_refs[3 * g + which]
            dil = DIL_GROUPS[g][1]
            for cc in range(CHUNKS_PER_GROUP):
                c = g * CHUNKS_PER_GROUP + cc
                sl = slice(cc * LANES, (cc + 1) * LANES)
                if dil == 1:
                    out_ref[0, :, sl] = chunks[c].astype(BF16)
                else:
                    d_ref[c] = chunks[c]
                    for r in range(dil):
                        out_ref[r, :, sl] = d_ref[c, pl.ds(r, tm // dil, stride=dil), :].astype(BF16)

    for g in range(N_DIL):
        kst, vst = st_refs[2 * g], st_refs[2 * g + 1]
        win = min(DIL_GROUPS[g][0], seq)
        if win >= tm:
            cond, r0 = i >= (seq - win) // tm, 0
        else:
            cond, r0 = i == n_tiles - 1, tm - win

        @pl.when(cond)
        def _(g=g, kst=kst, vst=vst, r0=r0):
            for cc in range(CHUNKS_PER_GROUP):
                c = g * CHUNKS_PER_GROUP + cc
                kst[cc * LANES:(cc + 1) * LANES, :] = k_chunks[c][r0:, :].T
                vst[cc * LANES:(cc + 1) * LANES, :] = v_chunks[c][r0:, :].T


def _inproj_prompt(x, mod4, g1, w_in_bf, conv_w, wco_bf, rope_q, rope_k, *, tm):
    bsz, seq, _ = x.shape
    n_tiles = seq // tm
    const2 = lambda b, i: (0, 0)
    tile3 = lambda b, i: (b, i, 0)
    in_specs = [
        pl.BlockSpec((None, tm, D_MODEL), tile3),
        pl.BlockSpec((None, 6, 1, D_MODEL), lambda b, i: (b, 0, 0, 0)),
        pl.BlockSpec((1, D_MODEL), const2),
        pl.BlockSpec((D_MODEL, IN_COLS), const2, pipeline_mode=pl.Buffered(1)),
        pl.BlockSpec((CONV_K, D_MODEL), const2),
        pl.BlockSpec((D_MODEL, D_MODEL), const2, pipeline_mode=pl.Buffered(1)),
        pl.BlockSpec((3, tm, LANES), lambda b, i: (0, i, 0)),
        pl.BlockSpec((3, tm, LANES), lambda b, i: (0, i, 0)),
    ]
    out_shape, out_specs = [], []
    for _, dil in DIL_GROUPS:
        assert tm % (dil * 16) == 0
        for _ in range(3):
            out_shape.append(jax.ShapeDtypeStruct((bsz, dil, seq // dil, GROUP_WIDTH), BF16))
            out_specs.append(pl.BlockSpec((None, dil, tm // dil, GROUP_WIDTH), lambda b, i: (b, 0, i, 0)))
    out_shape += [jax.ShapeDtypeStruct((bsz, seq, D_MODEL), F32), jax.ShapeDtypeStruct((bsz, seq, D_MODEL), F32),
                  jax.ShapeDtypeStruct((bsz, CONV_K - 1, D_MODEL), F32)]
    out_specs += [pl.BlockSpec((None, tm, D_MODEL), tile3), pl.BlockSpec((None, tm, D_MODEL), tile3),
                  pl.BlockSpec((None, CONV_K - 1, D_MODEL), lambda b, i: (b, 0, 0))]
    for win, _ in DIL_GROUPS:
        win = min(win, seq)
        cols = min(win, tm)
        if win >= tm:
            imap = lambda b, i, ft=(seq - win) // tm: (b, 0, jnp.maximum(i - ft, 0))
        else:
            imap = lambda b, i: (b, 0, 0)
        for _ in range(2):
            out_shape.append(jax.ShapeDtypeStruct((bsz, GROUP_WIDTH, win), F32))
            out_specs.append(pl.BlockSpec((None, GROUP_WIDTH, cols), imap))
    return pl.pallas_call(
        functools.partial(_inproj_kernel, tm=tm, n_tiles=n_tiles, seq=seq),
        grid=(bsz, n_tiles),
        in_specs=in_specs,
        out_specs=out_specs,
        out_shape=out_shape,
        scratch_shapes=[pltpu.VMEM((tm + SUBLANES, D_MODEL), F32),
                        pltpu.VMEM((ATTN_WIDTH // LANES, tm, LANES), F32)],
        compiler_params=_cparams(("arbitrary", "arbitrary")),
        name="inproj_prompt",
    )(x, mod4, g1, w_in_bf, conv_w, wco_bf, rope_q, rope_k)


def _attn_kernel(*refs, chunk, has_halo):
    if has_halo:
        q_ref, k_ref, v_ref, kh_ref, vh_ref, o_ref, lse_ref, kbuf, vbuf = refs
    else:
        q_ref, k_ref, v_ref, o_ref, lse_ref, kbuf, vbuf = refs
    c = pl.program_id(2)
    if has_halo:
        kbuf[0:Q_BLOCK, :] = kh_ref[...]
        vbuf[0:Q_BLOCK, :] = vh_ref[...]
    else:
        kbuf[0:Q_BLOCK, :] = jnp.zeros((Q_BLOCK, GROUP_WIDTH), BF16)
        vbuf[0:Q_BLOCK, :] = jnp.zeros((Q_BLOCK, GROUP_WIDTH), BF16)
    kbuf[Q_BLOCK:Q_BLOCK + chunk, :] = k_ref[...]
    vbuf[Q_BLOCK:Q_BLOCK + chunk, :] = v_ref[...]

    row = lax.broadcasted_iota(jnp.int32, (Q_BLOCK, 2 * Q_BLOCK), 0)
    col = lax.broadcasted_iota(jnp.int32, (Q_BLOCK, 2 * Q_BLOCK), 1)
    bias_main = jnp.where(col >= row, jnp.where(col <= row + Q_BLOCK, 0.0, NEG_BIG), NEG_BIG)
    bias_first = jnp.where(col >= Q_BLOCK, bias_main, NEG_BIG)
    lane = lax.broadcasted_iota(jnp.int32, (1, LANES), 1)
    lo_half = lane < HEAD_DIM

    def body(qb, carry):
        r0 = pl.multiple_of(qb * Q_BLOCK, Q_BLOCK)
        kt = kbuf[pl.ds(r0, 2 * Q_BLOCK), :]
        vt = vbuf[pl.ds(r0, 2 * Q_BLOCK), :]
        qt = q_ref[pl.ds(r0, Q_BLOCK), :]
        is_first = jnp.logical_and(qb == 0, c == 0)
        bias = jnp.where(is_first, bias_first, bias_main)
        lse_tile = jnp.zeros((Q_BLOCK, LANES), F32)
        for p in range(CHUNKS_PER_GROUP):
            sl = slice(p * LANES, (p + 1) * LANES)
            qp, kp, vp = qt[:, sl], kt[:, sl], vt[:, sl]
            outs = []
            for hh in range(2):
                head_lanes = lo_half if hh == 0 else jnp.logical_not(lo_half)
                qm = jnp.where(head_lanes, qp, jnp.zeros_like(qp))
                s = lax.dot_general(qm, kp, (((1,), (1,)), ((), ())), preferred_element_type=F32) + bias
                m = jnp.max(s, axis=1, keepdims=True)
                e = jnp.exp(s - m)
                l = jnp.sum(e, axis=1, keepdims=True)
                o = jnp.dot(e.astype(BF16), vp, preferred_element_type=F32)
                outs.append(o * (1.0 / l))
                lse_tile = jnp.where(lane == 2 * p + hh, m + jnp.log(l), lse_tile)
            o_ref[pl.ds(r0, Q_BLOCK), sl] = jnp.where(lo_half, outs[0], outs[1])
        lse_ref[pl.ds(r0, Q_BLOCK), :] = lse_tile
        return carry

    lax.fori_loop(0, chunk // Q_BLOCK, body, 0)


def _attn_prompt_group(q, k, v, g, *, max_chunk):
    bsz, dil, cls_len, _ = q.shape
    chunk = min(max_chunk, cls_len)
    n_chunks = cls_len // chunk
    has_halo = n_chunks > 1
    main = pl.BlockSpec((None, None, chunk, GROUP_WIDTH), lambda b, r, c: (b, r, c, 0))
    in_specs = [main, main, main]
    args = [q, k, v]
    if has_halo:
        per = chunk // Q_BLOCK
        halo = pl.BlockSpec((None, None, Q_BLOCK, GROUP_WIDTH), lambda b, r, c: (b, r, jnp.maximum(c * per - 1, 0), 0))
        in_specs += [halo, halo]
        args += [k, v]
    return pl.pallas_call(
        functools.partial(_attn_kernel, chunk=chunk, has_halo=has_halo),
        grid=(bsz, dil, n_chunks),
        in_specs=in_specs,
        out_specs=[pl.BlockSpec((None, None, chunk, GROUP_WIDTH), lambda b, r, c: (b, r, c, 0)),
                   pl.BlockSpec((None, None, chunk, LANES), lambda b, r, c: (b, r, c, 0))],
        out_shape=[jax.ShapeDtypeStruct((bsz, dil, cls_len, GROUP_WIDTH), F32),
                   jax.ShapeDtypeStruct((bsz, dil, cls_len, LANES), F32)],
        scratch_shapes=[pltpu.VMEM((chunk + Q_BLOCK, GROUP_WIDTH), BF16),
                        pltpu.VMEM((chunk + Q_BLOCK, GROUP_WIDTH), BF16)],
        compiler_params=_cparams(("arbitrary", "arbitrary", "arbitrary")),
        name=f"attn_prompt_g{g}",
    )(*args)


def _split_bf16(x, n):
    parts = []
    r = x
    for _ in range(n):
        p = r.astype(BF16)
        parts.append(p)
        r = r - p.astype(F32)
    return parts


def _post_kernel(o0_ref, o1_ref, o2_ref, l0_ref, l1_ref, l2_ref, pc_ref, sga_ref, x_ref, mod_ref,
                 wao_ref, wo_ref, g2_ref, wr_ref, br_ref, exp_ref,
                 x1_ref, h2_ref, comb_ref, o_scr, l_scr, *, tm, dils, full_precision):
    def mm(a, w_ref):
        if full_precision:
            return jnp.dot(a, w_ref[...], precision=HIGHEST, preferred_element_type=F32)
        return jnp.dot(a.astype(BF16), w_ref[...], preferred_element_type=F32)

    def natural_order(ref, scr, dil, n_chunks):
        if dil == 1:
            return [ref[0, :, c * LANES:(c + 1) * LANES] for c in range(n_chunks)]
        out = []
        for c in range(n_chunks):
            for r in range(dil):
                scr[c, pl.ds(r, tm // dil, stride=dil), :] = ref[r, :, c * LANES:(c + 1) * LANES]
            out.append(scr[c])
        return out

    lses = [natural_order(ref, l_scr.at[g], dils[g], 1)[0] for g, ref in enumerate((l0_ref, l1_ref, l2_ref))]
    mx = jnp.maximum(lses[0], jnp.maximum(lses[1], lses[2]))
    es = [jnp.exp(v - mx) for v in lses]
    inv = 1.0 / (es[0] + es[1] + es[2])
    expand = exp_ref[...]
    attn_o = None
    for g, o_ref in enumerate((o0_ref, o1_ref, o2_ref)):
        w = es[g] * inv
        we = None
        for part in _split_bf16(w, 3 if full_precision else 2):
            t = jnp.dot(part, expand, preferred_element_type=F32)
            we = t if we is None else we + t
        o_nat = jnp.concatenate(natural_order(o_ref, o_scr.at[g], dils[g], CHUNKS_PER_GROUP), axis=1)
        term = we * o_nat
        attn_o = term if attn_o is None else attn_o + term

    y_attn = mm(attn_o, wao_ref)
    mixed = mm(pc_ref[...] + sga_ref[...] * y_attn, wo_ref)
    x1 = x_ref[...] + mod_ref[2] * mixed
    x1_ref[...] = x1
    var = jnp.mean(x1 * x1, axis=-1, keepdims=True)
    h2 = (x1 * lax.rsqrt(var + RMS_EPS)) * (g2_ref[...] * (1.0 + mod_ref[4])) + mod_ref[3]
    h2_ref[...] = h2.astype(h2_ref.dtype)

    lg = jnp.dot(h2, wr_ref[...], precision=HIGHEST, preferred_element_type=F32) + br_ref[...]
    lane_i = lax.broadcasted_iota(jnp.int32, (1, LANES), 1)
    lane = lane_i.astype(F32)
    lane_group = ((lane_i - N_EXPERT_GROUPS) >> 2).astype(F32)
    big = jnp.float32(1e9)
    gl = jnp.where(lane_i < N_EXPERT_GROUPS, lg, NEG_BIG)
    gmax = jnp.max(gl, axis=1, keepdims=True)
    gidx = jnp.min(jnp.where(gl == gmax, lane, big), axis=1, keepdims=True)
    g_w = 1.0 / jnp.sum(jnp.exp(gl - gmax), axis=1, keepdims=True)
    el = jnp.where(lane_group == gidx, lg, NEG_BIG)
    v1 = jnp.max(el, axis=1, keepdims=True)
    i1 = jnp.min(jnp.where(el == v1, lane, big), axis=1, keepdims=True)
    el2 = jnp.where(lane == i1, NEG_BIG, el)
    v2 = jnp.max(el2, axis=1, keepdims=True)
    i2 = jnp.min(jnp.where(el2 == v2, lane, big), axis=1, keepdims=True)
    t = jnp.exp(v2 - v1)
    den = 1.0 / (1.0 + t)
    comb_ref[...] = jnp.where(lane == i1, g_w * den, jnp.where(lane == i2, g_w * (t * den), 0.0))


def _post(o_list, lse_list, pc, sga, x, mod4, wao, wo, g2, wr, br, expand, *, tm, full_precision):
    nb, rows, _ = x.shape
    tiles = rows // tm
    dils = tuple(o.shape[1] for o in o_list)
    tile3 = lambda b, i: (b, i, 0)
    const2 = lambda b, i: (0, 0)
    mod_rows = mod4.shape[2]
    if mod_rows == 1:
        mod_spec = pl.BlockSpec((None, 6, 1, D_MODEL), lambda b, i: (b, 0, 0, 0))
    else:
        mod_spec = pl.BlockSpec((None, 6, tm, D_MODEL), lambda b, i: (b, 0, i, 0))
    cls4 = lambda b, i: (b, 0, i, 0)
    in_specs = (
        [pl.BlockSpec((None, d, tm // d, GROUP_WIDTH), cls4) for d in dils]
        + [pl.BlockSpec((None, d, tm // d, LANES), cls4) for d in dils]
        + [pl.BlockSpec((None, tm, D_MODEL), tile3)] * 3 + [mod_spec]
        + [pl.BlockSpec(wao.shape, const2), pl.BlockSpec(wo.shape, const2), pl.BlockSpec((1, D_MODEL), const2),
           pl.BlockSpec((D_MODEL, LANES), const2), pl.BlockSpec((1, LANES), const2),
           pl.BlockSpec((LANES, GROUP_WIDTH), const2)]
    )
    h2_dtype = F32 if full_precision else BF16
    return pl.pallas_call(
        functools.partial(_post_kernel, tm=tm, dils=dils, full_precision=full_precision),
        grid=(nb, tiles),
        in_specs=in_specs,
        out_specs=[pl.BlockSpec((None, tm, D_MODEL), tile3), pl.BlockSpec((None, tm, D_MODEL), tile3),
                   pl.BlockSpec((None, tm, LANES), tile3)],
        out_shape=[jax.ShapeDtypeStruct((nb, rows, D_MODEL), F32), jax.ShapeDtypeStruct((nb, rows, D_MODEL), h2_dtype),
                   jax.ShapeDtypeStruct((nb, rows, LANES), F32)],
        scratch_shapes=[pltpu.VMEM((N_DIL, CHUNKS_PER_GROUP, tm, LANES), F32), pltpu.VMEM((N_DIL, 1, tm, LANES), F32)],
        compiler_params=_cparams(("arbitrary", "arbitrary")),
        name="post_sample" if full_precision else "post_prompt",
    )(*o_list, *lse_list, pc, sga, x, mod4, wao, wo, g2, wr, br, expand)


def _moe_kernel(h_ref, comb_ref, x1_ref, mod_ref, gf_ref, w1_ref, w3_ref, w2_ref, y_ref, acc_ref):
    e = pl.program_id(2)

    @pl.when(e == 0)
    def _():
        acc_ref[...] = jnp.zeros_like(acc_ref)

    h = h_ref[...].astype(BF16)
    a = jnp.dot(h, w1_ref[...], preferred_element_type=F32)
    b = jnp.dot(h, w3_ref[...], preferred_element_type=F32)
    lane = lax.broadcasted_iota(jnp.int32, (1, LANES), 1)
    cw = jnp.sum(jnp.where(lane == e + N_EXPERT_GROUPS, comb_ref[...], 0.0), axis=1, keepdims=True)
    hid = (a * _sigmoid(a)) * b * cw
    acc_ref[...] += jnp.dot(hid.astype(BF16), w2_ref[...], preferred_element_type=F32)

    @pl.when(e == N_EXPERTS - 1)
    def _():
        x2 = x1_ref[...] + mod_ref[5] * acc_ref[...]
        var = jnp.mean(x2 * x2, axis=-1, keepdims=True)
        y_ref[...] = (x2 * lax.rsqrt(var + RMS_EPS)) * gf_ref[...]


def _moe_dense(h2, comb, x1, mod4, gf, w1_bf, w3_bf, w2_bf, *, tm):
    nb, rows, _ = x1.shape
    tiles = rows // tm
    tile3 = lambda b, i, e: (b, i, 0)
    if mod4.shape[2] == 1:
        mod_spec = pl.BlockSpec((None, 6, 1, D_MODEL), lambda b, i, e: (b, 0, 0, 0))
    else:
        mod_spec = pl.BlockSpec((None, 6, tm, D_MODEL), lambda b, i, e: (b, 0, i, 0))
    return pl.pallas_call(
        _moe_kernel,
        grid=(nb, tiles, N_EXPERTS),
        in_specs=[
            pl.BlockSpec((None, tm, D_MODEL), tile3),
            pl.BlockSpec((None, tm, LANES), tile3),
            pl.BlockSpec((None, tm, D_MODEL), tile3),
            mod_spec,
            pl.BlockSpec((1, D_MODEL), lambda b, i, e: (0, 0)),
            pl.BlockSpec((None, D_MODEL, D_EXPERT), lambda b, i, e: (e, 0, 0)),
            pl.BlockSpec((None, D_MODEL, D_EXPERT), lambda b, i, e: (e, 0, 0)),
            pl.BlockSpec((None, D_EXPERT, D_MODEL), lambda b, i, e: (e, 0, 0)),
        ],
        out_specs=pl.BlockSpec((None, tm, D_MODEL), tile3),
        out_shape=jax.ShapeDtypeStruct((nb, rows, D_MODEL), F32),
        scratch_shapes=[pltpu.VMEM((tm, D_MODEL), F32)],
        compiler_params=_cparams(("arbitrary", "arbitrary", "arbitrary")),
        name=f"moe_dense_{nb * rows}",
    )(h2, comb, x1, mod4, gf, w1_bf, w3_bf, w2_bf)


def _s_inproj_kernel(x_ref, mod_ref, g1_ref, w_ref, z_ref):
    x = x_ref[...]
    var = jnp.mean(x * x, axis=-1, keepdims=True)
    h = (x * lax.rsqrt(var + RMS_EPS)) * (g1_ref[...] * (1.0 + mod_ref[1])) + mod_ref[0]
    z_ref[...] = jnp.dot(h, w_ref[...], precision=HIGHEST, preferred_element_type=F32)


def _s_inproj(x, mod_tok, g1, w_in, *, col_block):
    n = x.shape[0]
    return pl.pallas_call(
        _s_inproj_kernel,
        grid=(IN_COLS // col_block,),
        in_specs=[pl.BlockSpec((n, D_MODEL), lambda j: (0, 0)),
                  pl.BlockSpec((6, n, D_MODEL), lambda j: (0, 0, 0)),
                  pl.BlockSpec((1, D_MODEL), lambda j: (0, 0)),
                  pl.BlockSpec((D_MODEL, col_block), lambda j: (0, j))],
        out_specs=pl.BlockSpec((n, col_block), lambda j: (0, j)),
        out_shape=jax.ShapeDtypeStruct((n, IN_COLS), F32),
        compiler_params=_cparams(("arbitrary",)),
        name="inproj_sample",
    )(x, mod_tok, g1, w_in)


def _s_mid_kernel(z_ref, p0_ref, p1_ref, cw_ref, wco_ref, rq_ref, rk_ref,
                  cu_ref, pc_ref, sga_ref, q_ref, k_ref, v_ref, *, t_len):
    n = z_ref.shape[0]
    cu = z_ref[:, OFF_GC:OFF_GC + D_MODEL] * z_ref[:, OFF_U:OFF_U + D_MODEL]
    cu_ref[...] = cu
    t = lax.broadcasted_iota(jnp.int32, (n, D_MODEL), 0) & (t_len - 1)
    prev1 = jnp.where(t >= 1, pltpu.roll(cu, 1, 0), p1_ref[...])
    prev2 = jnp.where(t >= 2, pltpu.roll(cu, 2, 0), jnp.where(t == 0, p0_ref[...], p1_ref[...]))
    cw = cw_ref[...]
    conv = cw[0:1] * prev2 + cw[1:2] * prev1 + cw[2:3] * cu
    yc = jnp.dot(z_ref[:, OFF_GB:OFF_GB + D_MODEL] * conv, wco_ref[...], precision=HIGHEST,
                 preferred_element_type=F32)
    pc_ref[...] = _sigmoid(z_ref[:, OFF_GCONV:OFF_GCONV + D_MODEL]) * yc
    sga_ref[...] = _sigmoid(z_ref[:, OFF_GATTN:OFF_GATTN + D_MODEL])
    aq, bmq, bpq = rq_ref[0], rq_ref[1], rq_ref[2]
    ak, bmk, bpk = rk_ref[0], rk_ref[1], rk_ref[2]
    for c in range(ATTN_WIDTH // LANES):
        sl = slice(c * LANES, (c + 1) * LANES)
        q_ref[:, sl] = _rope_chunk(z_ref[:, OFF_Q + c * LANES:OFF_Q + (c + 1) * LANES], aq, bmq, bpq)
        k_ref[:, sl] = _rope_chunk(z_ref[:, OFF_K + c * LANES:OFF_K + (c + 1) * LANES], ak, bmk, bpk)
    v_ref[...] = z_ref[:, OFF_V:OFF_V + ATTN_WIDTH]


def _s_mid(z, p0e, p1e, conv_w, w_conv_out, rope_q, rope_k, *, t_len):
    n = z.shape[0]
    assert t_len & (t_len - 1) == 0
    full = lambda shape: pl.BlockSpec(shape, lambda i: (0,) * len(shape))
    out_shape = [jax.ShapeDtypeStruct((n, D_MODEL), F32)] * 3 + [jax.ShapeDtypeStruct((n, ATTN_WIDTH), F32)] * 3
    return pl.pallas_call(
        functools.partial(_s_mid_kernel, t_len=t_len),
        grid=(1,),
        in_specs=[full(z.shape), full(p0e.shape), full(p1e.shape), full(conv_w.shape), full(w_conv_out.shape),
                  full(rope_q.shape), full(rope_k.shape)],
        out_specs=[full((n, D_MODEL))] * 3 + [full((n, ATTN_WIDTH))] * 3,
        out_shape=out_shape,
        compiler_params=_cparams(("arbitrary",)),
        name="mid_sample",
    )(z, p0e, p1e, conv_w, w_conv_out, rope_q, rope_k)


def _head_sum(x):
    return jnp.sum(x.reshape(HEADS_PER_GROUP, HEAD_DIM, x.shape[-1]), axis=1)


def _head_expand(x):
    n = x.shape[-1]
    return jnp.broadcast_to(x[:, None, :], (HEADS_PER_GROUP, HEAD_DIM, n)).reshape(GROUP_WIDTH, n)


def _s_attn_kernel(q_ref, kn_ref, vn_ref, kt_ref, vt_ref, ck_ref, cv_ref, o_ref, lse_ref, ko_ref, vo_ref, s_scr,
                   *, win, dil, t_len):
    n_tiles = win // LANES
    lane = lax.broadcasted_iota(jnp.int32, (1, LANES), 1)
    new_idx = lax.broadcasted_iota(jnp.int32, (1, t_len), 1)
    kn = kn_ref[...]
    vn = vn_ref[...]
    for t in range(t_len):
        qb = jnp.broadcast_to(q_ref[:, t:t + 1], (GROUP_WIDTH, LANES))
        m = None
        for j in range(n_tiles):
            pos = lane + j * LANES
            s = _head_sum(ck_ref[:, j * LANES:(j + 1) * LANES] * qb)
            s = jnp.where(pos >= t, jnp.where(((pos - t) & (dil - 1)) == 0, s, NEG_BIG), NEG_BIG)
            s_scr[:, j * LANES:(j + 1) * LANES] = s
            mj = jnp.max(s, axis=1, keepdims=True)
            m = mj if m is None else jnp.maximum(m, mj)
        s_new = _head_sum(kn * qb[:, 0:t_len])
        s_new = jnp.where(new_idx <= t, jnp.where(((t - new_idx) & (dil - 1)) == 0, s_new, NEG_BIG), NEG_BIG)
        m = jnp.maximum(m, jnp.max(s_new, axis=1, keepdims=True))
        e_new = jnp.exp(s_new - m)
        l = jnp.sum(e_new, axis=1, keepdims=True)
        acc = None
        for j in range(n_tiles):
            e = jnp.exp(s_scr[:, j * LANES:(j + 1) * LANES] - m)
            l = l + jnp.sum(e, axis=1, keepdims=True)
            term = cv_ref[:, j * LANES:(j + 1) * LANES] * _head_expand(e)
            acc = term if acc is None else acc + term
        o = jnp.sum(acc, axis=1, keepdims=True) + jnp.sum(vn * _head_expand(e_new), axis=1, keepdims=True)
        o_ref[:, t:t + 1] = o * _head_expand(1.0 / l)
        lse_ref[:, t:t + 1] = m + jnp.log(l)

    for c_ref, tail_ref, out_ref in ((ck_ref, kt_ref, ko_ref), (cv_ref, vt_ref, vo_ref)):
        rolled = pltpu.roll(c_ref[...], win - t_len, 1)
        if win > LANES:
            out_ref[:, 0:win - LANES] = rolled[:, 0:win - LANES]
        out_ref[:, win - LANES:win] = jnp.where(lane >= LANES - t_len, tail_ref[...], rolled[:, win - LANES:win])


def _s_attn_group(q_t, kn_t, vn_t, k_tail, v_tail, cache_k, cache_v, g):
    bsz, _, t_len = q_t.shape
    win_full, dil = DIL_GROUPS[g]
    win = cache_k.shape[2]
    assert win == win_full and win == (KEYS_PER_QUERY - 1) * dil and win % LANES == 0
    grp = lambda b: (b, g, 0)
    per_b = lambda b: (b, 0, 0)
    return pl.pallas_call(
        functools.partial(_s_attn_kernel, win=win, dil=dil, t_len=t_len),
        grid=(bsz,),
        in_specs=[pl.BlockSpec((None, GROUP_WIDTH, t_len), grp)] * 3
        + [pl.BlockSpec((None, GROUP_WIDTH, LANES), grp)] * 2
        + [pl.BlockSpec((None, GROUP_WIDTH, win), per_b)] * 2,
        out_specs=[pl.BlockSpec((None, GROUP_WIDTH, t_len), per_b),
                   pl.BlockSpec((None, HEADS_PER_GROUP, t_len), per_b),
                   pl.BlockSpec((None, GROUP_WIDTH, win), per_b),
                   pl.BlockSpec((None, GROUP_WIDTH, win), per_b)],
        out_shape=[jax.ShapeDtypeStruct((bsz, GROUP_WIDTH, t_len), F32),
                   jax.ShapeDtypeStruct((bsz, HEADS_PER_GROUP, t_len), F32),
                   jax.ShapeDtypeStruct((bsz, GROUP_WIDTH, win), F32),
                   jax.ShapeDtypeStruct((bsz, GROUP_WIDTH, win), F32)],
        scratch_shapes=[pltpu.VMEM((HEADS_PER_GROUP, win), F32)],
        compiler_params=_cparams(("arbitrary",)),
        name=f"attn_sample_g{g}",
    )(q_t, kn_t, vn_t, k_tail, v_tail, cache_k, cache_v)


TM_INPROJ = 256
TM_POST = 256
TM_MOE = 1024
ATTN_MAX_CHUNK = 1024
S_COL_BLOCK = 512


def _to_state(a_t):
    b, _, length = a_t.shape
    return jnp.transpose(a_t.reshape(b, HEADS_PER_GROUP, HEAD_DIM, length), (0, 3, 1, 2))[None]


def _from_state(a):
    b, length = a.shape[0], a.shape[1]
    return jnp.transpose(a, (0, 2, 3, 1)).reshape(b, GROUP_WIDTH, length)


def kernel(x_prompt, x_sample, cache_k1, cache_v1, cache_k2, cache_v2, cache_k3, cache_v3, state_conv,
           c_prompt, c_sample, norm1_g, norm2_g, normf_g, w_ada, b_ada, w_in, conv_w, w_conv_out,
           w_attn_out, w_o, w_rg, b_rg, w_re, b_re, w1, w3, w2):
    depth = w_in.shape[0]
    assert depth == 1
    bsz, seq, _ = x_prompt.shape
    dbsz, t_len, _ = x_sample.shape
    n_s = dbsz * t_len
    l = 0

    w_in_bf = w_in[l].astype(BF16)
    wco_bf = w_conv_out[l].astype(BF16)
    wao_bf = w_attn_out[l].astype(BF16)
    wo_bf = w_o[l].astype(BF16)
    w1_bf = w1[l].reshape(N_EXPERTS, D_MODEL, D_EXPERT).astype(BF16)
    w3_bf = w3[l].reshape(N_EXPERTS, D_MODEL, D_EXPERT).astype(BF16)
    w2_bf = w2[l].reshape(N_EXPERTS, D_EXPERT, D_MODEL).astype(BF16)
    g1 = norm1_g[l].reshape(1, D_MODEL)
    g2 = norm2_g[l].reshape(1, D_MODEL)
    gf = normf_g.reshape(1, D_MODEL)
    n_route = N_EXPERT_GROUPS + N_EXPERTS
    wr = jnp.pad(jnp.concatenate([w_rg[l], w_re[l]], axis=1), ((0, 0), (0, LANES - n_route)))
    br = jnp.pad(jnp.concatenate([b_rg[l], b_re[l]]), (0, LANES - n_route)).reshape(1, LANES)
    head_of_lane = np.arange(GROUP_WIDTH) // HEAD_DIM
    expand_bf = jnp.asarray((np.arange(LANES)[:, None] == head_of_lane[None, :]).astype(np.float32), dtype=BF16)

    mod = _adaln(jnp.concatenate([c_prompt, c_sample], axis=0), w_ada[l], b_ada[l])
    mod_p4 = mod[:bsz].reshape(bsz, 6, 1, D_MODEL)
    mod_tok = jnp.repeat(mod[bsz:].reshape(dbsz, 1, 6, D_MODEL), t_len, axis=1)
    mod_tok = jnp.transpose(mod_tok.reshape(n_s, 6, D_MODEL), (1, 0, 2))
    mod_s4 = mod_tok[None]

    pos_p = jnp.arange(seq, dtype=jnp.int32)
    rope_q_p = _rope_tables(pos_p, HEAD_DIM ** -0.5)
    rope_k_p = _rope_tables(pos_p, 1.0)
    outs = _inproj_prompt(x_prompt, mod_p4, g1, w_in_bf, conv_w[l], wco_bf, rope_q_p, rope_k_p, tm=TM_INPROJ)
    qkv_p = outs[0:3 * N_DIL]
    pc_p, sga_p, conv_p = outs[3 * N_DIL:3 * N_DIL + 3]
    states_p = outs[3 * N_DIL + 3:]
    o_p, lse_p = [], []
    for g in range(N_DIL):
        o_g, lse_g = _attn_prompt_group(qkv_p[3 * g], qkv_p[3 * g + 1], qkv_p[3 * g + 2], g, max_chunk=ATTN_MAX_CHUNK)
        o_p.append(o_g)
        lse_p.append(lse_g)
    x1_p, h2_p, comb_p = _post(o_p, lse_p, pc_p, sga_p, x_prompt, mod_p4, wao_bf, wo_bf, g2, wr, br, expand_bf,
                               tm=TM_POST, full_precision=False)
    y_p = _moe_dense(h2_p, comb_p, x1_p, mod_p4, gf, w1_bf, w3_bf, w2_bf, tm=TM_MOE)

    pos_s = PAST_LEN + jnp.arange(t_len, dtype=jnp.int32)
    rope_q_s = jnp.tile(_rope_tables(pos_s, HEAD_DIM ** -0.5), (1, dbsz, 1))
    rope_k_s = jnp.tile(_rope_tables(pos_s, 1.0), (1, dbsz, 1))
    xs = x_sample.reshape(n_s, D_MODEL)
    z_s = _s_inproj(xs, mod_tok, g1, w_in[l], col_block=S_COL_BLOCK)
    past = state_conv[l]
    p0e = jnp.repeat(past[:, 0], t_len, axis=0)
    p1e = jnp.repeat(past[:, 1], t_len, axis=0)
    cu_s, pc_s, sga_s, q_s, k_s, v_s = _s_mid(z_s, p0e, p1e, conv_w[l], w_conv_out[l], rope_q_s, rope_k_s,
                                              t_len=t_len)
    to_cols = lambda a: jnp.transpose(a.reshape(dbsz, t_len, ATTN_WIDTH), (0, 2, 1))
    q_t, kn_t, vn_t = to_cols(q_s), to_cols(k_s), to_cols(v_s)
    k_tail = jnp.pad(kn_t, ((0, 0), (0, 0), (LANES - t_len, 0)))
    v_tail = jnp.pad(vn_t, ((0, 0), (0, 0), (LANES - t_len, 0)))
    caches = ((cache_k1, cache_v1), (cache_k2, cache_v2), (cache_k3, cache_v3))
    o_s, lse_s, kv_s = [], [], []
    for g, (ck, cv) in enumerate(caches):
        o_g, lse_g, ko, vo = _s_attn_group(q_t, kn_t, vn_t, k_tail, v_tail, _from_state(ck[l]), _from_state(cv[l]), g)
        o_s.append(jnp.transpose(o_g, (0, 2, 1)).reshape(1, 1, n_s, GROUP_WIDTH))
        lse_rows = jnp.transpose(lse_g, (0, 2, 1)).reshape(n_s, HEADS_PER_GROUP)
        lse_s.append(jnp.pad(lse_rows, ((0, 0), (0, LANES - HEADS_PER_GROUP))).reshape(1, 1, n_s, LANES))
        kv_s += [_to_state(ko), _to_state(vo)]
    x1_s, h2_s, comb_s = _post(o_s, lse_s, pc_s[None], sga_s[None], xs[None], mod_s4, w_attn_out[l], w_o[l], g2, wr, br,
                               expand_bf, tm=n_s, full_precision=True)
    y_s = _moe_dense(h2_s, comb_s, x1_s, mod_s4, gf, w1_bf, w3_bf, w2_bf, tm=n_s)

    conv_s = cu_s.reshape(dbsz, t_len, D_MODEL)[:, t_len - (CONV_K - 1):]
    return (y_p, y_s.reshape(dbsz, t_len, D_MODEL),
            *[_to_state(a) for a in states_p], conv_p.reshape(1, bsz, CONV_K - 1, D_MODEL),
            *kv_s, conv_s.reshape(1, dbsz, CONV_K - 1, D_MODEL))
```

```python
import functools

import numpy as np
import jax
import jax.numpy as jnp
from jax import lax
from jax.experimental import pallas as pl
from jax.experimental.pallas import tpu as pltpu

F32 = jnp.float32
BF16 = jnp.bfloat16
HIGHEST = lax.Precision.HIGHEST

D_MODEL = 1024
HEAD_DIM = 64
HEADS_PER_GROUP = 8
GROUP_WIDTH = HEADS_PER_GROUP * HEAD_DIM
DIL_GROUPS = ((128, 1), (512, 4), (2048, 16))
N_DIL = len(DIL_GROUPS)
ATTN_WIDTH = N_DIL * GROUP_WIDTH
ROT_DIM = HEAD_DIM // 4
ROPE_THETA = 500000.0
PAST_LEN = 16384
CONV_K = 3
N_EXPERT_GROUPS = 4
EXPERTS_PER_GROUP = 4
N_EXPERTS = N_EXPERT_GROUPS * EXPERTS_PER_GROUP
D_EXPERT = 512
RMS_EPS = 1e-6
IN_COLS = 3 * D_MODEL + 3 * ATTN_WIDTH + 2 * D_MODEL
OFF_U, OFF_GC, OFF_GB = 0, D_MODEL, 2 * D_MODEL
OFF_Q = 3 * D_MODEL
OFF_K = OFF_Q + ATTN_WIDTH
OFF_V = OFF_K + ATTN_WIDTH
OFF_GCONV = OFF_V + ATTN_WIDTH
OFF_GATTN = OFF_GCONV + D_MODEL

LANES = 128
SUBLANES = 8
CHUNKS_PER_GROUP = GROUP_WIDTH // LANES
KEYS_PER_QUERY = 129
Q_BLOCK = 128
NEG_BIG = -1e30

VMEM_LIMIT = 56 * 1024 * 1024


def _sigmoid(x):
    return 1.0 / (1.0 + jnp.exp(-x))


def _cparams(sem):
    return pltpu.CompilerParams(dimension_semantics=sem, vmem_limit_bytes=VMEM_LIMIT)


def _adaln_kernel(c_ref, w_ref, b_ref, o_ref):
    c = c_ref[...]
    s = c * _sigmoid(c)
    o_ref[...] = jnp.dot(s, w_ref[...], precision=HIGHEST, preferred_element_type=F32) + b_ref[...]


def _adaln(c_all, w_ada, b_ada):
    rows = c_all.shape[0]
    n_col = w_ada.shape[1] // D_MODEL
    return pl.pallas_call(
        _adaln_kernel,
        grid=(n_col,),
        in_specs=[
            pl.BlockSpec((rows, D_MODEL), lambda j: (0, 0)),
            pl.BlockSpec((D_MODEL, D_MODEL), lambda j: (0, j)),
            pl.BlockSpec((1, D_MODEL), lambda j: (0, j)),
        ],
        out_specs=pl.BlockSpec((rows, D_MODEL), lambda j: (0, j)),
        out_shape=jax.ShapeDtypeStruct((rows, w_ada.shape[1]), F32),
        compiler_params=_cparams(("arbitrary",)),
        name="adaln",
    )(c_all, w_ada, b_ada.reshape(1, -1))


def _rope_tables(pos, scale):
    half = ROT_DIM // 2
    inv_freq = jnp.power(jnp.float32(ROPE_THETA), -jnp.arange(half, dtype=F32) / half)
    ang = pos.astype(F32)[:, None] * inv_freq[None, :]
    cos, sin = jnp.cos(ang), jnp.sin(ang)
    lane_in_head = np.arange(LANES) % HEAD_DIM
    freq = lane_in_head % half
    first = lane_in_head < half
    second = (lane_in_head >= half) & (lane_in_head < ROT_DIM)
    a = jnp.where(first | second, cos[:, freq], 1.0)
    bm = jnp.where(first, -sin[:, freq], 0.0)
    bp = jnp.where(second, sin[:, freq], 0.0)
    return jnp.stack([a, bm, bp]) * scale


def _rope_chunk(zc, a, bm, bp):
    return zc * a + pltpu.roll(zc, LANES - ROT_DIM // 2, 1) * bm + pltpu.roll(zc, ROT_DIM // 2, 1) * bp


def _modulated_norm_bf16(x_ref, mod_ref, g1_ref):
    x = x_ref[...]
    var = jnp.mean(x * x, axis=-1, keepdims=True)
    h = (x * lax.rsqrt(var + RMS_EPS)) * (g1_ref[...] * (1.0 + mod_ref[1])) + mod_ref[0]
    return h.astype(BF16)


def _convproj_kernel(x_ref, mod_ref, g1_ref, wa_ref, wg_ref, cw_ref, wco_ref, pc_ref, sga_ref, cst_ref, s_ref,
                     *, tm, n_tiles):
    i = pl.program_id(1)
    hb = _modulated_norm_bf16(x_ref, mod_ref, g1_ref)

    def proj(w_ref, lo):
        return jnp.dot(hb, w_ref[:, lo:lo + D_MODEL], preferred_element_type=F32)

    cu = proj(wa_ref, OFF_GC) * proj(wa_ref, OFF_U)

    @pl.when(i == 0)
    def _():
        s_ref[0:SUBLANES, :] = jnp.zeros((SUBLANES, D_MODEL), F32)

    s_ref[SUBLANES:SUBLANES + tm, :] = cu
    cw = cw_ref[...]
    conv = (cw[0:1] * s_ref[SUBLANES - 2:SUBLANES - 2 + tm, :]
            + cw[1:2] * s_ref[SUBLANES - 1:SUBLANES - 1 + tm, :]
            + cw[2:3] * cu)
    yc = jnp.dot((proj(wa_ref, OFF_GB) * conv).astype(BF16), wco_ref[...], preferred_element_type=F32)
    pc_ref[...] = (_sigmoid(proj(wg_ref, 0)) * yc).astype(pc_ref.dtype)
    sga_ref[...] = _sigmoid(proj(wg_ref, D_MODEL)).astype(sga_ref.dtype)

    @pl.when(i == n_tiles - 1)
    def _():
        cst_ref[...] = s_ref[tm + SUBLANES - 2:tm + SUBLANES, :]

    s_ref[0:SUBLANES, :] = s_ref[tm:tm + SUBLANES, :]


def _convproj_prompt(x, mod4, g1, wa_bf, wg_bf, conv_w, wco_bf, *, tm):
    bsz, seq, _ = x.shape
    n_tiles = seq // tm
    const2 = lambda b, i: (0, 0)
    tile3 = lambda b, i: (b, i, 0)
    return pl.pallas_call(
        functools.partial(_convproj_kernel, tm=tm, n_tiles=n_tiles),
        grid=(bsz, n_tiles),
        in_specs=[
            pl.BlockSpec((None, tm, D_MODEL), tile3),
            pl.BlockSpec((None, 6, 1, D_MODEL), lambda b, i: (b, 0, 0, 0)),
            pl.BlockSpec((1, D_MODEL), const2),
            pl.BlockSpec(wa_bf.shape, const2, pipeline_mode=pl.Buffered(1)),
            pl.BlockSpec(wg_bf.shape, const2, pipeline_mode=pl.Buffered(1)),
            pl.BlockSpec((CONV_K, D_MODEL), const2),
            pl.BlockSpec((D_MODEL, D_MODEL), const2, pipeline_mode=pl.Buffered(1)),
        ],
        out_specs=[pl.BlockSpec((None, tm, D_MODEL), tile3), pl.BlockSpec((None, tm, D_MODEL), tile3),
                   pl.BlockSpec((None, CONV_K - 1, D_MODEL), lambda b, i: (b, 0, 0))],
        out_shape=[jax.ShapeDtypeStruct((bsz, seq, D_MODEL), BF16), jax.ShapeDtypeStruct((bsz, seq, D_MODEL), BF16),
                   jax.ShapeDtypeStruct((bsz, CONV_K - 1, D_MODEL), F32)],
        scratch_shapes=[pltpu.VMEM((tm + SUBLANES, D_MODEL), F32)],
        compiler_params=_cparams(("arbitrary", "arbitrary")),
        name="convproj_prompt",
    )(x, mod4, g1, wa_bf, wg_bf, conv_w, wco_bf)


def _qkvproj_kernel(x_ref, mod_ref, g1_ref, w_ref, rq_ref, rk_ref, *rest, tm, n_tiles, seq):
    qkv_refs = rest[0:3 * N_DIL]
    st_refs = rest[3 * N_DIL:3 * N_DIL + 2 * N_DIL]
    d_ref = rest[-1]
    i = pl.program_id(1)
    hb = _modulated_norm_bf16(x_ref, mod_ref, g1_ref)

    def proj(lo, width):
        return jnp.dot(hb, w_ref[:, lo:lo + width], preferred_element_type=F32)

    zq = proj(0, ATTN_WIDTH)
    zk = proj(ATTN_WIDTH, ATTN_WIDTH)
    zv = proj(2 * ATTN_WIDTH, ATTN_WIDTH)
    aq, bmq, bpq = rq_ref[0], rq_ref[1], rq_ref[2]
    ak, bmk, bpk = rk_ref[0], rk_ref[1], rk_ref[2]
    n_chunks = ATTN_WIDTH // LANES
    q_chunks, k_chunks, v_chunks = [], [], []
    for c in range(n_chunks):
        sl = slice(c * LANES, (c + 1) * LANES)
        q_chunks.append(_rope_chunk(zq[:, sl], aq, bmq, bpq))
        k_chunks.append(_rope_chunk(zk[:, sl], ak, bmk, bpk))
        v_chunks.append(zv[:, sl])

    for which, chunks in enumerate((q_chunks, k_chunks, v_chunks)):
        for g in range(N_DIL):
            out_ref = qkv_refs[3 * g + which]
            dil = DIL_GROUPS[g][1]
            for cc in range(CHUNKS_PER_GROUP):
                c = g * CHUNKS_PER_GROUP + cc
                sl = slice(cc * LANES, (cc + 1) * LANES)
                if dil == 1:
                    out_ref[0, :, sl] = chunks[c].astype(BF16)
                else:
                    d_ref[c] = chunks[c]
                    for r in range(dil):
                        out_ref[r, :, sl] = d_ref[c, pl.ds(r, tm // dil, stride=dil), :].astype(BF16)

    for g in range(N_DIL):
        kst, vst = st_refs[2 * g], st_refs[2 * g + 1]
        win = min(DIL_GROUPS[g][0], seq)
        if win >= tm:
            cond, r0 = i >= (seq - win) // tm, 0
        else:
            cond, r0 = i == n_tiles - 1, tm - win

        @pl.when(cond)
        def _(g=g, kst=kst, vst=vst, r0=r0):
            for cc in range(CHUNKS_PER_GROUP):
                c = g * CHUNKS_PER_GROUP + cc
                kst[cc * LANES:(cc + 1) * LANES, :] = k_chunks[c][r0:, :].T
                vst[cc * LANES:(cc + 1) * LANES, :] = v_chunks[c][r0:, :].T


def _qkvproj_prompt(x, mod4, g1, wqkv_bf, rope_q, rope_k, *, tm):
    bsz, seq, _ = x.shape
    n_tiles = seq // tm
    const2 = lambda b, i: (0, 0)
    tile3 = lambda b, i: (b, i, 0)
    in_specs = [
        pl.BlockSpec((None, tm, D_MODEL), tile3),
        pl.BlockSpec((None, 6, 1, D_MODEL), lambda b, i: (b, 0, 0, 0)),
        pl.BlockSpec((1, D_MODEL), const2),
        pl.BlockSpec(wqkv_bf.shape, const2, pipeline_mode=pl.Buffered(1)),
        pl.BlockSpec((3, tm, LANES), lambda b, i: (0, i, 0)),
        pl.BlockSpec((3, tm, LANES), lambda b, i: (0, i, 0)),
    ]
    out_shape, out_specs = [], []
    for _, dil in DIL_GROUPS:
        assert tm % (dil * 16) == 0
        for _ in range(3):
            out_shape.append(jax.ShapeDtypeStruct((bsz, dil, seq // dil, GROUP_WIDTH), BF16))
            out_specs.append(pl.BlockSpec((None, dil, tm // dil, GROUP_WIDTH), lambda b, i: (b, 0, i, 0)))
    for win, _ in DIL_GROUPS:
        win = min(win, seq)
        cols = min(win, tm)
        if win >= tm:
            imap = lambda b, i, ft=(seq - win) // tm: (b, 0, jnp.maximum(i - ft, 0))
        else:
            imap = lambda b, i: (b, 0, 0)
        for _ in range(2):
            out_shape.append(jax.ShapeDtypeStruct((bsz, GROUP_WIDTH, win), F32))
            out_specs.append(pl.BlockSpec((None, GROUP_WIDTH, cols), imap))
    return pl.pallas_call(
        functools.partial(_qkvproj_kernel, tm=tm, n_tiles=n_tiles, seq=seq),
        grid=(bsz, n_tiles),
        in_specs=in_specs,
        out_specs=out_specs,
        out_shape=out_shape,
        scratch_shapes=[pltpu.VMEM((ATTN_WIDTH // LANES, tm, LANES), F32)],
        compiler_params=_cparams(("arbitrary", "arbitrary")),
        name="qkvproj_prompt",
    )(x, mod4, g1, wqkv_bf, rope_q, rope_k)


def _attn_kernel(*refs, chunk, has_halo):
    if has_halo:
        q_ref, k_ref, v_ref, kh_ref, vh_ref, o_ref, lse_ref, kbuf, vbuf = refs
    else:
        q_ref, k_ref, v_ref, o_ref, lse_ref, kbuf, vbuf = refs
    c = pl.program_id(2)
    if has_halo:
        kbuf[0:Q_BLOCK, :] = kh_ref[...]
        vbuf[0:Q_BLOCK, :] = vh_ref[...]
    else:
        kbuf[0:Q_BLOCK, :] = jnp.zeros((Q_BLOCK, GROUP_WIDTH), BF16)
        vbuf[0:Q_BLOCK, :] = jnp.zeros((Q_BLOCK, GROUP_WIDTH), BF16)
    kbuf[Q_BLOCK:Q_BLOCK + chunk, :] = k_ref[...]
    vbuf[Q_BLOCK:Q_BLOCK + chunk, :] = v_ref[...]

    row = lax.broadcasted_iota(jnp.int32, (Q_BLOCK, 2 * Q_BLOCK), 0)
    col = lax.broadcasted_iota(jnp.int32, (Q_BLOCK, 2 * Q_BLOCK), 1)
    bias_main = jnp.where(col >= row, jnp.where(col <= row + Q_BLOCK, 0.0, NEG_BIG), NEG_BIG)
    bias_first = jnp.where(col >= Q_BLOCK, bias_main, NEG_BIG)
    lane = lax.broadcasted_iota(jnp.int32, (1, LANES), 1)
    lo_half = lane < HEAD_DIM

    def body(qb, carry):
        r0 = pl.multiple_of(qb * Q_BLOCK, Q_BLOCK)
        kt = kbuf[pl.ds(r0, 2 * Q_BLOCK), :]
        vt = vbuf[pl.ds(r0, 2 * Q_BLOCK), :]
        qt = q_ref[pl.ds(r0, Q_BLOCK), :]
        is_first = jnp.logical_and(qb == 0, c == 0)
        bias = jnp.where(is_first, bias_first, bias_main)
        lse_tile = jnp.zeros((Q_BLOCK, LANES), F32)
        for p in range(CHUNKS_PER_GROUP):
            sl = slice(p * LANES, (p + 1) * LANES)
            qp, kp, vp = qt[:, sl], kt[:, sl], vt[:, sl]
            outs = []
            for hh in range(2):
                head_lanes = lo_half if hh == 0 else jnp.logical_not(lo_half)
                qm = jnp.where(head_lanes, qp, jnp.zeros_like(qp))
                s = lax.dot_general(qm, kp, (((1,), (1,)), ((), ())), preferred_element_type=F32) + bias
                m = jnp.max(s, axis=1, keepdims=True)
                e = jnp.exp(s - m)
                l = jnp.sum(e, axis=1, keepdims=True)
                o = jnp.dot(e.astype(BF16), vp, preferred_element_type=F32)
                outs.append(o * (1.0 / l))
                lse_tile = jnp.where(lane == 2 * p + hh, m + jnp.log(l), lse_tile)
            o_ref[pl.ds(r0, Q_BLOCK), sl] = jnp.where(lo_half, outs[0], outs[1])
        lse_ref[pl.ds(r0, Q_BLOCK), :] = lse_tile
        return carry

    lax.fori_loop(0, chunk // Q_BLOCK, body, 0)


def _attn_prompt_group(q, k, v, g, *, max_chunk):
    bsz, dil, cls_len, _ = q.shape
    chunk = min(max_chunk, cls_len)
    n_chunks = cls_len // chunk
    has_halo = n_chunks > 1
    main = pl.BlockSpec((None, None, chunk, GROUP_WIDTH), lambda b, r, c: (b, r, c, 0))
    in_specs = [main, main, main]
    args = [q, k, v]
    if has_halo:
        per = chunk // Q_BLOCK
        halo = pl.BlockSpec((None, None, Q_BLOCK, GROUP_WIDTH), lambda b, r, c: (b, r, jnp.maximum(c * per - 1, 0), 0))
        in_specs += [halo, halo]
        args += [k, v]
    return pl.pallas_call(
        functools.partial(_attn_kernel, chunk=chunk, has_halo=has_halo),
        grid=(bsz, dil, n_chunks),
        in_specs=in_specs,
        out_specs=[pl.BlockSpec((None, None, chunk, GROUP_WIDTH), lambda b, r, c: (b, r, c, 0)),
                   pl.BlockSpec((None, None, chunk, LANES), lambda b, r, c: (b, r, c, 0))],
        out_shape=[jax.ShapeDtypeStruct((bsz, dil, cls_len, GROUP_WIDTH), F32),
                   jax.ShapeDtypeStruct((bsz, dil, cls_len, LANES), F32)],
        scratch_shapes=[pltpu.VMEM((chunk + Q_BLOCK, GROUP_WIDTH), BF16),
                        pltpu.VMEM((chunk + Q_BLOCK, GROUP_WIDTH), BF16)],
        compiler_params=_cparams(("arbitrary", "arbitrary", "arbitrary")),
        name=f"attn_prompt_g{g}",
    )(*args)


def _split_bf16(x, n):
    parts = []
    r = x
    for _ in range(n):
        p = r.astype(BF16)
        parts.append(p)
        r = r - p.astype(F32)
    return parts


def _post_kernel(o0_ref, o1_ref, o2_ref, l0_ref, l1_ref, l2_ref, pc_ref, sga_ref, x_ref, mod_ref,
                 wao_ref, wo_ref, g2_ref, wr_ref, wrh_ref, wrl_ref, br_ref, exp_ref,
                 x1_ref, h2_ref, comb_ref, o_scr, l_scr, *, tm, dils, full_precision):
    def mm(a, w_ref):
        if full_precision:
            return jnp.dot(a, w_ref[...], precision=HIGHEST, preferred_element_type=F32)
        return jnp.dot(a.astype(BF16), w_ref[...], preferred_element_type=F32)

    def natural_order(ref, scr, dil, n_chunks):
        if dil == 1:
            return [ref[0, :, c * LANES:(c + 1) * LANES] for c in range(n_chunks)]
        out = []
        for c in range(n_chunks):
            for r in range(dil):
                scr[c, pl.ds(r, tm // dil, stride=dil), :] = ref[r, :, c * LANES:(c + 1) * LANES]
            out.append(scr[c])
        return out

    lses = [natural_order(ref, l_scr.at[g], dils[g], 1)[0] for g, ref in enumerate((l0_ref, l1_ref, l2_ref))]
    mx = jnp.maximum(lses[0], jnp.maximum(lses[1], lses[2]))
    es = [jnp.exp(v - mx) for v in lses]
    inv = 1.0 / (es[0] + es[1] + es[2])
    expand = exp_ref[...]
    attn_o = None
    for g, o_ref in enumerate((o0_ref, o1_ref, o2_ref)):
        w = es[g] * inv
        we = None
        for part in _split_bf16(w, 3 if full_precision else 2):
            t = jnp.dot(part, expand, preferred_element_type=F32)
            we = t if we is None else we + t
        o_nat = jnp.concatenate(natural_order(o_ref, o_scr.at[g], dils[g], CHUNKS_PER_GROUP), axis=1)
        term = we * o_nat
        attn_o = term if attn_o is None else attn_o + term

    y_attn = mm(attn_o, wao_ref)
    mixed = mm(pc_ref[...] + sga_ref[...] * y_attn, wo_ref)
    x1 = x_ref[...] + mod_ref[2] * mixed
    x1_ref[...] = x1
    var = jnp.mean(x1 * x1, axis=-1, keepdims=True)
    h2 = (x1 * lax.rsqrt(var + RMS_EPS)) * (g2_ref[...] * (1.0 + mod_ref[4])) + mod_ref[3]
    h2_ref[...] = h2.astype(h2_ref.dtype)

    if full_precision:
        lg = jnp.dot(h2, wr_ref[...], precision=HIGHEST, preferred_element_type=F32) + br_ref[...]
    else:
        h_hi, h_lo = _split_bf16(h2, 2)
        lg = (jnp.dot(h_hi, wrh_ref[...], preferred_element_type=F32)
              + jnp.dot(h_lo, wrh_ref[...], preferred_element_type=F32)
              + jnp.dot(h_hi, wrl_ref[...], preferred_element_type=F32)) + br_ref[...]
    lane_i = lax.broadcasted_iota(jnp.int32, (1, LANES), 1)
    lane = lane_i.astype(F32)
    lane_group = ((lane_i - N_EXPERT_GROUPS) >> 2).astype(F32)
    big = jnp.float32(1e9)
    gl = jnp.where(lane_i < N_EXPERT_GROUPS, lg, NEG_BIG)
    gmax = jnp.max(gl, axis=1, keepdims=True)
    gidx = jnp.min(jnp.where(gl == gmax, lane, big), axis=1, keepdims=True)
    g_w = 1.0 / jnp.sum(jnp.exp(gl - gmax), axis=1, keepdims=True)
    el = jnp.where(lane_group == gidx, lg, NEG_BIG)
    v1 = jnp.max(el, axis=1, keepdims=True)
    i1 = jnp.min(jnp.where(el == v1, lane, big), axis=1, keepdims=True)
    el2 = jnp.where(lane == i1, NEG_BIG, el)
    v2 = jnp.max(el2, axis=1, keepdims=True)
    i2 = jnp.min(jnp.where(el2 == v2, lane, big), axis=1, keepdims=True)
    t = jnp.exp(v2 - v1)
    den = 1.0 / (1.0 + t)
    comb_ref[...] = jnp.where(lane == i1, g_w * den, jnp.where(lane == i2, g_w * (t * den), 0.0))


def _post(o_list, lse_list, pc, sga, x, mod4, wao, wo, g2, wr, br, expand, *, tm, full_precision):
    nb, rows, _ = x.shape
    tiles = rows // tm
    dils = tuple(o.shape[1] for o in o_list)
    tile3 = lambda b, i: (b, i, 0)
    const2 = lambda b, i: (0, 0)
    mod_rows = mod4.shape[2]
    if mod_rows == 1:
        mod_spec = pl.BlockSpec((None, 6, 1, D_MODEL), lambda b, i: (b, 0, 0, 0))
    else:
        mod_spec = pl.BlockSpec((None, 6, tm, D_MODEL), lambda b, i: (b, 0, i, 0))
    cls4 = lambda b, i: (b, 0, i, 0)
    in_specs = (
        [pl.BlockSpec((None, d, tm // d, GROUP_WIDTH), cls4) for d in dils]
        + [pl.BlockSpec((None, d, tm // d, LANES), cls4) for d in dils]
        + [pl.BlockSpec((None, tm, D_MODEL), tile3)] * 3 + [mod_spec]
        + [pl.BlockSpec(wao.shape, const2), pl.BlockSpec(wo.shape, const2), pl.BlockSpec((1, D_MODEL), const2)]
        + [pl.BlockSpec((D_MODEL, LANES), const2)] * 3
        + [pl.BlockSpec((1, LANES), const2), pl.BlockSpec((LANES, GROUP_WIDTH), const2)]
    )
    wr_hi = wr.astype(BF16)
    wr_lo = (wr - wr_hi.astype(F32)).astype(BF16)
    h2_dtype = F32 if full_precision else BF16
    return pl.pallas_call(
        functools.partial(_post_kernel, tm=tm, dils=dils, full_precision=full_precision),
        grid=(nb, tiles),
        in_specs=in_specs,
        out_specs=[pl.BlockSpec((None, tm, D_MODEL), tile3), pl.BlockSpec((None, tm, D_MODEL), tile3),
                   pl.BlockSpec((None, tm, LANES), tile3)],
        out_shape=[jax.ShapeDtypeStruct((nb, rows, D_MODEL), F32), jax.ShapeDtypeStruct((nb, rows, D_MODEL), h2_dtype),
                   jax.ShapeDtypeStruct((nb, rows, LANES), F32)],
        scratch_shapes=[pltpu.VMEM((N_DIL, CHUNKS_PER_GROUP, tm, LANES), F32), pltpu.VMEM((N_DIL, 1, tm, LANES), F32)],
        compiler_params=_cparams(("arbitrary", "arbitrary")),
        name="post_sample" if full_precision else "post_prompt",
    )(*o_list, *lse_list, pc, sga, x, mod4, wao, wo, g2, wr, wr_hi, wr_lo, br, expand)


def _moe_kernel(h_ref, comb_ref, x1_ref, mod_ref, gf_ref, w1_ref, w3_ref, w2_ref, y_ref, acc_ref):
    e = pl.program_id(2)

    @pl.when(e == 0)
    def _():
        acc_ref[...] = jnp.zeros_like(acc_ref)

    h = h_ref[...].astype(BF16)
    a = jnp.dot(h, w1_ref[...], preferred_element_type=F32)
    b = jnp.dot(h, w3_ref[...], preferred_element_type=F32)
    lane = lax.broadcasted_iota(jnp.int32, (1, LANES), 1)
    cw = jnp.sum(jnp.where(lane == e + N_EXPERT_GROUPS, comb_ref[...], 0.0), axis=1, keepdims=True)
    hid = (a * _sigmoid(a)) * b * cw
    acc_ref[...] += jnp.dot(hid.astype(BF16), w2_ref[...], preferred_element_type=F32)

    @pl.when(e == N_EXPERTS - 1)
    def _():
        x2 = x1_ref[...] + mod_ref[5] * acc_ref[...]
        var = jnp.mean(x2 * x2, axis=-1, keepdims=True)
        y_ref[...] = (x2 * lax.rsqrt(var + RMS_EPS)) * gf_ref[...]


def _moe_dense(h2, comb, x1, mod4, gf, w1_bf, w3_bf, w2_bf, *, tm):
    nb, rows, _ = x1.shape
    tiles = rows // tm
    tile3 = lambda b, i, e: (b, i, 0)
    if mod4.shape[2] == 1:
        mod_spec = pl.BlockSpec((None, 6, 1, D_MODEL), lambda b, i, e: (b, 0, 0, 0))
    else:
        mod_spec = pl.BlockSpec((None, 6, tm, D_MODEL), lambda b, i, e: (b, 0, i, 0))
    return pl.pallas_call(
        _moe_kernel,
        grid=(nb, tiles, N_EXPERTS),
        in_specs=[
            pl.BlockSpec((None, tm, D_MODEL), tile3),
            pl.BlockSpec((None, tm, LANES), tile3),
            pl.BlockSpec((None, tm, D_MODEL), tile3),
            mod_spec,
            pl.BlockSpec((1, D_MODEL), lambda b, i, e: (0, 0)),
            pl.BlockSpec((None, D_MODEL, D_EXPERT), lambda b, i, e: (e, 0, 0)),
            pl.BlockSpec((None, D_MODEL, D_EXPERT), lambda b, i, e: (e, 0, 0)),
            pl.BlockSpec((None, D_EXPERT, D_MODEL), lambda b, i, e: (e, 0, 0)),
        ],
        out_specs=pl.BlockSpec((None, tm, D_MODEL), tile3),
        out_shape=jax.ShapeDtypeStruct((nb, rows, D_MODEL), F32),
        scratch_shapes=[pltpu.VMEM((tm, D_MODEL), F32)],
        compiler_params=_cparams(("arbitrary", "arbitrary", "arbitrary")),
        name=f"moe_dense_{nb * rows}",
    )(h2, comb, x1, mod4, gf, w1_bf, w3_bf, w2_bf)


def _s_inproj_kernel(x_ref, mod_ref, g1_ref, w_ref, z_ref):
    x = x_ref[...]
    var = jnp.mean(x * x, axis=-1, keepdims=True)
    h = (x * lax.rsqrt(var + RMS_EPS)) * (g1_ref[...] * (1.0 + mod_ref[1])) + mod_ref[0]
    z_ref[...] = jnp.dot(h, w_ref[...], precision=HIGHEST, preferred_element_type=F32)


def _s_inproj(x, mod_tok, g1, w_in, *, col_block):
    n = x.shape[0]
    return pl.pallas_call(
        _s_inproj_kernel,
        grid=(IN_COLS // col_block,),
        in_specs=[pl.BlockSpec((n, D_MODEL), lambda j: (0, 0)),
                  pl.BlockSpec((6, n, D_MODEL), lambda j: (0, 0, 0)),
                  pl.BlockSpec((1, D_MODEL), lambda j: (0, 0)),
                  pl.BlockSpec((D_MODEL, col_block), lambda j: (0, j))],
        out_specs=pl.BlockSpec((n, col_block), lambda j: (0, j)),
        out_shape=jax.ShapeDtypeStruct((n, IN_COLS), F32),
        compiler_params=_cparams(("arbitrary",)),
        name="inproj_sample",
    )(x, mod_tok, g1, w_in)


def _s_mid_kernel(z_ref, p0_ref, p1_ref, cw_ref, wco_ref, rq_ref, rk_ref,
                  cu_ref, pc_ref, sga_ref, q_ref, k_ref, v_ref, *, t_len):
    n = z_ref.shape[0]
    cu = z_ref[:, OFF_GC:OFF_GC + D_MODEL] * z_ref[:, OFF_U:OFF_U + D_MODEL]
    cu_ref[...] = cu
    t = lax.broadcasted_iota(jnp.int32, (n, D_MODEL), 0) & (t_len - 1)
    prev1 = jnp.where(t >= 1, pltpu.roll(cu, 1, 0), p1_ref[...])
    prev2 = jnp.where(t >= 2, pltpu.roll(cu, 2, 0), jnp.where(t == 0, p0_ref[...], p1_ref[...]))
    cw = cw_ref[...]
    conv = cw[0:1] * prev2 + cw[1:2] * prev1 + cw[2:3] * cu
    yc = jnp.dot(z_ref[:, OFF_GB:OFF_GB + D_MODEL] * conv, wco_ref[...], precision=HIGHEST,
                 preferred_element_type=F32)
    pc_ref[...] = _sigmoid(z_ref[:, OFF_GCONV:OFF_GCONV + D_MODEL]) * yc
    sga_ref[...] = _sigmoid(z_ref[:, OFF_GATTN:OFF_GATTN + D_MODEL])
    aq, bmq, bpq = rq_ref[0], rq_ref[1], rq_ref[2]
    ak, bmk, bpk = rk_ref[0], rk_ref[1], rk_ref[2]
    for c in range(ATTN_WIDTH // LANES):
        sl = slice(c * LANES, (c + 1) * LANES)
        q_ref[:, sl] = _rope_chunk(z_ref[:, OFF_Q + c * LANES:OFF_Q + (c + 1) * LANES], aq, bmq, bpq)
        k_ref[:, sl] = _rope_chunk(z_ref[:, OFF_K + c * LANES:OFF_K + (c + 1) * LANES], ak, bmk, bpk)
    v_ref[...] = z_ref[:, OFF_V:OFF_V + ATTN_WIDTH]


def _s_mid(z, p0e, p1e, conv_w, w_conv_out, rope_q, rope_k, *, t_len):
    n = z.shape[0]
    assert t_len & (t_len - 1) == 0
    full = lambda shape: pl.BlockSpec(shape, lambda i: (0,) * len(shape))
    out_shape = [jax.ShapeDtypeStruct((n, D_MODEL), F32)] * 3 + [jax.ShapeDtypeStruct((n, ATTN_WIDTH), F32)] * 3
    return pl.pallas_call(
        functools.partial(_s_mid_kernel, t_len=t_len),
        grid=(1,),
        in_specs=[full(z.shape), full(p0e.shape), full(p1e.shape), full(conv_w.shape), full(w_conv_out.shape),
                  full(rope_q.shape), full(rope_k.shape)],
        out_specs=[full((n, D_MODEL))] * 3 + [full((n, ATTN_WIDTH))] * 3,
        out_shape=out_shape,
        compiler_params=_cparams(("arbitrary",)),
        name="mid_sample",
    )(z, p0e, p1e, conv_w, w_conv_out, rope_q, rope_k)


def _head_sum(x):
    return jnp.sum(x.reshape(HEADS_PER_GROUP, HEAD_DIM, x.shape[-1]), axis=1)


def _head_expand(x):
    n = x.shape[-1]
    return jnp.broadcast_to(x[:, None, :], (HEADS_PER_GROUP, HEAD_DIM, n)).reshape(GROUP_WIDTH, n)


def _s_attn_disjoint(q_ref, kn, vn, ck_ref, cv_ref, o_ref, lse_ref, s_scr, lane, new_idx, *, n_tiles, dil, t_len):
    cls = lane & (dil - 1)
    q_all = q_ref[...]
    qsel = jnp.zeros((GROUP_WIDTH, LANES), F32)
    for t in range(t_len):
        qsel = jnp.where(cls == t, jnp.broadcast_to(q_all[:, t:t + 1], (GROUP_WIDTH, LANES)), qsel)
    smax = None
    for j in range(n_tiles):
        s = jnp.where(cls < t_len, _head_sum(ck_ref[:, j * LANES:(j + 1) * LANES] * qsel), NEG_BIG)
        s_scr[:, j * LANES:(j + 1) * LANES] = s
        smax = s if smax is None else jnp.maximum(smax, s)
    s_new = _head_sum(kn * q_all)
    m_cols = []
    m_lane = jnp.zeros((HEADS_PER_GROUP, LANES), F32)
    m_new = jnp.zeros((HEADS_PER_GROUP, t_len), F32)
    for t in range(t_len):
        mt = jnp.maximum(jnp.max(jnp.where(cls == t, smax, NEG_BIG), axis=1, keepdims=True), s_new[:, t:t + 1])
        m_cols.append(mt)
        m_lane = jnp.where(cls == t, mt, m_lane)
        m_new = jnp.where(new_idx == t, mt, m_new)
    esum = jnp.zeros((HEADS_PER_GROUP, LANES), F32)
    for j in range(n_tiles):
        e = jnp.exp(s_scr[:, j * LANES:(j + 1) * LANES] - m_lane)
        s_scr[:, j * LANES:(j + 1) * LANES] = e
        esum = esum + e
    e_new = jnp.exp(s_new - m_new)
    l_cols = []
    inv_lane = jnp.zeros((HEADS_PER_GROUP, LANES), F32)
    inv_new = jnp.zeros((HEADS_PER_GROUP, t_len), F32)
    for t in range(t_len):
        lt = jnp.sum(jnp.where(cls == t, esum, 0.0), axis=1, keepdims=True) + e_new[:, t:t + 1]
        l_cols.append(lt)
        inv_lane = jnp.where(cls == t, 1.0 / lt, inv_lane)
        inv_new = jnp.where(new_idx == t, 1.0 / lt, inv_new)
    acc = None
    for j in range(n_tiles):
        term = cv_ref[:, j * LANES:(j + 1) * LANES] * _head_expand(s_scr[:, j * LANES:(j + 1) * LANES] * inv_lane)
        acc = term if acc is None else acc + term
    o_new = vn * _head_expand(e_new * inv_new)
    for t in range(t_len):
        o_ref[:, t:t + 1] = jnp.sum(jnp.where(cls == t, acc, 0.0), axis=1, keepdims=True) + o_new[:, t:t + 1]
        lse_ref[:, t:t + 1] = m_cols[t] + jnp.log(l_cols[t])


def _s_attn_kernel(q_ref, kn_ref, vn_ref, kt_ref, vt_ref, ck_ref, cv_ref, o_ref, lse_ref, ko_ref, vo_ref, s_scr,
                   *, win, dil, t_len):
    n_tiles = win // LANES
    lane = lax.broadcasted_iota(jnp.int32, (1, LANES), 1)
    new_idx = lax.broadcasted_iota(jnp.int32, (1, t_len), 1)
    kn = kn_ref[...]
    vn = vn_ref[...]
    if dil >= t_len:
        _s_attn_disjoint(q_ref, kn, vn, ck_ref, cv_ref, o_ref, lse_ref, s_scr, lane, new_idx,
                         n_tiles=n_tiles, dil=dil, t_len=t_len)
    for t in range(t_len if dil < t_len else 0):
        qb = jnp.broadcast_to(q_ref[:, t:t + 1], (GROUP_WIDTH, LANES))
        m = None
        for j in range(n_tiles):
            pos = lane + j * LANES
            s = _head_sum(ck_ref[:, j * LANES:(j + 1) * LANES] * qb)
            s = jnp.where(pos >= t, jnp.where(((pos - t) & (dil - 1)) == 0, s, NEG_BIG), NEG_BIG)
            s_scr[:, j * LANES:(j + 1) * LANES] = s
            mj = jnp.max(s, axis=1, keepdims=True)
            m = mj if m is None else jnp.maximum(m, mj)
        s_new = _head_sum(kn * qb[:, 0:t_len])
        s_new = jnp.where(new_idx <= t, jnp.where(((t - new_idx) & (dil - 1)) == 0, s_new, NEG_BIG), NEG_BIG)
        m = jnp.maximum(m, jnp.max(s_new, axis=1, keepdims=True))
        e_new = jnp.exp(s_new - m)
        l = jnp.sum(e_new, axis=1, keepdims=True)
        acc = None
        for j in range(n_tiles):
            e = jnp.exp(s_scr[:, j * LANES:(j + 1) * LANES] - m)
            l = l + jnp.sum(e, axis=1, keepdims=True)
            term = cv_ref[:, j * LANES:(j + 1) * LANES] * _head_expand(e)
            acc = term if acc is None else acc + term
        o = jnp.sum(acc, axis=1, keepdims=True) + jnp.sum(vn * _head_expand(e_new), axis=1, keepdims=True)
        o_ref[:, t:t + 1] = o * _head_expand(1.0 / l)
        lse_ref[:, t:t + 1] = m + jnp.log(l)

    for c_ref, tail_ref, out_ref in ((ck_ref, kt_ref, ko_ref), (cv_ref, vt_ref, vo_ref)):
        rolled = pltpu.roll(c_ref[...], win - t_len, 1)
        if win > LANES:
            out_ref[:, 0:win - LANES] = rolled[:, 0:win - LANES]
        out_ref[:, win - LANES:win] = jnp.where(lane >= LANES - t_len, tail_ref[...], rolled[:, win - LANES:win])


def _s_attn_group(q_t, kn_t, vn_t, k_tail, v_tail, cache_k, cache_v, g):
    bsz, _, t_len = q_t.shape
    win_full, dil = DIL_GROUPS[g]
    win = cache_k.shape[2]
    assert win == win_full and win == (KEYS_PER_QUERY - 1) * dil and win % LANES == 0
    assert dil & (dil - 1) == 0 and LANES % dil == 0
    grp = lambda b: (b, g, 0)
    per_b = lambda b: (b, 0, 0)
    return pl.pallas_call(
        functools.partial(_s_attn_kernel, win=win, dil=dil, t_len=t_len),
        grid=(bsz,),
        in_specs=[pl.BlockSpec((None, GROUP_WIDTH, t_len), grp)] * 3
        + [pl.BlockSpec((None, GROUP_WIDTH, LANES), grp)] * 2
        + [pl.BlockSpec((None, GROUP_WIDTH, win), per_b)] * 2,
        out_specs=[pl.BlockSpec((None, GROUP_WIDTH, t_len), per_b),
                   pl.BlockSpec((None, HEADS_PER_GROUP, t_len), per_b),
                   pl.BlockSpec((None, GROUP_WIDTH, win), per_b),
                   pl.BlockSpec((None, GROUP_WIDTH, win), per_b)],
        out_shape=[jax.ShapeDtypeStruct((bsz, GROUP_WIDTH, t_len), F32),
                   jax.ShapeDtypeStruct((bsz, HEADS_PER_GROUP, t_len), F32),
                   jax.ShapeDtypeStruct((bsz, GROUP_WIDTH, win), F32),
                   jax.ShapeDtypeStruct((bsz, GROUP_WIDTH, win), F32)],
        scratch_shapes=[pltpu.VMEM((HEADS_PER_GROUP, win), F32)],
        compiler_params=_cparams(("arbitrary",)),
        name=f"attn_sample_g{g}",
    )(q_t, kn_t, vn_t, k_tail, v_tail, cache_k, cache_v)


TM_INPROJ = 512
TM_POST = 512
TM_MOE = 1024
ATTN_MAX_CHUNK = 1024
S_COL_BLOCK = 512


def _to_state(a_t):
    b, _, length = a_t.shape
    return jnp.transpose(a_t.reshape(b, HEADS_PER_GROUP, HEAD_DIM, length), (0, 3, 1, 2))[None]


def _from_state(a):
    b, length = a.shape[0], a.shape[1]
    return jnp.transpose(a, (0, 2, 3, 1)).reshape(b, GROUP_WIDTH, length)


def kernel(x_prompt, x_sample, cache_k1, cache_v1, cache_k2, cache_v2, cache_k3, cache_v3, state_conv,
           c_prompt, c_sample, norm1_g, norm2_g, normf_g, w_ada, b_ada, w_in, conv_w, w_conv_out,
           w_attn_out, w_o, w_rg, b_rg, w_re, b_re, w1, w3, w2):
    depth = w_in.shape[0]
    assert depth == 1
    bsz, seq, _ = x_prompt.shape
    dbsz, t_len, _ = x_sample.shape
    n_s = dbsz * t_len
    l = 0

    w_in_bf = w_in[l].astype(BF16)
    wco_bf = w_conv_out[l].astype(BF16)
    wao_bf = w_attn_out[l].astype(BF16)
    wo_bf = w_o[l].astype(BF16)
    w1_bf = w1[l].reshape(N_EXPERTS, D_MODEL, D_EXPERT).astype(BF16)
    w3_bf = w3[l].reshape(N_EXPERTS, D_MODEL, D_EXPERT).astype(BF16)
    w2_bf = w2[l].reshape(N_EXPERTS, D_EXPERT, D_MODEL).astype(BF16)
    g1 = norm1_g[l].reshape(1, D_MODEL)
    g2 = norm2_g[l].reshape(1, D_MODEL)
    gf = normf_g.reshape(1, D_MODEL)
    n_route = N_EXPERT_GROUPS + N_EXPERTS
    wr = jnp.pad(jnp.concatenate([w_rg[l], w_re[l]], axis=1), ((0, 0), (0, LANES - n_route)))
    br = jnp.pad(jnp.concatenate([b_rg[l], b_re[l]]), (0, LANES - n_route)).reshape(1, LANES)
    head_of_lane = np.arange(GROUP_WIDTH) // HEAD_DIM
    expand_bf = jnp.asarray((np.arange(LANES)[:, None] == head_of_lane[None, :]).astype(np.float32), dtype=BF16)

    mod = _adaln(jnp.concatenate([c_prompt, c_sample], axis=0), w_ada[l], b_ada[l])
    mod_p4 = mod[:bsz].reshape(bsz, 6, 1, D_MODEL)
    mod_tok = jnp.repeat(mod[bsz:].reshape(dbsz, 1, 6, D_MODEL), t_len, axis=1)
    mod_tok = jnp.transpose(mod_tok.reshape(n_s, 6, D_MODEL), (1, 0, 2))
    mod_s4 = mod_tok[None]

    pos_p = jnp.arange(seq, dtype=jnp.int32)
    rope_q_p = _rope_tables(pos_p, HEAD_DIM ** -0.5)
    rope_k_p = _rope_tables(pos_p, 1.0)
    pc_p, sga_p, conv_p = _convproj_prompt(x_prompt, mod_p4, g1, w_in_bf[:, :OFF_Q], w_in_bf[:, OFF_GCONV:],
                                           conv_w[l], wco_bf, tm=TM_INPROJ)
    outs = _qkvproj_prompt(x_prompt, mod_p4, g1, w_in_bf[:, OFF_Q:OFF_GCONV], rope_q_p, rope_k_p, tm=TM_INPROJ)
    qkv_p = outs[0:3 * N_DIL]
    states_p = outs[3 * N_DIL:]
    o_p, lse_p = [], []
    for g in range(N_DIL):
        o_g, lse_g = _attn_prompt_group(qkv_p[3 * g], qkv_p[3 * g + 1], qkv_p[3 * g + 2], g, max_chunk=ATTN_MAX_CHUNK)
        o_p.append(o_g)
        lse_p.append(lse_g)
    x1_p, h2_p, comb_p = _post(o_p, lse_p, pc_p, sga_p, x_prompt, mod_p4, wao_bf, wo_bf, g2, wr, br, expand_bf,
                               tm=TM_POST, full_precision=False)
    y_p = _moe_dense(h2_p, comb_p, x1_p, mod_p4, gf, w1_bf, w3_bf, w2_bf, tm=TM_MOE)

    pos_s = PAST_LEN + jnp.arange(t_len, dtype=jnp.int32)
    rope_q_s = jnp.tile(_rope_tables(pos_s, HEAD_DIM ** -0.5), (1, dbsz, 1))
    rope_k_s = jnp.tile(_rope_tables(pos_s, 1.0), (1, dbsz, 1))
    xs = x_sample.reshape(n_s, D_MODEL)
    z_s = _s_inproj(xs, mod_tok, g1, w_in[l], col_block=S_COL_BLOCK)
    past = state_conv[l]
    p0e = jnp.repeat(past[:, 0], t_len, axis=0)
    p1e = jnp.repeat(past[:, 1], t_len, axis=0)
    cu_s, pc_s, sga_s, q_s, k_s, v_s = _s_mid(z_s, p0e, p1e, conv_w[l], w_conv_out[l], rope_q_s, rope_k_s,
                                              t_len=t_len)
    to_cols = lambda a: jnp.transpose(a.reshape(dbsz, t_len, ATTN_WIDTH), (0, 2, 1))
    q_t, kn_t, vn_t = to_cols(q_s), to_cols(k_s), to_cols(v_s)
    k_tail = jnp.pad(kn_t, ((0, 0), (0, 0), (LANES - t_len, 0)))
    v_tail = jnp.pad(vn_t, ((0, 0), (0, 0), (LANES - t_len, 0)))
    caches = ((cache_k1, cache_v1), (cache_k2, cache_v2), (cache_k3, cache_v3))
    o_s, lse_s, kv_s = [], [], []
    for g, (ck, cv) in enumerate(caches):
        o_g, lse_g, ko, vo = _s_attn_group(q_t, kn_t, vn_t, k_tail, v_tail, _from_state(ck[l]), _from_state(cv[l]), g)
        o_s.append(jnp.transpose(o_g, (0, 2, 1)).reshape(1, 1, n_s, GROUP_WIDTH))
        lse_rows = jnp.transpose(lse_g, (0, 2, 1)).reshape(n_s, HEADS_PER_GROUP)
        lse_s.append(jnp.pad(lse_rows, ((0, 0), (0, LANES - HEADS_PER_GROUP))).reshape(1, 1, n_s, LANES))
        kv_s += [_to_state(ko), _to_state(vo)]
    x1_s, h2_s, comb_s = _post(o_s, lse_s, pc_s[None], sga_s[None], xs[None], mod_s4, w_attn_out[l], w_o[l], g2, wr, br,
                               expand_bf, tm=n_s, full_precision=True)
    y_s = _moe_dense(h2_s, comb_s, x1_s, mod_s4, gf, w1_bf, w3_bf, w2_bf, tm=n_s)

    conv_s = cu_s.reshape(dbsz, t_len, D_MODEL)[:, t_len - (CONV_K - 1):]
    return (y_p, y_s.reshape(dbsz, t_len, D_MODEL),
            *[_to_state(a) for a in states_p], conv_p.reshape(1, bsz, CONV_K - 1, D_MODEL),
            *kv_s, conv_s.reshape(1, dbsz, CONV_K - 1, D_MODEL))
```

```python
import functools

import numpy as np
import jax
import jax.numpy as jnp
from jax import lax
from jax.experimental import pallas as pl
from jax.experimental.pallas import tpu as pltpu

F32 = jnp.float32
BF16 = jnp.bfloat16
HIGHEST = lax.Precision.HIGHEST

D_MODEL = 1024
HEAD_DIM = 64
HEADS_PER_GROUP = 8
GROUP_WIDTH = HEADS_PER_GROUP * HEAD_DIM
DIL_GROUPS = ((128, 1), (512, 4), (2048, 16))
N_DIL = len(DIL_GROUPS)
ATTN_WIDTH = N_DIL * GROUP_WIDTH
ROT_DIM = HEAD_DIM // 4
ROPE_THETA = 500000.0
PAST_LEN = 16384
CONV_K = 3
N_EXPERT_GROUPS = 4
EXPERTS_PER_GROUP = 4
N_EXPERTS = N_EXPERT_GROUPS * EXPERTS_PER_GROUP
D_EXPERT = 512
RMS_EPS = 1e-6
IN_COLS = 3 * D_MODEL + 3 * ATTN_WIDTH + 2 * D_MODEL
OFF_U, OFF_GC, OFF_GB = 0, D_MODEL, 2 * D_MODEL
OFF_Q = 3 * D_MODEL
OFF_K = OFF_Q + ATTN_WIDTH
OFF_V = OFF_K + ATTN_WIDTH
OFF_GCONV = OFF_V + ATTN_WIDTH
OFF_GATTN = OFF_GCONV + D_MODEL

LANES = 128
SUBLANES = 8
CHUNKS_PER_GROUP = GROUP_WIDTH // LANES
KEYS_PER_QUERY = 129
Q_BLOCK = 128
NEG_BIG = -1e30

VMEM_LIMIT = 56 * 1024 * 1024


def _sigmoid(x):
    return 1.0 / (1.0 + jnp.exp(-x))


def _cparams(sem):
    return pltpu.CompilerParams(dimension_semantics=sem, vmem_limit_bytes=VMEM_LIMIT)


def _adaln_kernel(c_ref, w_ref, b_ref, o_ref):
    c = c_ref[...]
    s = c * _sigmoid(c)
    o_ref[...] = jnp.dot(s, w_ref[...], precision=HIGHEST, preferred_element_type=F32) + b_ref[...]


def _adaln(c_all, w_ada, b_ada):
    rows = c_all.shape[0]
    n_col = w_ada.shape[1] // D_MODEL
    return pl.pallas_call(
        _adaln_kernel,
        grid=(n_col,),
        in_specs=[
            pl.BlockSpec((rows, D_MODEL), lambda j: (0, 0)),
            pl.BlockSpec((D_MODEL, D_MODEL), lambda j: (0, j)),
            pl.BlockSpec((1, D_MODEL), lambda j: (0, j)),
        ],
        out_specs=pl.BlockSpec((rows, D_MODEL), lambda j: (0, j)),
        out_shape=jax.ShapeDtypeStruct((rows, w_ada.shape[1]), F32),
        compiler_params=_cparams(("arbitrary",)),
        name="adaln",
    )(c_all, w_ada, b_ada.reshape(1, -1))


def _rope_tables(pos, scale):
    half = ROT_DIM // 2
    inv_freq = jnp.power(jnp.float32(ROPE_THETA), -jnp.arange(half, dtype=F32) / half)
    ang = pos.astype(F32)[:, None] * inv_freq[None, :]
    cos, sin = jnp.cos(ang), jnp.sin(ang)
    lane_in_head = np.arange(LANES) % HEAD_DIM
    freq = lane_in_head % half
    first = lane_in_head < half
    second = (lane_in_head >= half) & (lane_in_head < ROT_DIM)
    a = jnp.where(first | second, cos[:, freq], 1.0)
    bm = jnp.where(first, -sin[:, freq], 0.0)
    bp = jnp.where(second, sin[:, freq], 0.0)
    return jnp.stack([a, bm, bp]) * scale


def _rope_chunk(zc, a, bm, bp):
    return zc * a + pltpu.roll(zc, LANES - ROT_DIM // 2, 1) * bm + pltpu.roll(zc, ROT_DIM // 2, 1) * bp


def _modulated_norm_bf16(x_ref, mod_ref, g1_ref):
    x = x_ref[...]
    var = jnp.mean(x * x, axis=-1, keepdims=True)
    h = (x * lax.rsqrt(var + RMS_EPS)) * (g1_ref[...] * (1.0 + mod_ref[1])) + mod_ref[0]
    return h.astype(BF16)


def _convproj_kernel(x_ref, mod_ref, g1_ref, wa_ref, wg_ref, cw_ref, wco_ref, pc_ref, sga_ref, cst_ref, s_ref,
                     *, tm, n_tiles):
    i = pl.program_id(1)
    hb = _modulated_norm_bf16(x_ref, mod_ref, g1_ref)

    def proj(w_ref, lo):
        return jnp.dot(hb, w_ref[:, lo:lo + D_MODEL], preferred_element_type=F32)

    cu = proj(wa_ref, OFF_GC) * proj(wa_ref, OFF_U)

    @pl.when(i == 0)
    def _():
        s_ref[0:SUBLANES, :] = jnp.zeros((SUBLANES, D_MODEL), F32)

    s_ref[SUBLANES:SUBLANES + tm, :] = cu
    cw = cw_ref[...]
    conv = (cw[0:1] * s_ref[SUBLANES - 2:SUBLANES - 2 + tm, :]
            + cw[1:2] * s_ref[SUBLANES - 1:SUBLANES - 1 + tm, :]
            + cw[2:3] * cu)
    yc = jnp.dot((proj(wa_ref, OFF_GB) * conv).astype(BF16), wco_ref[...], preferred_element_type=F32)
    pc_ref[...] = (_sigmoid(proj(wg_ref, 0)) * yc).astype(pc_ref.dtype)
    sga_ref[...] = _sigmoid(proj(wg_ref, D_MODEL)).astype(sga_ref.dtype)

    @pl.when(i == n_tiles - 1)
    def _():
        cst_ref[...] = s_ref[tm + SUBLANES - 2:tm + SUBLANES, :]

    s_ref[0:SUBLANES, :] = s_ref[tm:tm + SUBLANES, :]


def _convproj_prompt(x, mod4, g1, wa_bf, wg_bf, conv_w, wco_bf, *, tm):
    bsz, seq, _ = x.shape
    n_tiles = seq // tm
    const2 = lambda b, i: (0, 0)
    tile3 = lambda b, i: (b, i, 0)
    return pl.pallas_call(
        functools.partial(_convproj_kernel, tm=tm, n_tiles=n_tiles),
        grid=(bsz, n_tiles),
        in_specs=[
            pl.BlockSpec((None, tm, D_MODEL), tile3),
            pl.BlockSpec((None, 6, 1, D_MODEL), lambda b, i: (b, 0, 0, 0)),
            pl.BlockSpec((1, D_MODEL), const2),
            pl.BlockSpec(wa_bf.shape, const2, pipeline_mode=pl.Buffered(1)),
            pl.BlockSpec(wg_bf.shape, const2, pipeline_mode=pl.Buffered(1)),
            pl.BlockSpec((CONV_K, D_MODEL), const2),
            pl.BlockSpec((D_MODEL, D_MODEL), const2, pipeline_mode=pl.Buffered(1)),
        ],
        out_specs=[pl.BlockSpec((None, tm, D_MODEL), tile3), pl.BlockSpec((None, tm, D_MODEL), tile3),
                   pl.BlockSpec((None, CONV_K - 1, D_MODEL), lambda b, i: (b, 0, 0))],
        out_shape=[jax.ShapeDtypeStruct((bsz, seq, D_MODEL), BF16), jax.ShapeDtypeStruct((bsz, seq, D_MODEL), BF16),
                   jax.ShapeDtypeStruct((bsz, CONV_K - 1, D_MODEL), F32)],
        scratch_shapes=[pltpu.VMEM((tm + SUBLANES, D_MODEL), F32)],
        compiler_params=_cparams(("arbitrary", "arbitrary")),
        name="convproj_prompt",
    )(x, mod4, g1, wa_bf, wg_bf, conv_w, wco_bf)


def _qkvproj_kernel(x_ref, mod_ref, g1_ref, w_ref, rq_ref, rk_ref, *rest, tm, n_tiles, seq):
    qkv_refs = rest[0:3 * N_DIL]
    st_refs = rest[3 * N_DIL:3 * N_DIL + 2 * N_DIL]
    d_ref = rest[-1]
    i = pl.program_id(1)
    hb = _modulated_norm_bf16(x_ref, mod_ref, g1_ref)

    def proj(lo, width):
        return jnp.dot(hb, w_ref[:, lo:lo + width], preferred_element_type=F32)

    zq = proj(0, ATTN_WIDTH)
    zk = proj(ATTN_WIDTH, ATTN_WIDTH)
    zv = proj(2 * ATTN_WIDTH, ATTN_WIDTH)
    aq, bmq, bpq = rq_ref[0], rq_ref[1], rq_ref[2]
    ak, bmk, bpk = rk_ref[0], rk_ref[1], rk_ref[2]
    n_chunks = ATTN_WIDTH // LANES
    q_chunks, k_chunks, v_chunks = [], [], []
    for c in range(n_chunks):
        sl = slice(c * LANES, (c + 1) * LANES)
        q_chunks.append(_rope_chunk(zq[:, sl], aq, bmq, bpq))
        k_chunks.append(_rope_chunk(zk[:, sl], ak, bmk, bpk))
        v_chunks.append(zv[:, sl])

    for which, chunks in enumerate((q_chunks, k_chunks, v_chunks)):
        for g in range(N_DIL):
            out_ref = qkv_refs[3 * g + which]
            dil = DIL_GROUPS[g][1]
            for cc in range(CHUNKS_PER_GROUP):
                c = g * CHUNKS_PER_GROUP + cc
                sl = slice(cc * LANES, (cc + 1) * LANES)
                if dil == 1:
                    out_ref[0, :, sl] = chunks[c].astype(BF16)
                else:
                    d_ref[c] = chunks[c]
                    for r in range(dil):
                        out_ref[r, :, sl] = d_ref[c, pl.ds(r, tm // dil, stride=dil), :].astype(BF16)

    for g in range(N_DIL):
        kst, vst = st_refs[2 * g], st_refs[2 * g + 1]
        win = min(DIL_GROUPS[g][0], seq)
        if win >= tm:
            cond, r0 = i >= (seq - win) // tm, 0
        else:
            cond, r0 = i == n_tiles - 1, tm - win

        @pl.when(cond)
        def _(g=g, kst=kst, vst=vst, r0=r0):
            for cc in range(CHUNKS_PER_GROUP):
                c = g * CHUNKS_PER_GROUP + cc
                kst[cc * LANES:(cc + 1) * LANES, :] = k_chunks[c][r0:, :].T
                vst[cc * LANES:(cc + 1) * LANES, :] = v_chunks[c][r0:, :].T


def _qkvproj_prompt(x, mod4, g1, wqkv_bf, rope_q, rope_k, *, tm):
    bsz, seq, _ = x.shape
    n_tiles = seq // tm
    const2 = lambda b, i: (0, 0)
    tile3 = lambda b, i: (b, i, 0)
    in_specs = [
        pl.BlockSpec((None, tm, D_MODEL), tile3),
        pl.BlockSpec((None, 6, 1, D_MODEL), lambda b, i: (b, 0, 0, 0)),
        pl.BlockSpec((1, D_MODEL), const2),
        pl.BlockSpec(wqkv_bf.shape, const2, pipeline_mode=pl.Buffered(1)),
        pl.BlockSpec((3, tm, LANES), lambda b, i: (0, i, 0)),
        pl.BlockSpec((3, tm, LANES), lambda b, i: (0, i, 0)),
    ]
    out_shape, out_specs = [], []
    for _, dil in DIL_GROUPS:
        assert tm % (dil * 16) == 0
        for _ in range(3):
            out_shape.append(jax.ShapeDtypeStruct((bsz, dil, seq // dil, GROUP_WIDTH), BF16))
            out_specs.append(pl.BlockSpec((None, dil, tm // dil, GROUP_WIDTH), lambda b, i: (b, 0, i, 0)))
    for win, _ in DIL_GROUPS:
        win = min(win, seq)
        cols = min(win, tm)
        if win >= tm:
            imap = lambda b, i, ft=(seq - win) // tm: (b, 0, jnp.maximum(i - ft, 0))
        else:
            imap = lambda b, i: (b, 0, 0)
        for _ in range(2):
            out_shape.append(jax.ShapeDtypeStruct((bsz, GROUP_WIDTH, win), F32))
            out_specs.append(pl.BlockSpec((None, GROUP_WIDTH, cols), imap))
    return pl.pallas_call(
        functools.partial(_qkvproj_kernel, tm=tm, n_tiles=n_tiles, seq=seq),
        grid=(bsz, n_tiles),
        in_specs=in_specs,
        out_specs=out_specs,
        out_shape=out_shape,
        scratch_shapes=[pltpu.VMEM((ATTN_WIDTH // LANES, tm, LANES), F32)],
        compiler_params=_cparams(("arbitrary", "arbitrary")),
        name="qkvproj_prompt",
    )(x, mod4, g1, wqkv_bf, rope_q, rope_k)


def _attn_kernel(*refs, chunk, has_halo):
    if has_halo:
        q_ref, k_ref, v_ref, kh_ref, vh_ref, o_ref, lse_ref, kbuf, vbuf = refs
    else:
        q_ref, k_ref, v_ref, o_ref, lse_ref, kbuf, vbuf = refs
    c = pl.program_id(2)
    if has_halo:
        kbuf[0:Q_BLOCK, :] = kh_ref[...]
        vbuf[0:Q_BLOCK, :] = vh_ref[...]
    else:
        kbuf[0:Q_BLOCK, :] = jnp.zeros((Q_BLOCK, GROUP_WIDTH), BF16)
        vbuf[0:Q_BLOCK, :] = jnp.zeros((Q_BLOCK, GROUP_WIDTH), BF16)
    kbuf[Q_BLOCK:Q_BLOCK + chunk, :] = k_ref[...]
    vbuf[Q_BLOCK:Q_BLOCK + chunk, :] = v_ref[...]

    row = lax.broadcasted_iota(jnp.int32, (Q_BLOCK, 2 * Q_BLOCK), 0)
    col = lax.broadcasted_iota(jnp.int32, (Q_BLOCK, 2 * Q_BLOCK), 1)
    bias_main = jnp.where(col >= row, jnp.where(col <= row + Q_BLOCK, 0.0, NEG_BIG), NEG_BIG)
    bias_first = jnp.where(col >= Q_BLOCK, bias_main, NEG_BIG)
    lane = lax.broadcasted_iota(jnp.int32, (1, LANES), 1)
    lo_half = lane < HEAD_DIM

    def body(qb, carry):
        r0 = pl.multiple_of(qb * Q_BLOCK, Q_BLOCK)
        kt = kbuf[pl.ds(r0, 2 * Q_BLOCK), :]
        vt = vbuf[pl.ds(r0, 2 * Q_BLOCK), :]
        qt = q_ref[pl.ds(r0, Q_BLOCK), :]
        is_first = jnp.logical_and(qb == 0, c == 0)
        bias = jnp.where(is_first, bias_first, bias_main)
        lse_tile = jnp.zeros((Q_BLOCK, LANES), F32)
        for p in range(CHUNKS_PER_GROUP):
            sl = slice(p * LANES, (p + 1) * LANES)
            qp, kp, vp = qt[:, sl], kt[:, sl], vt[:, sl]
            outs = []
            for hh in range(2):
                head_lanes = lo_half if hh == 0 else jnp.logical_not(lo_half)
                qm = jnp.where(head_lanes, qp, jnp.zeros_like(qp))
                s = lax.dot_general(qm, kp, (((1,), (1,)), ((), ())), preferred_element_type=F32) + bias
                m = jnp.max(s, axis=1, keepdims=True)
                e = jnp.exp(s - m)
                l = jnp.sum(e, axis=1, keepdims=True)
                o = jnp.dot(e.astype(BF16), vp, preferred_element_type=F32)
                outs.append(o * (1.0 / l))
                lse_tile = jnp.where(lane == 2 * p + hh, m + jnp.log(l), lse_tile)
            o_ref[pl.ds(r0, Q_BLOCK), sl] = jnp.where(lo_half, outs[0], outs[1])
        lse_ref[pl.ds(r0, Q_BLOCK), :] = lse_tile
        return carry

    lax.fori_loop(0, chunk // Q_BLOCK, body, 0)


def _attn_prompt_group(q, k, v, g, *, max_chunk):
    bsz, dil, cls_len, _ = q.shape
    chunk = min(max_chunk, cls_len)
    n_chunks = cls_len // chunk
    has_halo = n_chunks > 1
    main = pl.BlockSpec((None, None, chunk, GROUP_WIDTH), lambda b, r, c: (b, r, c, 0))
    in_specs = [main, main, main]
    args = [q, k, v]
    if has_halo:
        per = chunk // Q_BLOCK
        halo = pl.BlockSpec((None, None, Q_BLOCK, GROUP_WIDTH), lambda b, r, c: (b, r, jnp.maximum(c * per - 1, 0), 0))
        in_specs += [halo, halo]
        args += [k, v]
    return pl.pallas_call(
        functools.partial(_attn_kernel, chunk=chunk, has_halo=has_halo),
        grid=(bsz, dil, n_chunks),
        in_specs=in_specs,
        out_specs=[pl.BlockSpec((None, None, chunk, GROUP_WIDTH), lambda b, r, c: (b, r, c, 0)),
                   pl.BlockSpec((None, None, chunk, LANES), lambda b, r, c: (b, r, c, 0))],
        out_shape=[jax.ShapeDtypeStruct((bsz, dil, cls_len, GROUP_WIDTH), F32),
                   jax.ShapeDtypeStruct((bsz, dil, cls_len, LANES), F32)],
        scratch_shapes=[pltpu.VMEM((chunk + Q_BLOCK, GROUP_WIDTH), BF16),
                        pltpu.VMEM((chunk + Q_BLOCK, GROUP_WIDTH), BF16)],
        compiler_params=_cparams(("arbitrary", "arbitrary", "arbitrary")),
        name=f"attn_prompt_g{g}",
    )(*args)


def _split_bf16(x, n):
    parts = []
    r = x
    for _ in range(n):
        p = r.astype(BF16)
        parts.append(p)
        r = r - p.astype(F32)
    return parts


def _pack_bf16_pair(a, b):
    ua = pltpu.bitcast(a.astype(BF16).astype(F32), jnp.uint32)
    ub = pltpu.bitcast(b.astype(BF16).astype(F32), jnp.uint32)
    return ua | (ub >> 16)


def _unpack_bf16_pair(w):
    a = pltpu.bitcast(w & jnp.uint32(0xFFFF0000), F32)
    b = pltpu.bitcast(w << 16, F32)
    return a, b


_PAIRS = ((0, 1), (0, 2), (0, 3), (1, 2), (1, 3), (2, 3))
N_CLASSES = N_EXPERT_GROUPS * len(_PAIRS)
META_CLASS, META_RANK, META_WA, META_WB = 0, 1, 2, 3


def _post_kernel(*refs, tm, dils, full_precision, routed):
    (o0_ref, o1_ref, o2_ref, l0_ref, l1_ref, l2_ref, pc_ref, sga_ref, x_ref, mod_ref,
     wao_ref, wo_ref, g2_ref, wr_ref, wrh_ref, wrl_ref, br_ref, exp_ref) = refs[:18]
    if routed:
        tri_ref, x1_ref, h2_ref, meta_ref, cnt_ref, o_scr, l_scr, run_scr = refs[18:]
    else:
        x1_ref, h2_ref, comb_ref, o_scr, l_scr = refs[18:]

    def mm(a, w_ref):
        if full_precision:
            return jnp.dot(a, w_ref[...], precision=HIGHEST, preferred_element_type=F32)
        return jnp.dot(a.astype(BF16), w_ref[...], preferred_element_type=F32)

    def natural_order(ref, scr, dil, n_chunks):
        if dil == 1:
            return [ref[0, :, c * LANES:(c + 1) * LANES] for c in range(n_chunks)]
        out = []
        for c in range(n_chunks):
            for r in range(dil):
                scr[c, pl.ds(r, tm // dil, stride=dil), :] = ref[r, :, c * LANES:(c + 1) * LANES]
            out.append(scr[c])
        return out

    lses = [natural_order(ref, l_scr.at[g], dils[g], 1)[0] for g, ref in enumerate((l0_ref, l1_ref, l2_ref))]
    mx = jnp.maximum(lses[0], jnp.maximum(lses[1], lses[2]))
    es = [jnp.exp(v - mx) for v in lses]
    inv = 1.0 / (es[0] + es[1] + es[2])
    expand = exp_ref[...]
    attn_o = None
    for g, o_ref in enumerate((o0_ref, o1_ref, o2_ref)):
        w = es[g] * inv
        we = None
        for part in _split_bf16(w, 3 if full_precision else 2):
            t = jnp.dot(part, expand, preferred_element_type=F32)
            we = t if we is None else we + t
        o_nat = jnp.concatenate(natural_order(o_ref, o_scr.at[g], dils[g], CHUNKS_PER_GROUP), axis=1)
        term = we * o_nat
        attn_o = term if attn_o is None else attn_o + term

    y_attn = mm(attn_o, wao_ref)
    mixed = mm(pc_ref[...] + sga_ref[...] * y_attn, wo_ref)
    x1 = x_ref[...] + mod_ref[2] * mixed
    x1_ref[...] = x1
    var = jnp.mean(x1 * x1, axis=-1, keepdims=True)
    h2 = (x1 * lax.rsqrt(var + RMS_EPS)) * (g2_ref[...] * (1.0 + mod_ref[4])) + mod_ref[3]
    if routed:
        h2_ref[...] = _pack_bf16_pair(h2[:, :D_MODEL // 2], h2[:, D_MODEL // 2:])
    else:
        h2_ref[...] = h2.astype(h2_ref.dtype)

    if full_precision:
        lg = jnp.dot(h2, wr_ref[...], precision=HIGHEST, preferred_element_type=F32) + br_ref[...]
    else:
        h_hi, h_lo = _split_bf16(h2, 2)
        lg = (jnp.dot(h_hi, wrh_ref[...], preferred_element_type=F32)
              + jnp.dot(h_lo, wrh_ref[...], preferred_element_type=F32)
              + jnp.dot(h_hi, wrl_ref[...], preferred_element_type=F32)) + br_ref[...]
    lane_i = lax.broadcasted_iota(jnp.int32, (1, LANES), 1)
    lane = lane_i.astype(F32)
    lane_group = ((lane_i - N_EXPERT_GROUPS) >> 2).astype(F32)
    big = jnp.float32(1e9)
    gl = jnp.where(lane_i < N_EXPERT_GROUPS, lg, NEG_BIG)
    gmax = jnp.max(gl, axis=1, keepdims=True)
    gidx = jnp.min(jnp.where(gl == gmax, lane, big), axis=1, keepdims=True)
    g_w = 1.0 / jnp.sum(jnp.exp(gl - gmax), axis=1, keepdims=True)
    el = jnp.where(lane_group == gidx, lg, NEG_BIG)
    v1 = jnp.max(el, axis=1, keepdims=True)
    i1 = jnp.min(jnp.where(el == v1, lane, big), axis=1, keepdims=True)
    el2 = jnp.where(lane == i1, NEG_BIG, el)
    v2 = jnp.max(el2, axis=1, keepdims=True)
    i2 = jnp.min(jnp.where(el2 == v2, lane, big), axis=1, keepdims=True)
    t = jnp.exp(v2 - v1)
    den = 1.0 / (1.0 + t)
    w_top1, w_top2 = g_w * den, g_w * (t * den)
    if not routed:
        comb_ref[...] = jnp.where(lane == i1, w_top1, jnp.where(lane == i2, w_top2, 0.0))
        return

    base = jnp.float32(N_EXPERT_GROUPS) + jnp.float32(EXPERTS_PER_GROUP) * gidx
    e1, e2 = i1 - base, i2 - base
    first_is_lower = e1 < e2
    ea = jnp.where(first_is_lower, e1, e2)
    eb = jnp.where(first_is_lower, e2, e1)
    pair = ea * (7.0 - ea) * 0.5 + (eb - ea - 1.0)
    cls = gidx * jnp.float32(len(_PAIRS)) + pair
    onehot = lane == cls
    earlier = jnp.dot(tri_ref[...], jnp.where(onehot, 1.0, 0.0).astype(BF16), preferred_element_type=F32)

    @pl.when(jnp.logical_and(pl.program_id(0) == 0, pl.program_id(1) == 0))
    def _():
        run_scr[...] = jnp.zeros_like(run_scr)

    running = run_scr[...]
    rank = jnp.sum(jnp.where(onehot, earlier + running, 0.0), axis=1, keepdims=True)
    running = running + jnp.sum(jnp.where(onehot, 1.0, 0.0), axis=0, keepdims=True)
    run_scr[...] = running
    cnt_ref[...] = running
    w_a = jnp.where(first_is_lower, w_top1, w_top2)
    w_b = jnp.where(first_is_lower, w_top2, w_top1)
    meta_ref[...] = jnp.where(lane_i == META_CLASS, cls,
                              jnp.where(lane_i == META_RANK, rank,
                                        jnp.where(lane_i == META_WA, w_a, jnp.where(lane_i == META_WB, w_b, 0.0))))


def _post(o_list, lse_list, pc, sga, x, mod4, wao, wo, g2, wr, br, expand, *, tm, full_precision, routed):
    nb, rows, _ = x.shape
    tiles = rows // tm
    dils = tuple(o.shape[1] for o in o_list)
    tile3 = lambda b, i: (b, i, 0)
    const2 = lambda b, i: (0, 0)
    mod_rows = mod4.shape[2]
    if mod_rows == 1:
        mod_spec = pl.BlockSpec((None, 6, 1, D_MODEL), lambda b, i: (b, 0, 0, 0))
    else:
        mod_spec = pl.BlockSpec((None, 6, tm, D_MODEL), lambda b, i: (b, 0, i, 0))
    cls4 = lambda b, i: (b, 0, i, 0)
    in_specs = (
        [pl.BlockSpec((None, d, tm // d, GROUP_WIDTH), cls4) for d in dils]
        + [pl.BlockSpec((None, d, tm // d, LANES), cls4) for d in dils]
        + [pl.BlockSpec((None, tm, D_MODEL), tile3)] * 3 + [mod_spec]
        + [pl.BlockSpec(wao.shape, const2), pl.BlockSpec(wo.shape, const2), pl.BlockSpec((1, D_MODEL), const2)]
        + [pl.BlockSpec((D_MODEL, LANES), const2)] * 3
        + [pl.BlockSpec((1, LANES), const2), pl.BlockSpec((LANES, GROUP_WIDTH), const2)]
    )
    wr_hi = wr.astype(BF16)
    wr_lo = (wr - wr_hi.astype(F32)).astype(BF16)
    args = [*o_list, *lse_list, pc, sga, x, mod4, wao, wo, g2, wr, wr_hi, wr_lo, br, expand]
    scratch = [pltpu.VMEM((N_DIL, CHUNKS_PER_GROUP, tm, LANES), F32), pltpu.VMEM((N_DIL, 1, tm, LANES), F32)]
    if routed:
        tri = jnp.asarray(np.tril(np.ones((tm, tm), np.float32), -1), dtype=BF16)
        args.append(tri)
        in_specs = in_specs + [pl.BlockSpec((tm, tm), const2)]
        out_specs = [pl.BlockSpec((None, tm, D_MODEL), tile3), pl.BlockSpec((None, tm, D_MODEL // 2), tile3),
                     pl.BlockSpec((None, tm, LANES), tile3), pl.BlockSpec((1, LANES), const2)]
        out_shape = [jax.ShapeDtypeStruct((nb, rows, D_MODEL), F32),
                     jax.ShapeDtypeStruct((nb, rows, D_MODEL // 2), jnp.uint32),
                     jax.ShapeDtypeStruct((nb, rows, LANES), F32), jax.ShapeDtypeStruct((1, LANES), F32)]
        scratch.append(pltpu.VMEM((1, LANES), F32))
    else:
        out_specs = [pl.BlockSpec((None, tm, D_MODEL), tile3), pl.BlockSpec((None, tm, D_MODEL), tile3),
                     pl.BlockSpec((None, tm, LANES), tile3)]
        out_shape = [jax.ShapeDtypeStruct((nb, rows, D_MODEL), F32), jax.ShapeDtypeStruct((nb, rows, D_MODEL), F32),
                     jax.ShapeDtypeStruct((nb, rows, LANES), F32)]
    return pl.pallas_call(
        functools.partial(_post_kernel, tm=tm, dils=dils, full_precision=full_precision, routed=routed),
        grid=(nb, tiles),
        in_specs=in_specs,
        out_specs=out_specs,
        out_shape=out_shape,
        scratch_shapes=scratch,
        compiler_params=_cparams(("arbitrary", "arbitrary")),
        name="post_sample" if full_precision else "post_prompt",
    )(*args)


DMA_LOOP_UNROLL = 8


def _row_copy_kernel(dest_ref, src_hbm, dst_hbm, sems, *, rows_per_step, n_steps):
    i = pl.program_id(0)
    slot = i % 2

    def wait_slot(s):
        def body(r, carry):
            pltpu.make_async_copy(src_hbm.at[pl.ds(0, 1)], dst_hbm.at[pl.ds(0, 1)],
                                  sems.at[s * rows_per_step + r]).wait()
            return carry
        lax.fori_loop(0, rows_per_step, body, 0, unroll=DMA_LOOP_UNROLL)

    @pl.when(i >= 1)
    def _():
        wait_slot(1 - slot)

    def issue(r, carry):
        pltpu.make_async_copy(src_hbm.at[pl.ds(i * rows_per_step + r, 1)], dst_hbm.at[pl.ds(dest_ref[0, r], 1)],
                              sems.at[slot * rows_per_step + r]).start()
        return carry

    lax.fori_loop(0, rows_per_step, issue, 0, unroll=DMA_LOOP_UNROLL)

    @pl.when(i == n_steps - 1)
    def _():
        wait_slot(slot)


def _dispatch_rows(src, dest, *, rows_per_step):
    n, width = src.shape
    n_steps = n // rows_per_step
    return pl.pallas_call(
        functools.partial(_row_copy_kernel, rows_per_step=rows_per_step, n_steps=n_steps),
        grid=(n_steps,),
        in_specs=[pl.BlockSpec((None, 1, rows_per_step), lambda i: (i, 0, 0), memory_space=pltpu.SMEM),
                  pl.BlockSpec(memory_space=pl.ANY)],
        out_specs=pl.BlockSpec(memory_space=pl.ANY),
        out_shape=jax.ShapeDtypeStruct((n, width), src.dtype),
        scratch_shapes=[pltpu.SemaphoreType.DMA((2 * rows_per_step,))],
        compiler_params=_cparams(("arbitrary",)),
        name="moe_dispatch",
    )(dest.reshape(n_steps, 1, rows_per_step), src)


def _experts_kernel(tile_ref, ea_ref, eb_ref, lo_ref, hi_ref, first_ref, nv_ref,
                    x_ref, w1a_ref, w1b_ref, w3a_ref, w3b_ref, w2a_ref, w2b_ref, y_ref, *, tm):
    i = pl.program_id(0)

    @pl.when(i < nv_ref[0])
    def _():
        xa, xb = _unpack_bf16_pair(x_ref[...])
        x = jnp.concatenate([xa, xb], axis=1).astype(BF16)

        def expert(w1_ref, w3_ref, w2_ref):
            a = jnp.dot(x, w1_ref[...], preferred_element_type=F32)
            b = jnp.dot(x, w3_ref[...], preferred_element_type=F32)
            hid = (a * _sigmoid(a)) * b
            return jnp.dot(hid.astype(BF16), w2_ref[...], preferred_element_type=F32)

        word = _pack_bf16_pair(expert(w1a_ref, w3a_ref, w2a_ref), expert(w1b_ref, w3b_ref, w2b_ref))

        @pl.when(first_ref[i] == 1)
        def _():
            y_ref[...] = word

        @pl.when(first_ref[i] == 0)
        def _():
            row = lax.broadcasted_iota(jnp.int32, (tm, 1), 0)
            old = y_ref[...]
            y_ref[...] = jnp.where(row >= lo_ref[i], jnp.where(row < hi_ref[i], word, old), old)


def _experts_routed(hs, tables, w1_bf, w3_bf, w2_bf, *, tm):
    n = hs.shape[0]
    n_items = tables[0].shape[0]
    x_map = lambda i, tile, ea, eb, lo, hi, first, nv: (tile[i], 0)
    wa_map = lambda i, tile, ea, eb, lo, hi, first, nv: (ea[i], 0, 0)
    wb_map = lambda i, tile, ea, eb, lo, hi, first, nv: (eb[i], 0, 0)
    w13 = lambda m: pl.BlockSpec((None, D_MODEL, D_EXPERT), m)
    w2s = lambda m: pl.BlockSpec((None, D_EXPERT, D_MODEL), m)
    grid_spec = pltpu.PrefetchScalarGridSpec(
        num_scalar_prefetch=7,
        grid=(n_items,),
        in_specs=[pl.BlockSpec((tm, D_MODEL // 2), x_map),
                  w13(wa_map), w13(wb_map), w13(wa_map), w13(wb_map), w2s(wa_map), w2s(wb_map)],
        out_specs=pl.BlockSpec((tm, D_MODEL), x_map),
    )
    return pl.pallas_call(
        functools.partial(_experts_kernel, tm=tm),
        grid_spec=grid_spec,
        out_shape=jax.ShapeDtypeStruct((n, D_MODEL), jnp.uint32),
        compiler_params=_cparams(("arbitrary",)),
        name="moe_experts",
    )(*tables, hs, w1_bf, w1_bf, w3_bf, w3_bf, w2_bf, w2_bf)


def _combine_kernel(dest_ref, dest_next_ref, yp_hbm, x1_ref, meta_ref, mod_ref, gf_ref, y_ref, ybuf, sems,
                    *, rows_per_step, n_steps):
    i = pl.program_id(0)
    slot = i % 2

    def issue(d_ref, s):
        def body(r, carry):
            pltpu.make_async_copy(yp_hbm.at[pl.ds(d_ref[0, r], 1)], ybuf.at[s, pl.ds(r, 1)],
                                  sems.at[s * rows_per_step + r]).start()
            return carry
        lax.fori_loop(0, rows_per_step, body, 0, unroll=DMA_LOOP_UNROLL)

    @pl.when(i == 0)
    def _():
        issue(dest_ref, 0)

    @pl.when(i + 1 < n_steps)
    def _():
        issue(dest_next_ref, 1 - slot)

    def wait_body(r, carry):
        pltpu.make_async_copy(yp_hbm.at[pl.ds(0, 1)], ybuf.at[slot, pl.ds(r, 1)],
                              sems.at[slot * rows_per_step + r]).wait()
        return carry

    lax.fori_loop(0, rows_per_step, wait_body, 0, unroll=DMA_LOOP_UNROLL)
    ya, yb = _unpack_bf16_pair(ybuf[slot])
    meta = meta_ref[...]
    lane = lax.broadcasted_iota(jnp.int32, (1, LANES), 1)
    w_a = jnp.sum(jnp.where(lane == META_WA, meta, 0.0), axis=1, keepdims=True)
    w_b = jnp.sum(jnp.where(lane == META_WB, meta, 0.0), axis=1, keepdims=True)
    x2 = x1_ref[...] + mod_ref[5] * (w_a * ya + w_b * yb)
    var = jnp.mean(x2 * x2, axis=-1, keepdims=True)
    y_ref[...] = (x2 * lax.rsqrt(var + RMS_EPS)) * gf_ref[...]


def _combine_final(yp, dest, x1, meta, mod4, gf, *, rows_per_step, rows_per_mod):
    n = x1.shape[0]
    n_steps = n // rows_per_step
    steps_per_mod = rows_per_mod // rows_per_step
    row = lambda i: (i, 0)
    dest3 = dest.reshape(n_steps, 1, rows_per_step)
    return pl.pallas_call(
        functools.partial(_combine_kernel, rows_per_step=rows_per_step, n_steps=n_steps),
        grid=(n_steps,),
        in_specs=[pl.BlockSpec((None, 1, rows_per_step), lambda i: (i, 0, 0), memory_space=pltpu.SMEM),
                  pl.BlockSpec((None, 1, rows_per_step), lambda i: (jnp.minimum(i + 1, n_steps - 1), 0, 0),
                               memory_space=pltpu.SMEM),
                  pl.BlockSpec(memory_space=pl.ANY),
                  pl.BlockSpec((rows_per_step, D_MODEL), row),
                  pl.BlockSpec((rows_per_step, LANES), row),
                  pl.BlockSpec((None, 6, 1, D_MODEL), lambda i: (i // steps_per_mod, 0, 0, 0)),
                  pl.BlockSpec((1, D_MODEL), lambda i: (0, 0))],
        out_specs=pl.BlockSpec((rows_per_step, D_MODEL), row),
        out_shape=jax.ShapeDtypeStruct((n, D_MODEL), F32),
        scratch_shapes=[pltpu.VMEM((2, rows_per_step, D_MODEL), jnp.uint32),
                        pltpu.SemaphoreType.DMA((2 * rows_per_step,))],
        compiler_params=_cparams(("arbitrary",)),
        name="moe_combine",
    )(dest3, dest3, yp, x1, meta, mod4, gf)


def _routing_tables(cls, rank, counts, *, n, tm):
    counts = counts.astype(jnp.int32)
    ends = jnp.cumsum(counts)
    starts = ends - counts
    dest = starts[cls] + rank
    n_tiles = n // tm
    n_items = n_tiles + N_CLASSES
    first_tile = starts // tm
    last_tile = jnp.maximum(ends - 1, starts) // tm
    visits = jnp.where(counts > 0, last_tile - first_tile + 1, 0)
    item_end = jnp.cumsum(visits)
    item_start = item_end - visits
    n_valid = item_end[-1]
    idx = jnp.minimum(jnp.arange(n_items, dtype=jnp.int32), n_valid - 1)
    c = jnp.searchsorted(item_end, idx, side="right").astype(jnp.int32)
    tile = first_tile[c] + (idx - item_start[c])
    live = jnp.arange(n_items, dtype=jnp.int32) < n_valid
    lo = jnp.where(live, jnp.clip(starts[c] - tile * tm, 0, tm), 0)
    hi = jnp.where(live, jnp.clip(ends[c] - tile * tm, 0, tm), 0)
    prev_tile = jnp.concatenate([jnp.full((1,), -1, jnp.int32), tile[:-1]])
    first = (tile != prev_tile).astype(jnp.int32)
    pairs = jnp.asarray(_PAIRS, dtype=jnp.int32)
    group, pair = c // len(_PAIRS), c % len(_PAIRS)
    ea = group * EXPERTS_PER_GROUP + pairs[pair, 0]
    eb = group * EXPERTS_PER_GROUP + pairs[pair, 1]
    as_i32 = lambda a: a.astype(jnp.int32)
    return dest.astype(jnp.int32), tuple(as_i32(a) for a in (tile, ea, eb, lo, hi, first, n_valid.reshape(1)))


def _moe_kernel(h_ref, comb_ref, x1_ref, mod_ref, gf_ref, w1_ref, w3_ref, w2_ref, y_ref, acc_ref):
    e = pl.program_id(2)

    @pl.when(e == 0)
    def _():
        acc_ref[...] = jnp.zeros_like(acc_ref)

    h = h_ref[...].astype(BF16)
    a = jnp.dot(h, w1_ref[...], preferred_element_type=F32)
    b = jnp.dot(h, w3_ref[...], preferred_element_type=F32)
    lane = lax.broadcasted_iota(jnp.int32, (1, LANES), 1)
    cw = jnp.sum(jnp.where(lane == e + N_EXPERT_GROUPS, comb_ref[...], 0.0), axis=1, keepdims=True)
    hid = (a * _sigmoid(a)) * b * cw
    acc_ref[...] += jnp.dot(hid.astype(BF16), w2_ref[...], preferred_element_type=F32)

    @pl.when(e == N_EXPERTS - 1)
    def _():
        x2 = x1_ref[...] + mod_ref[5] * acc_ref[...]
        var = jnp.mean(x2 * x2, axis=-1, keepdims=True)
        y_ref[...] = (x2 * lax.rsqrt(var + RMS_EPS)) * gf_ref[...]


def _moe_dense(h2, comb, x1, mod4, gf, w1_bf, w3_bf, w2_bf, *, tm):
    nb, rows, _ = x1.shape
    tiles = rows // tm
    tile3 = lambda b, i, e: (b, i, 0)
    if mod4.shape[2] == 1:
        mod_spec = pl.BlockSpec((None, 6, 1, D_MODEL), lambda b, i, e: (b, 0, 0, 0))
    else:
        mod_spec = pl.BlockSpec((None, 6, tm, D_MODEL), lambda b, i, e: (b, 0, i, 0))
    return pl.pallas_call(
        _moe_kernel,
        grid=(nb, tiles, N_EXPERTS),
        in_specs=[
            pl.BlockSpec((None, tm, D_MODEL), tile3),
            pl.BlockSpec((None, tm, LANES), tile3),
            pl.BlockSpec((None, tm, D_MODEL), tile3),
            mod_spec,
            pl.BlockSpec((1, D_MODEL), lambda b, i, e: (0, 0)),
            pl.BlockSpec((None, D_MODEL, D_EXPERT), lambda b, i, e: (e, 0, 0)),
            pl.BlockSpec((None, D_MODEL, D_EXPERT), lambda b, i, e: (e, 0, 0)),
            pl.BlockSpec((None, D_EXPERT, D_MODEL), lambda b, i, e: (e, 0, 0)),
        ],
        out_specs=pl.BlockSpec((None, tm, D_MODEL), tile3),
        out_shape=jax.ShapeDtypeStruct((nb, rows, D_MODEL), F32),
        scratch_shapes=[pltpu.VMEM((tm, D_MODEL), F32)],
        compiler_params=_cparams(("arbitrary", "arbitrary", "arbitrary")),
        name=f"moe_dense_{nb * rows}",
    )(h2, comb, x1, mod4, gf, w1_bf, w3_bf, w2_bf)


def _s_inproj_kernel(x_ref, mod_ref, g1_ref, w_ref, z_ref):
    x = x_ref[...]
    var = jnp.mean(x * x, axis=-1, keepdims=True)
    h = (x * lax.rsqrt(var + RMS_EPS)) * (g1_ref[...] * (1.0 + mod_ref[1])) + mod_ref[0]
    z_ref[...] = jnp.dot(h, w_ref[...], precision=HIGHEST, preferred_element_type=F32)


def _s_inproj(x, mod_tok, g1, w_in, *, col_block):
    n = x.shape[0]
    return pl.pallas_call(
        _s_inproj_kernel,
        grid=(IN_COLS // col_block,),
        in_specs=[pl.BlockSpec((n, D_MODEL), lambda j: (0, 0)),
                  pl.BlockSpec((6, n, D_MODEL), lambda j: (0, 0, 0)),
                  pl.BlockSpec((1, D_MODEL), lambda j: (0, 0)),
                  pl.BlockSpec((D_MODEL, col_block), lambda j: (0, j))],
        out_specs=pl.BlockSpec((n, col_block), lambda j: (0, j)),
        out_shape=jax.ShapeDtypeStruct((n, IN_COLS), F32),
        compiler_params=_cparams(("arbitrary",)),
        name="inproj_sample",
    )(x, mod_tok, g1, w_in)


def _s_mid_kernel(z_ref, p0_ref, p1_ref, cw_ref, wco_ref, rq_ref, rk_ref,
                  cu_ref, pc_ref, sga_ref, q_ref, k_ref, v_ref, *, t_len):
    n = z_ref.shape[0]
    cu = z_ref[:, OFF_GC:OFF_GC + D_MODEL] * z_ref[:, OFF_U:OFF_U + D_MODEL]
    cu_ref[...] = cu
    t = lax.broadcasted_iota(jnp.int32, (n, D_MODEL), 0) & (t_len - 1)
    prev1 = jnp.where(t >= 1, pltpu.roll(cu, 1, 0), p1_ref[...])
    prev2 = jnp.where(t >= 2, pltpu.roll(cu, 2, 0), jnp.where(t == 0, p0_ref[...], p1_ref[...]))
    cw = cw_ref[...]
    conv = cw[0:1] * prev2 + cw[1:2] * prev1 + cw[2:3] * cu
    yc = jnp.dot(z_ref[:, OFF_GB:OFF_GB + D_MODEL] * conv, wco_ref[...], precision=HIGHEST,
                 preferred_element_type=F32)
    pc_ref[...] = _sigmoid(z_ref[:, OFF_GCONV:OFF_GCONV + D_MODEL]) * yc
    sga_ref[...] = _sigmoid(z_ref[:, OFF_GATTN:OFF_GATTN + D_MODEL])
    aq, bmq, bpq = rq_ref[0], rq_ref[1], rq_ref[2]
    ak, bmk, bpk = rk_ref[0], rk_ref[1], rk_ref[2]
    for c in range(ATTN_WIDTH // LANES):
        sl = slice(c * LANES, (c + 1) * LANES)
        q_ref[:, sl] = _rope_chunk(z_ref[:, OFF_Q + c * LANES:OFF_Q + (c + 1) * LANES], aq, bmq, bpq)
        k_ref[:, sl] = _rope_chunk(z_ref[:, OFF_K + c * LANES:OFF_K + (c + 1) * LANES], ak, bmk, bpk)
    v_ref[...] = z_ref[:, OFF_V:OFF_V + ATTN_WIDTH]


def _s_mid(z, p0e, p1e, conv_w, w_conv_out, rope_q, rope_k, *, t_len):
    n = z.shape[0]
    assert t_len & (t_len - 1) == 0
    full = lambda shape: pl.BlockSpec(shape, lambda i: (0,) * len(shape))
    out_shape = [jax.ShapeDtypeStruct((n, D_MODEL), F32)] * 3 + [jax.ShapeDtypeStruct((n, ATTN_WIDTH), F32)] * 3
    return pl.pallas_call(
        functools.partial(_s_mid_kernel, t_len=t_len),
        grid=(1,),
        in_specs=[full(z.shape), full(p0e.shape), full(p1e.shape), full(conv_w.shape), full(w_conv_out.shape),
                  full(rope_q.shape), full(rope_k.shape)],
        out_specs=[full((n, D_MODEL))] * 3 + [full((n, ATTN_WIDTH))] * 3,
        out_shape=out_shape,
        compiler_params=_cparams(("arbitrary",)),
        name="mid_sample",
    )(z, p0e, p1e, conv_w, w_conv_out, rope_q, rope_k)


def _head_sum(x):
    return jnp.sum(x.reshape(HEADS_PER_GROUP, HEAD_DIM, x.shape[-1]), axis=1)


def _head_expand(x):
    n = x.shape[-1]
    return jnp.broadcast_to(x[:, None, :], (HEADS_PER_GROUP, HEAD_DIM, n)).reshape(GROUP_WIDTH, n)


def _s_attn_disjoint(q_ref, kn, vn, ck_ref, cv_ref, o_ref, lse_ref, s_scr, lane, new_idx, *, n_tiles, dil, t_len):
    cls = lane & (dil - 1)
    q_all = q_ref[...]
    qsel = jnp.zeros((GROUP_WIDTH, LANES), F32)
    for t in range(t_len):
        qsel = jnp.where(cls == t, jnp.broadcast_to(q_all[:, t:t + 1], (GROUP_WIDTH, LANES)), qsel)
    smax = None
    for j in range(n_tiles):
        s = jnp.where(cls < t_len, _head_sum(ck_ref[:, j * LANES:(j + 1) * LANES] * qsel), NEG_BIG)
        s_scr[:, j * LANES:(j + 1) * LANES] = s
        smax = s if smax is None else jnp.maximum(smax, s)
    s_new = _head_sum(kn * q_all)
    m_cols = []
    m_lane = jnp.zeros((HEADS_PER_GROUP, LANES), F32)
    m_new = jnp.zeros((HEADS_PER_GROUP, t_len), F32)
    for t in range(t_len):
        mt = jnp.maximum(jnp.max(jnp.where(cls == t, smax, NEG_BIG), axis=1, keepdims=True), s_new[:, t:t + 1])
        m_cols.append(mt)
        m_lane = jnp.where(cls == t, mt, m_lane)
        m_new = jnp.where(new_idx == t, mt, m_new)
    esum = jnp.zeros((HEADS_PER_GROUP, LANES), F32)
    for j in range(n_tiles):
        e = jnp.exp(s_scr[:, j * LANES:(j + 1) * LANES] - m_lane)
        s_scr[:, j * LANES:(j + 1) * LANES] = e
        esum = esum + e
    e_new = jnp.exp(s_new - m_new)
    l_cols = []
    inv_lane = jnp.zeros((HEADS_PER_GROUP, LANES), F32)
    inv_new = jnp.zeros((HEADS_PER_GROUP, t_len), F32)
    for t in range(t_len):
        lt = jnp.sum(jnp.where(cls == t, esum, 0.0), axis=1, keepdims=True) + e_new[:, t:t + 1]
        l_cols.append(lt)
        inv_lane = jnp.where(cls == t, 1.0 / lt, inv_lane)
        inv_new = jnp.where(new_idx == t, 1.0 / lt, inv_new)
    acc = None
    for j in range(n_tiles):
        term = cv_ref[:, j * LANES:(j + 1) * LANES] * _head_expand(s_scr[:, j * LANES:(j + 1) * LANES] * inv_lane)
        acc = term if acc is None else acc + term
    o_new = vn * _head_expand(e_new * inv_new)
    for t in range(t_len):
        o_ref[:, t:t + 1] = jnp.sum(jnp.where(cls == t, acc, 0.0), axis=1, keepdims=True) + o_new[:, t:t + 1]
        lse_ref[:, t:t + 1] = m_cols[t] + jnp.log(l_cols[t])


def _s_attn_kernel(q_ref, kn_ref, vn_ref, kt_ref, vt_ref, ck_ref, cv_ref, o_ref, lse_ref, ko_ref, vo_ref, s_scr,
                   *, win, dil, t_len):
    n_tiles = win // LANES
    lane = lax.broadcasted_iota(jnp.int32, (1, LANES), 1)
    new_idx = lax.broadcasted_iota(jnp.int32, (1, t_len), 1)
    kn = kn_ref[...]
    vn = vn_ref[...]
    if dil >= t_len:
        _s_attn_disjoint(q_ref, kn, vn, ck_ref, cv_ref, o_ref, lse_ref, s_scr, lane, new_idx,
                         n_tiles=n_tiles, dil=dil, t_len=t_len)
    for t in range(t_len if dil < t_len else 0):
        qb = jnp.broadcast_to(q_ref[:, t:t + 1], (GROUP_WIDTH, LANES))
        m = None
        for j in range(n_tiles):
            pos = lane + j * LANES
            s = _head_sum(ck_ref[:, j * LANES:(j + 1) * LANES] * qb)
            s = jnp.where(pos >= t, jnp.where(((pos - t) & (dil - 1)) == 0, s, NEG_BIG), NEG_BIG)
            s_scr[:, j * LANES:(j + 1) * LANES] = s
            mj = jnp.max(s, axis=1, keepdims=True)
            m = mj if m is None else jnp.maximum(m, mj)
        s_new = _head_sum(kn * qb[:, 0:t_len])
        s_new = jnp.where(new_idx <= t, jnp.where(((t - new_idx) & (dil - 1)) == 0, s_new, NEG_BIG), NEG_BIG)
        m = jnp.maximum(m, jnp.max(s_new, axis=1, keepdims=True))
        e_new = jnp.exp(s_new - m)
        l = jnp.sum(e_new, axis=1, keepdims=True)
        acc = None
        for j in range(n_tiles):
            e = jnp.exp(s_scr[:, j * LANES:(j + 1) * LANES] - m)
            l = l + jnp.sum(e, axis=1, keepdims=True)
            term = cv_ref[:, j * LANES:(j + 1) * LANES] * _head_expand(e)
            acc = term if acc is None else acc + term
        o = jnp.sum(acc, axis=1, keepdims=True) + jnp.sum(vn * _head_expand(e_new), axis=1, keepdims=True)
        o_ref[:, t:t + 1] = o * _head_expand(1.0 / l)
        lse_ref[:, t:t + 1] = m + jnp.log(l)

    for c_ref, tail_ref, out_ref in ((ck_ref, kt_ref, ko_ref), (cv_ref, vt_ref, vo_ref)):
        rolled = pltpu.roll(c_ref[...], win - t_len, 1)
        if win > LANES:
            out_ref[:, 0:win - LANES] = rolled[:, 0:win - LANES]
        out_ref[:, win - LANES:win] = jnp.where(lane >= LANES - t_len, tail_ref[...], rolled[:, win - LANES:win])


def _s_attn_group(q_t, kn_t, vn_t, k_tail, v_tail, cache_k, cache_v, g):
    bsz, _, t_len = q_t.shape
    win_full, dil = DIL_GROUPS[g]
    win = cache_k.shape[2]
    assert win == win_full and win == (KEYS_PER_QUERY - 1) * dil and win % LANES == 0
    assert dil & (dil - 1) == 0 and LANES % dil == 0
    grp = lambda b: (b, g, 0)
    per_b = lambda b: (b, 0, 0)
    return pl.pallas_call(
        functools.partial(_s_attn_kernel, win=win, dil=dil, t_len=t_len),
        grid=(bsz,),
        in_specs=[pl.BlockSpec((None, GROUP_WIDTH, t_len), grp)] * 3
        + [pl.BlockSpec((None, GROUP_WIDTH, LANES), grp)] * 2
        + [pl.BlockSpec((None, GROUP_WIDTH, win), per_b)] * 2,
        out_specs=[pl.BlockSpec((None, GROUP_WIDTH, t_len), per_b),
                   pl.BlockSpec((None, HEADS_PER_GROUP, t_len), per_b),
                   pl.BlockSpec((None, GROUP_WIDTH, win), per_b),
                   pl.BlockSpec((None, GROUP_WIDTH, win), per_b)],
        out_shape=[jax.ShapeDtypeStruct((bsz, GROUP_WIDTH, t_len), F32),
                   jax.ShapeDtypeStruct((bsz, HEADS_PER_GROUP, t_len), F32),
                   jax.ShapeDtypeStruct((bsz, GROUP_WIDTH, win), F32),
                   jax.ShapeDtypeStruct((bsz, GROUP_WIDTH, win), F32)],
        scratch_shapes=[pltpu.VMEM((HEADS_PER_GROUP, win), F32)],
        compiler_params=_cparams(("arbitrary",)),
        name=f"attn_sample_g{g}",
    )(q_t, kn_t, vn_t, k_tail, v_tail, cache_k, cache_v)


TM_INPROJ = 512
TM_POST = 512
TM_EXPERT = 256
ROWS_PER_DMA_STEP = 128
ATTN_MAX_CHUNK = 1024
S_COL_BLOCK = 512


def _to_state(a_t):
    b, _, length = a_t.shape
    return jnp.transpose(a_t.reshape(b, HEADS_PER_GROUP, HEAD_DIM, length), (0, 3, 1, 2))[None]


def _from_state(a):
    b, length = a.shape[0], a.shape[1]
    return jnp.transpose(a, (0, 2, 3, 1)).reshape(b, GROUP_WIDTH, length)


def kernel(x_prompt, x_sample, cache_k1, cache_v1, cache_k2, cache_v2, cache_k3, cache_v3, state_conv,
           c_prompt, c_sample, norm1_g, norm2_g, normf_g, w_ada, b_ada, w_in, conv_w, w_conv_out,
           w_attn_out, w_o, w_rg, b_rg, w_re, b_re, w1, w3, w2):
    depth = w_in.shape[0]
    assert depth == 1
    bsz, seq, _ = x_prompt.shape
    dbsz, t_len, _ = x_sample.shape
    n_s = dbsz * t_len
    l = 0

    w_in_bf = w_in[l].astype(BF16)
    wco_bf = w_conv_out[l].astype(BF16)
    wao_bf = w_attn_out[l].astype(BF16)
    wo_bf = w_o[l].astype(BF16)
    w1_bf = w1[l].reshape(N_EXPERTS, D_MODEL, D_EXPERT).astype(BF16)
    w3_bf = w3[l].reshape(N_EXPERTS, D_MODEL, D_EXPERT).astype(BF16)
    w2_bf = w2[l].reshape(N_EXPERTS, D_EXPERT, D_MODEL).astype(BF16)
    g1 = norm1_g[l].reshape(1, D_MODEL)
    g2 = norm2_g[l].reshape(1, D_MODEL)
    gf = normf_g.reshape(1, D_MODEL)
    n_route = N_EXPERT_GROUPS + N_EXPERTS
    wr = jnp.pad(jnp.concatenate([w_rg[l], w_re[l]], axis=1), ((0, 0), (0, LANES - n_route)))
    br = jnp.pad(jnp.concatenate([b_rg[l], b_re[l]]), (0, LANES - n_route)).reshape(1, LANES)
    head_of_lane = np.arange(GROUP_WIDTH) // HEAD_DIM
    expand_bf = jnp.asarray((np.arange(LANES)[:, None] == head_of_lane[None, :]).astype(np.float32), dtype=BF16)

    mod = _adaln(jnp.concatenate([c_prompt, c_sample], axis=0), w_ada[l], b_ada[l])
    mod_p4 = mod[:bsz].reshape(bsz, 6, 1, D_MODEL)
    mod_tok = jnp.repeat(mod[bsz:].reshape(dbsz, 1, 6, D_MODEL), t_len, axis=1)
    mod_tok = jnp.transpose(mod_tok.reshape(n_s, 6, D_MODEL), (1, 0, 2))
    mod_s4 = mod_tok[None]

    pos_p = jnp.arange(seq, dtype=jnp.int32)
    rope_q_p = _rope_tables(pos_p, HEAD_DIM ** -0.5)
    rope_k_p = _rope_tables(pos_p, 1.0)
    pc_p, sga_p, conv_p = _convproj_prompt(x_prompt, mod_p4, g1, w_in_bf[:, :OFF_Q], w_in_bf[:, OFF_GCONV:],
                                           conv_w[l], wco_bf, tm=TM_INPROJ)
    outs = _qkvproj_prompt(x_prompt, mod_p4, g1, w_in_bf[:, OFF_Q:OFF_GCONV], rope_q_p, rope_k_p, tm=TM_INPROJ)
    qkv_p = outs[0:3 * N_DIL]
    states_p = outs[3 * N_DIL:]
    o_p, lse_p = [], []
    for g in range(N_DIL):
        o_g, lse_g = _attn_prompt_group(qkv_p[3 * g], qkv_p[3 * g + 1], qkv_p[3 * g + 2], g, max_chunk=ATTN_MAX_CHUNK)
        o_p.append(o_g)
        lse_p.append(lse_g)
    x1_p, h2pk_p, meta_p, cnt_p = _post(o_p, lse_p, pc_p, sga_p, x_prompt, mod_p4, wao_bf, wo_bf, g2, wr, br, expand_bf,
                                        tm=TM_POST, full_precision=False, routed=True)
    n_p = bsz * seq
    meta_flat = meta_p.reshape(n_p, LANES)
    dest, tables = _routing_tables(meta_flat[:, META_CLASS].astype(jnp.int32), meta_flat[:, META_RANK].astype(jnp.int32),
                                   cnt_p[0, :N_CLASSES], n=n_p, tm=TM_EXPERT)
    hs = _dispatch_rows(h2pk_p.reshape(n_p, D_MODEL // 2), dest, rows_per_step=ROWS_PER_DMA_STEP)
    yp = _experts_routed(hs, tables, w1_bf, w3_bf, w2_bf, tm=TM_EXPERT)
    y_p = _combine_final(yp, dest, x1_p.reshape(n_p, D_MODEL), meta_flat, mod_p4, gf,
                         rows_per_step=ROWS_PER_DMA_STEP, rows_per_mod=seq).reshape(bsz, seq, D_MODEL)

    pos_s = PAST_LEN + jnp.arange(t_len, dtype=jnp.int32)
    rope_q_s = jnp.tile(_rope_tables(pos_s, HEAD_DIM ** -0.5), (1, dbsz, 1))
    rope_k_s = jnp.tile(_rope_tables(pos_s, 1.0), (1, dbsz, 1))
    xs = x_sample.reshape(n_s, D_MODEL)
    z_s = _s_inproj(xs, mod_tok, g1, w_in[l], col_block=S_COL_BLOCK)
    past = state_conv[l]
    p0e = jnp.repeat(past[:, 0], t_len, axis=0)
    p1e = jnp.repeat(past[:, 1], t_len, axis=0)
    cu_s, pc_s, sga_s, q_s, k_s, v_s = _s_mid(z_s, p0e, p1e, conv_w[l], w_conv_out[l], rope_q_s, rope_k_s,
                                              t_len=t_len)
    to_cols = lambda a: jnp.transpose(a.reshape(dbsz, t_len, ATTN_WIDTH), (0, 2, 1))
    q_t, kn_t, vn_t = to_cols(q_s), to_cols(k_s), to_cols(v_s)
    k_tail = jnp.pad(kn_t, ((0, 0), (0, 0), (LANES - t_len, 0)))
    v_tail = jnp.pad(vn_t, ((0, 0), (0, 0), (LANES - t_len, 0)))
    caches = ((cache_k1, cache_v1), (cache_k2, cache_v2), (cache_k3, cache_v3))
    o_s, lse_s, kv_s = [], [], []
    for g, (ck, cv) in enumerate(caches):
        o_g, lse_g, ko, vo = _s_attn_group(q_t, kn_t, vn_t, k_tail, v_tail, _from_state(ck[l]), _from_state(cv[l]), g)
        o_s.append(jnp.transpose(o_g, (0, 2, 1)).reshape(1, 1, n_s, GROUP_WIDTH))
        lse_rows = jnp.transpose(lse_g, (0, 2, 1)).reshape(n_s, HEADS_PER_GROUP)
        lse_s.append(jnp.pad(lse_rows, ((0, 0), (0, LANES - HEADS_PER_GROUP))).reshape(1, 1, n_s, LANES))
        kv_s += [_to_state(ko), _to_state(vo)]
    x1_s, h2_s, comb_s = _post(o_s, lse_s, pc_s[None], sga_s[None], xs[None], mod_s4, w_attn_out[l], w_o[l], g2, wr, br,
                               expand_bf, tm=n_s, full_precision=True, routed=False)
    y_s = _moe_dense(h2_s, comb_s, x1_s, mod_s4, gf, w1_bf, w3_bf, w2_bf, tm=n_s)

    conv_s = cu_s.reshape(dbsz, t_len, D_MODEL)[:, t_len - (CONV_K - 1):]
    return (y_p, y_s.reshape(dbsz, t_len, D_MODEL),
            *[_to_state(a) for a in states_p], conv_p.reshape(1, bsz, CONV_K - 1, D_MODEL),
            *kv_s, conv_s.reshape(1, dbsz, CONV_K - 1, D_MODEL))
```

```python
import functools

import numpy as np
import jax
import jax.numpy as jnp
from jax import lax
from jax.experimental import pallas as pl
from jax.experimental.pallas import tpu as pltpu

F32 = jnp.float32
BF16 = jnp.bfloat16
HIGHEST = lax.Precision.HIGHEST

D_MODEL = 1024
HEAD_DIM = 64
HEADS_PER_GROUP = 8
GROUP_WIDTH = HEADS_PER_GROUP * HEAD_DIM
DIL_GROUPS = ((128, 1), (512, 4), (2048, 16))
N_DIL = len(DIL_GROUPS)
ATTN_WIDTH = N_DIL * GROUP_WIDTH
ROT_DIM = HEAD_DIM // 4
ROPE_THETA = 500000.0
PAST_LEN = 16384
CONV_K = 3
N_EXPERT_GROUPS = 4
EXPERTS_PER_GROUP = 4
N_EXPERTS = N_EXPERT_GROUPS * EXPERTS_PER_GROUP
D_EXPERT = 512
RMS_EPS = 1e-6
IN_COLS = 3 * D_MODEL + 3 * ATTN_WIDTH + 2 * D_MODEL
OFF_U, OFF_GC, OFF_GB = 0, D_MODEL, 2 * D_MODEL
OFF_Q = 3 * D_MODEL
OFF_K = OFF_Q + ATTN_WIDTH
OFF_V = OFF_K + ATTN_WIDTH
OFF_GCONV = OFF_V + ATTN_WIDTH
OFF_GATTN = OFF_GCONV + D_MODEL

LANES = 128
SUBLANES = 8
CHUNKS_PER_GROUP = GROUP_WIDTH // LANES
KEYS_PER_QUERY = 129
Q_BLOCK = 128
NEG_BIG = -1e30

VMEM_LIMIT = 56 * 1024 * 1024


def _sigmoid(x):
    return 1.0 / (1.0 + jnp.exp(-x))


def _cparams(sem):
    return pltpu.CompilerParams(dimension_semantics=sem, vmem_limit_bytes=VMEM_LIMIT)


def _adaln_kernel(c_ref, w_ref, b_ref, o_ref):
    c = c_ref[...]
    s = c * _sigmoid(c)
    o_ref[...] = jnp.dot(s, w_ref[...], precision=HIGHEST, preferred_element_type=F32) + b_ref[...]


def _adaln(c_all, w_ada, b_ada):
    rows = c_all.shape[0]
    n_col = w_ada.shape[1] // D_MODEL
    return pl.pallas_call(
        _adaln_kernel,
        grid=(n_col,),
        in_specs=[
            pl.BlockSpec((rows, D_MODEL), lambda j: (0, 0)),
            pl.BlockSpec((D_MODEL, D_MODEL), lambda j: (0, j)),
            pl.BlockSpec((1, D_MODEL), lambda j: (0, j)),
        ],
        out_specs=pl.BlockSpec((rows, D_MODEL), lambda j: (0, j)),
        out_shape=jax.ShapeDtypeStruct((rows, w_ada.shape[1]), F32),
        compiler_params=_cparams(("arbitrary",)),
        name="adaln",
    )(c_all, w_ada, b_ada.reshape(1, -1))


def _rope_tables(pos, scale):
    half = ROT_DIM // 2
    inv_freq = jnp.power(jnp.float32(ROPE_THETA), -jnp.arange(half, dtype=F32) / half)
    ang = pos.astype(F32)[:, None] * inv_freq[None, :]
    cos, sin = jnp.cos(ang), jnp.sin(ang)
    lane_in_head = np.arange(LANES) % HEAD_DIM
    freq = lane_in_head % half
    first = lane_in_head < half
    second = (lane_in_head >= half) & (lane_in_head < ROT_DIM)
    a = jnp.where(first | second, cos[:, freq], 1.0)
    bm = jnp.where(first, -sin[:, freq], 0.0)
    bp = jnp.where(second, sin[:, freq], 0.0)
    return jnp.stack([a, bm, bp]) * scale


def _rope_chunk(zc, a, bm, bp):
    return zc * a + pltpu.roll(zc, LANES - ROT_DIM // 2, 1) * bm + pltpu.roll(zc, ROT_DIM // 2, 1) * bp


def _modulated_norm_bf16(x_ref, mod_ref, g1_ref):
    x = x_ref[...]
    var = jnp.mean(x * x, axis=-1, keepdims=True)
    h = (x * lax.rsqrt(var + RMS_EPS)) * (g1_ref[...] * (1.0 + mod_ref[1])) + mod_ref[0]
    return h.astype(BF16)


def _convproj_kernel(x_ref, mod_ref, g1_ref, wa_ref, wg_ref, cw_ref, wco_ref, pc_ref, sga_ref, cst_ref, s_ref,
                     *, tm, n_tiles):
    i = pl.program_id(1)
    hb = _modulated_norm_bf16(x_ref, mod_ref, g1_ref)

    def proj(w_ref, lo):
        return jnp.dot(hb, w_ref[:, lo:lo + D_MODEL], preferred_element_type=F32)

    cu = proj(wa_ref, OFF_GC) * proj(wa_ref, OFF_U)

    @pl.when(i == 0)
    def _():
        s_ref[0:SUBLANES, :] = jnp.zeros((SUBLANES, D_MODEL), F32)

    s_ref[SUBLANES:SUBLANES + tm, :] = cu
    cw = cw_ref[...]
    conv = (cw[0:1] * s_ref[SUBLANES - 2:SUBLANES - 2 + tm, :]
            + cw[1:2] * s_ref[SUBLANES - 1:SUBLANES - 1 + tm, :]
            + cw[2:3] * cu)
    yc = jnp.dot((proj(wa_ref, OFF_GB) * conv).astype(BF16), wco_ref[...], preferred_element_type=F32)
    pc_ref[...] = (_sigmoid(proj(wg_ref, 0)) * yc).astype(pc_ref.dtype)
    sga_ref[...] = _sigmoid(proj(wg_ref, D_MODEL)).astype(sga_ref.dtype)

    @pl.when(i == n_tiles - 1)
    def _():
        cst_ref[...] = s_ref[tm + SUBLANES - 2:tm + SUBLANES, :]

    s_ref[0:SUBLANES, :] = s_ref[tm:tm + SUBLANES, :]


def _convproj_prompt(x, mod4, g1, wa_bf, wg_bf, conv_w, wco_bf, *, tm):
    bsz, seq, _ = x.shape
    n_tiles = seq // tm
    const2 = lambda b, i: (0, 0)
    tile3 = lambda b, i: (b, i, 0)
    return pl.pallas_call(
        functools.partial(_convproj_kernel, tm=tm, n_tiles=n_tiles),
        grid=(bsz, n_tiles),
        in_specs=[
            pl.BlockSpec((None, tm, D_MODEL), tile3),
            pl.BlockSpec((None, 6, 1, D_MODEL), lambda b, i: (b, 0, 0, 0)),
            pl.BlockSpec((1, D_MODEL), const2),
            pl.BlockSpec(wa_bf.shape, const2, pipeline_mode=pl.Buffered(1)),
            pl.BlockSpec(wg_bf.shape, const2, pipeline_mode=pl.Buffered(1)),
            pl.BlockSpec((CONV_K, D_MODEL), const2),
            pl.BlockSpec((D_MODEL, D_MODEL), const2, pipeline_mode=pl.Buffered(1)),
        ],
        out_specs=[pl.BlockSpec((None, tm, D_MODEL), tile3), pl.BlockSpec((None, tm, D_MODEL), tile3),
                   pl.BlockSpec((None, CONV_K - 1, D_MODEL), lambda b, i: (b, 0, 0))],
        out_shape=[jax.ShapeDtypeStruct((bsz, seq, D_MODEL), BF16), jax.ShapeDtypeStruct((bsz, seq, D_MODEL), BF16),
                   jax.ShapeDtypeStruct((bsz, CONV_K - 1, D_MODEL), F32)],
        scratch_shapes=[pltpu.VMEM((tm + SUBLANES, D_MODEL), F32)],
        compiler_params=_cparams(("arbitrary", "arbitrary")),
        name="convproj_prompt",
    )(x, mod4, g1, wa_bf, wg_bf, conv_w, wco_bf)


def _qkvproj_kernel(x_ref, mod_ref, g1_ref, w_ref, rq_ref, rk_ref, *rest, tm, n_tiles, seq):
    qkv_refs = rest[0:3 * N_DIL]
    st_refs = rest[3 * N_DIL:3 * N_DIL + 2 * N_DIL]
    d_ref = rest[-1]
    i = pl.program_id(1)
    hb = _modulated_norm_bf16(x_ref, mod_ref, g1_ref)

    def proj(lo, width):
        return jnp.dot(hb, w_ref[:, lo:lo + width], preferred_element_type=F32)

    zq = proj(0, ATTN_WIDTH)
    zk = proj(ATTN_WIDTH, ATTN_WIDTH)
    zv = proj(2 * ATTN_WIDTH, ATTN_WIDTH)
    aq, bmq, bpq = rq_ref[0], rq_ref[1], rq_ref[2]
    ak, bmk, bpk = rk_ref[0], rk_ref[1], rk_ref[2]
    n_chunks = ATTN_WIDTH // LANES
    q_chunks, k_chunks, v_chunks = [], [], []
    for c in range(n_chunks):
        sl = slice(c * LANES, (c + 1) * LANES)
        q_chunks.append(_rope_chunk(zq[:, sl], aq, bmq, bpq))
        k_chunks.append(_rope_chunk(zk[:, sl], ak, bmk, bpk))
        v_chunks.append(zv[:, sl])

    for which, chunks in enumerate((q_chunks, k_chunks, v_chunks)):
        for g in range(N_DIL):
            out_ref = qkv_refs[3 * g + which]
            dil = DIL_GROUPS[g][1]
            for cc in range(CHUNKS_PER_GROUP):
                c = g * CHUNKS_PER_GROUP + cc
                sl = slice(cc * LANES, (cc + 1) * LANES)
                if dil == 1:
                    out_ref[0, :, sl] = chunks[c].astype(BF16)
                else:
                    d_ref[c] = chunks[c]
                    for r in range(dil):
                        out_ref[r, :, sl] = d_ref[c, pl.ds(r, tm // dil, stride=dil), :].astype(BF16)

    for g in range(N_DIL):
        kst, vst = st_refs[2 * g], st_refs[2 * g + 1]
        win = min(DIL_GROUPS[g][0], seq)
        if win >= tm:
            cond, r0 = i >= (seq - win) // tm, 0
        else:
            cond, r0 = i == n_tiles - 1, tm - win

        @pl.when(cond)
        def _(g=g, kst=kst, vst=vst, r0=r0):
            for cc in range(CHUNKS_PER_GROUP):
                c = g * CHUNKS_PER_GROUP + cc
                kst[cc * LANES:(cc + 1) * LANES, :] = k_chunks[c][r0:, :].T
                vst[cc * LANES:(cc + 1) * LANES, :] = v_chunks[c][r0:, :].T


def _qkvproj_prompt(x, mod4, g1, wqkv_bf, rope_q, rope_k, *, tm):
    bsz, seq, _ = x.shape
    n_tiles = seq // tm
    const2 = lambda b, i: (0, 0)
    tile3 = lambda b, i: (b, i, 0)
    in_specs = [
        pl.BlockSpec((None, tm, D_MODEL), tile3),
        pl.BlockSpec((None, 6, 1, D_MODEL), lambda b, i: (b, 0, 0, 0)),
        pl.BlockSpec((1, D_MODEL), const2),
        pl.BlockSpec(wqkv_bf.shape, const2, pipeline_mode=pl.Buffered(1)),
        pl.BlockSpec((3, tm, LANES), lambda b, i: (0, i, 0)),
        pl.BlockSpec((3, tm, LANES), lambda b, i: (0, i, 0)),
    ]
    out_shape, out_specs = [], []
    for _, dil in DIL_GROUPS:
        assert tm % (dil * 16) == 0
        for _ in range(3):
            out_shape.append(jax.ShapeDtypeStruct((bsz, dil, seq // dil, GROUP_WIDTH), BF16))
            out_specs.append(pl.BlockSpec((None, dil, tm // dil, GROUP_WIDTH), lambda b, i: (b, 0, i, 0)))
    for win, _ in DIL_GROUPS:
        win = min(win, seq)
        cols = min(win, tm)
        if win >= tm:
            imap = lambda b, i, ft=(seq - win) // tm: (b, 0, jnp.maximum(i - ft, 0))
        else:
            imap = lambda b, i: (b, 0, 0)
        for _ in range(2):
            out_shape.append(jax.ShapeDtypeStruct((bsz, GROUP_WIDTH, win), F32))
            out_specs.append(pl.BlockSpec((None, GROUP_WIDTH, cols), imap))
    return pl.pallas_call(
        functools.partial(_qkvproj_kernel, tm=tm, n_tiles=n_tiles, seq=seq),
        grid=(bsz, n_tiles),
        in_specs=in_specs,
        out_specs=out_specs,
        out_shape=out_shape,
        scratch_shapes=[pltpu.VMEM((ATTN_WIDTH // LANES, tm, LANES), F32)],
        compiler_params=_cparams(("arbitrary", "arbitrary")),
        name="qkvproj_prompt",
    )(x, mod4, g1, wqkv_bf, rope_q, rope_k)


def _attn_kernel(*refs, chunk, has_halo):
    if has_halo:
        q_ref, k_ref, v_ref, kh_ref, vh_ref, o_ref, lse_ref, kbuf, vbuf = refs
    else:
        q_ref, k_ref, v_ref, o_ref, lse_ref, kbuf, vbuf = refs
    c = pl.program_id(2)
    if has_halo:
        kbuf[0:Q_BLOCK, :] = kh_ref[...]
        vbuf[0:Q_BLOCK, :] = vh_ref[...]
    else:
        kbuf[0:Q_BLOCK, :] = jnp.zeros((Q_BLOCK, GROUP_WIDTH), BF16)
        vbuf[0:Q_BLOCK, :] = jnp.zeros((Q_BLOCK, GROUP_WIDTH), BF16)
    kbuf[Q_BLOCK:Q_BLOCK + chunk, :] = k_ref[...]
    vbuf[Q_BLOCK:Q_BLOCK + chunk, :] = v_ref[...]

    row = lax.broadcasted_iota(jnp.int32, (Q_BLOCK, 2 * Q_BLOCK), 0)
    col = lax.broadcasted_iota(jnp.int32, (Q_BLOCK, 2 * Q_BLOCK), 1)
    bias_main = jnp.where(col >= row, jnp.where(col <= row + Q_BLOCK, 0.0, NEG_BIG), NEG_BIG)
    bias_first = jnp.where(col >= Q_BLOCK, bias_main, NEG_BIG)
    lane = lax.broadcasted_iota(jnp.int32, (1, LANES), 1)
    lo_half = lane < HEAD_DIM

    def body(qb, carry):
        r0 = pl.multiple_of(qb * Q_BLOCK, Q_BLOCK)
        kt = kbuf[pl.ds(r0, 2 * Q_BLOCK), :]
        vt = vbuf[pl.ds(r0, 2 * Q_BLOCK), :]
        qt = q_ref[pl.ds(r0, Q_BLOCK), :]
        is_first = jnp.logical_and(qb == 0, c == 0)
        bias = jnp.where(is_first, bias_first, bias_main)
        lse_tile = jnp.zeros((Q_BLOCK, LANES), F32)
        for p in range(CHUNKS_PER_GROUP):
            sl = slice(p * LANES, (p + 1) * LANES)
            qp, kp, vp = qt[:, sl], kt[:, sl], vt[:, sl]
            outs = []
            for hh in range(2):
                head_lanes = lo_half if hh == 0 else jnp.logical_not(lo_half)
                qm = jnp.where(head_lanes, qp, jnp.zeros_like(qp))
                s = lax.dot_general(qm, kp, (((1,), (1,)), ((), ())), preferred_element_type=F32) + bias
                m = jnp.max(s, axis=1, keepdims=True)
                e = jnp.exp(s - m)
                l = jnp.sum(e, axis=1, keepdims=True)
                o = jnp.dot(e.astype(BF16), vp, preferred_element_type=F32)
                outs.append(o * (1.0 / l))
                lse_tile = jnp.where(lane == 2 * p + hh, m + jnp.log(l), lse_tile)
            o_ref[pl.ds(r0, Q_BLOCK), sl] = jnp.where(lo_half, outs[0], outs[1])
        lse_ref[pl.ds(r0, Q_BLOCK), :] = lse_tile
        return carry

    lax.fori_loop(0, chunk // Q_BLOCK, body, 0)


def _attn_prompt_group(q, k, v, g, *, max_chunk):
    bsz, dil, cls_len, _ = q.shape
    chunk = min(max_chunk, cls_len)
    n_chunks = cls_len // chunk
    has_halo = n_chunks > 1
    main = pl.BlockSpec((None, None, chunk, GROUP_WIDTH), lambda b, r, c: (b, r, c, 0))
    in_specs = [main, main, main]
    args = [q, k, v]
    if has_halo:
        per = chunk // Q_BLOCK
        halo = pl.BlockSpec((None, None, Q_BLOCK, GROUP_WIDTH), lambda b, r, c: (b, r, jnp.maximum(c * per - 1, 0), 0))
        in_specs += [halo, halo]
        args += [k, v]
    return pl.pallas_call(
        functools.partial(_attn_kernel, chunk=chunk, has_halo=has_halo),
        grid=(bsz, dil, n_chunks),
        in_specs=in_specs,
        out_specs=[pl.BlockSpec((None, None, chunk, GROUP_WIDTH), lambda b, r, c: (b, r, c, 0)),
                   pl.BlockSpec((None, None, chunk, LANES), lambda b, r, c: (b, r, c, 0))],
        out_shape=[jax.ShapeDtypeStruct((bsz, dil, cls_len, GROUP_WIDTH), F32),
                   jax.ShapeDtypeStruct((bsz, dil, cls_len, LANES), F32)],
        scratch_shapes=[pltpu.VMEM((chunk + Q_BLOCK, GROUP_WIDTH), BF16),
                        pltpu.VMEM((chunk + Q_BLOCK, GROUP_WIDTH), BF16)],
        compiler_params=_cparams(("arbitrary", "arbitrary", "arbitrary")),
        name=f"attn_prompt_g{g}",
    )(*args)


def _split_bf16(x, n):
    parts = []
    r = x
    for _ in range(n):
        p = r.astype(BF16)
        parts.append(p)
        r = r - p.astype(F32)
    return parts


def _pack_bf16_pair(a, b):
    ua = pltpu.bitcast(a.astype(BF16).astype(F32), jnp.uint32)
    ub = pltpu.bitcast(b.astype(BF16).astype(F32), jnp.uint32)
    return ua | (ub >> 16)


def _unpack_bf16_pair(w):
    a = pltpu.bitcast(w & jnp.uint32(0xFFFF0000), F32)
    b = pltpu.bitcast(w << 16, F32)
    return a, b


_PAIRS = ((0, 1), (0, 2), (0, 3), (1, 2), (1, 3), (2, 3))
N_CLASSES = N_EXPERT_GROUPS * len(_PAIRS)
META_CLASS, META_RANK, META_WA, META_WB = 0, 1, 2, 3


def _post_kernel(*refs, tm, dils, full_precision, routed):
    (o0_ref, o1_ref, o2_ref, l0_ref, l1_ref, l2_ref, pc_ref, sga_ref, x_ref, mod_ref,
     wao_ref, wo_ref, g2_ref, wr_ref, wrh_ref, wrl_ref, br_ref, exp_ref) = refs[:18]
    if routed:
        tri_ref, x1_ref, h2_ref, meta_ref, route_ref, cnt_ref, o_scr, l_scr, run_scr = refs[18:]
    else:
        x1_ref, h2_ref, comb_ref, o_scr, l_scr = refs[18:]

    def mm(a, w_ref):
        if full_precision:
            return jnp.dot(a, w_ref[...], precision=HIGHEST, preferred_element_type=F32)
        return jnp.dot(a.astype(BF16), w_ref[...], preferred_element_type=F32)

    def natural_order(ref, scr, dil, n_chunks):
        if dil == 1:
            return [ref[0, :, c * LANES:(c + 1) * LANES] for c in range(n_chunks)]
        out = []
        for c in range(n_chunks):
            for r in range(dil):
                scr[c, pl.ds(r, tm // dil, stride=dil), :] = ref[r, :, c * LANES:(c + 1) * LANES]
            out.append(scr[c])
        return out

    lses = [natural_order(ref, l_scr.at[g], dils[g], 1)[0] for g, ref in enumerate((l0_ref, l1_ref, l2_ref))]
    mx = jnp.maximum(lses[0], jnp.maximum(lses[1], lses[2]))
    es = [jnp.exp(v - mx) for v in lses]
    inv = 1.0 / (es[0] + es[1] + es[2])
    expand = exp_ref[...]
    attn_o = None
    for g, o_ref in enumerate((o0_ref, o1_ref, o2_ref)):
        w = es[g] * inv
        we = None
        for part in _split_bf16(w, 3 if full_precision else 2):
            t = jnp.dot(part, expand, preferred_element_type=F32)
            we = t if we is None else we + t
        o_nat = jnp.concatenate(natural_order(o_ref, o_scr.at[g], dils[g], CHUNKS_PER_GROUP), axis=1)
        term = we * o_nat
        attn_o = term if attn_o is None else attn_o + term

    y_attn = mm(attn_o, wao_ref)
    mixed = mm(pc_ref[...] + sga_ref[...] * y_attn, wo_ref)
    x1 = x_ref[...] + mod_ref[2] * mixed
    x1_ref[...] = x1
    var = jnp.mean(x1 * x1, axis=-1, keepdims=True)
    h2 = (x1 * lax.rsqrt(var + RMS_EPS)) * (g2_ref[...] * (1.0 + mod_ref[4])) + mod_ref[3]
    if routed:
        h2_ref[...] = _pack_bf16_pair(h2[:, :D_MODEL // 2], h2[:, D_MODEL // 2:])
    else:
        h2_ref[...] = h2.astype(h2_ref.dtype)

    if full_precision:
        lg = jnp.dot(h2, wr_ref[...], precision=HIGHEST, preferred_element_type=F32) + br_ref[...]
    else:
        h_hi, h_lo = _split_bf16(h2, 2)
        lg = (jnp.dot(h_hi, wrh_ref[...], preferred_element_type=F32)
              + jnp.dot(h_lo, wrh_ref[...], preferred_element_type=F32)
              + jnp.dot(h_hi, wrl_ref[...], preferred_element_type=F32)) + br_ref[...]
    lane_i = lax.broadcasted_iota(jnp.int32, (1, LANES), 1)
    lane = lane_i.astype(F32)
    lane_group = ((lane_i - N_EXPERT_GROUPS) >> 2).astype(F32)
    big = jnp.float32(1e9)
    gl = jnp.where(lane_i < N_EXPERT_GROUPS, lg, NEG_BIG)
    gmax = jnp.max(gl, axis=1, keepdims=True)
    gidx = jnp.min(jnp.where(gl == gmax, lane, big), axis=1, keepdims=True)
    g_w = 1.0 / jnp.sum(jnp.exp(gl - gmax), axis=1, keepdims=True)
    el = jnp.where(lane_group == gidx, lg, NEG_BIG)
    v1 = jnp.max(el, axis=1, keepdims=True)
    i1 = jnp.min(jnp.where(el == v1, lane, big), axis=1, keepdims=True)
    el2 = jnp.where(lane == i1, NEG_BIG, el)
    v2 = jnp.max(el2, axis=1, keepdims=True)
    i2 = jnp.min(jnp.where(el2 == v2, lane, big), axis=1, keepdims=True)
    t = jnp.exp(v2 - v1)
    den = 1.0 / (1.0 + t)
    w_top1, w_top2 = g_w * den, g_w * (t * den)
    if not routed:
        comb_ref[...] = jnp.where(lane == i1, w_top1, jnp.where(lane == i2, w_top2, 0.0))
        return

    base = jnp.float32(N_EXPERT_GROUPS) + jnp.float32(EXPERTS_PER_GROUP) * gidx
    e1, e2 = i1 - base, i2 - base
    first_is_lower = e1 < e2
    ea = jnp.where(first_is_lower, e1, e2)
    eb = jnp.where(first_is_lower, e2, e1)
    pair = ea * (7.0 - ea) * 0.5 + (eb - ea - 1.0)
    cls = gidx * jnp.float32(len(_PAIRS)) + pair
    onehot = lane == cls
    earlier = jnp.dot(tri_ref[...], jnp.where(onehot, 1.0, 0.0).astype(BF16), preferred_element_type=F32)

    @pl.when(jnp.logical_and(pl.program_id(0) == 0, pl.program_id(1) == 0))
    def _():
        run_scr[...] = jnp.zeros_like(run_scr)

    running = run_scr[...]
    rank = jnp.sum(jnp.where(onehot, earlier + running, 0.0), axis=1, keepdims=True)
    running = running + jnp.sum(jnp.where(onehot, 1.0, 0.0), axis=0, keepdims=True)
    run_scr[...] = running
    cnt_ref[...] = running
    w_a = jnp.where(first_is_lower, w_top1, w_top2)
    w_b = jnp.where(first_is_lower, w_top2, w_top1)
    meta = jnp.where(lane_i == META_CLASS, cls,
                     jnp.where(lane_i == META_RANK, rank,
                               jnp.where(lane_i == META_WA, w_a, jnp.where(lane_i == META_WB, w_b, 0.0))))
    meta_ref[...] = meta
    route_ref[...] = meta.T[0:SUBLANES, :].astype(jnp.int32)


def _post(o_list, lse_list, pc, sga, x, mod4, wao, wo, g2, wr, br, expand, *, tm, full_precision, routed):
    nb, rows, _ = x.shape
    tiles = rows // tm
    dils = tuple(o.shape[1] for o in o_list)
    tile3 = lambda b, i: (b, i, 0)
    const2 = lambda b, i: (0, 0)
    mod_rows = mod4.shape[2]
    if mod_rows == 1:
        mod_spec = pl.BlockSpec((None, 6, 1, D_MODEL), lambda b, i: (b, 0, 0, 0))
    else:
        mod_spec = pl.BlockSpec((None, 6, tm, D_MODEL), lambda b, i: (b, 0, i, 0))
    cls4 = lambda b, i: (b, 0, i, 0)
    in_specs = (
        [pl.BlockSpec((None, d, tm // d, GROUP_WIDTH), cls4) for d in dils]
        + [pl.BlockSpec((None, d, tm // d, LANES), cls4) for d in dils]
        + [pl.BlockSpec((None, tm, D_MODEL), tile3)] * 3 + [mod_spec]
        + [pl.BlockSpec(wao.shape, const2), pl.BlockSpec(wo.shape, const2), pl.BlockSpec((1, D_MODEL), const2)]
        + [pl.BlockSpec((D_MODEL, LANES), const2)] * 3
        + [pl.BlockSpec((1, LANES), const2), pl.BlockSpec((LANES, GROUP_WIDTH), const2)]
    )
    wr_hi = wr.astype(BF16)
    wr_lo = (wr - wr_hi.astype(F32)).astype(BF16)
    args = [*o_list, *lse_list, pc, sga, x, mod4, wao, wo, g2, wr, wr_hi, wr_lo, br, expand]
    scratch = [pltpu.VMEM((N_DIL, CHUNKS_PER_GROUP, tm, LANES), F32), pltpu.VMEM((N_DIL, 1, tm, LANES), F32)]
    if routed:
        tri = jnp.asarray(np.tril(np.ones((tm, tm), np.float32), -1), dtype=BF16)
        args.append(tri)
        in_specs = in_specs + [pl.BlockSpec((tm, tm), const2)]
        out_specs = [pl.BlockSpec((None, tm, D_MODEL), tile3), pl.BlockSpec((None, tm, D_MODEL // 2), tile3),
                     pl.BlockSpec((None, tm, LANES), tile3),
                     pl.BlockSpec((None, None, SUBLANES, tm), lambda b, i: (b, i, 0, 0)),
                     pl.BlockSpec((1, LANES), const2)]
        out_shape = [jax.ShapeDtypeStruct((nb, rows, D_MODEL), F32),
                     jax.ShapeDtypeStruct((nb, rows, D_MODEL // 2), jnp.uint32),
                     jax.ShapeDtypeStruct((nb, rows, LANES), F32),
                     jax.ShapeDtypeStruct((nb, tiles, SUBLANES, tm), jnp.int32),
                     jax.ShapeDtypeStruct((1, LANES), F32)]
        scratch.append(pltpu.VMEM((1, LANES), F32))
    else:
        out_specs = [pl.BlockSpec((None, tm, D_MODEL), tile3), pl.BlockSpec((None, tm, D_MODEL), tile3),
                     pl.BlockSpec((None, tm, LANES), tile3)]
        out_shape = [jax.ShapeDtypeStruct((nb, rows, D_MODEL), F32), jax.ShapeDtypeStruct((nb, rows, D_MODEL), F32),
                     jax.ShapeDtypeStruct((nb, rows, LANES), F32)]
    return pl.pallas_call(
        functools.partial(_post_kernel, tm=tm, dils=dils, full_precision=full_precision, routed=routed),
        grid=(nb, tiles),
        in_specs=in_specs,
        out_specs=out_specs,
        out_shape=out_shape,
        scratch_shapes=scratch,
        compiler_params=_cparams(("arbitrary", "arbitrary")),
        name="post_sample" if full_precision else "post_prompt",
    )(*args)


DMA_LOOP_UNROLL = 8


ROUTE_CLASS_ROW, ROUTE_RANK_ROW = META_CLASS, META_RANK


def _sorted_slot(starts_ref, route_ref, r):
    return starts_ref[route_ref[ROUTE_CLASS_ROW, r]] + route_ref[ROUTE_RANK_ROW, r]


def _dispatch_kernel(starts_ref, route_ref, src_ref, dst_hbm, buf, sems, *, rows_per_step, n_steps):
    i = pl.program_id(0)
    slot = i % 2

    def wait_slot(s):
        def body(r, carry):
            pltpu.make_async_copy(buf.at[s, pl.ds(0, 1)], dst_hbm.at[pl.ds(0, 1)],
                                  sems.at[s * rows_per_step + r]).wait()
            return carry
        lax.fori_loop(0, rows_per_step, body, 0, unroll=DMA_LOOP_UNROLL)

    @pl.when(i >= 1)
    def _():
        wait_slot(1 - slot)

    buf[slot] = src_ref[...]

    def issue(r, carry):
        pltpu.make_async_copy(buf.at[slot, pl.ds(r, 1)], dst_hbm.at[pl.ds(_sorted_slot(starts_ref, route_ref, r), 1)],
                              sems.at[slot * rows_per_step + r]).start()
        return carry

    lax.fori_loop(0, rows_per_step, issue, 0, unroll=DMA_LOOP_UNROLL)

    @pl.when(i == n_steps - 1)
    def _():
        wait_slot(slot)


def _route_spec(rows_per_step, route_tile, shift=0, n_steps=None):
    per_tile = route_tile // rows_per_step

    def imap(i, starts):
        s = i + shift if n_steps is None else jnp.minimum(i + shift, n_steps - 1)
        return (s // per_tile, 0, s % per_tile)

    return pl.BlockSpec((None, SUBLANES, rows_per_step), imap, memory_space=pltpu.SMEM)


def _dispatch_rows(src, route, starts, *, rows_per_step):
    n, width = src.shape
    n_steps = n // rows_per_step
    grid_spec = pltpu.PrefetchScalarGridSpec(
        num_scalar_prefetch=1,
        grid=(n_steps,),
        in_specs=[_route_spec(rows_per_step, route.shape[2]),
                  pl.BlockSpec((rows_per_step, width), lambda i, starts: (i, 0))],
        out_specs=pl.BlockSpec(memory_space=pl.ANY),
        scratch_shapes=[pltpu.VMEM((2, rows_per_step, width), src.dtype),
                        pltpu.SemaphoreType.DMA((2 * rows_per_step,))],
    )
    return pl.pallas_call(
        functools.partial(_dispatch_kernel, rows_per_step=rows_per_step, n_steps=n_steps),
        grid_spec=grid_spec,
        out_shape=jax.ShapeDtypeStruct((n, width), src.dtype),
        compiler_params=_cparams(("arbitrary",)),
        name="moe_dispatch",
    )(starts, route, src)


def _experts_kernel(tile_ref, ea_ref, eb_ref, lo_ref, hi_ref, first_ref, nv_ref,
                    x_ref, w1a_ref, w1b_ref, w3a_ref, w3b_ref, w2a_ref, w2b_ref, y_ref, *, tm):
    i = pl.program_id(0)

    @pl.when(i < nv_ref[0])
    def _():
        xa, xb = _unpack_bf16_pair(x_ref[...])
        x = jnp.concatenate([xa, xb], axis=1).astype(BF16)

        def expert(w1_ref, w3_ref, w2_ref):
            a = jnp.dot(x, w1_ref[...], preferred_element_type=F32)
            b = jnp.dot(x, w3_ref[...], preferred_element_type=F32)
            hid = (a * _sigmoid(a)) * b
            return jnp.dot(hid.astype(BF16), w2_ref[...], preferred_element_type=F32)

        word = _pack_bf16_pair(expert(w1a_ref, w3a_ref, w2a_ref), expert(w1b_ref, w3b_ref, w2b_ref))

        @pl.when(first_ref[i] == 1)
        def _():
            y_ref[...] = word

        @pl.when(first_ref[i] == 0)
        def _():
            row = lax.broadcasted_iota(jnp.int32, (tm, 1), 0)
            old = y_ref[...]
            y_ref[...] = jnp.where(row >= lo_ref[i], jnp.where(row < hi_ref[i], word, old), old)


def _experts_routed(hs, tables, w1_bf, w3_bf, w2_bf, *, tm):
    n = hs.shape[0]
    n_items = tables[0].shape[0]
    x_map = lambda i, tile, ea, eb, lo, hi, first, nv: (tile[i], 0)
    wa_map = lambda i, tile, ea, eb, lo, hi, first, nv: (ea[i], 0, 0)
    wb_map = lambda i, tile, ea, eb, lo, hi, first, nv: (eb[i], 0, 0)
    w13 = lambda m: pl.BlockSpec((None, D_MODEL, D_EXPERT), m)
    w2s = lambda m: pl.BlockSpec((None, D_EXPERT, D_MODEL), m)
    grid_spec = pltpu.PrefetchScalarGridSpec(
        num_scalar_prefetch=7,
        grid=(n_items,),
        in_specs=[pl.BlockSpec((tm, D_MODEL // 2), x_map),
                  w13(wa_map), w13(wb_map), w13(wa_map), w13(wb_map), w2s(wa_map), w2s(wb_map)],
        out_specs=pl.BlockSpec((tm, D_MODEL), x_map),
    )
    return pl.pallas_call(
        functools.partial(_experts_kernel, tm=tm),
        grid_spec=grid_spec,
        out_shape=jax.ShapeDtypeStruct((n, D_MODEL), jnp.uint32),
        compiler_params=_cparams(("arbitrary",)),
        name="moe_experts",
    )(*tables, hs, w1_bf, w1_bf, w3_bf, w3_bf, w2_bf, w2_bf)


def _combine_kernel(starts_ref, route_ref, route_next_ref, yp_hbm, x1_ref, meta_ref, mod_ref, gf_ref, y_ref, ybuf, sems,
                    *, rows_per_step, n_steps):
    i = pl.program_id(0)
    slot = i % 2

    def issue(r_ref, s):
        def body(r, carry):
            pltpu.make_async_copy(yp_hbm.at[pl.ds(_sorted_slot(starts_ref, r_ref, r), 1)], ybuf.at[s, pl.ds(r, 1)],
                                  sems.at[s * rows_per_step + r]).start()
            return carry
        lax.fori_loop(0, rows_per_step, body, 0, unroll=DMA_LOOP_UNROLL)

    @pl.when(i == 0)
    def _():
        issue(route_ref, 0)

    @pl.when(i + 1 < n_steps)
    def _():
        issue(route_next_ref, 1 - slot)

    def wait_body(r, carry):
        pltpu.make_async_copy(yp_hbm.at[pl.ds(0, 1)], ybuf.at[slot, pl.ds(r, 1)],
                              sems.at[slot * rows_per_step + r]).wait()
        return carry

    lax.fori_loop(0, rows_per_step, wait_body, 0, unroll=DMA_LOOP_UNROLL)
    ya, yb = _unpack_bf16_pair(ybuf[slot])
    meta = meta_ref[...]
    lane = lax.broadcasted_iota(jnp.int32, (1, LANES), 1)
    w_a = jnp.sum(jnp.where(lane == META_WA, meta, 0.0), axis=1, keepdims=True)
    w_b = jnp.sum(jnp.where(lane == META_WB, meta, 0.0), axis=1, keepdims=True)
    x2 = x1_ref[...] + mod_ref[5] * (w_a * ya + w_b * yb)
    var = jnp.mean(x2 * x2, axis=-1, keepdims=True)
    y_ref[...] = (x2 * lax.rsqrt(var + RMS_EPS)) * gf_ref[...]


def _combine_final(yp, route, starts, x1, meta, mod4, gf, *, rows_per_step, rows_per_mod):
    n = x1.shape[0]
    n_steps = n // rows_per_step
    steps_per_mod = rows_per_mod // rows_per_step
    row = lambda i, starts: (i, 0)
    grid_spec = pltpu.PrefetchScalarGridSpec(
        num_scalar_prefetch=1,
        grid=(n_steps,),
        in_specs=[_route_spec(rows_per_step, route.shape[2]),
                  _route_spec(rows_per_step, route.shape[2], shift=1, n_steps=n_steps),
                  pl.BlockSpec(memory_space=pl.ANY),
                  pl.BlockSpec((rows_per_step, D_MODEL), row),
                  pl.BlockSpec((rows_per_step, LANES), row),
                  pl.BlockSpec((None, 6, 1, D_MODEL), lambda i, starts: (i // steps_per_mod, 0, 0, 0)),
                  pl.BlockSpec((1, D_MODEL), lambda i, starts: (0, 0))],
        out_specs=pl.BlockSpec((rows_per_step, D_MODEL), row),
        scratch_shapes=[pltpu.VMEM((2, rows_per_step, D_MODEL), jnp.uint32),
                        pltpu.SemaphoreType.DMA((2 * rows_per_step,))],
    )
    return pl.pallas_call(
        functools.partial(_combine_kernel, rows_per_step=rows_per_step, n_steps=n_steps),
        grid_spec=grid_spec,
        out_shape=jax.ShapeDtypeStruct((n, D_MODEL), F32),
        compiler_params=_cparams(("arbitrary",)),
        name="moe_combine",
    )(starts, route, route, yp, x1, meta, mod4, gf)


def _routing_tables(counts, *, n, tm):
    counts = counts.astype(jnp.int32)
    ends = jnp.cumsum(counts)
    starts = ends - counts
    n_items = n // tm + N_CLASSES
    first_tile = starts // tm
    last_tile = jnp.maximum(ends - 1, starts) // tm
    visits = jnp.where(counts > 0, last_tile - first_tile + 1, 0)
    item_end = jnp.cumsum(visits)
    item_start = item_end - visits
    n_valid = item_end[-1]
    item = jnp.arange(n_items, dtype=jnp.int32)
    idx = jnp.minimum(item, n_valid - 1)
    c = jnp.sum((idx[:, None] >= item_end[None, :]).astype(jnp.int32), axis=1)
    class_ids = jnp.arange(N_CLASSES, dtype=jnp.int32)

    def pick(table):
        return jnp.sum(jnp.where(c[:, None] == class_ids[None, :], table[None, :], 0), axis=1)

    tile = pick(first_tile) + (idx - pick(item_start))
    live = item < n_valid
    lo = jnp.where(live, jnp.clip(pick(starts) - tile * tm, 0, tm), 0)
    hi = jnp.where(live, jnp.clip(pick(ends) - tile * tm, 0, tm), 0)
    prev_tile = jnp.concatenate([jnp.full((1,), -1, jnp.int32), tile[:-1]])
    first = (tile != prev_tile).astype(jnp.int32)
    group_of_class = np.arange(N_CLASSES) // len(_PAIRS)
    pair_of_class = np.arange(N_CLASSES) % len(_PAIRS)
    pairs = np.asarray(_PAIRS)
    ea = pick(jnp.asarray(group_of_class * EXPERTS_PER_GROUP + pairs[pair_of_class, 0], dtype=jnp.int32))
    eb = pick(jnp.asarray(group_of_class * EXPERTS_PER_GROUP + pairs[pair_of_class, 1], dtype=jnp.int32))
    as_i32 = lambda a: a.astype(jnp.int32)
    return as_i32(starts), tuple(as_i32(a) for a in (tile, ea, eb, lo, hi, first, n_valid.reshape(1)))


def _moe_kernel(h_ref, comb_ref, x1_ref, mod_ref, gf_ref, w1_ref, w3_ref, w2_ref, y_ref, acc_ref):
    e = pl.program_id(2)

    @pl.when(e == 0)
    def _():
        acc_ref[...] = jnp.zeros_like(acc_ref)

    h = h_ref[...].astype(BF16)
    a = jnp.dot(h, w1_ref[...], preferred_element_type=F32)
    b = jnp.dot(h, w3_ref[...], preferred_element_type=F32)
    lane = lax.broadcasted_iota(jnp.int32, (1, LANES), 1)
    cw = jnp.sum(jnp.where(lane == e + N_EXPERT_GROUPS, comb_ref[...], 0.0), axis=1, keepdims=True)
    hid = (a * _sigmoid(a)) * b * cw
    acc_ref[...] += jnp.dot(hid.astype(BF16), w2_ref[...], preferred_element_type=F32)

    @pl.when(e == N_EXPERTS - 1)
    def _():
        x2 = x1_ref[...] + mod_ref[5] * acc_ref[...]
        var = jnp.mean(x2 * x2, axis=-1, keepdims=True)
        y_ref[...] = (x2 * lax.rsqrt(var + RMS_EPS)) * gf_ref[...]


def _moe_dense(h2, comb, x1, mod4, gf, w1_bf, w3_bf, w2_bf, *, tm):
    nb, rows, _ = x1.shape
    tiles = rows // tm
    tile3 = lambda b, i, e: (b, i, 0)
    if mod4.shape[2] == 1:
        mod_spec = pl.BlockSpec((None, 6, 1, D_MODEL), lambda b, i, e: (b, 0, 0, 0))
    else:
        mod_spec = pl.BlockSpec((None, 6, tm, D_MODEL), lambda b, i, e: (b, 0, i, 0))
    return pl.pallas_call(
        _moe_kernel,
        grid=(nb, tiles, N_EXPERTS),
        in_specs=[
            pl.BlockSpec((None, tm, D_MODEL), tile3),
            pl.BlockSpec((None, tm, LANES), tile3),
            pl.BlockSpec((None, tm, D_MODEL), tile3),
            mod_spec,
            pl.BlockSpec((1, D_MODEL), lambda b, i, e: (0, 0)),
            pl.BlockSpec((None, D_MODEL, D_EXPERT), lambda b, i, e: (e, 0, 0)),
            pl.BlockSpec((None, D_MODEL, D_EXPERT), lambda b, i, e: (e, 0, 0)),
            pl.BlockSpec((None, D_EXPERT, D_MODEL), lambda b, i, e: (e, 0, 0)),
        ],
        out_specs=pl.BlockSpec((None, tm, D_MODEL), tile3),
        out_shape=jax.ShapeDtypeStruct((nb, rows, D_MODEL), F32),
        scratch_shapes=[pltpu.VMEM((tm, D_MODEL), F32)],
        compiler_params=_cparams(("arbitrary", "arbitrary", "arbitrary")),
        name=f"moe_dense_{nb * rows}",
    )(h2, comb, x1, mod4, gf, w1_bf, w3_bf, w2_bf)


def _s_inproj_kernel(x_ref, mod_ref, g1_ref, w_ref, z_ref):
    x = x_ref[...]
    var = jnp.mean(x * x, axis=-1, keepdims=True)
    h = (x * lax.rsqrt(var + RMS_EPS)) * (g1_ref[...] * (1.0 + mod_ref[1])) + mod_ref[0]
    z_ref[...] = jnp.dot(h, w_ref[...], precision=HIGHEST, preferred_element_type=F32)


def _s_inproj(x, mod_tok, g1, w_in, *, col_block):
    n = x.shape[0]
    return pl.pallas_call(
        _s_inproj_kernel,
        grid=(IN_COLS // col_block,),
        in_specs=[pl.BlockSpec((n, D_MODEL), lambda j: (0, 0)),
                  pl.BlockSpec((6, n, D_MODEL), lambda j: (0, 0, 0)),
                  pl.BlockSpec((1, D_MODEL), lambda j: (0, 0)),
                  pl.BlockSpec((D_MODEL, col_block), lambda j: (0, j))],
        out_specs=pl.BlockSpec((n, col_block), lambda j: (0, j)),
        out_shape=jax.ShapeDtypeStruct((n, IN_COLS), F32),
        compiler_params=_cparams(("arbitrary",)),
        name="inproj_sample",
    )(x, mod_tok, g1, w_in)


def _s_mid_kernel(z_ref, p0_ref, p1_ref, cw_ref, wco_ref, rq_ref, rk_ref,
                  cu_ref, pc_ref, sga_ref, q_ref, k_ref, v_ref, *, t_len):
    n = z_ref.shape[0]
    cu = z_ref[:, OFF_GC:OFF_GC + D_MODEL] * z_ref[:, OFF_U:OFF_U + D_MODEL]
    cu_ref[...] = cu
    t = lax.broadcasted_iota(jnp.int32, (n, D_MODEL), 0) & (t_len - 1)
    prev1 = jnp.where(t >= 1, pltpu.roll(cu, 1, 0), p1_ref[...])
    prev2 = jnp.where(t >= 2, pltpu.roll(cu, 2, 0), jnp.where(t == 0, p0_ref[...], p1_ref[...]))
    cw = cw_ref[...]
    conv = cw[0:1] * prev2 + cw[1:2] * prev1 + cw[2:3] * cu
    yc = jnp.dot(z_ref[:, OFF_GB:OFF_GB + D_MODEL] * conv, wco_ref[...], precision=HIGHEST,
                 preferred_element_type=F32)
    pc_ref[...] = _sigmoid(z_ref[:, OFF_GCONV:OFF_GCONV + D_MODEL]) * yc
    sga_ref[...] = _sigmoid(z_ref[:, OFF_GATTN:OFF_GATTN + D_MODEL])
    aq, bmq, bpq = rq_ref[0], rq_ref[1], rq_ref[2]
    ak, bmk, bpk = rk_ref[0], rk_ref[1], rk_ref[2]
    for c in range(ATTN_WIDTH // LANES):
        sl = slice(c * LANES, (c + 1) * LANES)
        q_ref[:, sl] = _rope_chunk(z_ref[:, OFF_Q + c * LANES:OFF_Q + (c + 1) * LANES], aq, bmq, bpq)
        k_ref[:, sl] = _rope_chunk(z_ref[:, OFF_K + c * LANES:OFF_K + (c + 1) * LANES], ak, bmk, bpk)
    v_ref[...] = z_ref[:, OFF_V:OFF_V + ATTN_WIDTH]


def _s_mid(z, p0e, p1e, conv_w, w_conv_out, rope_q, rope_k, *, t_len):
    n = z.shape[0]
    assert t_len & (t_len - 1) == 0
    full = lambda shape: pl.BlockSpec(shape, lambda i: (0,) * len(shape))
    out_shape = [jax.ShapeDtypeStruct((n, D_MODEL), F32)] * 3 + [jax.ShapeDtypeStruct((n, ATTN_WIDTH), F32)] * 3
    return pl.pallas_call(
        functools.partial(_s_mid_kernel, t_len=t_len),
        grid=(1,),
        in_specs=[full(z.shape), full(p0e.shape), full(p1e.shape), full(conv_w.shape), full(w_conv_out.shape),
                  full(rope_q.shape), full(rope_k.shape)],
        out_specs=[full((n, D_MODEL))] * 3 + [full((n, ATTN_WIDTH))] * 3,
        out_shape=out_shape,
        compiler_params=_cparams(("arbitrary",)),
        name="mid_sample",
    )(z, p0e, p1e, conv_w, w_conv_out, rope_q, rope_k)


def _head_sum(x):
    return jnp.sum(x.reshape(HEADS_PER_GROUP, HEAD_DIM, x.shape[-1]), axis=1)


def _head_expand(x):
    n = x.shape[-1]
    return jnp.broadcast_to(x[:, None, :], (HEADS_PER_GROUP, HEAD_DIM, n)).reshape(GROUP_WIDTH, n)


def _s_attn_disjoint(q_ref, kn, vn, ck_ref, cv_ref, o_ref, lse_ref, s_scr, lane, new_idx, *, n_tiles, dil, t_len):
    cls = lane & (dil - 1)
    q_all = q_ref[...]
    qsel = jnp.zeros((GROUP_WIDTH, LANES), F32)
    for t in range(t_len):
        qsel = jnp.where(cls == t, jnp.broadcast_to(q_all[:, t:t + 1], (GROUP_WIDTH, LANES)), qsel)
    smax = None
    for j in range(n_tiles):
        s = jnp.where(cls < t_len, _head_sum(ck_ref[:, j * LANES:(j + 1) * LANES] * qsel), NEG_BIG)
        s_scr[:, j * LANES:(j + 1) * LANES] = s
        smax = s if smax is None else jnp.maximum(smax, s)
    s_new = _head_sum(kn * q_all)
    m_cols = []
    m_lane = jnp.zeros((HEADS_PER_GROUP, LANES), F32)
    m_new = jnp.zeros((HEADS_PER_GROUP, t_len), F32)
    for t in range(t_len):
        mt = jnp.maximum(jnp.max(jnp.where(cls == t, smax, NEG_BIG), axis=1, keepdims=True), s_new[:, t:t + 1])
        m_cols.append(mt)
        m_lane = jnp.where(cls == t, mt, m_lane)
        m_new = jnp.where(new_idx == t, mt, m_new)
    esum = jnp.zeros((HEADS_PER_GROUP, LANES), F32)
    for j in range(n_tiles):
        e = jnp.exp(s_scr[:, j * LANES:(j + 1) * LANES] - m_lane)
        s_scr[:, j * LANES:(j + 1) * LANES] = e
        esum = esum + e
    e_new = jnp.exp(s_new - m_new)
    l_cols = []
    inv_lane = jnp.zeros((HEADS_PER_GROUP, LANES), F32)
    inv_new = jnp.zeros((HEADS_PER_GROUP, t_len), F32)
    for t in range(t_len):
        lt = jnp.sum(jnp.where(cls == t, esum, 0.0), axis=1, keepdims=True) + e_new[:, t:t + 1]
        l_cols.append(lt)
        inv_lane = jnp.where(cls == t, 1.0 / lt, inv_lane)
        inv_new = jnp.where(new_idx == t, 1.0 / lt, inv_new)
    acc = None
    for j in range(n_tiles):
        term = cv_ref[:, j * LANES:(j + 1) * LANES] * _head_expand(s_scr[:, j * LANES:(j + 1) * LANES] * inv_lane)
        acc = term if acc is None else acc + term
    o_new = vn * _head_expand(e_new * inv_new)
    for t in range(t_len):
        o_ref[:, t:t + 1] = jnp.sum(jnp.where(cls == t, acc, 0.0), axis=1, keepdims=True) + o_new[:, t:t + 1]
        lse_ref[:, t:t + 1] = m_cols[t] + jnp.log(l_cols[t])


def _s_attn_kernel(q_ref, kn_ref, vn_ref, kt_ref, vt_ref, ck_ref, cv_ref, o_ref, lse_ref, ko_ref, vo_ref, s_scr,
                   *, win, dil, t_len):
    n_tiles = win // LANES
    lane = lax.broadcasted_iota(jnp.int32, (1, LANES), 1)
    new_idx = lax.broadcasted_iota(jnp.int32, (1, t_len), 1)
    kn = kn_ref[...]
    vn = vn_ref[...]
    if dil >= t_len:
        _s_attn_disjoint(q_ref, kn, vn, ck_ref, cv_ref, o_ref, lse_ref, s_scr, lane, new_idx,
                         n_tiles=n_tiles, dil=dil, t_len=t_len)
    for t in range(t_len if dil < t_len else 0):
        qb = jnp.broadcast_to(q_ref[:, t:t + 1], (GROUP_WIDTH, LANES))
        m = None
        for j in range(n_tiles):
            pos = lane + j * LANES
            s = _head_sum(ck_ref[:, j * LANES:(j + 1) * LANES] * qb)
            s = jnp.where(pos >= t, jnp.where(((pos - t) & (dil - 1)) == 0, s, NEG_BIG), NEG_BIG)
            s_scr[:, j * LANES:(j + 1) * LANES] = s
            mj = jnp.max(s, axis=1, keepdims=True)
            m = mj if m is None else jnp.maximum(m, mj)
        s_new = _head_sum(kn * qb[:, 0:t_len])
        s_new = jnp.where(new_idx <= t, jnp.where(((t - new_idx) & (dil - 1)) == 0, s_new, NEG_BIG), NEG_BIG)
        m = jnp.maximum(m, jnp.max(s_new, axis=1, keepdims=True))
        e_new = jnp.exp(s_new - m)
        l = jnp.sum(e_new, axis=1, keepdims=True)
        acc = None
        for j in range(n_tiles):
            e = jnp.exp(s_scr[:, j * LANES:(j + 1) * LANES] - m)
            l = l + jnp.sum(e, axis=1, keepdims=True)
            term = cv_ref[:, j * LANES:(j + 1) * LANES] * _head_expand(e)
            acc = term if acc is None else acc + term
        o = jnp.sum(acc, axis=1, keepdims=True) + jnp.sum(vn * _head_expand(e_new), axis=1, keepdims=True)
        o_ref[:, t:t + 1] = o * _head_expand(1.0 / l)
        lse_ref[:, t:t + 1] = m + jnp.log(l)

    for c_ref, tail_ref, out_ref in ((ck_ref, kt_ref, ko_ref), (cv_ref, vt_ref, vo_ref)):
        rolled = pltpu.roll(c_ref[...], win - t_len, 1)
        if win > LANES:
            out_ref[:, 0:win - LANES] = rolled[:, 0:win - LANES]
        out_ref[:, win - LANES:win] = jnp.where(lane >= LANES - t_len, tail_ref[...], rolled[:, win - LANES:win])


def _s_attn_group(q_t, kn_t, vn_t, k_tail, v_tail, cache_k, cache_v, g):
    bsz, _, t_len = q_t.shape
    win_full, dil = DIL_GROUPS[g]
    win = cache_k.shape[2]
    assert win == win_full and win == (KEYS_PER_QUERY - 1) * dil and win % LANES == 0
    assert dil & (dil - 1) == 0 and LANES % dil == 0
    grp = lambda b: (b, g, 0)
    per_b = lambda b: (b, 0, 0)
    return pl.pallas_call(
        functools.partial(_s_attn_kernel, win=win, dil=dil, t_len=t_len),
        grid=(bsz,),
        in_specs=[pl.BlockSpec((None, GROUP_WIDTH, t_len), grp)] * 3
        + [pl.BlockSpec((None, GROUP_WIDTH, LANES), grp)] * 2
        + [pl.BlockSpec((None, GROUP_WIDTH, win), per_b)] * 2,
        out_specs=[pl.BlockSpec((None, GROUP_WIDTH, t_len), per_b),
                   pl.BlockSpec((None, HEADS_PER_GROUP, t_len), per_b),
                   pl.BlockSpec((None, GROUP_WIDTH, win), per_b),
                   pl.BlockSpec((None, GROUP_WIDTH, win), per_b)],
        out_shape=[jax.ShapeDtypeStruct((bsz, GROUP_WIDTH, t_len), F32),
                   jax.ShapeDtypeStruct((bsz, HEADS_PER_GROUP, t_len), F32),
                   jax.ShapeDtypeStruct((bsz, GROUP_WIDTH, win), F32),
                   jax.ShapeDtypeStruct((bsz, GROUP_WIDTH, win), F32)],
        scratch_shapes=[pltpu.VMEM((HEADS_PER_GROUP, win), F32)],
        compiler_params=_cparams(("arbitrary",)),
        name=f"attn_sample_g{g}",
    )(q_t, kn_t, vn_t, k_tail, v_tail, cache_k, cache_v)


TM_INPROJ = 512
TM_POST = 512
TM_EXPERT = 256
ROWS_PER_DMA_STEP = 128
ATTN_MAX_CHUNK = 1024
S_COL_BLOCK = 512


def _to_state(a_t):
    b, _, length = a_t.shape
    return jnp.transpose(a_t.reshape(b, HEADS_PER_GROUP, HEAD_DIM, length), (0, 3, 1, 2))[None]


def _from_state(a):
    b, length = a.shape[0], a.shape[1]
    return jnp.transpose(a, (0, 2, 3, 1)).reshape(b, GROUP_WIDTH, length)


def kernel(x_prompt, x_sample, cache_k1, cache_v1, cache_k2, cache_v2, cache_k3, cache_v3, state_conv,
           c_prompt, c_sample, norm1_g, norm2_g, normf_g, w_ada, b_ada, w_in, conv_w, w_conv_out,
           w_attn_out, w_o, w_rg, b_rg, w_re, b_re, w1, w3, w2):
    depth = w_in.shape[0]
    assert depth == 1
    bsz, seq, _ = x_prompt.shape
    dbsz, t_len, _ = x_sample.shape
    n_s = dbsz * t_len
    l = 0

    w_in_bf = w_in[l].astype(BF16)
    wco_bf = w_conv_out[l].astype(BF16)
    wao_bf = w_attn_out[l].astype(BF16)
    wo_bf = w_o[l].astype(BF16)
    w1_bf = w1[l].reshape(N_EXPERTS, D_MODEL, D_EXPERT).astype(BF16)
    w3_bf = w3[l].reshape(N_EXPERTS, D_MODEL, D_EXPERT).astype(BF16)
    w2_bf = w2[l].reshape(N_EXPERTS, D_EXPERT, D_MODEL).astype(BF16)
    g1 = norm1_g[l].reshape(1, D_MODEL)
    g2 = norm2_g[l].reshape(1, D_MODEL)
    gf = normf_g.reshape(1, D_MODEL)
    n_route = N_EXPERT_GROUPS + N_EXPERTS
    wr = jnp.pad(jnp.concatenate([w_rg[l], w_re[l]], axis=1), ((0, 0), (0, LANES - n_route)))
    br = jnp.pad(jnp.concatenate([b_rg[l], b_re[l]]), (0, LANES - n_route)).reshape(1, LANES)
    head_of_lane = np.arange(GROUP_WIDTH) // HEAD_DIM
    expand_bf = jnp.asarray((np.arange(LANES)[:, None] == head_of_lane[None, :]).astype(np.float32), dtype=BF16)

    mod = _adaln(jnp.concatenate([c_prompt, c_sample], axis=0), w_ada[l], b_ada[l])
    mod_p4 = mod[:bsz].reshape(bsz, 6, 1, D_MODEL)
    mod_tok = jnp.repeat(mod[bsz:].reshape(dbsz, 1, 6, D_MODEL), t_len, axis=1)
    mod_tok = jnp.transpose(mod_tok.reshape(n_s, 6, D_MODEL), (1, 0, 2))
    mod_s4 = mod_tok[None]

    pos_p = jnp.arange(seq, dtype=jnp.int32)
    rope_q_p = _rope_tables(pos_p, HEAD_DIM ** -0.5)
    rope_k_p = _rope_tables(pos_p, 1.0)
    pc_p, sga_p, conv_p = _convproj_prompt(x_prompt, mod_p4, g1, w_in_bf[:, :OFF_Q], w_in_bf[:, OFF_GCONV:],
                                           conv_w[l], wco_bf, tm=TM_INPROJ)
    outs = _qkvproj_prompt(x_prompt, mod_p4, g1, w_in_bf[:, OFF_Q:OFF_GCONV], rope_q_p, rope_k_p, tm=TM_INPROJ)
    qkv_p = outs[0:3 * N_DIL]
    states_p = outs[3 * N_DIL:]
    o_p, lse_p = [], []
    for g in range(N_DIL):
        o_g, lse_g = _attn_prompt_group(qkv_p[3 * g], qkv_p[3 * g + 1], qkv_p[3 * g + 2], g, max_chunk=ATTN_MAX_CHUNK)
        o_p.append(o_g)
        lse_p.append(lse_g)
    x1_p, h2pk_p, meta_p, route_p, cnt_p = _post(o_p, lse_p, pc_p, sga_p, x_prompt, mod_p4, wao_bf, wo_bf, g2, wr, br,
                                                 expand_bf, tm=TM_POST, full_precision=False, routed=True)
    n_p = bsz * seq
    route = route_p.reshape(n_p // TM_POST, SUBLANES, TM_POST)
    starts, tables = _routing_tables(cnt_p[0, :N_CLASSES], n=n_p, tm=TM_EXPERT)
    hs = _dispatch_rows(h2pk_p.reshape(n_p, D_MODEL // 2), route, starts, rows_per_step=ROWS_PER_DMA_STEP)
    yp = _experts_routed(hs, tables, w1_bf, w3_bf, w2_bf, tm=TM_EXPERT)
    y_p = _combine_final(yp, route, starts, x1_p.reshape(n_p, D_MODEL), meta_p.reshape(n_p, LANES), mod_p4, gf,
                         rows_per_step=ROWS_PER_DMA_STEP, rows_per_mod=seq).reshape(bsz, seq, D_MODEL)

    pos_s = PAST_LEN + jnp.arange(t_len, dtype=jnp.int32)
    rope_q_s = jnp.tile(_rope_tables(pos_s, HEAD_DIM ** -0.5), (1, dbsz, 1))
    rope_k_s = jnp.tile(_rope_tables(pos_s, 1.0), (1, dbsz, 1))
    xs = x_sample.reshape(n_s, D_MODEL)
    z_s = _s_inproj(xs, mod_tok, g1, w_in[l], col_block=S_COL_BLOCK)
    past = state_conv[l]
    p0e = jnp.repeat(past[:, 0], t_len, axis=0)
    p1e = jnp.repeat(past[:, 1], t_len, axis=0)
    cu_s, pc_s, sga_s, q_s, k_s, v_s = _s_mid(z_s, p0e, p1e, conv_w[l], w_conv_out[l], rope_q_s, rope_k_s,
                                              t_len=t_len)
    to_cols = lambda a: jnp.transpose(a.reshape(dbsz, t_len, ATTN_WIDTH), (0, 2, 1))
    q_t, kn_t, vn_t = to_cols(q_s), to_cols(k_s), to_cols(v_s)
    k_tail = jnp.pad(kn_t, ((0, 0), (0, 0), (LANES - t_len, 0)))
    v_tail = jnp.pad(vn_t, ((0, 0), (0, 0), (LANES - t_len, 0)))
    caches = ((cache_k1, cache_v1), (cache_k2, cache_v2), (cache_k3, cache_v3))
    o_s, lse_s, kv_s = [], [], []
    for g, (ck, cv) in enumerate(caches):
        o_g, lse_g, ko, vo = _s_attn_group(q_t, kn_t, vn_t, k_tail, v_tail, _from_state(ck[l]), _from_state(cv[l]), g)
        o_s.append(jnp.transpose(o_g, (0, 2, 1)).reshape(1, 1, n_s, GROUP_WIDTH))
        lse_rows = jnp.transpose(lse_g, (0, 2, 1)).reshape(n_s, HEADS_PER_GROUP)
        lse_s.append(jnp.pad(lse_rows, ((0, 0), (0, LANES - HEADS_PER_GROUP))).reshape(1, 1, n_s, LANES))
        kv_s += [_to_state(ko), _to_state(vo)]
    x1_s, h2_s, comb_s = _post(o_s, lse_s, pc_s[None], sga_s[None], xs[None], mod_s4, w_attn_out[l], w_o[l], g2, wr, br,
                               expand_bf, tm=n_s, full_precision=True, routed=False)
    y_s = _moe_dense(h2_s, comb_s, x1_s, mod_s4, gf, w1_bf, w3_bf, w2_bf, tm=n_s)

    conv_s = cu_s.reshape(dbsz, t_len, D_MODEL)[:, t_len - (CONV_K - 1):]
    return (y_p, y_s.reshape(dbsz, t_len, D_MODEL),
            *[_to_state(a) for a in states_p], conv_p.reshape(1, bsz, CONV_K - 1, D_MODEL),
            *kv_s, conv_s.reshape(1, dbsz, CONV_K - 1, D_MODEL))
```

```python
import functools

import numpy as np
import jax
import jax.numpy as jnp
from jax import lax
from jax.experimental import pallas as pl
from jax.experimental.pallas import tpu as pltpu

F32 = jnp.float32
BF16 = jnp.bfloat16
HIGHEST = lax.Precision.HIGHEST

D_MODEL = 1024
HEAD_DIM = 64
HEADS_PER_GROUP = 8
GROUP_WIDTH = HEADS_PER_GROUP * HEAD_DIM
DIL_GROUPS = ((128, 1), (512, 4), (2048, 16))
N_DIL = len(DIL_GROUPS)
ATTN_WIDTH = N_DIL * GROUP_WIDTH
ROT_DIM = HEAD_DIM // 4
ROPE_THETA = 500000.0
PAST_LEN = 16384
CONV_K = 3
N_EXPERT_GROUPS = 4
EXPERTS_PER_GROUP = 4
N_EXPERTS = N_EXPERT_GROUPS * EXPERTS_PER_GROUP
D_EXPERT = 512
RMS_EPS = 1e-6
IN_COLS = 3 * D_MODEL + 3 * ATTN_WIDTH + 2 * D_MODEL
OFF_U, OFF_GC, OFF_GB = 0, D_MODEL, 2 * D_MODEL
OFF_Q = 3 * D_MODEL
OFF_K = OFF_Q + ATTN_WIDTH
OFF_V = OFF_K + ATTN_WIDTH
OFF_GCONV = OFF_V + ATTN_WIDTH
OFF_GATTN = OFF_GCONV + D_MODEL

LANES = 128
SUBLANES = 8
CHUNKS_PER_GROUP = GROUP_WIDTH // LANES
KEYS_PER_QUERY = 129
Q_BLOCK = 128
NEG_BIG = -1e30

VMEM_LIMIT = 56 * 1024 * 1024


def _sigmoid(x):
    return 1.0 / (1.0 + jnp.exp(-x))


def _cparams(sem):
    return pltpu.CompilerParams(dimension_semantics=sem, vmem_limit_bytes=VMEM_LIMIT)


def _adaln_kernel(c_ref, w_ref, b_ref, o_ref):
    c = c_ref[...]
    s = c * _sigmoid(c)
    o_ref[...] = jnp.dot(s, w_ref[...], precision=HIGHEST, preferred_element_type=F32) + b_ref[...]


def _adaln(c_all, w_ada, b_ada):
    rows = c_all.shape[0]
    n_col = w_ada.shape[1] // D_MODEL
    return pl.pallas_call(
        _adaln_kernel,
        grid=(n_col,),
        in_specs=[
            pl.BlockSpec((rows, D_MODEL), lambda j: (0, 0)),
            pl.BlockSpec((D_MODEL, D_MODEL), lambda j: (0, j)),
            pl.BlockSpec((1, D_MODEL), lambda j: (0, j)),
        ],
        out_specs=pl.BlockSpec((rows, D_MODEL), lambda j: (0, j)),
        out_shape=jax.ShapeDtypeStruct((rows, w_ada.shape[1]), F32),
        compiler_params=_cparams(("arbitrary",)),
        name="adaln",
    )(c_all, w_ada, b_ada.reshape(1, -1))


def _rope_tables(pos, scale):
    half = ROT_DIM // 2
    inv_freq = jnp.power(jnp.float32(ROPE_THETA), -jnp.arange(half, dtype=F32) / half)
    ang = pos.astype(F32)[:, None] * inv_freq[None, :]
    cos, sin = jnp.cos(ang), jnp.sin(ang)
    lane_in_head = np.arange(LANES) % HEAD_DIM
    freq = lane_in_head % half
    first = lane_in_head < half
    second = (lane_in_head >= half) & (lane_in_head < ROT_DIM)
    a = jnp.where(first | second, cos[:, freq], 1.0)
    bm = jnp.where(first, -sin[:, freq], 0.0)
    bp = jnp.where(second, sin[:, freq], 0.0)
    return jnp.stack([a, bm, bp]) * scale


def _rope_chunk(zc, a, bm, bp):
    return zc * a + pltpu.roll(zc, LANES - ROT_DIM // 2, 1) * bm + pltpu.roll(zc, ROT_DIM // 2, 1) * bp


def _modulated_norm_bf16(x_ref, mod_ref, g1_ref):
    x = x_ref[...]
    var = jnp.mean(x * x, axis=-1, keepdims=True)
    h = (x * lax.rsqrt(var + RMS_EPS)) * (g1_ref[...] * (1.0 + mod_ref[1])) + mod_ref[0]
    return h.astype(BF16)


def _convproj_kernel(x_ref, mod_ref, g1_ref, wa_ref, wg_ref, cw_ref, wco_ref, pc_ref, sga_ref, cst_ref, s_ref,
                     *, tm, n_tiles):
    i = pl.program_id(1)
    hb = _modulated_norm_bf16(x_ref, mod_ref, g1_ref)

    def proj(w_ref, lo):
        return jnp.dot(hb, w_ref[:, lo:lo + D_MODEL], preferred_element_type=F32)

    cu = proj(wa_ref, OFF_GC) * proj(wa_ref, OFF_U)

    @pl.when(i == 0)
    def _():
        s_ref[0:SUBLANES, :] = jnp.zeros((SUBLANES, D_MODEL), F32)

    s_ref[SUBLANES:SUBLANES + tm, :] = cu
    cw = cw_ref[...]
    conv = (cw[0:1] * s_ref[SUBLANES - 2:SUBLANES - 2 + tm, :]
            + cw[1:2] * s_ref[SUBLANES - 1:SUBLANES - 1 + tm, :]
            + cw[2:3] * cu)
    yc = jnp.dot((proj(wa_ref, OFF_GB) * conv).astype(BF16), wco_ref[...], preferred_element_type=F32)
    pc_ref[...] = (_sigmoid(proj(wg_ref, 0)) * yc).astype(pc_ref.dtype)
    sga_ref[...] = _sigmoid(proj(wg_ref, D_MODEL)).astype(sga_ref.dtype)

    @pl.when(i == n_tiles - 1)
    def _():
        cst_ref[...] = s_ref[tm + SUBLANES - 2:tm + SUBLANES, :]

    s_ref[0:SUBLANES, :] = s_ref[tm:tm + SUBLANES, :]


def _convproj_prompt(x, mod4, g1, wa_bf, wg_bf, conv_w, wco_bf, *, tm):
    bsz, seq, _ = x.shape
    n_tiles = seq // tm
    const2 = lambda b, i: (0, 0)
    tile3 = lambda b, i: (b, i, 0)
    return pl.pallas_call(
        functools.partial(_convproj_kernel, tm=tm, n_tiles=n_tiles),
        grid=(bsz, n_tiles),
        in_specs=[
            pl.BlockSpec((None, tm, D_MODEL), tile3),
            pl.BlockSpec((None, 6, 1, D_MODEL), lambda b, i: (b, 0, 0, 0)),
            pl.BlockSpec((1, D_MODEL), const2),
            pl.BlockSpec(wa_bf.shape, const2, pipeline_mode=pl.Buffered(1)),
            pl.BlockSpec(wg_bf.shape, const2, pipeline_mode=pl.Buffered(1)),
            pl.BlockSpec((CONV_K, D_MODEL), const2),
            pl.BlockSpec((D_MODEL, D_MODEL), const2, pipeline_mode=pl.Buffered(1)),
        ],
        out_specs=[pl.BlockSpec((None, tm, D_MODEL), tile3), pl.BlockSpec((None, tm, D_MODEL), tile3),
                   pl.BlockSpec((None, CONV_K - 1, D_MODEL), lambda b, i: (b, 0, 0))],
        out_shape=[jax.ShapeDtypeStruct((bsz, seq, D_MODEL), BF16), jax.ShapeDtypeStruct((bsz, seq, D_MODEL), BF16),
                   jax.ShapeDtypeStruct((bsz, CONV_K - 1, D_MODEL), F32)],
        scratch_shapes=[pltpu.VMEM((tm + SUBLANES, D_MODEL), F32)],
        compiler_params=_cparams(("arbitrary", "arbitrary")),
        name="convproj_prompt",
    )(x, mod4, g1, wa_bf, wg_bf, conv_w, wco_bf)


def _qkvproj_kernel(x_ref, mod_ref, g1_ref, w_ref, rq_ref, rk_ref, *rest, tm, n_tiles, seq):
    qkv_refs = rest[0:3 * N_DIL]
    st_refs = rest[3 * N_DIL:3 * N_DIL + 2 * N_DIL]
    d_ref = rest[-1]
    i = pl.program_id(1)
    hb = _modulated_norm_bf16(x_ref, mod_ref, g1_ref)

    def proj(lo, width):
        return jnp.dot(hb, w_ref[:, lo:lo + width], preferred_element_type=F32)

    zq = proj(0, ATTN_WIDTH)
    zk = proj(ATTN_WIDTH, ATTN_WIDTH)
    zv = proj(2 * ATTN_WIDTH, ATTN_WIDTH)
    aq, bmq, bpq = rq_ref[0], rq_ref[1], rq_ref[2]
    ak, bmk, bpk = rk_ref[0], rk_ref[1], rk_ref[2]
    n_chunks = ATTN_WIDTH // LANES
    q_chunks, k_chunks, v_chunks = [], [], []
    for c in range(n_chunks):
        sl = slice(c * LANES, (c + 1) * LANES)
        q_chunks.append(_rope_chunk(zq[:, sl], aq, bmq, bpq))
        k_chunks.append(_rope_chunk(zk[:, sl], ak, bmk, bpk))
        v_chunks.append(zv[:, sl])

    for which, chunks in enumerate((q_chunks, k_chunks, v_chunks)):
        for g in range(N_DIL):
            out_ref = qkv_refs[3 * g + which]
            dil = DIL_GROUPS[g][1]
            for cc in range(CHUNKS_PER_GROUP):
                c = g * CHUNKS_PER_GROUP + cc
                sl = slice(cc * LANES, (cc + 1) * LANES)
                if dil == 1:
                    out_ref[0, :, sl] = chunks[c].astype(BF16)
                else:
                    d_ref[c] = chunks[c]
                    for r in range(dil):
                        out_ref[r, :, sl] = d_ref[c, pl.ds(r, tm // dil, stride=dil), :].astype(BF16)

    for g in range(N_DIL):
        kst, vst = st_refs[2 * g], st_refs[2 * g + 1]
        win = min(DIL_GROUPS[g][0], seq)
        if win >= tm:
            cond, r0 = i >= (seq - win) // tm, 0
        else:
            cond, r0 = i == n_tiles - 1, tm - win

        @pl.when(cond)
        def _(g=g, kst=kst, vst=vst, r0=r0):
            for cc in range(CHUNKS_PER_GROUP):
                c = g * CHUNKS_PER_GROUP + cc
                kst[cc * LANES:(cc + 1) * LANES, :] = k_chunks[c][r0:, :].T
                vst[cc * LANES:(cc + 1) * LANES, :] = v_chunks[c][r0:, :].T


def _qkvproj_prompt(x, mod4, g1, wqkv_bf, rope_q, rope_k, *, tm):
    bsz, seq, _ = x.shape
    n_tiles = seq // tm
    const2 = lambda b, i: (0, 0)
    tile3 = lambda b, i: (b, i, 0)
    in_specs = [
        pl.BlockSpec((None, tm, D_MODEL), tile3),
        pl.BlockSpec((None, 6, 1, D_MODEL), lambda b, i: (b, 0, 0, 0)),
        pl.BlockSpec((1, D_MODEL), const2),
        pl.BlockSpec(wqkv_bf.shape, const2, pipeline_mode=pl.Buffered(1)),
        pl.BlockSpec((3, tm, LANES), lambda b, i: (0, i, 0)),
        pl.BlockSpec((3, tm, LANES), lambda b, i: (0, i, 0)),
    ]
    out_shape, out_specs = [], []
    for _, dil in DIL_GROUPS:
        assert tm % (dil * 16) == 0
        for _ in range(3):
            out_shape.append(jax.ShapeDtypeStruct((bsz, dil, seq // dil, GROUP_WIDTH), BF16))
            out_specs.append(pl.BlockSpec((None, dil, tm // dil, GROUP_WIDTH), lambda b, i: (b, 0, i, 0)))
    for win, _ in DIL_GROUPS:
        win = min(win, seq)
        cols = min(win, tm)
        if win >= tm:
            imap = lambda b, i, ft=(seq - win) // tm: (b, 0, jnp.maximum(i - ft, 0))
        else:
            imap = lambda b, i: (b, 0, 0)
        for _ in range(2):
            out_shape.append(jax.ShapeDtypeStruct((bsz, GROUP_WIDTH, win), F32))
            out_specs.append(pl.BlockSpec((None, GROUP_WIDTH, cols), imap))
    return pl.pallas_call(
        functools.partial(_qkvproj_kernel, tm=tm, n_tiles=n_tiles, seq=seq),
        grid=(bsz, n_tiles),
        in_specs=in_specs,
        out_specs=out_specs,
        out_shape=out_shape,
        scratch_shapes=[pltpu.VMEM((ATTN_WIDTH // LANES, tm, LANES), F32)],
        compiler_params=_cparams(("arbitrary", "arbitrary")),
        name="qkvproj_prompt",
    )(x, mod4, g1, wqkv_bf, rope_q, rope_k)


def _attn_kernel(*refs, chunk, has_halo):
    if has_halo:
        q_ref, k_ref, v_ref, kh_ref, vh_ref, o_ref, lse_ref, kbuf, vbuf = refs
    else:
        q_ref, k_ref, v_ref, o_ref, lse_ref, kbuf, vbuf = refs
    c = pl.program_id(2)
    if has_halo:
        kbuf[0:Q_BLOCK, :] = kh_ref[...]
        vbuf[0:Q_BLOCK, :] = vh_ref[...]
    else:
        kbuf[0:Q_BLOCK, :] = jnp.zeros((Q_BLOCK, GROUP_WIDTH), BF16)
        vbuf[0:Q_BLOCK, :] = jnp.zeros((Q_BLOCK, GROUP_WIDTH), BF16)
    kbuf[Q_BLOCK:Q_BLOCK + chunk, :] = k_ref[...]
    vbuf[Q_BLOCK:Q_BLOCK + chunk, :] = v_ref[...]

    row = lax.broadcasted_iota(jnp.int32, (Q_BLOCK, 2 * Q_BLOCK), 0)
    col = lax.broadcasted_iota(jnp.int32, (Q_BLOCK, 2 * Q_BLOCK), 1)
    bias_main = jnp.where(col >= row, jnp.where(col <= row + Q_BLOCK, 0.0, NEG_BIG), NEG_BIG)
    bias_first = jnp.where(col >= Q_BLOCK, bias_main, NEG_BIG)
    lane = lax.broadcasted_iota(jnp.int32, (1, LANES), 1)
    lo_half = lane < HEAD_DIM

    def body(qb, carry):
        r0 = pl.multiple_of(qb * Q_BLOCK, Q_BLOCK)
        kt = kbuf[pl.ds(r0, 2 * Q_BLOCK), :]
        vt = vbuf[pl.ds(r0, 2 * Q_BLOCK), :]
        qt = q_ref[pl.ds(r0, Q_BLOCK), :]
        is_first = jnp.logical_and(qb == 0, c == 0)
        bias = jnp.where(is_first, bias_first, bias_main)
        lse_tile = jnp.zeros((Q_BLOCK, LANES), F32)
        for p in range(CHUNKS_PER_GROUP):
            sl = slice(p * LANES, (p + 1) * LANES)
            qp, kp, vp = qt[:, sl], kt[:, sl], vt[:, sl]
            outs = []
            for hh in range(2):
                head_lanes = lo_half if hh == 0 else jnp.logical_not(lo_half)
                qm = jnp.where(head_lanes, qp, jnp.zeros_like(qp))
                s = lax.dot_general(qm, kp, (((1,), (1,)), ((), ())), preferred_element_type=F32) + bias
                m = jnp.max(s, axis=1, keepdims=True)
                e = jnp.exp(s - m)
                l = jnp.sum(e, axis=1, keepdims=True)
                o = jnp.dot(e.astype(BF16), vp, preferred_element_type=F32)
                outs.append(o * (1.0 / l))
                lse_tile = jnp.where(lane == 2 * p + hh, m + jnp.log(l), lse_tile)
            o_ref[pl.ds(r0, Q_BLOCK), sl] = jnp.where(lo_half, outs[0], outs[1])
        lse_ref[pl.ds(r0, Q_BLOCK), :] = lse_tile
        return carry

    lax.fori_loop(0, chunk // Q_BLOCK, body, 0)


def _attn_prompt_group(q, k, v, g, *, max_chunk):
    bsz, dil, cls_len, _ = q.shape
    chunk = min(max_chunk, cls_len)
    n_chunks = cls_len // chunk
    has_halo = n_chunks > 1
    main = pl.BlockSpec((None, None, chunk, GROUP_WIDTH), lambda b, r, c: (b, r, c, 0))
    in_specs = [main, main, main]
    args = [q, k, v]
    if has_halo:
        per = chunk // Q_BLOCK
        halo = pl.BlockSpec((None, None, Q_BLOCK, GROUP_WIDTH), lambda b, r, c: (b, r, jnp.maximum(c * per - 1, 0), 0))
        in_specs += [halo, halo]
        args += [k, v]
    return pl.pallas_call(
        functools.partial(_attn_kernel, chunk=chunk, has_halo=has_halo),
        grid=(bsz, dil, n_chunks),
        in_specs=in_specs,
        out_specs=[pl.BlockSpec((None, None, chunk, GROUP_WIDTH), lambda b, r, c: (b, r, c, 0)),
                   pl.BlockSpec((None, None, chunk, LANES), lambda b, r, c: (b, r, c, 0))],
        out_shape=[jax.ShapeDtypeStruct((bsz, dil, cls_len, GROUP_WIDTH), F32),
                   jax.ShapeDtypeStruct((bsz, dil, cls_len, LANES), F32)],
        scratch_shapes=[pltpu.VMEM((chunk + Q_BLOCK, GROUP_WIDTH), BF16),
                        pltpu.VMEM((chunk + Q_BLOCK, GROUP_WIDTH), BF16)],
        compiler_params=_cparams(("arbitrary", "arbitrary", "arbitrary")),
        name=f"attn_prompt_g{g}",
    )(*args)


def _split_bf16(x, n):
    parts = []
    r = x
    for _ in range(n):
        p = r.astype(BF16)
        parts.append(p)
        r = r - p.astype(F32)
    return parts


def _store_token_rows(ref, value, n_tokens, first_chunk=0, rows_per_token=None):
    n_chunks = value.shape[1] // LANES
    rows_per_token = rows_per_token or n_chunks
    for c in range(n_chunks):
        ref[pl.ds(first_chunk + c, n_tokens, stride=rows_per_token), :] = value[:, c * LANES:(c + 1) * LANES]


def _load_token_rows(ref, n_tokens, n_chunks, first_chunk=0, rows_per_token=None, lead=None):
    rows_per_token = rows_per_token or n_chunks

    def chunk(c):
        rows = pl.ds(first_chunk + c, n_tokens, stride=rows_per_token)
        return ref[rows, :] if lead is None else ref[lead, rows, :]

    return jnp.concatenate([chunk(c) for c in range(n_chunks)], axis=1)


H_ROWS = D_MODEL // LANES
Y_ROWS = 2 * D_MODEL // LANES

_PAIRS = ((0, 1), (0, 2), (0, 3), (1, 2), (1, 3), (2, 3))
N_CLASSES = N_EXPERT_GROUPS * len(_PAIRS)
META_CLASS, META_RANK, META_WA, META_WB = 0, 1, 2, 3


def _post_kernel(*refs, tm, dils, full_precision, routed):
    (o0_ref, o1_ref, o2_ref, l0_ref, l1_ref, l2_ref, pc_ref, sga_ref, x_ref, mod_ref,
     wao_ref, wo_ref, g2_ref, wr_ref, wrh_ref, wrl_ref, br_ref, exp_ref) = refs[:18]
    if routed:
        tri_ref, x1_ref, h2_ref, meta_ref, route_ref, cnt_ref, o_scr, l_scr, run_scr = refs[18:]
    else:
        x1_ref, h2_ref, comb_ref, o_scr, l_scr = refs[18:]

    def mm(a, w_ref):
        if full_precision:
            return jnp.dot(a, w_ref[...], precision=HIGHEST, preferred_element_type=F32)
        return jnp.dot(a.astype(BF16), w_ref[...], preferred_element_type=F32)

    def natural_order(ref, scr, dil, n_chunks):
        if dil == 1:
            return [ref[0, :, c * LANES:(c + 1) * LANES] for c in range(n_chunks)]
        out = []
        for c in range(n_chunks):
            for r in range(dil):
                scr[c, pl.ds(r, tm // dil, stride=dil), :] = ref[r, :, c * LANES:(c + 1) * LANES]
            out.append(scr[c])
        return out

    lses = [natural_order(ref, l_scr.at[g], dils[g], 1)[0] for g, ref in enumerate((l0_ref, l1_ref, l2_ref))]
    mx = jnp.maximum(lses[0], jnp.maximum(lses[1], lses[2]))
    es = [jnp.exp(v - mx) for v in lses]
    inv = 1.0 / (es[0] + es[1] + es[2])
    expand = exp_ref[...]
    attn_o = None
    for g, o_ref in enumerate((o0_ref, o1_ref, o2_ref)):
        w = es[g] * inv
        we = None
        for part in _split_bf16(w, 3 if full_precision else 2):
            t = jnp.dot(part, expand, preferred_element_type=F32)
            we = t if we is None else we + t
        o_nat = jnp.concatenate(natural_order(o_ref, o_scr.at[g], dils[g], CHUNKS_PER_GROUP), axis=1)
        term = we * o_nat
        attn_o = term if attn_o is None else attn_o + term

    y_attn = mm(attn_o, wao_ref)
    mixed = mm(pc_ref[...] + sga_ref[...] * y_attn, wo_ref)
    x1 = x_ref[...] + mod_ref[2] * mixed
    x1_ref[...] = x1
    var = jnp.mean(x1 * x1, axis=-1, keepdims=True)
    h2 = (x1 * lax.rsqrt(var + RMS_EPS)) * (g2_ref[...] * (1.0 + mod_ref[4])) + mod_ref[3]
    if routed:
        _store_token_rows(h2_ref, h2, tm)
    else:
        h2_ref[...] = h2.astype(h2_ref.dtype)

    if full_precision:
        lg = jnp.dot(h2, wr_ref[...], precision=HIGHEST, preferred_element_type=F32) + br_ref[...]
    else:
        h_hi, h_lo = _split_bf16(h2, 2)
        lg = (jnp.dot(h_hi, wrh_ref[...], preferred_element_type=F32)
              + jnp.dot(h_lo, wrh_ref[...], preferred_element_type=F32)
              + jnp.dot(h_hi, wrl_ref[...], preferred_element_type=F32)) + br_ref[...]
    lane_i = lax.broadcasted_iota(jnp.int32, (1, LANES), 1)
    lane = lane_i.astype(F32)
    lane_group = ((lane_i - N_EXPERT_GROUPS) >> 2).astype(F32)
    big = jnp.float32(1e9)
    gl = jnp.where(lane_i < N_EXPERT_GROUPS, lg, NEG_BIG)
    gmax = jnp.max(gl, axis=1, keepdims=True)
    gidx = jnp.min(jnp.where(gl == gmax, lane, big), axis=1, keepdims=True)
    g_w = 1.0 / jnp.sum(jnp.exp(gl - gmax), axis=1, keepdims=True)
    el = jnp.where(lane_group == gidx, lg, NEG_BIG)
    v1 = jnp.max(el, axis=1, keepdims=True)
    i1 = jnp.min(jnp.where(el == v1, lane, big), axis=1, keepdims=True)
    el2 = jnp.where(lane == i1, NEG_BIG, el)
    v2 = jnp.max(el2, axis=1, keepdims=True)
    i2 = jnp.min(jnp.where(el2 == v2, lane, big), axis=1, keepdims=True)
    t = jnp.exp(v2 - v1)
    den = 1.0 / (1.0 + t)
    w_top1, w_top2 = g_w * den, g_w * (t * den)
    if not routed:
        comb_ref[...] = jnp.where(lane == i1, w_top1, jnp.where(lane == i2, w_top2, 0.0))
        return

    base = jnp.float32(N_EXPERT_GROUPS) + jnp.float32(EXPERTS_PER_GROUP) * gidx
    e1, e2 = i1 - base, i2 - base
    first_is_lower = e1 < e2
    ea = jnp.where(first_is_lower, e1, e2)
    eb = jnp.where(first_is_lower, e2, e1)
    pair = ea * (7.0 - ea) * 0.5 + (eb - ea - 1.0)
    cls = gidx * jnp.float32(len(_PAIRS)) + pair
    onehot = lane == cls
    earlier = jnp.dot(tri_ref[...], jnp.where(onehot, 1.0, 0.0).astype(BF16), preferred_element_type=F32)

    @pl.when(jnp.logical_and(pl.program_id(0) == 0, pl.program_id(1) == 0))
    def _():
        run_scr[...] = jnp.zeros_like(run_scr)

    running = run_scr[...]
    rank = jnp.sum(jnp.where(onehot, earlier + running, 0.0), axis=1, keepdims=True)
    running = running + jnp.sum(jnp.where(onehot, 1.0, 0.0), axis=0, keepdims=True)
    run_scr[...] = running
    cnt_ref[...] = running
    w_a = jnp.where(first_is_lower, w_top1, w_top2)
    w_b = jnp.where(first_is_lower, w_top2, w_top1)
    meta = jnp.where(lane_i == META_CLASS, cls,
                     jnp.where(lane_i == META_RANK, rank,
                               jnp.where(lane_i == META_WA, w_a, jnp.where(lane_i == META_WB, w_b, 0.0))))
    meta_ref[...] = meta
    route_ref[...] = meta.T[0:SUBLANES, :].astype(jnp.int32)


def _post(o_list, lse_list, pc, sga, x, mod4, wao, wo, g2, wr, br, expand, *, tm, full_precision, routed):
    nb, rows, _ = x.shape
    tiles = rows // tm
    dils = tuple(o.shape[1] for o in o_list)
    tile3 = lambda b, i: (b, i, 0)
    const2 = lambda b, i: (0, 0)
    mod_rows = mod4.shape[2]
    if mod_rows == 1:
        mod_spec = pl.BlockSpec((None, 6, 1, D_MODEL), lambda b, i: (b, 0, 0, 0))
    else:
        mod_spec = pl.BlockSpec((None, 6, tm, D_MODEL), lambda b, i: (b, 0, i, 0))
    cls4 = lambda b, i: (b, 0, i, 0)
    in_specs = (
        [pl.BlockSpec((None, d, tm // d, GROUP_WIDTH), cls4) for d in dils]
        + [pl.BlockSpec((None, d, tm // d, LANES), cls4) for d in dils]
        + [pl.BlockSpec((None, tm, D_MODEL), tile3)] * 3 + [mod_spec]
        + [pl.BlockSpec(wao.shape, const2), pl.BlockSpec(wo.shape, const2), pl.BlockSpec((1, D_MODEL), const2)]
        + [pl.BlockSpec((D_MODEL, LANES), const2)] * 3
        + [pl.BlockSpec((1, LANES), const2), pl.BlockSpec((LANES, GROUP_WIDTH), const2)]
    )
    wr_hi = wr.astype(BF16)
    wr_lo = (wr - wr_hi.astype(F32)).astype(BF16)
    args = [*o_list, *lse_list, pc, sga, x, mod4, wao, wo, g2, wr, wr_hi, wr_lo, br, expand]
    scratch = [pltpu.VMEM((N_DIL, CHUNKS_PER_GROUP, tm, LANES), F32), pltpu.VMEM((N_DIL, 1, tm, LANES), F32)]
    if routed:
        tri = jnp.asarray(np.tril(np.ones((tm, tm), np.float32), -1), dtype=BF16)
        args.append(tri)
        in_specs = in_specs + [pl.BlockSpec((tm, tm), const2)]
        out_specs = [pl.BlockSpec((None, tm, D_MODEL), tile3), pl.BlockSpec((None, tm * H_ROWS, LANES), tile3),
                     pl.BlockSpec((None, tm, LANES), tile3),
                     pl.BlockSpec((None, None, SUBLANES, tm), lambda b, i: (b, i, 0, 0)),
                     pl.BlockSpec((1, LANES), const2)]
        out_shape = [jax.ShapeDtypeStruct((nb, rows, D_MODEL), F32),
                     jax.ShapeDtypeStruct((nb, rows * H_ROWS, LANES), F32),
                     jax.ShapeDtypeStruct((nb, rows, LANES), F32),
                     jax.ShapeDtypeStruct((nb, tiles, SUBLANES, tm), jnp.int32),
                     jax.ShapeDtypeStruct((1, LANES), F32)]
        scratch.append(pltpu.VMEM((1, LANES), F32))
    else:
        out_specs = [pl.BlockSpec((None, tm, D_MODEL), tile3), pl.BlockSpec((None, tm, D_MODEL), tile3),
                     pl.BlockSpec((None, tm, LANES), tile3)]
        out_shape = [jax.ShapeDtypeStruct((nb, rows, D_MODEL), F32), jax.ShapeDtypeStruct((nb, rows, D_MODEL), F32),
                     jax.ShapeDtypeStruct((nb, rows, LANES), F32)]
    return pl.pallas_call(
        functools.partial(_post_kernel, tm=tm, dils=dils, full_precision=full_precision, routed=routed),
        grid=(nb, tiles),
        in_specs=in_specs,
        out_specs=out_specs,
        out_shape=out_shape,
        scratch_shapes=scratch,
        compiler_params=_cparams(("arbitrary", "arbitrary")),
        name="post_sample" if full_precision else "post_prompt",
    )(*args)


DMA_LOOP_UNROLL = 8


ROUTE_CLASS_ROW, ROUTE_RANK_ROW = META_CLASS, META_RANK


def _sorted_slot(starts_ref, route_ref, r):
    return starts_ref[route_ref[ROUTE_CLASS_ROW, r]] + route_ref[ROUTE_RANK_ROW, r]


def _token_rows(token, rows_per_token):
    return pl.ds(pl.multiple_of(token * rows_per_token, rows_per_token), rows_per_token)


def _dispatch_kernel(starts_ref, route_ref, src_ref, dst_hbm, buf, sems, *, rows_per_step, n_steps):
    i = pl.program_id(0)
    slot = i % 2

    def wait_slot(s):
        def body(r, carry):
            pltpu.make_async_copy(buf.at[s, _token_rows(0, H_ROWS)], dst_hbm.at[_token_rows(0, H_ROWS)],
                                  sems.at[s * rows_per_step + r]).wait()
            return carry
        lax.fori_loop(0, rows_per_step, body, 0, unroll=DMA_LOOP_UNROLL)

    @pl.when(i >= 1)
    def _():
        wait_slot(1 - slot)

    buf[slot] = src_ref[...]

    def issue(r, carry):
        pltpu.make_async_copy(buf.at[slot, _token_rows(r, H_ROWS)],
                              dst_hbm.at[_token_rows(_sorted_slot(starts_ref, route_ref, r), H_ROWS)],
                              sems.at[slot * rows_per_step + r]).start()
        return carry

    lax.fori_loop(0, rows_per_step, issue, 0, unroll=DMA_LOOP_UNROLL)

    @pl.when(i == n_steps - 1)
    def _():
        wait_slot(slot)


def _route_spec(rows_per_step, route_tile, shift=0, n_steps=None):
    per_tile = route_tile // rows_per_step

    def imap(i, starts):
        s = i + shift if n_steps is None else jnp.minimum(i + shift, n_steps - 1)
        return (s // per_tile, 0, s % per_tile)

    return pl.BlockSpec((None, SUBLANES, rows_per_step), imap, memory_space=pltpu.SMEM)


def _dispatch_rows(src, route, starts, *, rows_per_step):
    n = src.shape[0] // H_ROWS
    n_steps = n // rows_per_step
    grid_spec = pltpu.PrefetchScalarGridSpec(
        num_scalar_prefetch=1,
        grid=(n_steps,),
        in_specs=[_route_spec(rows_per_step, route.shape[2]),
                  pl.BlockSpec((rows_per_step * H_ROWS, LANES), lambda i, starts: (i, 0))],
        out_specs=pl.BlockSpec(memory_space=pl.ANY),
        scratch_shapes=[pltpu.VMEM((2, rows_per_step * H_ROWS, LANES), src.dtype),
                        pltpu.SemaphoreType.DMA((2 * rows_per_step,))],
    )
    return pl.pallas_call(
        functools.partial(_dispatch_kernel, rows_per_step=rows_per_step, n_steps=n_steps),
        grid_spec=grid_spec,
        out_shape=jax.ShapeDtypeStruct(src.shape, src.dtype),
        compiler_params=_cparams(("arbitrary",)),
        name="moe_dispatch",
    )(starts, route, src)


def _experts_kernel(tile_ref, ea_ref, eb_ref, lo_ref, hi_ref, first_ref, nv_ref,
                    x_ref, w1a_ref, w1b_ref, w3a_ref, w3b_ref, w2a_ref, w2b_ref, y_ref, *, tm):
    i = pl.program_id(0)

    @pl.when(i < nv_ref[0])
    def _():
        x = _load_token_rows(x_ref, tm, H_ROWS).astype(BF16)

        def expert(w1_ref, w3_ref, w2_ref):
            a = jnp.dot(x, w1_ref[...], preferred_element_type=F32)
            b = jnp.dot(x, w3_ref[...], preferred_element_type=F32)
            hid = (a * _sigmoid(a)) * b
            return jnp.dot(hid.astype(BF16), w2_ref[...], preferred_element_type=F32)

        y_both = jnp.concatenate([expert(w1a_ref, w3a_ref, w2a_ref), expert(w1b_ref, w3b_ref, w2b_ref)], axis=1)

        @pl.when(first_ref[i] == 1)
        def _():
            _store_token_rows(y_ref, y_both, tm)

        @pl.when(first_ref[i] == 0)
        def _():
            row = lax.broadcasted_iota(jnp.int32, (tm, 1), 0)
            old = _load_token_rows(y_ref, tm, Y_ROWS)
            _store_token_rows(y_ref, jnp.where(row >= lo_ref[i], jnp.where(row < hi_ref[i], y_both, old), old), tm)


def _experts_routed(hs, tables, w1_bf, w3_bf, w2_bf, *, tm):
    n = hs.shape[0] // H_ROWS
    n_items = tables[0].shape[0]
    x_map = lambda i, tile, ea, eb, lo, hi, first, nv: (tile[i], 0)
    wa_map = lambda i, tile, ea, eb, lo, hi, first, nv: (ea[i], 0, 0)
    wb_map = lambda i, tile, ea, eb, lo, hi, first, nv: (eb[i], 0, 0)
    w13 = lambda m: pl.BlockSpec((None, D_MODEL, D_EXPERT), m)
    w2s = lambda m: pl.BlockSpec((None, D_EXPERT, D_MODEL), m)
    grid_spec = pltpu.PrefetchScalarGridSpec(
        num_scalar_prefetch=7,
        grid=(n_items,),
        in_specs=[pl.BlockSpec((tm * H_ROWS, LANES), x_map),
                  w13(wa_map), w13(wb_map), w13(wa_map), w13(wb_map), w2s(wa_map), w2s(wb_map)],
        out_specs=pl.BlockSpec((tm * Y_ROWS, LANES), x_map),
    )
    return pl.pallas_call(
        functools.partial(_experts_kernel, tm=tm),
        grid_spec=grid_spec,
        out_shape=jax.ShapeDtypeStruct((n * Y_ROWS, LANES), F32),
        compiler_params=_cparams(("arbitrary",)),
        name="moe_experts",
    )(*tables, hs, w1_bf, w1_bf, w3_bf, w3_bf, w2_bf, w2_bf)


def _combine_kernel(starts_ref, route_ref, route_next_ref, yp_hbm, x1_ref, meta_ref, mod_ref, gf_ref, y_ref, ybuf, sems,
                    *, rows_per_step, n_steps):
    i = pl.program_id(0)
    slot = i % 2

    def issue(r_ref, s):
        def body(r, carry):
            pltpu.make_async_copy(yp_hbm.at[_token_rows(_sorted_slot(starts_ref, r_ref, r), Y_ROWS)],
                                  ybuf.at[s, _token_rows(r, Y_ROWS)], sems.at[s * rows_per_step + r]).start()
            return carry
        lax.fori_loop(0, rows_per_step, body, 0, unroll=DMA_LOOP_UNROLL)

    @pl.when(i == 0)
    def _():
        issue(route_ref, 0)

    @pl.when(i + 1 < n_steps)
    def _():
        issue(route_next_ref, 1 - slot)

    def wait_body(r, carry):
        pltpu.make_async_copy(yp_hbm.at[_token_rows(0, Y_ROWS)], ybuf.at[slot, _token_rows(r, Y_ROWS)],
                              sems.at[slot * rows_per_step + r]).wait()
        return carry

    lax.fori_loop(0, rows_per_step, wait_body, 0, unroll=DMA_LOOP_UNROLL)
    ya = _load_token_rows(ybuf, rows_per_step, H_ROWS, first_chunk=0, rows_per_token=Y_ROWS, lead=slot)
    yb = _load_token_rows(ybuf, rows_per_step, H_ROWS, first_chunk=H_ROWS, rows_per_token=Y_ROWS, lead=slot)
    meta = meta_ref[...]
    lane = lax.broadcasted_iota(jnp.int32, (1, LANES), 1)
    w_a = jnp.sum(jnp.where(lane == META_WA, meta, 0.0), axis=1, keepdims=True)
    w_b = jnp.sum(jnp.where(lane == META_WB, meta, 0.0), axis=1, keepdims=True)
    x2 = x1_ref[...] + mod_ref[5] * (w_a * ya + w_b * yb)
    var = jnp.mean(x2 * x2, axis=-1, keepdims=True)
    y_ref[...] = (x2 * lax.rsqrt(var + RMS_EPS)) * gf_ref[...]


def _combine_final(yp, route, starts, x1, meta, mod4, gf, *, rows_per_step, rows_per_mod):
    n = x1.shape[0]
    n_steps = n // rows_per_step
    steps_per_mod = rows_per_mod // rows_per_step
    row = lambda i, starts: (i, 0)
    grid_spec = pltpu.PrefetchScalarGridSpec(
        num_scalar_prefetch=1,
        grid=(n_steps,),
        in_specs=[_route_spec(rows_per_step, route.shape[2]),
                  _route_spec(rows_per_step, route.shape[2], shift=1, n_steps=n_steps),
                  pl.BlockSpec(memory_space=pl.ANY),
                  pl.BlockSpec((rows_per_step, D_MODEL), row),
                  pl.BlockSpec((rows_per_step, LANES), row),
                  pl.BlockSpec((None, 6, 1, D_MODEL), lambda i, starts: (i // steps_per_mod, 0, 0, 0)),
                  pl.BlockSpec((1, D_MODEL), lambda i, starts: (0, 0))],
        out_specs=pl.BlockSpec((rows_per_step, D_MODEL), row),
        scratch_shapes=[pltpu.VMEM((2, rows_per_step * Y_ROWS, LANES), F32),
                        pltpu.SemaphoreType.DMA((2 * rows_per_step,))],
    )
    return pl.pallas_call(
        functools.partial(_combine_kernel, rows_per_step=rows_per_step, n_steps=n_steps),
        grid_spec=grid_spec,
        out_shape=jax.ShapeDtypeStruct((n, D_MODEL), F32),
        compiler_params=_cparams(("arbitrary",)),
        name="moe_combine",
    )(starts, route, route, yp, x1, meta, mod4, gf)


def _routing_tables(counts, *, n, tm):
    counts = counts.astype(jnp.int32)
    ends = jnp.cumsum(counts)
    starts = ends - counts
    n_items = n // tm + N_CLASSES
    first_tile = starts // tm
    last_tile = jnp.maximum(ends - 1, starts) // tm
    visits = jnp.where(counts > 0, last_tile - first_tile + 1, 0)
    item_end = jnp.cumsum(visits)
    item_start = item_end - visits
    n_valid = item_end[-1]
    item = jnp.arange(n_items, dtype=jnp.int32)
    idx = jnp.minimum(item, n_valid - 1)
    c = jnp.sum((idx[:, None] >= item_end[None, :]).astype(jnp.int32), axis=1)
    class_ids = jnp.arange(N_CLASSES, dtype=jnp.int32)

    def pick(table):
        return jnp.sum(jnp.where(c[:, None] == class_ids[None, :], table[None, :], 0), axis=1)

    tile = pick(first_tile) + (idx - pick(item_start))
    live = item < n_valid
    lo = jnp.where(live, jnp.clip(pick(starts) - tile * tm, 0, tm), 0)
    hi = jnp.where(live, jnp.clip(pick(ends) - tile * tm, 0, tm), 0)
    prev_tile = jnp.concatenate([jnp.full((1,), -1, jnp.int32), tile[:-1]])
    first = (tile != prev_tile).astype(jnp.int32)
    group_of_class = np.arange(N_CLASSES) // len(_PAIRS)
    pair_of_class = np.arange(N_CLASSES) % len(_PAIRS)
    pairs = np.asarray(_PAIRS)
    ea = pick(jnp.asarray(group_of_class * EXPERTS_PER_GROUP + pairs[pair_of_class, 0], dtype=jnp.int32))
    eb = pick(jnp.asarray(group_of_class * EXPERTS_PER_GROUP + pairs[pair_of_class, 1], dtype=jnp.int32))
    as_i32 = lambda a: a.astype(jnp.int32)
    return as_i32(starts), tuple(as_i32(a) for a in (tile, ea, eb, lo, hi, first, n_valid.reshape(1)))


def _moe_kernel(h_ref, comb_ref, x1_ref, mod_ref, gf_ref, w1_ref, w3_ref, w2_ref, y_ref, acc_ref):
    e = pl.program_id(2)

    @pl.when(e == 0)
    def _():
        acc_ref[...] = jnp.zeros_like(acc_ref)

    h = h_ref[...].astype(BF16)
    a = jnp.dot(h, w1_ref[...], preferred_element_type=F32)
    b = jnp.dot(h, w3_ref[...], preferred_element_type=F32)
    lane = lax.broadcasted_iota(jnp.int32, (1, LANES), 1)
    cw = jnp.sum(jnp.where(lane == e + N_EXPERT_GROUPS, comb_ref[...], 0.0), axis=1, keepdims=True)
    hid = (a * _sigmoid(a)) * b * cw
    acc_ref[...] += jnp.dot(hid.astype(BF16), w2_ref[...], preferred_element_type=F32)

    @pl.when(e == N_EXPERTS - 1)
    def _():
        x2 = x1_ref[...] + mod_ref[5] * acc_ref[...]
        var = jnp.mean(x2 * x2, axis=-1, keepdims=True)
        y_ref[...] = (x2 * lax.rsqrt(var + RMS_EPS)) * gf_ref[...]


def _moe_dense(h2, comb, x1, mod4, gf, w1_bf, w3_bf, w2_bf, *, tm):
    nb, rows, _ = x1.shape
    tiles = rows // tm
    tile3 = lambda b, i, e: (b, i, 0)
    if mod4.shape[2] == 1:
        mod_spec = pl.BlockSpec((None, 6, 1, D_MODEL), lambda b, i, e: (b, 0, 0, 0))
    else:
        mod_spec = pl.BlockSpec((None, 6, tm, D_MODEL), lambda b, i, e: (b, 0, i, 0))
    return pl.pallas_call(
        _moe_kernel,
        grid=(nb, tiles, N_EXPERTS),
        in_specs=[
            pl.BlockSpec((None, tm, D_MODEL), tile3),
            pl.BlockSpec((None, tm, LANES), tile3),
            pl.BlockSpec((None, tm, D_MODEL), tile3),
            mod_spec,
            pl.BlockSpec((1, D_MODEL), lambda b, i, e: (0, 0)),
            pl.BlockSpec((None, D_MODEL, D_EXPERT), lambda b, i, e: (e, 0, 0)),
            pl.BlockSpec((None, D_MODEL, D_EXPERT), lambda b, i, e: (e, 0, 0)),
            pl.BlockSpec((None, D_EXPERT, D_MODEL), lambda b, i, e: (e, 0, 0)),
        ],
        out_specs=pl.BlockSpec((None, tm, D_MODEL), tile3),
        out_shape=jax.ShapeDtypeStruct((nb, rows, D_MODEL), F32),
        scratch_shapes=[pltpu.VMEM((tm, D_MODEL), F32)],
        compiler_params=_cparams(("arbitrary", "arbitrary", "arbitrary")),
        name=f"moe_dense_{nb * rows}",
    )(h2, comb, x1, mod4, gf, w1_bf, w3_bf, w2_bf)


def _s_inproj_kernel(x_ref, mod_ref, g1_ref, w_ref, z_ref):
    x = x_ref[...]
    var = jnp.mean(x * x, axis=-1, keepdims=True)
    h = (x * lax.rsqrt(var + RMS_EPS)) * (g1_ref[...] * (1.0 + mod_ref[1])) + mod_ref[0]
    z_ref[...] = jnp.dot(h, w_ref[...], precision=HIGHEST, preferred_element_type=F32)


def _s_inproj(x, mod_tok, g1, w_in, *, col_block):
    n = x.shape[0]
    return pl.pallas_call(
        _s_inproj_kernel,
        grid=(IN_COLS // col_block,),
        in_specs=[pl.BlockSpec((n, D_MODEL), lambda j: (0, 0)),
                  pl.BlockSpec((6, n, D_MODEL), lambda j: (0, 0, 0)),
                  pl.BlockSpec((1, D_MODEL), lambda j: (0, 0)),
                  pl.BlockSpec((D_MODEL, col_block), lambda j: (0, j))],
        out_specs=pl.BlockSpec((n, col_block), lambda j: (0, j)),
        out_shape=jax.ShapeDtypeStruct((n, IN_COLS), F32),
        compiler_params=_cparams(("arbitrary",)),
        name="inproj_sample",
    )(x, mod_tok, g1, w_in)


def _s_mid_kernel(z_ref, p0_ref, p1_ref, cw_ref, wco_ref, rq_ref, rk_ref,
                  cu_ref, pc_ref, sga_ref, q_ref, k_ref, v_ref, *, t_len):
    n = z_ref.shape[0]
    cu = z_ref[:, OFF_GC:OFF_GC + D_MODEL] * z_ref[:, OFF_U:OFF_U + D_MODEL]
    cu_ref[...] = cu
    t = lax.broadcasted_iota(jnp.int32, (n, D_MODEL), 0) & (t_len - 1)
    prev1 = jnp.where(t >= 1, pltpu.roll(cu, 1, 0), p1_ref[...])
    prev2 = jnp.where(t >= 2, pltpu.roll(cu, 2, 0), jnp.where(t == 0, p0_ref[...], p1_ref[...]))
    cw = cw_ref[...]
    conv = cw[0:1] * prev2 + cw[1:2] * prev1 + cw[2:3] * cu
    yc = jnp.dot(z_ref[:, OFF_GB:OFF_GB + D_MODEL] * conv, wco_ref[...], precision=HIGHEST,
                 preferred_element_type=F32)
    pc_ref[...] = _sigmoid(z_ref[:, OFF_GCONV:OFF_GCONV + D_MODEL]) * yc
    sga_ref[...] = _sigmoid(z_ref[:, OFF_GATTN:OFF_GATTN + D_MODEL])
    aq, bmq, bpq = rq_ref[0], rq_ref[1], rq_ref[2]
    ak, bmk, bpk = rk_ref[0], rk_ref[1], rk_ref[2]
    for c in range(ATTN_WIDTH // LANES):
        sl = slice(c * LANES, (c + 1) * LANES)
        q_ref[:, sl] = _rope_chunk(z_ref[:, OFF_Q + c * LANES:OFF_Q + (c + 1) * LANES], aq, bmq, bpq)
        k_ref[:, sl] = _rope_chunk(z_ref[:, OFF_K + c * LANES:OFF_K + (c + 1) * LANES], ak, bmk, bpk)
    v_ref[...] = z_ref[:, OFF_V:OFF_V + ATTN_WIDTH]


def _s_mid(z, p0e, p1e, conv_w, w_conv_out, rope_q, rope_k, *, t_len):
    n = z.shape[0]
    assert t_len & (t_len - 1) == 0
    full = lambda shape: pl.BlockSpec(shape, lambda i: (0,) * len(shape))
    out_shape = [jax.ShapeDtypeStruct((n, D_MODEL), F32)] * 3 + [jax.ShapeDtypeStruct((n, ATTN_WIDTH), F32)] * 3
    return pl.pallas_call(
        functools.partial(_s_mid_kernel, t_len=t_len),
        grid=(1,),
        in_specs=[full(z.shape), full(p0e.shape), full(p1e.shape), full(conv_w.shape), full(w_conv_out.shape),
                  full(rope_q.shape), full(rope_k.shape)],
        out_specs=[full((n, D_MODEL))] * 3 + [full((n, ATTN_WIDTH))] * 3,
        out_shape=out_shape,
        compiler_params=_cparams(("arbitrary",)),
        name="mid_sample",
    )(z, p0e, p1e, conv_w, w_conv_out, rope_q, rope_k)


def _head_sum(x):
    return jnp.sum(x.reshape(HEADS_PER_GROUP, HEAD_DIM, x.shape[-1]), axis=1)


def _head_expand(x):
    n = x.shape[-1]
    return jnp.broadcast_to(x[:, None, :], (HEADS_PER_GROUP, HEAD_DIM, n)).reshape(GROUP_WIDTH, n)


def _s_attn_disjoint(q_ref, kn, vn, ck_ref, cv_ref, o_ref, lse_ref, s_scr, lane, new_idx, *, n_tiles, dil, t_len):
    cls = lane & (dil - 1)
    q_all = q_ref[...]
    qsel = jnp.zeros((GROUP_WIDTH, LANES), F32)
    for t in range(t_len):
        qsel = jnp.where(cls == t, jnp.broadcast_to(q_all[:, t:t + 1], (GROUP_WIDTH, LANES)), qsel)
    smax = None
    for j in range(n_tiles):
        s = jnp.where(cls < t_len, _head_sum(ck_ref[:, j * LANES:(j + 1) * LANES] * qsel), NEG_BIG)
        s_scr[:, j * LANES:(j + 1) * LANES] = s
        smax = s if smax is None else jnp.maximum(smax, s)
    s_new = _head_sum(kn * q_all)
    m_cols = []
    m_lane = jnp.zeros((HEADS_PER_GROUP, LANES), F32)
    m_new = jnp.zeros((HEADS_PER_GROUP, t_len), F32)
    for t in range(t_len):
        mt = jnp.maximum(jnp.max(jnp.where(cls == t, smax, NEG_BIG), axis=1, keepdims=True), s_new[:, t:t + 1])
        m_cols.append(mt)
        m_lane = jnp.where(cls == t, mt, m_lane)
        m_new = jnp.where(new_idx == t, mt, m_new)
    esum = jnp.zeros((HEADS_PER_GROUP, LANES), F32)
    for j in range(n_tiles):
        e = jnp.exp(s_scr[:, j * LANES:(j + 1) * LANES] - m_lane)
        s_scr[:, j * LANES:(j + 1) * LANES] = e
        esum = esum + e
    e_new = jnp.exp(s_new - m_new)
    l_cols = []
    inv_lane = jnp.zeros((HEADS_PER_GROUP, LANES), F32)
    inv_new = jnp.zeros((HEADS_PER_GROUP, t_len), F32)
    for t in range(t_len):
        lt = jnp.sum(jnp.where(cls == t, esum, 0.0), axis=1, keepdims=True) + e_new[:, t:t + 1]
        l_cols.append(lt)
        inv_lane = jnp.where(cls == t, 1.0 / lt, inv_lane)
        inv_new = jnp.where(new_idx == t, 1.0 / lt, inv_new)
    acc = None
    for j in range(n_tiles):
        term = cv_ref[:, j * LANES:(j + 1) * LANES] * _head_expand(s_scr[:, j * LANES:(j + 1) * LANES] * inv_lane)
        acc = term if acc is None else acc + term
    o_new = vn * _head_expand(e_new * inv_new)
    for t in range(t_len):
        o_ref[:, t:t + 1] = jnp.sum(jnp.where(cls == t, acc, 0.0), axis=1, keepdims=True) + o_new[:, t:t + 1]
        lse_ref[:, t:t + 1] = m_cols[t] + jnp.log(l_cols[t])


def _s_attn_kernel(q_ref, kn_ref, vn_ref, kt_ref, vt_ref, ck_ref, cv_ref, o_ref, lse_ref, ko_ref, vo_ref, s_scr,
                   *, win, dil, t_len):
    n_tiles = win // LANES
    lane = lax.broadcasted_iota(jnp.int32, (1, LANES), 1)
    new_idx = lax.broadcasted_iota(jnp.int32, (1, t_len), 1)
    kn = kn_ref[...]
    vn = vn_ref[...]
    if dil >= t_len:
        _s_attn_disjoint(q_ref, kn, vn, ck_ref, cv_ref, o_ref, lse_ref, s_scr, lane, new_idx,
                         n_tiles=n_tiles, dil=dil, t_len=t_len)
    for t in range(t_len if dil < t_len else 0):
        qb = jnp.broadcast_to(q_ref[:, t:t + 1], (GROUP_WIDTH, LANES))
        m = None
        for j in range(n_tiles):
            pos = lane + j * LANES
            s = _head_sum(ck_ref[:, j * LANES:(j + 1) * LANES] * qb)
            s = jnp.where(pos >= t, jnp.where(((pos - t) & (dil - 1)) == 0, s, NEG_BIG), NEG_BIG)
            s_scr[:, j * LANES:(j + 1) * LANES] = s
            mj = jnp.max(s, axis=1, keepdims=True)
            m = mj if m is None else jnp.maximum(m, mj)
        s_new = _head_sum(kn * qb[:, 0:t_len])
        s_new = jnp.where(new_idx <= t, jnp.where(((t - new_idx) & (dil - 1)) == 0, s_new, NEG_BIG), NEG_BIG)
        m = jnp.maximum(m, jnp.max(s_new, axis=1, keepdims=True))
        e_new = jnp.exp(s_new - m)
        l = jnp.sum(e_new, axis=1, keepdims=True)
        acc = None
        for j in range(n_tiles):
            e = jnp.exp(s_scr[:, j * LANES:(j + 1) * LANES] - m)
            l = l + jnp.sum(e, axis=1, keepdims=True)
            term = cv_ref[:, j * LANES:(j + 1) * LANES] * _head_expand(e)
            acc = term if acc is None else acc + term
        o = jnp.sum(acc, axis=1, keepdims=True) + jnp.sum(vn * _head_expand(e_new), axis=1, keepdims=True)
        o_ref[:, t:t + 1] = o * _head_expand(1.0 / l)
        lse_ref[:, t:t + 1] = m + jnp.log(l)

    for c_ref, tail_ref, out_ref in ((ck_ref, kt_ref, ko_ref), (cv_ref, vt_ref, vo_ref)):
        rolled = pltpu.roll(c_ref[...], win - t_len, 1)
        if win > LANES:
            out_ref[:, 0:win - LANES] = rolled[:, 0:win - LANES]
        out_ref[:, win - LANES:win] = jnp.where(lane >= LANES - t_len, tail_ref[...], rolled[:, win - LANES:win])


def _s_attn_group(q_t, kn_t, vn_t, k_tail, v_tail, cache_k, cache_v, g):
    bsz, _, t_len = q_t.shape
    win_full, dil = DIL_GROUPS[g]
    win = cache_k.shape[2]
    assert win == win_full and win == (KEYS_PER_QUERY - 1) * dil and win % LANES == 0
    assert dil & (dil - 1) == 0 and LANES % dil == 0
    grp = lambda b: (b, g, 0)
    per_b = lambda b: (b, 0, 0)
    return pl.pallas_call(
        functools.partial(_s_attn_kernel, win=win, dil=dil, t_len=t_len),
        grid=(bsz,),
        in_specs=[pl.BlockSpec((None, GROUP_WIDTH, t_len), grp)] * 3
        + [pl.BlockSpec((None, GROUP_WIDTH, LANES), grp)] * 2
        + [pl.BlockSpec((None, GROUP_WIDTH, win), per_b)] * 2,
        out_specs=[pl.BlockSpec((None, GROUP_WIDTH, t_len), per_b),
                   pl.BlockSpec((None, HEADS_PER_GROUP, t_len), per_b),
                   pl.BlockSpec((None, GROUP_WIDTH, win), per_b),
                   pl.BlockSpec((None, GROUP_WIDTH, win), per_b)],
        out_shape=[jax.ShapeDtypeStruct((bsz, GROUP_WIDTH, t_len), F32),
                   jax.ShapeDtypeStruct((bsz, HEADS_PER_GROUP, t_len), F32),
                   jax.ShapeDtypeStruct((bsz, GROUP_WIDTH, win), F32),
                   jax.ShapeDtypeStruct((bsz, GROUP_WIDTH, win), F32)],
        scratch_shapes=[pltpu.VMEM((HEADS_PER_GROUP, win), F32)],
        compiler_params=_cparams(("arbitrary",)),
        name=f"attn_sample_g{g}",
    )(q_t, kn_t, vn_t, k_tail, v_tail, cache_k, cache_v)


TM_INPROJ = 512
TM_POST = 512
TM_EXPERT = 256
ROWS_PER_DMA_STEP = 128
ATTN_MAX_CHUNK = 1024
S_COL_BLOCK = 512


def _to_state(a_t):
    b, _, length = a_t.shape
    return jnp.transpose(a_t.reshape(b, HEADS_PER_GROUP, HEAD_DIM, length), (0, 3, 1, 2))[None]


def _from_state(a):
    b, length = a.shape[0], a.shape[1]
    return jnp.transpose(a, (0, 2, 3, 1)).reshape(b, GROUP_WIDTH, length)


def kernel(x_prompt, x_sample, cache_k1, cache_v1, cache_k2, cache_v2, cache_k3, cache_v3, state_conv,
           c_prompt, c_sample, norm1_g, norm2_g, normf_g, w_ada, b_ada, w_in, conv_w, w_conv_out,
           w_attn_out, w_o, w_rg, b_rg, w_re, b_re, w1, w3, w2):
    depth = w_in.shape[0]
    assert depth == 1
    bsz, seq, _ = x_prompt.shape
    dbsz, t_len, _ = x_sample.shape
    n_s = dbsz * t_len
    l = 0

    w_in_bf = w_in[l].astype(BF16)
    wco_bf = w_conv_out[l].astype(BF16)
    wao_bf = w_attn_out[l].astype(BF16)
    wo_bf = w_o[l].astype(BF16)
    w1_bf = w1[l].reshape(N_EXPERTS, D_MODEL, D_EXPERT).astype(BF16)
    w3_bf = w3[l].reshape(N_EXPERTS, D_MODEL, D_EXPERT).astype(BF16)
    w2_bf = w2[l].reshape(N_EXPERTS, D_EXPERT, D_MODEL).astype(BF16)
    g1 = norm1_g[l].reshape(1, D_MODEL)
    g2 = norm2_g[l].reshape(1, D_MODEL)
    gf = normf_g.reshape(1, D_MODEL)
    n_route = N_EXPERT_GROUPS + N_EXPERTS
    wr = jnp.pad(jnp.concatenate([w_rg[l], w_re[l]], axis=1), ((0, 0), (0, LANES - n_route)))
    br = jnp.pad(jnp.concatenate([b_rg[l], b_re[l]]), (0, LANES - n_route)).reshape(1, LANES)
    head_of_lane = np.arange(GROUP_WIDTH) // HEAD_DIM
    expand_bf = jnp.asarray((np.arange(LANES)[:, None] == head_of_lane[None, :]).astype(np.float32), dtype=BF16)

    mod = _adaln(jnp.concatenate([c_prompt, c_sample], axis=0), w_ada[l], b_ada[l])
    mod_p4 = mod[:bsz].reshape(bsz, 6, 1, D_MODEL)
    mod_tok = jnp.repeat(mod[bsz:].reshape(dbsz, 1, 6, D_MODEL), t_len, axis=1)
    mod_tok = jnp.transpose(mod_tok.reshape(n_s, 6, D_MODEL), (1, 0, 2))
    mod_s4 = mod_tok[None]

    pos_p = jnp.arange(seq, dtype=jnp.int32)
    rope_q_p = _rope_tables(pos_p, HEAD_DIM ** -0.5)
    rope_k_p = _rope_tables(pos_p, 1.0)
    pc_p, sga_p, conv_p = _convproj_prompt(x_prompt, mod_p4, g1, w_in_bf[:, :OFF_Q], w_in_bf[:, OFF_GCONV:],
                                           conv_w[l], wco_bf, tm=TM_INPROJ)
    outs = _qkvproj_prompt(x_prompt, mod_p4, g1, w_in_bf[:, OFF_Q:OFF_GCONV], rope_q_p, rope_k_p, tm=TM_INPROJ)
    qkv_p = outs[0:3 * N_DIL]
    states_p = outs[3 * N_DIL:]
    o_p, lse_p = [], []
    for g in range(N_DIL):
        o_g, lse_g = _attn_prompt_group(qkv_p[3 * g], qkv_p[3 * g + 1], qkv_p[3 * g + 2], g, max_chunk=ATTN_MAX_CHUNK)
        o_p.append(o_g)
        lse_p.append(lse_g)
    x1_p, h2rows_p, meta_p, route_p, cnt_p = _post(o_p, lse_p, pc_p, sga_p, x_prompt, mod_p4, wao_bf, wo_bf, g2, wr, br,
                                                   expand_bf, tm=TM_POST, full_precision=False, routed=True)
    n_p = bsz * seq
    route = route_p.reshape(n_p // TM_POST, SUBLANES, TM_POST)
    starts, tables = _routing_tables(cnt_p[0, :N_CLASSES], n=n_p, tm=TM_EXPERT)
    hs = _dispatch_rows(h2rows_p.reshape(n_p * H_ROWS, LANES), route, starts, rows_per_step=ROWS_PER_DMA_STEP)
    yp = _experts_routed(hs, tables, w1_bf, w3_bf, w2_bf, tm=TM_EXPERT)
    y_p = _combine_final(yp, route, starts, x1_p.reshape(n_p, D_MODEL), meta_p.reshape(n_p, LANES), mod_p4, gf,
                         rows_per_step=ROWS_PER_DMA_STEP, rows_per_mod=seq).reshape(bsz, seq, D_MODEL)

    pos_s = PAST_LEN + jnp.arange(t_len, dtype=jnp.int32)
    rope_q_s = jnp.tile(_rope_tables(pos_s, HEAD_DIM ** -0.5), (1, dbsz, 1))
    rope_k_s = jnp.tile(_rope_tables(pos_s, 1.0), (1, dbsz, 1))
    xs = x_sample.reshape(n_s, D_MODEL)
    z_s = _s_inproj(xs, mod_tok, g1, w_in[l], col_block=S_COL_BLOCK)
    past = state_conv[l]
    p0e = jnp.repeat(past[:, 0], t_len, axis=0)
    p1e = jnp.repeat(past[:, 1], t_len, axis=0)
    cu_s, pc_s, sga_s, q_s, k_s, v_s = _s_mid(z_s, p0e, p1e, conv_w[l], w_conv_out[l], rope_q_s, rope_k_s,
                                              t_len=t_len)
    to_cols = lambda a: jnp.transpose(a.reshape(dbsz, t_len, ATTN_WIDTH), (0, 2, 1))
    q_t, kn_t, vn_t = to_cols(q_s), to_cols(k_s), to_cols(v_s)
    k_tail = jnp.pad(kn_t, ((0, 0), (0, 0), (LANES - t_len, 0)))
    v_tail = jnp.pad(vn_t, ((0, 0), (0, 0), (LANES - t_len, 0)))
    caches = ((cache_k1, cache_v1), (cache_k2, cache_v2), (cache_k3, cache_v3))
    o_s, lse_s, kv_s = [], [], []
    for g, (ck, cv) in enumerate(caches):
        o_g, lse_g, ko, vo = _s_attn_group(q_t, kn_t, vn_t, k_tail, v_tail, _from_state(ck[l]), _from_state(cv[l]), g)
        o_s.append(jnp.transpose(o_g, (0, 2, 1)).reshape(1, 1, n_s, GROUP_WIDTH))
        lse_rows = jnp.transpose(lse_g, (0, 2, 1)).reshape(n_s, HEADS_PER_GROUP)
        lse_s.append(jnp.pad(lse_rows, ((0, 0), (0, LANES - HEADS_PER_GROUP))).reshape(1, 1, n_s, LANES))
        kv_s += [_to_state(ko), _to_state(vo)]
    x1_s, h2_s, comb_s = _post(o_s, lse_s, pc_s[None], sga_s[None], xs[None], mod_s4, w_attn_out[l], w_o[l], g2, wr, br,
                               expand_bf, tm=n_s, full_precision=True, routed=False)
    y_s = _moe_dense(h2_s, comb_s, x1_s, mod_s4, gf, w1_bf, w3_bf, w2_bf, tm=n_s)

    conv_s = cu_s.reshape(dbsz, t_len, D_MODEL)[:, t_len - (CONV_K - 1):]
    return (y_p, y_s.reshape(dbsz, t_len, D_MODEL),
            *[_to_state(a) for a in states_p], conv_p.reshape(1, bsz, CONV_K - 1, D_MODEL),
            *kv_s, conv_s.reshape(1, dbsz, CONV_K - 1, D_MODEL))
```

```python
import functools

import numpy as np
import jax
import jax.numpy as jnp
from jax import lax
from jax.experimental import pallas as pl
from jax.experimental.pallas import tpu as pltpu

F32 = jnp.float32
BF16 = jnp.bfloat16
HIGHEST = lax.Precision.HIGHEST

D_MODEL = 1024
HEAD_DIM = 64
HEADS_PER_GROUP = 8
GROUP_WIDTH = HEADS_PER_GROUP * HEAD_DIM
DIL_GROUPS = ((128, 1), (512, 4), (2048, 16))
N_DIL = len(DIL_GROUPS)
ATTN_WIDTH = N_DIL * GROUP_WIDTH
ROT_DIM = HEAD_DIM // 4
ROPE_THETA = 500000.0
PAST_LEN = 16384
CONV_K = 3
N_EXPERT_GROUPS = 4
EXPERTS_PER_GROUP = 4
N_EXPERTS = N_EXPERT_GROUPS * EXPERTS_PER_GROUP
D_EXPERT = 512
RMS_EPS = 1e-6
IN_COLS = 3 * D_MODEL + 3 * ATTN_WIDTH + 2 * D_MODEL
OFF_U, OFF_GC, OFF_GB = 0, D_MODEL, 2 * D_MODEL
OFF_Q = 3 * D_MODEL
OFF_K = OFF_Q + ATTN_WIDTH
OFF_V = OFF_K + ATTN_WIDTH
OFF_GCONV = OFF_V + ATTN_WIDTH
OFF_GATTN = OFF_GCONV + D_MODEL

LANES = 128
SUBLANES = 8
CHUNKS_PER_GROUP = GROUP_WIDTH // LANES
KEYS_PER_QUERY = 129
Q_BLOCK = 128
NEG_BIG = -1e30

VMEM_LIMIT = 56 * 1024 * 1024


def _sigmoid(x):
    return 1.0 / (1.0 + jnp.exp(-x))


def _cparams(sem):
    return pltpu.CompilerParams(dimension_semantics=sem, vmem_limit_bytes=VMEM_LIMIT)


def _adaln_kernel(c_ref, w_ref, b_ref, o_ref):
    c = c_ref[...]
    s = c * _sigmoid(c)
    o_ref[...] = jnp.dot(s, w_ref[...], precision=HIGHEST, preferred_element_type=F32) + b_ref[...]


def _adaln(c_all, w_ada, b_ada):
    rows = c_all.shape[0]
    n_col = w_ada.shape[1] // D_MODEL
    return pl.pallas_call(
        _adaln_kernel,
        grid=(n_col,),
        in_specs=[
            pl.BlockSpec((rows, D_MODEL), lambda j: (0, 0)),
            pl.BlockSpec((D_MODEL, D_MODEL), lambda j: (0, j)),
            pl.BlockSpec((1, D_MODEL), lambda j: (0, j)),
        ],
        out_specs=pl.BlockSpec((rows, D_MODEL), lambda j: (0, j)),
        out_shape=jax.ShapeDtypeStruct((rows, w_ada.shape[1]), F32),
        compiler_params=_cparams(("arbitrary",)),
        name="adaln",
    )(c_all, w_ada, b_ada.reshape(1, -1))


def _rope_tables(pos, scale):
    half = ROT_DIM // 2
    inv_freq = jnp.power(jnp.float32(ROPE_THETA), -jnp.arange(half, dtype=F32) / half)
    ang = pos.astype(F32)[:, None] * inv_freq[None, :]
    cos, sin = jnp.cos(ang), jnp.sin(ang)
    lane_in_head = np.arange(LANES) % HEAD_DIM
    freq = lane_in_head % half
    first = lane_in_head < half
    second = (lane_in_head >= half) & (lane_in_head < ROT_DIM)
    a = jnp.where(first | second, cos[:, freq], 1.0)
    bm = jnp.where(first, -sin[:, freq], 0.0)
    bp = jnp.where(second, sin[:, freq], 0.0)
    return jnp.stack([a, bm, bp]) * scale


def _rope_chunk(zc, a, bm, bp):
    return zc * a + pltpu.roll(zc, LANES - ROT_DIM // 2, 1) * bm + pltpu.roll(zc, ROT_DIM // 2, 1) * bp


def _modulated_norm_bf16(x_ref, mod_ref, g1_ref):
    x = x_ref[...]
    var = jnp.mean(x * x, axis=-1, keepdims=True)
    h = (x * lax.rsqrt(var + RMS_EPS)) * (g1_ref[...] * (1.0 + mod_ref[1])) + mod_ref[0]
    return h.astype(BF16)


def _convproj_kernel(x_ref, mod_ref, g1_ref, wa_ref, wg_ref, cw_ref, wco_ref, pc_ref, sga_ref, cst_ref, s_ref,
                     *, tm, n_tiles):
    i = pl.program_id(1)
    hb = _modulated_norm_bf16(x_ref, mod_ref, g1_ref)

    def proj(w_ref, lo):
        return jnp.dot(hb, w_ref[:, lo:lo + D_MODEL], preferred_element_type=F32)

    cu = proj(wa_ref, OFF_GC) * proj(wa_ref, OFF_U)

    @pl.when(i == 0)
    def _():
        s_ref[0:SUBLANES, :] = jnp.zeros((SUBLANES, D_MODEL), F32)

    s_ref[SUBLANES:SUBLANES + tm, :] = cu
    cw = cw_ref[...]
    conv = (cw[0:1] * s_ref[SUBLANES - 2:SUBLANES - 2 + tm, :]
            + cw[1:2] * s_ref[SUBLANES - 1:SUBLANES - 1 + tm, :]
            + cw[2:3] * cu)
    yc = jnp.dot((proj(wa_ref, OFF_GB) * conv).astype(BF16), wco_ref[...], preferred_element_type=F32)
    pc_ref[...] = (_sigmoid(proj(wg_ref, 0)) * yc).astype(pc_ref.dtype)
    sga_ref[...] = _sigmoid(proj(wg_ref, D_MODEL)).astype(sga_ref.dtype)

    @pl.when(i == n_tiles - 1)
    def _():
        cst_ref[...] = s_ref[tm + SUBLANES - 2:tm + SUBLANES, :]

    s_ref[0:SUBLANES, :] = s_ref[tm:tm + SUBLANES, :]


def _convproj_prompt(x, mod4, g1, wa_bf, wg_bf, conv_w, wco_bf, *, tm):
    bsz, seq, _ = x.shape
    n_tiles = seq // tm
    const2 = lambda b, i: (0, 0)
    tile3 = lambda b, i: (b, i, 0)
    return pl.pallas_call(
        functools.partial(_convproj_kernel, tm=tm, n_tiles=n_tiles),
        grid=(bsz, n_tiles),
        in_specs=[
            pl.BlockSpec((None, tm, D_MODEL), tile3),
            pl.BlockSpec((None, 6, 1, D_MODEL), lambda b, i: (b, 0, 0, 0)),
            pl.BlockSpec((1, D_MODEL), const2),
            pl.BlockSpec(wa_bf.shape, const2, pipeline_mode=pl.Buffered(1)),
            pl.BlockSpec(wg_bf.shape, const2, pipeline_mode=pl.Buffered(1)),
            pl.BlockSpec((CONV_K, D_MODEL), const2),
            pl.BlockSpec((D_MODEL, D_MODEL), const2, pipeline_mode=pl.Buffered(1)),
        ],
        out_specs=[pl.BlockSpec((None, tm, D_MODEL), tile3), pl.BlockSpec((None, tm, D_MODEL), tile3),
                   pl.BlockSpec((None, CONV_K - 1, D_MODEL), lambda b, i: (b, 0, 0))],
        out_shape=[jax.ShapeDtypeStruct((bsz, seq, D_MODEL), BF16), jax.ShapeDtypeStruct((bsz, seq, D_MODEL), BF16),
                   jax.ShapeDtypeStruct((bsz, CONV_K - 1, D_MODEL), F32)],
        scratch_shapes=[pltpu.VMEM((tm + SUBLANES, D_MODEL), F32)],
        compiler_params=_cparams(("arbitrary", "arbitrary")),
        name="convproj_prompt",
    )(x, mod4, g1, wa_bf, wg_bf, conv_w, wco_bf)


def _qkvproj_kernel(x_ref, mod_ref, g1_ref, w_ref, rq_ref, rk_ref, *rest, tm, n_tiles, seq):
    qkv_refs = rest[0:3 * N_DIL]
    st_refs = rest[3 * N_DIL:3 * N_DIL + 2 * N_DIL]
    d_ref = rest[-1]
    i = pl.program_id(1)
    hb = _modulated_norm_bf16(x_ref, mod_ref, g1_ref)

    def proj(lo, width):
        return jnp.dot(hb, w_ref[:, lo:lo + width], preferred_element_type=F32)

    zq = proj(0, ATTN_WIDTH)
    zk = proj(ATTN_WIDTH, ATTN_WIDTH)
    zv = proj(2 * ATTN_WIDTH, ATTN_WIDTH)
    aq, bmq, bpq = rq_ref[0], rq_ref[1], rq_ref[2]
    ak, bmk, bpk = rk_ref[0], rk_ref[1], rk_ref[2]
    n_chunks = ATTN_WIDTH // LANES
    q_chunks, k_chunks, v_chunks = [], [], []
    for c in range(n_chunks):
        sl = slice(c * LANES, (c + 1) * LANES)
        q_chunks.append(_rope_chunk(zq[:, sl], aq, bmq, bpq))
        k_chunks.append(_rope_chunk(zk[:, sl], ak, bmk, bpk))
        v_chunks.append(zv[:, sl])

    for which, chunks in enumerate((q_chunks, k_chunks, v_chunks)):
        for g in range(N_DIL):
            out_ref = qkv_refs[3 * g + which]
            dil = DIL_GROUPS[g][1]
            for cc in range(CHUNKS_PER_GROUP):
                c = g * CHUNKS_PER_GROUP + cc
                sl = slice(cc * LANES, (cc + 1) * LANES)
                if dil == 1:
                    out_ref[0, :, sl] = chunks[c].astype(BF16)
                else:
                    d_ref[c] = chunks[c]
                    for r in range(dil):
                        out_ref[r, :, sl] = d_ref[c, pl.ds(r, tm // dil, stride=dil), :].astype(BF16)

    for g in range(N_DIL):
        kst, vst = st_refs[2 * g], st_refs[2 * g + 1]
        win = min(DIL_GROUPS[g][0], seq)
        if win >= tm:
            cond, r0 = i >= (seq - win) // tm, 0
        else:
            cond, r0 = i == n_tiles - 1, tm - win

        @pl.when(cond)
        def _(g=g, kst=kst, vst=vst, r0=r0):
            for cc in range(CHUNKS_PER_GROUP):
                c = g * CHUNKS_PER_GROUP + cc
                kst[cc * LANES:(cc + 1) * LANES, :] = k_chunks[c][r0:, :].T
                vst[cc * LANES:(cc + 1) * LANES, :] = v_chunks[c][r0:, :].T


def _qkvproj_prompt(x, mod4, g1, wqkv_bf, rope_q, rope_k, *, tm):
    bsz, seq, _ = x.shape
    n_tiles = seq // tm
    const2 = lambda b, i: (0, 0)
    tile3 = lambda b, i: (b, i, 0)
    in_specs = [
        pl.BlockSpec((None, tm, D_MODEL), tile3),
        pl.BlockSpec((None, 6, 1, D_MODEL), lambda b, i: (b, 0, 0, 0)),
        pl.BlockSpec((1, D_MODEL), const2),
        pl.BlockSpec(wqkv_bf.shape, const2, pipeline_mode=pl.Buffered(1)),
        pl.BlockSpec((3, tm, LANES), lambda b, i: (0, i, 0)),
        pl.BlockSpec((3, tm, LANES), lambda b, i: (0, i, 0)),
    ]
    out_shape, out_specs = [], []
    for _, dil in DIL_GROUPS:
        assert tm % (dil * 16) == 0
        for _ in range(3):
            out_shape.append(jax.ShapeDtypeStruct((bsz, dil, seq // dil, GROUP_WIDTH), BF16))
            out_specs.append(pl.BlockSpec((None, dil, tm // dil, GROUP_WIDTH), lambda b, i: (b, 0, i, 0)))
    for win, _ in DIL_GROUPS:
        win = min(win, seq)
        cols = min(win, tm)
        if win >= tm:
            imap = lambda b, i, ft=(seq - win) // tm: (b, 0, jnp.maximum(i - ft, 0))
        else:
            imap = lambda b, i: (b, 0, 0)
        for _ in range(2):
            out_shape.append(jax.ShapeDtypeStruct((bsz, GROUP_WIDTH, win), F32))
            out_specs.append(pl.BlockSpec((None, GROUP_WIDTH, cols), imap))
    return pl.pallas_call(
        functools.partial(_qkvproj_kernel, tm=tm, n_tiles=n_tiles, seq=seq),
        grid=(bsz, n_tiles),
        in_specs=in_specs,
        out_specs=out_specs,
        out_shape=out_shape,
        scratch_shapes=[pltpu.VMEM((ATTN_WIDTH // LANES, tm, LANES), F32)],
        compiler_params=_cparams(("arbitrary", "arbitrary")),
        name="qkvproj_prompt",
    )(x, mod4, g1, wqkv_bf, rope_q, rope_k)


ATTN_UNROLL = 4


def _attn_kernel(*refs, chunk, has_halo, n_cls):
    if has_halo:
        q_ref, k_ref, v_ref, kh_ref, vh_ref, o_ref, lse_ref, kbuf, vbuf = refs
    else:
        q_ref, k_ref, v_ref, o_ref, lse_ref, kbuf, vbuf = refs
    c = pl.program_id(2)
    if has_halo:
        kbuf[:, 0:Q_BLOCK, :] = kh_ref[...]
        vbuf[:, 0:Q_BLOCK, :] = vh_ref[...]
    else:
        kbuf[:, 0:Q_BLOCK, :] = jnp.zeros((n_cls, Q_BLOCK, GROUP_WIDTH), BF16)
        vbuf[:, 0:Q_BLOCK, :] = jnp.zeros((n_cls, Q_BLOCK, GROUP_WIDTH), BF16)
    kbuf[:, Q_BLOCK:Q_BLOCK + chunk, :] = k_ref[...]
    vbuf[:, Q_BLOCK:Q_BLOCK + chunk, :] = v_ref[...]
    blocks_per_cls = chunk // Q_BLOCK

    row = lax.broadcasted_iota(jnp.int32, (Q_BLOCK, 2 * Q_BLOCK), 0)
    col = lax.broadcasted_iota(jnp.int32, (Q_BLOCK, 2 * Q_BLOCK), 1)
    bias_main = jnp.where(col >= row, jnp.where(col <= row + Q_BLOCK, 0.0, NEG_BIG), NEG_BIG)
    bias_first = jnp.where(col >= Q_BLOCK, bias_main, NEG_BIG)
    lane = lax.broadcasted_iota(jnp.int32, (1, LANES), 1)
    lo_half = lane < HEAD_DIM

    def body(u, carry):
        cls, qb = u // blocks_per_cls, u % blocks_per_cls
        r0 = pl.multiple_of(qb * Q_BLOCK, Q_BLOCK)
        kt = kbuf[cls, pl.ds(r0, 2 * Q_BLOCK), :]
        vt = vbuf[cls, pl.ds(r0, 2 * Q_BLOCK), :]
        qt = q_ref[cls, pl.ds(r0, Q_BLOCK), :]
        is_first = jnp.logical_and(qb == 0, c == 0)
        bias = jnp.where(is_first, bias_first, bias_main)
        bias2 = jnp.concatenate([bias, bias], axis=0)
        lse_tile = jnp.zeros((Q_BLOCK, LANES), F32)
        for p in range(CHUNKS_PER_GROUP):
            sl = slice(p * LANES, (p + 1) * LANES)
            qp, kp, vp = qt[:, sl], kt[:, sl], vt[:, sl]
            zero = jnp.zeros_like(qp)
            q2 = jnp.concatenate([jnp.where(lo_half, qp, zero), jnp.where(lo_half, zero, qp)], axis=0)
            s = lax.dot_general(q2, kp, (((1,), (1,)), ((), ())), preferred_element_type=F32) + bias2
            m = jnp.max(s, axis=1, keepdims=True)
            e = jnp.exp(s - m)
            l = jnp.sum(e, axis=1, keepdims=True)
            o = jnp.dot(e.astype(BF16), vp, preferred_element_type=F32) * (1.0 / l)
            lse = m + jnp.log(l)
            for hh in range(2):
                lse_tile = jnp.where(lane == 2 * p + hh, lse[hh * Q_BLOCK:(hh + 1) * Q_BLOCK], lse_tile)
            o_ref[cls, pl.ds(r0, Q_BLOCK), sl] = jnp.where(lo_half, o[:Q_BLOCK], o[Q_BLOCK:])
        lse_ref[cls, pl.ds(r0, Q_BLOCK), :] = lse_tile
        return carry

    n_units = n_cls * blocks_per_cls
    lax.fori_loop(0, n_units, body, 0, unroll=min(ATTN_UNROLL, n_units))


def _attn_prompt_group(q, k, v, g, *, max_rows):
    bsz, dil, cls_len, _ = q.shape
    chunk = min(max_rows, cls_len)
    n_chunks = cls_len // chunk
    n_cls = min(dil, max_rows // chunk)
    has_halo = n_chunks > 1
    main = pl.BlockSpec((None, n_cls, chunk, GROUP_WIDTH), lambda b, r, c: (b, r, c, 0))
    in_specs = [main, main, main]
    args = [q, k, v]
    if has_halo:
        per = chunk // Q_BLOCK
        halo = pl.BlockSpec((None, n_cls, Q_BLOCK, GROUP_WIDTH), lambda b, r, c: (b, r, jnp.maximum(c * per - 1, 0), 0))
        in_specs += [halo, halo]
        args += [k, v]
    return pl.pallas_call(
        functools.partial(_attn_kernel, chunk=chunk, has_halo=has_halo, n_cls=n_cls),
        grid=(bsz, dil // n_cls, n_chunks),
        in_specs=in_specs,
        out_specs=[pl.BlockSpec((None, n_cls, chunk, GROUP_WIDTH), lambda b, r, c: (b, r, c, 0)),
                   pl.BlockSpec((None, n_cls, chunk, LANES), lambda b, r, c: (b, r, c, 0))],
        out_shape=[jax.ShapeDtypeStruct((bsz, dil, cls_len, GROUP_WIDTH), F32),
                   jax.ShapeDtypeStruct((bsz, dil, cls_len, LANES), F32)],
        scratch_shapes=[pltpu.VMEM((n_cls, chunk + Q_BLOCK, GROUP_WIDTH), BF16),
                        pltpu.VMEM((n_cls, chunk + Q_BLOCK, GROUP_WIDTH), BF16)],
        compiler_params=_cparams(("arbitrary", "arbitrary", "arbitrary")),
        name=f"attn_prompt_g{g}",
    )(*args)


def _split_bf16(x, n):
    parts = []
    r = x
    for _ in range(n):
        p = r.astype(BF16)
        parts.append(p)
        r = r - p.astype(F32)
    return parts


def _store_token_rows(ref, value, n_tokens, first_chunk=0, rows_per_token=None):
    n_chunks = value.shape[1] // LANES
    rows_per_token = rows_per_token or n_chunks
    for c in range(n_chunks):
        ref[pl.ds(first_chunk + c, n_tokens, stride=rows_per_token), :] = value[:, c * LANES:(c + 1) * LANES]


def _load_token_rows(ref, n_tokens, n_chunks, first_chunk=0, rows_per_token=None, lead=None):
    rows_per_token = rows_per_token or n_chunks

    def chunk(c):
        rows = pl.ds(first_chunk + c, n_tokens, stride=rows_per_token)
        return ref[rows, :] if lead is None else ref[lead, rows, :]

    return jnp.concatenate([chunk(c) for c in range(n_chunks)], axis=1)


H_ROWS = D_MODEL // LANES
Y_ROWS = 2 * D_MODEL // LANES

_PAIRS = ((0, 1), (0, 2), (0, 3), (1, 2), (1, 3), (2, 3))
N_CLASSES = N_EXPERT_GROUPS * len(_PAIRS)
META_CLASS, META_RANK, META_WA, META_WB = 0, 1, 2, 3


def _post_kernel(*refs, tm, dils, full_precision, routed):
    (o0_ref, o1_ref, o2_ref, l0_ref, l1_ref, l2_ref, pc_ref, sga_ref, x_ref, mod_ref,
     wao_ref, wo_ref, g2_ref, wr_ref, wrh_ref, wrl_ref, br_ref, exp_ref) = refs[:18]
    if routed:
        tri_ref, x1_ref, h2_ref, meta_ref, route_ref, cnt_ref, o_scr, l_scr, run_scr = refs[18:]
    else:
        x1_ref, h2_ref, comb_ref, o_scr, l_scr = refs[18:]

    def mm(a, w_ref):
        if full_precision:
            return jnp.dot(a, w_ref[...], precision=HIGHEST, preferred_element_type=F32)
        return jnp.dot(a.astype(BF16), w_ref[...], preferred_element_type=F32)

    def natural_order(ref, scr, dil, n_chunks):
        if dil == 1:
            return [ref[0, :, c * LANES:(c + 1) * LANES] for c in range(n_chunks)]
        out = []
        for c in range(n_chunks):
            for r in range(dil):
                scr[c, pl.ds(r, tm // dil, stride=dil), :] = ref[r, :, c * LANES:(c + 1) * LANES]
            out.append(scr[c])
        return out

    lses = [natural_order(ref, l_scr.at[g], dils[g], 1)[0] for g, ref in enumerate((l0_ref, l1_ref, l2_ref))]
    mx = jnp.maximum(lses[0], jnp.maximum(lses[1], lses[2]))
    es = [jnp.exp(v - mx) for v in lses]
    inv = 1.0 / (es[0] + es[1] + es[2])
    expand = exp_ref[...]
    attn_o = None
    for g, o_ref in enumerate((o0_ref, o1_ref, o2_ref)):
        w = es[g] * inv
        we = None
        for part in _split_bf16(w, 3 if full_precision else 2):
            t = jnp.dot(part, expand, preferred_element_type=F32)
            we = t if we is None else we + t
        o_nat = jnp.concatenate(natural_order(o_ref, o_scr.at[g], dils[g], CHUNKS_PER_GROUP), axis=1)
        term = we * o_nat
        attn_o = term if attn_o is None else attn_o + term

    y_attn = mm(attn_o, wao_ref)
    mixed = mm(pc_ref[...] + sga_ref[...] * y_attn, wo_ref)
    x1 = x_ref[...] + mod_ref[2] * mixed
    x1_ref[...] = x1
    var = jnp.mean(x1 * x1, axis=-1, keepdims=True)
    h2 = (x1 * lax.rsqrt(var + RMS_EPS)) * (g2_ref[...] * (1.0 + mod_ref[4])) + mod_ref[3]
    if routed:
        _store_token_rows(h2_ref, h2, tm)
    else:
        h2_ref[...] = h2.astype(h2_ref.dtype)

    if full_precision:
        lg = jnp.dot(h2, wr_ref[...], precision=HIGHEST, preferred_element_type=F32) + br_ref[...]
    else:
        h_hi, h_lo = _split_bf16(h2, 2)
        lg = (jnp.dot(h_hi, wrh_ref[...], preferred_element_type=F32)
              + jnp.dot(h_lo, wrh_ref[...], preferred_element_type=F32)
              + jnp.dot(h_hi, wrl_ref[...], preferred_element_type=F32)) + br_ref[...]
    lane_i = lax.broadcasted_iota(jnp.int32, (1, LANES), 1)
    lane = lane_i.astype(F32)
    lane_group = ((lane_i - N_EXPERT_GROUPS) >> 2).astype(F32)
    big = jnp.float32(1e9)
    gl = jnp.where(lane_i < N_EXPERT_GROUPS, lg, NEG_BIG)
    gmax = jnp.max(gl, axis=1, keepdims=True)
    gidx = jnp.min(jnp.where(gl == gmax, lane, big), axis=1, keepdims=True)
    g_w = 1.0 / jnp.sum(jnp.exp(gl - gmax), axis=1, keepdims=True)
    el = jnp.where(lane_group == gidx, lg, NEG_BIG)
    v1 = jnp.max(el, axis=1, keepdims=True)
    i1 = jnp.min(jnp.where(el == v1, lane, big), axis=1, keepdims=True)
    el2 = jnp.where(lane == i1, NEG_BIG, el)
    v2 = jnp.max(el2, axis=1, keepdims=True)
    i2 = jnp.min(jnp.where(el2 == v2, lane, big), axis=1, keepdims=True)
    t = jnp.exp(v2 - v1)
    den = 1.0 / (1.0 + t)
    w_top1, w_top2 = g_w * den, g_w * (t * den)
    if not routed:
        comb_ref[...] = jnp.where(lane == i1, w_top1, jnp.where(lane == i2, w_top2, 0.0))
        return

    base = jnp.float32(N_EXPERT_GROUPS) + jnp.float32(EXPERTS_PER_GROUP) * gidx
    e1, e2 = i1 - base, i2 - base
    first_is_lower = e1 < e2
    ea = jnp.where(first_is_lower, e1, e2)
    eb = jnp.where(first_is_lower, e2, e1)
    pair = ea * (7.0 - ea) * 0.5 + (eb - ea - 1.0)
    cls = gidx * jnp.float32(len(_PAIRS)) + pair
    onehot = lane == cls
    earlier = jnp.dot(tri_ref[...], jnp.where(onehot, 1.0, 0.0).astype(BF16), preferred_element_type=F32)

    @pl.when(jnp.logical_and(pl.program_id(0) == 0, pl.program_id(1) == 0))
    def _():
        run_scr[...] = jnp.zeros_like(run_scr)

    running = run_scr[...]
    rank = jnp.sum(jnp.where(onehot, earlier + running, 0.0), axis=1, keepdims=True)
    running = running + jnp.sum(jnp.where(onehot, 1.0, 0.0), axis=0, keepdims=True)
    run_scr[...] = running
    cnt_ref[...] = running
    w_a = jnp.where(first_is_lower, w_top1, w_top2)
    w_b = jnp.where(first_is_lower, w_top2, w_top1)
    meta = jnp.where(lane_i == META_CLASS, cls,
                     jnp.where(lane_i == META_RANK, rank,
                               jnp.where(lane_i == META_WA, w_a, jnp.where(lane_i == META_WB, w_b, 0.0))))
    meta_ref[...] = meta
    route_ref[...] = meta.T[0:SUBLANES, :].astype(jnp.int32)


def _post(o_list, lse_list, pc, sga, x, mod4, wao, wo, g2, wr, br, expand, *, tm, full_precision, routed):
    nb, rows, _ = x.shape
    tiles = rows // tm
    dils = tuple(o.shape[1] for o in o_list)
    tile3 = lambda b, i: (b, i, 0)
    const2 = lambda b, i: (0, 0)
    mod_rows = mod4.shape[2]
    if mod_rows == 1:
        mod_spec = pl.BlockSpec((None, 6, 1, D_MODEL), lambda b, i: (b, 0, 0, 0))
    else:
        mod_spec = pl.BlockSpec((None, 6, tm, D_MODEL), lambda b, i: (b, 0, i, 0))
    cls4 = lambda b, i: (b, 0, i, 0)
    in_specs = (
        [pl.BlockSpec((None, d, tm // d, GROUP_WIDTH), cls4) for d in dils]
        + [pl.BlockSpec((None, d, tm // d, LANES), cls4) for d in dils]
        + [pl.BlockSpec((None, tm, D_MODEL), tile3)] * 3 + [mod_spec]
        + [pl.BlockSpec(wao.shape, const2), pl.BlockSpec(wo.shape, const2), pl.BlockSpec((1, D_MODEL), const2)]
        + [pl.BlockSpec((D_MODEL, LANES), const2)] * 3
        + [pl.BlockSpec((1, LANES), const2), pl.BlockSpec((LANES, GROUP_WIDTH), const2)]
    )
    wr_hi = wr.astype(BF16)
    wr_lo = (wr - wr_hi.astype(F32)).astype(BF16)
    args = [*o_list, *lse_list, pc, sga, x, mod4, wao, wo, g2, wr, wr_hi, wr_lo, br, expand]
    scratch = [pltpu.VMEM((N_DIL, CHUNKS_PER_GROUP, tm, LANES), F32), pltpu.VMEM((N_DIL, 1, tm, LANES), F32)]
    if routed:
        tri = jnp.asarray(np.tril(np.ones((tm, tm), np.float32), -1), dtype=BF16)
        args.append(tri)
        in_specs = in_specs + [pl.BlockSpec((tm, tm), const2)]
        out_specs = [pl.BlockSpec((None, tm, D_MODEL), tile3), pl.BlockSpec((None, tm * H_ROWS, LANES), tile3),
                     pl.BlockSpec((None, tm, LANES), tile3),
                     pl.BlockSpec((None, None, SUBLANES, tm), lambda b, i: (b, i, 0, 0)),
                     pl.BlockSpec((1, LANES), const2)]
        out_shape = [jax.ShapeDtypeStruct((nb, rows, D_MODEL), F32),
                     jax.ShapeDtypeStruct((nb, rows * H_ROWS, LANES), F32),
                     jax.ShapeDtypeStruct((nb, rows, LANES), F32),
                     jax.ShapeDtypeStruct((nb, tiles, SUBLANES, tm), jnp.int32),
                     jax.ShapeDtypeStruct((1, LANES), F32)]
        scratch.append(pltpu.VMEM((1, LANES), F32))
    else:
        out_specs = [pl.BlockSpec((None, tm, D_MODEL), tile3), pl.BlockSpec((None, tm, D_MODEL), tile3),
                     pl.BlockSpec((None, tm, LANES), tile3)]
        out_shape = [jax.ShapeDtypeStruct((nb, rows, D_MODEL), F32), jax.ShapeDtypeStruct((nb, rows, D_MODEL), F32),
                     jax.ShapeDtypeStruct((nb, rows, LANES), F32)]
    return pl.pallas_call(
        functools.partial(_post_kernel, tm=tm, dils=dils, full_precision=full_precision, routed=routed),
        grid=(nb, tiles),
        in_specs=in_specs,
        out_specs=out_specs,
        out_shape=out_shape,
        scratch_shapes=scratch,
        compiler_params=_cparams(("arbitrary", "arbitrary")),
        name="post_sample" if full_precision else "post_prompt",
    )(*args)


DMA_LOOP_UNROLL = 8


ROUTE_CLASS_ROW, ROUTE_RANK_ROW = META_CLASS, META_RANK


def _sorted_slot(starts_ref, route_ref, r):
    return starts_ref[route_ref[ROUTE_CLASS_ROW, r]] + route_ref[ROUTE_RANK_ROW, r]


def _token_rows(token, rows_per_token):
    return pl.ds(pl.multiple_of(token * rows_per_token, rows_per_token), rows_per_token)


def _dispatch_kernel(starts_ref, route_ref, src_ref, dst_hbm, buf, sems, *, rows_per_step, n_steps):
    i = pl.program_id(0)
    slot = i % 2

    def wait_slot(s):
        def body(r, carry):
            pltpu.make_async_copy(buf.at[s, _token_rows(0, H_ROWS)], dst_hbm.at[_token_rows(0, H_ROWS)],
                                  sems.at[s * rows_per_step + r]).wait()
            return carry
        lax.fori_loop(0, rows_per_step, body, 0, unroll=DMA_LOOP_UNROLL)

    @pl.when(i >= 1)
    def _():
        wait_slot(1 - slot)

    buf[slot] = src_ref[...]

    def issue(r, carry):
        pltpu.make_async_copy(buf.at[slot, _token_rows(r, H_ROWS)],
                              dst_hbm.at[_token_rows(_sorted_slot(starts_ref, route_ref, r), H_ROWS)],
                              sems.at[slot * rows_per_step + r]).start()
        return carry

    lax.fori_loop(0, rows_per_step, issue, 0, unroll=DMA_LOOP_UNROLL)

    @pl.when(i == n_steps - 1)
    def _():
        wait_slot(slot)


def _route_spec(rows_per_step, route_tile, shift=0, n_steps=None):
    per_tile = route_tile // rows_per_step

    def imap(i, starts):
        s = i + shift if n_steps is None else jnp.minimum(i + shift, n_steps - 1)
        return (s // per_tile, 0, s % per_tile)

    return pl.BlockSpec((None, SUBLANES, rows_per_step), imap, memory_space=pltpu.SMEM)


def _dispatch_rows(src, route, starts, *, rows_per_step):
    n = src.shape[0] // H_ROWS
    n_steps = n // rows_per_step
    grid_spec = pltpu.PrefetchScalarGridSpec(
        num_scalar_prefetch=1,
        grid=(n_steps,),
        in_specs=[_route_spec(rows_per_step, route.shape[2]),
                  pl.BlockSpec((rows_per_step * H_ROWS, LANES), lambda i, starts: (i, 0))],
        out_specs=pl.BlockSpec(memory_space=pl.ANY),
        scratch_shapes=[pltpu.VMEM((2, rows_per_step * H_ROWS, LANES), src.dtype),
                        pltpu.SemaphoreType.DMA((2 * rows_per_step,))],
    )
    return pl.pallas_call(
        functools.partial(_dispatch_kernel, rows_per_step=rows_per_step, n_steps=n_steps),
        grid_spec=grid_spec,
        out_shape=jax.ShapeDtypeStruct(src.shape, src.dtype),
        compiler_params=_cparams(("arbitrary",)),
        name="moe_dispatch",
    )(starts, route, src)


def _experts_kernel(tile_ref, ea_ref, eb_ref, lo_ref, hi_ref, first_ref, nv_ref,
                    x_ref, w1a_ref, w1b_ref, w3a_ref, w3b_ref, w2a_ref, w2b_ref, y_ref, *, tm):
    i = pl.program_id(0)

    @pl.when(i < nv_ref[0])
    def _():
        x = _load_token_rows(x_ref, tm, H_ROWS).astype(BF16)

        def expert(w1_ref, w3_ref, w2_ref):
            a = jnp.dot(x, w1_ref[...], preferred_element_type=F32)
            b = jnp.dot(x, w3_ref[...], preferred_element_type=F32)
            hid = (a * _sigmoid(a)) * b
            return jnp.dot(hid.astype(BF16), w2_ref[...], preferred_element_type=F32)

        y_both = jnp.concatenate([expert(w1a_ref, w3a_ref, w2a_ref), expert(w1b_ref, w3b_ref, w2b_ref)], axis=1)

        @pl.when(first_ref[i] == 1)
        def _():
            _store_token_rows(y_ref, y_both, tm)

        @pl.when(first_ref[i] == 0)
        def _():
            row = lax.broadcasted_iota(jnp.int32, (tm, 1), 0)
            old = _load_token_rows(y_ref, tm, Y_ROWS)
            _store_token_rows(y_ref, jnp.where(row >= lo_ref[i], jnp.where(row < hi_ref[i], y_both, old), old), tm)


def _experts_routed(hs, tables, w1_bf, w3_bf, w2_bf, *, tm):
    n = hs.shape[0] // H_ROWS
    n_items = tables[0].shape[0]
    x_map = lambda i, tile, ea, eb, lo, hi, first, nv: (tile[i], 0)
    wa_map = lambda i, tile, ea, eb, lo, hi, first, nv: (ea[i], 0, 0)
    wb_map = lambda i, tile, ea, eb, lo, hi, first, nv: (eb[i], 0, 0)
    w13 = lambda m: pl.BlockSpec((None, D_MODEL, D_EXPERT), m)
    w2s = lambda m: pl.BlockSpec((None, D_EXPERT, D_MODEL), m)
    grid_spec = pltpu.PrefetchScalarGridSpec(
        num_scalar_prefetch=7,
        grid=(n_items,),
        in_specs=[pl.BlockSpec((tm * H_ROWS, LANES), x_map),
                  w13(wa_map), w13(wb_map), w13(wa_map), w13(wb_map), w2s(wa_map), w2s(wb_map)],
        out_specs=pl.BlockSpec((tm * Y_ROWS, LANES), x_map),
    )
    return pl.pallas_call(
        functools.partial(_experts_kernel, tm=tm),
        grid_spec=grid_spec,
        out_shape=jax.ShapeDtypeStruct((n * Y_ROWS, LANES), F32),
        compiler_params=_cparams(("arbitrary",)),
        name="moe_experts",
    )(*tables, hs, w1_bf, w1_bf, w3_bf, w3_bf, w2_bf, w2_bf)


def _combine_kernel(starts_ref, route_ref, route_next_ref, yp_hbm, x1_ref, meta_ref, mod_ref, gf_ref, y_ref, ybuf, sems,
                    *, rows_per_step, n_steps):
    i = pl.program_id(0)
    slot = i % 2

    def issue(r_ref, s):
        def body(r, carry):
            pltpu.make_async_copy(yp_hbm.at[_token_rows(_sorted_slot(starts_ref, r_ref, r), Y_ROWS)],
                                  ybuf.at[s, _token_rows(r, Y_ROWS)], sems.at[s * rows_per_step + r]).start()
            return carry
        lax.fori_loop(0, rows_per_step, body, 0, unroll=DMA_LOOP_UNROLL)

    @pl.when(i == 0)
    def _():
        issue(route_ref, 0)

    @pl.when(i + 1 < n_steps)
    def _():
        issue(route_next_ref, 1 - slot)

    def wait_body(r, carry):
        pltpu.make_async_copy(yp_hbm.at[_token_rows(0, Y_ROWS)], ybuf.at[slot, _token_rows(r, Y_ROWS)],
                              sems.at[slot * rows_per_step + r]).wait()
        return carry

    lax.fori_loop(0, rows_per_step, wait_body, 0, unroll=DMA_LOOP_UNROLL)
    ya = _load_token_rows(ybuf, rows_per_step, H_ROWS, first_chunk=0, rows_per_token=Y_ROWS, lead=slot)
    yb = _load_token_rows(ybuf, rows_per_step, H_ROWS, first_chunk=H_ROWS, rows_per_token=Y_ROWS, lead=slot)
    meta = meta_ref[...]
    lane = lax.broadcasted_iota(jnp.int32, (1, LANES), 1)
    w_a = jnp.sum(jnp.where(lane == META_WA, meta, 0.0), axis=1, keepdims=True)
    w_b = jnp.sum(jnp.where(lane == META_WB, meta, 0.0), axis=1, keepdims=True)
    x2 = x1_ref[...] + mod_ref[5] * (w_a * ya + w_b * yb)
    var = jnp.mean(x2 * x2, axis=-1, keepdims=True)
    y_ref[...] = (x2 * lax.rsqrt(var + RMS_EPS)) * gf_ref[...]


def _combine_final(yp, route, starts, x1, meta, mod4, gf, *, rows_per_step, rows_per_mod):
    n = x1.shape[0]
    n_steps = n // rows_per_step
    steps_per_mod = rows_per_mod // rows_per_step
    row = lambda i, starts: (i, 0)
    grid_spec = pltpu.PrefetchScalarGridSpec(
        num_scalar_prefetch=1,
        grid=(n_steps,),
        in_specs=[_route_spec(rows_per_step, route.shape[2]),
                  _route_spec(rows_per_step, route.shape[2], shift=1, n_steps=n_steps),
                  pl.BlockSpec(memory_space=pl.ANY),
                  pl.BlockSpec((rows_per_step, D_MODEL), row),
                  pl.BlockSpec((rows_per_step, LANES), row),
                  pl.BlockSpec((None, 6, 1, D_MODEL), lambda i, starts: (i // steps_per_mod, 0, 0, 0)),
                  pl.BlockSpec((1, D_MODEL), lambda i, starts: (0, 0))],
        out_specs=pl.BlockSpec((rows_per_step, D_MODEL), row),
        scratch_shapes=[pltpu.VMEM((2, rows_per_step * Y_ROWS, LANES), F32),
                        pltpu.SemaphoreType.DMA((2 * rows_per_step,))],
    )
    return pl.pallas_call(
        functools.partial(_combine_kernel, rows_per_step=rows_per_step, n_steps=n_steps),
        grid_spec=grid_spec,
        out_shape=jax.ShapeDtypeStruct((n, D_MODEL), F32),
        compiler_params=_cparams(("arbitrary",)),
        name="moe_combine",
    )(starts, route, route, yp, x1, meta, mod4, gf)


def _routing_tables(counts, *, n, tm):
    counts = counts.astype(jnp.int32)
    ends = jnp.cumsum(counts)
    starts = ends - counts
    n_items = n // tm + N_CLASSES
    first_tile = starts // tm
    last_tile = jnp.maximum(ends - 1, starts) // tm
    visits = jnp.where(counts > 0, last_tile - first_tile + 1, 0)
    item_end = jnp.cumsum(visits)
    item_start = item_end - visits
    n_valid = item_end[-1]
    item = jnp.arange(n_items, dtype=jnp.int32)
    idx = jnp.minimum(item, n_valid - 1)
    c = jnp.sum((idx[:, None] >= item_end[None, :]).astype(jnp.int32), axis=1)
    class_ids = jnp.arange(N_CLASSES, dtype=jnp.int32)

    def pick(table):
        return jnp.sum(jnp.where(c[:, None] == class_ids[None, :], table[None, :], 0), axis=1)

    tile = pick(first_tile) + (idx - pick(item_start))
    live = item < n_valid
    lo = jnp.where(live, jnp.clip(pick(starts) - tile * tm, 0, tm), 0)
    hi = jnp.where(live, jnp.clip(pick(ends) - tile * tm, 0, tm), 0)
    prev_tile = jnp.concatenate([jnp.full((1,), -1, jnp.int32), tile[:-1]])
    first = (tile != prev_tile).astype(jnp.int32)
    group_of_class = np.arange(N_CLASSES) // len(_PAIRS)
    pair_of_class = np.arange(N_CLASSES) % len(_PAIRS)
    pairs = np.asarray(_PAIRS)
    ea = pick(jnp.asarray(group_of_class * EXPERTS_PER_GROUP + pairs[pair_of_class, 0], dtype=jnp.int32))
    eb = pick(jnp.asarray(group_of_class * EXPERTS_PER_GROUP + pairs[pair_of_class, 1], dtype=jnp.int32))
    as_i32 = lambda a: a.astype(jnp.int32)
    return as_i32(starts), tuple(as_i32(a) for a in (tile, ea, eb, lo, hi, first, n_valid.reshape(1)))


def _moe_kernel(h_ref, comb_ref, x1_ref, mod_ref, gf_ref, w1_ref, w3_ref, w2_ref, y_ref, acc_ref):
    e = pl.program_id(2)

    @pl.when(e == 0)
    def _():
        acc_ref[...] = jnp.zeros_like(acc_ref)

    h = h_ref[...].astype(BF16)
    a = jnp.dot(h, w1_ref[...], preferred_element_type=F32)
    b = jnp.dot(h, w3_ref[...], preferred_element_type=F32)
    lane = lax.broadcasted_iota(jnp.int32, (1, LANES), 1)
    cw = jnp.sum(jnp.where(lane == e + N_EXPERT_GROUPS, comb_ref[...], 0.0), axis=1, keepdims=True)
    hid = (a * _sigmoid(a)) * b * cw
    acc_ref[...] += jnp.dot(hid.astype(BF16), w2_ref[...], preferred_element_type=F32)

    @pl.when(e == N_EXPERTS - 1)
    def _():
        x2 = x1_ref[...] + mod_ref[5] * acc_ref[...]
        var = jnp.mean(x2 * x2, axis=-1, keepdims=True)
        y_ref[...] = (x2 * lax.rsqrt(var + RMS_EPS)) * gf_ref[...]


def _moe_dense(h2, comb, x1, mod4, gf, w1_bf, w3_bf, w2_bf, *, tm):
    nb, rows, _ = x1.shape
    tiles = rows // tm
    tile3 = lambda b, i, e: (b, i, 0)
    if mod4.shape[2] == 1:
        mod_spec = pl.BlockSpec((None, 6, 1, D_MODEL), lambda b, i, e: (b, 0, 0, 0))
    else:
        mod_spec = pl.BlockSpec((None, 6, tm, D_MODEL), lambda b, i, e: (b, 0, i, 0))
    return pl.pallas_call(
        _moe_kernel,
        grid=(nb, tiles, N_EXPERTS),
        in_specs=[
            pl.BlockSpec((None, tm, D_MODEL), tile3),
            pl.BlockSpec((None, tm, LANES), tile3),
            pl.BlockSpec((None, tm, D_MODEL), tile3),
            mod_spec,
            pl.BlockSpec((1, D_MODEL), lambda b, i, e: (0, 0)),
            pl.BlockSpec((None, D_MODEL, D_EXPERT), lambda b, i, e: (e, 0, 0)),
            pl.BlockSpec((None, D_MODEL, D_EXPERT), lambda b, i, e: (e, 0, 0)),
            pl.BlockSpec((None, D_EXPERT, D_MODEL), lambda b, i, e: (e, 0, 0)),
        ],
        out_specs=pl.BlockSpec((None, tm, D_MODEL), tile3),
        out_shape=jax.ShapeDtypeStruct((nb, rows, D_MODEL), F32),
        scratch_shapes=[pltpu.VMEM((tm, D_MODEL), F32)],
        compiler_params=_cparams(("arbitrary", "arbitrary", "arbitrary")),
        name=f"moe_dense_{nb * rows}",
    )(h2, comb, x1, mod4, gf, w1_bf, w3_bf, w2_bf)


def _s_inproj_kernel(x_ref, mod_ref, g1_ref, w_ref, z_ref):
    x = x_ref[...]
    var = jnp.mean(x * x, axis=-1, keepdims=True)
    h = (x * lax.rsqrt(var + RMS_EPS)) * (g1_ref[...] * (1.0 + mod_ref[1])) + mod_ref[0]
    z_ref[...] = jnp.dot(h, w_ref[...], precision=HIGHEST, preferred_element_type=F32)


def _s_inproj(x, mod_tok, g1, w_in, *, col_block):
    n = x.shape[0]
    return pl.pallas_call(
        _s_inproj_kernel,
        grid=(IN_COLS // col_block,),
        in_specs=[pl.BlockSpec((n, D_MODEL), lambda j: (0, 0)),
                  pl.BlockSpec((6, n, D_MODEL), lambda j: (0, 0, 0)),
                  pl.BlockSpec((1, D_MODEL), lambda j: (0, 0)),
                  pl.BlockSpec((D_MODEL, col_block), lambda j: (0, j))],
        out_specs=pl.BlockSpec((n, col_block), lambda j: (0, j)),
        out_shape=jax.ShapeDtypeStruct((n, IN_COLS), F32),
        compiler_params=_cparams(("arbitrary",)),
        name="inproj_sample",
    )(x, mod_tok, g1, w_in)


def _s_mid_kernel(z_ref, p0_ref, p1_ref, cw_ref, wco_ref, rq_ref, rk_ref,
                  cu_ref, pc_ref, sga_ref, q_ref, k_ref, v_ref, *, t_len):
    n = z_ref.shape[0]
    cu = z_ref[:, OFF_GC:OFF_GC + D_MODEL] * z_ref[:, OFF_U:OFF_U + D_MODEL]
    cu_ref[...] = cu
    t = lax.broadcasted_iota(jnp.int32, (n, D_MODEL), 0) & (t_len - 1)
    prev1 = jnp.where(t >= 1, pltpu.roll(cu, 1, 0), p1_ref[...])
    prev2 = jnp.where(t >= 2, pltpu.roll(cu, 2, 0), jnp.where(t == 0, p0_ref[...], p1_ref[...]))
    cw = cw_ref[...]
    conv = cw[0:1] * prev2 + cw[1:2] * prev1 + cw[2:3] * cu
    yc = jnp.dot(z_ref[:, OFF_GB:OFF_GB + D_MODEL] * conv, wco_ref[...], precision=HIGHEST,
                 preferred_element_type=F32)
    pc_ref[...] = _sigmoid(z_ref[:, OFF_GCONV:OFF_GCONV + D_MODEL]) * yc
    sga_ref[...] = _sigmoid(z_ref[:, OFF_GATTN:OFF_GATTN + D_MODEL])
    aq, bmq, bpq = rq_ref[0], rq_ref[1], rq_ref[2]
    ak, bmk, bpk = rk_ref[0], rk_ref[1], rk_ref[2]
    for c in range(ATTN_WIDTH // LANES):
        sl = slice(c * LANES, (c + 1) * LANES)
        q_ref[:, sl] = _rope_chunk(z_ref[:, OFF_Q + c * LANES:OFF_Q + (c + 1) * LANES], aq, bmq, bpq)
        k_ref[:, sl] = _rope_chunk(z_ref[:, OFF_K + c * LANES:OFF_K + (c + 1) * LANES], ak, bmk, bpk)
    v_ref[...] = z_ref[:, OFF_V:OFF_V + ATTN_WIDTH]


def _s_mid(z, p0e, p1e, conv_w, w_conv_out, rope_q, rope_k, *, t_len):
    n = z.shape[0]
    assert t_len & (t_len - 1) == 0
    full = lambda shape: pl.BlockSpec(shape, lambda i: (0,) * len(shape))
    out_shape = [jax.ShapeDtypeStruct((n, D_MODEL), F32)] * 3 + [jax.ShapeDtypeStruct((n, ATTN_WIDTH), F32)] * 3
    return pl.pallas_call(
        functools.partial(_s_mid_kernel, t_len=t_len),
        grid=(1,),
        in_specs=[full(z.shape), full(p0e.shape), full(p1e.shape), full(conv_w.shape), full(w_conv_out.shape),
                  full(rope_q.shape), full(rope_k.shape)],
        out_specs=[full((n, D_MODEL))] * 3 + [full((n, ATTN_WIDTH))] * 3,
        out_shape=out_shape,
        compiler_params=_cparams(("arbitrary",)),
        name="mid_sample",
    )(z, p0e, p1e, conv_w, w_conv_out, rope_q, rope_k)


def _head_sum(x):
    return jnp.sum(x.reshape(HEADS_PER_GROUP, HEAD_DIM, x.shape[-1]), axis=1)


def _head_expand(x):
    n = x.shape[-1]
    return jnp.broadcast_to(x[:, None, :], (HEADS_PER_GROUP, HEAD_DIM, n)).reshape(GROUP_WIDTH, n)


def _s_attn_disjoint(q_ref, kn, vn, ck_ref, cv_ref, o_ref, lse_ref, s_scr, lane, new_idx, *, n_tiles, dil, t_len):
    cls = lane & (dil - 1)
    q_all = q_ref[...]
    qsel = jnp.zeros((GROUP_WIDTH, LANES), F32)
    for t in range(t_len):
        qsel = jnp.where(cls == t, jnp.broadcast_to(q_all[:, t:t + 1], (GROUP_WIDTH, LANES)), qsel)
    smax = None
    for j in range(n_tiles):
        s = jnp.where(cls < t_len, _head_sum(ck_ref[:, j * LANES:(j + 1) * LANES] * qsel), NEG_BIG)
        s_scr[:, j * LANES:(j + 1) * LANES] = s
        smax = s if smax is None else jnp.maximum(smax, s)
    s_new = _head_sum(kn * q_all)
    m_cols = []
    m_lane = jnp.zeros((HEADS_PER_GROUP, LANES), F32)
    m_new = jnp.zeros((HEADS_PER_GROUP, t_len), F32)
    for t in range(t_len):
        mt = jnp.maximum(jnp.max(jnp.where(cls == t, smax, NEG_BIG), axis=1, keepdims=True), s_new[:, t:t + 1])
        m_cols.append(mt)
        m_lane = jnp.where(cls == t, mt, m_lane)
        m_new = jnp.where(new_idx == t, mt, m_new)
    esum = jnp.zeros((HEADS_PER_GROUP, LANES), F32)
    for j in range(n_tiles):
        e = jnp.exp(s_scr[:, j * LANES:(j + 1) * LANES] - m_lane)
        s_scr[:, j * LANES:(j + 1) * LANES] = e
        esum = esum + e
    e_new = jnp.exp(s_new - m_new)
    l_cols = []
    inv_lane = jnp.zeros((HEADS_PER_GROUP, LANES), F32)
    inv_new = jnp.zeros((HEADS_PER_GROUP, t_len), F32)
    for t in range(t_len):
        lt = jnp.sum(jnp.where(cls == t, esum, 0.0), axis=1, keepdims=True) + e_new[:, t:t + 1]
        l_cols.append(lt)
        inv_lane = jnp.where(cls == t, 1.0 / lt, inv_lane)
        inv_new = jnp.where(new_idx == t, 1.0 / lt, inv_new)
    acc = None
    for j in range(n_tiles):
        term = cv_ref[:, j * LANES:(j + 1) * LANES] * _head_expand(s_scr[:, j * LANES:(j + 1) * LANES] * inv_lane)
        acc = term if acc is None else acc + term
    o_new = vn * _head_expand(e_new * inv_new)
    for t in range(t_len):
        o_ref[:, t:t + 1] = jnp.sum(jnp.where(cls == t, acc, 0.0), axis=1, keepdims=True) + o_new[:, t:t + 1]
        lse_ref[:, t:t + 1] = m_cols[t] + jnp.log(l_cols[t])


def _s_attn_kernel(q_ref, kn_ref, vn_ref, kt_ref, vt_ref, ck_ref, cv_ref, o_ref, lse_ref, ko_ref, vo_ref, s_scr,
                   *, win, dil, t_len):
    n_tiles = win // LANES
    lane = lax.broadcasted_iota(jnp.int32, (1, LANES), 1)
    new_idx = lax.broadcasted_iota(jnp.int32, (1, t_len), 1)
    kn = kn_ref[...]
    vn = vn_ref[...]
    if dil >= t_len:
        _s_attn_disjoint(q_ref, kn, vn, ck_ref, cv_ref, o_ref, lse_ref, s_scr, lane, new_idx,
                         n_tiles=n_tiles, dil=dil, t_len=t_len)
    for t in range(t_len if dil < t_len else 0):
        qb = jnp.broadcast_to(q_ref[:, t:t + 1], (GROUP_WIDTH, LANES))
        m = None
        for j in range(n_tiles):
            pos = lane + j * LANES
            s = _head_sum(ck_ref[:, j * LANES:(j + 1) * LANES] * qb)
            s = jnp.where(pos >= t, jnp.where(((pos - t) & (dil - 1)) == 0, s, NEG_BIG), NEG_BIG)
            s_scr[:, j * LANES:(j + 1) * LANES] = s
            mj = jnp.max(s, axis=1, keepdims=True)
            m = mj if m is None else jnp.maximum(m, mj)
        s_new = _head_sum(kn * qb[:, 0:t_len])
        s_new = jnp.where(new_idx <= t, jnp.where(((t - new_idx) & (dil - 1)) == 0, s_new, NEG_BIG), NEG_BIG)
        m = jnp.maximum(m, jnp.max(s_new, axis=1, keepdims=True))
        e_new = jnp.exp(s_new - m)
        l = jnp.sum(e_new, axis=1, keepdims=True)
        acc = None
        for j in range(n_tiles):
            e = jnp.exp(s_scr[:, j * LANES:(j + 1) * LANES] - m)
            l = l + jnp.sum(e, axis=1, keepdims=True)
            term = cv_ref[:, j * LANES:(j + 1) * LANES] * _head_expand(e)
            acc = term if acc is None else acc + term
        o = jnp.sum(acc, axis=1, keepdims=True) + jnp.sum(vn * _head_expand(e_new), axis=1, keepdims=True)
        o_ref[:, t:t + 1] = o * _head_expand(1.0 / l)
        lse_ref[:, t:t + 1] = m + jnp.log(l)

    for c_ref, tail_ref, out_ref in ((ck_ref, kt_ref, ko_ref), (cv_ref, vt_ref, vo_ref)):
        rolled = pltpu.roll(c_ref[...], win - t_len, 1)
        if win > LANES:
            out_ref[:, 0:win - LANES] = rolled[:, 0:win - LANES]
        out_ref[:, win - LANES:win] = jnp.where(lane >= LANES - t_len, tail_ref[...], rolled[:, win - LANES:win])


def _s_attn_group(q_t, kn_t, vn_t, k_tail, v_tail, cache_k, cache_v, g):
    bsz, _, t_len = q_t.shape
    win_full, dil = DIL_GROUPS[g]
    win = cache_k.shape[2]
    assert win == win_full and win == (KEYS_PER_QUERY - 1) * dil and win % LANES == 0
    assert dil & (dil - 1) == 0 and LANES % dil == 0
    grp = lambda b: (b, g, 0)
    per_b = lambda b: (b, 0, 0)
    return pl.pallas_call(
        functools.partial(_s_attn_kernel, win=win, dil=dil, t_len=t_len),
        grid=(bsz,),
        in_specs=[pl.BlockSpec((None, GROUP_WIDTH, t_len), grp)] * 3
        + [pl.BlockSpec((None, GROUP_WIDTH, LANES), grp)] * 2
        + [pl.BlockSpec((None, GROUP_WIDTH, win), per_b)] * 2,
        out_specs=[pl.BlockSpec((None, GROUP_WIDTH, t_len), per_b),
                   pl.BlockSpec((None, HEADS_PER_GROUP, t_len), per_b),
                   pl.BlockSpec((None, GROUP_WIDTH, win), per_b),
                   pl.BlockSpec((None, GROUP_WIDTH, win), per_b)],
        out_shape=[jax.ShapeDtypeStruct((bsz, GROUP_WIDTH, t_len), F32),
                   jax.ShapeDtypeStruct((bsz, HEADS_PER_GROUP, t_len), F32),
                   jax.ShapeDtypeStruct((bsz, GROUP_WIDTH, win), F32),
                   jax.ShapeDtypeStruct((bsz, GROUP_WIDTH, win), F32)],
        scratch_shapes=[pltpu.VMEM((HEADS_PER_GROUP, win), F32)],
        compiler_params=_cparams(("arbitrary",)),
        name=f"attn_sample_g{g}",
    )(q_t, kn_t, vn_t, k_tail, v_tail, cache_k, cache_v)


TM_INPROJ = 512
TM_POST = 512
TM_EXPERT = 256
ROWS_PER_DMA_STEP = 128
ATTN_ROWS_PER_STEP = 1024
S_COL_BLOCK = 512


def _to_state(a_t):
    b, _, length = a_t.shape
    return jnp.transpose(a_t.reshape(b, HEADS_PER_GROUP, HEAD_DIM, length), (0, 3, 1, 2))[None]


def _from_state(a):
    b, length = a.shape[0], a.shape[1]
    return jnp.transpose(a, (0, 2, 3, 1)).reshape(b, GROUP_WIDTH, length)


def kernel(x_prompt, x_sample, cache_k1, cache_v1, cache_k2, cache_v2, cache_k3, cache_v3, state_conv,
           c_prompt, c_sample, norm1_g, norm2_g, normf_g, w_ada, b_ada, w_in, conv_w, w_conv_out,
           w_attn_out, w_o, w_rg, b_rg, w_re, b_re, w1, w3, w2):
    depth = w_in.shape[0]
    assert depth == 1
    bsz, seq, _ = x_prompt.shape
    dbsz, t_len, _ = x_sample.shape
    n_s = dbsz * t_len
    l = 0

    w_in_bf = w_in[l].astype(BF16)
    wco_bf = w_conv_out[l].astype(BF16)
    wao_bf = w_attn_out[l].astype(BF16)
    wo_bf = w_o[l].astype(BF16)
    w1_bf = w1[l].reshape(N_EXPERTS, D_MODEL, D_EXPERT).astype(BF16)
    w3_bf = w3[l].reshape(N_EXPERTS, D_MODEL, D_EXPERT).astype(BF16)
    w2_bf = w2[l].reshape(N_EXPERTS, D_EXPERT, D_MODEL).astype(BF16)
    g1 = norm1_g[l].reshape(1, D_MODEL)
    g2 = norm2_g[l].reshape(1, D_MODEL)
    gf = normf_g.reshape(1, D_MODEL)
    n_route = N_EXPERT_GROUPS + N_EXPERTS
    wr = jnp.pad(jnp.concatenate([w_rg[l], w_re[l]], axis=1), ((0, 0), (0, LANES - n_route)))
    br = jnp.pad(jnp.concatenate([b_rg[l], b_re[l]]), (0, LANES - n_route)).reshape(1, LANES)
    head_of_lane = np.arange(GROUP_WIDTH) // HEAD_DIM
    expand_bf = jnp.asarray((np.arange(LANES)[:, None] == head_of_lane[None, :]).astype(np.float32), dtype=BF16)

    mod = _adaln(jnp.concatenate([c_prompt, c_sample], axis=0), w_ada[l], b_ada[l])
    mod_p4 = mod[:bsz].reshape(bsz, 6, 1, D_MODEL)
    mod_tok = jnp.repeat(mod[bsz:].reshape(dbsz, 1, 6, D_MODEL), t_len, axis=1)
    mod_tok = jnp.transpose(mod_tok.reshape(n_s, 6, D_MODEL), (1, 0, 2))
    mod_s4 = mod_tok[None]

    pos_p = jnp.arange(seq, dtype=jnp.int32)
    rope_q_p = _rope_tables(pos_p, HEAD_DIM ** -0.5)
    rope_k_p = _rope_tables(pos_p, 1.0)
    pc_p, sga_p, conv_p = _convproj_prompt(x_prompt, mod_p4, g1, w_in_bf[:, :OFF_Q], w_in_bf[:, OFF_GCONV:],
                                           conv_w[l], wco_bf, tm=TM_INPROJ)
    outs = _qkvproj_prompt(x_prompt, mod_p4, g1, w_in_bf[:, OFF_Q:OFF_GCONV], rope_q_p, rope_k_p, tm=TM_INPROJ)
    qkv_p = outs[0:3 * N_DIL]
    states_p = outs[3 * N_DIL:]
    o_p, lse_p = [], []
    for g in range(N_DIL):
        o_g, lse_g = _attn_prompt_group(qkv_p[3 * g], qkv_p[3 * g + 1], qkv_p[3 * g + 2], g, max_rows=ATTN_ROWS_PER_STEP)
        o_p.append(o_g)
        lse_p.append(lse_g)
    x1_p, h2rows_p, meta_p, route_p, cnt_p = _post(o_p, lse_p, pc_p, sga_p, x_prompt, mod_p4, wao_bf, wo_bf, g2, wr, br,
                                                   expand_bf, tm=TM_POST, full_precision=False, routed=True)
    n_p = bsz * seq
    route = route_p.reshape(n_p // TM_POST, SUBLANES, TM_POST)
    starts, tables = _routing_tables(cnt_p[0, :N_CLASSES], n=n_p, tm=TM_EXPERT)
    hs = _dispatch_rows(h2rows_p.reshape(n_p * H_ROWS, LANES), route, starts, rows_per_step=ROWS_PER_DMA_STEP)
    yp = _experts_routed(hs, tables, w1_bf, w3_bf, w2_bf, tm=TM_EXPERT)
    y_p = _combine_final(yp, route, starts, x1_p.reshape(n_p, D_MODEL), meta_p.reshape(n_p, LANES), mod_p4, gf,
                         rows_per_step=ROWS_PER_DMA_STEP, rows_per_mod=seq).reshape(bsz, seq, D_MODEL)

    pos_s = PAST_LEN + jnp.arange(t_len, dtype=jnp.int32)
    rope_q_s = jnp.tile(_rope_tables(pos_s, HEAD_DIM ** -0.5), (1, dbsz, 1))
    rope_k_s = jnp.tile(_rope_tables(pos_s, 1.0), (1, dbsz, 1))
    xs = x_sample.reshape(n_s, D_MODEL)
    z_s = _s_inproj(xs, mod_tok, g1, w_in[l], col_block=S_COL_BLOCK)
    past = state_conv[l]
    p0e = jnp.repeat(past[:, 0], t_len, axis=0)
    p1e = jnp.repeat(past[:, 1], t_len, axis=0)
    cu_s, pc_s, sga_s, q_s, k_s, v_s = _s_mid(z_s, p0e, p1e, conv_w[l], w_conv_out[l], rope_q_s, rope_k_s,
                                              t_len=t_len)
    to_cols = lambda a: jnp.transpose(a.reshape(dbsz, t_len, ATTN_WIDTH), (0, 2, 1))
    q_t, kn_t, vn_t = to_cols(q_s), to_cols(k_s), to_cols(v_s)
    k_tail = jnp.pad(kn_t, ((0, 0), (0, 0), (LANES - t_len, 0)))
    v_tail = jnp.pad(vn_t, ((0, 0), (0, 0), (LANES - t_len, 0)))
    caches = ((cache_k1, cache_v1), (cache_k2, cache_v2), (cache_k3, cache_v3))
    o_s, lse_s, kv_s = [], [], []
    for g, (ck, cv) in enumerate(caches):
        o_g, lse_g, ko, vo = _s_attn_group(q_t, kn_t, vn_t, k_tail, v_tail, _from_state(ck[l]), _from_state(cv[l]), g)
        o_s.append(jnp.transpose(o_g, (0, 2, 1)).reshape(1, 1, n_s, GROUP_WIDTH))
        lse_rows = jnp.transpose(lse_g, (0, 2, 1)).reshape(n_s, HEADS_PER_GROUP)
        lse_s.append(jnp.pad(lse_rows, ((0, 0), (0, LANES - HEADS_PER_GROUP))).reshape(1, 1, n_s, LANES))
        kv_s += [_to_state(ko), _to_state(vo)]
    x1_s, h2_s, comb_s = _post(o_s, lse_s, pc_s[None], sga_s[None], xs[None], mod_s4, w_attn_out[l], w_o[l], g2, wr, br,
                               expand_bf, tm=n_s, full_precision=True, routed=False)
    y_s = _moe_dense(h2_s, comb_s, x1_s, mod_s4, gf, w1_bf, w3_bf, w2_bf, tm=n_s)

    conv_s = cu_s.reshape(dbsz, t_len, D_MODEL)[:, t_len - (CONV_K - 1):]
    return (y_p, y_s.reshape(dbsz, t_len, D_MODEL),
            *[_to_state(a) for a in states_p], conv_p.reshape(1, bsz, CONV_K - 1, D_MODEL),
            *kv_s, conv_s.reshape(1, dbsz, CONV_K - 1, D_MODEL))
```

```python
import functools

import numpy as np
import jax
import jax.numpy as jnp
from jax import lax
from jax.experimental import pallas as pl
from jax.experimental.pallas import tpu as pltpu
from jax.experimental.pallas import tpu_sc as plsc

F32 = jnp.float32
BF16 = jnp.bfloat16
HIGHEST = lax.Precision.HIGHEST

D_MODEL = 1024
HEAD_DIM = 64
HEADS_PER_GROUP = 8
GROUP_WIDTH = HEADS_PER_GROUP * HEAD_DIM
DIL_GROUPS = ((128, 1), (512, 4), (2048, 16))
N_DIL = len(DIL_GROUPS)
ATTN_WIDTH = N_DIL * GROUP_WIDTH
ROT_DIM = HEAD_DIM // 4
ROPE_THETA = 500000.0
PAST_LEN = 16384
CONV_K = 3
N_EXPERT_GROUPS = 4
EXPERTS_PER_GROUP = 4
N_EXPERTS = N_EXPERT_GROUPS * EXPERTS_PER_GROUP
D_EXPERT = 512
RMS_EPS = 1e-6
IN_COLS = 3 * D_MODEL + 3 * ATTN_WIDTH + 2 * D_MODEL
OFF_U, OFF_GC, OFF_GB = 0, D_MODEL, 2 * D_MODEL
OFF_Q = 3 * D_MODEL
OFF_K = OFF_Q + ATTN_WIDTH
OFF_V = OFF_K + ATTN_WIDTH
OFF_GCONV = OFF_V + ATTN_WIDTH
OFF_GATTN = OFF_GCONV + D_MODEL

LANES = 128
SUBLANES = 8
CHUNKS_PER_GROUP = GROUP_WIDTH // LANES
KEYS_PER_QUERY = 129
Q_BLOCK = 128
NEG_BIG = -1e30

VMEM_LIMIT = 56 * 1024 * 1024


def _sigmoid(x):
    return 1.0 / (1.0 + jnp.exp(-x))


def _cparams(sem):
    return pltpu.CompilerParams(dimension_semantics=sem, vmem_limit_bytes=VMEM_LIMIT)


def _adaln_kernel(c_ref, w_ref, b_ref, o_ref):
    c = c_ref[...]
    s = c * _sigmoid(c)
    o_ref[...] = jnp.dot(s, w_ref[...], precision=HIGHEST, preferred_element_type=F32) + b_ref[...]


def _adaln(c_all, w_ada, b_ada):
    rows = c_all.shape[0]
    n_col = w_ada.shape[1] // D_MODEL
    return pl.pallas_call(
        _adaln_kernel,
        grid=(n_col,),
        in_specs=[
            pl.BlockSpec((rows, D_MODEL), lambda j: (0, 0)),
            pl.BlockSpec((D_MODEL, D_MODEL), lambda j: (0, j)),
            pl.BlockSpec((1, D_MODEL), lambda j: (0, j)),
        ],
        out_specs=pl.BlockSpec((rows, D_MODEL), lambda j: (0, j)),
        out_shape=jax.ShapeDtypeStruct((rows, w_ada.shape[1]), F32),
        compiler_params=_cparams(("arbitrary",)),
        name="adaln",
    )(c_all, w_ada, b_ada.reshape(1, -1))


def _rope_tables(pos, scale):
    half = ROT_DIM // 2
    inv_freq = jnp.power(jnp.float32(ROPE_THETA), -jnp.arange(half, dtype=F32) / half)
    ang = pos.astype(F32)[:, None] * inv_freq[None, :]
    cos, sin = jnp.cos(ang), jnp.sin(ang)
    lane_in_head = np.arange(LANES) % HEAD_DIM
    freq = lane_in_head % half
    first = lane_in_head < half
    second = (lane_in_head >= half) & (lane_in_head < ROT_DIM)
    a = jnp.where(first | second, cos[:, freq], 1.0)
    bm = jnp.where(first, -sin[:, freq], 0.0)
    bp = jnp.where(second, sin[:, freq], 0.0)
    return jnp.stack([a, bm, bp]) * scale


def _rope_chunk(zc, a, bm, bp):
    return zc * a + pltpu.roll(zc, LANES - ROT_DIM // 2, 1) * bm + pltpu.roll(zc, ROT_DIM // 2, 1) * bp


def _modulated_norm_bf16(x_ref, mod_ref, g1_ref):
    x = x_ref[...]
    var = jnp.mean(x * x, axis=-1, keepdims=True)
    h = (x * lax.rsqrt(var + RMS_EPS)) * (g1_ref[...] * (1.0 + mod_ref[1])) + mod_ref[0]
    return h.astype(BF16)


def _convproj_kernel(x_ref, mod_ref, g1_ref, wa_ref, wg_ref, cw_ref, wco_ref, pc_ref, sga_ref, cst_ref, s_ref,
                     *, tm, n_tiles):
    i = pl.program_id(1)
    hb = _modulated_norm_bf16(x_ref, mod_ref, g1_ref)

    def proj(w_ref, lo):
        return jnp.dot(hb, w_ref[:, lo:lo + D_MODEL], preferred_element_type=F32)

    cu = proj(wa_ref, OFF_GC) * proj(wa_ref, OFF_U)

    @pl.when(i == 0)
    def _():
        s_ref[0:SUBLANES, :] = jnp.zeros((SUBLANES, D_MODEL), F32)

    s_ref[SUBLANES:SUBLANES + tm, :] = cu
    cw = cw_ref[...]
    conv = (cw[0:1] * s_ref[SUBLANES - 2:SUBLANES - 2 + tm, :]
            + cw[1:2] * s_ref[SUBLANES - 1:SUBLANES - 1 + tm, :]
            + cw[2:3] * cu)
    yc = jnp.dot((proj(wa_ref, OFF_GB) * conv).astype(BF16), wco_ref[...], preferred_element_type=F32)
    pc_ref[...] = (_sigmoid(proj(wg_ref, 0)) * yc).astype(pc_ref.dtype)
    sga_ref[...] = _sigmoid(proj(wg_ref, D_MODEL)).astype(sga_ref.dtype)

    @pl.when(i == n_tiles - 1)
    def _():
        cst_ref[...] = s_ref[tm + SUBLANES - 2:tm + SUBLANES, :]

    s_ref[0:SUBLANES, :] = s_ref[tm:tm + SUBLANES, :]


def _convproj_prompt(x, mod4, g1, wa_bf, wg_bf, conv_w, wco_bf, *, tm):
    bsz, seq, _ = x.shape
    n_tiles = seq // tm
    const2 = lambda b, i: (0, 0)
    tile3 = lambda b, i: (b, i, 0)
    return pl.pallas_call(
        functools.partial(_convproj_kernel, tm=tm, n_tiles=n_tiles),
        grid=(bsz, n_tiles),
        in_specs=[
            pl.BlockSpec((None, tm, D_MODEL), tile3),
            pl.BlockSpec((None, 6, 1, D_MODEL), lambda b, i: (b, 0, 0, 0)),
            pl.BlockSpec((1, D_MODEL), const2),
            pl.BlockSpec(wa_bf.shape, const2, pipeline_mode=pl.Buffered(1)),
            pl.BlockSpec(wg_bf.shape, const2, pipeline_mode=pl.Buffered(1)),
            pl.BlockSpec((CONV_K, D_MODEL), const2),
            pl.BlockSpec((D_MODEL, D_MODEL), const2, pipeline_mode=pl.Buffered(1)),
        ],
        out_specs=[pl.BlockSpec((None, tm, D_MODEL), tile3), pl.BlockSpec((None, tm, D_MODEL), tile3),
                   pl.BlockSpec((None, CONV_K - 1, D_MODEL), lambda b, i: (b, 0, 0))],
        out_shape=[jax.ShapeDtypeStruct((bsz, seq, D_MODEL), BF16), jax.ShapeDtypeStruct((bsz, seq, D_MODEL), BF16),
                   jax.ShapeDtypeStruct((bsz, CONV_K - 1, D_MODEL), F32)],
        scratch_shapes=[pltpu.VMEM((tm + SUBLANES, D_MODEL), F32)],
        compiler_params=_cparams(("arbitrary", "arbitrary")),
        name="convproj_prompt",
    )(x, mod4, g1, wa_bf, wg_bf, conv_w, wco_bf)


def _qkvproj_kernel(x_ref, mod_ref, g1_ref, w_ref, rq_ref, rk_ref, *rest, tm, n_tiles, seq):
    qkv_refs = rest[0:3 * N_DIL]
    st_refs = rest[3 * N_DIL:3 * N_DIL + 2 * N_DIL]
    d_ref = rest[-1]
    i = pl.program_id(1)
    hb = _modulated_norm_bf16(x_ref, mod_ref, g1_ref)

    def proj(lo, width):
        return jnp.dot(hb, w_ref[:, lo:lo + width], preferred_element_type=F32)

    zq = proj(0, ATTN_WIDTH)
    zk = proj(ATTN_WIDTH, ATTN_WIDTH)
    zv = proj(2 * ATTN_WIDTH, ATTN_WIDTH)
    aq, bmq, bpq = rq_ref[0], rq_ref[1], rq_ref[2]
    ak, bmk, bpk = rk_ref[0], rk_ref[1], rk_ref[2]
    n_chunks = ATTN_WIDTH // LANES
    q_chunks, k_chunks, v_chunks = [], [], []
    for c in range(n_chunks):
        sl = slice(c * LANES, (c + 1) * LANES)
        q_chunks.append(_rope_chunk(zq[:, sl], aq, bmq, bpq))
        k_chunks.append(_rope_chunk(zk[:, sl], ak, bmk, bpk))
        v_chunks.append(zv[:, sl])

    for which, chunks in enumerate((q_chunks, k_chunks, v_chunks)):
        for g in range(N_DIL):
            out_ref = qkv_refs[3 * g + which]
            dil = DIL_GROUPS[g][1]
            for cc in range(CHUNKS_PER_GROUP):
                c = g * CHUNKS_PER_GROUP + cc
                sl = slice(cc * LANES, (cc + 1) * LANES)
                if dil == 1:
                    out_ref[0, :, sl] = chunks[c].astype(BF16)
                else:
                    d_ref[c] = chunks[c]
                    for r in range(dil):
                        out_ref[r, :, sl] = d_ref[c, pl.ds(r, tm // dil, stride=dil), :].astype(BF16)

    for g in range(N_DIL):
        kst, vst = st_refs[2 * g], st_refs[2 * g + 1]
        win = min(DIL_GROUPS[g][0], seq)
        if win >= tm:
            cond, r0 = i >= (seq - win) // tm, 0
        else:
            cond, r0 = i == n_tiles - 1, tm - win

        @pl.when(cond)
        def _(g=g, kst=kst, vst=vst, r0=r0):
            for cc in range(CHUNKS_PER_GROUP):
                c = g * CHUNKS_PER_GROUP + cc
                kst[cc * LANES:(cc + 1) * LANES, :] = k_chunks[c][r0:, :].T
                vst[cc * LANES:(cc + 1) * LANES, :] = v_chunks[c][r0:, :].T


def _qkvproj_prompt(x, mod4, g1, wqkv_bf, rope_q, rope_k, *, tm):
    bsz, seq, _ = x.shape
    n_tiles = seq // tm
    const2 = lambda b, i: (0, 0)
    tile3 = lambda b, i: (b, i, 0)
    in_specs = [
        pl.BlockSpec((None, tm, D_MODEL), tile3),
        pl.BlockSpec((None, 6, 1, D_MODEL), lambda b, i: (b, 0, 0, 0)),
        pl.BlockSpec((1, D_MODEL), const2),
        pl.BlockSpec(wqkv_bf.shape, const2, pipeline_mode=pl.Buffered(1)),
        pl.BlockSpec((3, tm, LANES), lambda b, i: (0, i, 0)),
        pl.BlockSpec((3, tm, LANES), lambda b, i: (0, i, 0)),
    ]
    out_shape, out_specs = [], []
    for _, dil in DIL_GROUPS:
        assert tm % (dil * 16) == 0
        for _ in range(3):
            out_shape.append(jax.ShapeDtypeStruct((bsz, dil, seq // dil, GROUP_WIDTH), BF16))
            out_specs.append(pl.BlockSpec((None, dil, tm // dil, GROUP_WIDTH), lambda b, i: (b, 0, i, 0)))
    for win, _ in DIL_GROUPS:
        win = min(win, seq)
        cols = min(win, tm)
        if win >= tm:
            imap = lambda b, i, ft=(seq - win) // tm: (b, 0, jnp.maximum(i - ft, 0))
        else:
            imap = lambda b, i: (b, 0, 0)
        for _ in range(2):
            out_shape.append(jax.ShapeDtypeStruct((bsz, GROUP_WIDTH, win), F32))
            out_specs.append(pl.BlockSpec((None, GROUP_WIDTH, cols), imap))
    return pl.pallas_call(
        functools.partial(_qkvproj_kernel, tm=tm, n_tiles=n_tiles, seq=seq),
        grid=(bsz, n_tiles),
        in_specs=in_specs,
        out_specs=out_specs,
        out_shape=out_shape,
        scratch_shapes=[pltpu.VMEM((ATTN_WIDTH // LANES, tm, LANES), F32)],
        compiler_params=_cparams(("arbitrary", "arbitrary")),
        name="qkvproj_prompt",
    )(x, mod4, g1, wqkv_bf, rope_q, rope_k)


ATTN_UNROLL = 4


def _attn_kernel(*refs, chunk, has_halo, n_cls):
    if has_halo:
        q_ref, k_ref, v_ref, kh_ref, vh_ref, o_ref, lse_ref, kbuf, vbuf = refs
    else:
        q_ref, k_ref, v_ref, o_ref, lse_ref, kbuf, vbuf = refs
    c = pl.program_id(2)
    if has_halo:
        kbuf[:, 0:Q_BLOCK, :] = kh_ref[...]
        vbuf[:, 0:Q_BLOCK, :] = vh_ref[...]
    else:
        kbuf[:, 0:Q_BLOCK, :] = jnp.zeros((n_cls, Q_BLOCK, GROUP_WIDTH), BF16)
        vbuf[:, 0:Q_BLOCK, :] = jnp.zeros((n_cls, Q_BLOCK, GROUP_WIDTH), BF16)
    kbuf[:, Q_BLOCK:Q_BLOCK + chunk, :] = k_ref[...]
    vbuf[:, Q_BLOCK:Q_BLOCK + chunk, :] = v_ref[...]
    blocks_per_cls = chunk // Q_BLOCK

    row = lax.broadcasted_iota(jnp.int32, (Q_BLOCK, 2 * Q_BLOCK), 0)
    col = lax.broadcasted_iota(jnp.int32, (Q_BLOCK, 2 * Q_BLOCK), 1)
    bias_main = jnp.where(col >= row, jnp.where(col <= row + Q_BLOCK, 0.0, NEG_BIG), NEG_BIG)
    bias_first = jnp.where(col >= Q_BLOCK, bias_main, NEG_BIG)
    lane = lax.broadcasted_iota(jnp.int32, (1, LANES), 1)
    lo_half = lane < HEAD_DIM

    def body(u, carry):
        cls, qb = u // blocks_per_cls, u % blocks_per_cls
        r0 = pl.multiple_of(qb * Q_BLOCK, Q_BLOCK)
        kt = kbuf[cls, pl.ds(r0, 2 * Q_BLOCK), :]
        vt = vbuf[cls, pl.ds(r0, 2 * Q_BLOCK), :]
        qt = q_ref[cls, pl.ds(r0, Q_BLOCK), :]
        is_first = jnp.logical_and(qb == 0, c == 0)
        bias = jnp.where(is_first, bias_first, bias_main)
        bias2 = jnp.concatenate([bias, bias], axis=0)
        lse_tile = jnp.zeros((Q_BLOCK, LANES), F32)
        for p in range(CHUNKS_PER_GROUP):
            sl = slice(p * LANES, (p + 1) * LANES)
            qp, kp, vp = qt[:, sl], kt[:, sl], vt[:, sl]
            zero = jnp.zeros_like(qp)
            q2 = jnp.concatenate([jnp.where(lo_half, qp, zero), jnp.where(lo_half, zero, qp)], axis=0)
            s = lax.dot_general(q2, kp, (((1,), (1,)), ((), ())), preferred_element_type=F32) + bias2
            m = jnp.max(s, axis=1, keepdims=True)
            e = jnp.exp(s - m)
            l = jnp.sum(e, axis=1, keepdims=True)
            o = jnp.dot(e.astype(BF16), vp, preferred_element_type=F32) * (1.0 / l)
            lse = m + jnp.log(l)
            for hh in range(2):
                lse_tile = jnp.where(lane == 2 * p + hh, lse[hh * Q_BLOCK:(hh + 1) * Q_BLOCK], lse_tile)
            o_ref[cls, pl.ds(r0, Q_BLOCK), sl] = jnp.where(lo_half, o[:Q_BLOCK], o[Q_BLOCK:])
        lse_ref[cls, pl.ds(r0, Q_BLOCK), :] = lse_tile
        return carry

    n_units = n_cls * blocks_per_cls
    lax.fori_loop(0, n_units, body, 0, unroll=min(ATTN_UNROLL, n_units))


def _attn_prompt_group(q, k, v, g, *, max_rows):
    bsz, dil, cls_len, _ = q.shape
    chunk = min(max_rows, cls_len)
    n_chunks = cls_len // chunk
    n_cls = min(dil, max_rows // chunk)
    has_halo = n_chunks > 1
    main = pl.BlockSpec((None, n_cls, chunk, GROUP_WIDTH), lambda b, r, c: (b, r, c, 0))
    in_specs = [main, main, main]
    args = [q, k, v]
    if has_halo:
        per = chunk // Q_BLOCK
        halo = pl.BlockSpec((None, n_cls, Q_BLOCK, GROUP_WIDTH), lambda b, r, c: (b, r, jnp.maximum(c * per - 1, 0), 0))
        in_specs += [halo, halo]
        args += [k, v]
    return pl.pallas_call(
        functools.partial(_attn_kernel, chunk=chunk, has_halo=has_halo, n_cls=n_cls),
        grid=(bsz, dil // n_cls, n_chunks),
        in_specs=in_specs,
        out_specs=[pl.BlockSpec((None, n_cls, chunk, GROUP_WIDTH), lambda b, r, c: (b, r, c, 0)),
                   pl.BlockSpec((None, n_cls, chunk, LANES), lambda b, r, c: (b, r, c, 0))],
        out_shape=[jax.ShapeDtypeStruct((bsz, dil, cls_len, GROUP_WIDTH), F32),
                   jax.ShapeDtypeStruct((bsz, dil, cls_len, LANES), F32)],
        scratch_shapes=[pltpu.VMEM((n_cls, chunk + Q_BLOCK, GROUP_WIDTH), BF16),
                        pltpu.VMEM((n_cls, chunk + Q_BLOCK, GROUP_WIDTH), BF16)],
        compiler_params=_cparams(("arbitrary", "arbitrary", "arbitrary")),
        name=f"attn_prompt_g{g}",
    )(*args)


def _split_bf16(x, n):
    parts = []
    r = x
    for _ in range(n):
        p = r.astype(BF16)
        parts.append(p)
        r = r - p.astype(F32)
    return parts


def _store_token_rows(ref, value, n_tokens, first_chunk=0, rows_per_token=None):
    n_chunks = value.shape[1] // LANES
    rows_per_token = rows_per_token or n_chunks
    for c in range(n_chunks):
        ref[pl.ds(first_chunk + c, n_tokens, stride=rows_per_token), :] = value[:, c * LANES:(c + 1) * LANES]


def _load_token_rows(ref, n_tokens, n_chunks, first_chunk=0, rows_per_token=None, lead=None):
    rows_per_token = rows_per_token or n_chunks

    def chunk(c):
        rows = pl.ds(first_chunk + c, n_tokens, stride=rows_per_token)
        return ref[rows, :] if lead is None else ref[lead, rows, :]

    return jnp.concatenate([chunk(c) for c in range(n_chunks)], axis=1)


H_ROWS = D_MODEL // LANES
Y_ROWS = 2 * D_MODEL // LANES

_PAIRS = ((0, 1), (0, 2), (0, 3), (1, 2), (1, 3), (2, 3))
N_CLASSES = N_EXPERT_GROUPS * len(_PAIRS)
META_CLASS, META_RANK, META_WA, META_WB = 0, 1, 2, 3


def _post_kernel(*refs, tm, dils, full_precision, routed):
    (o0_ref, o1_ref, o2_ref, l0_ref, l1_ref, l2_ref, pc_ref, sga_ref, x_ref, mod_ref,
     wao_ref, wo_ref, g2_ref, wr_ref, wrh_ref, wrl_ref, br_ref, exp_ref) = refs[:18]
    if routed:
        tri_ref, x1_ref, h2_ref, meta_ref, route_ref, cnt_ref, o_scr, l_scr, run_scr = refs[18:]
    else:
        x1_ref, h2_ref, comb_ref, o_scr, l_scr = refs[18:]

    def mm(a, w_ref):
        if full_precision:
            return jnp.dot(a, w_ref[...], precision=HIGHEST, preferred_element_type=F32)
        return jnp.dot(a.astype(BF16), w_ref[...], preferred_element_type=F32)

    def natural_order(ref, scr, dil, n_chunks):
        if dil == 1:
            return [ref[0, :, c * LANES:(c + 1) * LANES] for c in range(n_chunks)]
        out = []
        for c in range(n_chunks):
            for r in range(dil):
                scr[c, pl.ds(r, tm // dil, stride=dil), :] = ref[r, :, c * LANES:(c + 1) * LANES]
            out.append(scr[c])
        return out

    lses = [natural_order(ref, l_scr.at[g], dils[g], 1)[0] for g, ref in enumerate((l0_ref, l1_ref, l2_ref))]
    mx = jnp.maximum(lses[0], jnp.maximum(lses[1], lses[2]))
    es = [jnp.exp(v - mx) for v in lses]
    inv = 1.0 / (es[0] + es[1] + es[2])
    expand = exp_ref[...]
    attn_o = None
    for g, o_ref in enumerate((o0_ref, o1_ref, o2_ref)):
        w = es[g] * inv
        we = None
        for part in _split_bf16(w, 3 if full_precision else 2):
            t = jnp.dot(part, expand, preferred_element_type=F32)
            we = t if we is None else we + t
        o_nat = jnp.concatenate(natural_order(o_ref, o_scr.at[g], dils[g], CHUNKS_PER_GROUP), axis=1)
        term = we * o_nat
        attn_o = term if attn_o is None else attn_o + term

    y_attn = mm(attn_o, wao_ref)
    mixed = mm(pc_ref[...] + sga_ref[...] * y_attn, wo_ref)
    x1 = x_ref[...] + mod_ref[2] * mixed
    x1_ref[...] = x1
    var = jnp.mean(x1 * x1, axis=-1, keepdims=True)
    h2 = (x1 * lax.rsqrt(var + RMS_EPS)) * (g2_ref[...] * (1.0 + mod_ref[4])) + mod_ref[3]
    if routed:
        _store_token_rows(h2_ref, h2, tm)
    else:
        h2_ref[...] = h2.astype(h2_ref.dtype)

    if full_precision:
        lg = jnp.dot(h2, wr_ref[...], precision=HIGHEST, preferred_element_type=F32) + br_ref[...]
    else:
        h_hi, h_lo = _split_bf16(h2, 2)
        lg = (jnp.dot(h_hi, wrh_ref[...], preferred_element_type=F32)
              + jnp.dot(h_lo, wrh_ref[...], preferred_element_type=F32)
              + jnp.dot(h_hi, wrl_ref[...], preferred_element_type=F32)) + br_ref[...]
    lane_i = lax.broadcasted_iota(jnp.int32, (1, LANES), 1)
    lane = lane_i.astype(F32)
    lane_group = ((lane_i - N_EXPERT_GROUPS) >> 2).astype(F32)
    big = jnp.float32(1e9)
    gl = jnp.where(lane_i < N_EXPERT_GROUPS, lg, NEG_BIG)
    gmax = jnp.max(gl, axis=1, keepdims=True)
    gidx = jnp.min(jnp.where(gl == gmax, lane, big), axis=1, keepdims=True)
    g_w = 1.0 / jnp.sum(jnp.exp(gl - gmax), axis=1, keepdims=True)
    el = jnp.where(lane_group == gidx, lg, NEG_BIG)
    v1 = jnp.max(el, axis=1, keepdims=True)
    i1 = jnp.min(jnp.where(el == v1, lane, big), axis=1, keepdims=True)
    el2 = jnp.where(lane == i1, NEG_BIG, el)
    v2 = jnp.max(el2, axis=1, keepdims=True)
    i2 = jnp.min(jnp.where(el2 == v2, lane, big), axis=1, keepdims=True)
    t = jnp.exp(v2 - v1)
    den = 1.0 / (1.0 + t)
    w_top1, w_top2 = g_w * den, g_w * (t * den)
    if not routed:
        comb_ref[...] = jnp.where(lane == i1, w_top1, jnp.where(lane == i2, w_top2, 0.0))
        return

    base = jnp.float32(N_EXPERT_GROUPS) + jnp.float32(EXPERTS_PER_GROUP) * gidx
    e1, e2 = i1 - base, i2 - base
    first_is_lower = e1 < e2
    ea = jnp.where(first_is_lower, e1, e2)
    eb = jnp.where(first_is_lower, e2, e1)
    pair = ea * (7.0 - ea) * 0.5 + (eb - ea - 1.0)
    cls = gidx * jnp.float32(len(_PAIRS)) + pair
    onehot = lane == cls
    earlier = jnp.dot(tri_ref[...], jnp.where(onehot, 1.0, 0.0).astype(BF16), preferred_element_type=F32)

    @pl.when(jnp.logical_and(pl.program_id(0) == 0, pl.program_id(1) == 0))
    def _():
        run_scr[...] = jnp.zeros_like(run_scr)

    running = run_scr[...]
    rank = jnp.sum(jnp.where(onehot, earlier + running, 0.0), axis=1, keepdims=True)
    running = running + jnp.sum(jnp.where(onehot, 1.0, 0.0), axis=0, keepdims=True)
    run_scr[...] = running
    cnt_ref[...] = running
    w_a = jnp.where(first_is_lower, w_top1, w_top2)
    w_b = jnp.where(first_is_lower, w_top2, w_top1)
    meta = jnp.where(lane_i == META_CLASS, cls,
                     jnp.where(lane_i == META_RANK, rank,
                               jnp.where(lane_i == META_WA, w_a, jnp.where(lane_i == META_WB, w_b, 0.0))))
    meta_ref[...] = meta
    route_ref[...] = meta.T[0:SUBLANES, :].astype(jnp.int32)


def _post(o_list, lse_list, pc, sga, x, mod4, wao, wo, g2, wr, br, expand, *, tm, full_precision, routed):
    nb, rows, _ = x.shape
    tiles = rows // tm
    dils = tuple(o.shape[1] for o in o_list)
    tile3 = lambda b, i: (b, i, 0)
    const2 = lambda b, i: (0, 0)
    mod_rows = mod4.shape[2]
    if mod_rows == 1:
        mod_spec = pl.BlockSpec((None, 6, 1, D_MODEL), lambda b, i: (b, 0, 0, 0))
    else:
        mod_spec = pl.BlockSpec((None, 6, tm, D_MODEL), lambda b, i: (b, 0, i, 0))
    cls4 = lambda b, i: (b, 0, i, 0)
    in_specs = (
        [pl.BlockSpec((None, d, tm // d, GROUP_WIDTH), cls4) for d in dils]
        + [pl.BlockSpec((None, d, tm // d, LANES), cls4) for d in dils]
        + [pl.BlockSpec((None, tm, D_MODEL), tile3)] * 3 + [mod_spec]
        + [pl.BlockSpec(wao.shape, const2), pl.BlockSpec(wo.shape, const2), pl.BlockSpec((1, D_MODEL), const2)]
        + [pl.BlockSpec((D_MODEL, LANES), const2)] * 3
        + [pl.BlockSpec((1, LANES), const2), pl.BlockSpec((LANES, GROUP_WIDTH), const2)]
    )
    wr_hi = wr.astype(BF16)
    wr_lo = (wr - wr_hi.astype(F32)).astype(BF16)
    args = [*o_list, *lse_list, pc, sga, x, mod4, wao, wo, g2, wr, wr_hi, wr_lo, br, expand]
    scratch = [pltpu.VMEM((N_DIL, CHUNKS_PER_GROUP, tm, LANES), F32), pltpu.VMEM((N_DIL, 1, tm, LANES), F32)]
    if routed:
        tri = jnp.asarray(np.tril(np.ones((tm, tm), np.float32), -1), dtype=BF16)
        args.append(tri)
        in_specs = in_specs + [pl.BlockSpec((tm, tm), const2)]
        out_specs = [pl.BlockSpec((None, tm, D_MODEL), tile3), pl.BlockSpec((None, tm * H_ROWS, LANES), tile3),
                     pl.BlockSpec((None, tm, LANES), tile3),
                     pl.BlockSpec((None, None, SUBLANES, tm), lambda b, i: (b, i, 0, 0)),
                     pl.BlockSpec((1, LANES), const2)]
        out_shape = [jax.ShapeDtypeStruct((nb, rows, D_MODEL), F32),
                     jax.ShapeDtypeStruct((nb, rows * H_ROWS, LANES), F32),
                     jax.ShapeDtypeStruct((nb, rows, LANES), F32),
                     jax.ShapeDtypeStruct((nb, tiles, SUBLANES, tm), jnp.int32),
                     jax.ShapeDtypeStruct((1, LANES), F32)]
        scratch.append(pltpu.VMEM((1, LANES), F32))
    else:
        out_specs = [pl.BlockSpec((None, tm, D_MODEL), tile3), pl.BlockSpec((None, tm, D_MODEL), tile3),
                     pl.BlockSpec((None, tm, LANES), tile3)]
        out_shape = [jax.ShapeDtypeStruct((nb, rows, D_MODEL), F32), jax.ShapeDtypeStruct((nb, rows, D_MODEL), F32),
                     jax.ShapeDtypeStruct((nb, rows, LANES), F32)]
    return pl.pallas_call(
        functools.partial(_post_kernel, tm=tm, dils=dils, full_precision=full_precision, routed=routed),
        grid=(nb, tiles),
        in_specs=in_specs,
        out_specs=out_specs,
        out_shape=out_shape,
        scratch_shapes=scratch,
        compiler_params=_cparams(("arbitrary", "arbitrary")),
        name="post_sample" if full_precision else "post_prompt",
    )(*args)


DMA_LOOP_UNROLL = 8


ROUTE_CLASS_ROW, ROUTE_RANK_ROW = META_CLASS, META_RANK


def _sorted_slot(starts_ref, route_ref, r):
    return starts_ref[route_ref[ROUTE_CLASS_ROW, r]] + route_ref[ROUTE_RANK_ROW, r]


def _token_rows(token, rows_per_token):
    return pl.ds(pl.multiple_of(token * rows_per_token, rows_per_token), rows_per_token)


def _dispatch_kernel(starts_ref, route_ref, src_ref, dst_hbm, buf, sems, *, rows_per_step, n_steps):
    i = pl.program_id(0)
    slot = i % 2

    def wait_slot(s):
        def body(r, carry):
            pltpu.make_async_copy(buf.at[s, _token_rows(0, H_ROWS)], dst_hbm.at[_token_rows(0, H_ROWS)],
                                  sems.at[s * rows_per_step + r]).wait()
            return carry
        lax.fori_loop(0, rows_per_step, body, 0, unroll=DMA_LOOP_UNROLL)

    @pl.when(i >= 1)
    def _():
        wait_slot(1 - slot)

    buf[slot] = src_ref[...]

    def issue(r, carry):
        pltpu.make_async_copy(buf.at[slot, _token_rows(r, H_ROWS)],
                              dst_hbm.at[_token_rows(_sorted_slot(starts_ref, route_ref, r), H_ROWS)],
                              sems.at[slot * rows_per_step + r]).start()
        return carry

    lax.fori_loop(0, rows_per_step, issue, 0, unroll=DMA_LOOP_UNROLL)

    @pl.when(i == n_steps - 1)
    def _():
        wait_slot(slot)


def _route_spec(rows_per_step, route_tile, shift=0, n_steps=None):
    per_tile = route_tile // rows_per_step

    def imap(i, starts):
        s = i + shift if n_steps is None else jnp.minimum(i + shift, n_steps - 1)
        return (s // per_tile, 0, s % per_tile)

    return pl.BlockSpec((None, SUBLANES, rows_per_step), imap, memory_space=pltpu.SMEM)


def _dispatch_rows(src, route, starts, *, rows_per_step):
    n = src.shape[0] // H_ROWS
    n_steps = n // rows_per_step
    grid_spec = pltpu.PrefetchScalarGridSpec(
        num_scalar_prefetch=1,
        grid=(n_steps,),
        in_specs=[_route_spec(rows_per_step, route.shape[2]),
                  pl.BlockSpec((rows_per_step * H_ROWS, LANES), lambda i, starts: (i, 0))],
        out_specs=pl.BlockSpec(memory_space=pl.ANY),
        scratch_shapes=[pltpu.VMEM((2, rows_per_step * H_ROWS, LANES), src.dtype),
                        pltpu.SemaphoreType.DMA((2 * rows_per_step,))],
    )
    return pl.pallas_call(
        functools.partial(_dispatch_kernel, rows_per_step=rows_per_step, n_steps=n_steps),
        grid_spec=grid_spec,
        out_shape=jax.ShapeDtypeStruct(src.shape, src.dtype),
        compiler_params=_cparams(("arbitrary",)),
        name="moe_dispatch",
    )(starts, route, src)


def _experts_kernel(tile_ref, ea_ref, eb_ref, lo_ref, hi_ref, first_ref, nv_ref,
                    x_ref, w1a_ref, w1b_ref, w3a_ref, w3b_ref, w2a_ref, w2b_ref, y_ref, *, tm):
    i = pl.program_id(0)

    @pl.when(i < nv_ref[0])
    def _():
        x = _load_token_rows(x_ref, tm, H_ROWS).astype(BF16)

        def expert(w1_ref, w3_ref, w2_ref):
            a = jnp.dot(x, w1_ref[...], preferred_element_type=F32)
            b = jnp.dot(x, w3_ref[...], preferred_element_type=F32)
            hid = (a * _sigmoid(a)) * b
            return jnp.dot(hid.astype(BF16), w2_ref[...], preferred_element_type=F32)

        y_both = jnp.concatenate([expert(w1a_ref, w3a_ref, w2a_ref), expert(w1b_ref, w3b_ref, w2b_ref)], axis=1)

        @pl.when(first_ref[i] == 1)
        def _():
            _store_token_rows(y_ref, y_both, tm)

        @pl.when(first_ref[i] == 0)
        def _():
            row = lax.broadcasted_iota(jnp.int32, (tm, 1), 0)
            old = _load_token_rows(y_ref, tm, Y_ROWS)
            _store_token_rows(y_ref, jnp.where(row >= lo_ref[i], jnp.where(row < hi_ref[i], y_both, old), old), tm)


def _experts_routed(hs, tables, w1_bf, w3_bf, w2_bf, *, tm):
    n = hs.shape[0] // H_ROWS
    n_items = tables[0].shape[0]
    x_map = lambda i, tile, ea, eb, lo, hi, first, nv: (tile[i], 0)
    wa_map = lambda i, tile, ea, eb, lo, hi, first, nv: (ea[i], 0, 0)
    wb_map = lambda i, tile, ea, eb, lo, hi, first, nv: (eb[i], 0, 0)
    w13 = lambda m: pl.BlockSpec((None, D_MODEL, D_EXPERT), m)
    w2s = lambda m: pl.BlockSpec((None, D_EXPERT, D_MODEL), m)
    grid_spec = pltpu.PrefetchScalarGridSpec(
        num_scalar_prefetch=7,
        grid=(n_items,),
        in_specs=[pl.BlockSpec((tm * H_ROWS, LANES), x_map),
                  w13(wa_map), w13(wb_map), w13(wa_map), w13(wb_map), w2s(wa_map), w2s(wb_map)],
        out_specs=pl.BlockSpec((tm * Y_ROWS, LANES), x_map),
    )
    return pl.pallas_call(
        functools.partial(_experts_kernel, tm=tm),
        grid_spec=grid_spec,
        out_shape=jax.ShapeDtypeStruct((n * Y_ROWS, LANES), F32),
        compiler_params=_cparams(("arbitrary",)),
        name="moe_experts",
    )(*tables, hs, w1_bf, w1_bf, w3_bf, w3_bf, w2_bf, w2_bf)


def _combine_kernel(starts_ref, route_ref, route_next_ref, yp_hbm, x1_ref, meta_ref, mod_ref, gf_ref, y_ref, ybuf, sems,
                    *, rows_per_step, n_steps):
    i = pl.program_id(0)
    slot = i % 2

    def issue(r_ref, s):
        def body(r, carry):
            pltpu.make_async_copy(yp_hbm.at[_token_rows(_sorted_slot(starts_ref, r_ref, r), Y_ROWS)],
                                  ybuf.at[s, _token_rows(r, Y_ROWS)], sems.at[s * rows_per_step + r]).start()
            return carry
        lax.fori_loop(0, rows_per_step, body, 0, unroll=DMA_LOOP_UNROLL)

    @pl.when(i == 0)
    def _():
        issue(route_ref, 0)

    @pl.when(i + 1 < n_steps)
    def _():
        issue(route_next_ref, 1 - slot)

    def wait_body(r, carry):
        pltpu.make_async_copy(yp_hbm.at[_token_rows(0, Y_ROWS)], ybuf.at[slot, _token_rows(r, Y_ROWS)],
                              sems.at[slot * rows_per_step + r]).wait()
        return carry

    lax.fori_loop(0, rows_per_step, wait_body, 0, unroll=DMA_LOOP_UNROLL)
    ya = _load_token_rows(ybuf, rows_per_step, H_ROWS, first_chunk=0, rows_per_token=Y_ROWS, lead=slot)
    yb = _load_token_rows(ybuf, rows_per_step, H_ROWS, first_chunk=H_ROWS, rows_per_token=Y_ROWS, lead=slot)
    meta = meta_ref[...]
    lane = lax.broadcasted_iota(jnp.int32, (1, LANES), 1)
    w_a = jnp.sum(jnp.where(lane == META_WA, meta, 0.0), axis=1, keepdims=True)
    w_b = jnp.sum(jnp.where(lane == META_WB, meta, 0.0), axis=1, keepdims=True)
    x2 = x1_ref[...] + mod_ref[5] * (w_a * ya + w_b * yb)
    var = jnp.mean(x2 * x2, axis=-1, keepdims=True)
    y_ref[...] = (x2 * lax.rsqrt(var + RMS_EPS)) * gf_ref[...]


def _combine_final(yp, route, starts, x1, meta, mod4, gf, *, rows_per_step, rows_per_mod):
    n = x1.shape[0]
    n_steps = n // rows_per_step
    steps_per_mod = rows_per_mod // rows_per_step
    row = lambda i, starts: (i, 0)
    grid_spec = pltpu.PrefetchScalarGridSpec(
        num_scalar_prefetch=1,
        grid=(n_steps,),
        in_specs=[_route_spec(rows_per_step, route.shape[2]),
                  _route_spec(rows_per_step, route.shape[2], shift=1, n_steps=n_steps),
                  pl.BlockSpec(memory_space=pl.ANY),
                  pl.BlockSpec((rows_per_step, D_MODEL), row),
                  pl.BlockSpec((rows_per_step, LANES), row),
                  pl.BlockSpec((None, 6, 1, D_MODEL), lambda i, starts: (i // steps_per_mod, 0, 0, 0)),
                  pl.BlockSpec((1, D_MODEL), lambda i, starts: (0, 0))],
        out_specs=pl.BlockSpec((rows_per_step, D_MODEL), row),
        scratch_shapes=[pltpu.VMEM((2, rows_per_step * Y_ROWS, LANES), F32),
                        pltpu.SemaphoreType.DMA((2 * rows_per_step,))],
    )
    return pl.pallas_call(
        functools.partial(_combine_kernel, rows_per_step=rows_per_step, n_steps=n_steps),
        grid_spec=grid_spec,
        out_shape=jax.ShapeDtypeStruct((n, D_MODEL), F32),
        compiler_params=_cparams(("arbitrary",)),
        name="moe_combine",
    )(starts, route, route, yp, x1, meta, mod4, gf)


SC_CORES, SC_SUBCORES = 2, 16
SC_GATHER_TOKENS = 32


def _sc_gather_tokens(table, idx):
    n = idx.shape[0]
    _, rows, lanes = table.shape
    per_worker = n // (SC_CORES * SC_SUBCORES)
    chunks = per_worker // SC_GATHER_TOKENS
    mesh = plsc.VectorSubcoreMesh(core_axis_name="c", subcore_axis_name="s")

    @functools.partial(
        pl.kernel, mesh=mesh, out_type=jax.ShapeDtypeStruct((n, rows, lanes), table.dtype),
        scratch_types=[pltpu.VMEM((SC_GATHER_TOKENS,), jnp.int32),
                       pltpu.VMEM((SC_GATHER_TOKENS, rows, lanes), table.dtype),
                       pltpu.SemaphoreType.DMA],
        name="moe_gather_sc")
    def gather(table_hbm, idx_hbm, out_hbm, idx_v, rows_v, sem):
        worker = lax.axis_index("s") * SC_CORES + lax.axis_index("c")

        @pl.loop(0, chunks)
        def _(j):
            base = pl.multiple_of(worker * per_worker + j * SC_GATHER_TOKENS, SUBLANES)
            pltpu.sync_copy(idx_hbm.at[pl.ds(base, SC_GATHER_TOKENS)], idx_v)
            pltpu.async_copy(table_hbm.at[idx_v], rows_v, sem).wait()
            pltpu.sync_copy(rows_v, out_hbm.at[pl.ds(base, SC_GATHER_TOKENS)])

    return gather(table, idx)


def _final_kernel(y_ref, x1_ref, meta_ref, mod_ref, gf_ref, o_ref, *, tm):
    ya = _load_token_rows(y_ref, tm, H_ROWS, first_chunk=0, rows_per_token=Y_ROWS)
    yb = _load_token_rows(y_ref, tm, H_ROWS, first_chunk=H_ROWS, rows_per_token=Y_ROWS)
    meta = meta_ref[...]
    lane = lax.broadcasted_iota(jnp.int32, (1, LANES), 1)
    w_a = jnp.sum(jnp.where(lane == META_WA, meta, 0.0), axis=1, keepdims=True)
    w_b = jnp.sum(jnp.where(lane == META_WB, meta, 0.0), axis=1, keepdims=True)
    x2 = x1_ref[...] + mod_ref[5] * (w_a * ya + w_b * yb)
    var = jnp.mean(x2 * x2, axis=-1, keepdims=True)
    o_ref[...] = (x2 * lax.rsqrt(var + RMS_EPS)) * gf_ref[...]


def _final(y_rows, x1, meta, mod4, gf, *, tm, rows_per_mod):
    n = x1.shape[0]
    steps_per_mod = rows_per_mod // tm
    row = lambda i: (i, 0)
    return pl.pallas_call(
        functools.partial(_final_kernel, tm=tm),
        grid=(n // tm,),
        in_specs=[pl.BlockSpec((tm * Y_ROWS, LANES), row),
                  pl.BlockSpec((tm, D_MODEL), row),
                  pl.BlockSpec((tm, LANES), row),
                  pl.BlockSpec((None, 6, 1, D_MODEL), lambda i: (i // steps_per_mod, 0, 0, 0)),
                  pl.BlockSpec((1, D_MODEL), lambda i: (0, 0))],
        out_specs=pl.BlockSpec((tm, D_MODEL), row),
        out_shape=jax.ShapeDtypeStruct((n, D_MODEL), F32),
        compiler_params=_cparams(("arbitrary",)),
        name="moe_final",
    )(y_rows, x1, meta, mod4, gf)


def _routing_tables(counts, *, n, tm):
    counts = counts.astype(jnp.int32)
    ends = jnp.cumsum(counts)
    starts = ends - counts
    n_items = n // tm + N_CLASSES
    first_tile = starts // tm
    last_tile = jnp.maximum(ends - 1, starts) // tm
    visits = jnp.where(counts > 0, last_tile - first_tile + 1, 0)
    item_end = jnp.cumsum(visits)
    item_start = item_end - visits
    n_valid = item_end[-1]
    item = jnp.arange(n_items, dtype=jnp.int32)
    idx = jnp.minimum(item, n_valid - 1)
    c = jnp.sum((idx[:, None] >= item_end[None, :]).astype(jnp.int32), axis=1)
    class_ids = jnp.arange(N_CLASSES, dtype=jnp.int32)

    def pick(table):
        return jnp.sum(jnp.where(c[:, None] == class_ids[None, :], table[None, :], 0), axis=1)

    tile = pick(first_tile) + (idx - pick(item_start))
    live = item < n_valid
    lo = jnp.where(live, jnp.clip(pick(starts) - tile * tm, 0, tm), 0)
    hi = jnp.where(live, jnp.clip(pick(ends) - tile * tm, 0, tm), 0)
    prev_tile = jnp.concatenate([jnp.full((1,), -1, jnp.int32), tile[:-1]])
    first = (tile != prev_tile).astype(jnp.int32)
    group_of_class = np.arange(N_CLASSES) // len(_PAIRS)
    pair_of_class = np.arange(N_CLASSES) % len(_PAIRS)
    pairs = np.asarray(_PAIRS)
    ea = pick(jnp.asarray(group_of_class * EXPERTS_PER_GROUP + pairs[pair_of_class, 0], dtype=jnp.int32))
    eb = pick(jnp.asarray(group_of_class * EXPERTS_PER_GROUP + pairs[pair_of_class, 1], dtype=jnp.int32))
    as_i32 = lambda a: a.astype(jnp.int32)
    return as_i32(starts), tuple(as_i32(a) for a in (tile, ea, eb, lo, hi, first, n_valid.reshape(1)))


def _moe_kernel(h_ref, comb_ref, x1_ref, mod_ref, gf_ref, w1_ref, w3_ref, w2_ref, y_ref, acc_ref):
    e = pl.program_id(2)

    @pl.when(e == 0)
    def _():
        acc_ref[...] = jnp.zeros_like(acc_ref)

    h = h_ref[...].astype(BF16)
    a = jnp.dot(h, w1_ref[...], preferred_element_type=F32)
    b = jnp.dot(h, w3_ref[...], preferred_element_type=F32)
    lane = lax.broadcasted_iota(jnp.int32, (1, LANES), 1)
    cw = jnp.sum(jnp.where(lane == e + N_EXPERT_GROUPS, comb_ref[...], 0.0), axis=1, keepdims=True)
    hid = (a * _sigmoid(a)) * b * cw
    acc_ref[...] += jnp.dot(hid.astype(BF16), w2_ref[...], preferred_element_type=F32)

    @pl.when(e == N_EXPERTS - 1)
    def _():
        x2 = x1_ref[...] + mod_ref[5] * acc_ref[...]
        var = jnp.mean(x2 * x2, axis=-1, keepdims=True)
        y_ref[...] = (x2 * lax.rsqrt(var + RMS_EPS)) * gf_ref[...]


def _moe_dense(h2, comb, x1, mod4, gf, w1_bf, w3_bf, w2_bf, *, tm):
    nb, rows, _ = x1.shape
    tiles = rows // tm
    tile3 = lambda b, i, e: (b, i, 0)
    if mod4.shape[2] == 1:
        mod_spec = pl.BlockSpec((None, 6, 1, D_MODEL), lambda b, i, e: (b, 0, 0, 0))
    else:
        mod_spec = pl.BlockSpec((None, 6, tm, D_MODEL), lambda b, i, e: (b, 0, i, 0))
    return pl.pallas_call(
        _moe_kernel,
        grid=(nb, tiles, N_EXPERTS),
        in_specs=[
            pl.BlockSpec((None, tm, D_MODEL), tile3),
            pl.BlockSpec((None, tm, LANES), tile3),
            pl.BlockSpec((None, tm, D_MODEL), tile3),
            mod_spec,
            pl.BlockSpec((1, D_MODEL), lambda b, i, e: (0, 0)),
            pl.BlockSpec((None, D_MODEL, D_EXPERT), lambda b, i, e: (e, 0, 0)),
            pl.BlockSpec((None, D_MODEL, D_EXPERT), lambda b, i, e: (e, 0, 0)),
            pl.BlockSpec((None, D_EXPERT, D_MODEL), lambda b, i, e: (e, 0, 0)),
        ],
        out_specs=pl.BlockSpec((None, tm, D_MODEL), tile3),
        out_shape=jax.ShapeDtypeStruct((nb, rows, D_MODEL), F32),
        scratch_shapes=[pltpu.VMEM((tm, D_MODEL), F32)],
        compiler_params=_cparams(("arbitrary", "arbitrary", "arbitrary")),
        name=f"moe_dense_{nb * rows}",
    )(h2, comb, x1, mod4, gf, w1_bf, w3_bf, w2_bf)


def _s_inproj_kernel(x_ref, mod_ref, g1_ref, w_ref, z_ref):
    x = x_ref[...]
    var = jnp.mean(x * x, axis=-1, keepdims=True)
    h = (x * lax.rsqrt(var + RMS_EPS)) * (g1_ref[...] * (1.0 + mod_ref[1])) + mod_ref[0]
    z_ref[...] = jnp.dot(h, w_ref[...], precision=HIGHEST, preferred_element_type=F32)


def _s_inproj(x, mod_tok, g1, w_in, *, col_block):
    n = x.shape[0]
    return pl.pallas_call(
        _s_inproj_kernel,
        grid=(IN_COLS // col_block,),
        in_specs=[pl.BlockSpec((n, D_MODEL), lambda j: (0, 0)),
                  pl.BlockSpec((6, n, D_MODEL), lambda j: (0, 0, 0)),
                  pl.BlockSpec((1, D_MODEL), lambda j: (0, 0)),
                  pl.BlockSpec((D_MODEL, col_block), lambda j: (0, j))],
        out_specs=pl.BlockSpec((n, col_block), lambda j: (0, j)),
        out_shape=jax.ShapeDtypeStruct((n, IN_COLS), F32),
        compiler_params=_cparams(("arbitrary",)),
        name="inproj_sample",
    )(x, mod_tok, g1, w_in)


def _s_mid_kernel(z_ref, p0_ref, p1_ref, cw_ref, wco_ref, rq_ref, rk_ref,
                  cu_ref, pc_ref, sga_ref, q_ref, k_ref, v_ref, *, t_len):
    n = z_ref.shape[0]
    cu = z_ref[:, OFF_GC:OFF_GC + D_MODEL] * z_ref[:, OFF_U:OFF_U + D_MODEL]
    cu_ref[...] = cu
    t = lax.broadcasted_iota(jnp.int32, (n, D_MODEL), 0) & (t_len - 1)
    prev1 = jnp.where(t >= 1, pltpu.roll(cu, 1, 0), p1_ref[...])
    prev2 = jnp.where(t >= 2, pltpu.roll(cu, 2, 0), jnp.where(t == 0, p0_ref[...], p1_ref[...]))
    cw = cw_ref[...]
    conv = cw[0:1] * prev2 + cw[1:2] * prev1 + cw[2:3] * cu
    yc = jnp.dot(z_ref[:, OFF_GB:OFF_GB + D_MODEL] * conv, wco_ref[...], precision=HIGHEST,
                 preferred_element_type=F32)
    pc_ref[...] = _sigmoid(z_ref[:, OFF_GCONV:OFF_GCONV + D_MODEL]) * yc
    sga_ref[...] = _sigmoid(z_ref[:, OFF_GATTN:OFF_GATTN + D_MODEL])
    aq, bmq, bpq = rq_ref[0], rq_ref[1], rq_ref[2]
    ak, bmk, bpk = rk_ref[0], rk_ref[1], rk_ref[2]
    for c in range(ATTN_WIDTH // LANES):
        sl = slice(c * LANES, (c + 1) * LANES)
        q_ref[:, sl] = _rope_chunk(z_ref[:, OFF_Q + c * LANES:OFF_Q + (c + 1) * LANES], aq, bmq, bpq)
        k_ref[:, sl] = _rope_chunk(z_ref[:, OFF_K + c * LANES:OFF_K + (c + 1) * LANES], ak, bmk, bpk)
    v_ref[...] = z_ref[:, OFF_V:OFF_V + ATTN_WIDTH]


def _s_mid(z, p0e, p1e, conv_w, w_conv_out, rope_q, rope_k, *, t_len):
    n = z.shape[0]
    assert t_len & (t_len - 1) == 0
    full = lambda shape: pl.BlockSpec(shape, lambda i: (0,) * len(shape))
    out_shape = [jax.ShapeDtypeStruct((n, D_MODEL), F32)] * 3 + [jax.ShapeDtypeStruct((n, ATTN_WIDTH), F32)] * 3
    return pl.pallas_call(
        functools.partial(_s_mid_kernel, t_len=t_len),
        grid=(1,),
        in_specs=[full(z.shape), full(p0e.shape), full(p1e.shape), full(conv_w.shape), full(w_conv_out.shape),
                  full(rope_q.shape), full(rope_k.shape)],
        out_specs=[full((n, D_MODEL))] * 3 + [full((n, ATTN_WIDTH))] * 3,
        out_shape=out_shape,
        compiler_params=_cparams(("arbitrary",)),
        name="mid_sample",
    )(z, p0e, p1e, conv_w, w_conv_out, rope_q, rope_k)


def _head_sum(x):
    return jnp.sum(x.reshape(HEADS_PER_GROUP, HEAD_DIM, x.shape[-1]), axis=1)


def _head_expand(x):
    n = x.shape[-1]
    return jnp.broadcast_to(x[:, None, :], (HEADS_PER_GROUP, HEAD_DIM, n)).reshape(GROUP_WIDTH, n)


def _s_attn_disjoint(q_ref, kn, vn, ck_ref, cv_ref, o_ref, lse_ref, s_scr, lane, new_idx, *, n_tiles, dil, t_len):
    cls = lane & (dil - 1)
    q_all = q_ref[...]
    qsel = jnp.zeros((GROUP_WIDTH, LANES), F32)
    for t in range(t_len):
        qsel = jnp.where(cls == t, jnp.broadcast_to(q_all[:, t:t + 1], (GROUP_WIDTH, LANES)), qsel)
    smax = None
    for j in range(n_tiles):
        s = jnp.where(cls < t_len, _head_sum(ck_ref[:, j * LANES:(j + 1) * LANES] * qsel), NEG_BIG)
        s_scr[:, j * LANES:(j + 1) * LANES] = s
        smax = s if smax is None else jnp.maximum(smax, s)
    s_new = _head_sum(kn * q_all)
    m_cols = []
    m_lane = jnp.zeros((HEADS_PER_GROUP, LANES), F32)
    m_new = jnp.zeros((HEADS_PER_GROUP, t_len), F32)
    for t in range(t_len):
        mt = jnp.maximum(jnp.max(jnp.where(cls == t, smax, NEG_BIG), axis=1, keepdims=True), s_new[:, t:t + 1])
        m_cols.append(mt)
        m_lane = jnp.where(cls == t, mt, m_lane)
        m_new = jnp.where(new_idx == t, mt, m_new)
    esum = jnp.zeros((HEADS_PER_GROUP, LANES), F32)
    for j in range(n_tiles):
        e = jnp.exp(s_scr[:, j * LANES:(j + 1) * LANES] - m_lane)
        s_scr[:, j * LANES:(j + 1) * LANES] = e
        esum = esum + e
    e_new = jnp.exp(s_new - m_new)
    l_cols = []
    inv_lane = jnp.zeros((HEADS_PER_GROUP, LANES), F32)
    inv_new = jnp.zeros((HEADS_PER_GROUP, t_len), F32)
    for t in range(t_len):
        lt = jnp.sum(jnp.where(cls == t, esum, 0.0), axis=1, keepdims=True) + e_new[:, t:t + 1]
        l_cols.append(lt)
        inv_lane = jnp.where(cls == t, 1.0 / lt, inv_lane)
        inv_new = jnp.where(new_idx == t, 1.0 / lt, inv_new)
    acc = None
    for j in range(n_tiles):
        term = cv_ref[:, j * LANES:(j + 1) * LANES] * _head_expand(s_scr[:, j * LANES:(j + 1) * LANES] * inv_lane)
        acc = term if acc is None else acc + term
    o_new = vn * _head_expand(e_new * inv_new)
    for t in range(t_len):
        o_ref[:, t:t + 1] = jnp.sum(jnp.where(cls == t, acc, 0.0), axis=1, keepdims=True) + o_new[:, t:t + 1]
        lse_ref[:, t:t + 1] = m_cols[t] + jnp.log(l_cols[t])


def _s_attn_kernel(q_ref, kn_ref, vn_ref, kt_ref, vt_ref, ck_ref, cv_ref, o_ref, lse_ref, ko_ref, vo_ref, s_scr,
                   *, win, dil, t_len):
    n_tiles = win // LANES
    lane = lax.broadcasted_iota(jnp.int32, (1, LANES), 1)
    new_idx = lax.broadcasted_iota(jnp.int32, (1, t_len), 1)
    kn = kn_ref[...]
    vn = vn_ref[...]
    if dil >= t_len:
        _s_attn_disjoint(q_ref, kn, vn, ck_ref, cv_ref, o_ref, lse_ref, s_scr, lane, new_idx,
                         n_tiles=n_tiles, dil=dil, t_len=t_len)
    for t in range(t_len if dil < t_len else 0):
        qb = jnp.broadcast_to(q_ref[:, t:t + 1], (GROUP_WIDTH, LANES))
        m = None
        for j in range(n_tiles):
            pos = lane + j * LANES
            s = _head_sum(ck_ref[:, j * LANES:(j + 1) * LANES] * qb)
            s = jnp.where(pos >= t, jnp.where(((pos - t) & (dil - 1)) == 0, s, NEG_BIG), NEG_BIG)
            s_scr[:, j * LANES:(j + 1) * LANES] = s
            mj = jnp.max(s, axis=1, keepdims=True)
            m = mj if m is None else jnp.maximum(m, mj)
        s_new = _head_sum(kn * qb[:, 0:t_len])
        s_new = jnp.where(new_idx <= t, jnp.where(((t - new_idx) & (dil - 1)) == 0, s_new, NEG_BIG), NEG_BIG)
        m = jnp.maximum(m, jnp.max(s_new, axis=1, keepdims=True))
        e_new = jnp.exp(s_new - m)
        l = jnp.sum(e_new, axis=1, keepdims=True)
        acc = None
        for j in range(n_tiles):
            e = jnp.exp(s_scr[:, j * LANES:(j + 1) * LANES] - m)
            l = l + jnp.sum(e, axis=1, keepdims=True)
            term = cv_ref[:, j * LANES:(j + 1) * LANES] * _head_expand(e)
            acc = term if acc is None else acc + term
        o = jnp.sum(acc, axis=1, keepdims=True) + jnp.sum(vn * _head_expand(e_new), axis=1, keepdims=True)
        o_ref[:, t:t + 1] = o * _head_expand(1.0 / l)
        lse_ref[:, t:t + 1] = m + jnp.log(l)

    for c_ref, tail_ref, out_ref in ((ck_ref, kt_ref, ko_ref), (cv_ref, vt_ref, vo_ref)):
        rolled = pltpu.roll(c_ref[...], win - t_len, 1)
        if win > LANES:
            out_ref[:, 0:win - LANES] = rolled[:, 0:win - LANES]
        out_ref[:, win - LANES:win] = jnp.where(lane >= LANES - t_len, tail_ref[...], rolled[:, win - LANES:win])


def _s_attn_group(q_t, kn_t, vn_t, k_tail, v_tail, cache_k, cache_v, g):
    bsz, _, t_len = q_t.shape
    win_full, dil = DIL_GROUPS[g]
    win = cache_k.shape[2]
    assert win == win_full and win == (KEYS_PER_QUERY - 1) * dil and win % LANES == 0
    assert dil & (dil - 1) == 0 and LANES % dil == 0
    grp = lambda b: (b, g, 0)
    per_b = lambda b: (b, 0, 0)
    return pl.pallas_call(
        functools.partial(_s_attn_kernel, win=win, dil=dil, t_len=t_len),
        grid=(bsz,),
        in_specs=[pl.BlockSpec((None, GROUP_WIDTH, t_len), grp)] * 3
        + [pl.BlockSpec((None, GROUP_WIDTH, LANES), grp)] * 2
        + [pl.BlockSpec((None, GROUP_WIDTH, win), per_b)] * 2,
        out_specs=[pl.BlockSpec((None, GROUP_WIDTH, t_len), per_b),
                   pl.BlockSpec((None, HEADS_PER_GROUP, t_len), per_b),
                   pl.BlockSpec((None, GROUP_WIDTH, win), per_b),
                   pl.BlockSpec((None, GROUP_WIDTH, win), per_b)],
        out_shape=[jax.ShapeDtypeStruct((bsz, GROUP_WIDTH, t_len), F32),
                   jax.ShapeDtypeStruct((bsz, HEADS_PER_GROUP, t_len), F32),
                   jax.ShapeDtypeStruct((bsz, GROUP_WIDTH, win), F32),
                   jax.ShapeDtypeStruct((bsz, GROUP_WIDTH, win), F32)],
        scratch_shapes=[pltpu.VMEM((HEADS_PER_GROUP, win), F32)],
        compiler_params=_cparams(("arbitrary",)),
        name=f"attn_sample_g{g}",
    )(q_t, kn_t, vn_t, k_tail, v_tail, cache_k, cache_v)


TM_INPROJ = 512
TM_POST = 512
TM_EXPERT = 256
TM_FINAL = 256
ROWS_PER_DMA_STEP = 128
ATTN_ROWS_PER_STEP = 1024
S_COL_BLOCK = 512


def _to_state(a_t):
    b, _, length = a_t.shape
    return jnp.transpose(a_t.reshape(b, HEADS_PER_GROUP, HEAD_DIM, length), (0, 3, 1, 2))[None]


def _from_state(a):
    b, length = a.shape[0], a.shape[1]
    return jnp.transpose(a, (0, 2, 3, 1)).reshape(b, GROUP_WIDTH, length)


def kernel(x_prompt, x_sample, cache_k1, cache_v1, cache_k2, cache_v2, cache_k3, cache_v3, state_conv,
           c_prompt, c_sample, norm1_g, norm2_g, normf_g, w_ada, b_ada, w_in, conv_w, w_conv_out,
           w_attn_out, w_o, w_rg, b_rg, w_re, b_re, w1, w3, w2):
    depth = w_in.shape[0]
    assert depth == 1
    bsz, seq, _ = x_prompt.shape
    dbsz, t_len, _ = x_sample.shape
    n_s = dbsz * t_len
    l = 0

    w_in_bf = w_in[l].astype(BF16)
    wco_bf = w_conv_out[l].astype(BF16)
    wao_bf = w_attn_out[l].astype(BF16)
    wo_bf = w_o[l].astype(BF16)
    w1_bf = w1[l].reshape(N_EXPERTS, D_MODEL, D_EXPERT).astype(BF16)
    w3_bf = w3[l].reshape(N_EXPERTS, D_MODEL, D_EXPERT).astype(BF16)
    w2_bf = w2[l].reshape(N_EXPERTS, D_EXPERT, D_MODEL).astype(BF16)
    g1 = norm1_g[l].reshape(1, D_MODEL)
    g2 = norm2_g[l].reshape(1, D_MODEL)
    gf = normf_g.reshape(1, D_MODEL)
    n_route = N_EXPERT_GROUPS + N_EXPERTS
    wr = jnp.pad(jnp.concatenate([w_rg[l], w_re[l]], axis=1), ((0, 0), (0, LANES - n_route)))
    br = jnp.pad(jnp.concatenate([b_rg[l], b_re[l]]), (0, LANES - n_route)).reshape(1, LANES)
    head_of_lane = np.arange(GROUP_WIDTH) // HEAD_DIM
    expand_bf = jnp.asarray((np.arange(LANES)[:, None] == head_of_lane[None, :]).astype(np.float32), dtype=BF16)

    mod = _adaln(jnp.concatenate([c_prompt, c_sample], axis=0), w_ada[l], b_ada[l])
    mod_p4 = mod[:bsz].reshape(bsz, 6, 1, D_MODEL)
    mod_tok = jnp.repeat(mod[bsz:].reshape(dbsz, 1, 6, D_MODEL), t_len, axis=1)
    mod_tok = jnp.transpose(mod_tok.reshape(n_s, 6, D_MODEL), (1, 0, 2))
    mod_s4 = mod_tok[None]

    pos_p = jnp.arange(seq, dtype=jnp.int32)
    rope_q_p = _rope_tables(pos_p, HEAD_DIM ** -0.5)
    rope_k_p = _rope_tables(pos_p, 1.0)
    pc_p, sga_p, conv_p = _convproj_prompt(x_prompt, mod_p4, g1, w_in_bf[:, :OFF_Q], w_in_bf[:, OFF_GCONV:],
                                           conv_w[l], wco_bf, tm=TM_INPROJ)
    outs = _qkvproj_prompt(x_prompt, mod_p4, g1, w_in_bf[:, OFF_Q:OFF_GCONV], rope_q_p, rope_k_p, tm=TM_INPROJ)
    qkv_p = outs[0:3 * N_DIL]
    states_p = outs[3 * N_DIL:]
    o_p, lse_p = [], []
    for g in range(N_DIL):
        o_g, lse_g = _attn_prompt_group(qkv_p[3 * g], qkv_p[3 * g + 1], qkv_p[3 * g + 2], g, max_rows=ATTN_ROWS_PER_STEP)
        o_p.append(o_g)
        lse_p.append(lse_g)
    x1_p, h2rows_p, meta_p, route_p, cnt_p = _post(o_p, lse_p, pc_p, sga_p, x_prompt, mod_p4, wao_bf, wo_bf, g2, wr, br,
                                                   expand_bf, tm=TM_POST, full_precision=False, routed=True)
    n_p = bsz * seq
    route = route_p.reshape(n_p // TM_POST, SUBLANES, TM_POST)
    starts, tables = _routing_tables(cnt_p[0, :N_CLASSES], n=n_p, tm=TM_EXPERT)
    hs = _dispatch_rows(h2rows_p.reshape(n_p * H_ROWS, LANES), route, starts, rows_per_step=ROWS_PER_DMA_STEP)
    yp = _experts_routed(hs, tables, w1_bf, w3_bf, w2_bf, tm=TM_EXPERT)
    cls_tok = jnp.transpose(route[:, ROUTE_CLASS_ROW:ROUTE_RANK_ROW + 1, :], (1, 0, 2)).reshape(2, n_p)
    slot = jnp.sum(jnp.where(cls_tok[0][:, None] == jnp.arange(N_CLASSES, dtype=jnp.int32)[None, :], starts[None, :], 0),
                   axis=1) + cls_tok[1]
    y_tok = _sc_gather_tokens(yp.reshape(n_p, Y_ROWS, LANES), slot)
    y_p = _final(y_tok.reshape(n_p * Y_ROWS, LANES), x1_p.reshape(n_p, D_MODEL), meta_p.reshape(n_p, LANES), mod_p4, gf,
                 tm=TM_FINAL, rows_per_mod=seq).reshape(bsz, seq, D_MODEL)

    pos_s = PAST_LEN + jnp.arange(t_len, dtype=jnp.int32)
    rope_q_s = jnp.tile(_rope_tables(pos_s, HEAD_DIM ** -0.5), (1, dbsz, 1))
    rope_k_s = jnp.tile(_rope_tables(pos_s, 1.0), (1, dbsz, 1))
    xs = x_sample.reshape(n_s, D_MODEL)
    z_s = _s_inproj(xs, mod_tok, g1, w_in[l], col_block=S_COL_BLOCK)
    past = state_conv[l]
    p0e = jnp.repeat(past[:, 0], t_len, axis=0)
    p1e = jnp.repeat(past[:, 1], t_len, axis=0)
    cu_s, pc_s, sga_s, q_s, k_s, v_s = _s_mid(z_s, p0e, p1e, conv_w[l], w_conv_out[l], rope_q_s, rope_k_s,
                                              t_len=t_len)
    to_cols = lambda a: jnp.transpose(a.reshape(dbsz, t_len, ATTN_WIDTH), (0, 2, 1))
    q_t, kn_t, vn_t = to_cols(q_s), to_cols(k_s), to_cols(v_s)
    k_tail = jnp.pad(kn_t, ((0, 0), (0, 0), (LANES - t_len, 0)))
    v_tail = jnp.pad(vn_t, ((0, 0), (0, 0), (LANES - t_len, 0)))
    caches = ((cache_k1, cache_v1), (cache_k2, cache_v2), (cache_k3, cache_v3))
    o_s, lse_s, kv_s = [], [], []
    for g, (ck, cv) in enumerate(caches):
        o_g, lse_g, ko, vo = _s_attn_group(q_t, kn_t, vn_t, k_tail, v_tail, _from_state(ck[l]), _from_state(cv[l]), g)
        o_s.append(jnp.transpose(o_g, (0, 2, 1)).reshape(1, 1, n_s, GROUP_WIDTH))
        lse_rows = jnp.transpose(lse_g, (0, 2, 1)).reshape(n_s, HEADS_PER_GROUP)
        lse_s.append(jnp.pad(lse_rows, ((0, 0), (0, LANES - HEADS_PER_GROUP))).reshape(1, 1, n_s, LANES))
        kv_s += [_to_state(ko), _to_state(vo)]
    x1_s, h2_s, comb_s = _post(o_s, lse_s, pc_s[None], sga_s[None], xs[None], mod_s4, w_attn_out[l], w_o[l], g2, wr, br,
                               expand_bf, tm=n_s, full_precision=True, routed=False)
    y_s = _moe_dense(h2_s, comb_s, x1_s, mod_s4, gf, w1_bf, w3_bf, w2_bf, tm=n_s)

    conv_s = cu_s.reshape(dbsz, t_len, D_MODEL)[:, t_len - (CONV_K - 1):]
    return (y_p, y_s.reshape(dbsz, t_len, D_MODEL),
            *[_to_state(a) for a in states_p], conv_p.reshape(1, bsz, CONV_K - 1, D_MODEL),
            *kv_s, conv_s.reshape(1, dbsz, CONV_K - 1, D_MODEL))
```

```python
import functools

import numpy as np
import jax
import jax.numpy as jnp
from jax import lax
from jax.experimental import pallas as pl
from jax.experimental.pallas import tpu as pltpu

F32 = jnp.float32
BF16 = jnp.bfloat16
HIGHEST = lax.Precision.HIGHEST

D_MODEL = 1024
HEAD_DIM = 64
HEADS_PER_GROUP = 8
GROUP_WIDTH = HEADS_PER_GROUP * HEAD_DIM
DIL_GROUPS = ((128, 1), (512, 4), (2048, 16))
N_DIL = len(DIL_GROUPS)
ATTN_WIDTH = N_DIL * GROUP_WIDTH
ROT_DIM = HEAD_DIM // 4
ROPE_THETA = 500000.0
PAST_LEN = 16384
CONV_K = 3
N_EXPERT_GROUPS = 4
EXPERTS_PER_GROUP = 4
N_EXPERTS = N_EXPERT_GROUPS * EXPERTS_PER_GROUP
D_EXPERT = 512
RMS_EPS = 1e-6
IN_COLS = 3 * D_MODEL + 3 * ATTN_WIDTH + 2 * D_MODEL
OFF_U, OFF_GC, OFF_GB = 0, D_MODEL, 2 * D_MODEL
OFF_Q = 3 * D_MODEL
OFF_K = OFF_Q + ATTN_WIDTH
OFF_V = OFF_K + ATTN_WIDTH
OFF_GCONV = OFF_V + ATTN_WIDTH
OFF_GATTN = OFF_GCONV + D_MODEL

LANES = 128
SUBLANES = 8
CHUNKS_PER_GROUP = GROUP_WIDTH // LANES
KEYS_PER_QUERY = 129
Q_BLOCK = 128
NEG_BIG = -1e30

VMEM_LIMIT = 56 * 1024 * 1024


def _sigmoid(x):
    return 1.0 / (1.0 + jnp.exp(-x))


def _cparams(sem):
    return pltpu.CompilerParams(dimension_semantics=sem, vmem_limit_bytes=VMEM_LIMIT)


def _adaln_kernel(c_ref, w_ref, b_ref, o_ref):
    c = c_ref[...]
    s = c * _sigmoid(c)
    o_ref[...] = jnp.dot(s, w_ref[...], precision=HIGHEST, preferred_element_type=F32) + b_ref[...]


def _adaln(c_all, w_ada, b_ada):
    rows = c_all.shape[0]
    n_col = w_ada.shape[1] // D_MODEL
    return pl.pallas_call(
        _adaln_kernel,
        grid=(n_col,),
        in_specs=[
            pl.BlockSpec((rows, D_MODEL), lambda j: (0, 0)),
            pl.BlockSpec((D_MODEL, D_MODEL), lambda j: (0, j)),
            pl.BlockSpec((1, D_MODEL), lambda j: (0, j)),
        ],
        out_specs=pl.BlockSpec((rows, D_MODEL), lambda j: (0, j)),
        out_shape=jax.ShapeDtypeStruct((rows, w_ada.shape[1]), F32),
        compiler_params=_cparams(("arbitrary",)),
        name="adaln",
    )(c_all, w_ada, b_ada.reshape(1, -1))


def _rope_tables(pos, scale):
    half = ROT_DIM // 2
    inv_freq = jnp.power(jnp.float32(ROPE_THETA), -jnp.arange(half, dtype=F32) / half)
    ang = pos.astype(F32)[:, None] * inv_freq[None, :]
    cos, sin = jnp.cos(ang), jnp.sin(ang)
    lane_in_head = np.arange(LANES) % HEAD_DIM
    freq = lane_in_head % half
    first = lane_in_head < half
    second = (lane_in_head >= half) & (lane_in_head < ROT_DIM)
    a = jnp.where(first | second, cos[:, freq], 1.0)
    bm = jnp.where(first, -sin[:, freq], 0.0)
    bp = jnp.where(second, sin[:, freq], 0.0)
    return jnp.stack([a, bm, bp]) * scale


def _rope_chunk(zc, a, bm, bp):
    return zc * a + pltpu.roll(zc, LANES - ROT_DIM // 2, 1) * bm + pltpu.roll(zc, ROT_DIM // 2, 1) * bp


def _modulated_norm_bf16(x_ref, mod_ref, g1_ref):
    x = x_ref[...]
    var = jnp.mean(x * x, axis=-1, keepdims=True)
    h = (x * lax.rsqrt(var + RMS_EPS)) * (g1_ref[...] * (1.0 + mod_ref[1])) + mod_ref[0]
    return h.astype(BF16)


def _convproj_kernel(x_ref, mod_ref, g1_ref, wa_ref, wg_ref, cw_ref, wco_ref, pc_ref, sga_ref, cst_ref, s_ref,
                     *, tm, n_tiles):
    i = pl.program_id(1)
    hb = _modulated_norm_bf16(x_ref, mod_ref, g1_ref)

    def proj(w_ref, lo):
        return jnp.dot(hb, w_ref[:, lo:lo + D_MODEL], preferred_element_type=F32)

    cu = proj(wa_ref, OFF_GC) * proj(wa_ref, OFF_U)

    @pl.when(i == 0)
    def _():
        s_ref[0:SUBLANES, :] = jnp.zeros((SUBLANES, D_MODEL), F32)

    s_ref[SUBLANES:SUBLANES + tm, :] = cu
    cw = cw_ref[...]
    conv = (cw[0:1] * s_ref[SUBLANES - 2:SUBLANES - 2 + tm, :]
            + cw[1:2] * s_ref[SUBLANES - 1:SUBLANES - 1 + tm, :]
            + cw[2:3] * cu)
    yc = jnp.dot((proj(wa_ref, OFF_GB) * conv).astype(BF16), wco_ref[...], preferred_element_type=F32)
    pc_ref[...] = (_sigmoid(proj(wg_ref, 0)) * yc).astype(pc_ref.dtype)
    sga_ref[...] = _sigmoid(proj(wg_ref, D_MODEL)).astype(sga_ref.dtype)

    @pl.when(i == n_tiles - 1)
    def _():
        cst_ref[...] = s_ref[tm + SUBLANES - 2:tm + SUBLANES, :]

    s_ref[0:SUBLANES, :] = s_ref[tm:tm + SUBLANES, :]


def _convproj_prompt(x, mod4, g1, wa_bf, wg_bf, conv_w, wco_bf, *, tm):
    bsz, seq, _ = x.shape
    n_tiles = seq // tm
    const2 = lambda b, i: (0, 0)
    tile3 = lambda b, i: (b, i, 0)
    return pl.pallas_call(
        functools.partial(_convproj_kernel, tm=tm, n_tiles=n_tiles),
        grid=(bsz, n_tiles),
        in_specs=[
            pl.BlockSpec((None, tm, D_MODEL), tile3),
            pl.BlockSpec((None, 6, 1, D_MODEL), lambda b, i: (b, 0, 0, 0)),
            pl.BlockSpec((1, D_MODEL), const2),
            pl.BlockSpec(wa_bf.shape, const2, pipeline_mode=pl.Buffered(1)),
            pl.BlockSpec(wg_bf.shape, const2, pipeline_mode=pl.Buffered(1)),
            pl.BlockSpec((CONV_K, D_MODEL), const2),
            pl.BlockSpec((D_MODEL, D_MODEL), const2, pipeline_mode=pl.Buffered(1)),
        ],
        out_specs=[pl.BlockSpec((None, tm, D_MODEL), tile3), pl.BlockSpec((None, tm, D_MODEL), tile3),
                   pl.BlockSpec((None, CONV_K - 1, D_MODEL), lambda b, i: (b, 0, 0))],
        out_shape=[jax.ShapeDtypeStruct((bsz, seq, D_MODEL), BF16), jax.ShapeDtypeStruct((bsz, seq, D_MODEL), BF16),
                   jax.ShapeDtypeStruct((bsz, CONV_K - 1, D_MODEL), F32)],
        scratch_shapes=[pltpu.VMEM((tm + SUBLANES, D_MODEL), F32)],
        compiler_params=_cparams(("arbitrary", "arbitrary")),
        name="convproj_prompt",
    )(x, mod4, g1, wa_bf, wg_bf, conv_w, wco_bf)


def _qkvproj_kernel(x_ref, mod_ref, g1_ref, w_ref, rq_ref, rk_ref, *rest, tm, n_tiles, seq):
    qkv_refs = rest[0:3 * N_DIL]
    st_refs = rest[3 * N_DIL:3 * N_DIL + 2 * N_DIL]
    d_ref = rest[-1]
    i = pl.program_id(1)
    hb = _modulated_norm_bf16(x_ref, mod_ref, g1_ref)

    def proj(lo, width):
        return jnp.dot(hb, w_ref[:, lo:lo + width], preferred_element_type=F32)

    zq = proj(0, ATTN_WIDTH)
    zk = proj(ATTN_WIDTH, ATTN_WIDTH)
    zv = proj(2 * ATTN_WIDTH, ATTN_WIDTH)
    aq, bmq, bpq = rq_ref[0], rq_ref[1], rq_ref[2]
    ak, bmk, bpk = rk_ref[0], rk_ref[1], rk_ref[2]
    n_chunks = ATTN_WIDTH // LANES
    q_chunks, k_chunks, v_chunks = [], [], []
    for c in range(n_chunks):
        sl = slice(c * LANES, (c + 1) * LANES)
        q_chunks.append(_rope_chunk(zq[:, sl], aq, bmq, bpq))
        k_chunks.append(_rope_chunk(zk[:, sl], ak, bmk, bpk))
        v_chunks.append(zv[:, sl])

    for which, chunks in enumerate((q_chunks, k_chunks, v_chunks)):
        for g in range(N_DIL):
            out_ref = qkv_refs[3 * g + which]
            dil = DIL_GROUPS[g][1]
            for cc in range(CHUNKS_PER_GROUP):
                c = g * CHUNKS_PER_GROUP + cc
                sl = slice(cc * LANES, (cc + 1) * LANES)
                if dil == 1:
                    out_ref[0, :, sl] = chunks[c].astype(BF16)
                else:
                    d_ref[c] = chunks[c]
                    for r in range(dil):
                        out_ref[r, :, sl] = d_ref[c, pl.ds(r, tm // dil, stride=dil), :].astype(BF16)

    for g in range(N_DIL):
        kst, vst = st_refs[2 * g], st_refs[2 * g + 1]
        win = min(DIL_GROUPS[g][0], seq)
        if win >= tm:
            cond, r0 = i >= (seq - win) // tm, 0
        else:
            cond, r0 = i == n_tiles - 1, tm - win

        @pl.when(cond)
        def _(g=g, kst=kst, vst=vst, r0=r0):
            for cc in range(CHUNKS_PER_GROUP):
                c = g * CHUNKS_PER_GROUP + cc
                kst[cc * LANES:(cc + 1) * LANES, :] = k_chunks[c][r0:, :].T
                vst[cc * LANES:(cc + 1) * LANES, :] = v_chunks[c][r0:, :].T


def _qkvproj_prompt(x, mod4, g1, wqkv_bf, rope_q, rope_k, *, tm):
    bsz, seq, _ = x.shape
    n_tiles = seq // tm
    const2 = lambda b, i: (0, 0)
    tile3 = lambda b, i: (b, i, 0)
    in_specs = [
        pl.BlockSpec((None, tm, D_MODEL), tile3),
        pl.BlockSpec((None, 6, 1, D_MODEL), lambda b, i: (b, 0, 0, 0)),
        pl.BlockSpec((1, D_MODEL), const2),
        pl.BlockSpec(wqkv_bf.shape, const2, pipeline_mode=pl.Buffered(1)),
        pl.BlockSpec((3, tm, LANES), lambda b, i: (0, i, 0)),
        pl.BlockSpec((3, tm, LANES), lambda b, i: (0, i, 0)),
    ]
    out_shape, out_specs = [], []
    for _, dil in DIL_GROUPS:
        assert tm % (dil * 16) == 0
        for _ in range(3):
            out_shape.append(jax.ShapeDtypeStruct((bsz, dil, seq // dil, GROUP_WIDTH), BF16))
            out_specs.append(pl.BlockSpec((None, dil, tm // dil, GROUP_WIDTH), lambda b, i: (b, 0, i, 0)))
    for win, _ in DIL_GROUPS:
        win = min(win, seq)
        cols = min(win, tm)
        if win >= tm:
            imap = lambda b, i, ft=(seq - win) // tm: (b, 0, jnp.maximum(i - ft, 0))
        else:
            imap = lambda b, i: (b, 0, 0)
        for _ in range(2):
            out_shape.append(jax.ShapeDtypeStruct((bsz, GROUP_WIDTH, win), F32))
            out_specs.append(pl.BlockSpec((None, GROUP_WIDTH, cols), imap))
    return pl.pallas_call(
        functools.partial(_qkvproj_kernel, tm=tm, n_tiles=n_tiles, seq=seq),
        grid=(bsz, n_tiles),
        in_specs=in_specs,
        out_specs=out_specs,
        out_shape=out_shape,
        scratch_shapes=[pltpu.VMEM((ATTN_WIDTH // LANES, tm, LANES), F32)],
        compiler_params=_cparams(("arbitrary", "arbitrary")),
        name="qkvproj_prompt",
    )(x, mod4, g1, wqkv_bf, rope_q, rope_k)


ATTN_UNROLL = 4


def _attn_kernel(*refs, chunk, has_halo, n_cls):
    if has_halo:
        q_ref, k_ref, v_ref, kh_ref, vh_ref, o_ref, lse_ref, kbuf, vbuf = refs
    else:
        q_ref, k_ref, v_ref, o_ref, lse_ref, kbuf, vbuf = refs
    c = pl.program_id(2)
    if has_halo:
        kbuf[:, 0:Q_BLOCK, :] = kh_ref[...]
        vbuf[:, 0:Q_BLOCK, :] = vh_ref[...]
    else:
        kbuf[:, 0:Q_BLOCK, :] = jnp.zeros((n_cls, Q_BLOCK, GROUP_WIDTH), BF16)
        vbuf[:, 0:Q_BLOCK, :] = jnp.zeros((n_cls, Q_BLOCK, GROUP_WIDTH), BF16)
    kbuf[:, Q_BLOCK:Q_BLOCK + chunk, :] = k_ref[...]
    vbuf[:, Q_BLOCK:Q_BLOCK + chunk, :] = v_ref[...]
    blocks_per_cls = chunk // Q_BLOCK

    row = lax.broadcasted_iota(jnp.int32, (Q_BLOCK, 2 * Q_BLOCK), 0)
    col = lax.broadcasted_iota(jnp.int32, (Q_BLOCK, 2 * Q_BLOCK), 1)
    bias_main = jnp.where(col >= row, jnp.where(col <= row + Q_BLOCK, 0.0, NEG_BIG), NEG_BIG)
    bias_first = jnp.where(col >= Q_BLOCK, bias_main, NEG_BIG)
    lane = lax.broadcasted_iota(jnp.int32, (1, LANES), 1)
    lo_half = lane < HEAD_DIM

    def body(u, carry):
        cls, qb = u // blocks_per_cls, u % blocks_per_cls
        r0 = pl.multiple_of(qb * Q_BLOCK, Q_BLOCK)
        kt = kbuf[cls, pl.ds(r0, 2 * Q_BLOCK), :]
        vt = vbuf[cls, pl.ds(r0, 2 * Q_BLOCK), :]
        qt = q_ref[cls, pl.ds(r0, Q_BLOCK), :]
        is_first = jnp.logical_and(qb == 0, c == 0)
        bias = jnp.where(is_first, bias_first, bias_main)
        bias2 = jnp.concatenate([bias, bias], axis=0)
        lse_tile = jnp.zeros((Q_BLOCK, LANES), F32)
        for p in range(CHUNKS_PER_GROUP):
            sl = slice(p * LANES, (p + 1) * LANES)
            qp, kp, vp = qt[:, sl], kt[:, sl], vt[:, sl]
            zero = jnp.zeros_like(qp)
            q2 = jnp.concatenate([jnp.where(lo_half, qp, zero), jnp.where(lo_half, zero, qp)], axis=0)
            s = lax.dot_general(q2, kp, (((1,), (1,)), ((), ())), preferred_element_type=F32) + bias2
            m = jnp.max(s, axis=1, keepdims=True)
            e = jnp.exp(s - m)
            l = jnp.sum(e, axis=1, keepdims=True)
            o = jnp.dot(e.astype(BF16), vp, preferred_element_type=F32) * (1.0 / l)
            lse = m + jnp.log(l)
            for hh in range(2):
                lse_tile = jnp.where(lane == 2 * p + hh, lse[hh * Q_BLOCK:(hh + 1) * Q_BLOCK], lse_tile)
            o_ref[cls, pl.ds(r0, Q_BLOCK), sl] = jnp.where(lo_half, o[:Q_BLOCK], o[Q_BLOCK:])
        lse_ref[cls, pl.ds(r0, Q_BLOCK), :] = lse_tile
        return carry

    n_units = n_cls * blocks_per_cls
    lax.fori_loop(0, n_units, body, 0, unroll=min(ATTN_UNROLL, n_units))


def _attn_prompt_group(q, k, v, g, *, max_rows):
    bsz, dil, cls_len, _ = q.shape
    chunk = min(max_rows, cls_len)
    n_chunks = cls_len // chunk
    n_cls = min(dil, max_rows // chunk)
    has_halo = n_chunks > 1
    main = pl.BlockSpec((None, n_cls, chunk, GROUP_WIDTH), lambda b, r, c: (b, r, c, 0))
    in_specs = [main, main, main]
    args = [q, k, v]
    if has_halo:
        per = chunk // Q_BLOCK
        halo = pl.BlockSpec((None, n_cls, Q_BLOCK, GROUP_WIDTH), lambda b, r, c: (b, r, jnp.maximum(c * per - 1, 0), 0))
        in_specs += [halo, halo]
        args += [k, v]
    return pl.pallas_call(
        functools.partial(_attn_kernel, chunk=chunk, has_halo=has_halo, n_cls=n_cls),
        grid=(bsz, dil // n_cls, n_chunks),
        in_specs=in_specs,
        out_specs=[pl.BlockSpec((None, n_cls, chunk, GROUP_WIDTH), lambda b, r, c: (b, r, c, 0)),
                   pl.BlockSpec((None, n_cls, chunk, LANES), lambda b, r, c: (b, r, c, 0))],
        out_shape=[jax.ShapeDtypeStruct((bsz, dil, cls_len, GROUP_WIDTH), F32),
                   jax.ShapeDtypeStruct((bsz, dil, cls_len, LANES), F32)],
        scratch_shapes=[pltpu.VMEM((n_cls, chunk + Q_BLOCK, GROUP_WIDTH), BF16),
                        pltpu.VMEM((n_cls, chunk + Q_BLOCK, GROUP_WIDTH), BF16)],
        compiler_params=_cparams(("arbitrary", "arbitrary", "arbitrary")),
        name=f"attn_prompt_g{g}",
    )(*args)


def _split_bf16(x, n):
    parts = []
    r = x
    for _ in range(n):
        p = r.astype(BF16)
        parts.append(p)
        r = r - p.astype(F32)
    return parts


def _store_token_rows(ref, value, n_tokens, first_chunk=0, rows_per_token=None):
    n_chunks = value.shape[1] // LANES
    rows_per_token = rows_per_token or n_chunks
    for c in range(n_chunks):
        ref[pl.ds(first_chunk + c, n_tokens, stride=rows_per_token), :] = value[:, c * LANES:(c + 1) * LANES]


def _load_token_rows(ref, n_tokens, n_chunks, first_chunk=0, rows_per_token=None, lead=None):
    rows_per_token = rows_per_token or n_chunks

    def chunk(c):
        rows = pl.ds(first_chunk + c, n_tokens, stride=rows_per_token)
        return ref[rows, :] if lead is None else ref[lead, rows, :]

    return jnp.concatenate([chunk(c) for c in range(n_chunks)], axis=1)


H_ROWS = D_MODEL // LANES
Y_ROWS = 2 * D_MODEL // LANES

_PAIRS = ((0, 1), (0, 2), (0, 3), (1, 2), (1, 3), (2, 3))
N_CLASSES = N_EXPERT_GROUPS * len(_PAIRS)
META_CLASS, META_RANK, META_WA, META_WB = 0, 1, 2, 3


def _post_kernel(*refs, tm, dils, full_precision, routed):
    (o0_ref, o1_ref, o2_ref, l0_ref, l1_ref, l2_ref, pc_ref, sga_ref, x_ref, mod_ref,
     wao_ref, wo_ref, g2_ref, wr_ref, wrh_ref, wrl_ref, br_ref, exp_ref) = refs[:18]
    if routed:
        tri_ref, x1_ref, h2_ref, meta_ref, route_ref, cnt_ref, o_scr, l_scr, run_scr = refs[18:]
    else:
        x1_ref, h2_ref, comb_ref, o_scr, l_scr = refs[18:]

    def mm(a, w_ref):
        if full_precision:
            return jnp.dot(a, w_ref[...], precision=HIGHEST, preferred_element_type=F32)
        return jnp.dot(a.astype(BF16), w_ref[...], preferred_element_type=F32)

    def natural_order(ref, scr, dil, n_chunks):
        if dil == 1:
            return [ref[0, :, c * LANES:(c + 1) * LANES] for c in range(n_chunks)]
        out = []
        for c in range(n_chunks):
            for r in range(dil):
                scr[c, pl.ds(r, tm // dil, stride=dil), :] = ref[r, :, c * LANES:(c + 1) * LANES]
            out.append(scr[c])
        return out

    lses = [natural_order(ref, l_scr.at[g], dils[g], 1)[0] for g, ref in enumerate((l0_ref, l1_ref, l2_ref))]
    mx = jnp.maximum(lses[0], jnp.maximum(lses[1], lses[2]))
    es = [jnp.exp(v - mx) for v in lses]
    inv = 1.0 / (es[0] + es[1] + es[2])
    expand = exp_ref[...]
    attn_o = None
    for g, o_ref in enumerate((o0_ref, o1_ref, o2_ref)):
        w = es[g] * inv
        we = None
        for part in _split_bf16(w, 3 if full_precision else 2):
            t = jnp.dot(part, expand, preferred_element_type=F32)
            we = t if we is None else we + t
        o_nat = jnp.concatenate(natural_order(o_ref, o_scr.at[g], dils[g], CHUNKS_PER_GROUP), axis=1)
        term = we * o_nat
        attn_o = term if attn_o is None else attn_o + term

    y_attn = mm(attn_o, wao_ref)
    mixed = mm(pc_ref[...] + sga_ref[...] * y_attn, wo_ref)
    x1 = x_ref[...] + mod_ref[2] * mixed
    x1_ref[...] = x1
    var = jnp.mean(x1 * x1, axis=-1, keepdims=True)
    h2 = (x1 * lax.rsqrt(var + RMS_EPS)) * (g2_ref[...] * (1.0 + mod_ref[4])) + mod_ref[3]
    if routed:
        _store_token_rows(h2_ref, h2, tm)
    else:
        h2_ref[...] = h2.astype(h2_ref.dtype)

    if full_precision:
        lg = jnp.dot(h2, wr_ref[...], precision=HIGHEST, preferred_element_type=F32) + br_ref[...]
    else:
        h_hi, h_lo = _split_bf16(h2, 2)
        lg = (jnp.dot(h_hi, wrh_ref[...], preferred_element_type=F32)
              + jnp.dot(h_lo, wrh_ref[...], preferred_element_type=F32)
              + jnp.dot(h_hi, wrl_ref[...], preferred_element_type=F32)) + br_ref[...]
    lane_i = lax.broadcasted_iota(jnp.int32, (1, LANES), 1)
    lane = lane_i.astype(F32)
    lane_group = ((lane_i - N_EXPERT_GROUPS) >> 2).astype(F32)
    big = jnp.float32(1e9)
    gl = jnp.where(lane_i < N_EXPERT_GROUPS, lg, NEG_BIG)
    gmax = jnp.max(gl, axis=1, keepdims=True)
    gidx = jnp.min(jnp.where(gl == gmax, lane, big), axis=1, keepdims=True)
    g_w = 1.0 / jnp.sum(jnp.exp(gl - gmax), axis=1, keepdims=True)
    el = jnp.where(lane_group == gidx, lg, NEG_BIG)
    v1 = jnp.max(el, axis=1, keepdims=True)
    i1 = jnp.min(jnp.where(el == v1, lane, big), axis=1, keepdims=True)
    el2 = jnp.where(lane == i1, NEG_BIG, el)
    v2 = jnp.max(el2, axis=1, keepdims=True)
    i2 = jnp.min(jnp.where(el2 == v2, lane, big), axis=1, keepdims=True)
    t = jnp.exp(v2 - v1)
    den = 1.0 / (1.0 + t)
    w_top1, w_top2 = g_w * den, g_w * (t * den)
    if not routed:
        comb_ref[...] = jnp.where(lane == i1, w_top1, jnp.where(lane == i2, w_top2, 0.0))
        return

    base = jnp.float32(N_EXPERT_GROUPS) + jnp.float32(EXPERTS_PER_GROUP) * gidx
    e1, e2 = i1 - base, i2 - base
    first_is_lower = e1 < e2
    ea = jnp.where(first_is_lower, e1, e2)
    eb = jnp.where(first_is_lower, e2, e1)
    pair = ea * (7.0 - ea) * 0.5 + (eb - ea - 1.0)
    cls = gidx * jnp.float32(len(_PAIRS)) + pair
    onehot = lane == cls
    earlier = jnp.dot(tri_ref[...], jnp.where(onehot, 1.0, 0.0).astype(BF16), preferred_element_type=F32)

    @pl.when(jnp.logical_and(pl.program_id(0) == 0, pl.program_id(1) == 0))
    def _():
        run_scr[...] = jnp.zeros_like(run_scr)

    running = run_scr[...]
    rank = jnp.sum(jnp.where(onehot, earlier + running, 0.0), axis=1, keepdims=True)
    running = running + jnp.sum(jnp.where(onehot, 1.0, 0.0), axis=0, keepdims=True)
    run_scr[...] = running
    cnt_ref[...] = running
    w_a = jnp.where(first_is_lower, w_top1, w_top2)
    w_b = jnp.where(first_is_lower, w_top2, w_top1)
    meta = jnp.where(lane_i == META_CLASS, cls,
                     jnp.where(lane_i == META_RANK, rank,
                               jnp.where(lane_i == META_WA, w_a, jnp.where(lane_i == META_WB, w_b, 0.0))))
    meta_ref[...] = meta
    route_ref[...] = meta.T[0:SUBLANES, :].astype(jnp.int32)


def _post(o_list, lse_list, pc, sga, x, mod4, wao, wo, g2, wr, br, expand, *, tm, full_precision, routed):
    nb, rows, _ = x.shape
    tiles = rows // tm
    dils = tuple(o.shape[1] for o in o_list)
    tile3 = lambda b, i: (b, i, 0)
    const2 = lambda b, i: (0, 0)
    mod_rows = mod4.shape[2]
    if mod_rows == 1:
        mod_spec = pl.BlockSpec((None, 6, 1, D_MODEL), lambda b, i: (b, 0, 0, 0))
    else:
        mod_spec = pl.BlockSpec((None, 6, tm, D_MODEL), lambda b, i: (b, 0, i, 0))
    cls4 = lambda b, i: (b, 0, i, 0)
    in_specs = (
        [pl.BlockSpec((None, d, tm // d, GROUP_WIDTH), cls4) for d in dils]
        + [pl.BlockSpec((None, d, tm // d, LANES), cls4) for d in dils]
        + [pl.BlockSpec((None, tm, D_MODEL), tile3)] * 3 + [mod_spec]
        + [pl.BlockSpec(wao.shape, const2), pl.BlockSpec(wo.shape, const2), pl.BlockSpec((1, D_MODEL), const2)]
        + [pl.BlockSpec((D_MODEL, LANES), const2)] * 3
        + [pl.BlockSpec((1, LANES), const2), pl.BlockSpec((LANES, GROUP_WIDTH), const2)]
    )
    wr_hi = wr.astype(BF16)
    wr_lo = (wr - wr_hi.astype(F32)).astype(BF16)
    args = [*o_list, *lse_list, pc, sga, x, mod4, wao, wo, g2, wr, wr_hi, wr_lo, br, expand]
    scratch = [pltpu.VMEM((N_DIL, CHUNKS_PER_GROUP, tm, LANES), F32), pltpu.VMEM((N_DIL, 1, tm, LANES), F32)]
    if routed:
        tri = jnp.asarray(np.tril(np.ones((tm, tm), np.float32), -1), dtype=BF16)
        args.append(tri)
        in_specs = in_specs + [pl.BlockSpec((tm, tm), const2)]
        out_specs = [pl.BlockSpec((None, tm, D_MODEL), tile3), pl.BlockSpec((None, tm * H_ROWS, LANES), tile3),
                     pl.BlockSpec((None, tm, LANES), tile3),
                     pl.BlockSpec((None, None, SUBLANES, tm), lambda b, i: (b, i, 0, 0)),
                     pl.BlockSpec((1, LANES), const2)]
        out_shape = [jax.ShapeDtypeStruct((nb, rows, D_MODEL), F32),
                     jax.ShapeDtypeStruct((nb, rows * H_ROWS, LANES), F32),
                     jax.ShapeDtypeStruct((nb, rows, LANES), F32),
                     jax.ShapeDtypeStruct((nb, tiles, SUBLANES, tm), jnp.int32),
                     jax.ShapeDtypeStruct((1, LANES), F32)]
        scratch.append(pltpu.VMEM((1, LANES), F32))
    else:
        out_specs = [pl.BlockSpec((None, tm, D_MODEL), tile3), pl.BlockSpec((None, tm, D_MODEL), tile3),
                     pl.BlockSpec((None, tm, LANES), tile3)]
        out_shape = [jax.ShapeDtypeStruct((nb, rows, D_MODEL), F32), jax.ShapeDtypeStruct((nb, rows, D_MODEL), F32),
                     jax.ShapeDtypeStruct((nb, rows, LANES), F32)]
    return pl.pallas_call(
        functools.partial(_post_kernel, tm=tm, dils=dils, full_precision=full_precision, routed=routed),
        grid=(nb, tiles),
        in_specs=in_specs,
        out_specs=out_specs,
        out_shape=out_shape,
        scratch_shapes=scratch,
        compiler_params=_cparams(("arbitrary", "arbitrary")),
        name="post_sample" if full_precision else "post_prompt",
    )(*args)


DMA_LOOP_UNROLL = 8


ROUTE_CLASS_ROW, ROUTE_RANK_ROW = META_CLASS, META_RANK


def _sorted_slot(starts_ref, route_ref, r):
    return starts_ref[route_ref[ROUTE_CLASS_ROW, r]] + route_ref[ROUTE_RANK_ROW, r]


def _token_rows(token, rows_per_token):
    return pl.ds(pl.multiple_of(token * rows_per_token, rows_per_token), rows_per_token)


def _dispatch_kernel(starts_ref, route_ref, src_ref, dst_hbm, buf, sems, *, rows_per_step, n_steps):
    i = pl.program_id(0)
    slot = i % 2

    def wait_slot(s):
        def body(r, carry):
            pltpu.make_async_copy(buf.at[s, _token_rows(0, H_ROWS)], dst_hbm.at[_token_rows(0, H_ROWS)],
                                  sems.at[s * rows_per_step + r]).wait()
            return carry
        lax.fori_loop(0, rows_per_step, body, 0, unroll=DMA_LOOP_UNROLL)

    @pl.when(i >= 1)
    def _():
        wait_slot(1 - slot)

    buf[slot] = src_ref[...]

    def issue(r, carry):
        pltpu.make_async_copy(buf.at[slot, _token_rows(r, H_ROWS)],
                              dst_hbm.at[_token_rows(_sorted_slot(starts_ref, route_ref, r), H_ROWS)],
                              sems.at[slot * rows_per_step + r]).start()
        return carry

    lax.fori_loop(0, rows_per_step, issue, 0, unroll=DMA_LOOP_UNROLL)

    @pl.when(i == n_steps - 1)
    def _():
        wait_slot(slot)


def _route_spec(rows_per_step, route_tile, shift=0, n_steps=None):
    per_tile = route_tile // rows_per_step

    def imap(i, starts):
        s = i + shift if n_steps is None else jnp.minimum(i + shift, n_steps - 1)
        return (s // per_tile, 0, s % per_tile)

    return pl.BlockSpec((None, SUBLANES, rows_per_step), imap, memory_space=pltpu.SMEM)


def _dispatch_rows(src, route, starts, *, rows_per_step):
    n = src.shape[0] // H_ROWS
    n_steps = n // rows_per_step
    grid_spec = pltpu.PrefetchScalarGridSpec(
        num_scalar_prefetch=1,
        grid=(n_steps,),
        in_specs=[_route_spec(rows_per_step, route.shape[2]),
                  pl.BlockSpec((rows_per_step * H_ROWS, LANES), lambda i, starts: (i, 0))],
        out_specs=pl.BlockSpec(memory_space=pl.ANY),
        scratch_shapes=[pltpu.VMEM((2, rows_per_step * H_ROWS, LANES), src.dtype),
                        pltpu.SemaphoreType.DMA((2 * rows_per_step,))],
    )
    return pl.pallas_call(
        functools.partial(_dispatch_kernel, rows_per_step=rows_per_step, n_steps=n_steps),
        grid_spec=grid_spec,
        out_shape=jax.ShapeDtypeStruct(src.shape, src.dtype),
        compiler_params=_cparams(("arbitrary",)),
        name="moe_dispatch",
    )(starts, route, src)


def _experts_kernel(tile_ref, ea_ref, eb_ref, lo_ref, hi_ref, first_ref, switch_ref, nv_ref,
                    x_ref, w1a_ref, w1b_ref, w3a_ref, w3b_ref, w2a_ref, w2b_ref, y_ref, w13_scr, w2_scr, *, tm):
    i = pl.program_id(0)

    @pl.when(i < nv_ref[0])
    def _():
        @pl.when(switch_ref[i] == 1)
        def _():
            for k, w_ref in enumerate((w1a_ref, w3a_ref, w1b_ref, w3b_ref)):
                w13_scr[k] = w_ref[...].astype(BF16)
            for k, w_ref in enumerate((w2a_ref, w2b_ref)):
                w2_scr[k] = w_ref[...].astype(BF16)

        x = _load_token_rows(x_ref, tm, H_ROWS).astype(BF16)

        def expert(k):
            a = jnp.dot(x, w13_scr[2 * k], preferred_element_type=F32)
            b = jnp.dot(x, w13_scr[2 * k + 1], preferred_element_type=F32)
            hid = (a * _sigmoid(a)) * b
            return jnp.dot(hid.astype(BF16), w2_scr[k], preferred_element_type=F32)

        y_both = jnp.concatenate([expert(0), expert(1)], axis=1)

        @pl.when(first_ref[i] == 1)
        def _():
            _store_token_rows(y_ref, y_both, tm)

        @pl.when(first_ref[i] == 0)
        def _():
            row = lax.broadcasted_iota(jnp.int32, (tm, 1), 0)
            old = _load_token_rows(y_ref, tm, Y_ROWS)
            _store_token_rows(y_ref, jnp.where(row >= lo_ref[i], jnp.where(row < hi_ref[i], y_both, old), old), tm)


def _experts_routed(hs, tables, w1, w3, w2, *, tm):
    n = hs.shape[0] // H_ROWS
    n_items = tables[0].shape[0]
    x_map = lambda i, tile, ea, eb, lo, hi, first, switch, nv: (tile[i], 0)
    wa_map = lambda i, tile, ea, eb, lo, hi, first, switch, nv: (ea[i], 0, 0)
    wb_map = lambda i, tile, ea, eb, lo, hi, first, switch, nv: (eb[i], 0, 0)
    w13 = lambda m: pl.BlockSpec((None, D_MODEL, D_EXPERT), m)
    w2s = lambda m: pl.BlockSpec((None, D_EXPERT, D_MODEL), m)
    grid_spec = pltpu.PrefetchScalarGridSpec(
        num_scalar_prefetch=len(tables),
        grid=(n_items,),
        in_specs=[pl.BlockSpec((tm * H_ROWS, LANES), x_map),
                  w13(wa_map), w13(wb_map), w13(wa_map), w13(wb_map), w2s(wa_map), w2s(wb_map)],
        out_specs=pl.BlockSpec((tm * Y_ROWS, LANES), x_map),
        scratch_shapes=[pltpu.VMEM((4, D_MODEL, D_EXPERT), BF16), pltpu.VMEM((2, D_EXPERT, D_MODEL), BF16)],
    )
    return pl.pallas_call(
        functools.partial(_experts_kernel, tm=tm),
        grid_spec=grid_spec,
        out_shape=jax.ShapeDtypeStruct((n * Y_ROWS, LANES), F32),
        compiler_params=_cparams(("arbitrary",)),
        name="moe_experts",
    )(*tables, hs, w1, w1, w3, w3, w2, w2)


def _combine_kernel(starts_ref, route_ref, route_next_ref, yp_hbm, x1_ref, meta_ref, mod_ref, gf_ref, y_ref, ybuf, sems,
                    *, rows_per_step, n_steps):
    i = pl.program_id(0)
    slot = i % 2

    def issue(r_ref, s):
        def body(r, carry):
            pltpu.make_async_copy(yp_hbm.at[_token_rows(_sorted_slot(starts_ref, r_ref, r), Y_ROWS)],
                                  ybuf.at[s, _token_rows(r, Y_ROWS)], sems.at[s * rows_per_step + r]).start()
            return carry
        lax.fori_loop(0, rows_per_step, body, 0, unroll=DMA_LOOP_UNROLL)

    @pl.when(i == 0)
    def _():
        issue(route_ref, 0)

    @pl.when(i + 1 < n_steps)
    def _():
        issue(route_next_ref, 1 - slot)

    def wait_body(r, carry):
        pltpu.make_async_copy(yp_hbm.at[_token_rows(0, Y_ROWS)], ybuf.at[slot, _token_rows(r, Y_ROWS)],
                              sems.at[slot * rows_per_step + r]).wait()
        return carry

    lax.fori_loop(0, rows_per_step, wait_body, 0, unroll=DMA_LOOP_UNROLL)
    ya = _load_token_rows(ybuf, rows_per_step, H_ROWS, first_chunk=0, rows_per_token=Y_ROWS, lead=slot)
    yb = _load_token_rows(ybuf, rows_per_step, H_ROWS, first_chunk=H_ROWS, rows_per_token=Y_ROWS, lead=slot)
    meta = meta_ref[...]
    lane = lax.broadcasted_iota(jnp.int32, (1, LANES), 1)
    w_a = jnp.sum(jnp.where(lane == META_WA, meta, 0.0), axis=1, keepdims=True)
    w_b = jnp.sum(jnp.where(lane == META_WB, meta, 0.0), axis=1, keepdims=True)
    x2 = x1_ref[...] + mod_ref[5] * (w_a * ya + w_b * yb)
    var = jnp.mean(x2 * x2, axis=-1, keepdims=True)
    y_ref[...] = (x2 * lax.rsqrt(var + RMS_EPS)) * gf_ref[...]


def _combine_final(yp, route, starts, x1, meta, mod4, gf, *, rows_per_step, rows_per_mod):
    n = x1.shape[0]
    n_steps = n // rows_per_step
    steps_per_mod = rows_per_mod // rows_per_step
    row = lambda i, starts: (i, 0)
    grid_spec = pltpu.PrefetchScalarGridSpec(
        num_scalar_prefetch=1,
        grid=(n_steps,),
        in_specs=[_route_spec(rows_per_step, route.shape[2]),
                  _route_spec(rows_per_step, route.shape[2], shift=1, n_steps=n_steps),
                  pl.BlockSpec(memory_space=pl.ANY),
                  pl.BlockSpec((rows_per_step, D_MODEL), row),
                  pl.BlockSpec((rows_per_step, LANES), row),
                  pl.BlockSpec((None, 6, 1, D_MODEL), lambda i, starts: (i // steps_per_mod, 0, 0, 0)),
                  pl.BlockSpec((1, D_MODEL), lambda i, starts: (0, 0))],
        out_specs=pl.BlockSpec((rows_per_step, D_MODEL), row),
        scratch_shapes=[pltpu.VMEM((2, rows_per_step * Y_ROWS, LANES), F32),
                        pltpu.SemaphoreType.DMA((2 * rows_per_step,))],
    )
    return pl.pallas_call(
        functools.partial(_combine_kernel, rows_per_step=rows_per_step, n_steps=n_steps),
        grid_spec=grid_spec,
        out_shape=jax.ShapeDtypeStruct((n, D_MODEL), F32),
        compiler_params=_cparams(("arbitrary",)),
        name="moe_combine",
    )(starts, route, route, yp, x1, meta, mod4, gf)


def _routing_tables(counts, *, n, tm):
    counts = counts.astype(jnp.int32)
    ends = jnp.cumsum(counts)
    starts = ends - counts
    n_items = n // tm + N_CLASSES
    first_tile = starts // tm
    last_tile = jnp.maximum(ends - 1, starts) // tm
    visits = jnp.where(counts > 0, last_tile - first_tile + 1, 0)
    item_end = jnp.cumsum(visits)
    item_start = item_end - visits
    n_valid = item_end[-1]
    item = jnp.arange(n_items, dtype=jnp.int32)
    idx = jnp.minimum(item, n_valid - 1)
    c = jnp.sum((idx[:, None] >= item_end[None, :]).astype(jnp.int32), axis=1)
    class_ids = jnp.arange(N_CLASSES, dtype=jnp.int32)

    def pick(table):
        return jnp.sum(jnp.where(c[:, None] == class_ids[None, :], table[None, :], 0), axis=1)

    tile = pick(first_tile) + (idx - pick(item_start))
    live = item < n_valid
    lo = jnp.where(live, jnp.clip(pick(starts) - tile * tm, 0, tm), 0)
    hi = jnp.where(live, jnp.clip(pick(ends) - tile * tm, 0, tm), 0)
    prev_tile = jnp.concatenate([jnp.full((1,), -1, jnp.int32), tile[:-1]])
    first = (tile != prev_tile).astype(jnp.int32)
    prev_c = jnp.concatenate([jnp.full((1,), -1, jnp.int32), c[:-1]])
    switch = (c != prev_c).astype(jnp.int32)
    group_of_class = np.arange(N_CLASSES) // len(_PAIRS)
    pair_of_class = np.arange(N_CLASSES) % len(_PAIRS)
    pairs = np.asarray(_PAIRS)
    ea = pick(jnp.asarray(group_of_class * EXPERTS_PER_GROUP + pairs[pair_of_class, 0], dtype=jnp.int32))
    eb = pick(jnp.asarray(group_of_class * EXPERTS_PER_GROUP + pairs[pair_of_class, 1], dtype=jnp.int32))
    as_i32 = lambda a: a.astype(jnp.int32)
    return as_i32(starts), tuple(as_i32(a) for a in (tile, ea, eb, lo, hi, first, switch, n_valid.reshape(1)))


def _moe_kernel(h_ref, comb_ref, x1_ref, mod_ref, gf_ref, w1_ref, w3_ref, w2_ref, y_ref, acc_ref):
    e = pl.program_id(2)

    @pl.when(e == 0)
    def _():
        acc_ref[...] = jnp.zeros_like(acc_ref)

    h = h_ref[...].astype(BF16)
    a = jnp.dot(h, w1_ref[...].astype(BF16), preferred_element_type=F32)
    b = jnp.dot(h, w3_ref[...].astype(BF16), preferred_element_type=F32)
    lane = lax.broadcasted_iota(jnp.int32, (1, LANES), 1)
    cw = jnp.sum(jnp.where(lane == e + N_EXPERT_GROUPS, comb_ref[...], 0.0), axis=1, keepdims=True)
    hid = (a * _sigmoid(a)) * b * cw
    acc_ref[...] += jnp.dot(hid.astype(BF16), w2_ref[...].astype(BF16), preferred_element_type=F32)

    @pl.when(e == N_EXPERTS - 1)
    def _():
        x2 = x1_ref[...] + mod_ref[5] * acc_ref[...]
        var = jnp.mean(x2 * x2, axis=-1, keepdims=True)
        y_ref[...] = (x2 * lax.rsqrt(var + RMS_EPS)) * gf_ref[...]


def _moe_dense(h2, comb, x1, mod4, gf, w1_bf, w3_bf, w2_bf, *, tm):
    nb, rows, _ = x1.shape
    tiles = rows // tm
    tile3 = lambda b, i, e: (b, i, 0)
    if mod4.shape[2] == 1:
        mod_spec = pl.BlockSpec((None, 6, 1, D_MODEL), lambda b, i, e: (b, 0, 0, 0))
    else:
        mod_spec = pl.BlockSpec((None, 6, tm, D_MODEL), lambda b, i, e: (b, 0, i, 0))
    return pl.pallas_call(
        _moe_kernel,
        grid=(nb, tiles, N_EXPERTS),
        in_specs=[
            pl.BlockSpec((None, tm, D_MODEL), tile3),
            pl.BlockSpec((None, tm, LANES), tile3),
            pl.BlockSpec((None, tm, D_MODEL), tile3),
            mod_spec,
            pl.BlockSpec((1, D_MODEL), lambda b, i, e: (0, 0)),
            pl.BlockSpec((None, D_MODEL, D_EXPERT), lambda b, i, e: (e, 0, 0)),
            pl.BlockSpec((None, D_MODEL, D_EXPERT), lambda b, i, e: (e, 0, 0)),
            pl.BlockSpec((None, D_EXPERT, D_MODEL), lambda b, i, e: (e, 0, 0)),
        ],
        out_specs=pl.BlockSpec((None, tm, D_MODEL), tile3),
        out_shape=jax.ShapeDtypeStruct((nb, rows, D_MODEL), F32),
        scratch_shapes=[pltpu.VMEM((tm, D_MODEL), F32)],
        compiler_params=_cparams(("arbitrary", "arbitrary", "arbitrary")),
        name=f"moe_dense_{nb * rows}",
    )(h2, comb, x1, mod4, gf, w1_bf, w3_bf, w2_bf)


def _s_inproj_kernel(x_ref, mod_ref, g1_ref, w_ref, z_ref):
    x = x_ref[...]
    var = jnp.mean(x * x, axis=-1, keepdims=True)
    h = (x * lax.rsqrt(var + RMS_EPS)) * (g1_ref[...] * (1.0 + mod_ref[1])) + mod_ref[0]
    z_ref[...] = jnp.dot(h, w_ref[...], precision=HIGHEST, preferred_element_type=F32)


def _s_inproj(x, mod_tok, g1, w_in, *, col_block):
    n = x.shape[0]
    return pl.pallas_call(
        _s_inproj_kernel,
        grid=(IN_COLS // col_block,),
        in_specs=[pl.BlockSpec((n, D_MODEL), lambda j: (0, 0)),
                  pl.BlockSpec((6, n, D_MODEL), lambda j: (0, 0, 0)),
                  pl.BlockSpec((1, D_MODEL), lambda j: (0, 0)),
                  pl.BlockSpec((D_MODEL, col_block), lambda j: (0, j))],
        out_specs=pl.BlockSpec((n, col_block), lambda j: (0, j)),
        out_shape=jax.ShapeDtypeStruct((n, IN_COLS), F32),
        compiler_params=_cparams(("arbitrary",)),
        name="inproj_sample",
    )(x, mod_tok, g1, w_in)


def _s_mid_kernel(z_ref, p0_ref, p1_ref, cw_ref, wco_ref, rq_ref, rk_ref,
                  cu_ref, pc_ref, sga_ref, q_ref, k_ref, v_ref, *, t_len):
    n = z_ref.shape[0]
    cu = z_ref[:, OFF_GC:OFF_GC + D_MODEL] * z_ref[:, OFF_U:OFF_U + D_MODEL]
    cu_ref[...] = cu
    t = lax.broadcasted_iota(jnp.int32, (n, D_MODEL), 0) & (t_len - 1)
    prev1 = jnp.where(t >= 1, pltpu.roll(cu, 1, 0), p1_ref[...])
    prev2 = jnp.where(t >= 2, pltpu.roll(cu, 2, 0), jnp.where(t == 0, p0_ref[...], p1_ref[...]))
    cw = cw_ref[...]
    conv = cw[0:1] * prev2 + cw[1:2] * prev1 + cw[2:3] * cu
    yc = jnp.dot(z_ref[:, OFF_GB:OFF_GB + D_MODEL] * conv, wco_ref[...], precision=HIGHEST,
                 preferred_element_type=F32)
    pc_ref[...] = _sigmoid(z_ref[:, OFF_GCONV:OFF_GCONV + D_MODEL]) * yc
    sga_ref[...] = _sigmoid(z_ref[:, OFF_GATTN:OFF_GATTN + D_MODEL])
    aq, bmq, bpq = rq_ref[0], rq_ref[1], rq_ref[2]
    ak, bmk, bpk = rk_ref[0], rk_ref[1], rk_ref[2]
    for c in range(ATTN_WIDTH // LANES):
        sl = slice(c * LANES, (c + 1) * LANES)
        q_ref[:, sl] = _rope_chunk(z_ref[:, OFF_Q + c * LANES:OFF_Q + (c + 1) * LANES], aq, bmq, bpq)
        k_ref[:, sl] = _rope_chunk(z_ref[:, OFF_K + c * LANES:OFF_K + (c + 1) * LANES], ak, bmk, bpk)
    v_ref[...] = z_ref[:, OFF_V:OFF_V + ATTN_WIDTH]


def _s_mid(z, p0e, p1e, conv_w, w_conv_out, rope_q, rope_k, *, t_len):
    n = z.shape[0]
    assert t_len & (t_len - 1) == 0
    full = lambda shape: pl.BlockSpec(shape, lambda i: (0,) * len(shape))
    out_shape = [jax.ShapeDtypeStruct((n, D_MODEL), F32)] * 3 + [jax.ShapeDtypeStruct((n, ATTN_WIDTH), F32)] * 3
    return pl.pallas_call(
        functools.partial(_s_mid_kernel, t_len=t_len),
        grid=(1,),
        in_specs=[full(z.shape), full(p0e.shape), full(p1e.shape), full(conv_w.shape), full(w_conv_out.shape),
                  full(rope_q.shape), full(rope_k.shape)],
        out_specs=[full((n, D_MODEL))] * 3 + [full((n, ATTN_WIDTH))] * 3,
        out_shape=out_shape,
        compiler_params=_cparams(("arbitrary",)),
        name="mid_sample",
    )(z, p0e, p1e, conv_w, w_conv_out, rope_q, rope_k)


def _head_sum(x):
    return jnp.sum(x.reshape(HEADS_PER_GROUP, HEAD_DIM, x.shape[-1]), axis=1)


def _head_expand(x):
    n = x.shape[-1]
    return jnp.broadcast_to(x[:, None, :], (HEADS_PER_GROUP, HEAD_DIM, n)).reshape(GROUP_WIDTH, n)


def _s_attn_disjoint(q_ref, kn, vn, ck_ref, cv_ref, o_ref, lse_ref, s_scr, lane, new_idx, *, n_tiles, dil, t_len):
    cls = lane & (dil - 1)
    q_all = q_ref[...]
    qsel = jnp.zeros((GROUP_WIDTH, LANES), F32)
    for t in range(t_len):
        qsel = jnp.where(cls == t, jnp.broadcast_to(q_all[:, t:t + 1], (GROUP_WIDTH, LANES)), qsel)
    smax = None
    for j in range(n_tiles):
        s = jnp.where(cls < t_len, _head_sum(ck_ref[:, j * LANES:(j + 1) * LANES] * qsel), NEG_BIG)
        s_scr[:, j * LANES:(j + 1) * LANES] = s
        smax = s if smax is None else jnp.maximum(smax, s)
    s_new = _head_sum(kn * q_all)
    m_cols = []
    m_lane = jnp.zeros((HEADS_PER_GROUP, LANES), F32)
    m_new = jnp.zeros((HEADS_PER_GROUP, t_len), F32)
    for t in range(t_len):
        mt = jnp.maximum(jnp.max(jnp.where(cls == t, smax, NEG_BIG), axis=1, keepdims=True), s_new[:, t:t + 1])
        m_cols.append(mt)
        m_lane = jnp.where(cls == t, mt, m_lane)
        m_new = jnp.where(new_idx == t, mt, m_new)
    esum = jnp.zeros((HEADS_PER_GROUP, LANES), F32)
    for j in range(n_tiles):
        e = jnp.exp(s_scr[:, j * LANES:(j + 1) * LANES] - m_lane)
        s_scr[:, j * LANES:(j + 1) * LANES] = e
        esum = esum + e
    e_new = jnp.exp(s_new - m_new)
    l_cols = []
    inv_lane = jnp.zeros((HEADS_PER_GROUP, LANES), F32)
    inv_new = jnp.zeros((HEADS_PER_GROUP, t_len), F32)
    for t in range(t_len):
        lt = jnp.sum(jnp.where(cls == t, esum, 0.0), axis=1, keepdims=True) + e_new[:, t:t + 1]
        l_cols.append(lt)
        inv_lane = jnp.where(cls == t, 1.0 / lt, inv_lane)
        inv_new = jnp.where(new_idx == t, 1.0 / lt, inv_new)
    acc = None
    for j in range(n_tiles):
        term = cv_ref[:, j * LANES:(j + 1) * LANES] * _head_expand(s_scr[:, j * LANES:(j + 1) * LANES] * inv_lane)
        acc = term if acc is None else acc + term
    o_new = vn * _head_expand(e_new * inv_new)
    for t in range(t_len):
        o_ref[:, t:t + 1] = jnp.sum(jnp.where(cls == t, acc, 0.0), axis=1, keepdims=True) + o_new[:, t:t + 1]
        lse_ref[:, t:t + 1] = m_cols[t] + jnp.log(l_cols[t])


def _s_attn_kernel(q_ref, kn_ref, vn_ref, ck_ref, cv_ref, o_ref, lse_ref, ko_ref, vo_ref, s_scr,
                   *, win, dil, t_len):
    n_tiles = win // LANES
    lane = lax.broadcasted_iota(jnp.int32, (1, LANES), 1)
    new_idx = lax.broadcasted_iota(jnp.int32, (1, t_len), 1)
    kn = kn_ref[...]
    vn = vn_ref[...]
    if dil >= t_len:
        _s_attn_disjoint(q_ref, kn, vn, ck_ref, cv_ref, o_ref, lse_ref, s_scr, lane, new_idx,
                         n_tiles=n_tiles, dil=dil, t_len=t_len)
    for t in range(t_len if dil < t_len else 0):
        qb = jnp.broadcast_to(q_ref[:, t:t + 1], (GROUP_WIDTH, LANES))
        m = None
        for j in range(n_tiles):
            pos = lane + j * LANES
            s = _head_sum(ck_ref[:, j * LANES:(j + 1) * LANES] * qb)
            s = jnp.where(pos >= t, jnp.where(((pos - t) & (dil - 1)) == 0, s, NEG_BIG), NEG_BIG)
            s_scr[:, j * LANES:(j + 1) * LANES] = s
            mj = jnp.max(s, axis=1, keepdims=True)
            m = mj if m is None else jnp.maximum(m, mj)
        s_new = _head_sum(kn * qb[:, 0:t_len])
        s_new = jnp.where(new_idx <= t, jnp.where(((t - new_idx) & (dil - 1)) == 0, s_new, NEG_BIG), NEG_BIG)
        m = jnp.maximum(m, jnp.max(s_new, axis=1, keepdims=True))
        e_new = jnp.exp(s_new - m)
        l = jnp.sum(e_new, axis=1, keepdims=True)
        acc = None
        for j in range(n_tiles):
            e = jnp.exp(s_scr[:, j * LANES:(j + 1) * LANES] - m)
            l = l + jnp.sum(e, axis=1, keepdims=True)
            term = cv_ref[:, j * LANES:(j + 1) * LANES] * _head_expand(e)
            acc = term if acc is None else acc + term
        o = jnp.sum(acc, axis=1, keepdims=True) + jnp.sum(vn * _head_expand(e_new), axis=1, keepdims=True)
        o_ref[:, t:t + 1] = o * _head_expand(1.0 / l)
        lse_ref[:, t:t + 1] = m + jnp.log(l)

    for c_ref, new, out_ref in ((ck_ref, kn, ko_ref), (cv_ref, vn, vo_ref)):
        out_ref[...] = pltpu.roll(c_ref[...], win - t_len, 1)
        out_ref[:, win - t_len:win] = new


def _s_attn_group(q_t, kn_t, vn_t, cache_k, cache_v, g):
    bsz, _, t_len = q_t.shape
    win_full, dil = DIL_GROUPS[g]
    win = cache_k.shape[2]
    assert win == win_full and win == (KEYS_PER_QUERY - 1) * dil and win % LANES == 0
    assert dil & (dil - 1) == 0 and LANES % dil == 0
    grp = lambda b: (b, g, 0)
    per_b = lambda b: (b, 0, 0)
    return pl.pallas_call(
        functools.partial(_s_attn_kernel, win=win, dil=dil, t_len=t_len),
        grid=(bsz,),
        in_specs=[pl.BlockSpec((None, GROUP_WIDTH, t_len), grp)] * 3
        + [pl.BlockSpec((None, GROUP_WIDTH, win), per_b)] * 2,
        out_specs=[pl.BlockSpec((None, GROUP_WIDTH, t_len), per_b),
                   pl.BlockSpec((None, HEADS_PER_GROUP, t_len), per_b),
                   pl.BlockSpec((None, GROUP_WIDTH, win), per_b),
                   pl.BlockSpec((None, GROUP_WIDTH, win), per_b)],
        out_shape=[jax.ShapeDtypeStruct((bsz, GROUP_WIDTH, t_len), F32),
                   jax.ShapeDtypeStruct((bsz, HEADS_PER_GROUP, t_len), F32),
                   jax.ShapeDtypeStruct((bsz, GROUP_WIDTH, win), F32),
                   jax.ShapeDtypeStruct((bsz, GROUP_WIDTH, win), F32)],
        scratch_shapes=[pltpu.VMEM((HEADS_PER_GROUP, win), F32)],
        compiler_params=_cparams(("arbitrary",)),
        name=f"attn_sample_g{g}",
    )(q_t, kn_t, vn_t, cache_k, cache_v)


TM_INPROJ = 512
TM_POST = 512
TM_EXPERT = 256
ROWS_PER_DMA_STEP = 128
ATTN_ROWS_PER_STEP = 1024
S_COL_BLOCK = 512


def _to_state(a_t):
    b, _, length = a_t.shape
    return jnp.transpose(a_t.reshape(b, HEADS_PER_GROUP, HEAD_DIM, length), (0, 3, 1, 2))[None]


def _from_state(a):
    b, length = a.shape[0], a.shape[1]
    return jnp.transpose(a, (0, 2, 3, 1)).reshape(b, GROUP_WIDTH, length)


def kernel(x_prompt, x_sample, cache_k1, cache_v1, cache_k2, cache_v2, cache_k3, cache_v3, state_conv,
           c_prompt, c_sample, norm1_g, norm2_g, normf_g, w_ada, b_ada, w_in, conv_w, w_conv_out,
           w_attn_out, w_o, w_rg, b_rg, w_re, b_re, w1, w3, w2):
    depth = w_in.shape[0]
    assert depth == 1
    bsz, seq, _ = x_prompt.shape
    dbsz, t_len, _ = x_sample.shape
    n_s = dbsz * t_len
    l = 0

    w_in_bf = w_in[l].astype(BF16)
    wco_bf = w_conv_out[l].astype(BF16)
    wao_bf = w_attn_out[l].astype(BF16)
    wo_bf = w_o[l].astype(BF16)
    w1_e = w1[l].reshape(N_EXPERTS, D_MODEL, D_EXPERT)
    w3_e = w3[l].reshape(N_EXPERTS, D_MODEL, D_EXPERT)
    w2_e = w2[l].reshape(N_EXPERTS, D_EXPERT, D_MODEL)
    g1 = norm1_g[l].reshape(1, D_MODEL)
    g2 = norm2_g[l].reshape(1, D_MODEL)
    gf = normf_g.reshape(1, D_MODEL)
    n_route = N_EXPERT_GROUPS + N_EXPERTS
    wr = jnp.pad(jnp.concatenate([w_rg[l], w_re[l]], axis=1), ((0, 0), (0, LANES - n_route)))
    br = jnp.pad(jnp.concatenate([b_rg[l], b_re[l]]), (0, LANES - n_route)).reshape(1, LANES)
    head_of_lane = np.arange(GROUP_WIDTH) // HEAD_DIM
    expand_bf = jnp.asarray((np.arange(LANES)[:, None] == head_of_lane[None, :]).astype(np.float32), dtype=BF16)

    mod = _adaln(jnp.concatenate([c_prompt, c_sample], axis=0), w_ada[l], b_ada[l])
    mod_p4 = mod[:bsz].reshape(bsz, 6, 1, D_MODEL)
    mod_tok = jnp.repeat(mod[bsz:].reshape(dbsz, 1, 6, D_MODEL), t_len, axis=1)
    mod_tok = jnp.transpose(mod_tok.reshape(n_s, 6, D_MODEL), (1, 0, 2))
    mod_s4 = mod_tok[None]

    pos_p = jnp.arange(seq, dtype=jnp.int32)
    rope_q_p = _rope_tables(pos_p, HEAD_DIM ** -0.5)
    rope_k_p = _rope_tables(pos_p, 1.0)
    pc_p, sga_p, conv_p = _convproj_prompt(x_prompt, mod_p4, g1, w_in_bf[:, :OFF_Q], w_in_bf[:, OFF_GCONV:],
                                           conv_w[l], wco_bf, tm=TM_INPROJ)
    outs = _qkvproj_prompt(x_prompt, mod_p4, g1, w_in_bf[:, OFF_Q:OFF_GCONV], rope_q_p, rope_k_p, tm=TM_INPROJ)
    qkv_p = outs[0:3 * N_DIL]
    states_p = outs[3 * N_DIL:]
    o_p, lse_p = [], []
    for g in range(N_DIL):
        o_g, lse_g = _attn_prompt_group(qkv_p[3 * g], qkv_p[3 * g + 1], qkv_p[3 * g + 2], g, max_rows=ATTN_ROWS_PER_STEP)
        o_p.append(o_g)
        lse_p.append(lse_g)
    x1_p, h2rows_p, meta_p, route_p, cnt_p = _post(o_p, lse_p, pc_p, sga_p, x_prompt, mod_p4, wao_bf, wo_bf, g2, wr, br,
                                                   expand_bf, tm=TM_POST, full_precision=False, routed=True)
    n_p = bsz * seq
    route = route_p.reshape(n_p // TM_POST, SUBLANES, TM_POST)
    starts, tables = _routing_tables(cnt_p[0, :N_CLASSES], n=n_p, tm=TM_EXPERT)
    hs = _dispatch_rows(h2rows_p.reshape(n_p * H_ROWS, LANES), route, starts, rows_per_step=ROWS_PER_DMA_STEP)
    yp = _experts_routed(hs, tables, w1_e, w3_e, w2_e, tm=TM_EXPERT)
    y_p = _combine_final(yp, route, starts, x1_p.reshape(n_p, D_MODEL), meta_p.reshape(n_p, LANES), mod_p4, gf,
                         rows_per_step=ROWS_PER_DMA_STEP, rows_per_mod=seq).reshape(bsz, seq, D_MODEL)

    pos_s = PAST_LEN + jnp.arange(t_len, dtype=jnp.int32)
    rope_q_s = jnp.tile(_rope_tables(pos_s, HEAD_DIM ** -0.5), (1, dbsz, 1))
    rope_k_s = jnp.tile(_rope_tables(pos_s, 1.0), (1, dbsz, 1))
    xs = x_sample.reshape(n_s, D_MODEL)
    z_s = _s_inproj(xs, mod_tok, g1, w_in[l], col_block=S_COL_BLOCK)
    past = state_conv[l]
    p0e = jnp.repeat(past[:, 0], t_len, axis=0)
    p1e = jnp.repeat(past[:, 1], t_len, axis=0)
    cu_s, pc_s, sga_s, q_s, k_s, v_s = _s_mid(z_s, p0e, p1e, conv_w[l], w_conv_out[l], rope_q_s, rope_k_s,
                                              t_len=t_len)
    to_cols = lambda a: jnp.transpose(a.reshape(dbsz, t_len, ATTN_WIDTH), (0, 2, 1))
    q_t, kn_t, vn_t = to_cols(q_s), to_cols(k_s), to_cols(v_s)
    caches = ((cache_k1, cache_v1), (cache_k2, cache_v2), (cache_k3, cache_v3))
    o_s, lse_s, kv_s = [], [], []
    for g, (ck, cv) in enumerate(caches):
        o_g, lse_g, ko, vo = _s_attn_group(q_t, kn_t, vn_t, _from_state(ck[l]), _from_state(cv[l]), g)
        o_s.append(jnp.transpose(o_g, (0, 2, 1)).reshape(1, 1, n_s, GROUP_WIDTH))
        lse_rows = jnp.transpose(lse_g, (0, 2, 1)).reshape(n_s, HEADS_PER_GROUP)
        lse_s.append(jnp.pad(lse_rows, ((0, 0), (0, LANES - HEADS_PER_GROUP))).reshape(1, 1, n_s, LANES))
        kv_s += [_to_state(ko), _to_state(vo)]
    x1_s, h2_s, comb_s = _post(o_s, lse_s, pc_s[None], sga_s[None], xs[None], mod_s4, w_attn_out[l], w_o[l], g2, wr, br,
                               expand_bf, tm=n_s, full_precision=True, routed=False)
    y_s = _moe_dense(h2_s, comb_s, x1_s, mod_s4, gf, w1_e, w3_e, w2_e, tm=n_s)

    conv_s = cu_s.reshape(dbsz, t_len, D_MODEL)[:, t_len - (CONV_K - 1):]
    return (y_p, y_s.reshape(dbsz, t_len, D_MODEL),
            *[_to_state(a) for a in states_p], conv_p.reshape(1, bsz, CONV_K - 1, D_MODEL),
            *kv_s, conv_s.reshape(1, dbsz, CONV_K - 1, D_MODEL))
```

```python
import functools

import numpy as np
import jax
import jax.numpy as jnp
from jax import lax
from jax.experimental import pallas as pl
from jax.experimental.pallas import tpu as pltpu

F32 = jnp.float32
BF16 = jnp.bfloat16
HIGHEST = lax.Precision.HIGHEST

D_MODEL = 1024
HEAD_DIM = 64
HEADS_PER_GROUP = 8
GROUP_WIDTH = HEADS_PER_GROUP * HEAD_DIM
DIL_GROUPS = ((128, 1), (512, 4), (2048, 16))
N_DIL = len(DIL_GROUPS)
ATTN_WIDTH = N_DIL * GROUP_WIDTH
ROT_DIM = HEAD_DIM // 4
ROPE_THETA = 500000.0
PAST_LEN = 16384
CONV_K = 3
N_EXPERT_GROUPS = 4
EXPERTS_PER_GROUP = 4
N_EXPERTS = N_EXPERT_GROUPS * EXPERTS_PER_GROUP
D_EXPERT = 512
RMS_EPS = 1e-6
IN_COLS = 3 * D_MODEL + 3 * ATTN_WIDTH + 2 * D_MODEL
OFF_U, OFF_GC, OFF_GB = 0, D_MODEL, 2 * D_MODEL
OFF_Q = 3 * D_MODEL
OFF_K = OFF_Q + ATTN_WIDTH
OFF_V = OFF_K + ATTN_WIDTH
OFF_GCONV = OFF_V + ATTN_WIDTH
OFF_GATTN = OFF_GCONV + D_MODEL

LANES = 128
SUBLANES = 8
CHUNKS_PER_GROUP = GROUP_WIDTH // LANES
KEYS_PER_QUERY = 129
Q_BLOCK = 128
NEG_BIG = -1e30

VMEM_LIMIT = 56 * 1024 * 1024


def _sigmoid(x):
    return 1.0 / (1.0 + jnp.exp(-x))


def _cparams(sem):
    return pltpu.CompilerParams(dimension_semantics=sem, vmem_limit_bytes=VMEM_LIMIT)


def _adaln_kernel(c_ref, w_ref, b_ref, o_ref):
    c = c_ref[...]
    s = c * _sigmoid(c)
    o_ref[...] = jnp.dot(s, w_ref[...], precision=HIGHEST, preferred_element_type=F32) + b_ref[...]


def _adaln(c_all, w_ada, b_ada):
    rows = c_all.shape[0]
    n_col = w_ada.shape[1] // D_MODEL
    return pl.pallas_call(
        _adaln_kernel,
        grid=(n_col,),
        in_specs=[
            pl.BlockSpec((rows, D_MODEL), lambda j: (0, 0)),
            pl.BlockSpec((D_MODEL, D_MODEL), lambda j: (0, j)),
            pl.BlockSpec((1, D_MODEL), lambda j: (0, j)),
        ],
        out_specs=pl.BlockSpec((rows, D_MODEL), lambda j: (0, j)),
        out_shape=jax.ShapeDtypeStruct((rows, w_ada.shape[1]), F32),
        compiler_params=_cparams(("arbitrary",)),
        name="adaln",
    )(c_all, w_ada, b_ada.reshape(1, -1))


def _rope_tables(pos, scale):
    half = ROT_DIM // 2
    inv_freq = jnp.power(jnp.float32(ROPE_THETA), -jnp.arange(half, dtype=F32) / half)
    ang = pos.astype(F32)[:, None] * inv_freq[None, :]
    cos, sin = jnp.cos(ang), jnp.sin(ang)
    lane_in_head = np.arange(LANES) % HEAD_DIM
    freq = lane_in_head % half
    first = lane_in_head < half
    second = (lane_in_head >= half) & (lane_in_head < ROT_DIM)
    a = jnp.where(first | second, cos[:, freq], 1.0)
    bm = jnp.where(first, -sin[:, freq], 0.0)
    bp = jnp.where(second, sin[:, freq], 0.0)
    return jnp.stack([a, bm, bp]) * scale


def _rope_chunk(zc, a, bm, bp):
    return zc * a + pltpu.roll(zc, LANES - ROT_DIM // 2, 1) * bm + pltpu.roll(zc, ROT_DIM // 2, 1) * bp


def _modulated_norm_bf16(x_ref, mod_ref, g1_ref):
    x = x_ref[...]
    var = jnp.mean(x * x, axis=-1, keepdims=True)
    h = (x * lax.rsqrt(var + RMS_EPS)) * (g1_ref[...] * (1.0 + mod_ref[1])) + mod_ref[0]
    return h.astype(BF16)


def _convproj_kernel(x_ref, mod_ref, g1_ref, wa_ref, wg_ref, cw_ref, wco_ref, pc_ref, sga_ref, cst_ref, s_ref,
                     *, tm, n_tiles):
    i = pl.program_id(1)
    hb = _modulated_norm_bf16(x_ref, mod_ref, g1_ref)

    def proj(w_ref, lo):
        return jnp.dot(hb, w_ref[:, lo:lo + D_MODEL], preferred_element_type=F32)

    cu = proj(wa_ref, OFF_GC) * proj(wa_ref, OFF_U)

    @pl.when(i == 0)
    def _():
        s_ref[0:SUBLANES, :] = jnp.zeros((SUBLANES, D_MODEL), F32)

    s_ref[SUBLANES:SUBLANES + tm, :] = cu
    cw = cw_ref[...]
    conv = (cw[0:1] * s_ref[SUBLANES - 2:SUBLANES - 2 + tm, :]
            + cw[1:2] * s_ref[SUBLANES - 1:SUBLANES - 1 + tm, :]
            + cw[2:3] * cu)
    yc = jnp.dot((proj(wa_ref, OFF_GB) * conv).astype(BF16), wco_ref[...], preferred_element_type=F32)
    pc_ref[...] = (_sigmoid(proj(wg_ref, 0)) * yc).astype(pc_ref.dtype)
    sga_ref[...] = _sigmoid(proj(wg_ref, D_MODEL)).astype(sga_ref.dtype)

    @pl.when(i == n_tiles - 1)
    def _():
        cst_ref[...] = s_ref[tm + SUBLANES - 2:tm + SUBLANES, :]

    s_ref[0:SUBLANES, :] = s_ref[tm:tm + SUBLANES, :]


def _convproj_prompt(x, mod4, g1, wa_bf, wg_bf, conv_w, wco_bf, *, tm):
    bsz, seq, _ = x.shape
    n_tiles = seq // tm
    const2 = lambda b, i: (0, 0)
    tile3 = lambda b, i: (b, i, 0)
    return pl.pallas_call(
        functools.partial(_convproj_kernel, tm=tm, n_tiles=n_tiles),
        grid=(bsz, n_tiles),
        in_specs=[
            pl.BlockSpec((None, tm, D_MODEL), tile3),
            pl.BlockSpec((None, 6, 1, D_MODEL), lambda b, i: (b, 0, 0, 0)),
            pl.BlockSpec((1, D_MODEL), const2),
            pl.BlockSpec(wa_bf.shape, const2, pipeline_mode=pl.Buffered(1)),
            pl.BlockSpec(wg_bf.shape, const2, pipeline_mode=pl.Buffered(1)),
            pl.BlockSpec((CONV_K, D_MODEL), const2),
            pl.BlockSpec((D_MODEL, D_MODEL), const2, pipeline_mode=pl.Buffered(1)),
        ],
        out_specs=[pl.BlockSpec((None, tm, D_MODEL), tile3), pl.BlockSpec((None, tm, D_MODEL), tile3),
                   pl.BlockSpec((None, CONV_K - 1, D_MODEL), lambda b, i: (b, 0, 0))],
        out_shape=[jax.ShapeDtypeStruct((bsz, seq, D_MODEL), BF16), jax.ShapeDtypeStruct((bsz, seq, D_MODEL), BF16),
                   jax.ShapeDtypeStruct((bsz, CONV_K - 1, D_MODEL), F32)],
        scratch_shapes=[pltpu.VMEM((tm + SUBLANES, D_MODEL), F32)],
        compiler_params=_cparams(("arbitrary", "arbitrary")),
        name="convproj_prompt",
    )(x, mod4, g1, wa_bf, wg_bf, conv_w, wco_bf)


def _qkvproj_kernel(x_ref, mod_ref, g1_ref, w_ref, rq_ref, rk_ref, *rest, tm, n_tiles, seq):
    qkv_refs = rest[0:3 * N_DIL]
    st_refs = rest[3 * N_DIL:3 * N_DIL + 2 * N_DIL]
    d_ref = rest[-1]
    i = pl.program_id(1)
    hb = _modulated_norm_bf16(x_ref, mod_ref, g1_ref)

    def proj(lo, width):
        return jnp.dot(hb, w_ref[:, lo:lo + width], preferred_element_type=F32)

    zq = proj(0, ATTN_WIDTH)
    zk = proj(ATTN_WIDTH, ATTN_WIDTH)
    zv = proj(2 * ATTN_WIDTH, ATTN_WIDTH)
    aq, bmq, bpq = rq_ref[0], rq_ref[1], rq_ref[2]
    ak, bmk, bpk = rk_ref[0], rk_ref[1], rk_ref[2]
    n_chunks = ATTN_WIDTH // LANES
    q_chunks, k_chunks, v_chunks = [], [], []
    for c in range(n_chunks):
        sl = slice(c * LANES, (c + 1) * LANES)
        q_chunks.append(_rope_chunk(zq[:, sl], aq, bmq, bpq))
        k_chunks.append(_rope_chunk(zk[:, sl], ak, bmk, bpk))
        v_chunks.append(zv[:, sl])

    for which, chunks in enumerate((q_chunks, k_chunks, v_chunks)):
        for g in range(N_DIL):
            out_ref = qkv_refs[3 * g + which]
            dil = DIL_GROUPS[g][1]
            for cc in range(CHUNKS_PER_GROUP):
                c = g * CHUNKS_PER_GROUP + cc
                sl = slice(cc * LANES, (cc + 1) * LANES)
                if dil == 1:
                    out_ref[0, :, sl] = chunks[c].astype(BF16)
                else:
                    d_ref[c] = chunks[c]
                    for r in range(dil):
                        out_ref[r, :, sl] = d_ref[c, pl.ds(r, tm // dil, stride=dil), :].astype(BF16)

    for g in range(N_DIL):
        kst, vst = st_refs[2 * g], st_refs[2 * g + 1]
        win = min(DIL_GROUPS[g][0], seq)
        if win >= tm:
            cond, r0 = i >= (seq - win) // tm, 0
        else:
            cond, r0 = i == n_tiles - 1, tm - win

        @pl.when(cond)
        def _(g=g, kst=kst, vst=vst, r0=r0):
            for cc in range(CHUNKS_PER_GROUP):
                c = g * CHUNKS_PER_GROUP + cc
                kst[cc * LANES:(cc + 1) * LANES, :] = k_chunks[c][r0:, :].T
                vst[cc * LANES:(cc + 1) * LANES, :] = v_chunks[c][r0:, :].T


def _qkvproj_prompt(x, mod4, g1, wqkv_bf, rope_q, rope_k, *, tm):
    bsz, seq, _ = x.shape
    n_tiles = seq // tm
    const2 = lambda b, i: (0, 0)
    tile3 = lambda b, i: (b, i, 0)
    in_specs = [
        pl.BlockSpec((None, tm, D_MODEL), tile3),
        pl.BlockSpec((None, 6, 1, D_MODEL), lambda b, i: (b, 0, 0, 0)),
        pl.BlockSpec((1, D_MODEL), const2),
        pl.BlockSpec(wqkv_bf.shape, const2, pipeline_mode=pl.Buffered(1)),
        pl.BlockSpec((3, tm, LANES), lambda b, i: (0, i, 0)),
        pl.BlockSpec((3, tm, LANES), lambda b, i: (0, i, 0)),
    ]
    out_shape, out_specs = [], []
    for _, dil in DIL_GROUPS:
        assert tm % (dil * 16) == 0
        for _ in range(3):
            out_shape.append(jax.ShapeDtypeStruct((bsz, dil, seq // dil, GROUP_WIDTH), BF16))
            out_specs.append(pl.BlockSpec((None, dil, tm // dil, GROUP_WIDTH), lambda b, i: (b, 0, i, 0)))
    for win, _ in DIL_GROUPS:
        win = min(win, seq)
        cols = min(win, tm)
        if win >= tm:
            imap = lambda b, i, ft=(seq - win) // tm: (b, 0, jnp.maximum(i - ft, 0))
        else:
            imap = lambda b, i: (b, 0, 0)
        for _ in range(2):
            out_shape.append(jax.ShapeDtypeStruct((bsz, GROUP_WIDTH, win), F32))
            out_specs.append(pl.BlockSpec((None, GROUP_WIDTH, cols), imap))
    return pl.pallas_call(
        functools.partial(_qkvproj_kernel, tm=tm, n_tiles=n_tiles, seq=seq),
        grid=(bsz, n_tiles),
        in_specs=in_specs,
        out_specs=out_specs,
        out_shape=out_shape,
        scratch_shapes=[pltpu.VMEM((ATTN_WIDTH // LANES, tm, LANES), F32)],
        compiler_params=_cparams(("arbitrary", "arbitrary")),
        name="qkvproj_prompt",
    )(x, mod4, g1, wqkv_bf, rope_q, rope_k)


ATTN_UNROLL = 4


def _attn_kernel(*refs, chunk, has_halo, n_cls):
    if has_halo:
        q_ref, k_ref, v_ref, kh_ref, vh_ref, o_ref, lse_ref, kbuf, vbuf = refs
    else:
        q_ref, k_ref, v_ref, o_ref, lse_ref, kbuf, vbuf = refs
    c = pl.program_id(2)
    if has_halo:
        kbuf[:, 0:Q_BLOCK, :] = kh_ref[...]
        vbuf[:, 0:Q_BLOCK, :] = vh_ref[...]
    else:
        kbuf[:, 0:Q_BLOCK, :] = jnp.zeros((n_cls, Q_BLOCK, GROUP_WIDTH), BF16)
        vbuf[:, 0:Q_BLOCK, :] = jnp.zeros((n_cls, Q_BLOCK, GROUP_WIDTH), BF16)
    kbuf[:, Q_BLOCK:Q_BLOCK + chunk, :] = k_ref[...]
    vbuf[:, Q_BLOCK:Q_BLOCK + chunk, :] = v_ref[...]
    blocks_per_cls = chunk // Q_BLOCK

    row = lax.broadcasted_iota(jnp.int32, (Q_BLOCK, 2 * Q_BLOCK), 0)
    col = lax.broadcasted_iota(jnp.int32, (Q_BLOCK, 2 * Q_BLOCK), 1)
    bias_main = jnp.where(col >= row, jnp.where(col <= row + Q_BLOCK, 0.0, NEG_BIG), NEG_BIG)
    bias_first = jnp.where(col >= Q_BLOCK, bias_main, NEG_BIG)
    lane = lax.broadcasted_iota(jnp.int32, (1, LANES), 1)
    lo_half = lane < HEAD_DIM

    def body(u, carry):
        cls, qb = u // blocks_per_cls, u % blocks_per_cls
        r0 = pl.multiple_of(qb * Q_BLOCK, Q_BLOCK)
        kt = kbuf[cls, pl.ds(r0, 2 * Q_BLOCK), :]
        vt = vbuf[cls, pl.ds(r0, 2 * Q_BLOCK), :]
        qt = q_ref[cls, pl.ds(r0, Q_BLOCK), :]
        is_first = jnp.logical_and(qb == 0, c == 0)
        bias = jnp.where(is_first, bias_first, bias_main)
        bias2 = jnp.concatenate([bias, bias], axis=0)
        lse_tile = jnp.zeros((Q_BLOCK, LANES), F32)
        for p in range(CHUNKS_PER_GROUP):
            sl = slice(p * LANES, (p + 1) * LANES)
            qp, kp, vp = qt[:, sl], kt[:, sl], vt[:, sl]
            zero = jnp.zeros_like(qp)
            q2 = jnp.concatenate([jnp.where(lo_half, qp, zero), jnp.where(lo_half, zero, qp)], axis=0)
            s = lax.dot_general(q2, kp, (((1,), (1,)), ((), ())), preferred_element_type=F32) + bias2
            m = jnp.max(s, axis=1, keepdims=True)
            e = jnp.exp(s - m)
            l = jnp.sum(e, axis=1, keepdims=True)
            o = jnp.dot(e.astype(BF16), vp, preferred_element_type=F32) * (1.0 / l)
            lse = m + jnp.log(l)
            for hh in range(2):
                lse_tile = jnp.where(lane == 2 * p + hh, lse[hh * Q_BLOCK:(hh + 1) * Q_BLOCK], lse_tile)
            o_ref[cls, pl.ds(r0, Q_BLOCK), sl] = jnp.where(lo_half, o[:Q_BLOCK], o[Q_BLOCK:])
        lse_ref[cls, pl.ds(r0, Q_BLOCK), :] = lse_tile
        return carry

    n_units = n_cls * blocks_per_cls
    lax.fori_loop(0, n_units, body, 0, unroll=min(ATTN_UNROLL, n_units))


def _attn_prompt_group(q, k, v, g, *, max_rows):
    bsz, dil, cls_len, _ = q.shape
    chunk = min(max_rows, cls_len)
    n_chunks = cls_len // chunk
    n_cls = min(dil, max_rows // chunk)
    has_halo = n_chunks > 1
    main = pl.BlockSpec((None, n_cls, chunk, GROUP_WIDTH), lambda b, r, c: (b, r, c, 0))
    in_specs = [main, main, main]
    args = [q, k, v]
    if has_halo:
        per = chunk // Q_BLOCK
        halo = pl.BlockSpec((None, n_cls, Q_BLOCK, GROUP_WIDTH), lambda b, r, c: (b, r, jnp.maximum(c * per - 1, 0), 0))
        in_specs += [halo, halo]
        args += [k, v]
    return pl.pallas_call(
        functools.partial(_attn_kernel, chunk=chunk, has_halo=has_halo, n_cls=n_cls),
        grid=(bsz, dil // n_cls, n_chunks),
        in_specs=in_specs,
        out_specs=[pl.BlockSpec((None, n_cls, chunk, GROUP_WIDTH), lambda b, r, c: (b, r, c, 0)),
                   pl.BlockSpec((None, n_cls, chunk, LANES), lambda b, r, c: (b, r, c, 0))],
        out_shape=[jax.ShapeDtypeStruct((bsz, dil, cls_len, GROUP_WIDTH), F32),
                   jax.ShapeDtypeStruct((bsz, dil, cls_len, LANES), F32)],
        scratch_shapes=[pltpu.VMEM((n_cls, chunk + Q_BLOCK, GROUP_WIDTH), BF16),
                        pltpu.VMEM((n_cls, chunk + Q_BLOCK, GROUP_WIDTH), BF16)],
        compiler_params=_cparams(("arbitrary", "arbitrary", "arbitrary")),
        name=f"attn_prompt_g{g}",
    )(*args)


def _split_bf16(x, n):
    parts = []
    r = x
    for _ in range(n):
        p = r.astype(BF16)
        parts.append(p)
        r = r - p.astype(F32)
    return parts


def _store_token_rows(ref, value, n_tokens, first_chunk=0, rows_per_token=None):
    n_chunks = value.shape[1] // LANES
    rows_per_token = rows_per_token or n_chunks
    for c in range(n_chunks):
        ref[pl.ds(first_chunk + c, n_tokens, stride=rows_per_token), :] = value[:, c * LANES:(c + 1) * LANES]


def _load_token_rows(ref, n_tokens, n_chunks, first_chunk=0, rows_per_token=None, lead=None):
    rows_per_token = rows_per_token or n_chunks

    def chunk(c):
        rows = pl.ds(first_chunk + c, n_tokens, stride=rows_per_token)
        return ref[rows, :] if lead is None else ref[lead, rows, :]

    return jnp.concatenate([chunk(c) for c in range(n_chunks)], axis=1)


H_ROWS = D_MODEL // LANES
Y_ROWS = 2 * D_MODEL // LANES

_PAIRS = ((0, 1), (0, 2), (0, 3), (1, 2), (1, 3), (2, 3))
N_CLASSES = N_EXPERT_GROUPS * len(_PAIRS)
META_CLASS, META_RANK, META_WA, META_WB = 0, 1, 2, 3


def _post_kernel(*refs, tm, dils, full_precision, routed):
    (o0_ref, o1_ref, o2_ref, l0_ref, l1_ref, l2_ref, pc_ref, sga_ref, x_ref, mod_ref,
     wao_ref, wo_ref, g2_ref, wr_ref, wrh_ref, wrl_ref, br_ref, exp_ref) = refs[:18]
    if routed:
        tri_ref, x1_ref, h2_ref, meta_ref, route_ref, cnt_ref, o_scr, l_scr, run_scr = refs[18:]
    else:
        x1_ref, h2_ref, comb_ref, o_scr, l_scr = refs[18:]

    def mm(a, w_ref):
        if full_precision:
            return jnp.dot(a, w_ref[...], precision=HIGHEST, preferred_element_type=F32)
        return jnp.dot(a.astype(BF16), w_ref[...], preferred_element_type=F32)

    def natural_order(ref, scr, dil, n_chunks):
        if dil == 1:
            return [ref[0, :, c * LANES:(c + 1) * LANES] for c in range(n_chunks)]
        out = []
        for c in range(n_chunks):
            for r in range(dil):
                scr[c, pl.ds(r, tm // dil, stride=dil), :] = ref[r, :, c * LANES:(c + 1) * LANES]
            out.append(scr[c])
        return out

    lses = [natural_order(ref, l_scr.at[g], dils[g], 1)[0] for g, ref in enumerate((l0_ref, l1_ref, l2_ref))]
    mx = jnp.maximum(lses[0], jnp.maximum(lses[1], lses[2]))
    es = [jnp.exp(v - mx) for v in lses]
    inv = 1.0 / (es[0] + es[1] + es[2])
    expand = exp_ref[...]
    attn_o = None
    for g, o_ref in enumerate((o0_ref, o1_ref, o2_ref)):
        w = es[g] * inv
        we = None
        for part in _split_bf16(w, 3 if full_precision else 2):
            t = jnp.dot(part, expand, preferred_element_type=F32)
            we = t if we is None else we + t
        o_nat = jnp.concatenate(natural_order(o_ref, o_scr.at[g], dils[g], CHUNKS_PER_GROUP), axis=1)
        term = we * o_nat
        attn_o = term if attn_o is None else attn_o + term

    y_attn = mm(attn_o, wao_ref)
    mixed = mm(pc_ref[...] + sga_ref[...] * y_attn, wo_ref)
    x1 = x_ref[...] + mod_ref[2] * mixed
    x1_ref[...] = x1
    var = jnp.mean(x1 * x1, axis=-1, keepdims=True)
    h2 = (x1 * lax.rsqrt(var + RMS_EPS)) * (g2_ref[...] * (1.0 + mod_ref[4])) + mod_ref[3]
    if routed:
        _store_token_rows(h2_ref, h2, tm)
    else:
        h2_ref[...] = h2.astype(h2_ref.dtype)

    if full_precision:
        lg = jnp.dot(h2, wr_ref[...], precision=HIGHEST, preferred_element_type=F32) + br_ref[...]
    else:
        h_hi, h_lo = _split_bf16(h2, 2)
        lg = (jnp.dot(h_hi, wrh_ref[...], preferred_element_type=F32)
              + jnp.dot(h_lo, wrh_ref[...], preferred_element_type=F32)
              + jnp.dot(h_hi, wrl_ref[...], preferred_element_type=F32)) + br_ref[...]
    lane_i = lax.broadcasted_iota(jnp.int32, (1, LANES), 1)
    lane = lane_i.astype(F32)
    lane_group = ((lane_i - N_EXPERT_GROUPS) >> 2).astype(F32)
    big = jnp.float32(1e9)
    gl = jnp.where(lane_i < N_EXPERT_GROUPS, lg, NEG_BIG)
    gmax = jnp.max(gl, axis=1, keepdims=True)
    gidx = jnp.min(jnp.where(gl == gmax, lane, big), axis=1, keepdims=True)
    g_w = 1.0 / jnp.sum(jnp.exp(gl - gmax), axis=1, keepdims=True)
    el = jnp.where(lane_group == gidx, lg, NEG_BIG)
    v1 = jnp.max(el, axis=1, keepdims=True)
    i1 = jnp.min(jnp.where(el == v1, lane, big), axis=1, keepdims=True)
    el2 = jnp.where(lane == i1, NEG_BIG, el)
    v2 = jnp.max(el2, axis=1, keepdims=True)
    i2 = jnp.min(jnp.where(el2 == v2, lane, big), axis=1, keepdims=True)
    t = jnp.exp(v2 - v1)
    den = 1.0 / (1.0 + t)
    w_top1, w_top2 = g_w * den, g_w * (t * den)
    if not routed:
        comb_ref[...] = jnp.where(lane == i1, w_top1, jnp.where(lane == i2, w_top2, 0.0))
        return

    base = jnp.float32(N_EXPERT_GROUPS) + jnp.float32(EXPERTS_PER_GROUP) * gidx
    e1, e2 = i1 - base, i2 - base
    first_is_lower = e1 < e2
    ea = jnp.where(first_is_lower, e1, e2)
    eb = jnp.where(first_is_lower, e2, e1)
    pair = ea * (7.0 - ea) * 0.5 + (eb - ea - 1.0)
    cls = gidx * jnp.float32(len(_PAIRS)) + pair
    onehot = lane == cls
    earlier = jnp.dot(tri_ref[...], jnp.where(onehot, 1.0, 0.0).astype(BF16), preferred_element_type=F32)

    @pl.when(jnp.logical_and(pl.program_id(0) == 0, pl.program_id(1) == 0))
    def _():
        run_scr[...] = jnp.zeros_like(run_scr)

    running = run_scr[...]
    rank = jnp.sum(jnp.where(onehot, earlier + running, 0.0), axis=1, keepdims=True)
    running = running + jnp.sum(jnp.where(onehot, 1.0, 0.0), axis=0, keepdims=True)
    run_scr[...] = running
    cnt_ref[...] = running
    w_a = jnp.where(first_is_lower, w_top1, w_top2)
    w_b = jnp.where(first_is_lower, w_top2, w_top1)
    meta = jnp.where(lane_i == META_CLASS, cls,
                     jnp.where(lane_i == META_RANK, rank,
                               jnp.where(lane_i == META_WA, w_a, jnp.where(lane_i == META_WB, w_b, 0.0))))
    meta_ref[...] = meta
    route_ref[...] = meta.T[0:SUBLANES, :].astype(jnp.int32)


def _post(o_list, lse_list, pc, sga, x, mod4, wao, wo, g2, wr, br, expand, *, tm, full_precision, routed):
    nb, rows, _ = x.shape
    tiles = rows // tm
    dils = tuple(o.shape[1] for o in o_list)
    tile3 = lambda b, i: (b, i, 0)
    const2 = lambda b, i: (0, 0)
    mod_rows = mod4.shape[2]
    if mod_rows == 1:
        mod_spec = pl.BlockSpec((None, 6, 1, D_MODEL), lambda b, i: (b, 0, 0, 0))
    else:
        mod_spec = pl.BlockSpec((None, 6, tm, D_MODEL), lambda b, i: (b, 0, i, 0))
    cls4 = lambda b, i: (b, 0, i, 0)
    in_specs = (
        [pl.BlockSpec((None, d, tm // d, GROUP_WIDTH), cls4) for d in dils]
        + [pl.BlockSpec((None, d, tm // d, LANES), cls4) for d in dils]
        + [pl.BlockSpec((None, tm, D_MODEL), tile3)] * 3 + [mod_spec]
        + [pl.BlockSpec(wao.shape, const2), pl.BlockSpec(wo.shape, const2), pl.BlockSpec((1, D_MODEL), const2)]
        + [pl.BlockSpec((D_MODEL, LANES), const2)] * 3
        + [pl.BlockSpec((1, LANES), const2), pl.BlockSpec((LANES, GROUP_WIDTH), const2)]
    )
    wr_hi = wr.astype(BF16)
    wr_lo = (wr - wr_hi.astype(F32)).astype(BF16)
    args = [*o_list, *lse_list, pc, sga, x, mod4, wao, wo, g2, wr, wr_hi, wr_lo, br, expand]
    scratch = [pltpu.VMEM((N_DIL, CHUNKS_PER_GROUP, tm, LANES), F32), pltpu.VMEM((N_DIL, 1, tm, LANES), F32)]
    if routed:
        tri = jnp.asarray(np.tril(np.ones((tm, tm), np.float32), -1), dtype=BF16)
        args.append(tri)
        in_specs = in_specs + [pl.BlockSpec((tm, tm), const2)]
        out_specs = [pl.BlockSpec((None, tm, D_MODEL), tile3), pl.BlockSpec((None, tm * H_ROWS, LANES), tile3),
                     pl.BlockSpec((None, tm, LANES), tile3),
                     pl.BlockSpec((None, None, SUBLANES, tm), lambda b, i: (b, i, 0, 0)),
                     pl.BlockSpec((1, LANES), const2)]
        out_shape = [jax.ShapeDtypeStruct((nb, rows, D_MODEL), F32),
                     jax.ShapeDtypeStruct((nb, rows * H_ROWS, LANES), F32),
                     jax.ShapeDtypeStruct((nb, rows, LANES), F32),
                     jax.ShapeDtypeStruct((nb, tiles, SUBLANES, tm), jnp.int32),
                     jax.ShapeDtypeStruct((1, LANES), F32)]
        scratch.append(pltpu.VMEM((1, LANES), F32))
    else:
        out_specs = [pl.BlockSpec((None, tm, D_MODEL), tile3), pl.BlockSpec((None, tm, D_MODEL), tile3),
                     pl.BlockSpec((None, tm, LANES), tile3)]
        out_shape = [jax.ShapeDtypeStruct((nb, rows, D_MODEL), F32), jax.ShapeDtypeStruct((nb, rows, D_MODEL), F32),
                     jax.ShapeDtypeStruct((nb, rows, LANES), F32)]
    return pl.pallas_call(
        functools.partial(_post_kernel, tm=tm, dils=dils, full_precision=full_precision, routed=routed),
        grid=(nb, tiles),
        in_specs=in_specs,
        out_specs=out_specs,
        out_shape=out_shape,
        scratch_shapes=scratch,
        compiler_params=_cparams(("arbitrary", "arbitrary")),
        name="post_sample" if full_precision else "post_prompt",
    )(*args)


DMA_LOOP_UNROLL = 8
DMA_PRIORITIES = 2


ROUTE_CLASS_ROW, ROUTE_RANK_ROW = META_CLASS, META_RANK


def _sorted_slot(starts_ref, route_ref, r):
    return starts_ref[route_ref[ROUTE_CLASS_ROW, r]] + route_ref[ROUTE_RANK_ROW, r]


def _token_rows(token, rows_per_token):
    return pl.ds(pl.multiple_of(token * rows_per_token, rows_per_token), rows_per_token)


def _dispatch_kernel(starts_ref, route_ref, src_ref, dst_hbm, buf, sems, *, rows_per_step, n_steps):
    i = pl.program_id(0)
    slot = i % 2

    def wait_slot(s):
        def body(r, carry):
            pltpu.make_async_copy(buf.at[s, _token_rows(0, H_ROWS)], dst_hbm.at[_token_rows(0, H_ROWS)],
                                  sems.at[s * rows_per_step + r]).wait()
            return carry
        lax.fori_loop(0, rows_per_step, body, 0, unroll=DMA_LOOP_UNROLL)

    @pl.when(i >= 1)
    def _():
        wait_slot(1 - slot)

    buf[slot] = src_ref[...]

    def issue(pair, carry):
        for prio in range(DMA_PRIORITIES):
            r = pair * DMA_PRIORITIES + prio
            pltpu.make_async_copy(buf.at[slot, _token_rows(r, H_ROWS)],
                                  dst_hbm.at[_token_rows(_sorted_slot(starts_ref, route_ref, r), H_ROWS)],
                                  sems.at[slot * rows_per_step + r]).start(priority=prio)
        return carry

    lax.fori_loop(0, rows_per_step // DMA_PRIORITIES, issue, 0, unroll=DMA_LOOP_UNROLL // DMA_PRIORITIES)

    @pl.when(i == n_steps - 1)
    def _():
        wait_slot(slot)


def _route_spec(rows_per_step, route_tile, shift=0, n_steps=None):
    per_tile = route_tile // rows_per_step

    def imap(i, starts):
        s = i + shift if n_steps is None else jnp.minimum(i + shift, n_steps - 1)
        return (s // per_tile, 0, s % per_tile)

    return pl.BlockSpec((None, SUBLANES, rows_per_step), imap, memory_space=pltpu.SMEM)


def _dispatch_rows(src, route, starts, *, rows_per_step):
    n = src.shape[0] // H_ROWS
    n_steps = n // rows_per_step
    grid_spec = pltpu.PrefetchScalarGridSpec(
        num_scalar_prefetch=1,
        grid=(n_steps,),
        in_specs=[_route_spec(rows_per_step, route.shape[2]),
                  pl.BlockSpec((rows_per_step * H_ROWS, LANES), lambda i, starts: (i, 0))],
        out_specs=pl.BlockSpec(memory_space=pl.ANY),
        scratch_shapes=[pltpu.VMEM((2, rows_per_step * H_ROWS, LANES), src.dtype),
                        pltpu.SemaphoreType.DMA((2 * rows_per_step,))],
    )
    return pl.pallas_call(
        functools.partial(_dispatch_kernel, rows_per_step=rows_per_step, n_steps=n_steps),
        grid_spec=grid_spec,
        out_shape=jax.ShapeDtypeStruct(src.shape, src.dtype),
        compiler_params=_cparams(("arbitrary",)),
        name="moe_dispatch",
    )(starts, route, src)


def _experts_kernel(tile_ref, ea_ref, eb_ref, lo_ref, hi_ref, first_ref, switch_ref, nv_ref,
                    x_ref, w1a_ref, w1b_ref, w3a_ref, w3b_ref, w2a_ref, w2b_ref, y_ref, w13_scr, w2_scr, *, tm):
    i = pl.program_id(0)

    @pl.when(i < nv_ref[0])
    def _():
        @pl.when(switch_ref[i] == 1)
        def _():
            for k, w_ref in enumerate((w1a_ref, w3a_ref, w1b_ref, w3b_ref)):
                w13_scr[k] = w_ref[...].astype(BF16)
            for k, w_ref in enumerate((w2a_ref, w2b_ref)):
                w2_scr[k] = w_ref[...].astype(BF16)

        x = _load_token_rows(x_ref, tm, H_ROWS).astype(BF16)

        def expert(k):
            a = jnp.dot(x, w13_scr[2 * k], preferred_element_type=F32)
            b = jnp.dot(x, w13_scr[2 * k + 1], preferred_element_type=F32)
            hid = (a * _sigmoid(a)) * b
            return jnp.dot(hid.astype(BF16), w2_scr[k], preferred_element_type=F32)

        y_both = jnp.concatenate([expert(0), expert(1)], axis=1)

        @pl.when(first_ref[i] == 1)
        def _():
            _store_token_rows(y_ref, y_both, tm)

        @pl.when(first_ref[i] == 0)
        def _():
            row = lax.broadcasted_iota(jnp.int32, (tm, 1), 0)
            old = _load_token_rows(y_ref, tm, Y_ROWS)
            _store_token_rows(y_ref, jnp.where(row >= lo_ref[i], jnp.where(row < hi_ref[i], y_both, old), old), tm)


def _experts_routed(hs, tables, w1, w3, w2, *, tm):
    n = hs.shape[0] // H_ROWS
    n_items = tables[0].shape[0]
    x_map = lambda i, tile, ea, eb, lo, hi, first, switch, nv: (tile[i], 0)
    wa_map = lambda i, tile, ea, eb, lo, hi, first, switch, nv: (ea[i], 0, 0)
    wb_map = lambda i, tile, ea, eb, lo, hi, first, switch, nv: (eb[i], 0, 0)
    w13 = lambda m: pl.BlockSpec((None, D_MODEL, D_EXPERT), m)
    w2s = lambda m: pl.BlockSpec((None, D_EXPERT, D_MODEL), m)
    grid_spec = pltpu.PrefetchScalarGridSpec(
        num_scalar_prefetch=len(tables),
        grid=(n_items,),
        in_specs=[pl.BlockSpec((tm * H_ROWS, LANES), x_map),
                  w13(wa_map), w13(wb_map), w13(wa_map), w13(wb_map), w2s(wa_map), w2s(wb_map)],
        out_specs=pl.BlockSpec((tm * Y_ROWS, LANES), x_map),
        scratch_shapes=[pltpu.VMEM((4, D_MODEL, D_EXPERT), BF16), pltpu.VMEM((2, D_EXPERT, D_MODEL), BF16)],
    )
    return pl.pallas_call(
        functools.partial(_experts_kernel, tm=tm),
        grid_spec=grid_spec,
        out_shape=jax.ShapeDtypeStruct((n * Y_ROWS, LANES), F32),
        compiler_params=_cparams(("arbitrary",)),
        name="moe_experts",
    )(*tables, hs, w1, w1, w3, w3, w2, w2)


def _combine_kernel(starts_ref, route_ref, route_next_ref, yp_hbm, x1_ref, meta_ref, mod_ref, gf_ref, y_ref, ybuf, sems,
                    *, rows_per_step, n_steps):
    i = pl.program_id(0)
    slot = i % 2

    def issue(r_ref, s):
        def body(pair, carry):
            for prio in range(DMA_PRIORITIES):
                r = pair * DMA_PRIORITIES + prio
                pltpu.make_async_copy(yp_hbm.at[_token_rows(_sorted_slot(starts_ref, r_ref, r), Y_ROWS)],
                                      ybuf.at[s, _token_rows(r, Y_ROWS)],
                                      sems.at[s * rows_per_step + r]).start(priority=prio)
            return carry
        lax.fori_loop(0, rows_per_step // DMA_PRIORITIES, body, 0, unroll=DMA_LOOP_UNROLL // DMA_PRIORITIES)

    @pl.when(i == 0)
    def _():
        issue(route_ref, 0)

    @pl.when(i + 1 < n_steps)
    def _():
        issue(route_next_ref, 1 - slot)

    def wait_body(r, carry):
        pltpu.make_async_copy(yp_hbm.at[_token_rows(0, Y_ROWS)], ybuf.at[slot, _token_rows(r, Y_ROWS)],
                              sems.at[slot * rows_per_step + r]).wait()
        return carry

    lax.fori_loop(0, rows_per_step, wait_body, 0, unroll=DMA_LOOP_UNROLL)
    ya = _load_token_rows(ybuf, rows_per_step, H_ROWS, first_chunk=0, rows_per_token=Y_ROWS, lead=slot)
    yb = _load_token_rows(ybuf, rows_per_step, H_ROWS, first_chunk=H_ROWS, rows_per_token=Y_ROWS, lead=slot)
    meta = meta_ref[...]
    lane = lax.broadcasted_iota(jnp.int32, (1, LANES), 1)
    w_a = jnp.sum(jnp.where(lane == META_WA, meta, 0.0), axis=1, keepdims=True)
    w_b = jnp.sum(jnp.where(lane == META_WB, meta, 0.0), axis=1, keepdims=True)
    x2 = x1_ref[...] + mod_ref[5] * (w_a * ya + w_b * yb)
    var = jnp.mean(x2 * x2, axis=-1, keepdims=True)
    y_ref[...] = (x2 * lax.rsqrt(var + RMS_EPS)) * gf_ref[...]


def _combine_final(yp, route, starts, x1, meta, mod4, gf, *, rows_per_step, rows_per_mod):
    n = x1.shape[0]
    n_steps = n // rows_per_step
    steps_per_mod = rows_per_mod // rows_per_step
    row = lambda i, starts: (i, 0)
    grid_spec = pltpu.PrefetchScalarGridSpec(
        num_scalar_prefetch=1,
        grid=(n_steps,),
        in_specs=[_route_spec(rows_per_step, route.shape[2]),
                  _route_spec(rows_per_step, route.shape[2], shift=1, n_steps=n_steps),
                  pl.BlockSpec(memory_space=pl.ANY),
                  pl.BlockSpec((rows_per_step, D_MODEL), row),
                  pl.BlockSpec((rows_per_step, LANES), row),
                  pl.BlockSpec((None, 6, 1, D_MODEL), lambda i, starts: (i // steps_per_mod, 0, 0, 0)),
                  pl.BlockSpec((1, D_MODEL), lambda i, starts: (0, 0))],
        out_specs=pl.BlockSpec((rows_per_step, D_MODEL), row),
        scratch_shapes=[pltpu.VMEM((2, rows_per_step * Y_ROWS, LANES), F32),
                        pltpu.SemaphoreType.DMA((2 * rows_per_step,))],
    )
    return pl.pallas_call(
        functools.partial(_combine_kernel, rows_per_step=rows_per_step, n_steps=n_steps),
        grid_spec=grid_spec,
        out_shape=jax.ShapeDtypeStruct((n, D_MODEL), F32),
        compiler_params=_cparams(("arbitrary",)),
        name="moe_combine",
    )(starts, route, route, yp, x1, meta, mod4, gf)


def _routing_tables(counts, *, n, tm):
    counts = counts.astype(jnp.int32)
    ends = jnp.cumsum(counts)
    starts = ends - counts
    n_items = n // tm + N_CLASSES
    first_tile = starts // tm
    last_tile = jnp.maximum(ends - 1, starts) // tm
    visits = jnp.where(counts > 0, last_tile - first_tile + 1, 0)
    item_end = jnp.cumsum(visits)
    item_start = item_end - visits
    n_valid = item_end[-1]
    item = jnp.arange(n_items, dtype=jnp.int32)
    idx = jnp.minimum(item, n_valid - 1)
    c = jnp.sum((idx[:, None] >= item_end[None, :]).astype(jnp.int32), axis=1)
    class_ids = jnp.arange(N_CLASSES, dtype=jnp.int32)

    def pick(table):
        return jnp.sum(jnp.where(c[:, None] == class_ids[None, :], table[None, :], 0), axis=1)

    tile = pick(first_tile) + (idx - pick(item_start))
    live = item < n_valid
    lo = jnp.where(live, jnp.clip(pick(starts) - tile * tm, 0, tm), 0)
    hi = jnp.where(live, jnp.clip(pick(ends) - tile * tm, 0, tm), 0)
    prev_tile = jnp.concatenate([jnp.full((1,), -1, jnp.int32), tile[:-1]])
    first = (tile != prev_tile).astype(jnp.int32)
    prev_c = jnp.concatenate([jnp.full((1,), -1, jnp.int32), c[:-1]])
    switch = (c != prev_c).astype(jnp.int32)
    group_of_class = np.arange(N_CLASSES) // len(_PAIRS)
    pair_of_class = np.arange(N_CLASSES) % len(_PAIRS)
    pairs = np.asarray(_PAIRS)
    ea = pick(jnp.asarray(group_of_class * EXPERTS_PER_GROUP + pairs[pair_of_class, 0], dtype=jnp.int32))
    eb = pick(jnp.asarray(group_of_class * EXPERTS_PER_GROUP + pairs[pair_of_class, 1], dtype=jnp.int32))
    as_i32 = lambda a: a.astype(jnp.int32)
    return as_i32(starts), tuple(as_i32(a) for a in (tile, ea, eb, lo, hi, first, switch, n_valid.reshape(1)))


def _moe_kernel(h_ref, comb_ref, x1_ref, mod_ref, gf_ref, w1_ref, w3_ref, w2_ref, y_ref, acc_ref):
    e = pl.program_id(2)

    @pl.when(e == 0)
    def _():
        acc_ref[...] = jnp.zeros_like(acc_ref)

    h = h_ref[...].astype(BF16)
    a = jnp.dot(h, w1_ref[...].astype(BF16), preferred_element_type=F32)
    b = jnp.dot(h, w3_ref[...].astype(BF16), preferred_element_type=F32)
    lane = lax.broadcasted_iota(jnp.int32, (1, LANES), 1)
    cw = jnp.sum(jnp.where(lane == e + N_EXPERT_GROUPS, comb_ref[...], 0.0), axis=1, keepdims=True)
    hid = (a * _sigmoid(a)) * b * cw
    acc_ref[...] += jnp.dot(hid.astype(BF16), w2_ref[...].astype(BF16), preferred_element_type=F32)

    @pl.when(e == N_EXPERTS - 1)
    def _():
        x2 = x1_ref[...] + mod_ref[5] * acc_ref[...]
        var = jnp.mean(x2 * x2, axis=-1, keepdims=True)
        y_ref[...] = (x2 * lax.rsqrt(var + RMS_EPS)) * gf_ref[...]


def _moe_dense(h2, comb, x1, mod4, gf, w1_bf, w3_bf, w2_bf, *, tm):
    nb, rows, _ = x1.shape
    tiles = rows // tm
    tile3 = lambda b, i, e: (b, i, 0)
    if mod4.shape[2] == 1:
        mod_spec = pl.BlockSpec((None, 6, 1, D_MODEL), lambda b, i, e: (b, 0, 0, 0))
    else:
        mod_spec = pl.BlockSpec((None, 6, tm, D_MODEL), lambda b, i, e: (b, 0, i, 0))
    return pl.pallas_call(
        _moe_kernel,
        grid=(nb, tiles, N_EXPERTS),
        in_specs=[
            pl.BlockSpec((None, tm, D_MODEL), tile3),
            pl.BlockSpec((None, tm, LANES), tile3),
            pl.BlockSpec((None, tm, D_MODEL), tile3),
            mod_spec,
            pl.BlockSpec((1, D_MODEL), lambda b, i, e: (0, 0)),
            pl.BlockSpec((None, D_MODEL, D_EXPERT), lambda b, i, e: (e, 0, 0)),
            pl.BlockSpec((None, D_MODEL, D_EXPERT), lambda b, i, e: (e, 0, 0)),
            pl.BlockSpec((None, D_EXPERT, D_MODEL), lambda b, i, e: (e, 0, 0)),
        ],
        out_specs=pl.BlockSpec((None, tm, D_MODEL), tile3),
        out_shape=jax.ShapeDtypeStruct((nb, rows, D_MODEL), F32),
        scratch_shapes=[pltpu.VMEM((tm, D_MODEL), F32)],
        compiler_params=_cparams(("arbitrary", "arbitrary", "arbitrary")),
        name=f"moe_dense_{nb * rows}",
    )(h2, comb, x1, mod4, gf, w1_bf, w3_bf, w2_bf)


def _s_inproj_kernel(x_ref, mod_ref, g1_ref, w_ref, z_ref):
    x = x_ref[...]
    var = jnp.mean(x * x, axis=-1, keepdims=True)
    h = (x * lax.rsqrt(var + RMS_EPS)) * (g1_ref[...] * (1.0 + mod_ref[1])) + mod_ref[0]
    z_ref[...] = jnp.dot(h, w_ref[...], precision=HIGHEST, preferred_element_type=F32)


def _s_inproj(x, mod_tok, g1, w_in, *, col_block):
    n = x.shape[0]
    return pl.pallas_call(
        _s_inproj_kernel,
        grid=(IN_COLS // col_block,),
        in_specs=[pl.BlockSpec((n, D_MODEL), lambda j: (0, 0)),
                  pl.BlockSpec((6, n, D_MODEL), lambda j: (0, 0, 0)),
                  pl.BlockSpec((1, D_MODEL), lambda j: (0, 0)),
                  pl.BlockSpec((D_MODEL, col_block), lambda j: (0, j))],
        out_specs=pl.BlockSpec((n, col_block), lambda j: (0, j)),
        out_shape=jax.ShapeDtypeStruct((n, IN_COLS), F32),
        compiler_params=_cparams(("arbitrary",)),
        name="inproj_sample",
    )(x, mod_tok, g1, w_in)


def _s_mid_kernel(z_ref, p0_ref, p1_ref, cw_ref, wco_ref, rq_ref, rk_ref,
                  cu_ref, pc_ref, sga_ref, q_ref, k_ref, v_ref, *, t_len):
    n = z_ref.shape[0]
    cu = z_ref[:, OFF_GC:OFF_GC + D_MODEL] * z_ref[:, OFF_U:OFF_U + D_MODEL]
    cu_ref[...] = cu
    t = lax.broadcasted_iota(jnp.int32, (n, D_MODEL), 0) & (t_len - 1)
    prev1 = jnp.where(t >= 1, pltpu.roll(cu, 1, 0), p1_ref[...])
    prev2 = jnp.where(t >= 2, pltpu.roll(cu, 2, 0), jnp.where(t == 0, p0_ref[...], p1_ref[...]))
    cw = cw_ref[...]
    conv = cw[0:1] * prev2 + cw[1:2] * prev1 + cw[2:3] * cu
    yc = jnp.dot(z_ref[:, OFF_GB:OFF_GB + D_MODEL] * conv, wco_ref[...], precision=HIGHEST,
                 preferred_element_type=F32)
    pc_ref[...] = _sigmoid(z_ref[:, OFF_GCONV:OFF_GCONV + D_MODEL]) * yc
    sga_ref[...] = _sigmoid(z_ref[:, OFF_GATTN:OFF_GATTN + D_MODEL])
    aq, bmq, bpq = rq_ref[0], rq_ref[1], rq_ref[2]
    ak, bmk, bpk = rk_ref[0], rk_ref[1], rk_ref[2]
    for c in range(ATTN_WIDTH // LANES):
        sl = slice(c * LANES, (c + 1) * LANES)
        q_ref[:, sl] = _rope_chunk(z_ref[:, OFF_Q + c * LANES:OFF_Q + (c + 1) * LANES], aq, bmq, bpq)
        k_ref[:, sl] = _rope_chunk(z_ref[:, OFF_K + c * LANES:OFF_K + (c + 1) * LANES], ak, bmk, bpk)
    v_ref[...] = z_ref[:, OFF_V:OFF_V + ATTN_WIDTH]


def _s_mid(z, p0e, p1e, conv_w, w_conv_out, rope_q, rope_k, *, t_len):
    n = z.shape[0]
    assert t_len & (t_len - 1) == 0
    full = lambda shape: pl.BlockSpec(shape, lambda i: (0,) * len(shape))
    out_shape = [jax.ShapeDtypeStruct((n, D_MODEL), F32)] * 3 + [jax.ShapeDtypeStruct((n, ATTN_WIDTH), F32)] * 3
    return pl.pallas_call(
        functools.partial(_s_mid_kernel, t_len=t_len),
        grid=(1,),
        in_specs=[full(z.shape), full(p0e.shape), full(p1e.shape), full(conv_w.shape), full(w_conv_out.shape),
                  full(rope_q.shape), full(rope_k.shape)],
        out_specs=[full((n, D_MODEL))] * 3 + [full((n, ATTN_WIDTH))] * 3,
        out_shape=out_shape,
        compiler_params=_cparams(("arbitrary",)),
        name="mid_sample",
    )(z, p0e, p1e, conv_w, w_conv_out, rope_q, rope_k)


def _head_sum(x):
    return jnp.sum(x.reshape(HEADS_PER_GROUP, HEAD_DIM, x.shape[-1]), axis=1)


def _head_expand(x):
    n = x.shape[-1]
    return jnp.broadcast_to(x[:, None, :], (HEADS_PER_GROUP, HEAD_DIM, n)).reshape(GROUP_WIDTH, n)


def _s_attn_disjoint(q_ref, kn, vn, ck_ref, cv_ref, o_ref, lse_ref, s_scr, lane, new_idx, *, n_tiles, dil, t_len):
    cls = lane & (dil - 1)
    q_all = q_ref[...]
    qsel = jnp.zeros((GROUP_WIDTH, LANES), F32)
    for t in range(t_len):
        qsel = jnp.where(cls == t, jnp.broadcast_to(q_all[:, t:t + 1], (GROUP_WIDTH, LANES)), qsel)
    smax = None
    for j in range(n_tiles):
        s = jnp.where(cls < t_len, _head_sum(ck_ref[:, j * LANES:(j + 1) * LANES] * qsel), NEG_BIG)
        s_scr[:, j * LANES:(j + 1) * LANES] = s
        smax = s if smax is None else jnp.maximum(smax, s)
    s_new = _head_sum(kn * q_all)
    m_cols = []
    m_lane = jnp.zeros((HEADS_PER_GROUP, LANES), F32)
    m_new = jnp.zeros((HEADS_PER_GROUP, t_len), F32)
    for t in range(t_len):
        mt = jnp.maximum(jnp.max(jnp.where(cls == t, smax, NEG_BIG), axis=1, keepdims=True), s_new[:, t:t + 1])
        m_cols.append(mt)
        m_lane = jnp.where(cls == t, mt, m_lane)
        m_new = jnp.where(new_idx == t, mt, m_new)
    esum = jnp.zeros((HEADS_PER_GROUP, LANES), F32)
    for j in range(n_tiles):
        e = jnp.exp(s_scr[:, j * LANES:(j + 1) * LANES] - m_lane)
        s_scr[:, j * LANES:(j + 1) * LANES] = e
        esum = esum + e
    e_new = jnp.exp(s_new - m_new)
    l_cols = []
    inv_lane = jnp.zeros((HEADS_PER_GROUP, LANES), F32)
    inv_new = jnp.zeros((HEADS_PER_GROUP, t_len), F32)
    for t in range(t_len):
        lt = jnp.sum(jnp.where(cls == t, esum, 0.0), axis=1, keepdims=True) + e_new[:, t:t + 1]
        l_cols.append(lt)
        inv_lane = jnp.where(cls == t, 1.0 / lt, inv_lane)
        inv_new = jnp.where(new_idx == t, 1.0 / lt, inv_new)
    acc = None
    for j in range(n_tiles):
        term = cv_ref[:, j * LANES:(j + 1) * LANES] * _head_expand(s_scr[:, j * LANES:(j + 1) * LANES] * inv_lane)
        acc = term if acc is None else acc + term
    o_new = vn * _head_expand(e_new * inv_new)
    for t in range(t_len):
        o_ref[:, t:t + 1] = jnp.sum(jnp.where(cls == t, acc, 0.0), axis=1, keepdims=True) + o_new[:, t:t + 1]
        lse_ref[:, t:t + 1] = m_cols[t] + jnp.log(l_cols[t])


def _s_attn_kernel(q_ref, kn_ref, vn_ref, ck_ref, cv_ref, o_ref, lse_ref, ko_ref, vo_ref, s_scr,
                   *, win, dil, t_len):
    n_tiles = win // LANES
    lane = lax.broadcasted_iota(jnp.int32, (1, LANES), 1)
    new_idx = lax.broadcasted_iota(jnp.int32, (1, t_len), 1)
    kn = kn_ref[...]
    vn = vn_ref[...]
    if dil >= t_len:
        _s_attn_disjoint(q_ref, kn, vn, ck_ref, cv_ref, o_ref, lse_ref, s_scr, lane, new_idx,
                         n_tiles=n_tiles, dil=dil, t_len=t_len)
    for t in range(t_len if dil < t_len else 0):
        qb = jnp.broadcast_to(q_ref[:, t:t + 1], (GROUP_WIDTH, LANES))
        m = None
        for j in range(n_tiles):
            pos = lane + j * LANES
            s = _head_sum(ck_ref[:, j * LANES:(j + 1) * LANES] * qb)
            s = jnp.where(pos >= t, jnp.where(((pos - t) & (dil - 1)) == 0, s, NEG_BIG), NEG_BIG)
            s_scr[:, j * LANES:(j + 1) * LANES] = s
            mj = jnp.max(s, axis=1, keepdims=True)
            m = mj if m is None else jnp.maximum(m, mj)
        s_new = _head_sum(kn * qb[:, 0:t_len])
        s_new = jnp.where(new_idx <= t, jnp.where(((t - new_idx) & (dil - 1)) == 0, s_new, NEG_BIG), NEG_BIG)
        m = jnp.maximum(m, jnp.max(s_new, axis=1, keepdims=True))
        e_new = jnp.exp(s_new - m)
        l = jnp.sum(e_new, axis=1, keepdims=True)
        acc = None
        for j in range(n_tiles):
            e = jnp.exp(s_scr[:, j * LANES:(j + 1) * LANES] - m)
            l = l + jnp.sum(e, axis=1, keepdims=True)
            term = cv_ref[:, j * LANES:(j + 1) * LANES] * _head_expand(e)
            acc = term if acc is None else acc + term
        o = jnp.sum(acc, axis=1, keepdims=True) + jnp.sum(vn * _head_expand(e_new), axis=1, keepdims=True)
        o_ref[:, t:t + 1] = o * _head_expand(1.0 / l)
        lse_ref[:, t:t + 1] = m + jnp.log(l)

    for c_ref, new, out_ref in ((ck_ref, kn, ko_ref), (cv_ref, vn, vo_ref)):
        out_ref[...] = pltpu.roll(c_ref[...], win - t_len, 1)
        out_ref[:, win - t_len:win] = new


def _s_attn_group(q_t, kn_t, vn_t, cache_k, cache_v, g):
    bsz, _, t_len = q_t.shape
    win_full, dil = DIL_GROUPS[g]
    win = cache_k.shape[2]
    assert win == win_full and win == (KEYS_PER_QUERY - 1) * dil and win % LANES == 0
    assert dil & (dil - 1) == 0 and LANES % dil == 0
    grp = lambda b: (b, g, 0)
    per_b = lambda b: (b, 0, 0)
    return pl.pallas_call(
        functools.partial(_s_attn_kernel, win=win, dil=dil, t_len=t_len),
        grid=(bsz,),
        in_specs=[pl.BlockSpec((None, GROUP_WIDTH, t_len), grp)] * 3
        + [pl.BlockSpec((None, GROUP_WIDTH, win), per_b)] * 2,
        out_specs=[pl.BlockSpec((None, GROUP_WIDTH, t_len), per_b),
                   pl.BlockSpec((None, HEADS_PER_GROUP, t_len), per_b),
                   pl.BlockSpec((None, GROUP_WIDTH, win), per_b),
                   pl.BlockSpec((None, GROUP_WIDTH, win), per_b)],
        out_shape=[jax.ShapeDtypeStruct((bsz, GROUP_WIDTH, t_len), F32),
                   jax.ShapeDtypeStruct((bsz, HEADS_PER_GROUP, t_len), F32),
                   jax.ShapeDtypeStruct((bsz, GROUP_WIDTH, win), F32),
                   jax.ShapeDtypeStruct((bsz, GROUP_WIDTH, win), F32)],
        scratch_shapes=[pltpu.VMEM((HEADS_PER_GROUP, win), F32)],
        compiler_params=_cparams(("arbitrary",)),
        name=f"attn_sample_g{g}",
    )(q_t, kn_t, vn_t, cache_k, cache_v)


TM_INPROJ = 512
TM_POST = 512
TM_EXPERT = 256
ROWS_PER_DMA_STEP = 256
ATTN_ROWS_PER_STEP = 1024
S_COL_BLOCK = 512


def _to_state(a_t):
    b, _, length = a_t.shape
    return jnp.transpose(a_t.reshape(b, HEADS_PER_GROUP, HEAD_DIM, length), (0, 3, 1, 2))[None]


def _from_state(a):
    b, length = a.shape[0], a.shape[1]
    return jnp.transpose(a, (0, 2, 3, 1)).reshape(b, GROUP_WIDTH, length)


def kernel(x_prompt, x_sample, cache_k1, cache_v1, cache_k2, cache_v2, cache_k3, cache_v3, state_conv,
           c_prompt, c_sample, norm1_g, norm2_g, normf_g, w_ada, b_ada, w_in, conv_w, w_conv_out,
           w_attn_out, w_o, w_rg, b_rg, w_re, b_re, w1, w3, w2):
    depth = w_in.shape[0]
    assert depth == 1
    bsz, seq, _ = x_prompt.shape
    dbsz, t_len, _ = x_sample.shape
    n_s = dbsz * t_len
    l = 0

    w_in_bf = w_in[l].astype(BF16)
    wco_bf = w_conv_out[l].astype(BF16)
    wao_bf = w_attn_out[l].astype(BF16)
    wo_bf = w_o[l].astype(BF16)
    w1_e = w1[l].reshape(N_EXPERTS, D_MODEL, D_EXPERT)
    w3_e = w3[l].reshape(N_EXPERTS, D_MODEL, D_EXPERT)
    w2_e = w2[l].reshape(N_EXPERTS, D_EXPERT, D_MODEL)
    g1 = norm1_g[l].reshape(1, D_MODEL)
    g2 = norm2_g[l].reshape(1, D_MODEL)
    gf = normf_g.reshape(1, D_MODEL)
    n_route = N_EXPERT_GROUPS + N_EXPERTS
    wr = jnp.pad(jnp.concatenate([w_rg[l], w_re[l]], axis=1), ((0, 0), (0, LANES - n_route)))
    br = jnp.pad(jnp.concatenate([b_rg[l], b_re[l]]), (0, LANES - n_route)).reshape(1, LANES)
    head_of_lane = np.arange(GROUP_WIDTH) // HEAD_DIM
    expand_bf = jnp.asarray((np.arange(LANES)[:, None] == head_of_lane[None, :]).astype(np.float32), dtype=BF16)

    mod = _adaln(jnp.concatenate([c_prompt, c_sample], axis=0), w_ada[l], b_ada[l])
    mod_p4 = mod[:bsz].reshape(bsz, 6, 1, D_MODEL)
    mod_tok = jnp.repeat(mod[bsz:].reshape(dbsz, 1, 6, D_MODEL), t_len, axis=1)
    mod_tok = jnp.transpose(mod_tok.reshape(n_s, 6, D_MODEL), (1, 0, 2))
    mod_s4 = mod_tok[None]

    pos_p = jnp.arange(seq, dtype=jnp.int32)
    rope_q_p = _rope_tables(pos_p, HEAD_DIM ** -0.5)
    rope_k_p = _rope_tables(pos_p, 1.0)
    pc_p, sga_p, conv_p = _convproj_prompt(x_prompt, mod_p4, g1, w_in_bf[:, :OFF_Q], w_in_bf[:, OFF_GCONV:],
                                           conv_w[l], wco_bf, tm=TM_INPROJ)
    outs = _qkvproj_prompt(x_prompt, mod_p4, g1, w_in_bf[:, OFF_Q:OFF_GCONV], rope_q_p, rope_k_p, tm=TM_INPROJ)
    qkv_p = outs[0:3 * N_DIL]
    states_p = outs[3 * N_DIL:]
    o_p, lse_p = [], []
    for g in range(N_DIL):
        o_g, lse_g = _attn_prompt_group(qkv_p[3 * g], qkv_p[3 * g + 1], qkv_p[3 * g + 2], g, max_rows=ATTN_ROWS_PER_STEP)
        o_p.append(o_g)
        lse_p.append(lse_g)
    x1_p, h2rows_p, meta_p, route_p, cnt_p = _post(o_p, lse_p, pc_p, sga_p, x_prompt, mod_p4, wao_bf, wo_bf, g2, wr, br,
                                                   expand_bf, tm=TM_POST, full_precision=False, routed=True)
    n_p = bsz * seq
    route = route_p.reshape(n_p // TM_POST, SUBLANES, TM_POST)
    starts, tables = _routing_tables(cnt_p[0, :N_CLASSES], n=n_p, tm=TM_EXPERT)
    hs = _dispatch_rows(h2rows_p.reshape(n_p * H_ROWS, LANES), route, starts, rows_per_step=ROWS_PER_DMA_STEP)
    yp = _experts_routed(hs, tables, w1_e, w3_e, w2_e, tm=TM_EXPERT)
    y_p = _combine_final(yp, route, starts, x1_p.reshape(n_p, D_MODEL), meta_p.reshape(n_p, LANES), mod_p4, gf,
                         rows_per_step=ROWS_PER_DMA_STEP, rows_per_mod=seq).reshape(bsz, seq, D_MODEL)

    pos_s = PAST_LEN + jnp.arange(t_len, dtype=jnp.int32)
    rope_q_s = jnp.tile(_rope_tables(pos_s, HEAD_DIM ** -0.5), (1, dbsz, 1))
    rope_k_s = jnp.tile(_rope_tables(pos_s, 1.0), (1, dbsz, 1))
    xs = x_sample.reshape(n_s, D_MODEL)
    z_s = _s_inproj(xs, mod_tok, g1, w_in[l], col_block=S_COL_BLOCK)
    past = state_conv[l]
    p0e = jnp.repeat(past[:, 0], t_len, axis=0)
    p1e = jnp.repeat(past[:, 1], t_len, axis=0)
    cu_s, pc_s, sga_s, q_s, k_s, v_s = _s_mid(z_s, p0e, p1e, conv_w[l], w_conv_out[l], rope_q_s, rope_k_s,
                                              t_len=t_len)
    to_cols = lambda a: jnp.transpose(a.reshape(dbsz, t_len, ATTN_WIDTH), (0, 2, 1))
    q_t, kn_t, vn_t = to_cols(q_s), to_cols(k_s), to_cols(v_s)
    caches = ((cache_k1, cache_v1), (cache_k2, cache_v2), (cache_k3, cache_v3))
    o_s, lse_s, kv_s = [], [], []
    for g, (ck, cv) in enumerate(caches):
        o_g, lse_g, ko, vo = _s_attn_group(q_t, kn_t, vn_t, _from_state(ck[l]), _from_state(cv[l]), g)
        o_s.append(jnp.transpose(o_g, (0, 2, 1)).reshape(1, 1, n_s, GROUP_WIDTH))
        lse_rows = jnp.transpose(lse_g, (0, 2, 1)).reshape(n_s, HEADS_PER_GROUP)
        lse_s.append(jnp.pad(lse_rows, ((0, 0), (0, LANES - HEADS_PER_GROUP))).reshape(1, 1, n_s, LANES))
        kv_s += [_to_state(ko), _to_state(vo)]
    x1_s, h2_s, comb_s = _post(o_s, lse_s, pc_s[None], sga_s[None], xs[None], mod_s4, w_attn_out[l], w_o[l], g2, wr, br,
                               expand_bf, tm=n_s, full_precision=True, routed=False)
    y_s = _moe_dense(h2_s, comb_s, x1_s, mod_s4, gf, w1_e, w3_e, w2_e, tm=n_s)

    conv_s = cu_s.reshape(dbsz, t_len, D_MODEL)[:, t_len - (CONV_K - 1):]
    return (y_p, y_s.reshape(dbsz, t_len, D_MODEL),
            *[_to_state(a) for a in states_p], conv_p.reshape(1, bsz, CONV_K - 1, D_MODEL),
            *kv_s, conv_s.reshape(1, dbsz, CONV_K - 1, D_MODEL))
```

```python
import functools

import numpy as np
import jax
import jax.numpy as jnp
from jax import lax
from jax.experimental import pallas as pl
from jax.experimental.pallas import tpu as pltpu

F32 = jnp.float32
BF16 = jnp.bfloat16
HIGHEST = lax.Precision.HIGHEST

D_MODEL = 1024
HEAD_DIM = 64
HEADS_PER_GROUP = 8
GROUP_WIDTH = HEADS_PER_GROUP * HEAD_DIM
DIL_GROUPS = ((128, 1), (512, 4), (2048, 16))
N_DIL = len(DIL_GROUPS)
ATTN_WIDTH = N_DIL * GROUP_WIDTH
ROT_DIM = HEAD_DIM // 4
ROPE_THETA = 500000.0
PAST_LEN = 16384
CONV_K = 3
N_EXPERT_GROUPS = 4
EXPERTS_PER_GROUP = 4
N_EXPERTS = N_EXPERT_GROUPS * EXPERTS_PER_GROUP
D_EXPERT = 512
RMS_EPS = 1e-6
IN_COLS = 3 * D_MODEL + 3 * ATTN_WIDTH + 2 * D_MODEL
OFF_U, OFF_GC, OFF_GB = 0, D_MODEL, 2 * D_MODEL
OFF_Q = 3 * D_MODEL
OFF_K = OFF_Q + ATTN_WIDTH
OFF_V = OFF_K + ATTN_WIDTH
OFF_GCONV = OFF_V + ATTN_WIDTH
OFF_GATTN = OFF_GCONV + D_MODEL

LANES = 128
SUBLANES = 8
CHUNKS_PER_GROUP = GROUP_WIDTH // LANES
KEYS_PER_QUERY = 129
Q_BLOCK = 128
NEG_BIG = -1e30

VMEM_LIMIT = 56 * 1024 * 1024


def _sigmoid(x):
    return 1.0 / (1.0 + jnp.exp(-x))


def _cparams(sem):
    return pltpu.CompilerParams(dimension_semantics=sem, vmem_limit_bytes=VMEM_LIMIT)


def _adaln_kernel(c_ref, w_ref, b_ref, o_ref):
    c = c_ref[...]
    s = c * _sigmoid(c)
    o_ref[...] = jnp.dot(s, w_ref[...], precision=HIGHEST, preferred_element_type=F32) + b_ref[...]


def _adaln(c_all, w_ada, b_ada):
    rows = c_all.shape[0]
    n_col = w_ada.shape[1] // D_MODEL
    return pl.pallas_call(
        _adaln_kernel,
        grid=(n_col,),
        in_specs=[
            pl.BlockSpec((rows, D_MODEL), lambda j: (0, 0)),
            pl.BlockSpec((D_MODEL, D_MODEL), lambda j: (0, j)),
            pl.BlockSpec((1, D_MODEL), lambda j: (0, j)),
        ],
        out_specs=pl.BlockSpec((rows, D_MODEL), lambda j: (0, j)),
        out_shape=jax.ShapeDtypeStruct((rows, w_ada.shape[1]), F32),
        compiler_params=_cparams(("arbitrary",)),
        name="adaln",
    )(c_all, w_ada, b_ada.reshape(1, -1))


def _rope_tables(pos, scale):
    half = ROT_DIM // 2
    inv_freq = jnp.power(jnp.float32(ROPE_THETA), -jnp.arange(half, dtype=F32) / half)
    ang = pos.astype(F32)[:, None] * inv_freq[None, :]
    cos, sin = jnp.cos(ang), jnp.sin(ang)
    lane_in_head = np.arange(LANES) % HEAD_DIM
    freq = lane_in_head % half
    first = lane_in_head < half
    second = (lane_in_head >= half) & (lane_in_head < ROT_DIM)
    a = jnp.where(first | second, cos[:, freq], 1.0)
    bm = jnp.where(first, -sin[:, freq], 0.0)
    bp = jnp.where(second, sin[:, freq], 0.0)
    return jnp.stack([a, bm, bp]) * scale


def _rope_chunk(zc, a, bm, bp):
    return zc * a + pltpu.roll(zc, LANES - ROT_DIM // 2, 1) * bm + pltpu.roll(zc, ROT_DIM // 2, 1) * bp


def _modulated_norm_bf16(x_ref, mod_ref, g1_ref):
    x = x_ref[...]
    var = jnp.mean(x * x, axis=-1, keepdims=True)
    h = (x * lax.rsqrt(var + RMS_EPS)) * (g1_ref[...] * (1.0 + mod_ref[1])) + mod_ref[0]
    return h.astype(BF16)


def _convproj_kernel(x_ref, mod_ref, g1_ref, wa_ref, wg_ref, cw_ref, wco_ref, pc_ref, sga_ref, cst_ref, s_ref,
                     *, tm, n_tiles):
    i = pl.program_id(1)
    hb = _modulated_norm_bf16(x_ref, mod_ref, g1_ref)

    def proj(w_ref, lo):
        return jnp.dot(hb, w_ref[:, lo:lo + D_MODEL], preferred_element_type=F32)

    cu = proj(wa_ref, OFF_GC) * proj(wa_ref, OFF_U)

    @pl.when(i == 0)
    def _():
        s_ref[0:SUBLANES, :] = jnp.zeros((SUBLANES, D_MODEL), F32)

    s_ref[SUBLANES:SUBLANES + tm, :] = cu
    cw = cw_ref[...]
    conv = (cw[0:1] * s_ref[SUBLANES - 2:SUBLANES - 2 + tm, :]
            + cw[1:2] * s_ref[SUBLANES - 1:SUBLANES - 1 + tm, :]
            + cw[2:3] * cu)
    yc = jnp.dot((proj(wa_ref, OFF_GB) * conv).astype(BF16), wco_ref[...], preferred_element_type=F32)
    pc_ref[...] = (_sigmoid(proj(wg_ref, 0)) * yc).astype(pc_ref.dtype)
    sga_ref[...] = _sigmoid(proj(wg_ref, D_MODEL)).astype(sga_ref.dtype)

    @pl.when(i == n_tiles - 1)
    def _():
        cst_ref[...] = s_ref[tm + SUBLANES - 2:tm + SUBLANES, :]

    s_ref[0:SUBLANES, :] = s_ref[tm:tm + SUBLANES, :]


def _convproj_prompt(x, mod4, g1, wa_bf, wg_bf, conv_w, wco_bf, *, tm):
    bsz, seq, _ = x.shape
    n_tiles = seq // tm
    const2 = lambda b, i: (0, 0)
    tile3 = lambda b, i: (b, i, 0)
    return pl.pallas_call(
        functools.partial(_convproj_kernel, tm=tm, n_tiles=n_tiles),
        grid=(bsz, n_tiles),
        in_specs=[
            pl.BlockSpec((None, tm, D_MODEL), tile3),
            pl.BlockSpec((None, 6, 1, D_MODEL), lambda b, i: (b, 0, 0, 0)),
            pl.BlockSpec((1, D_MODEL), const2),
            pl.BlockSpec(wa_bf.shape, const2, pipeline_mode=pl.Buffered(1)),
            pl.BlockSpec(wg_bf.shape, const2, pipeline_mode=pl.Buffered(1)),
            pl.BlockSpec((CONV_K, D_MODEL), const2),
            pl.BlockSpec((D_MODEL, D_MODEL), const2, pipeline_mode=pl.Buffered(1)),
        ],
        out_specs=[pl.BlockSpec((None, tm, D_MODEL), tile3), pl.BlockSpec((None, tm, D_MODEL), tile3),
                   pl.BlockSpec((None, CONV_K - 1, D_MODEL), lambda b, i: (b, 0, 0))],
        out_shape=[jax.ShapeDtypeStruct((bsz, seq, D_MODEL), BF16), jax.ShapeDtypeStruct((bsz, seq, D_MODEL), BF16),
                   jax.ShapeDtypeStruct((bsz, CONV_K - 1, D_MODEL), F32)],
        scratch_shapes=[pltpu.VMEM((tm + SUBLANES, D_MODEL), F32)],
        compiler_params=_cparams(("arbitrary", "arbitrary")),
        name="convproj_prompt",
    )(x, mod4, g1, wa_bf, wg_bf, conv_w, wco_bf)


def _qkvproj_kernel(x_ref, mod_ref, g1_ref, w_ref, rq_ref, rk_ref, *rest, tm, n_tiles, seq):
    qkv_refs = rest[0:3 * N_DIL]
    st_refs = rest[3 * N_DIL:3 * N_DIL + 2 * N_DIL]
    d_ref = rest[-1]
    i = pl.program_id(1)
    hb = _modulated_norm_bf16(x_ref, mod_ref, g1_ref)

    def proj(lo, width):
        return jnp.dot(hb, w_ref[:, lo:lo + width], preferred_element_type=F32)

    zq = proj(0, ATTN_WIDTH)
    zk = proj(ATTN_WIDTH, ATTN_WIDTH)
    zv = proj(2 * ATTN_WIDTH, ATTN_WIDTH)
    aq, bmq, bpq = rq_ref[0], rq_ref[1], rq_ref[2]
    ak, bmk, bpk = rk_ref[0], rk_ref[1], rk_ref[2]
    n_chunks = ATTN_WIDTH // LANES
    q_chunks, k_chunks, v_chunks = [], [], []
    for c in range(n_chunks):
        sl = slice(c * LANES, (c + 1) * LANES)
        q_chunks.append(_rope_chunk(zq[:, sl], aq, bmq, bpq))
        k_chunks.append(_rope_chunk(zk[:, sl], ak, bmk, bpk))
        v_chunks.append(zv[:, sl])

    for which, chunks in enumerate((q_chunks, k_chunks, v_chunks)):
        for g in range(N_DIL):
            out_ref = qkv_refs[3 * g + which]
            dil = DIL_GROUPS[g][1]
            for cc in range(CHUNKS_PER_GROUP):
                c = g * CHUNKS_PER_GROUP + cc
                sl = slice(cc * LANES, (cc + 1) * LANES)
                if dil == 1:
                    out_ref[0, :, sl] = chunks[c].astype(BF16)
                else:
                    d_ref[c] = chunks[c]
                    for r in range(dil):
                        out_ref[r, :, sl] = d_ref[c, pl.ds(r, tm // dil, stride=dil), :].astype(BF16)

    for g in range(N_DIL):
        kst, vst = st_refs[2 * g], st_refs[2 * g + 1]
        win = min(DIL_GROUPS[g][0], seq)
        if win >= tm:
            cond, r0 = i >= (seq - win) // tm, 0
        else:
            cond, r0 = i == n_tiles - 1, tm - win

        @pl.when(cond)
        def _(g=g, kst=kst, vst=vst, r0=r0):
            for cc in range(CHUNKS_PER_GROUP):
                c = g * CHUNKS_PER_GROUP + cc
                kst[cc * LANES:(cc + 1) * LANES, :] = k_chunks[c][r0:, :].T
                vst[cc * LANES:(cc + 1) * LANES, :] = v_chunks[c][r0:, :].T


def _qkvproj_prompt(x, mod4, g1, wqkv_bf, rope_q, rope_k, *, tm):
    bsz, seq, _ = x.shape
    n_tiles = seq // tm
    const2 = lambda b, i: (0, 0)
    tile3 = lambda b, i: (b, i, 0)
    in_specs = [
        pl.BlockSpec((None, tm, D_MODEL), tile3),
        pl.BlockSpec((None, 6, 1, D_MODEL), lambda b, i: (b, 0, 0, 0)),
        pl.BlockSpec((1, D_MODEL), const2),
        pl.BlockSpec(wqkv_bf.shape, const2, pipeline_mode=pl.Buffered(1)),
        pl.BlockSpec((3, tm, LANES), lambda b, i: (0, i, 0)),
        pl.BlockSpec((3, tm, LANES), lambda b, i: (0, i, 0)),
    ]
    out_shape, out_specs = [], []
    for _, dil in DIL_GROUPS:
        assert tm % (dil * 16) == 0
        for _ in range(3):
            out_shape.append(jax.ShapeDtypeStruct((bsz, dil, seq // dil, GROUP_WIDTH), BF16))
            out_specs.append(pl.BlockSpec((None, dil, tm // dil, GROUP_WIDTH), lambda b, i: (b, 0, i, 0)))
    for win, _ in DIL_GROUPS:
        win = min(win, seq)
        cols = min(win, tm)
        if win >= tm:
            imap = lambda b, i, ft=(seq - win) // tm: (b, 0, jnp.maximum(i - ft, 0))
        else:
            imap = lambda b, i: (b, 0, 0)
        for _ in range(2):
            out_shape.append(jax.ShapeDtypeStruct((bsz, GROUP_WIDTH, win), F32))
            out_specs.append(pl.BlockSpec((None, GROUP_WIDTH, cols), imap))
    return pl.pallas_call(
        functools.partial(_qkvproj_kernel, tm=tm, n_tiles=n_tiles, seq=seq),
        grid=(bsz, n_tiles),
        in_specs=in_specs,
        out_specs=out_specs,
        out_shape=out_shape,
        scratch_shapes=[pltpu.VMEM((ATTN_WIDTH // LANES, tm, LANES), F32)],
        compiler_params=_cparams(("arbitrary", "arbitrary")),
        name="qkvproj_prompt",
    )(x, mod4, g1, wqkv_bf, rope_q, rope_k)


ATTN_UNROLL = 4


def _attn_kernel(*refs, chunk, has_halo, n_cls):
    if has_halo:
        q_ref, k_ref, v_ref, kh_ref, vh_ref, o_ref, lse_ref, kbuf, vbuf = refs
    else:
        q_ref, k_ref, v_ref, o_ref, lse_ref, kbuf, vbuf = refs
    c = pl.program_id(2)
    if has_halo:
        kbuf[:, 0:Q_BLOCK, :] = kh_ref[...]
        vbuf[:, 0:Q_BLOCK, :] = vh_ref[...]
    else:
        kbuf[:, 0:Q_BLOCK, :] = jnp.zeros((n_cls, Q_BLOCK, GROUP_WIDTH), BF16)
        vbuf[:, 0:Q_BLOCK, :] = jnp.zeros((n_cls, Q_BLOCK, GROUP_WIDTH), BF16)
    kbuf[:, Q_BLOCK:Q_BLOCK + chunk, :] = k_ref[...]
    vbuf[:, Q_BLOCK:Q_BLOCK + chunk, :] = v_ref[...]
    blocks_per_cls = chunk // Q_BLOCK

    row = lax.broadcasted_iota(jnp.int32, (Q_BLOCK, 2 * Q_BLOCK), 0)
    col = lax.broadcasted_iota(jnp.int32, (Q_BLOCK, 2 * Q_BLOCK), 1)
    bias_main = jnp.where(col >= row, jnp.where(col <= row + Q_BLOCK, 0.0, NEG_BIG), NEG_BIG)
    bias_first = jnp.where(col >= Q_BLOCK, bias_main, NEG_BIG)
    lane = lax.broadcasted_iota(jnp.int32, (1, LANES), 1)
    lo_half = lane < HEAD_DIM

    def body(u, carry):
        cls, qb = u // blocks_per_cls, u % blocks_per_cls
        r0 = pl.multiple_of(qb * Q_BLOCK, Q_BLOCK)
        kt = kbuf[cls, pl.ds(r0, 2 * Q_BLOCK), :]
        vt = vbuf[cls, pl.ds(r0, 2 * Q_BLOCK), :]
        qt = q_ref[cls, pl.ds(r0, Q_BLOCK), :]
        is_first = jnp.logical_and(qb == 0, c == 0)
        bias = jnp.where(is_first, bias_first, bias_main)
        bias2 = jnp.concatenate([bias, bias], axis=0)
        lse_tile = jnp.zeros((Q_BLOCK, LANES), F32)
        for p in range(CHUNKS_PER_GROUP):
            sl = slice(p * LANES, (p + 1) * LANES)
            qp, kp, vp = qt[:, sl], kt[:, sl], vt[:, sl]
            zero = jnp.zeros_like(qp)
            q2 = jnp.concatenate([jnp.where(lo_half, qp, zero), jnp.where(lo_half, zero, qp)], axis=0)
            s = lax.dot_general(q2, kp, (((1,), (1,)), ((), ())), preferred_element_type=F32) + bias2
            m = jnp.max(s, axis=1, keepdims=True)
            e = jnp.exp(s - m)
            l = jnp.sum(e, axis=1, keepdims=True)
            o = jnp.dot(e.astype(BF16), vp, preferred_element_type=F32) * (1.0 / l)
            lse = m + jnp.log(l)
            for hh in range(2):
                lse_tile = jnp.where(lane == 2 * p + hh, lse[hh * Q_BLOCK:(hh + 1) * Q_BLOCK], lse_tile)
            o_ref[cls, pl.ds(r0, Q_BLOCK), sl] = jnp.where(lo_half, o[:Q_BLOCK], o[Q_BLOCK:])
        lse_ref[cls, pl.ds(r0, Q_BLOCK), :] = lse_tile
        return carry

    n_units = n_cls * blocks_per_cls
    lax.fori_loop(0, n_units, body, 0, unroll=min(ATTN_UNROLL, n_units))


def _attn_prompt_group(q, k, v, g, *, max_rows):
    bsz, dil, cls_len, _ = q.shape
    chunk = min(max_rows, cls_len)
    n_chunks = cls_len // chunk
    n_cls = min(dil, max_rows // chunk)
    has_halo = n_chunks > 1
    main = pl.BlockSpec((None, n_cls, chunk, GROUP_WIDTH), lambda b, r, c: (b, r, c, 0))
    in_specs = [main, main, main]
    args = [q, k, v]
    if has_halo:
        per = chunk // Q_BLOCK
        halo = pl.BlockSpec((None, n_cls, Q_BLOCK, GROUP_WIDTH), lambda b, r, c: (b, r, jnp.maximum(c * per - 1, 0), 0))
        in_specs += [halo, halo]
        args += [k, v]
    return pl.pallas_call(
        functools.partial(_attn_kernel, chunk=chunk, has_halo=has_halo, n_cls=n_cls),
        grid=(bsz, dil // n_cls, n_chunks),
        in_specs=in_specs,
        out_specs=[pl.BlockSpec((None, n_cls, chunk, GROUP_WIDTH), lambda b, r, c: (b, r, c, 0)),
                   pl.BlockSpec((None, n_cls, chunk, LANES), lambda b, r, c: (b, r, c, 0))],
        out_shape=[jax.ShapeDtypeStruct((bsz, dil, cls_len, GROUP_WIDTH), F32),
                   jax.ShapeDtypeStruct((bsz, dil, cls_len, LANES), F32)],
        scratch_shapes=[pltpu.VMEM((n_cls, chunk + Q_BLOCK, GROUP_WIDTH), BF16),
                        pltpu.VMEM((n_cls, chunk + Q_BLOCK, GROUP_WIDTH), BF16)],
        compiler_params=_cparams(("arbitrary", "arbitrary", "arbitrary")),
        name=f"attn_prompt_g{g}",
    )(*args)


def _split_bf16(x, n):
    parts = []
    r = x
    for _ in range(n):
        p = r.astype(BF16)
        parts.append(p)
        r = r - p.astype(F32)
    return parts


def _store_token_rows(ref, value, n_tokens, first_chunk=0, rows_per_token=None):
    n_chunks = value.shape[1] // LANES
    rows_per_token = rows_per_token or n_chunks
    for c in range(n_chunks):
        ref[pl.ds(first_chunk + c, n_tokens, stride=rows_per_token), :] = value[:, c * LANES:(c + 1) * LANES]


def _load_token_rows(ref, n_tokens, n_chunks, first_chunk=0, rows_per_token=None, lead=None):
    rows_per_token = rows_per_token or n_chunks

    def chunk(c):
        rows = pl.ds(first_chunk + c, n_tokens, stride=rows_per_token)
        return ref[rows, :] if lead is None else ref[lead, rows, :]

    return jnp.concatenate([chunk(c) for c in range(n_chunks)], axis=1)


H_ROWS = D_MODEL // LANES
Y_ROWS = 2 * D_MODEL // LANES

_PAIRS = ((0, 1), (0, 2), (0, 3), (1, 2), (1, 3), (2, 3))
N_CLASSES = N_EXPERT_GROUPS * len(_PAIRS)
META_CLASS, META_RANK, META_WA, META_WB = 0, 1, 2, 3


def _post_kernel(*refs, tm, dils, full_precision, routed):
    (o0_ref, o1_ref, o2_ref, l0_ref, l1_ref, l2_ref, pc_ref, sga_ref, x_ref, mod_ref,
     wao_ref, wo_ref, g2_ref, wr_ref, wrh_ref, wrl_ref, br_ref, exp_ref) = refs[:18]
    if routed:
        tri_ref, x1_ref, h2_ref, meta_ref, route_ref, cnt_ref, o_scr, l_scr, run_scr = refs[18:]
    else:
        x1_ref, h2_ref, comb_ref, o_scr, l_scr = refs[18:]

    def mm(a, w_ref):
        if full_precision:
            return jnp.dot(a, w_ref[...], precision=HIGHEST, preferred_element_type=F32)
        return jnp.dot(a.astype(BF16), w_ref[...], preferred_element_type=F32)

    def natural_order(ref, scr, dil, n_chunks):
        if dil == 1:
            return [ref[0, :, c * LANES:(c + 1) * LANES] for c in range(n_chunks)]
        out = []
        for c in range(n_chunks):
            for r in range(dil):
                scr[c, pl.ds(r, tm // dil, stride=dil), :] = ref[r, :, c * LANES:(c + 1) * LANES]
            out.append(scr[c])
        return out

    lses = [natural_order(ref, l_scr.at[g], dils[g], 1)[0] for g, ref in enumerate((l0_ref, l1_ref, l2_ref))]
    mx = jnp.maximum(lses[0], jnp.maximum(lses[1], lses[2]))
    es = [jnp.exp(v - mx) for v in lses]
    inv = 1.0 / (es[0] + es[1] + es[2])
    expand = exp_ref[...]
    attn_o = None
    for g, o_ref in enumerate((o0_ref, o1_ref, o2_ref)):
        w = es[g] * inv
        we = None
        for part in _split_bf16(w, 3 if full_precision else 2):
            t = jnp.dot(part, expand, preferred_element_type=F32)
            we = t if we is None else we + t
        o_nat = jnp.concatenate(natural_order(o_ref, o_scr.at[g], dils[g], CHUNKS_PER_GROUP), axis=1)
        term = we * o_nat
        attn_o = term if attn_o is None else attn_o + term

    y_attn = mm(attn_o, wao_ref)
    mixed = mm(pc_ref[...] + sga_ref[...] * y_attn, wo_ref)
    x1 = x_ref[...] + mod_ref[2] * mixed
    x1_ref[...] = x1
    var = jnp.mean(x1 * x1, axis=-1, keepdims=True)
    h2 = (x1 * lax.rsqrt(var + RMS_EPS)) * (g2_ref[...] * (1.0 + mod_ref[4])) + mod_ref[3]
    if routed:
        _store_token_rows(h2_ref, h2, tm)
    else:
        h2_ref[...] = h2.astype(h2_ref.dtype)

    if full_precision:
        lg = jnp.dot(h2, wr_ref[...], precision=HIGHEST, preferred_element_type=F32) + br_ref[...]
    else:
        h_hi, h_lo = _split_bf16(h2, 2)
        lg = (jnp.dot(h_hi, wrh_ref[...], preferred_element_type=F32)
              + jnp.dot(h_lo, wrh_ref[...], preferred_element_type=F32)
              + jnp.dot(h_hi, wrl_ref[...], preferred_element_type=F32)) + br_ref[...]
    lane_i = lax.broadcasted_iota(jnp.int32, (1, LANES), 1)
    lane = lane_i.astype(F32)
    lane_group = ((lane_i - N_EXPERT_GROUPS) >> 2).astype(F32)
    big = jnp.float32(1e9)
    gl = jnp.where(lane_i < N_EXPERT_GROUPS, lg, NEG_BIG)
    gmax = jnp.max(gl, axis=1, keepdims=True)
    gidx = jnp.min(jnp.where(gl == gmax, lane, big), axis=1, keepdims=True)
    g_w = 1.0 / jnp.sum(jnp.exp(gl - gmax), axis=1, keepdims=True)
    el = jnp.where(lane_group == gidx, lg, NEG_BIG)
    v1 = jnp.max(el, axis=1, keepdims=True)
    i1 = jnp.min(jnp.where(el == v1, lane, big), axis=1, keepdims=True)
    el2 = jnp.where(lane == i1, NEG_BIG, el)
    v2 = jnp.max(el2, axis=1, keepdims=True)
    i2 = jnp.min(jnp.where(el2 == v2, lane, big), axis=1, keepdims=True)
    t = jnp.exp(v2 - v1)
    den = 1.0 / (1.0 + t)
    w_top1, w_top2 = g_w * den, g_w * (t * den)
    if not routed:
        comb_ref[...] = jnp.where(lane == i1, w_top1, jnp.where(lane == i2, w_top2, 0.0))
        return

    base = jnp.float32(N_EXPERT_GROUPS) + jnp.float32(EXPERTS_PER_GROUP) * gidx
    e1, e2 = i1 - base, i2 - base
    first_is_lower = e1 < e2
    ea = jnp.where(first_is_lower, e1, e2)
    eb = jnp.where(first_is_lower, e2, e1)
    pair = ea * (7.0 - ea) * 0.5 + (eb - ea - 1.0)
    cls = gidx * jnp.float32(len(_PAIRS)) + pair
    onehot = lane == cls
    earlier = jnp.dot(tri_ref[...], jnp.where(onehot, 1.0, 0.0).astype(BF16), preferred_element_type=F32)

    @pl.when(jnp.logical_and(pl.program_id(0) == 0, pl.program_id(1) == 0))
    def _():
        run_scr[...] = jnp.zeros_like(run_scr)

    running = run_scr[...]
    rank = jnp.sum(jnp.where(onehot, earlier + running, 0.0), axis=1, keepdims=True)
    running = running + jnp.sum(jnp.where(onehot, 1.0, 0.0), axis=0, keepdims=True)
    run_scr[...] = running
    cnt_ref[...] = running
    w_a = jnp.where(first_is_lower, w_top1, w_top2)
    w_b = jnp.where(first_is_lower, w_top2, w_top1)
    meta = jnp.where(lane_i == META_CLASS, cls,
                     jnp.where(lane_i == META_RANK, rank,
                               jnp.where(lane_i == META_WA, w_a, jnp.where(lane_i == META_WB, w_b, 0.0))))
    meta_ref[...] = meta
    route_ref[...] = meta.T[0:SUBLANES, :].astype(jnp.int32)


def _post(o_list, lse_list, pc, sga, x, mod4, wao, wo, g2, wr, br, expand, *, tm, full_precision, routed):
    nb, rows, _ = x.shape
    tiles = rows // tm
    dils = tuple(o.shape[1] for o in o_list)
    tile3 = lambda b, i: (b, i, 0)
    const2 = lambda b, i: (0, 0)
    mod_rows = mod4.shape[2]
    if mod_rows == 1:
        mod_spec = pl.BlockSpec((None, 6, 1, D_MODEL), lambda b, i: (b, 0, 0, 0))
    else:
        mod_spec = pl.BlockSpec((None, 6, tm, D_MODEL), lambda b, i: (b, 0, i, 0))
    cls4 = lambda b, i: (b, 0, i, 0)
    in_specs = (
        [pl.BlockSpec((None, d, tm // d, GROUP_WIDTH), cls4) for d in dils]
        + [pl.BlockSpec((None, d, tm // d, LANES), cls4) for d in dils]
        + [pl.BlockSpec((None, tm, D_MODEL), tile3)] * 3 + [mod_spec]
        + [pl.BlockSpec(wao.shape, const2), pl.BlockSpec(wo.shape, const2), pl.BlockSpec((1, D_MODEL), const2)]
        + [pl.BlockSpec((D_MODEL, LANES), const2)] * 3
        + [pl.BlockSpec((1, LANES), const2), pl.BlockSpec((LANES, GROUP_WIDTH), const2)]
    )
    wr_hi = wr.astype(BF16)
    wr_lo = (wr - wr_hi.astype(F32)).astype(BF16)
    args = [*o_list, *lse_list, pc, sga, x, mod4, wao, wo, g2, wr, wr_hi, wr_lo, br, expand]
    scratch = [pltpu.VMEM((N_DIL, CHUNKS_PER_GROUP, tm, LANES), F32), pltpu.VMEM((N_DIL, 1, tm, LANES), F32)]
    if routed:
        tri = jnp.asarray(np.tril(np.ones((tm, tm), np.float32), -1), dtype=BF16)
        args.append(tri)
        in_specs = in_specs + [pl.BlockSpec((tm, tm), const2)]
        out_specs = [pl.BlockSpec((None, tm, D_MODEL), tile3), pl.BlockSpec((None, tm * H_ROWS, LANES), tile3),
                     pl.BlockSpec((None, tm, LANES), tile3),
                     pl.BlockSpec((None, None, SUBLANES, tm), lambda b, i: (b, i, 0, 0)),
                     pl.BlockSpec((1, LANES), const2)]
        out_shape = [jax.ShapeDtypeStruct((nb, rows, D_MODEL), F32),
                     jax.ShapeDtypeStruct((nb, rows * H_ROWS, LANES), F32),
                     jax.ShapeDtypeStruct((nb, rows, LANES), F32),
                     jax.ShapeDtypeStruct((nb, tiles, SUBLANES, tm), jnp.int32),
                     jax.ShapeDtypeStruct((1, LANES), F32)]
        scratch.append(pltpu.VMEM((1, LANES), F32))
    else:
        out_specs = [pl.BlockSpec((None, tm, D_MODEL), tile3), pl.BlockSpec((None, tm, D_MODEL), tile3),
                     pl.BlockSpec((None, tm, LANES), tile3)]
        out_shape = [jax.ShapeDtypeStruct((nb, rows, D_MODEL), F32), jax.ShapeDtypeStruct((nb, rows, D_MODEL), F32),
                     jax.ShapeDtypeStruct((nb, rows, LANES), F32)]
    return pl.pallas_call(
        functools.partial(_post_kernel, tm=tm, dils=dils, full_precision=full_precision, routed=routed),
        grid=(nb, tiles),
        in_specs=in_specs,
        out_specs=out_specs,
        out_shape=out_shape,
        scratch_shapes=scratch,
        compiler_params=_cparams(("arbitrary", "arbitrary")),
        name="post_sample" if full_precision else "post_prompt",
    )(*args)


DMA_LOOP_UNROLL = 8
DMA_PRIORITIES = 2


ROUTE_CLASS_ROW, ROUTE_RANK_ROW = META_CLASS, META_RANK


def _sorted_slot(starts_ref, route_ref, r):
    return starts_ref[route_ref[ROUTE_CLASS_ROW, r]] + route_ref[ROUTE_RANK_ROW, r]


def _token_rows(token, rows_per_token):
    return pl.ds(pl.multiple_of(token * rows_per_token, rows_per_token), rows_per_token)


def _dispatch_kernel(starts_ref, route_ref, src_ref, dst_hbm, buf, sems, *, rows_per_step, n_steps):
    i = pl.program_id(0)
    slot = i % 2

    def wait_slot(s):
        def body(r, carry):
            pltpu.make_async_copy(buf.at[s, _token_rows(0, H_ROWS)], dst_hbm.at[_token_rows(0, H_ROWS)],
                                  sems.at[s * rows_per_step + r]).wait()
            return carry
        lax.fori_loop(0, rows_per_step, body, 0, unroll=DMA_LOOP_UNROLL)

    @pl.when(i >= 1)
    def _():
        wait_slot(1 - slot)

    buf[slot] = src_ref[...]

    def issue(pair, carry):
        for prio in range(DMA_PRIORITIES):
            r = pair * DMA_PRIORITIES + prio
            pltpu.make_async_copy(buf.at[slot, _token_rows(r, H_ROWS)],
                                  dst_hbm.at[_token_rows(_sorted_slot(starts_ref, route_ref, r), H_ROWS)],
                                  sems.at[slot * rows_per_step + r]).start(priority=prio)
        return carry

    lax.fori_loop(0, rows_per_step // DMA_PRIORITIES, issue, 0, unroll=DMA_LOOP_UNROLL // DMA_PRIORITIES)

    @pl.when(i == n_steps - 1)
    def _():
        wait_slot(slot)


def _route_spec(rows_per_step, route_tile, shift=0, n_steps=None):
    per_tile = route_tile // rows_per_step

    def imap(i, starts):
        s = i + shift if n_steps is None else jnp.minimum(i + shift, n_steps - 1)
        return (s // per_tile, 0, s % per_tile)

    return pl.BlockSpec((None, SUBLANES, rows_per_step), imap, memory_space=pltpu.SMEM)


def _dispatch_rows(src, route, starts, *, rows_per_step):
    n = src.shape[0] // H_ROWS
    n_steps = n // rows_per_step
    grid_spec = pltpu.PrefetchScalarGridSpec(
        num_scalar_prefetch=1,
        grid=(n_steps,),
        in_specs=[_route_spec(rows_per_step, route.shape[2]),
                  pl.BlockSpec((rows_per_step * H_ROWS, LANES), lambda i, starts: (i, 0))],
        out_specs=pl.BlockSpec(memory_space=pl.ANY),
        scratch_shapes=[pltpu.VMEM((2, rows_per_step * H_ROWS, LANES), src.dtype),
                        pltpu.SemaphoreType.DMA((2 * rows_per_step,))],
    )
    return pl.pallas_call(
        functools.partial(_dispatch_kernel, rows_per_step=rows_per_step, n_steps=n_steps),
        grid_spec=grid_spec,
        out_shape=jax.ShapeDtypeStruct(src.shape, src.dtype),
        compiler_params=_cparams(("arbitrary",)),
        name="moe_dispatch",
    )(starts, route, src)


def _experts_kernel(tile_ref, ea_ref, eb_ref, lo_ref, hi_ref, first_ref, switch_ref, nv_ref,
                    x_ref, w1a_ref, w1b_ref, w3a_ref, w3b_ref, w2a_ref, w2b_ref, y_ref, w13_scr, w2_scr, *, tm):
    i = pl.program_id(0)

    @pl.when(i < nv_ref[0])
    def _():
        @pl.when(switch_ref[i] == 1)
        def _():
            for k, w_ref in enumerate((w1a_ref, w3a_ref, w1b_ref, w3b_ref)):
                w13_scr[k] = w_ref[...].astype(BF16)
            for k, w_ref in enumerate((w2a_ref, w2b_ref)):
                w2_scr[k] = w_ref[...].astype(BF16)

        x = _load_token_rows(x_ref, tm, H_ROWS).astype(BF16)

        def expert(k):
            a = jnp.dot(x, w13_scr[2 * k], preferred_element_type=F32)
            b = jnp.dot(x, w13_scr[2 * k + 1], preferred_element_type=F32)
            hid = (a * _sigmoid(a)) * b
            return jnp.dot(hid.astype(BF16), w2_scr[k], preferred_element_type=F32)

        y_both = jnp.concatenate([expert(0), expert(1)], axis=1)

        @pl.when(first_ref[i] == 1)
        def _():
            _store_token_rows(y_ref, y_both, tm)

        @pl.when(first_ref[i] == 0)
        def _():
            row = lax.broadcasted_iota(jnp.int32, (tm, 1), 0)
            old = _load_token_rows(y_ref, tm, Y_ROWS)
            _store_token_rows(y_ref, jnp.where(row >= lo_ref[i], jnp.where(row < hi_ref[i], y_both, old), old), tm)


def _experts_routed(hs, tables, w1, w3, w2, *, tm):
    n = hs.shape[0] // H_ROWS
    n_items = tables[0].shape[0]
    x_map = lambda i, tile, ea, eb, lo, hi, first, switch, nv: (tile[i], 0)
    wa_map = lambda i, tile, ea, eb, lo, hi, first, switch, nv: (ea[i], 0, 0)
    wb_map = lambda i, tile, ea, eb, lo, hi, first, switch, nv: (eb[i], 0, 0)
    w13 = lambda m: pl.BlockSpec((None, D_MODEL, D_EXPERT), m)
    w2s = lambda m: pl.BlockSpec((None, D_EXPERT, D_MODEL), m)
    grid_spec = pltpu.PrefetchScalarGridSpec(
        num_scalar_prefetch=len(tables),
        grid=(n_items,),
        in_specs=[pl.BlockSpec((tm * H_ROWS, LANES), x_map),
                  w13(wa_map), w13(wb_map), w13(wa_map), w13(wb_map), w2s(wa_map), w2s(wb_map)],
        out_specs=pl.BlockSpec((tm * Y_ROWS, LANES), x_map),
        scratch_shapes=[pltpu.VMEM((4, D_MODEL, D_EXPERT), BF16), pltpu.VMEM((2, D_EXPERT, D_MODEL), BF16)],
    )
    return pl.pallas_call(
        functools.partial(_experts_kernel, tm=tm),
        grid_spec=grid_spec,
        out_shape=jax.ShapeDtypeStruct((n * Y_ROWS, LANES), F32),
        compiler_params=_cparams(("arbitrary",)),
        name="moe_experts",
    )(*tables, hs, w1, w1, w3, w3, w2, w2)


def _combine_kernel(starts_ref, route_ref, route_next_ref, yp_hbm, x1_ref, meta_ref, mod_ref, gf_ref, y_ref, ybuf, sems,
                    *, rows_per_step, n_steps):
    i = pl.program_id(0)
    slot = i % 2

    def issue(r_ref, s):
        def body(pair, carry):
            for prio in range(DMA_PRIORITIES):
                r = pair * DMA_PRIORITIES + prio
                pltpu.make_async_copy(yp_hbm.at[_token_rows(_sorted_slot(starts_ref, r_ref, r), Y_ROWS)],
                                      ybuf.at[s, :, pl.ds(r, 1), :],
                                      sems.at[s * rows_per_step + r]).start(priority=prio)
            return carry
        lax.fori_loop(0, rows_per_step // DMA_PRIORITIES, body, 0, unroll=DMA_LOOP_UNROLL // DMA_PRIORITIES)

    @pl.when(i == 0)
    def _():
        issue(route_ref, 0)

    @pl.when(i + 1 < n_steps)
    def _():
        issue(route_next_ref, 1 - slot)

    def wait_body(r, carry):
        pltpu.make_async_copy(yp_hbm.at[_token_rows(0, Y_ROWS)], ybuf.at[slot, :, pl.ds(r, 1), :],
                              sems.at[slot * rows_per_step + r]).wait()
        return carry

    lax.fori_loop(0, rows_per_step, wait_body, 0, unroll=DMA_LOOP_UNROLL)
    ya = jnp.concatenate([ybuf[slot, c] for c in range(H_ROWS)], axis=1)
    yb = jnp.concatenate([ybuf[slot, H_ROWS + c] for c in range(H_ROWS)], axis=1)
    meta = meta_ref[...]
    lane = lax.broadcasted_iota(jnp.int32, (1, LANES), 1)
    w_a = jnp.sum(jnp.where(lane == META_WA, meta, 0.0), axis=1, keepdims=True)
    w_b = jnp.sum(jnp.where(lane == META_WB, meta, 0.0), axis=1, keepdims=True)
    x2 = x1_ref[...] + mod_ref[5] * (w_a * ya + w_b * yb)
    var = jnp.mean(x2 * x2, axis=-1, keepdims=True)
    y_ref[...] = (x2 * lax.rsqrt(var + RMS_EPS)) * gf_ref[...]


def _combine_final(yp, route, starts, x1, meta, mod4, gf, *, rows_per_step, rows_per_mod):
    n = x1.shape[0]
    n_steps = n // rows_per_step
    steps_per_mod = rows_per_mod // rows_per_step
    row = lambda i, starts: (i, 0)
    grid_spec = pltpu.PrefetchScalarGridSpec(
        num_scalar_prefetch=1,
        grid=(n_steps,),
        in_specs=[_route_spec(rows_per_step, route.shape[2]),
                  _route_spec(rows_per_step, route.shape[2], shift=1, n_steps=n_steps),
                  pl.BlockSpec(memory_space=pl.ANY),
                  pl.BlockSpec((rows_per_step, D_MODEL), row),
                  pl.BlockSpec((rows_per_step, LANES), row),
                  pl.BlockSpec((None, 6, 1, D_MODEL), lambda i, starts: (i // steps_per_mod, 0, 0, 0)),
                  pl.BlockSpec((1, D_MODEL), lambda i, starts: (0, 0))],
        out_specs=pl.BlockSpec((rows_per_step, D_MODEL), row),
        scratch_shapes=[pltpu.VMEM((2, Y_ROWS, rows_per_step, LANES), F32),
                        pltpu.SemaphoreType.DMA((2 * rows_per_step,))],
    )
    yp_rows = yp.reshape(yp.shape[0], 1, LANES)
    return pl.pallas_call(
        functools.partial(_combine_kernel, rows_per_step=rows_per_step, n_steps=n_steps),
        grid_spec=grid_spec,
        out_shape=jax.ShapeDtypeStruct((n, D_MODEL), F32),
        compiler_params=_cparams(("arbitrary",)),
        name="moe_combine",
    )(starts, route, route, yp_rows, x1, meta, mod4, gf)


def _routing_tables(counts, *, n, tm):
    counts = counts.astype(jnp.int32)
    ends = jnp.cumsum(counts)
    starts = ends - counts
    n_items = n // tm + N_CLASSES
    first_tile = starts // tm
    last_tile = jnp.maximum(ends - 1, starts) // tm
    visits = jnp.where(counts > 0, last_tile - first_tile + 1, 0)
    item_end = jnp.cumsum(visits)
    item_start = item_end - visits
    n_valid = item_end[-1]
    item = jnp.arange(n_items, dtype=jnp.int32)
    idx = jnp.minimum(item, n_valid - 1)
    c = jnp.sum((idx[:, None] >= item_end[None, :]).astype(jnp.int32), axis=1)
    class_ids = jnp.arange(N_CLASSES, dtype=jnp.int32)

    def pick(table):
        return jnp.sum(jnp.where(c[:, None] == class_ids[None, :], table[None, :], 0), axis=1)

    tile = pick(first_tile) + (idx - pick(item_start))
    live = item < n_valid
    lo = jnp.where(live, jnp.clip(pick(starts) - tile * tm, 0, tm), 0)
    hi = jnp.where(live, jnp.clip(pick(ends) - tile * tm, 0, tm), 0)
    prev_tile = jnp.concatenate([jnp.full((1,), -1, jnp.int32), tile[:-1]])
    first = (tile != prev_tile).astype(jnp.int32)
    prev_c = jnp.concatenate([jnp.full((1,), -1, jnp.int32), c[:-1]])
    switch = (c != prev_c).astype(jnp.int32)
    group_of_class = np.arange(N_CLASSES) // len(_PAIRS)
    pair_of_class = np.arange(N_CLASSES) % len(_PAIRS)
    pairs = np.asarray(_PAIRS)
    ea = pick(jnp.asarray(group_of_class * EXPERTS_PER_GROUP + pairs[pair_of_class, 0], dtype=jnp.int32))
    eb = pick(jnp.asarray(group_of_class * EXPERTS_PER_GROUP + pairs[pair_of_class, 1], dtype=jnp.int32))
    as_i32 = lambda a: a.astype(jnp.int32)
    return as_i32(starts), tuple(as_i32(a) for a in (tile, ea, eb, lo, hi, first, switch, n_valid.reshape(1)))


def _moe_kernel(h_ref, comb_ref, x1_ref, mod_ref, gf_ref, w1_ref, w3_ref, w2_ref, y_ref, acc_ref):
    e = pl.program_id(2)

    @pl.when(e == 0)
    def _():
        acc_ref[...] = jnp.zeros_like(acc_ref)

    h = h_ref[...].astype(BF16)
    a = jnp.dot(h, w1_ref[...].astype(BF16), preferred_element_type=F32)
    b = jnp.dot(h, w3_ref[...].astype(BF16), preferred_element_type=F32)
    lane = lax.broadcasted_iota(jnp.int32, (1, LANES), 1)
    cw = jnp.sum(jnp.where(lane == e + N_EXPERT_GROUPS, comb_ref[...], 0.0), axis=1, keepdims=True)
    hid = (a * _sigmoid(a)) * b * cw
    acc_ref[...] += jnp.dot(hid.astype(BF16), w2_ref[...].astype(BF16), preferred_element_type=F32)

    @pl.when(e == N_EXPERTS - 1)
    def _():
        x2 = x1_ref[...] + mod_ref[5] * acc_ref[...]
        var = jnp.mean(x2 * x2, axis=-1, keepdims=True)
        y_ref[...] = (x2 * lax.rsqrt(var + RMS_EPS)) * gf_ref[...]


def _moe_dense(h2, comb, x1, mod4, gf, w1_bf, w3_bf, w2_bf, *, tm):
    nb, rows, _ = x1.shape
    tiles = rows // tm
    tile3 = lambda b, i, e: (b, i, 0)
    if mod4.shape[2] == 1:
        mod_spec = pl.BlockSpec((None, 6, 1, D_MODEL), lambda b, i, e: (b, 0, 0, 0))
    else:
        mod_spec = pl.BlockSpec((None, 6, tm, D_MODEL), lambda b, i, e: (b, 0, i, 0))
    return pl.pallas_call(
        _moe_kernel,
        grid=(nb, tiles, N_EXPERTS),
        in_specs=[
            pl.BlockSpec((None, tm, D_MODEL), tile3),
            pl.BlockSpec((None, tm, LANES), tile3),
            pl.BlockSpec((None, tm, D_MODEL), tile3),
            mod_spec,
            pl.BlockSpec((1, D_MODEL), lambda b, i, e: (0, 0)),
            pl.BlockSpec((None, D_MODEL, D_EXPERT), lambda b, i, e: (e, 0, 0)),
            pl.BlockSpec((None, D_MODEL, D_EXPERT), lambda b, i, e: (e, 0, 0)),
            pl.BlockSpec((None, D_EXPERT, D_MODEL), lambda b, i, e: (e, 0, 0)),
        ],
        out_specs=pl.BlockSpec((None, tm, D_MODEL), tile3),
        out_shape=jax.ShapeDtypeStruct((nb, rows, D_MODEL), F32),
        scratch_shapes=[pltpu.VMEM((tm, D_MODEL), F32)],
        compiler_params=_cparams(("arbitrary", "arbitrary", "arbitrary")),
        name=f"moe_dense_{nb * rows}",
    )(h2, comb, x1, mod4, gf, w1_bf, w3_bf, w2_bf)


def _s_inproj_kernel(x_ref, mod_ref, g1_ref, w_ref, z_ref):
    x = x_ref[...]
    var = jnp.mean(x * x, axis=-1, keepdims=True)
    h = (x * lax.rsqrt(var + RMS_EPS)) * (g1_ref[...] * (1.0 + mod_ref[1])) + mod_ref[0]
    z_ref[...] = jnp.dot(h, w_ref[...], precision=HIGHEST, preferred_element_type=F32)


def _s_inproj(x, mod_tok, g1, w_in, *, col_block):
    n = x.shape[0]
    return pl.pallas_call(
        _s_inproj_kernel,
        grid=(IN_COLS // col_block,),
        in_specs=[pl.BlockSpec((n, D_MODEL), lambda j: (0, 0)),
                  pl.BlockSpec((6, n, D_MODEL), lambda j: (0, 0, 0)),
                  pl.BlockSpec((1, D_MODEL), lambda j: (0, 0)),
                  pl.BlockSpec((D_MODEL, col_block), lambda j: (0, j))],
        out_specs=pl.BlockSpec((n, col_block), lambda j: (0, j)),
        out_shape=jax.ShapeDtypeStruct((n, IN_COLS), F32),
        compiler_params=_cparams(("arbitrary",)),
        name="inproj_sample",
    )(x, mod_tok, g1, w_in)


def _s_mid_kernel(z_ref, p0_ref, p1_ref, cw_ref, wco_ref, rq_ref, rk_ref,
                  cu_ref, pc_ref, sga_ref, q_ref, k_ref, v_ref, *, t_len):
    n = z_ref.shape[0]
    cu = z_ref[:, OFF_GC:OFF_GC + D_MODEL] * z_ref[:, OFF_U:OFF_U + D_MODEL]
    cu_ref[...] = cu
    t = lax.broadcasted_iota(jnp.int32, (n, D_MODEL), 0) & (t_len - 1)
    prev1 = jnp.where(t >= 1, pltpu.roll(cu, 1, 0), p1_ref[...])
    prev2 = jnp.where(t >= 2, pltpu.roll(cu, 2, 0), jnp.where(t == 0, p0_ref[...], p1_ref[...]))
    cw = cw_ref[...]
    conv = cw[0:1] * prev2 + cw[1:2] * prev1 + cw[2:3] * cu
    yc = jnp.dot(z_ref[:, OFF_GB:OFF_GB + D_MODEL] * conv, wco_ref[...], precision=HIGHEST,
                 preferred_element_type=F32)
    pc_ref[...] = _sigmoid(z_ref[:, OFF_GCONV:OFF_GCONV + D_MODEL]) * yc
    sga_ref[...] = _sigmoid(z_ref[:, OFF_GATTN:OFF_GATTN + D_MODEL])
    aq, bmq, bpq = rq_ref[0], rq_ref[1], rq_ref[2]
    ak, bmk, bpk = rk_ref[0], rk_ref[1], rk_ref[2]
    for c in range(ATTN_WIDTH // LANES):
        sl = slice(c * LANES, (c + 1) * LANES)
        q_ref[:, sl] = _rope_chunk(z_ref[:, OFF_Q + c * LANES:OFF_Q + (c + 1) * LANES], aq, bmq, bpq)
        k_ref[:, sl] = _rope_chunk(z_ref[:, OFF_K + c * LANES:OFF_K + (c + 1) * LANES], ak, bmk, bpk)
    v_ref[...] = z_ref[:, OFF_V:OFF_V + ATTN_WIDTH]


def _s_mid(z, p0e, p1e, conv_w, w_conv_out, rope_q, rope_k, *, t_len):
    n = z.shape[0]
    assert t_len & (t_len - 1) == 0
    full = lambda shape: pl.BlockSpec(shape, lambda i: (0,) * len(shape))
    out_shape = [jax.ShapeDtypeStruct((n, D_MODEL), F32)] * 3 + [jax.ShapeDtypeStruct((n, ATTN_WIDTH), F32)] * 3
    return pl.pallas_call(
        functools.partial(_s_mid_kernel, t_len=t_len),
        grid=(1,),
        in_specs=[full(z.shape), full(p0e.shape), full(p1e.shape), full(conv_w.shape), full(w_conv_out.shape),
                  full(rope_q.shape), full(rope_k.shape)],
        out_specs=[full((n, D_MODEL))] * 3 + [full((n, ATTN_WIDTH))] * 3,
        out_shape=out_shape,
        compiler_params=_cparams(("arbitrary",)),
        name="mid_sample",
    )(z, p0e, p1e, conv_w, w_conv_out, rope_q, rope_k)


def _head_sum(x):
    return jnp.sum(x.reshape(HEADS_PER_GROUP, HEAD_DIM, x.shape[-1]), axis=1)


def _head_expand(x):
    n = x.shape[-1]
    return jnp.broadcast_to(x[:, None, :], (HEADS_PER_GROUP, HEAD_DIM, n)).reshape(GROUP_WIDTH, n)


def _s_attn_disjoint(q_ref, kn, vn, ck_ref, cv_ref, o_ref, lse_ref, s_scr, lane, new_idx, *, n_tiles, dil, t_len):
    cls = lane & (dil - 1)
    q_all = q_ref[...]
    qsel = jnp.zeros((GROUP_WIDTH, LANES), F32)
    for t in range(t_len):
        qsel = jnp.where(cls == t, jnp.broadcast_to(q_all[:, t:t + 1], (GROUP_WIDTH, LANES)), qsel)
    smax = None
    for j in range(n_tiles):
        s = jnp.where(cls < t_len, _head_sum(ck_ref[:, j * LANES:(j + 1) * LANES] * qsel), NEG_BIG)
        s_scr[:, j * LANES:(j + 1) * LANES] = s
        smax = s if smax is None else jnp.maximum(smax, s)
    s_new = _head_sum(kn * q_all)
    m_cols = []
    m_lane = jnp.zeros((HEADS_PER_GROUP, LANES), F32)
    m_new = jnp.zeros((HEADS_PER_GROUP, t_len), F32)
    for t in range(t_len):
        mt = jnp.maximum(jnp.max(jnp.where(cls == t, smax, NEG_BIG), axis=1, keepdims=True), s_new[:, t:t + 1])
        m_cols.append(mt)
        m_lane = jnp.where(cls == t, mt, m_lane)
        m_new = jnp.where(new_idx == t, mt, m_new)
    esum = jnp.zeros((HEADS_PER_GROUP, LANES), F32)
    for j in range(n_tiles):
        e = jnp.exp(s_scr[:, j * LANES:(j + 1) * LANES] - m_lane)
        s_scr[:, j * LANES:(j + 1) * LANES] = e
        esum = esum + e
    e_new = jnp.exp(s_new - m_new)
    l_cols = []
    inv_lane = jnp.zeros((HEADS_PER_GROUP, LANES), F32)
    inv_new = jnp.zeros((HEADS_PER_GROUP, t_len), F32)
    for t in range(t_len):
        lt = jnp.sum(jnp.where(cls == t, esum, 0.0), axis=1, keepdims=True) + e_new[:, t:t + 1]
        l_cols.append(lt)
        inv_lane = jnp.where(cls == t, 1.0 / lt, inv_lane)
        inv_new = jnp.where(new_idx == t, 1.0 / lt, inv_new)
    acc = None
    for j in range(n_tiles):
        term = cv_ref[:, j * LANES:(j + 1) * LANES] * _head_expand(s_scr[:, j * LANES:(j + 1) * LANES] * inv_lane)
        acc = term if acc is None else acc + term
    o_new = vn * _head_expand(e_new * inv_new)
    for t in range(t_len):
        o_ref[:, t:t + 1] = jnp.sum(jnp.where(cls == t, acc, 0.0), axis=1, keepdims=True) + o_new[:, t:t + 1]
        lse_ref[:, t:t + 1] = m_cols[t] + jnp.log(l_cols[t])


def _s_attn_kernel(q_ref, kn_ref, vn_ref, ck_ref, cv_ref, o_ref, lse_ref, ko_ref, vo_ref, s_scr,
                   *, win, dil, t_len):
    n_tiles = win // LANES
    lane = lax.broadcasted_iota(jnp.int32, (1, LANES), 1)
    new_idx = lax.broadcasted_iota(jnp.int32, (1, t_len), 1)
    kn = kn_ref[...]
    vn = vn_ref[...]
    if dil >= t_len:
        _s_attn_disjoint(q_ref, kn, vn, ck_ref, cv_ref, o_ref, lse_ref, s_scr, lane, new_idx,
                         n_tiles=n_tiles, dil=dil, t_len=t_len)
    for t in range(t_len if dil < t_len else 0):
        qb = jnp.broadcast_to(q_ref[:, t:t + 1], (GROUP_WIDTH, LANES))
        m = None
        for j in range(n_tiles):
            pos = lane + j * LANES
            s = _head_sum(ck_ref[:, j * LANES:(j + 1) * LANES] * qb)
            s = jnp.where(pos >= t, jnp.where(((pos - t) & (dil - 1)) == 0, s, NEG_BIG), NEG_BIG)
            s_scr[:, j * LANES:(j + 1) * LANES] = s
            mj = jnp.max(s, axis=1, keepdims=True)
            m = mj if m is None else jnp.maximum(m, mj)
        s_new = _head_sum(kn * qb[:, 0:t_len])
        s_new = jnp.where(new_idx <= t, jnp.where(((t - new_idx) & (dil - 1)) == 0, s_new, NEG_BIG), NEG_BIG)
        m = jnp.maximum(m, jnp.max(s_new, axis=1, keepdims=True))
        e_new = jnp.exp(s_new - m)
        l = jnp.sum(e_new, axis=1, keepdims=True)
        acc = None
        for j in range(n_tiles):
            e = jnp.exp(s_scr[:, j * LANES:(j + 1) * LANES] - m)
            l = l + jnp.sum(e, axis=1, keepdims=True)
            term = cv_ref[:, j * LANES:(j + 1) * LANES] * _head_expand(e)
            acc = term if acc is None else acc + term
        o = jnp.sum(acc, axis=1, keepdims=True) + jnp.sum(vn * _head_expand(e_new), axis=1, keepdims=True)
        o_ref[:, t:t + 1] = o * _head_expand(1.0 / l)
        lse_ref[:, t:t + 1] = m + jnp.log(l)

    for c_ref, new, out_ref in ((ck_ref, kn, ko_ref), (cv_ref, vn, vo_ref)):
        out_ref[...] = pltpu.roll(c_ref[...], win - t_len, 1)
        out_ref[:, win - t_len:win] = new


def _s_attn_group(q_t, kn_t, vn_t, cache_k, cache_v, g):
    bsz, _, t_len = q_t.shape
    win_full, dil = DIL_GROUPS[g]
    win = cache_k.shape[2]
    assert win == win_full and win == (KEYS_PER_QUERY - 1) * dil and win % LANES == 0
    assert dil & (dil - 1) == 0 and LANES % dil == 0
    grp = lambda b: (b, g, 0)
    per_b = lambda b: (b, 0, 0)
    return pl.pallas_call(
        functools.partial(_s_attn_kernel, win=win, dil=dil, t_len=t_len),
        grid=(bsz,),
        in_specs=[pl.BlockSpec((None, GROUP_WIDTH, t_len), grp)] * 3
        + [pl.BlockSpec((None, GROUP_WIDTH, win), per_b)] * 2,
        out_specs=[pl.BlockSpec((None, GROUP_WIDTH, t_len), per_b),
                   pl.BlockSpec((None, HEADS_PER_GROUP, t_len), per_b),
                   pl.BlockSpec((None, GROUP_WIDTH, win), per_b),
                   pl.BlockSpec((None, GROUP_WIDTH, win), per_b)],
        out_shape=[jax.ShapeDtypeStruct((bsz, GROUP_WIDTH, t_len), F32),
                   jax.ShapeDtypeStruct((bsz, HEADS_PER_GROUP, t_len), F32),
                   jax.ShapeDtypeStruct((bsz, GROUP_WIDTH, win), F32),
                   jax.ShapeDtypeStruct((bsz, GROUP_WIDTH, win), F32)],
        scratch_shapes=[pltpu.VMEM((HEADS_PER_GROUP, win), F32)],
        compiler_params=_cparams(("arbitrary",)),
        name=f"attn_sample_g{g}",
    )(q_t, kn_t, vn_t, cache_k, cache_v)


TM_INPROJ = 512
TM_POST = 512
TM_EXPERT = 256
ROWS_PER_DMA_STEP = 256
ATTN_ROWS_PER_STEP = 1024
S_COL_BLOCK = 512


def _to_state(a_t):
    b, _, length = a_t.shape
    return jnp.transpose(a_t.reshape(b, HEADS_PER_GROUP, HEAD_DIM, length), (0, 3, 1, 2))[None]


def _from_state(a):
    b, length = a.shape[0], a.shape[1]
    return jnp.transpose(a, (0, 2, 3, 1)).reshape(b, GROUP_WIDTH, length)


def kernel(x_prompt, x_sample, cache_k1, cache_v1, cache_k2, cache_v2, cache_k3, cache_v3, state_conv,
           c_prompt, c_sample, norm1_g, norm2_g, normf_g, w_ada, b_ada, w_in, conv_w, w_conv_out,
           w_attn_out, w_o, w_rg, b_rg, w_re, b_re, w1, w3, w2):
    depth = w_in.shape[0]
    assert depth == 1
    bsz, seq, _ = x_prompt.shape
    dbsz, t_len, _ = x_sample.shape
    n_s = dbsz * t_len
    l = 0

    w_in_bf = w_in[l].astype(BF16)
    wco_bf = w_conv_out[l].astype(BF16)
    wao_bf = w_attn_out[l].astype(BF16)
    wo_bf = w_o[l].astype(BF16)
    w1_e = w1[l].reshape(N_EXPERTS, D_MODEL, D_EXPERT)
    w3_e = w3[l].reshape(N_EXPERTS, D_MODEL, D_EXPERT)
    w2_e = w2[l].reshape(N_EXPERTS, D_EXPERT, D_MODEL)
    g1 = norm1_g[l].reshape(1, D_MODEL)
    g2 = norm2_g[l].reshape(1, D_MODEL)
    gf = normf_g.reshape(1, D_MODEL)
    n_route = N_EXPERT_GROUPS + N_EXPERTS
    wr = jnp.pad(jnp.concatenate([w_rg[l], w_re[l]], axis=1), ((0, 0), (0, LANES - n_route)))
    br = jnp.pad(jnp.concatenate([b_rg[l], b_re[l]]), (0, LANES - n_route)).reshape(1, LANES)
    head_of_lane = np.arange(GROUP_WIDTH) // HEAD_DIM
    expand_bf = jnp.asarray((np.arange(LANES)[:, None] == head_of_lane[None, :]).astype(np.float32), dtype=BF16)

    mod = _adaln(jnp.concatenate([c_prompt, c_sample], axis=0), w_ada[l], b_ada[l])
    mod_p4 = mod[:bsz].reshape(bsz, 6, 1, D_MODEL)
    mod_tok = jnp.repeat(mod[bsz:].reshape(dbsz, 1, 6, D_MODEL), t_len, axis=1)
    mod_tok = jnp.transpose(mod_tok.reshape(n_s, 6, D_MODEL), (1, 0, 2))
    mod_s4 = mod_tok[None]

    pos_p = jnp.arange(seq, dtype=jnp.int32)
    rope_q_p = _rope_tables(pos_p, HEAD_DIM ** -0.5)
    rope_k_p = _rope_tables(pos_p, 1.0)
    pc_p, sga_p, conv_p = _convproj_prompt(x_prompt, mod_p4, g1, w_in_bf[:, :OFF_Q], w_in_bf[:, OFF_GCONV:],
                                           conv_w[l], wco_bf, tm=TM_INPROJ)
    outs = _qkvproj_prompt(x_prompt, mod_p4, g1, w_in_bf[:, OFF_Q:OFF_GCONV], rope_q_p, rope_k_p, tm=TM_INPROJ)
    qkv_p = outs[0:3 * N_DIL]
    states_p = outs[3 * N_DIL:]
    o_p, lse_p = [], []
    for g in range(N_DIL):
        o_g, lse_g = _attn_prompt_group(qkv_p[3 * g], qkv_p[3 * g + 1], qkv_p[3 * g + 2], g, max_rows=ATTN_ROWS_PER_STEP)
        o_p.append(o_g)
        lse_p.append(lse_g)
    x1_p, h2rows_p, meta_p, route_p, cnt_p = _post(o_p, lse_p, pc_p, sga_p, x_prompt, mod_p4, wao_bf, wo_bf, g2, wr, br,
                                                   expand_bf, tm=TM_POST, full_precision=False, routed=True)
    n_p = bsz * seq
    route = route_p.reshape(n_p // TM_POST, SUBLANES, TM_POST)
    starts, tables = _routing_tables(cnt_p[0, :N_CLASSES], n=n_p, tm=TM_EXPERT)
    hs = _dispatch_rows(h2rows_p.reshape(n_p * H_ROWS, LANES), route, starts, rows_per_step=ROWS_PER_DMA_STEP)
    yp = _experts_routed(hs, tables, w1_e, w3_e, w2_e, tm=TM_EXPERT)
    y_p = _combine_final(yp, route, starts, x1_p.reshape(n_p, D_MODEL), meta_p.reshape(n_p, LANES), mod_p4, gf,
                         rows_per_step=ROWS_PER_DMA_STEP, rows_per_mod=seq).reshape(bsz, seq, D_MODEL)

    pos_s = PAST_LEN + jnp.arange(t_len, dtype=jnp.int32)
    rope_q_s = jnp.tile(_rope_tables(pos_s, HEAD_DIM ** -0.5), (1, dbsz, 1))
    rope_k_s = jnp.tile(_rope_tables(pos_s, 1.0), (1, dbsz, 1))
    xs = x_sample.reshape(n_s, D_MODEL)
    z_s = _s_inproj(xs, mod_tok, g1, w_in[l], col_block=S_COL_BLOCK)
    past = state_conv[l]
    p0e = jnp.repeat(past[:, 0], t_len, axis=0)
    p1e = jnp.repeat(past[:, 1], t_len, axis=0)
    cu_s, pc_s, sga_s, q_s, k_s, v_s = _s_mid(z_s, p0e, p1e, conv_w[l], w_conv_out[l], rope_q_s, rope_k_s,
                                              t_len=t_len)
    to_cols = lambda a: jnp.transpose(a.reshape(dbsz, t_len, ATTN_WIDTH), (0, 2, 1))
    q_t, kn_t, vn_t = to_cols(q_s), to_cols(k_s), to_cols(v_s)
    caches = ((cache_k1, cache_v1), (cache_k2, cache_v2), (cache_k3, cache_v3))
    o_s, lse_s, kv_s = [], [], []
    for g, (ck, cv) in enumerate(caches):
        o_g, lse_g, ko, vo = _s_attn_group(q_t, kn_t, vn_t, _from_state(ck[l]), _from_state(cv[l]), g)
        o_s.append(jnp.transpose(o_g, (0, 2, 1)).reshape(1, 1, n_s, GROUP_WIDTH))
        lse_rows = jnp.transpose(lse_g, (0, 2, 1)).reshape(n_s, HEADS_PER_GROUP)
        lse_s.append(jnp.pad(lse_rows, ((0, 0), (0, LANES - HEADS_PER_GROUP))).reshape(1, 1, n_s, LANES))
        kv_s += [_to_state(ko), _to_state(vo)]
    x1_s, h2_s, comb_s = _post(o_s, lse_s, pc_s[None], sga_s[None], xs[None], mod_s4, w_attn_out[l], w_o[l], g2, wr, br,
                               expand_bf, tm=n_s, full_precision=True, routed=False)
    y_s = _moe_dense(h2_s, comb_s, x1_s, mod_s4, gf, w1_e, w3_e, w2_e, tm=n_s)

    conv_s = cu_s.reshape(dbsz, t_len, D_MODEL)[:, t_len - (CONV_K - 1):]
    return (y_p, y_s.reshape(dbsz, t_len, D_MODEL),
            *[_to_state(a) for a in states_p], conv_p.reshape(1, bsz, CONV_K - 1, D_MODEL),
            *kv_s, conv_s.reshape(1, dbsz, CONV_K - 1, D_MODEL))
```

```python
import functools

import numpy as np
import jax
import jax.numpy as jnp
from jax import lax
from jax.experimental import pallas as pl
from jax.experimental.pallas import tpu as pltpu

F32 = jnp.float32
BF16 = jnp.bfloat16
HIGHEST = lax.Precision.HIGHEST

D_MODEL = 1024
HEAD_DIM = 64
HEADS_PER_GROUP = 8
GROUP_WIDTH = HEADS_PER_GROUP * HEAD_DIM
DIL_GROUPS = ((128, 1), (512, 4), (2048, 16))
N_DIL = len(DIL_GROUPS)
ATTN_WIDTH = N_DIL * GROUP_WIDTH
ROT_DIM = HEAD_DIM // 4
ROPE_THETA = 500000.0
PAST_LEN = 16384
CONV_K = 3
N_EXPERT_GROUPS = 4
EXPERTS_PER_GROUP = 4
N_EXPERTS = N_EXPERT_GROUPS * EXPERTS_PER_GROUP
D_EXPERT = 512
RMS_EPS = 1e-6
IN_COLS = 3 * D_MODEL + 3 * ATTN_WIDTH + 2 * D_MODEL
OFF_U, OFF_GC, OFF_GB = 0, D_MODEL, 2 * D_MODEL
OFF_Q = 3 * D_MODEL
OFF_K = OFF_Q + ATTN_WIDTH
OFF_V = OFF_K + ATTN_WIDTH
OFF_GCONV = OFF_V + ATTN_WIDTH
OFF_GATTN = OFF_GCONV + D_MODEL

LANES = 128
SUBLANES = 8
CHUNKS_PER_GROUP = GROUP_WIDTH // LANES
KEYS_PER_QUERY = 129
Q_BLOCK = 128
NEG_BIG = -1e30

VMEM_LIMIT = 56 * 1024 * 1024


def _sigmoid(x):
    return 1.0 / (1.0 + jnp.exp(-x))


def _cparams(sem):
    return pltpu.CompilerParams(dimension_semantics=sem, vmem_limit_bytes=VMEM_LIMIT)


def _adaln_kernel(c_ref, w_ref, b_ref, o_ref):
    c = c_ref[...]
    s = c * _sigmoid(c)
    o_ref[...] = jnp.dot(s, w_ref[...], precision=HIGHEST, preferred_element_type=F32) + b_ref[...]


def _adaln(c_all, w_ada, b_ada):
    rows = c_all.shape[0]
    n_col = w_ada.shape[1] // D_MODEL
    return pl.pallas_call(
        _adaln_kernel,
        grid=(n_col,),
        in_specs=[
            pl.BlockSpec((rows, D_MODEL), lambda j: (0, 0)),
            pl.BlockSpec((D_MODEL, D_MODEL), lambda j: (0, j)),
            pl.BlockSpec((1, D_MODEL), lambda j: (0, j)),
        ],
        out_specs=pl.BlockSpec((rows, D_MODEL), lambda j: (0, j)),
        out_shape=jax.ShapeDtypeStruct((rows, w_ada.shape[1]), F32),
        compiler_params=_cparams(("arbitrary",)),
        name="adaln",
    )(c_all, w_ada, b_ada.reshape(1, -1))


def _rope_tables(pos, scale):
    half = ROT_DIM // 2
    inv_freq = jnp.power(jnp.float32(ROPE_THETA), -jnp.arange(half, dtype=F32) / half)
    ang = pos.astype(F32)[:, None] * inv_freq[None, :]
    cos, sin = jnp.cos(ang), jnp.sin(ang)
    lane_in_head = np.arange(LANES) % HEAD_DIM
    freq = lane_in_head % half
    first = lane_in_head < half
    second = (lane_in_head >= half) & (lane_in_head < ROT_DIM)
    a = jnp.where(first | second, cos[:, freq], 1.0)
    bm = jnp.where(first, -sin[:, freq], 0.0)
    bp = jnp.where(second, sin[:, freq], 0.0)
    return jnp.stack([a, bm, bp]) * scale


def _rope_chunk(zc, a, bm, bp):
    return zc * a + pltpu.roll(zc, LANES - ROT_DIM // 2, 1) * bm + pltpu.roll(zc, ROT_DIM // 2, 1) * bp


def _modulated_norm_bf16(x_ref, mod_ref, g1_ref):
    x = x_ref[...]
    var = jnp.mean(x * x, axis=-1, keepdims=True)
    h = (x * lax.rsqrt(var + RMS_EPS)) * (g1_ref[...] * (1.0 + mod_ref[1])) + mod_ref[0]
    return h.astype(BF16)


def _convproj_kernel(x_ref, mod_ref, g1_ref, wa_ref, wg_ref, cw_ref, wco_ref, pc_ref, sga_ref, cst_ref, s_ref,
                     *, tm, n_tiles):
    i = pl.program_id(1)
    hb = _modulated_norm_bf16(x_ref, mod_ref, g1_ref)

    def proj(w_ref, lo):
        return jnp.dot(hb, w_ref[:, lo:lo + D_MODEL], preferred_element_type=F32)

    cu = proj(wa_ref, OFF_GC) * proj(wa_ref, OFF_U)

    @pl.when(i == 0)
    def _():
        s_ref[0:SUBLANES, :] = jnp.zeros((SUBLANES, D_MODEL), F32)

    s_ref[SUBLANES:SUBLANES + tm, :] = cu
    cw = cw_ref[...]
    conv = (cw[0:1] * s_ref[SUBLANES - 2:SUBLANES - 2 + tm, :]
            + cw[1:2] * s_ref[SUBLANES - 1:SUBLANES - 1 + tm, :]
            + cw[2:3] * cu)
    yc = jnp.dot((proj(wa_ref, OFF_GB) * conv).astype(BF16), wco_ref[...], preferred_element_type=F32)
    pc_ref[...] = (_sigmoid(proj(wg_ref, 0)) * yc).astype(pc_ref.dtype)
    sga_ref[...] = _sigmoid(proj(wg_ref, D_MODEL)).astype(sga_ref.dtype)

    @pl.when(i == n_tiles - 1)
    def _():
        cst_ref[...] = s_ref[tm + SUBLANES - 2:tm + SUBLANES, :]

    s_ref[0:SUBLANES, :] = s_ref[tm:tm + SUBLANES, :]


def _convproj_prompt(x, mod4, g1, wa_bf, wg_bf, conv_w, wco_bf, *, tm):
    bsz, seq, _ = x.shape
    n_tiles = seq // tm
    const2 = lambda b, i: (0, 0)
    tile3 = lambda b, i: (b, i, 0)
    return pl.pallas_call(
        functools.partial(_convproj_kernel, tm=tm, n_tiles=n_tiles),
        grid=(bsz, n_tiles),
        in_specs=[
            pl.BlockSpec((None, tm, D_MODEL), tile3),
            pl.BlockSpec((None, 6, 1, D_MODEL), lambda b, i: (b, 0, 0, 0)),
            pl.BlockSpec((1, D_MODEL), const2),
            pl.BlockSpec(wa_bf.shape, const2, pipeline_mode=pl.Buffered(1)),
            pl.BlockSpec(wg_bf.shape, const2, pipeline_mode=pl.Buffered(1)),
            pl.BlockSpec((CONV_K, D_MODEL), const2),
            pl.BlockSpec((D_MODEL, D_MODEL), const2, pipeline_mode=pl.Buffered(1)),
        ],
        out_specs=[pl.BlockSpec((None, tm, D_MODEL), tile3), pl.BlockSpec((None, tm, D_MODEL), tile3),
                   pl.BlockSpec((None, CONV_K - 1, D_MODEL), lambda b, i: (b, 0, 0))],
        out_shape=[jax.ShapeDtypeStruct((bsz, seq, D_MODEL), BF16), jax.ShapeDtypeStruct((bsz, seq, D_MODEL), BF16),
                   jax.ShapeDtypeStruct((bsz, CONV_K - 1, D_MODEL), F32)],
        scratch_shapes=[pltpu.VMEM((tm + SUBLANES, D_MODEL), F32)],
        compiler_params=_cparams(("arbitrary", "arbitrary")),
        name="convproj_prompt",
    )(x, mod4, g1, wa_bf, wg_bf, conv_w, wco_bf)


def _qkvproj_kernel(x_ref, mod_ref, g1_ref, w_ref, rq_ref, rk_ref, *rest, tm, n_tiles, seq):
    qkv_refs = rest[0:3 * N_DIL]
    st_refs = rest[3 * N_DIL:3 * N_DIL + 2 * N_DIL]
    d_ref = rest[-1]
    i = pl.program_id(1)
    hb = _modulated_norm_bf16(x_ref, mod_ref, g1_ref)

    def proj(lo, width):
        return jnp.dot(hb, w_ref[:, lo:lo + width], preferred_element_type=F32)

    zq = proj(0, ATTN_WIDTH)
    zk = proj(ATTN_WIDTH, ATTN_WIDTH)
    zv = proj(2 * ATTN_WIDTH, ATTN_WIDTH)
    aq, bmq, bpq = rq_ref[0], rq_ref[1], rq_ref[2]
    ak, bmk, bpk = rk_ref[0], rk_ref[1], rk_ref[2]
    n_chunks = ATTN_WIDTH // LANES
    q_chunks, k_chunks, v_chunks = [], [], []
    for c in range(n_chunks):
        sl = slice(c * LANES, (c + 1) * LANES)
        q_chunks.append(_rope_chunk(zq[:, sl], aq, bmq, bpq))
        k_chunks.append(_rope_chunk(zk[:, sl], ak, bmk, bpk))
        v_chunks.append(zv[:, sl])

    for which, chunks in enumerate((q_chunks, k_chunks, v_chunks)):
        for g in range(N_DIL):
            out_ref = qkv_refs[3 * g + which]
            dil = DIL_GROUPS[g][1]
            for cc in range(CHUNKS_PER_GROUP):
                c = g * CHUNKS_PER_GROUP + cc
                sl = slice(cc * LANES, (cc + 1) * LANES)
                if dil == 1:
                    out_ref[0, :, sl] = chunks[c].astype(BF16)
                else:
                    d_ref[c] = chunks[c]
                    for r in range(dil):
                        out_ref[r, :, sl] = d_ref[c, pl.ds(r, tm // dil, stride=dil), :].astype(BF16)

    for g in range(N_DIL):
        kst, vst = st_refs[2 * g], st_refs[2 * g + 1]
        win = min(DIL_GROUPS[g][0], seq)
        if win >= tm:
            cond, r0 = i >= (seq - win) // tm, 0
        else:
            cond, r0 = i == n_tiles - 1, tm - win

        @pl.when(cond)
        def _(g=g, kst=kst, vst=vst, r0=r0):
            for cc in range(CHUNKS_PER_GROUP):
                c = g * CHUNKS_PER_GROUP + cc
                kst[cc * LANES:(cc + 1) * LANES, :] = k_chunks[c][r0:, :].T
                vst[cc * LANES:(cc + 1) * LANES, :] = v_chunks[c][r0:, :].T


def _qkvproj_prompt(x, mod4, g1, wqkv_bf, rope_q, rope_k, *, tm):
    bsz, seq, _ = x.shape
    n_tiles = seq // tm
    const2 = lambda b, i: (0, 0)
    tile3 = lambda b, i: (b, i, 0)
    in_specs = [
        pl.BlockSpec((None, tm, D_MODEL), tile3),
        pl.BlockSpec((None, 6, 1, D_MODEL), lambda b, i: (b, 0, 0, 0)),
        pl.BlockSpec((1, D_MODEL), const2),
        pl.BlockSpec(wqkv_bf.shape, const2, pipeline_mode=pl.Buffered(1)),
        pl.BlockSpec((3, tm, LANES), lambda b, i: (0, i, 0)),
        pl.BlockSpec((3, tm, LANES), lambda b, i: (0, i, 0)),
    ]
    out_shape, out_specs = [], []
    for _, dil in DIL_GROUPS:
        assert tm % (dil * 16) == 0
        for _ in range(3):
            out_shape.append(jax.ShapeDtypeStruct((bsz, dil, seq // dil, GROUP_WIDTH), BF16))
            out_specs.append(pl.BlockSpec((None, dil, tm // dil, GROUP_WIDTH), lambda b, i: (b, 0, i, 0)))
    for win, _ in DIL_GROUPS:
        win = min(win, seq)
        cols = min(win, tm)
        if win >= tm:
            imap = lambda b, i, ft=(seq - win) // tm: (b, 0, jnp.maximum(i - ft, 0))
        else:
            imap = lambda b, i: (b, 0, 0)
        for _ in range(2):
            out_shape.append(jax.ShapeDtypeStruct((bsz, GROUP_WIDTH, win), F32))
            out_specs.append(pl.BlockSpec((None, GROUP_WIDTH, cols), imap))
    return pl.pallas_call(
        functools.partial(_qkvproj_kernel, tm=tm, n_tiles=n_tiles, seq=seq),
        grid=(bsz, n_tiles),
        in_specs=in_specs,
        out_specs=out_specs,
        out_shape=out_shape,
        scratch_shapes=[pltpu.VMEM((ATTN_WIDTH // LANES, tm, LANES), F32)],
        compiler_params=_cparams(("arbitrary", "arbitrary")),
        name="qkvproj_prompt",
    )(x, mod4, g1, wqkv_bf, rope_q, rope_k)


ATTN_UNROLL = 8


def _attn_kernel(*refs, chunk, has_halo, n_cls):
    if has_halo:
        q_ref, k_ref, v_ref, kh_ref, vh_ref, o_ref, lse_ref, kbuf, vbuf = refs
    else:
        q_ref, k_ref, v_ref, o_ref, lse_ref, kbuf, vbuf = refs
    c = pl.program_id(2)
    if has_halo:
        kbuf[:, 0:Q_BLOCK, :] = kh_ref[...]
        vbuf[:, 0:Q_BLOCK, :] = vh_ref[...]
    else:
        kbuf[:, 0:Q_BLOCK, :] = jnp.zeros((n_cls, Q_BLOCK, GROUP_WIDTH), BF16)
        vbuf[:, 0:Q_BLOCK, :] = jnp.zeros((n_cls, Q_BLOCK, GROUP_WIDTH), BF16)
    kbuf[:, Q_BLOCK:Q_BLOCK + chunk, :] = k_ref[...]
    vbuf[:, Q_BLOCK:Q_BLOCK + chunk, :] = v_ref[...]
    blocks_per_cls = chunk // Q_BLOCK

    row = lax.broadcasted_iota(jnp.int32, (Q_BLOCK, 2 * Q_BLOCK), 0)
    col = lax.broadcasted_iota(jnp.int32, (Q_BLOCK, 2 * Q_BLOCK), 1)
    bias_main = jnp.where(col >= row, jnp.where(col <= row + Q_BLOCK, 0.0, NEG_BIG), NEG_BIG)
    bias_first = jnp.where(col >= Q_BLOCK, bias_main, NEG_BIG)
    lane = lax.broadcasted_iota(jnp.int32, (1, LANES), 1)
    lo_half = lane < HEAD_DIM

    def body(u, carry):
        cls, qb = u // blocks_per_cls, u % blocks_per_cls
        r0 = pl.multiple_of(qb * Q_BLOCK, Q_BLOCK)
        kt = kbuf[cls, pl.ds(r0, 2 * Q_BLOCK), :]
        vt = vbuf[cls, pl.ds(r0, 2 * Q_BLOCK), :]
        qt = q_ref[cls, pl.ds(r0, Q_BLOCK), :]
        is_first = jnp.logical_and(qb == 0, c == 0)
        bias = jnp.where(is_first, bias_first, bias_main)
        bias2 = jnp.concatenate([bias, bias], axis=0)
        lse_tile = jnp.zeros((Q_BLOCK, LANES), F32)
        for p in range(CHUNKS_PER_GROUP):
            sl = slice(p * LANES, (p + 1) * LANES)
            qp, kp, vp = qt[:, sl], kt[:, sl], vt[:, sl]
            zero = jnp.zeros_like(qp)
            q2 = jnp.concatenate([jnp.where(lo_half, qp, zero), jnp.where(lo_half, zero, qp)], axis=0)
            s = lax.dot_general(q2, kp, (((1,), (1,)), ((), ())), preferred_element_type=F32) + bias2
            m = jnp.max(s, axis=1, keepdims=True)
            e = jnp.exp(s - m)
            l = jnp.sum(e, axis=1, keepdims=True)
            o = jnp.dot(e.astype(BF16), vp, preferred_element_type=F32) * (1.0 / l)
            lse = m + jnp.log(l)
            for hh in range(2):
                lse_tile = jnp.where(lane == 2 * p + hh, lse[hh * Q_BLOCK:(hh + 1) * Q_BLOCK], lse_tile)
            o_ref[cls, pl.ds(r0, Q_BLOCK), sl] = jnp.where(lo_half, o[:Q_BLOCK], o[Q_BLOCK:])
        lse_ref[cls, pl.ds(r0, Q_BLOCK), :] = lse_tile
        return carry

    n_units = n_cls * blocks_per_cls
    lax.fori_loop(0, n_units, body, 0, unroll=min(ATTN_UNROLL, n_units))


def _attn_prompt_group(q, k, v, g, *, max_rows):
    bsz, dil, cls_len, _ = q.shape
    chunk = min(max_rows, cls_len)
    n_chunks = cls_len // chunk
    n_cls = min(dil, max_rows // chunk)
    has_halo = n_chunks > 1
    main = pl.BlockSpec((None, n_cls, chunk, GROUP_WIDTH), lambda b, r, c: (b, r, c, 0))
    in_specs = [main, main, main]
    args = [q, k, v]
    if has_halo:
        per = chunk // Q_BLOCK
        halo = pl.BlockSpec((None, n_cls, Q_BLOCK, GROUP_WIDTH), lambda b, r, c: (b, r, jnp.maximum(c * per - 1, 0), 0))
        in_specs += [halo, halo]
        args += [k, v]
    return pl.pallas_call(
        functools.partial(_attn_kernel, chunk=chunk, has_halo=has_halo, n_cls=n_cls),
        grid=(bsz, dil // n_cls, n_chunks),
        in_specs=in_specs,
        out_specs=[pl.BlockSpec((None, n_cls, chunk, GROUP_WIDTH), lambda b, r, c: (b, r, c, 0)),
                   pl.BlockSpec((None, n_cls, chunk, LANES), lambda b, r, c: (b, r, c, 0))],
        out_shape=[jax.ShapeDtypeStruct((bsz, dil, cls_len, GROUP_WIDTH), F32),
                   jax.ShapeDtypeStruct((bsz, dil, cls_len, LANES), F32)],
        scratch_shapes=[pltpu.VMEM((n_cls, chunk + Q_BLOCK, GROUP_WIDTH), BF16),
                        pltpu.VMEM((n_cls, chunk + Q_BLOCK, GROUP_WIDTH), BF16)],
        compiler_params=_cparams(("arbitrary", "arbitrary", "arbitrary")),
        name=f"attn_prompt_g{g}",
    )(*args)


def _split_bf16(x, n):
    parts = []
    r = x
    for _ in range(n):
        p = r.astype(BF16)
        parts.append(p)
        r = r - p.astype(F32)
    return parts


def _store_token_rows(ref, value, n_tokens, first_chunk=0, rows_per_token=None):
    n_chunks = value.shape[1] // LANES
    rows_per_token = rows_per_token or n_chunks
    for c in range(n_chunks):
        ref[pl.ds(first_chunk + c, n_tokens, stride=rows_per_token), :] = value[:, c * LANES:(c + 1) * LANES]


def _load_token_rows(ref, n_tokens, n_chunks, first_chunk=0, rows_per_token=None, lead=None):
    rows_per_token = rows_per_token or n_chunks

    def chunk(c):
        rows = pl.ds(first_chunk + c, n_tokens, stride=rows_per_token)
        return ref[rows, :] if lead is None else ref[lead, rows, :]

    return jnp.concatenate([chunk(c) for c in range(n_chunks)], axis=1)


H_ROWS = D_MODEL // LANES
Y_ROWS = 2 * D_MODEL // LANES

_PAIRS = ((0, 1), (0, 2), (0, 3), (1, 2), (1, 3), (2, 3))
N_CLASSES = N_EXPERT_GROUPS * len(_PAIRS)
META_CLASS, META_RANK, META_WA, META_WB = 0, 1, 2, 3


def _post_kernel(*refs, tm, dils, full_precision, routed):
    (o0_ref, o1_ref, o2_ref, l0_ref, l1_ref, l2_ref, pc_ref, sga_ref, x_ref, mod_ref,
     wao_ref, wo_ref, g2_ref, wr_ref, wrh_ref, wrl_ref, br_ref, exp_ref) = refs[:18]
    if routed:
        tri_ref, x1_ref, h2_ref, meta_ref, route_ref, cnt_ref, o_scr, l_scr, run_scr = refs[18:]
    else:
        x1_ref, h2_ref, comb_ref, o_scr, l_scr = refs[18:]

    def mm(a, w_ref):
        if full_precision:
            return jnp.dot(a, w_ref[...], precision=HIGHEST, preferred_element_type=F32)
        return jnp.dot(a.astype(BF16), w_ref[...], preferred_element_type=F32)

    def natural_order(ref, scr, dil, n_chunks):
        if dil == 1:
            return [ref[0, :, c * LANES:(c + 1) * LANES] for c in range(n_chunks)]
        out = []
        for c in range(n_chunks):
            for r in range(dil):
                scr[c, pl.ds(r, tm // dil, stride=dil), :] = ref[r, :, c * LANES:(c + 1) * LANES]
            out.append(scr[c])
        return out

    lses = [natural_order(ref, l_scr.at[g], dils[g], 1)[0] for g, ref in enumerate((l0_ref, l1_ref, l2_ref))]
    mx = jnp.maximum(lses[0], jnp.maximum(lses[1], lses[2]))
    es = [jnp.exp(v - mx) for v in lses]
    inv = 1.0 / (es[0] + es[1] + es[2])
    expand = exp_ref[...]
    attn_o = None
    for g, o_ref in enumerate((o0_ref, o1_ref, o2_ref)):
        w = es[g] * inv
        we = None
        for part in _split_bf16(w, 3 if full_precision else 2):
            t = jnp.dot(part, expand, preferred_element_type=F32)
            we = t if we is None else we + t
        o_nat = jnp.concatenate(natural_order(o_ref, o_scr.at[g], dils[g], CHUNKS_PER_GROUP), axis=1)
        term = we * o_nat
        attn_o = term if attn_o is None else attn_o + term

    y_attn = mm(attn_o, wao_ref)
    mixed = mm(pc_ref[...] + sga_ref[...] * y_attn, wo_ref)
    x1 = x_ref[...] + mod_ref[2] * mixed
    x1_ref[...] = x1
    var = jnp.mean(x1 * x1, axis=-1, keepdims=True)
    h2 = (x1 * lax.rsqrt(var + RMS_EPS)) * (g2_ref[...] * (1.0 + mod_ref[4])) + mod_ref[3]
    if routed:
        _store_token_rows(h2_ref, h2, tm)
    else:
        h2_ref[...] = h2.astype(h2_ref.dtype)

    if full_precision:
        lg = jnp.dot(h2, wr_ref[...], precision=HIGHEST, preferred_element_type=F32) + br_ref[...]
    else:
        h_hi, h_lo = _split_bf16(h2, 2)
        lg = (jnp.dot(h_hi, wrh_ref[...], preferred_element_type=F32)
              + jnp.dot(h_lo, wrh_ref[...], preferred_element_type=F32)
              + jnp.dot(h_hi, wrl_ref[...], preferred_element_type=F32)) + br_ref[...]
    lane_i = lax.broadcasted_iota(jnp.int32, (1, LANES), 1)
    lane = lane_i.astype(F32)
    lane_group = ((lane_i - N_EXPERT_GROUPS) >> 2).astype(F32)
    big = jnp.float32(1e9)
    gl = jnp.where(lane_i < N_EXPERT_GROUPS, lg, NEG_BIG)
    gmax = jnp.max(gl, axis=1, keepdims=True)
    gidx = jnp.min(jnp.where(gl == gmax, lane, big), axis=1, keepdims=True)
    g_w = 1.0 / jnp.sum(jnp.exp(gl - gmax), axis=1, keepdims=True)
    el = jnp.where(lane_group == gidx, lg, NEG_BIG)
    v1 = jnp.max(el, axis=1, keepdims=True)
    i1 = jnp.min(jnp.where(el == v1, lane, big), axis=1, keepdims=True)
    el2 = jnp.where(lane == i1, NEG_BIG, el)
    v2 = jnp.max(el2, axis=1, keepdims=True)
    i2 = jnp.min(jnp.where(el2 == v2, lane, big), axis=1, keepdims=True)
    t = jnp.exp(v2 - v1)
    den = 1.0 / (1.0 + t)
    w_top1, w_top2 = g_w * den, g_w * (t * den)
    if not routed:
        comb_ref[...] = jnp.where(lane == i1, w_top1, jnp.where(lane == i2, w_top2, 0.0))
        return

    base = jnp.float32(N_EXPERT_GROUPS) + jnp.float32(EXPERTS_PER_GROUP) * gidx
    e1, e2 = i1 - base, i2 - base
    first_is_lower = e1 < e2
    ea = jnp.where(first_is_lower, e1, e2)
    eb = jnp.where(first_is_lower, e2, e1)
    pair = ea * (7.0 - ea) * 0.5 + (eb - ea - 1.0)
    cls = gidx * jnp.float32(len(_PAIRS)) + pair
    onehot = lane == cls
    earlier = jnp.dot(tri_ref[...], jnp.where(onehot, 1.0, 0.0).astype(BF16), preferred_element_type=F32)

    @pl.when(jnp.logical_and(pl.program_id(0) == 0, pl.program_id(1) == 0))
    def _():
        run_scr[...] = jnp.zeros_like(run_scr)

    running = run_scr[...]
    rank = jnp.sum(jnp.where(onehot, earlier + running, 0.0), axis=1, keepdims=True)
    running = running + jnp.sum(jnp.where(onehot, 1.0, 0.0), axis=0, keepdims=True)
    run_scr[...] = running
    cnt_ref[...] = running
    w_a = jnp.where(first_is_lower, w_top1, w_top2)
    w_b = jnp.where(first_is_lower, w_top2, w_top1)
    meta = jnp.where(lane_i == META_CLASS, cls,
                     jnp.where(lane_i == META_RANK, rank,
                               jnp.where(lane_i == META_WA, w_a, jnp.where(lane_i == META_WB, w_b, 0.0))))
    meta_ref[...] = meta
    route_ref[...] = meta.T[0:SUBLANES, :].astype(jnp.int32)


def _post(o_list, lse_list, pc, sga, x, mod4, wao, wo, g2, wr, br, expand, *, tm, full_precision, routed):
    nb, rows, _ = x.shape
    tiles = rows // tm
    dils = tuple(o.shape[1] for o in o_list)
    tile3 = lambda b, i: (b, i, 0)
    const2 = lambda b, i: (0, 0)
    mod_rows = mod4.shape[2]
    if mod_rows == 1:
        mod_spec = pl.BlockSpec((None, 6, 1, D_MODEL), lambda b, i: (b, 0, 0, 0))
    else:
        mod_spec = pl.BlockSpec((None, 6, tm, D_MODEL), lambda b, i: (b, 0, i, 0))
    cls4 = lambda b, i: (b, 0, i, 0)
    in_specs = (
        [pl.BlockSpec((None, d, tm // d, GROUP_WIDTH), cls4) for d in dils]
        + [pl.BlockSpec((None, d, tm // d, LANES), cls4) for d in dils]
        + [pl.BlockSpec((None, tm, D_MODEL), tile3)] * 3 + [mod_spec]
        + [pl.BlockSpec(wao.shape, const2), pl.BlockSpec(wo.shape, const2), pl.BlockSpec((1, D_MODEL), const2)]
        + [pl.BlockSpec((D_MODEL, LANES), const2)] * 3
        + [pl.BlockSpec((1, LANES), const2), pl.BlockSpec((LANES, GROUP_WIDTH), const2)]
    )
    wr_hi = wr.astype(BF16)
    wr_lo = (wr - wr_hi.astype(F32)).astype(BF16)
    args = [*o_list, *lse_list, pc, sga, x, mod4, wao, wo, g2, wr, wr_hi, wr_lo, br, expand]
    scratch = [pltpu.VMEM((N_DIL, CHUNKS_PER_GROUP, tm, LANES), F32), pltpu.VMEM((N_DIL, 1, tm, LANES), F32)]
    if routed:
        tri = jnp.asarray(np.tril(np.ones((tm, tm), np.float32), -1), dtype=BF16)
        args.append(tri)
        in_specs = in_specs + [pl.BlockSpec((tm, tm), const2)]
        out_specs = [pl.BlockSpec((None, tm, D_MODEL), tile3), pl.BlockSpec((None, tm * H_ROWS, LANES), tile3),
                     pl.BlockSpec((None, tm, LANES), tile3),
                     pl.BlockSpec((None, None, SUBLANES, tm), lambda b, i: (b, i, 0, 0)),
                     pl.BlockSpec((1, LANES), const2)]
        out_shape = [jax.ShapeDtypeStruct((nb, rows, D_MODEL), F32),
                     jax.ShapeDtypeStruct((nb, rows * H_ROWS, LANES), F32),
                     jax.ShapeDtypeStruct((nb, rows, LANES), F32),
                     jax.ShapeDtypeStruct((nb, tiles, SUBLANES, tm), jnp.int32),
                     jax.ShapeDtypeStruct((1, LANES), F32)]
        scratch.append(pltpu.VMEM((1, LANES), F32))
    else:
        out_specs = [pl.BlockSpec((None, tm, D_MODEL), tile3), pl.BlockSpec((None, tm, D_MODEL), tile3),
                     pl.BlockSpec((None, tm, LANES), tile3)]
        out_shape = [jax.ShapeDtypeStruct((nb, rows, D_MODEL), F32), jax.ShapeDtypeStruct((nb, rows, D_MODEL), F32),
                     jax.ShapeDtypeStruct((nb, rows, LANES), F32)]
    return pl.pallas_call(
        functools.partial(_post_kernel, tm=tm, dils=dils, full_precision=full_precision, routed=routed),
        grid=(nb, tiles),
        in_specs=in_specs,
        out_specs=out_specs,
        out_shape=out_shape,
        scratch_shapes=scratch,
        compiler_params=_cparams(("arbitrary", "arbitrary")),
        name="post_sample" if full_precision else "post_prompt",
    )(*args)


DMA_LOOP_UNROLL = 8
DMA_PRIORITIES = 2


ROUTE_CLASS_ROW, ROUTE_RANK_ROW = META_CLASS, META_RANK


def _sorted_slot(starts_ref, route_ref, r):
    return starts_ref[route_ref[ROUTE_CLASS_ROW, r]] + route_ref[ROUTE_RANK_ROW, r]


def _token_rows(token, rows_per_token):
    return pl.ds(pl.multiple_of(token * rows_per_token, rows_per_token), rows_per_token)


def _dispatch_kernel(starts_ref, route_ref, src_ref, dst_hbm, buf, sems, *, rows_per_step, n_steps):
    i = pl.program_id(0)
    slot = i % 2

    def wait_slot(s):
        def body(r, carry):
            pltpu.make_async_copy(buf.at[s, _token_rows(0, H_ROWS)], dst_hbm.at[_token_rows(0, H_ROWS)],
                                  sems.at[s * rows_per_step + r]).wait()
            return carry
        lax.fori_loop(0, rows_per_step, body, 0, unroll=DMA_LOOP_UNROLL)

    @pl.when(i >= 1)
    def _():
        wait_slot(1 - slot)

    buf[slot] = src_ref[...]

    def issue(pair, carry):
        for prio in range(DMA_PRIORITIES):
            r = pair * DMA_PRIORITIES + prio
            pltpu.make_async_copy(buf.at[slot, _token_rows(r, H_ROWS)],
                                  dst_hbm.at[_token_rows(_sorted_slot(starts_ref, route_ref, r), H_ROWS)],
                                  sems.at[slot * rows_per_step + r]).start(priority=prio)
        return carry

    lax.fori_loop(0, rows_per_step // DMA_PRIORITIES, issue, 0, unroll=DMA_LOOP_UNROLL // DMA_PRIORITIES)

    @pl.when(i == n_steps - 1)
    def _():
        wait_slot(slot)


def _route_spec(rows_per_step, route_tile, shift=0, n_steps=None):
    per_tile = route_tile // rows_per_step

    def imap(i, starts):
        s = i + shift if n_steps is None else jnp.minimum(i + shift, n_steps - 1)
        return (s // per_tile, 0, s % per_tile)

    return pl.BlockSpec((None, SUBLANES, rows_per_step), imap, memory_space=pltpu.SMEM)


def _dispatch_rows(src, route, starts, *, rows_per_step):
    n = src.shape[0] // H_ROWS
    n_steps = n // rows_per_step
    grid_spec = pltpu.PrefetchScalarGridSpec(
        num_scalar_prefetch=1,
        grid=(n_steps,),
        in_specs=[_route_spec(rows_per_step, route.shape[2]),
                  pl.BlockSpec((rows_per_step * H_ROWS, LANES), lambda i, starts: (i, 0))],
        out_specs=pl.BlockSpec(memory_space=pl.ANY),
        scratch_shapes=[pltpu.VMEM((2, rows_per_step * H_ROWS, LANES), src.dtype),
                        pltpu.SemaphoreType.DMA((2 * rows_per_step,))],
    )
    return pl.pallas_call(
        functools.partial(_dispatch_kernel, rows_per_step=rows_per_step, n_steps=n_steps),
        grid_spec=grid_spec,
        out_shape=jax.ShapeDtypeStruct(src.shape, src.dtype),
        compiler_params=_cparams(("arbitrary",)),
        name="moe_dispatch",
    )(starts, route, src)


def _experts_kernel(tile_ref, ea_ref, eb_ref, lo_ref, hi_ref, first_ref, switch_ref, nv_ref,
                    x_ref, w1a_ref, w1b_ref, w3a_ref, w3b_ref, w2a_ref, w2b_ref, y_ref, w13_scr, w2_scr, *, tm):
    i = pl.program_id(0)

    @pl.when(i < nv_ref[0])
    def _():
        @pl.when(switch_ref[i] == 1)
        def _():
            for k, w_ref in enumerate((w1a_ref, w3a_ref, w1b_ref, w3b_ref)):
                w13_scr[k] = w_ref[...].astype(BF16)
            for k, w_ref in enumerate((w2a_ref, w2b_ref)):
                w2_scr[k] = w_ref[...].astype(BF16)

        x = _load_token_rows(x_ref, tm, H_ROWS).astype(BF16)

        def expert(k):
            a = jnp.dot(x, w13_scr[2 * k], preferred_element_type=F32)
            b = jnp.dot(x, w13_scr[2 * k + 1], preferred_element_type=F32)
            hid = (a * _sigmoid(a)) * b
            return jnp.dot(hid.astype(BF16), w2_scr[k], preferred_element_type=F32)

        y_both = jnp.concatenate([expert(0), expert(1)], axis=1)

        @pl.when(first_ref[i] == 1)
        def _():
            _store_token_rows(y_ref, y_both, tm)

        @pl.when(first_ref[i] == 0)
        def _():
            row = lax.broadcasted_iota(jnp.int32, (tm, 1), 0)
            old = _load_token_rows(y_ref, tm, Y_ROWS)
            _store_token_rows(y_ref, jnp.where(row >= lo_ref[i], jnp.where(row < hi_ref[i], y_both, old), old), tm)


def _experts_routed(hs, tables, w1, w3, w2, *, tm):
    n = hs.shape[0] // H_ROWS
    n_items = tables[0].shape[0]
    x_map = lambda i, tile, ea, eb, lo, hi, first, switch, nv: (tile[i], 0)
    wa_map = lambda i, tile, ea, eb, lo, hi, first, switch, nv: (ea[i], 0, 0)
    wb_map = lambda i, tile, ea, eb, lo, hi, first, switch, nv: (eb[i], 0, 0)
    w13 = lambda m: pl.BlockSpec((None, D_MODEL, D_EXPERT), m)
    w2s = lambda m: pl.BlockSpec((None, D_EXPERT, D_MODEL), m)
    grid_spec = pltpu.PrefetchScalarGridSpec(
        num_scalar_prefetch=len(tables),
        grid=(n_items,),
        in_specs=[pl.BlockSpec((tm * H_ROWS, LANES), x_map),
                  w13(wa_map), w13(wb_map), w13(wa_map), w13(wb_map), w2s(wa_map), w2s(wb_map)],
        out_specs=pl.BlockSpec((tm * Y_ROWS, LANES), x_map),
        scratch_shapes=[pltpu.VMEM((4, D_MODEL, D_EXPERT), BF16), pltpu.VMEM((2, D_EXPERT, D_MODEL), BF16)],
    )
    return pl.pallas_call(
        functools.partial(_experts_kernel, tm=tm),
        grid_spec=grid_spec,
        out_shape=jax.ShapeDtypeStruct((n * Y_ROWS, LANES), F32),
        compiler_params=_cparams(("arbitrary",)),
        name="moe_experts",
    )(*tables, hs, w1, w1, w3, w3, w2, w2)


def _combine_kernel(starts_ref, route_ref, route_next_ref, yp_hbm, x1_ref, meta_ref, mod_ref, gf_ref, y_ref, ybuf, sems,
                    *, rows_per_step, n_steps):
    i = pl.program_id(0)
    slot = i % 2

    def issue(r_ref, s):
        def body(pair, carry):
            for prio in range(DMA_PRIORITIES):
                r = pair * DMA_PRIORITIES + prio
                pltpu.make_async_copy(yp_hbm.at[_token_rows(_sorted_slot(starts_ref, r_ref, r), Y_ROWS)],
                                      ybuf.at[s, :, pl.ds(r, 1), :],
                                      sems.at[s * rows_per_step + r]).start(priority=prio)
            return carry
        lax.fori_loop(0, rows_per_step // DMA_PRIORITIES, body, 0, unroll=DMA_LOOP_UNROLL // DMA_PRIORITIES)

    @pl.when(i == 0)
    def _():
        issue(route_ref, 0)

    @pl.when(i + 1 < n_steps)
    def _():
        issue(route_next_ref, 1 - slot)

    def wait_body(r, carry):
        pltpu.make_async_copy(yp_hbm.at[_token_rows(0, Y_ROWS)], ybuf.at[slot, :, pl.ds(r, 1), :],
                              sems.at[slot * rows_per_step + r]).wait()
        return carry

    lax.fori_loop(0, rows_per_step, wait_body, 0, unroll=DMA_LOOP_UNROLL)
    ya = jnp.concatenate([ybuf[slot, c] for c in range(H_ROWS)], axis=1)
    yb = jnp.concatenate([ybuf[slot, H_ROWS + c] for c in range(H_ROWS)], axis=1)
    meta = meta_ref[...]
    lane = lax.broadcasted_iota(jnp.int32, (1, LANES), 1)
    w_a = jnp.sum(jnp.where(lane == META_WA, meta, 0.0), axis=1, keepdims=True)
    w_b = jnp.sum(jnp.where(lane == META_WB, meta, 0.0), axis=1, keepdims=True)
    x2 = x1_ref[...] + mod_ref[5] * (w_a * ya + w_b * yb)
    var = jnp.mean(x2 * x2, axis=-1, keepdims=True)
    y_ref[...] = (x2 * lax.rsqrt(var + RMS_EPS)) * gf_ref[...]


def _combine_final(yp, route, starts, x1, meta, mod4, gf, *, rows_per_step, rows_per_mod):
    n = x1.shape[0]
    n_steps = n // rows_per_step
    steps_per_mod = rows_per_mod // rows_per_step
    row = lambda i, starts: (i, 0)
    grid_spec = pltpu.PrefetchScalarGridSpec(
        num_scalar_prefetch=1,
        grid=(n_steps,),
        in_specs=[_route_spec(rows_per_step, route.shape[2]),
                  _route_spec(rows_per_step, route.shape[2], shift=1, n_steps=n_steps),
                  pl.BlockSpec(memory_space=pl.ANY),
                  pl.BlockSpec((rows_per_step, D_MODEL), row),
                  pl.BlockSpec((rows_per_step, LANES), row),
                  pl.BlockSpec((None, 6, 1, D_MODEL), lambda i, starts: (i // steps_per_mod, 0, 0, 0)),
                  pl.BlockSpec((1, D_MODEL), lambda i, starts: (0, 0))],
        out_specs=pl.BlockSpec((rows_per_step, D_MODEL), row),
        scratch_shapes=[pltpu.VMEM((2, Y_ROWS, rows_per_step, LANES), F32),
                        pltpu.SemaphoreType.DMA((2 * rows_per_step,))],
    )
    yp_rows = yp.reshape(yp.shape[0], 1, LANES)
    return pl.pallas_call(
        functools.partial(_combine_kernel, rows_per_step=rows_per_step, n_steps=n_steps),
        grid_spec=grid_spec,
        out_shape=jax.ShapeDtypeStruct((n, D_MODEL), F32),
        compiler_params=_cparams(("arbitrary",)),
        name="moe_combine",
    )(starts, route, route, yp_rows, x1, meta, mod4, gf)


def _routing_tables(counts, *, n, tm):
    counts = counts.astype(jnp.int32)
    ends = jnp.cumsum(counts)
    starts = ends - counts
    n_items = n // tm + N_CLASSES
    first_tile = starts // tm
    last_tile = jnp.maximum(ends - 1, starts) // tm
    visits = jnp.where(counts > 0, last_tile - first_tile + 1, 0)
    item_end = jnp.cumsum(visits)
    item_start = item_end - visits
    n_valid = item_end[-1]
    item = jnp.arange(n_items, dtype=jnp.int32)
    idx = jnp.minimum(item, n_valid - 1)
    c = jnp.sum((idx[:, None] >= item_end[None, :]).astype(jnp.int32), axis=1)
    class_ids = jnp.arange(N_CLASSES, dtype=jnp.int32)

    def pick(table):
        return jnp.sum(jnp.where(c[:, None] == class_ids[None, :], table[None, :], 0), axis=1)

    tile = pick(first_tile) + (idx - pick(item_start))
    live = item < n_valid
    lo = jnp.where(live, jnp.clip(pick(starts) - tile * tm, 0, tm), 0)
    hi = jnp.where(live, jnp.clip(pick(ends) - tile * tm, 0, tm), 0)
    prev_tile = jnp.concatenate([jnp.full((1,), -1, jnp.int32), tile[:-1]])
    first = (tile != prev_tile).astype(jnp.int32)
    prev_c = jnp.concatenate([jnp.full((1,), -1, jnp.int32), c[:-1]])
    switch = (c != prev_c).astype(jnp.int32)
    group_of_class = np.arange(N_CLASSES) // len(_PAIRS)
    pair_of_class = np.arange(N_CLASSES) % len(_PAIRS)
    pairs = np.asarray(_PAIRS)
    ea = pick(jnp.asarray(group_of_class * EXPERTS_PER_GROUP + pairs[pair_of_class, 0], dtype=jnp.int32))
    eb = pick(jnp.asarray(group_of_class * EXPERTS_PER_GROUP + pairs[pair_of_class, 1], dtype=jnp.int32))
    as_i32 = lambda a: a.astype(jnp.int32)
    return as_i32(starts), tuple(as_i32(a) for a in (tile, ea, eb, lo, hi, first, switch, n_valid.reshape(1)))


def _moe_kernel(h_ref, comb_ref, x1_ref, mod_ref, gf_ref, w1_ref, w3_ref, w2_ref, y_ref, acc_ref):
    e = pl.program_id(2)

    @pl.when(e == 0)
    def _():
        acc_ref[...] = jnp.zeros_like(acc_ref)

    h = h_ref[...].astype(BF16)
    a = jnp.dot(h, w1_ref[...].astype(BF16), preferred_element_type=F32)
    b = jnp.dot(h, w3_ref[...].astype(BF16), preferred_element_type=F32)
    lane = lax.broadcasted_iota(jnp.int32, (1, LANES), 1)
    cw = jnp.sum(jnp.where(lane == e + N_EXPERT_GROUPS, comb_ref[...], 0.0), axis=1, keepdims=True)
    hid = (a * _sigmoid(a)) * b * cw
    acc_ref[...] += jnp.dot(hid.astype(BF16), w2_ref[...].astype(BF16), preferred_element_type=F32)

    @pl.when(e == N_EXPERTS - 1)
    def _():
        x2 = x1_ref[...] + mod_ref[5] * acc_ref[...]
        var = jnp.mean(x2 * x2, axis=-1, keepdims=True)
        y_ref[...] = (x2 * lax.rsqrt(var + RMS_EPS)) * gf_ref[...]


def _moe_dense(h2, comb, x1, mod4, gf, w1_bf, w3_bf, w2_bf, *, tm):
    nb, rows, _ = x1.shape
    tiles = rows // tm
    tile3 = lambda b, i, e: (b, i, 0)
    if mod4.shape[2] == 1:
        mod_spec = pl.BlockSpec((None, 6, 1, D_MODEL), lambda b, i, e: (b, 0, 0, 0))
    else:
        mod_spec = pl.BlockSpec((None, 6, tm, D_MODEL), lambda b, i, e: (b, 0, i, 0))
    return pl.pallas_call(
        _moe_kernel,
        grid=(nb, tiles, N_EXPERTS),
        in_specs=[
            pl.BlockSpec((None, tm, D_MODEL), tile3),
            pl.BlockSpec((None, tm, LANES), tile3),
            pl.BlockSpec((None, tm, D_MODEL), tile3),
            mod_spec,
            pl.BlockSpec((1, D_MODEL), lambda b, i, e: (0, 0)),
            pl.BlockSpec((None, D_MODEL, D_EXPERT), lambda b, i, e: (e, 0, 0)),
            pl.BlockSpec((None, D_MODEL, D_EXPERT), lambda b, i, e: (e, 0, 0)),
            pl.BlockSpec((None, D_EXPERT, D_MODEL), lambda b, i, e: (e, 0, 0)),
        ],
        out_specs=pl.BlockSpec((None, tm, D_MODEL), tile3),
        out_shape=jax.ShapeDtypeStruct((nb, rows, D_MODEL), F32),
        scratch_shapes=[pltpu.VMEM((tm, D_MODEL), F32)],
        compiler_params=_cparams(("arbitrary", "arbitrary", "arbitrary")),
        name=f"moe_dense_{nb * rows}",
    )(h2, comb, x1, mod4, gf, w1_bf, w3_bf, w2_bf)


def _s_inproj_kernel(x_ref, mod_ref, g1_ref, w_ref, z_ref):
    x = x_ref[...]
    var = jnp.mean(x * x, axis=-1, keepdims=True)
    h = (x * lax.rsqrt(var + RMS_EPS)) * (g1_ref[...] * (1.0 + mod_ref[1])) + mod_ref[0]
    h_hi, h_lo = _split_bf16(h, 2)
    w_hi, w_lo = _split_bf16(w_ref[...], 2)
    z_ref[...] = (jnp.dot(h_hi, w_hi, preferred_element_type=F32) + jnp.dot(h_lo, w_hi, preferred_element_type=F32)
                  + jnp.dot(h_hi, w_lo, preferred_element_type=F32))


def _s_inproj(x, mod_tok, g1, w_in, *, col_block):
    n = x.shape[0]
    return pl.pallas_call(
        _s_inproj_kernel,
        grid=(IN_COLS // col_block,),
        in_specs=[pl.BlockSpec((n, D_MODEL), lambda j: (0, 0)),
                  pl.BlockSpec((6, n, D_MODEL), lambda j: (0, 0, 0)),
                  pl.BlockSpec((1, D_MODEL), lambda j: (0, 0)),
                  pl.BlockSpec((D_MODEL, col_block), lambda j: (0, j))],
        out_specs=pl.BlockSpec((n, col_block), lambda j: (0, j)),
        out_shape=jax.ShapeDtypeStruct((n, IN_COLS), F32),
        compiler_params=_cparams(("arbitrary",)),
        name="inproj_sample",
    )(x, mod_tok, g1, w_in)


def _s_mid_kernel(z_ref, p0_ref, p1_ref, cw_ref, wco_ref, rq_ref, rk_ref,
                  cu_ref, pc_ref, sga_ref, q_ref, k_ref, v_ref, *, t_len):
    n = z_ref.shape[0]
    cu = z_ref[:, OFF_GC:OFF_GC + D_MODEL] * z_ref[:, OFF_U:OFF_U + D_MODEL]
    cu_ref[...] = cu
    t = lax.broadcasted_iota(jnp.int32, (n, D_MODEL), 0) & (t_len - 1)
    prev1 = jnp.where(t >= 1, pltpu.roll(cu, 1, 0), p1_ref[...])
    prev2 = jnp.where(t >= 2, pltpu.roll(cu, 2, 0), jnp.where(t == 0, p0_ref[...], p1_ref[...]))
    cw = cw_ref[...]
    conv = cw[0:1] * prev2 + cw[1:2] * prev1 + cw[2:3] * cu
    yc = jnp.dot(z_ref[:, OFF_GB:OFF_GB + D_MODEL] * conv, wco_ref[...], precision=HIGHEST,
                 preferred_element_type=F32)
    pc_ref[...] = _sigmoid(z_ref[:, OFF_GCONV:OFF_GCONV + D_MODEL]) * yc
    sga_ref[...] = _sigmoid(z_ref[:, OFF_GATTN:OFF_GATTN + D_MODEL])
    aq, bmq, bpq = rq_ref[0], rq_ref[1], rq_ref[2]
    ak, bmk, bpk = rk_ref[0], rk_ref[1], rk_ref[2]
    for c in range(ATTN_WIDTH // LANES):
        sl = slice(c * LANES, (c + 1) * LANES)
        q_ref[:, sl] = _rope_chunk(z_ref[:, OFF_Q + c * LANES:OFF_Q + (c + 1) * LANES], aq, bmq, bpq)
        k_ref[:, sl] = _rope_chunk(z_ref[:, OFF_K + c * LANES:OFF_K + (c + 1) * LANES], ak, bmk, bpk)
    v_ref[...] = z_ref[:, OFF_V:OFF_V + ATTN_WIDTH]


def _s_mid(z, p0e, p1e, conv_w, w_conv_out, rope_q, rope_k, *, t_len):
    n = z.shape[0]
    assert t_len & (t_len - 1) == 0
    full = lambda shape: pl.BlockSpec(shape, lambda i: (0,) * len(shape))
    out_shape = [jax.ShapeDtypeStruct((n, D_MODEL), F32)] * 3 + [jax.ShapeDtypeStruct((n, ATTN_WIDTH), F32)] * 3
    return pl.pallas_call(
        functools.partial(_s_mid_kernel, t_len=t_len),
        grid=(1,),
        in_specs=[full(z.shape), full(p0e.shape), full(p1e.shape), full(conv_w.shape), full(w_conv_out.shape),
                  full(rope_q.shape), full(rope_k.shape)],
        out_specs=[full((n, D_MODEL))] * 3 + [full((n, ATTN_WIDTH))] * 3,
        out_shape=out_shape,
        compiler_params=_cparams(("arbitrary",)),
        name="mid_sample",
    )(z, p0e, p1e, conv_w, w_conv_out, rope_q, rope_k)


def _head_sum(x):
    return jnp.sum(x.reshape(HEADS_PER_GROUP, HEAD_DIM, x.shape[-1]), axis=1)


def _head_expand(x):
    n = x.shape[-1]
    return jnp.broadcast_to(x[:, None, :], (HEADS_PER_GROUP, HEAD_DIM, n)).reshape(GROUP_WIDTH, n)


def _s_attn_disjoint(q_ref, kn, vn, ck_ref, cv_ref, o_ref, lse_ref, s_scr, lane, new_idx, *, n_tiles, dil, t_len):
    cls = lane & (dil - 1)
    q_all = q_ref[...]
    qsel = jnp.zeros((GROUP_WIDTH, LANES), F32)
    for t in range(t_len):
        qsel = jnp.where(cls == t, jnp.broadcast_to(q_all[:, t:t + 1], (GROUP_WIDTH, LANES)), qsel)
    smax = None
    for j in range(n_tiles):
        s = jnp.where(cls < t_len, _head_sum(ck_ref[:, j * LANES:(j + 1) * LANES] * qsel), NEG_BIG)
        s_scr[:, j * LANES:(j + 1) * LANES] = s
        smax = s if smax is None else jnp.maximum(smax, s)
    s_new = _head_sum(kn * q_all)
    m_cols = []
    m_lane = jnp.zeros((HEADS_PER_GROUP, LANES), F32)
    m_new = jnp.zeros((HEADS_PER_GROUP, t_len), F32)
    for t in range(t_len):
        mt = jnp.maximum(jnp.max(jnp.where(cls == t, smax, NEG_BIG), axis=1, keepdims=True), s_new[:, t:t + 1])
        m_cols.append(mt)
        m_lane = jnp.where(cls == t, mt, m_lane)
        m_new = jnp.where(new_idx == t, mt, m_new)
    esum = jnp.zeros((HEADS_PER_GROUP, LANES), F32)
    for j in range(n_tiles):
        e = jnp.exp(s_scr[:, j * LANES:(j + 1) * LANES] - m_lane)
        s_scr[:, j * LANES:(j + 1) * LANES] = e
        esum = esum + e
    e_new = jnp.exp(s_new - m_new)
    l_cols = []
    inv_lane = jnp.zeros((HEADS_PER_GROUP, LANES), F32)
    inv_new = jnp.zeros((HEADS_PER_GROUP, t_len), F32)
    for t in range(t_len):
        lt = jnp.sum(jnp.where(cls == t, esum, 0.0), axis=1, keepdims=True) + e_new[:, t:t + 1]
        l_cols.append(lt)
        inv_lane = jnp.where(cls == t, 1.0 / lt, inv_lane)
        inv_new = jnp.where(new_idx == t, 1.0 / lt, inv_new)
    acc = None
    for j in range(n_tiles):
        term = cv_ref[:, j * LANES:(j + 1) * LANES] * _head_expand(s_scr[:, j * LANES:(j + 1) * LANES] * inv_lane)
        acc = term if acc is None else acc + term
    o_new = vn * _head_expand(e_new * inv_new)
    for t in range(t_len):
        o_ref[:, t:t + 1] = jnp.sum(jnp.where(cls == t, acc, 0.0), axis=1, keepdims=True) + o_new[:, t:t + 1]
        lse_ref[:, t:t + 1] = m_cols[t] + jnp.log(l_cols[t])


def _s_attn_kernel(q_ref, kn_ref, vn_ref, ck_ref, cv_ref, o_ref, lse_ref, ko_ref, vo_ref, s_scr,
                   *, win, dil, t_len):
    n_tiles = win // LANES
    lane = lax.broadcasted_iota(jnp.int32, (1, LANES), 1)
    new_idx = lax.broadcasted_iota(jnp.int32, (1, t_len), 1)
    kn = kn_ref[...]
    vn = vn_ref[...]
    if dil >= t_len:
        _s_attn_disjoint(q_ref, kn, vn, ck_ref, cv_ref, o_ref, lse_ref, s_scr, lane, new_idx,
                         n_tiles=n_tiles, dil=dil, t_len=t_len)
    for t in range(t_len if dil < t_len else 0):
        qb = jnp.broadcast_to(q_ref[:, t:t + 1], (GROUP_WIDTH, LANES))
        m = None
        for j in range(n_tiles):
            pos = lane + j * LANES
            s = _head_sum(ck_ref[:, j * LANES:(j + 1) * LANES] * qb)
            s = jnp.where(pos >= t, jnp.where(((pos - t) & (dil - 1)) == 0, s, NEG_BIG), NEG_BIG)
            s_scr[:, j * LANES:(j + 1) * LANES] = s
            mj = jnp.max(s, axis=1, keepdims=True)
            m = mj if m is None else jnp.maximum(m, mj)
        s_new = _head_sum(kn * qb[:, 0:t_len])
        s_new = jnp.where(new_idx <= t, jnp.where(((t - new_idx) & (dil - 1)) == 0, s_new, NEG_BIG), NEG_BIG)
        m = jnp.maximum(m, jnp.max(s_new, axis=1, keepdims=True))
        e_new = jnp.exp(s_new - m)
        l = jnp.sum(e_new, axis=1, keepdims=True)
        acc = None
        for j in range(n_tiles):
            e = jnp.exp(s_scr[:, j * LANES:(j + 1) * LANES] - m)
            l = l + jnp.sum(e, axis=1, keepdims=True)
            term = cv_ref[:, j * LANES:(j + 1) * LANES] * _head_expand(e)
            acc = term if acc is None else acc + term
        o = jnp.sum(acc, axis=1, keepdims=True) + jnp.sum(vn * _head_expand(e_new), axis=1, keepdims=True)
        o_ref[:, t:t + 1] = o * _head_expand(1.0 / l)
        lse_ref[:, t:t + 1] = m + jnp.log(l)

    for c_ref, new, out_ref in ((ck_ref, kn, ko_ref), (cv_ref, vn, vo_ref)):
        out_ref[...] = pltpu.roll(c_ref[...], win - t_len, 1)
        out_ref[:, win - t_len:win] = new


def _s_attn_group(q_t, kn_t, vn_t, cache_k, cache_v, g):
    bsz, _, t_len = q_t.shape
    win_full, dil = DIL_GROUPS[g]
    win = cache_k.shape[2]
    assert win == win_full and win == (KEYS_PER_QUERY - 1) * dil and win % LANES == 0
    assert dil & (dil - 1) == 0 and LANES % dil == 0
    grp = lambda b: (b, g, 0)
    per_b = lambda b: (b, 0, 0)
    return pl.pallas_call(
        functools.partial(_s_attn_kernel, win=win, dil=dil, t_len=t_len),
        grid=(bsz,),
        in_specs=[pl.BlockSpec((None, GROUP_WIDTH, t_len), grp)] * 3
        + [pl.BlockSpec((None, GROUP_WIDTH, win), per_b)] * 2,
        out_specs=[pl.BlockSpec((None, GROUP_WIDTH, t_len), per_b),
                   pl.BlockSpec((None, HEADS_PER_GROUP, t_len), per_b),
                   pl.BlockSpec((None, GROUP_WIDTH, win), per_b),
                   pl.BlockSpec((None, GROUP_WIDTH, win), per_b)],
        out_shape=[jax.ShapeDtypeStruct((bsz, GROUP_WIDTH, t_len), F32),
                   jax.ShapeDtypeStruct((bsz, HEADS_PER_GROUP, t_len), F32),
                   jax.ShapeDtypeStruct((bsz, GROUP_WIDTH, win), F32),
                   jax.ShapeDtypeStruct((bsz, GROUP_WIDTH, win), F32)],
        scratch_shapes=[pltpu.VMEM((HEADS_PER_GROUP, win), F32)],
        compiler_params=_cparams(("arbitrary",)),
        name=f"attn_sample_g{g}",
    )(q_t, kn_t, vn_t, cache_k, cache_v)


TM_INPROJ = 512
TM_POST = 512
TM_EXPERT = 256
ROWS_PER_DMA_STEP = 256
ATTN_ROWS_PER_STEP = 1024
S_COL_BLOCK = 512


def _to_state(a_t):
    b, _, length = a_t.shape
    return jnp.transpose(a_t.reshape(b, HEADS_PER_GROUP, HEAD_DIM, length), (0, 3, 1, 2))[None]


def _from_state(a):
    b, length = a.shape[0], a.shape[1]
    return jnp.transpose(a, (0, 2, 3, 1)).reshape(b, GROUP_WIDTH, length)


def kernel(x_prompt, x_sample, cache_k1, cache_v1, cache_k2, cache_v2, cache_k3, cache_v3, state_conv,
           c_prompt, c_sample, norm1_g, norm2_g, normf_g, w_ada, b_ada, w_in, conv_w, w_conv_out,
           w_attn_out, w_o, w_rg, b_rg, w_re, b_re, w1, w3, w2):
    depth = w_in.shape[0]
    assert depth == 1
    bsz, seq, _ = x_prompt.shape
    dbsz, t_len, _ = x_sample.shape
    n_s = dbsz * t_len
    l = 0

    w_a_bf = w_in[l][:, :OFF_Q].astype(BF16)
    w_g_bf = w_in[l][:, OFF_GCONV:].astype(BF16)
    w_qkv_bf = w_in[l][:, OFF_Q:OFF_GCONV].astype(BF16)
    wco_bf = w_conv_out[l].astype(BF16)
    wao_bf = w_attn_out[l].astype(BF16)
    wo_bf = w_o[l].astype(BF16)
    w1_e = w1[l].reshape(N_EXPERTS, D_MODEL, D_EXPERT)
    w3_e = w3[l].reshape(N_EXPERTS, D_MODEL, D_EXPERT)
    w2_e = w2[l].reshape(N_EXPERTS, D_EXPERT, D_MODEL)
    g1 = norm1_g[l].reshape(1, D_MODEL)
    g2 = norm2_g[l].reshape(1, D_MODEL)
    gf = normf_g.reshape(1, D_MODEL)
    n_route = N_EXPERT_GROUPS + N_EXPERTS
    wr = jnp.pad(jnp.concatenate([w_rg[l], w_re[l]], axis=1), ((0, 0), (0, LANES - n_route)))
    br = jnp.pad(jnp.concatenate([b_rg[l], b_re[l]]), (0, LANES - n_route)).reshape(1, LANES)
    head_of_lane = np.arange(GROUP_WIDTH) // HEAD_DIM
    expand_bf = jnp.asarray((np.arange(LANES)[:, None] == head_of_lane[None, :]).astype(np.float32), dtype=BF16)

    mod = _adaln(jnp.concatenate([c_prompt, c_sample], axis=0), w_ada[l], b_ada[l])
    mod_p4 = mod[:bsz].reshape(bsz, 6, 1, D_MODEL)
    mod_tok = jnp.repeat(mod[bsz:].reshape(dbsz, 1, 6, D_MODEL), t_len, axis=1)
    mod_tok = jnp.transpose(mod_tok.reshape(n_s, 6, D_MODEL), (1, 0, 2))
    mod_s4 = mod_tok[None]

    pos_p = jnp.arange(seq, dtype=jnp.int32)
    rope_q_p = _rope_tables(pos_p, HEAD_DIM ** -0.5)
    rope_k_p = _rope_tables(pos_p, 1.0)
    pc_p, sga_p, conv_p = _convproj_prompt(x_prompt, mod_p4, g1, w_a_bf, w_g_bf,
                                           conv_w[l], wco_bf, tm=TM_INPROJ)
    outs = _qkvproj_prompt(x_prompt, mod_p4, g1, w_qkv_bf, rope_q_p, rope_k_p, tm=TM_INPROJ)
    qkv_p = outs[0:3 * N_DIL]
    states_p = outs[3 * N_DIL:]
    o_p, lse_p = [], []
    for g in range(N_DIL):
        o_g, lse_g = _attn_prompt_group(qkv_p[3 * g], qkv_p[3 * g + 1], qkv_p[3 * g + 2], g, max_rows=ATTN_ROWS_PER_STEP)
        o_p.append(o_g)
        lse_p.append(lse_g)
    x1_p, h2rows_p, meta_p, route_p, cnt_p = _post(o_p, lse_p, pc_p, sga_p, x_prompt, mod_p4, wao_bf, wo_bf, g2, wr, br,
                                                   expand_bf, tm=TM_POST, full_precision=False, routed=True)
    n_p = bsz * seq
    route = route_p.reshape(n_p // TM_POST, SUBLANES, TM_POST)
    starts, tables = _routing_tables(cnt_p[0, :N_CLASSES], n=n_p, tm=TM_EXPERT)
    hs = _dispatch_rows(h2rows_p.reshape(n_p * H_ROWS, LANES), route, starts, rows_per_step=ROWS_PER_DMA_STEP)
    yp = _experts_routed(hs, tables, w1_e, w3_e, w2_e, tm=TM_EXPERT)
    y_p = _combine_final(yp, route, starts, x1_p.reshape(n_p, D_MODEL), meta_p.reshape(n_p, LANES), mod_p4, gf,
                         rows_per_step=ROWS_PER_DMA_STEP, rows_per_mod=seq).reshape(bsz, seq, D_MODEL)

    pos_s = PAST_LEN + jnp.arange(t_len, dtype=jnp.int32)
    rope_q_s = jnp.tile(_rope_tables(pos_s, HEAD_DIM ** -0.5), (1, dbsz, 1))
    rope_k_s = jnp.tile(_rope_tables(pos_s, 1.0), (1, dbsz, 1))
    xs = x_sample.reshape(n_s, D_MODEL)
    z_s = _s_inproj(xs, mod_tok, g1, w_in[l], col_block=S_COL_BLOCK)
    past = state_conv[l]
    p0e = jnp.repeat(past[:, 0], t_len, axis=0)
    p1e = jnp.repeat(past[:, 1], t_len, axis=0)
    cu_s, pc_s, sga_s, q_s, k_s, v_s = _s_mid(z_s, p0e, p1e, conv_w[l], w_conv_out[l], rope_q_s, rope_k_s,
                                              t_len=t_len)
    to_cols = lambda a: jnp.transpose(a.reshape(dbsz, t_len, ATTN_WIDTH), (0, 2, 1))
    q_t, kn_t, vn_t = to_cols(q_s), to_cols(k_s), to_cols(v_s)
    caches = ((cache_k1, cache_v1), (cache_k2, cache_v2), (cache_k3, cache_v3))
    o_s, lse_s, kv_s = [], [], []
    for g, (ck, cv) in enumerate(caches):
        o_g, lse_g, ko, vo = _s_attn_group(q_t, kn_t, vn_t, _from_state(ck[l]), _from_state(cv[l]), g)
        o_s.append(jnp.transpose(o_g, (0, 2, 1)).reshape(1, 1, n_s, GROUP_WIDTH))
        lse_rows = jnp.transpose(lse_g, (0, 2, 1)).reshape(n_s, HEADS_PER_GROUP)
        lse_s.append(jnp.pad(lse_rows, ((0, 0), (0, LANES - HEADS_PER_GROUP))).reshape(1, 1, n_s, LANES))
        kv_s += [_to_state(ko), _to_state(vo)]
    x1_s, h2_s, comb_s = _post(o_s, lse_s, pc_s[None], sga_s[None], xs[None], mod_s4, w_attn_out[l], w_o[l], g2, wr, br,
                               expand_bf, tm=n_s, full_precision=True, routed=False)
    y_s = _moe_dense(h2_s, comb_s, x1_s, mod_s4, gf, w1_e, w3_e, w2_e, tm=n_s)

    conv_s = cu_s.reshape(dbsz, t_len, D_MODEL)[:, t_len - (CONV_K - 1):]
    return (y_p, y_s.reshape(dbsz, t_len, D_MODEL),
            *[_to_state(a) for a in states_p], conv_p.reshape(1, bsz, CONV_K - 1, D_MODEL),
            *kv_s, conv_s.reshape(1, dbsz, CONV_K - 1, D_MODEL))
```

```python
import functools

import numpy as np
import jax
import jax.numpy as jnp
from jax import lax
from jax.experimental import pallas as pl
from jax.experimental.pallas import tpu as pltpu

F32 = jnp.float32
BF16 = jnp.bfloat16
HIGHEST = lax.Precision.HIGHEST

D_MODEL = 1024
HEAD_DIM = 64
HEADS_PER_GROUP = 8
GROUP_WIDTH = HEADS_PER_GROUP * HEAD_DIM
DIL_GROUPS = ((128, 1), (512, 4), (2048, 16))
N_DIL = len(DIL_GROUPS)
ATTN_WIDTH = N_DIL * GROUP_WIDTH
ROT_DIM = HEAD_DIM // 4
ROPE_THETA = 500000.0
PAST_LEN = 16384
CONV_K = 3
N_EXPERT_GROUPS = 4
EXPERTS_PER_GROUP = 4
N_EXPERTS = N_EXPERT_GROUPS * EXPERTS_PER_GROUP
D_EXPERT = 512
RMS_EPS = 1e-6
IN_COLS = 3 * D_MODEL + 3 * ATTN_WIDTH + 2 * D_MODEL
OFF_U, OFF_GC, OFF_GB = 0, D_MODEL, 2 * D_MODEL
OFF_Q = 3 * D_MODEL
OFF_K = OFF_Q + ATTN_WIDTH
OFF_V = OFF_K + ATTN_WIDTH
OFF_GCONV = OFF_V + ATTN_WIDTH
OFF_GATTN = OFF_GCONV + D_MODEL

LANES = 128
SUBLANES = 8
CHUNKS_PER_GROUP = GROUP_WIDTH // LANES
KEYS_PER_QUERY = 129
Q_BLOCK = 128
NEG_BIG = -1e30

VMEM_LIMIT = 56 * 1024 * 1024


def _sigmoid(x):
    return 1.0 / (1.0 + jnp.exp(-x))


def _cparams(sem):
    return pltpu.CompilerParams(dimension_semantics=sem, vmem_limit_bytes=VMEM_LIMIT)


def _adaln_kernel(c_ref, w_ref, b_ref, o_ref):
    c = c_ref[...]
    s = c * _sigmoid(c)
    o_ref[...] = jnp.dot(s, w_ref[...], precision=HIGHEST, preferred_element_type=F32) + b_ref[...]


def _adaln(c_all, w_ada, b_ada):
    rows = c_all.shape[0]
    n_col = w_ada.shape[1] // D_MODEL
    return pl.pallas_call(
        _adaln_kernel,
        grid=(n_col,),
        in_specs=[
            pl.BlockSpec((rows, D_MODEL), lambda j: (0, 0)),
            pl.BlockSpec((D_MODEL, D_MODEL), lambda j: (0, j)),
            pl.BlockSpec((1, D_MODEL), lambda j: (0, j)),
        ],
        out_specs=pl.BlockSpec((rows, D_MODEL), lambda j: (0, j)),
        out_shape=jax.ShapeDtypeStruct((rows, w_ada.shape[1]), F32),
        compiler_params=_cparams(("arbitrary",)),
        name="adaln",
    )(c_all, w_ada, b_ada.reshape(1, -1))


def _rope_tables(pos, scale):
    half = ROT_DIM // 2
    inv_freq = jnp.power(jnp.float32(ROPE_THETA), -jnp.arange(half, dtype=F32) / half)
    ang = pos.astype(F32)[:, None] * inv_freq[None, :]
    cos, sin = jnp.cos(ang), jnp.sin(ang)
    lane_in_head = np.arange(LANES) % HEAD_DIM
    freq = lane_in_head % half
    first = lane_in_head < half
    second = (lane_in_head >= half) & (lane_in_head < ROT_DIM)
    a = jnp.where(first | second, cos[:, freq], 1.0)
    bm = jnp.where(first, -sin[:, freq], 0.0)
    bp = jnp.where(second, sin[:, freq], 0.0)
    return jnp.stack([a, bm, bp]) * scale


def _rope_chunk(zc, a, bm, bp):
    return zc * a + pltpu.roll(zc, LANES - ROT_DIM // 2, 1) * bm + pltpu.roll(zc, ROT_DIM // 2, 1) * bp


def _modulated_norm_bf16(x_ref, mod_ref, g1_ref):
    x = x_ref[...]
    var = jnp.mean(x * x, axis=-1, keepdims=True)
    h = (x * lax.rsqrt(var + RMS_EPS)) * (g1_ref[...] * (1.0 + mod_ref[1])) + mod_ref[0]
    return h.astype(BF16)


def _convproj_kernel(x_ref, mod_ref, g1_ref, wa_ref, wg_ref, cw_ref, wco_ref, pc_ref, sga_ref, cst_ref, s_ref,
                     *, tm, n_tiles):
    i = pl.program_id(1)
    hb = _modulated_norm_bf16(x_ref, mod_ref, g1_ref)

    def proj(w_ref, lo):
        return jnp.dot(hb, w_ref[:, lo:lo + D_MODEL], preferred_element_type=F32)

    cu = proj(wa_ref, OFF_GC) * proj(wa_ref, OFF_U)

    @pl.when(i == 0)
    def _():
        s_ref[0:SUBLANES, :] = jnp.zeros((SUBLANES, D_MODEL), F32)

    s_ref[SUBLANES:SUBLANES + tm, :] = cu
    cw = cw_ref[...]
    conv = (cw[0:1] * s_ref[SUBLANES - 2:SUBLANES - 2 + tm, :]
            + cw[1:2] * s_ref[SUBLANES - 1:SUBLANES - 1 + tm, :]
            + cw[2:3] * cu)
    yc = jnp.dot((proj(wa_ref, OFF_GB) * conv).astype(BF16), wco_ref[...], preferred_element_type=F32)
    pc_ref[...] = (_sigmoid(proj(wg_ref, 0)) * yc).astype(pc_ref.dtype)
    sga_ref[...] = _sigmoid(proj(wg_ref, D_MODEL)).astype(sga_ref.dtype)

    @pl.when(i == n_tiles - 1)
    def _():
        cst_ref[...] = s_ref[tm + SUBLANES - 2:tm + SUBLANES, :]

    s_ref[0:SUBLANES, :] = s_ref[tm:tm + SUBLANES, :]


def _convproj_prompt(x, mod4, g1, wa_bf, wg_bf, conv_w, wco_bf, *, tm):
    bsz, seq, _ = x.shape
    n_tiles = seq // tm
    const2 = lambda b, i: (0, 0)
    tile3 = lambda b, i: (b, i, 0)
    return pl.pallas_call(
        functools.partial(_convproj_kernel, tm=tm, n_tiles=n_tiles),
        grid=(bsz, n_tiles),
        in_specs=[
            pl.BlockSpec((None, tm, D_MODEL), tile3),
            pl.BlockSpec((None, 6, 1, D_MODEL), lambda b, i: (b, 0, 0, 0)),
            pl.BlockSpec((1, D_MODEL), const2),
            pl.BlockSpec(wa_bf.shape, const2, pipeline_mode=pl.Buffered(1)),
            pl.BlockSpec(wg_bf.shape, const2, pipeline_mode=pl.Buffered(1)),
            pl.BlockSpec((CONV_K, D_MODEL), const2),
            pl.BlockSpec((D_MODEL, D_MODEL), const2, pipeline_mode=pl.Buffered(1)),
        ],
        out_specs=[pl.BlockSpec((None, tm, D_MODEL), tile3), pl.BlockSpec((None, tm, D_MODEL), tile3),
                   pl.BlockSpec((None, CONV_K - 1, D_MODEL), lambda b, i: (b, 0, 0))],
        out_shape=[jax.ShapeDtypeStruct((bsz, seq, D_MODEL), BF16), jax.ShapeDtypeStruct((bsz, seq, D_MODEL), BF16),
                   jax.ShapeDtypeStruct((bsz, CONV_K - 1, D_MODEL), F32)],
        scratch_shapes=[pltpu.VMEM((tm + SUBLANES, D_MODEL), F32)],
        compiler_params=_cparams(("arbitrary", "arbitrary")),
        name="convproj_prompt",
    )(x, mod4, g1, wa_bf, wg_bf, conv_w, wco_bf)


def _qkvproj_kernel(x_ref, mod_ref, g1_ref, w_ref, rq_ref, rk_ref, *rest, tm, n_tiles, seq):
    qkv_refs = rest[0:3 * N_DIL]
    st_refs = rest[3 * N_DIL:3 * N_DIL + 2 * N_DIL]
    d_ref = rest[-1]
    i = pl.program_id(1)
    hb = _modulated_norm_bf16(x_ref, mod_ref, g1_ref)

    def proj(lo, width):
        return jnp.dot(hb, w_ref[:, lo:lo + width], preferred_element_type=F32)

    zq = proj(0, ATTN_WIDTH)
    zk = proj(ATTN_WIDTH, ATTN_WIDTH)
    zv = proj(2 * ATTN_WIDTH, ATTN_WIDTH)
    aq, bmq, bpq = rq_ref[0], rq_ref[1], rq_ref[2]
    ak, bmk, bpk = rk_ref[0], rk_ref[1], rk_ref[2]
    n_chunks = ATTN_WIDTH // LANES
    q_chunks, k_chunks, v_chunks = [], [], []
    for c in range(n_chunks):
        sl = slice(c * LANES, (c + 1) * LANES)
        q_chunks.append(_rope_chunk(zq[:, sl], aq, bmq, bpq))
        k_chunks.append(_rope_chunk(zk[:, sl], ak, bmk, bpk))
        v_chunks.append(zv[:, sl])

    for which, chunks in enumerate((q_chunks, k_chunks, v_chunks)):
        for g in range(N_DIL):
            out_ref = qkv_refs[3 * g + which]
            dil = DIL_GROUPS[g][1]
            for cc in range(CHUNKS_PER_GROUP):
                c = g * CHUNKS_PER_GROUP + cc
                sl = slice(cc * LANES, (cc + 1) * LANES)
                if dil == 1:
                    out_ref[0, :, sl] = chunks[c].astype(BF16)
                else:
                    d_ref[c] = chunks[c]
                    for r in range(dil):
                        out_ref[r, :, sl] = d_ref[c, pl.ds(r, tm // dil, stride=dil), :].astype(BF16)

    for g in range(N_DIL):
        kst, vst = st_refs[2 * g], st_refs[2 * g + 1]
        win = min(DIL_GROUPS[g][0], seq)
        if win >= tm:
            cond, r0 = i >= (seq - win) // tm, 0
        else:
            cond, r0 = i == n_tiles - 1, tm - win

        @pl.when(cond)
        def _(g=g, kst=kst, vst=vst, r0=r0):
            for cc in range(CHUNKS_PER_GROUP):
                c = g * CHUNKS_PER_GROUP + cc
                kst[cc * LANES:(cc + 1) * LANES, :] = k_chunks[c][r0:, :].T
                vst[cc * LANES:(cc + 1) * LANES, :] = v_chunks[c][r0:, :].T


def _qkvproj_prompt(x, mod4, g1, wqkv_bf, rope_q, rope_k, *, tm):
    bsz, seq, _ = x.shape
    n_tiles = seq // tm
    const2 = lambda b, i: (0, 0)
    tile3 = lambda b, i: (b, i, 0)
    in_specs = [
        pl.BlockSpec((None, tm, D_MODEL), tile3),
        pl.BlockSpec((None, 6, 1, D_MODEL), lambda b, i: (b, 0, 0, 0)),
        pl.BlockSpec((1, D_MODEL), const2),
        pl.BlockSpec(wqkv_bf.shape, const2, pipeline_mode=pl.Buffered(1)),
        pl.BlockSpec((3, tm, LANES), lambda b, i: (0, i, 0)),
        pl.BlockSpec((3, tm, LANES), lambda b, i: (0, i, 0)),
    ]
    out_shape, out_specs = [], []
    for _, dil in DIL_GROUPS:
        assert tm % (dil * 16) == 0
        for _ in range(3):
            out_shape.append(jax.ShapeDtypeStruct((bsz, dil, seq // dil, GROUP_WIDTH), BF16))
            out_specs.append(pl.BlockSpec((None, dil, tm // dil, GROUP_WIDTH), lambda b, i: (b, 0, i, 0)))
    for win, _ in DIL_GROUPS:
        win = min(win, seq)
        cols = min(win, tm)
        if win >= tm:
            imap = lambda b, i, ft=(seq - win) // tm: (b, 0, jnp.maximum(i - ft, 0))
        else:
            imap = lambda b, i: (b, 0, 0)
        for _ in range(2):
            out_shape.append(jax.ShapeDtypeStruct((bsz, GROUP_WIDTH, win), F32))
            out_specs.append(pl.BlockSpec((None, GROUP_WIDTH, cols), imap))
    return pl.pallas_call(
        functools.partial(_qkvproj_kernel, tm=tm, n_tiles=n_tiles, seq=seq),
        grid=(bsz, n_tiles),
        in_specs=in_specs,
        out_specs=out_specs,
        out_shape=out_shape,
        scratch_shapes=[pltpu.VMEM((ATTN_WIDTH // LANES, tm, LANES), F32)],
        compiler_params=_cparams(("arbitrary", "arbitrary")),
        name="qkvproj_prompt",
    )(x, mod4, g1, wqkv_bf, rope_q, rope_k)


ATTN_UNROLL = 8


def _attn_kernel(*refs, chunk, has_halo, n_cls):
    if has_halo:
        q_ref, k_ref, v_ref, kh_ref, vh_ref, o_ref, lse_ref, kbuf, vbuf = refs
    else:
        q_ref, k_ref, v_ref, o_ref, lse_ref, kbuf, vbuf = refs
    c = pl.program_id(2)
    if has_halo:
        kbuf[:, 0:Q_BLOCK, :] = kh_ref[...]
        vbuf[:, 0:Q_BLOCK, :] = vh_ref[...]
    else:
        kbuf[:, 0:Q_BLOCK, :] = jnp.zeros((n_cls, Q_BLOCK, GROUP_WIDTH), BF16)
        vbuf[:, 0:Q_BLOCK, :] = jnp.zeros((n_cls, Q_BLOCK, GROUP_WIDTH), BF16)
    kbuf[:, Q_BLOCK:Q_BLOCK + chunk, :] = k_ref[...]
    vbuf[:, Q_BLOCK:Q_BLOCK + chunk, :] = v_ref[...]
    blocks_per_cls = chunk // Q_BLOCK

    row = lax.broadcasted_iota(jnp.int32, (Q_BLOCK, 2 * Q_BLOCK), 0)
    col = lax.broadcasted_iota(jnp.int32, (Q_BLOCK, 2 * Q_BLOCK), 1)
    bias_main = jnp.where(col >= row, jnp.where(col <= row + Q_BLOCK, 0.0, NEG_BIG), NEG_BIG)
    bias_first = jnp.where(col >= Q_BLOCK, bias_main, NEG_BIG)
    lane = lax.broadcasted_iota(jnp.int32, (1, LANES), 1)
    lo_half = lane < HEAD_DIM

    def body(u, carry):
        cls, qb = u // blocks_per_cls, u % blocks_per_cls
        r0 = pl.multiple_of(qb * Q_BLOCK, Q_BLOCK)
        kt = kbuf[cls, pl.ds(r0, 2 * Q_BLOCK), :]
        vt = vbuf[cls, pl.ds(r0, 2 * Q_BLOCK), :]
        qt = q_ref[cls, pl.ds(r0, Q_BLOCK), :]
        is_first = jnp.logical_and(qb == 0, c == 0)
        bias = jnp.where(is_first, bias_first, bias_main)
        bias2 = jnp.concatenate([bias, bias], axis=0)
        lse_tile = jnp.zeros((Q_BLOCK, LANES), F32)
        for p in range(CHUNKS_PER_GROUP):
            sl = slice(p * LANES, (p + 1) * LANES)
            qp, kp, vp = qt[:, sl], kt[:, sl], vt[:, sl]
            zero = jnp.zeros_like(qp)
            q2 = jnp.concatenate([jnp.where(lo_half, qp, zero), jnp.where(lo_half, zero, qp)], axis=0)
            s = lax.dot_general(q2, kp, (((1,), (1,)), ((), ())), preferred_element_type=F32) + bias2
            m = jnp.max(s, axis=1, keepdims=True)
            e = jnp.exp(s - m)
            l = jnp.sum(e, axis=1, keepdims=True)
            o = jnp.dot(e.astype(BF16), vp, preferred_element_type=F32) * (1.0 / l)
            lse = m + jnp.log(l)
            for hh in range(2):
                lse_tile = jnp.where(lane == 2 * p + hh, lse[hh * Q_BLOCK:(hh + 1) * Q_BLOCK], lse_tile)
            o_ref[cls, pl.ds(r0, Q_BLOCK), sl] = jnp.where(lo_half, o[:Q_BLOCK], o[Q_BLOCK:])
        lse_ref[cls, pl.ds(r0, Q_BLOCK), :] = lse_tile
        return carry

    n_units = n_cls * blocks_per_cls
    lax.fori_loop(0, n_units, body, 0, unroll=min(ATTN_UNROLL, n_units))


def _attn_prompt_group(q, k, v, g, *, max_rows):
    bsz, dil, cls_len, _ = q.shape
    chunk = min(max_rows, cls_len)
    n_chunks = cls_len // chunk
    n_cls = min(dil, max_rows // chunk)
    has_halo = n_chunks > 1
    main = pl.BlockSpec((None, n_cls, chunk, GROUP_WIDTH), lambda b, r, c: (b, r, c, 0))
    in_specs = [main, main, main]
    args = [q, k, v]
    if has_halo:
        per = chunk // Q_BLOCK
        halo = pl.BlockSpec((None, n_cls, Q_BLOCK, GROUP_WIDTH), lambda b, r, c: (b, r, jnp.maximum(c * per - 1, 0), 0))
        in_specs += [halo, halo]
        args += [k, v]
    return pl.pallas_call(
        functools.partial(_attn_kernel, chunk=chunk, has_halo=has_halo, n_cls=n_cls),
        grid=(bsz, dil // n_cls, n_chunks),
        in_specs=in_specs,
        out_specs=[pl.BlockSpec((None, n_cls, chunk, GROUP_WIDTH), lambda b, r, c: (b, r, c, 0)),
                   pl.BlockSpec((None, n_cls, chunk, LANES), lambda b, r, c: (b, r, c, 0))],
        out_shape=[jax.ShapeDtypeStruct((bsz, dil, cls_len, GROUP_WIDTH), F32),
                   jax.ShapeDtypeStruct((bsz, dil, cls_len, LANES), F32)],
        scratch_shapes=[pltpu.VMEM((n_cls, chunk + Q_BLOCK, GROUP_WIDTH), BF16),
                        pltpu.VMEM((n_cls, chunk + Q_BLOCK, GROUP_WIDTH), BF16)],
        compiler_params=_cparams(("arbitrary", "arbitrary", "arbitrary")),
        name=f"attn_prompt_g{g}",
    )(*args)


def _split_bf16(x, n):
    parts = []
    r = x
    for _ in range(n):
        p = r.astype(BF16)
        parts.append(p)
        r = r - p.astype(F32)
    return parts


def _store_token_rows(ref, value, n_tokens, first_chunk=0, rows_per_token=None):
    n_chunks = value.shape[1] // LANES
    rows_per_token = rows_per_token or n_chunks
    for c in range(n_chunks):
        ref[pl.ds(first_chunk + c, n_tokens, stride=rows_per_token), :] = value[:, c * LANES:(c + 1) * LANES]


def _load_token_rows(ref, n_tokens, n_chunks, first_chunk=0, rows_per_token=None, lead=None):
    rows_per_token = rows_per_token or n_chunks

    def chunk(c):
        rows = pl.ds(first_chunk + c, n_tokens, stride=rows_per_token)
        return ref[rows, :] if lead is None else ref[lead, rows, :]

    return jnp.concatenate([chunk(c) for c in range(n_chunks)], axis=1)


H_ROWS = D_MODEL // LANES
Y_ROWS = 2 * D_MODEL // LANES

_PAIRS = ((0, 1), (0, 2), (0, 3), (1, 2), (1, 3), (2, 3))
N_CLASSES = N_EXPERT_GROUPS * len(_PAIRS)
META_CLASS, META_RANK, META_WA, META_WB = 0, 1, 2, 3


def _post_kernel(*refs, tm, dils, full_precision, routed):
    (o0_ref, o1_ref, o2_ref, l0_ref, l1_ref, l2_ref, pc_ref, sga_ref, x_ref, mod_ref,
     wao_ref, wo_ref, g2_ref, wr_ref, wrh_ref, wrl_ref, br_ref, exp_ref) = refs[:18]
    if routed:
        tri_ref, x1_ref, h2_ref, meta_ref, route_ref, cnt_ref, o_scr, l_scr, run_scr = refs[18:]
    else:
        x1_ref, h2_ref, comb_ref, o_scr, l_scr = refs[18:]

    def mm(a, w_ref):
        if full_precision:
            return jnp.dot(a, w_ref[...], precision=HIGHEST, preferred_element_type=F32)
        return jnp.dot(a.astype(BF16), w_ref[...], preferred_element_type=F32)

    def natural_order(ref, scr, dil, n_chunks):
        if dil == 1:
            return [ref[0, :, c * LANES:(c + 1) * LANES] for c in range(n_chunks)]
        out = []
        for c in range(n_chunks):
            for r in range(dil):
                scr[c, pl.ds(r, tm // dil, stride=dil), :] = ref[r, :, c * LANES:(c + 1) * LANES]
            out.append(scr[c])
        return out

    lses = [natural_order(ref, l_scr.at[g], dils[g], 1)[0] for g, ref in enumerate((l0_ref, l1_ref, l2_ref))]
    mx = jnp.maximum(lses[0], jnp.maximum(lses[1], lses[2]))
    es = [jnp.exp(v - mx) for v in lses]
    inv = 1.0 / (es[0] + es[1] + es[2])
    expand = exp_ref[...]
    attn_o = None
    for g, o_ref in enumerate((o0_ref, o1_ref, o2_ref)):
        w = es[g] * inv
        we = None
        for part in _split_bf16(w, 3 if full_precision else 2):
            t = jnp.dot(part, expand, preferred_element_type=F32)
            we = t if we is None else we + t
        o_nat = jnp.concatenate(natural_order(o_ref, o_scr.at[g], dils[g], CHUNKS_PER_GROUP), axis=1)
        term = we * o_nat
        attn_o = term if attn_o is None else attn_o + term

    y_attn = mm(attn_o, wao_ref)
    mixed = mm(pc_ref[...] + sga_ref[...] * y_attn, wo_ref)
    x1 = x_ref[...] + mod_ref[2] * mixed
    x1_ref[...] = x1
    var = jnp.mean(x1 * x1, axis=-1, keepdims=True)
    h2 = (x1 * lax.rsqrt(var + RMS_EPS)) * (g2_ref[...] * (1.0 + mod_ref[4])) + mod_ref[3]
    if routed:
        _store_token_rows(h2_ref, h2, tm)
    else:
        h2_ref[...] = h2.astype(h2_ref.dtype)

    if full_precision:
        lg = jnp.dot(h2, wr_ref[...], precision=HIGHEST, preferred_element_type=F32) + br_ref[...]
    else:
        h_hi, h_lo = _split_bf16(h2, 2)
        lg = (jnp.dot(h_hi, wrh_ref[...], preferred_element_type=F32)
              + jnp.dot(h_lo, wrh_ref[...], preferred_element_type=F32)
              + jnp.dot(h_hi, wrl_ref[...], preferred_element_type=F32)) + br_ref[...]
    lane_i = lax.broadcasted_iota(jnp.int32, (1, LANES), 1)
    lane = lane_i.astype(F32)
    lane_group = ((lane_i - N_EXPERT_GROUPS) >> 2).astype(F32)
    big = jnp.float32(1e9)
    gl = jnp.where(lane_i < N_EXPERT_GROUPS, lg, NEG_BIG)
    gmax = jnp.max(gl, axis=1, keepdims=True)
    gidx = jnp.min(jnp.where(gl == gmax, lane, big), axis=1, keepdims=True)
    g_w = 1.0 / jnp.sum(jnp.exp(gl - gmax), axis=1, keepdims=True)
    el = jnp.where(lane_group == gidx, lg, NEG_BIG)
    v1 = jnp.max(el, axis=1, keepdims=True)
    i1 = jnp.min(jnp.where(el == v1, lane, big), axis=1, keepdims=True)
    el2 = jnp.where(lane == i1, NEG_BIG, el)
    v2 = jnp.max(el2, axis=1, keepdims=True)
    i2 = jnp.min(jnp.where(el2 == v2, lane, big), axis=1, keepdims=True)
    t = jnp.exp(v2 - v1)
    den = 1.0 / (1.0 + t)
    w_top1, w_top2 = g_w * den, g_w * (t * den)
    if not routed:
        comb_ref[...] = jnp.where(lane == i1, w_top1, jnp.where(lane == i2, w_top2, 0.0))
        return

    base = jnp.float32(N_EXPERT_GROUPS) + jnp.float32(EXPERTS_PER_GROUP) * gidx
    e1, e2 = i1 - base, i2 - base
    first_is_lower = e1 < e2
    ea = jnp.where(first_is_lower, e1, e2)
    eb = jnp.where(first_is_lower, e2, e1)
    pair = ea * (7.0 - ea) * 0.5 + (eb - ea - 1.0)
    cls = gidx * jnp.float32(len(_PAIRS)) + pair
    onehot = lane == cls
    earlier = jnp.dot(tri_ref[...], jnp.where(onehot, 1.0, 0.0).astype(BF16), preferred_element_type=F32)

    @pl.when(jnp.logical_and(pl.program_id(0) == 0, pl.program_id(1) == 0))
    def _():
        run_scr[...] = jnp.zeros_like(run_scr)

    running = run_scr[...]
    rank = jnp.sum(jnp.where(onehot, earlier + running, 0.0), axis=1, keepdims=True)
    running = running + jnp.sum(jnp.where(onehot, 1.0, 0.0), axis=0, keepdims=True)
    run_scr[...] = running
    cnt_ref[...] = running
    w_a = jnp.where(first_is_lower, w_top1, w_top2)
    w_b = jnp.where(first_is_lower, w_top2, w_top1)
    meta = jnp.where(lane_i == META_CLASS, cls,
                     jnp.where(lane_i == META_RANK, rank,
                               jnp.where(lane_i == META_WA, w_a, jnp.where(lane_i == META_WB, w_b, 0.0))))
    meta_ref[...] = meta
    route_ref[...] = meta.T[0:SUBLANES, :].astype(jnp.int32)


def _post(o_list, lse_list, pc, sga, x, mod4, wao, wo, g2, wr, br, expand, *, tm, full_precision, routed):
    nb, rows, _ = x.shape
    tiles = rows // tm
    dils = tuple(o.shape[1] for o in o_list)
    tile3 = lambda b, i: (b, i, 0)
    const2 = lambda b, i: (0, 0)
    mod_rows = mod4.shape[2]
    if mod_rows == 1:
        mod_spec = pl.BlockSpec((None, 6, 1, D_MODEL), lambda b, i: (b, 0, 0, 0))
    else:
        mod_spec = pl.BlockSpec((None, 6, tm, D_MODEL), lambda b, i: (b, 0, i, 0))
    cls4 = lambda b, i: (b, 0, i, 0)
    in_specs = (
        [pl.BlockSpec((None, d, tm // d, GROUP_WIDTH), cls4) for d in dils]
        + [pl.BlockSpec((None, d, tm // d, LANES), cls4) for d in dils]
        + [pl.BlockSpec((None, tm, D_MODEL), tile3)] * 3 + [mod_spec]
        + [pl.BlockSpec(wao.shape, const2), pl.BlockSpec(wo.shape, const2), pl.BlockSpec((1, D_MODEL), const2)]
        + [pl.BlockSpec((D_MODEL, LANES), const2)] * 3
        + [pl.BlockSpec((1, LANES), const2), pl.BlockSpec((LANES, GROUP_WIDTH), const2)]
    )
    wr_hi = wr.astype(BF16)
    wr_lo = (wr - wr_hi.astype(F32)).astype(BF16)
    args = [*o_list, *lse_list, pc, sga, x, mod4, wao, wo, g2, wr, wr_hi, wr_lo, br, expand]
    scratch = [pltpu.VMEM((N_DIL, CHUNKS_PER_GROUP, tm, LANES), F32), pltpu.VMEM((N_DIL, 1, tm, LANES), F32)]
    if routed:
        tri = jnp.asarray(np.tril(np.ones((tm, tm), np.float32), -1), dtype=BF16)
        args.append(tri)
        in_specs = in_specs + [pl.BlockSpec((tm, tm), const2)]
        out_specs = [pl.BlockSpec((None, tm, D_MODEL), tile3), pl.BlockSpec((None, tm * H_ROWS, LANES), tile3),
                     pl.BlockSpec((None, tm, LANES), tile3),
                     pl.BlockSpec((None, None, SUBLANES, tm), lambda b, i: (b, i, 0, 0)),
                     pl.BlockSpec((1, LANES), const2)]
        out_shape = [jax.ShapeDtypeStruct((nb, rows, D_MODEL), F32),
                     jax.ShapeDtypeStruct((nb, rows * H_ROWS, LANES), F32),
                     jax.ShapeDtypeStruct((nb, rows, LANES), F32),
                     jax.ShapeDtypeStruct((nb, tiles, SUBLANES, tm), jnp.int32),
                     jax.ShapeDtypeStruct((1, LANES), F32)]
        scratch.append(pltpu.VMEM((1, LANES), F32))
    else:
        out_specs = [pl.BlockSpec((None, tm, D_MODEL), tile3), pl.BlockSpec((None, tm, D_MODEL), tile3),
                     pl.BlockSpec((None, tm, LANES), tile3)]
        out_shape = [jax.ShapeDtypeStruct((nb, rows, D_MODEL), F32), jax.ShapeDtypeStruct((nb, rows, D_MODEL), F32),
                     jax.ShapeDtypeStruct((nb, rows, LANES), F32)]
    return pl.pallas_call(
        functools.partial(_post_kernel, tm=tm, dils=dils, full_precision=full_precision, routed=routed),
        grid=(nb, tiles),
        in_specs=in_specs,
        out_specs=out_specs,
        out_shape=out_shape,
        scratch_shapes=scratch,
        compiler_params=_cparams(("arbitrary", "arbitrary")),
        name="post_sample" if full_precision else "post_prompt",
    )(*args)


DMA_LOOP_UNROLL = 8
DMA_PRIORITIES = 2


ROUTE_CLASS_ROW, ROUTE_RANK_ROW = META_CLASS, META_RANK


def _sorted_slot(starts_ref, route_ref, r):
    return starts_ref[route_ref[ROUTE_CLASS_ROW, r]] + route_ref[ROUTE_RANK_ROW, r]


def _token_rows(token, rows_per_token):
    return pl.ds(pl.multiple_of(token * rows_per_token, rows_per_token), rows_per_token)


def _dispatch_kernel(starts_ref, route_ref, src_ref, dst_hbm, buf, sems, *, rows_per_step, n_steps):
    i = pl.program_id(0)
    slot = i % 2

    def wait_slot(s):
        def body(r, carry):
            pltpu.make_async_copy(buf.at[s, _token_rows(0, H_ROWS)], dst_hbm.at[_token_rows(0, H_ROWS)],
                                  sems.at[s * rows_per_step + r]).wait()
            return carry
        lax.fori_loop(0, rows_per_step, body, 0, unroll=DMA_LOOP_UNROLL)

    @pl.when(i >= 1)
    def _():
        wait_slot(1 - slot)

    buf[slot] = src_ref[...]

    def issue(pair, carry):
        for prio in range(DMA_PRIORITIES):
            r = pair * DMA_PRIORITIES + prio
            pltpu.make_async_copy(buf.at[slot, _token_rows(r, H_ROWS)],
                                  dst_hbm.at[_token_rows(_sorted_slot(starts_ref, route_ref, r), H_ROWS)],
                                  sems.at[slot * rows_per_step + r]).start(priority=prio)
        return carry

    lax.fori_loop(0, rows_per_step // DMA_PRIORITIES, issue, 0, unroll=DMA_LOOP_UNROLL // DMA_PRIORITIES)

    @pl.when(i == n_steps - 1)
    def _():
        wait_slot(slot)


def _route_spec(rows_per_step, route_tile, shift=0, n_steps=None):
    per_tile = route_tile // rows_per_step

    def imap(i, starts):
        s = i + shift if n_steps is None else jnp.minimum(i + shift, n_steps - 1)
        return (s // per_tile, 0, s % per_tile)

    return pl.BlockSpec((None, SUBLANES, rows_per_step), imap, memory_space=pltpu.SMEM)


def _dispatch_rows(src, route, starts, *, rows_per_step):
    n = src.shape[0] // H_ROWS
    n_steps = n // rows_per_step
    grid_spec = pltpu.PrefetchScalarGridSpec(
        num_scalar_prefetch=1,
        grid=(n_steps,),
        in_specs=[_route_spec(rows_per_step, route.shape[2]),
                  pl.BlockSpec((rows_per_step * H_ROWS, LANES), lambda i, starts: (i, 0))],
        out_specs=pl.BlockSpec(memory_space=pl.ANY),
        scratch_shapes=[pltpu.VMEM((2, rows_per_step * H_ROWS, LANES), src.dtype),
                        pltpu.SemaphoreType.DMA((2 * rows_per_step,))],
    )
    return pl.pallas_call(
        functools.partial(_dispatch_kernel, rows_per_step=rows_per_step, n_steps=n_steps),
        grid_spec=grid_spec,
        out_shape=jax.ShapeDtypeStruct(src.shape, src.dtype),
        compiler_params=_cparams(("arbitrary",)),
        name="moe_dispatch",
    )(starts, route, src)


def _experts_kernel(tile_ref, ea_ref, eb_ref, lo_ref, hi_ref, first_ref, switch_ref, nv_ref,
                    x_ref, w1a_ref, w1b_ref, w3a_ref, w3b_ref, w2a_ref, w2b_ref, y_ref, w13_scr, w2_scr, *, tm):
    i = pl.program_id(0)

    @pl.when(i < nv_ref[0])
    def _():
        @pl.when(switch_ref[i] == 1)
        def _():
            for k, w_ref in enumerate((w1a_ref, w3a_ref, w1b_ref, w3b_ref)):
                w13_scr[k] = w_ref[...].astype(BF16)
            for k, w_ref in enumerate((w2a_ref, w2b_ref)):
                w2_scr[k] = w_ref[...].astype(BF16)

        def expert(x, k):
            a = jnp.dot(x, w13_scr[2 * k], preferred_element_type=F32)
            b = jnp.dot(x, w13_scr[2 * k + 1], preferred_element_type=F32)
            hid = (a * _sigmoid(a)) * b
            return jnp.dot(hid.astype(BF16), w2_scr[k], preferred_element_type=F32)

        @pl.when(first_ref[i] == 1)
        def _():
            x = _load_token_rows(x_ref, tm, H_ROWS).astype(BF16)
            _store_token_rows(y_ref, expert(x, 0), tm, first_chunk=0, rows_per_token=Y_ROWS)
            _store_token_rows(y_ref, expert(x, 1), tm, first_chunk=H_ROWS, rows_per_token=Y_ROWS)

        @pl.when(first_ref[i] == 0)
        def _():
            x = _load_token_rows(x_ref, tm, H_ROWS).astype(BF16)
            row = lax.broadcasted_iota(jnp.int32, (tm, 1), 0)
            for k in range(2):
                old = _load_token_rows(y_ref, tm, H_ROWS, first_chunk=k * H_ROWS, rows_per_token=Y_ROWS)
                new = jnp.where(row >= lo_ref[i], jnp.where(row < hi_ref[i], expert(x, k), old), old)
                _store_token_rows(y_ref, new, tm, first_chunk=k * H_ROWS, rows_per_token=Y_ROWS)


def _experts_routed(hs, tables, w1, w3, w2, *, tm):
    n = hs.shape[0] // H_ROWS
    n_items = tables[0].shape[0]
    x_map = lambda i, tile, ea, eb, lo, hi, first, switch, nv: (tile[i], 0)
    wa_map = lambda i, tile, ea, eb, lo, hi, first, switch, nv: (ea[i], 0, 0)
    wb_map = lambda i, tile, ea, eb, lo, hi, first, switch, nv: (eb[i], 0, 0)
    w13 = lambda m: pl.BlockSpec((None, D_MODEL, D_EXPERT), m)
    w2s = lambda m: pl.BlockSpec((None, D_EXPERT, D_MODEL), m)
    grid_spec = pltpu.PrefetchScalarGridSpec(
        num_scalar_prefetch=len(tables),
        grid=(n_items,),
        in_specs=[pl.BlockSpec((tm * H_ROWS, LANES), x_map),
                  w13(wa_map), w13(wb_map), w13(wa_map), w13(wb_map), w2s(wa_map), w2s(wb_map)],
        out_specs=pl.BlockSpec((tm * Y_ROWS, LANES), x_map),
        scratch_shapes=[pltpu.VMEM((4, D_MODEL, D_EXPERT), BF16), pltpu.VMEM((2, D_EXPERT, D_MODEL), BF16)],
    )
    return pl.pallas_call(
        functools.partial(_experts_kernel, tm=tm),
        grid_spec=grid_spec,
        out_shape=jax.ShapeDtypeStruct((n * Y_ROWS, LANES), F32),
        compiler_params=_cparams(("arbitrary",)),
        name="moe_experts",
    )(*tables, hs, w1, w1, w3, w3, w2, w2)


def _combine_kernel(starts_ref, route_ref, route_next_ref, yp_hbm, x1_ref, meta_ref, mod_ref, gf_ref, y_ref, ybuf, sems,
                    *, rows_per_step, n_steps):
    i = pl.program_id(0)
    slot = i % 2

    def issue(r_ref, s):
        def body(pair, carry):
            for prio in range(DMA_PRIORITIES):
                r = pair * DMA_PRIORITIES + prio
                pltpu.make_async_copy(yp_hbm.at[_token_rows(_sorted_slot(starts_ref, r_ref, r), Y_ROWS)],
                                      ybuf.at[s, :, pl.ds(r, 1), :],
                                      sems.at[s * rows_per_step + r]).start(priority=prio)
            return carry
        lax.fori_loop(0, rows_per_step // DMA_PRIORITIES, body, 0, unroll=DMA_LOOP_UNROLL // DMA_PRIORITIES)

    @pl.when(i == 0)
    def _():
        issue(route_ref, 0)

    @pl.when(i + 1 < n_steps)
    def _():
        issue(route_next_ref, 1 - slot)

    def wait_body(r, carry):
        pltpu.make_async_copy(yp_hbm.at[_token_rows(0, Y_ROWS)], ybuf.at[slot, :, pl.ds(r, 1), :],
                              sems.at[slot * rows_per_step + r]).wait()
        return carry

    lax.fori_loop(0, rows_per_step, wait_body, 0, unroll=DMA_LOOP_UNROLL)
    ya = jnp.concatenate([ybuf[slot, c] for c in range(H_ROWS)], axis=1)
    yb = jnp.concatenate([ybuf[slot, H_ROWS + c] for c in range(H_ROWS)], axis=1)
    meta = meta_ref[...]
    lane = lax.broadcasted_iota(jnp.int32, (1, LANES), 1)
    w_a = jnp.sum(jnp.where(lane == META_WA, meta, 0.0), axis=1, keepdims=True)
    w_b = jnp.sum(jnp.where(lane == META_WB, meta, 0.0), axis=1, keepdims=True)
    x2 = x1_ref[...] + mod_ref[5] * (w_a * ya + w_b * yb)
    var = jnp.mean(x2 * x2, axis=-1, keepdims=True)
    y_ref[...] = (x2 * lax.rsqrt(var + RMS_EPS)) * gf_ref[...]


def _combine_final(yp, route, starts, x1, meta, mod4, gf, *, rows_per_step, rows_per_mod):
    n = x1.shape[0]
    n_steps = n // rows_per_step
    steps_per_mod = rows_per_mod // rows_per_step
    row = lambda i, starts: (i, 0)
    grid_spec = pltpu.PrefetchScalarGridSpec(
        num_scalar_prefetch=1,
        grid=(n_steps,),
        in_specs=[_route_spec(rows_per_step, route.shape[2]),
                  _route_spec(rows_per_step, route.shape[2], shift=1, n_steps=n_steps),
                  pl.BlockSpec(memory_space=pl.ANY),
                  pl.BlockSpec((rows_per_step, D_MODEL), row),
                  pl.BlockSpec((rows_per_step, LANES), row),
                  pl.BlockSpec((None, 6, 1, D_MODEL), lambda i, starts: (i // steps_per_mod, 0, 0, 0)),
                  pl.BlockSpec((1, D_MODEL), lambda i, starts: (0, 0))],
        out_specs=pl.BlockSpec((rows_per_step, D_MODEL), row),
        scratch_shapes=[pltpu.VMEM((2, Y_ROWS, rows_per_step, LANES), F32),
                        pltpu.SemaphoreType.DMA((2 * rows_per_step,))],
    )
    yp_rows = yp.reshape(yp.shape[0], 1, LANES)
    return pl.pallas_call(
        functools.partial(_combine_kernel, rows_per_step=rows_per_step, n_steps=n_steps),
        grid_spec=grid_spec,
        out_shape=jax.ShapeDtypeStruct((n, D_MODEL), F32),
        compiler_params=_cparams(("arbitrary",)),
        name="moe_combine",
    )(starts, route, route, yp_rows, x1, meta, mod4, gf)


def _routing_tables(counts, *, n, tm):
    counts = counts.astype(jnp.int32)
    ends = jnp.cumsum(counts)
    starts = ends - counts
    n_items = n // tm + N_CLASSES
    first_tile = starts // tm
    last_tile = jnp.maximum(ends - 1, starts) // tm
    visits = jnp.where(counts > 0, last_tile - first_tile + 1, 0)
    item_end = jnp.cumsum(visits)
    item_start = item_end - visits
    n_valid = item_end[-1]
    item = jnp.arange(n_items, dtype=jnp.int32)
    idx = jnp.minimum(item, n_valid - 1)
    c = jnp.sum((idx[:, None] >= item_end[None, :]).astype(jnp.int32), axis=1)
    class_ids = jnp.arange(N_CLASSES, dtype=jnp.int32)

    def pick(table):
        return jnp.sum(jnp.where(c[:, None] == class_ids[None, :], table[None, :], 0), axis=1)

    tile = pick(first_tile) + (idx - pick(item_start))
    live = item < n_valid
    lo = jnp.where(live, jnp.clip(pick(starts) - tile * tm, 0, tm), 0)
    hi = jnp.where(live, jnp.clip(pick(ends) - tile * tm, 0, tm), 0)
    prev_tile = jnp.concatenate([jnp.full((1,), -1, jnp.int32), tile[:-1]])
    first = (tile != prev_tile).astype(jnp.int32)
    prev_c = jnp.concatenate([jnp.full((1,), -1, jnp.int32), c[:-1]])
    switch = (c != prev_c).astype(jnp.int32)
    group_of_class = np.arange(N_CLASSES) // len(_PAIRS)
    pair_of_class = np.arange(N_CLASSES) % len(_PAIRS)
    pairs = np.asarray(_PAIRS)
    ea = pick(jnp.asarray(group_of_class * EXPERTS_PER_GROUP + pairs[pair_of_class, 0], dtype=jnp.int32))
    eb = pick(jnp.asarray(group_of_class * EXPERTS_PER_GROUP + pairs[pair_of_class, 1], dtype=jnp.int32))
    as_i32 = lambda a: a.astype(jnp.int32)
    return as_i32(starts), tuple(as_i32(a) for a in (tile, ea, eb, lo, hi, first, switch, n_valid.reshape(1)))


def _moe_kernel(h_ref, comb_ref, x1_ref, mod_ref, gf_ref, w1_ref, w3_ref, w2_ref, y_ref, acc_ref):
    e = pl.program_id(2)

    @pl.when(e == 0)
    def _():
        acc_ref[...] = jnp.zeros_like(acc_ref)

    h = h_ref[...].astype(BF16)
    a = jnp.dot(h, w1_ref[...].astype(BF16), preferred_element_type=F32)
    b = jnp.dot(h, w3_ref[...].astype(BF16), preferred_element_type=F32)
    lane = lax.broadcasted_iota(jnp.int32, (1, LANES), 1)
    cw = jnp.sum(jnp.where(lane == e + N_EXPERT_GROUPS, comb_ref[...], 0.0), axis=1, keepdims=True)
    hid = (a * _sigmoid(a)) * b * cw
    acc_ref[...] += jnp.dot(hid.astype(BF16), w2_ref[...].astype(BF16), preferred_element_type=F32)

    @pl.when(e == N_EXPERTS - 1)
    def _():
        x2 = x1_ref[...] + mod_ref[5] * acc_ref[...]
        var = jnp.mean(x2 * x2, axis=-1, keepdims=True)
        y_ref[...] = (x2 * lax.rsqrt(var + RMS_EPS)) * gf_ref[...]


def _moe_dense(h2, comb, x1, mod4, gf, w1_bf, w3_bf, w2_bf, *, tm):
    nb, rows, _ = x1.shape
    tiles = rows // tm
    tile3 = lambda b, i, e: (b, i, 0)
    if mod4.shape[2] == 1:
        mod_spec = pl.BlockSpec((None, 6, 1, D_MODEL), lambda b, i, e: (b, 0, 0, 0))
    else:
        mod_spec = pl.BlockSpec((None, 6, tm, D_MODEL), lambda b, i, e: (b, 0, i, 0))
    return pl.pallas_call(
        _moe_kernel,
        grid=(nb, tiles, N_EXPERTS),
        in_specs=[
            pl.BlockSpec((None, tm, D_MODEL), tile3),
            pl.BlockSpec((None, tm, LANES), tile3),
            pl.BlockSpec((None, tm, D_MODEL), tile3),
            mod_spec,
            pl.BlockSpec((1, D_MODEL), lambda b, i, e: (0, 0)),
            pl.BlockSpec((None, D_MODEL, D_EXPERT), lambda b, i, e: (e, 0, 0)),
            pl.BlockSpec((None, D_MODEL, D_EXPERT), lambda b, i, e: (e, 0, 0)),
            pl.BlockSpec((None, D_EXPERT, D_MODEL), lambda b, i, e: (e, 0, 0)),
        ],
        out_specs=pl.BlockSpec((None, tm, D_MODEL), tile3),
        out_shape=jax.ShapeDtypeStruct((nb, rows, D_MODEL), F32),
        scratch_shapes=[pltpu.VMEM((tm, D_MODEL), F32)],
        compiler_params=_cparams(("arbitrary", "arbitrary", "arbitrary")),
        name=f"moe_dense_{nb * rows}",
    )(h2, comb, x1, mod4, gf, w1_bf, w3_bf, w2_bf)


def _s_inproj_kernel(x_ref, mod_ref, g1_ref, w_ref, z_ref):
    x = x_ref[...]
    var = jnp.mean(x * x, axis=-1, keepdims=True)
    h = (x * lax.rsqrt(var + RMS_EPS)) * (g1_ref[...] * (1.0 + mod_ref[1])) + mod_ref[0]
    h_hi, h_lo = _split_bf16(h, 2)
    w_hi, w_lo = _split_bf16(w_ref[...], 2)
    z_ref[...] = (jnp.dot(h_hi, w_hi, preferred_element_type=F32) + jnp.dot(h_lo, w_hi, preferred_element_type=F32)
                  + jnp.dot(h_hi, w_lo, preferred_element_type=F32))


def _s_inproj(x, mod_tok, g1, w_in, *, col_block):
    n = x.shape[0]
    return pl.pallas_call(
        _s_inproj_kernel,
        grid=(IN_COLS // col_block,),
        in_specs=[pl.BlockSpec((n, D_MODEL), lambda j: (0, 0)),
                  pl.BlockSpec((6, n, D_MODEL), lambda j: (0, 0, 0)),
                  pl.BlockSpec((1, D_MODEL), lambda j: (0, 0)),
                  pl.BlockSpec((D_MODEL, col_block), lambda j: (0, j))],
        out_specs=pl.BlockSpec((n, col_block), lambda j: (0, j)),
        out_shape=jax.ShapeDtypeStruct((n, IN_COLS), F32),
        compiler_params=_cparams(("arbitrary",)),
        name="inproj_sample",
    )(x, mod_tok, g1, w_in)


def _s_mid_kernel(z_ref, p0_ref, p1_ref, cw_ref, wco_ref, rq_ref, rk_ref,
                  cu_ref, pc_ref, sga_ref, q_ref, k_ref, v_ref, *, t_len):
    n = z_ref.shape[0]
    cu = z_ref[:, OFF_GC:OFF_GC + D_MODEL] * z_ref[:, OFF_U:OFF_U + D_MODEL]
    cu_ref[...] = cu
    t = lax.broadcasted_iota(jnp.int32, (n, D_MODEL), 0) & (t_len - 1)
    prev1 = jnp.where(t >= 1, pltpu.roll(cu, 1, 0), p1_ref[...])
    prev2 = jnp.where(t >= 2, pltpu.roll(cu, 2, 0), jnp.where(t == 0, p0_ref[...], p1_ref[...]))
    cw = cw_ref[...]
    conv = cw[0:1] * prev2 + cw[1:2] * prev1 + cw[2:3] * cu
    yc = jnp.dot(z_ref[:, OFF_GB:OFF_GB + D_MODEL] * conv, wco_ref[...], precision=HIGHEST,
                 preferred_element_type=F32)
    pc_ref[...] = _sigmoid(z_ref[:, OFF_GCONV:OFF_GCONV + D_MODEL]) * yc
    sga_ref[...] = _sigmoid(z_ref[:, OFF_GATTN:OFF_GATTN + D_MODEL])
    aq, bmq, bpq = rq_ref[0], rq_ref[1], rq_ref[2]
    ak, bmk, bpk = rk_ref[0], rk_ref[1], rk_ref[2]
    for c in range(ATTN_WIDTH // LANES):
        sl = slice(c * LANES, (c + 1) * LANES)
        q_ref[:, sl] = _rope_chunk(z_ref[:, OFF_Q + c * LANES:OFF_Q + (c + 1) * LANES], aq, bmq, bpq)
        k_ref[:, sl] = _rope_chunk(z_ref[:, OFF_K + c * LANES:OFF_K + (c + 1) * LANES], ak, bmk, bpk)
    v_ref[...] = z_ref[:, OFF_V:OFF_V + ATTN_WIDTH]


def _s_mid(z, p0e, p1e, conv_w, w_conv_out, rope_q, rope_k, *, t_len):
    n = z.shape[0]
    assert t_len & (t_len - 1) == 0
    full = lambda shape: pl.BlockSpec(shape, lambda i: (0,) * len(shape))
    out_shape = [jax.ShapeDtypeStruct((n, D_MODEL), F32)] * 3 + [jax.ShapeDtypeStruct((n, ATTN_WIDTH), F32)] * 3
    return pl.pallas_call(
        functools.partial(_s_mid_kernel, t_len=t_len),
        grid=(1,),
        in_specs=[full(z.shape), full(p0e.shape), full(p1e.shape), full(conv_w.shape), full(w_conv_out.shape),
                  full(rope_q.shape), full(rope_k.shape)],
        out_specs=[full((n, D_MODEL))] * 3 + [full((n, ATTN_WIDTH))] * 3,
        out_shape=out_shape,
        compiler_params=_cparams(("arbitrary",)),
        name="mid_sample",
    )(z, p0e, p1e, conv_w, w_conv_out, rope_q, rope_k)


def _head_sum(x):
    return jnp.sum(x.reshape(HEADS_PER_GROUP, HEAD_DIM, x.shape[-1]), axis=1)


def _head_expand(x):
    n = x.shape[-1]
    return jnp.broadcast_to(x[:, None, :], (HEADS_PER_GROUP, HEAD_DIM, n)).reshape(GROUP_WIDTH, n)


def _s_attn_disjoint(q_ref, kn, vn, ck_ref, cv_ref, o_ref, lse_ref, s_scr, lane, new_idx, *, n_tiles, dil, t_len):
    cls = lane & (dil - 1)
    q_all = q_ref[...]
    qsel = jnp.zeros((GROUP_WIDTH, LANES), F32)
    for t in range(t_len):
        qsel = jnp.where(cls == t, jnp.broadcast_to(q_all[:, t:t + 1], (GROUP_WIDTH, LANES)), qsel)
    smax = None
    for j in range(n_tiles):
        s = jnp.where(cls < t_len, _head_sum(ck_ref[:, j * LANES:(j + 1) * LANES] * qsel), NEG_BIG)
        s_scr[:, j * LANES:(j + 1) * LANES] = s
        smax = s if smax is None else jnp.maximum(smax, s)
    s_new = _head_sum(kn * q_all)
    m_cols = []
    m_lane = jnp.zeros((HEADS_PER_GROUP, LANES), F32)
    m_new = jnp.zeros((HEADS_PER_GROUP, t_len), F32)
    for t in range(t_len):
        mt = jnp.maximum(jnp.max(jnp.where(cls == t, smax, NEG_BIG), axis=1, keepdims=True), s_new[:, t:t + 1])
        m_cols.append(mt)
        m_lane = jnp.where(cls == t, mt, m_lane)
        m_new = jnp.where(new_idx == t, mt, m_new)
    esum = jnp.zeros((HEADS_PER_GROUP, LANES), F32)
    for j in range(n_tiles):
        e = jnp.exp(s_scr[:, j * LANES:(j + 1) * LANES] - m_lane)
        s_scr[:, j * LANES:(j + 1) * LANES] = e
        esum = esum + e
    e_new = jnp.exp(s_new - m_new)
    l_cols = []
    inv_lane = jnp.zeros((HEADS_PER_GROUP, LANES), F32)
    inv_new = jnp.zeros((HEADS_PER_GROUP, t_len), F32)
    for t in range(t_len):
        lt = jnp.sum(jnp.where(cls == t, esum, 0.0), axis=1, keepdims=True) + e_new[:, t:t + 1]
        l_cols.append(lt)
        inv_lane = jnp.where(cls == t, 1.0 / lt, inv_lane)
        inv_new = jnp.where(new_idx == t, 1.0 / lt, inv_new)
    acc = None
    for j in range(n_tiles):
        term = cv_ref[:, j * LANES:(j + 1) * LANES] * _head_expand(s_scr[:, j * LANES:(j + 1) * LANES] * inv_lane)
        acc = term if acc is None else acc + term
    o_new = vn * _head_expand(e_new * inv_new)
    for t in range(t_len):
        o_ref[:, t:t + 1] = jnp.sum(jnp.where(cls == t, acc, 0.0), axis=1, keepdims=True) + o_new[:, t:t + 1]
        lse_ref[:, t:t + 1] = m_cols[t] + jnp.log(l_cols[t])


def _s_attn_kernel(q_ref, kn_ref, vn_ref, ck_ref, cv_ref, o_ref, lse_ref, ko_ref, vo_ref, s_scr,
                   *, win, dil, t_len):
    n_tiles = win // LANES
    lane = lax.broadcasted_iota(jnp.int32, (1, LANES), 1)
    new_idx = lax.broadcasted_iota(jnp.int32, (1, t_len), 1)
    kn = kn_ref[...]
    vn = vn_ref[...]
    if dil >= t_len:
        _s_attn_disjoint(q_ref, kn, vn, ck_ref, cv_ref, o_ref, lse_ref, s_scr, lane, new_idx,
                         n_tiles=n_tiles, dil=dil, t_len=t_len)
    for t in range(t_len if dil < t_len else 0):
        qb = jnp.broadcast_to(q_ref[:, t:t + 1], (GROUP_WIDTH, LANES))
        m = None
        for j in range(n_tiles):
            pos = lane + j * LANES
            s = _head_sum(ck_ref[:, j * LANES:(j + 1) * LANES] * qb)
            s = jnp.where(pos >= t, jnp.where(((pos - t) & (dil - 1)) == 0, s, NEG_BIG), NEG_BIG)
            s_scr[:, j * LANES:(j + 1) * LANES] = s
            mj = jnp.max(s, axis=1, keepdims=True)
            m = mj if m is None else jnp.maximum(m, mj)
        s_new = _head_sum(kn * qb[:, 0:t_len])
        s_new = jnp.where(new_idx <= t, jnp.where(((t - new_idx) & (dil - 1)) == 0, s_new, NEG_BIG), NEG_BIG)
        m = jnp.maximum(m, jnp.max(s_new, axis=1, keepdims=True))
        e_new = jnp.exp(s_new - m)
        l = jnp.sum(e_new, axis=1, keepdims=True)
        acc = None
        for j in range(n_tiles):
            e = jnp.exp(s_scr[:, j * LANES:(j + 1) * LANES] - m)
            l = l + jnp.sum(e, axis=1, keepdims=True)
            term = cv_ref[:, j * LANES:(j + 1) * LANES] * _head_expand(e)
            acc = term if acc is None else acc + term
        o = jnp.sum(acc, axis=1, keepdims=True) + jnp.sum(vn * _head_expand(e_new), axis=1, keepdims=True)
        o_ref[:, t:t + 1] = o * _head_expand(1.0 / l)
        lse_ref[:, t:t + 1] = m + jnp.log(l)

    for c_ref, new, out_ref in ((ck_ref, kn, ko_ref), (cv_ref, vn, vo_ref)):
        out_ref[...] = pltpu.roll(c_ref[...], win - t_len, 1)
        out_ref[:, win - t_len:win] = new


def _s_attn_group(q_t, kn_t, vn_t, cache_k, cache_v, g):
    bsz, _, t_len = q_t.shape
    win_full, dil = DIL_GROUPS[g]
    win = cache_k.shape[2]
    assert win == win_full and win == (KEYS_PER_QUERY - 1) * dil and win % LANES == 0
    assert dil & (dil - 1) == 0 and LANES % dil == 0
    grp = lambda b: (b, g, 0)
    per_b = lambda b: (b, 0, 0)
    return pl.pallas_call(
        functools.partial(_s_attn_kernel, win=win, dil=dil, t_len=t_len),
        grid=(bsz,),
        in_specs=[pl.BlockSpec((None, GROUP_WIDTH, t_len), grp)] * 3
        + [pl.BlockSpec((None, GROUP_WIDTH, win), per_b)] * 2,
        out_specs=[pl.BlockSpec((None, GROUP_WIDTH, t_len), per_b),
                   pl.BlockSpec((None, HEADS_PER_GROUP, t_len), per_b),
                   pl.BlockSpec((None, GROUP_WIDTH, win), per_b),
                   pl.BlockSpec((None, GROUP_WIDTH, win), per_b)],
        out_shape=[jax.ShapeDtypeStruct((bsz, GROUP_WIDTH, t_len), F32),
                   jax.ShapeDtypeStruct((bsz, HEADS_PER_GROUP, t_len), F32),
                   jax.ShapeDtypeStruct((bsz, GROUP_WIDTH, win), F32),
                   jax.ShapeDtypeStruct((bsz, GROUP_WIDTH, win), F32)],
        scratch_shapes=[pltpu.VMEM((HEADS_PER_GROUP, win), F32)],
        compiler_params=_cparams(("arbitrary",)),
        name=f"attn_sample_g{g}",
    )(q_t, kn_t, vn_t, cache_k, cache_v)


TM_INPROJ = 512
TM_POST = 512
TM_EXPERT = 256
ROWS_PER_DMA_STEP = 256
ATTN_ROWS_PER_STEP = 1024
S_COL_BLOCK = 512


def _to_state(a_t):
    b, _, length = a_t.shape
    return jnp.transpose(a_t.reshape(b, HEADS_PER_GROUP, HEAD_DIM, length), (0, 3, 1, 2))[None]


def _from_state(a):
    b, length = a.shape[0], a.shape[1]
    return jnp.transpose(a, (0, 2, 3, 1)).reshape(b, GROUP_WIDTH, length)


def kernel(x_prompt, x_sample, cache_k1, cache_v1, cache_k2, cache_v2, cache_k3, cache_v3, state_conv,
           c_prompt, c_sample, norm1_g, norm2_g, normf_g, w_ada, b_ada, w_in, conv_w, w_conv_out,
           w_attn_out, w_o, w_rg, b_rg, w_re, b_re, w1, w3, w2):
    depth = w_in.shape[0]
    assert depth == 1
    bsz, seq, _ = x_prompt.shape
    dbsz, t_len, _ = x_sample.shape
    n_s = dbsz * t_len
    l = 0

    w_a_bf = w_in[l][:, :OFF_Q].astype(BF16)
    w_g_bf = w_in[l][:, OFF_GCONV:].astype(BF16)
    w_qkv_bf = w_in[l][:, OFF_Q:OFF_GCONV].astype(BF16)
    wco_bf = w_conv_out[l].astype(BF16)
    wao_bf = w_attn_out[l].astype(BF16)
    wo_bf = w_o[l].astype(BF16)
    w1_e = w1[l].reshape(N_EXPERTS, D_MODEL, D_EXPERT)
    w3_e = w3[l].reshape(N_EXPERTS, D_MODEL, D_EXPERT)
    w2_e = w2[l].reshape(N_EXPERTS, D_EXPERT, D_MODEL)
    g1 = norm1_g[l].reshape(1, D_MODEL)
    g2 = norm2_g[l].reshape(1, D_MODEL)
    gf = normf_g.reshape(1, D_MODEL)
    n_route = N_EXPERT_GROUPS + N_EXPERTS
    wr = jnp.pad(jnp.concatenate([w_rg[l], w_re[l]], axis=1), ((0, 0), (0, LANES - n_route)))
    br = jnp.pad(jnp.concatenate([b_rg[l], b_re[l]]), (0, LANES - n_route)).reshape(1, LANES)
    head_of_lane = np.arange(GROUP_WIDTH) // HEAD_DIM
    expand_bf = jnp.asarray((np.arange(LANES)[:, None] == head_of_lane[None, :]).astype(np.float32), dtype=BF16)

    mod = _adaln(jnp.concatenate([c_prompt, c_sample], axis=0), w_ada[l], b_ada[l])
    mod_p4 = mod[:bsz].reshape(bsz, 6, 1, D_MODEL)
    mod_tok = jnp.repeat(mod[bsz:].reshape(dbsz, 1, 6, D_MODEL), t_len, axis=1)
    mod_tok = jnp.transpose(mod_tok.reshape(n_s, 6, D_MODEL), (1, 0, 2))
    mod_s4 = mod_tok[None]

    pos_p = jnp.arange(seq, dtype=jnp.int32)
    rope_q_p = _rope_tables(pos_p, HEAD_DIM ** -0.5)
    rope_k_p = _rope_tables(pos_p, 1.0)
    pc_p, sga_p, conv_p = _convproj_prompt(x_prompt, mod_p4, g1, w_a_bf, w_g_bf,
                                           conv_w[l], wco_bf, tm=TM_INPROJ)
    outs = _qkvproj_prompt(x_prompt, mod_p4, g1, w_qkv_bf, rope_q_p, rope_k_p, tm=TM_INPROJ)
    qkv_p = outs[0:3 * N_DIL]
    states_p = outs[3 * N_DIL:]
    o_p, lse_p = [], []
    for g in range(N_DIL):
        o_g, lse_g = _attn_prompt_group(qkv_p[3 * g], qkv_p[3 * g + 1], qkv_p[3 * g + 2], g, max_rows=ATTN_ROWS_PER_STEP)
        o_p.append(o_g)
        lse_p.append(lse_g)
    x1_p, h2rows_p, meta_p, route_p, cnt_p = _post(o_p, lse_p, pc_p, sga_p, x_prompt, mod_p4, wao_bf, wo_bf, g2, wr, br,
                                                   expand_bf, tm=TM_POST, full_precision=False, routed=True)
    n_p = bsz * seq
    route = route_p.reshape(n_p // TM_POST, SUBLANES, TM_POST)
    starts, tables = _routing_tables(cnt_p[0, :N_CLASSES], n=n_p, tm=TM_EXPERT)
    hs = _dispatch_rows(h2rows_p.reshape(n_p * H_ROWS, LANES), route, starts, rows_per_step=ROWS_PER_DMA_STEP)
    yp = _experts_routed(hs, tables, w1_e, w3_e, w2_e, tm=TM_EXPERT)
    y_p = _combine_final(yp, route, starts, x1_p.reshape(n_p, D_MODEL), meta_p.reshape(n_p, LANES), mod_p4, gf,
                         rows_per_step=ROWS_PER_DMA_STEP, rows_per_mod=seq).reshape(bsz, seq, D_MODEL)

    pos_s = PAST_LEN + jnp.arange(t_len, dtype=jnp.int32)
    rope_q_s = jnp.tile(_rope_tables(pos_s, HEAD_DIM ** -0.5), (1, dbsz, 1))
    rope_k_s = jnp.tile(_rope_tables(pos_s, 1.0), (1, dbsz, 1))
    xs = x_sample.reshape(n_s, D_MODEL)
    z_s = _s_inproj(xs, mod_tok, g1, w_in[l], col_block=S_COL_BLOCK)
    past = state_conv[l]
    p0e = jnp.repeat(past[:, 0], t_len, axis=0)
    p1e = jnp.repeat(past[:, 1], t_len, axis=0)
    cu_s, pc_s, sga_s, q_s, k_s, v_s = _s_mid(z_s, p0e, p1e, conv_w[l], w_conv_out[l], rope_q_s, rope_k_s,
                                              t_len=t_len)
    to_cols = lambda a: jnp.transpose(a.reshape(dbsz, t_len, ATTN_WIDTH), (0, 2, 1))
    q_t, kn_t, vn_t = to_cols(q_s), to_cols(k_s), to_cols(v_s)
    caches = ((cache_k1, cache_v1), (cache_k2, cache_v2), (cache_k3, cache_v3))
    o_s, lse_s, kv_s = [], [], []
    for g, (ck, cv) in enumerate(caches):
        o_g, lse_g, ko, vo = _s_attn_group(q_t, kn_t, vn_t, _from_state(ck[l]), _from_state(cv[l]), g)
        o_s.append(jnp.transpose(o_g, (0, 2, 1)).reshape(1, 1, n_s, GROUP_WIDTH))
        lse_rows = jnp.transpose(lse_g, (0, 2, 1)).reshape(n_s, HEADS_PER_GROUP)
        lse_s.append(jnp.pad(lse_rows, ((0, 0), (0, LANES - HEADS_PER_GROUP))).reshape(1, 1, n_s, LANES))
        kv_s += [_to_state(ko), _to_state(vo)]
    x1_s, h2_s, comb_s = _post(o_s, lse_s, pc_s[None], sga_s[None], xs[None], mod_s4, w_attn_out[l], w_o[l], g2, wr, br,
                               expand_bf, tm=n_s, full_precision=True, routed=False)
    y_s = _moe_dense(h2_s, comb_s, x1_s, mod_s4, gf, w1_e, w3_e, w2_e, tm=n_s)

    conv_s = cu_s.reshape(dbsz, t_len, D_MODEL)[:, t_len - (CONV_K - 1):]
    return (y_p, y_s.reshape(dbsz, t_len, D_MODEL),
            *[_to_state(a) for a in states_p], conv_p.reshape(1, bsz, CONV_K - 1, D_MODEL),
            *kv_s, conv_s.reshape(1, dbsz, CONV_K - 1, D_MODEL))
```

```python
import functools

import numpy as np
import jax
import jax.numpy as jnp
from jax import lax
from jax.experimental import pallas as pl
from jax.experimental.pallas import tpu as pltpu

F32 = jnp.float32
BF16 = jnp.bfloat16
HIGHEST = lax.Precision.HIGHEST

D_MODEL = 1024
HEAD_DIM = 64
HEADS_PER_GROUP = 8
GROUP_WIDTH = HEADS_PER_GROUP * HEAD_DIM
DIL_GROUPS = ((128, 1), (512, 4), (2048, 16))
N_DIL = len(DIL_GROUPS)
ATTN_WIDTH = N_DIL * GROUP_WIDTH
ROT_DIM = HEAD_DIM // 4
ROPE_THETA = 500000.0
PAST_LEN = 16384
CONV_K = 3
N_EXPERT_GROUPS = 4
EXPERTS_PER_GROUP = 4
N_EXPERTS = N_EXPERT_GROUPS * EXPERTS_PER_GROUP
D_EXPERT = 512
RMS_EPS = 1e-6
IN_COLS = 3 * D_MODEL + 3 * ATTN_WIDTH + 2 * D_MODEL
OFF_U, OFF_GC, OFF_GB = 0, D_MODEL, 2 * D_MODEL
OFF_Q = 3 * D_MODEL
OFF_K = OFF_Q + ATTN_WIDTH
OFF_V = OFF_K + ATTN_WIDTH
OFF_GCONV = OFF_V + ATTN_WIDTH
OFF_GATTN = OFF_GCONV + D_MODEL

LANES = 128
SUBLANES = 8
CHUNKS_PER_GROUP = GROUP_WIDTH // LANES
KEYS_PER_QUERY = 129
Q_BLOCK = 128
NEG_BIG = -1e30

VMEM_LIMIT = 56 * 1024 * 1024


def _sigmoid(x):
    return 1.0 / (1.0 + jnp.exp(-x))


def _cparams(sem):
    return pltpu.CompilerParams(dimension_semantics=sem, vmem_limit_bytes=VMEM_LIMIT)


def _adaln_kernel(c_ref, w_ref, b_ref, o_ref):
    c = c_ref[...]
    s = c * _sigmoid(c)
    o_ref[...] = jnp.dot(s, w_ref[...], precision=HIGHEST, preferred_element_type=F32) + b_ref[...]


def _adaln(c_all, w_ada, b_ada):
    rows = c_all.shape[0]
    n_col = w_ada.shape[1] // D_MODEL
    return pl.pallas_call(
        _adaln_kernel,
        grid=(n_col,),
        in_specs=[
            pl.BlockSpec((rows, D_MODEL), lambda j: (0, 0)),
            pl.BlockSpec((D_MODEL, D_MODEL), lambda j: (0, j)),
            pl.BlockSpec((1, D_MODEL), lambda j: (0, j)),
        ],
        out_specs=pl.BlockSpec((rows, D_MODEL), lambda j: (0, j)),
        out_shape=jax.ShapeDtypeStruct((rows, w_ada.shape[1]), F32),
        compiler_params=_cparams(("arbitrary",)),
        name="adaln",
    )(c_all, w_ada, b_ada.reshape(1, -1))


def _rope_tables(pos, scale):
    half = ROT_DIM // 2
    inv_freq = jnp.power(jnp.float32(ROPE_THETA), -jnp.arange(half, dtype=F32) / half)
    ang = pos.astype(F32)[:, None] * inv_freq[None, :]
    cos, sin = jnp.cos(ang), jnp.sin(ang)
    lane_in_head = np.arange(LANES) % HEAD_DIM
    freq = lane_in_head % half
    first = lane_in_head < half
    second = (lane_in_head >= half) & (lane_in_head < ROT_DIM)
    a = jnp.where(first | second, cos[:, freq], 1.0)
    bm = jnp.where(first, -sin[:, freq], 0.0)
    bp = jnp.where(second, sin[:, freq], 0.0)
    return jnp.stack([a, bm, bp]) * scale


def _rope_chunk(zc, a, bm, bp):
    return zc * a + pltpu.roll(zc, LANES - ROT_DIM // 2, 1) * bm + pltpu.roll(zc, ROT_DIM // 2, 1) * bp


def _modulated_norm_bf16(x_ref, mod_ref, g1_ref):
    x = x_ref[...]
    var = jnp.mean(x * x, axis=-1, keepdims=True)
    h = (x * lax.rsqrt(var + RMS_EPS)) * (g1_ref[...] * (1.0 + mod_ref[1])) + mod_ref[0]
    return h.astype(BF16)


def _convproj_kernel(x_ref, mod_ref, g1_ref, wa_ref, wg_ref, cw_ref, wco_ref, pc_ref, sga_ref, cst_ref, s_ref,
                     *, tm, n_tiles):
    i = pl.program_id(1)

    @pl.when(i == 0)
    def _():
        s_ref[0:SUBLANES, :] = jnp.zeros((SUBLANES, D_MODEL), F32)

    hb = _modulated_norm_bf16(x_ref, mod_ref, g1_ref)

    def proj(w_ref, lo):
        return jnp.dot(hb, w_ref[:, lo:lo + D_MODEL], preferred_element_type=F32)

    cu = proj(wa_ref, OFF_GC) * proj(wa_ref, OFF_U)
    s_ref[SUBLANES:SUBLANES + tm, :] = cu
    cw = cw_ref[...]
    conv = (cw[0:1] * s_ref[SUBLANES - 2:SUBLANES - 2 + tm, :]
            + cw[1:2] * s_ref[SUBLANES - 1:SUBLANES - 1 + tm, :]
            + cw[2:3] * cu)
    yc = jnp.dot((proj(wa_ref, OFF_GB) * conv).astype(BF16), wco_ref[...], preferred_element_type=F32)
    pc_ref[...] = (_sigmoid(proj(wg_ref, 0)) * yc).astype(pc_ref.dtype)
    sga_ref[...] = _sigmoid(proj(wg_ref, D_MODEL)).astype(sga_ref.dtype)
    s_ref[0:SUBLANES, :] = s_ref[tm:tm + SUBLANES, :]

    @pl.when(i == n_tiles - 1)
    def _():
        cst_ref[...] = s_ref[tm + SUBLANES - 2:tm + SUBLANES, :]


def _convproj_prompt(x, mod4, g1, wa_bf, wg_bf, conv_w, wco_bf, *, tm):
    bsz, seq, _ = x.shape
    n_tiles = seq // tm
    const2 = lambda b, i: (0, 0)
    tile3 = lambda b, i: (b, i, 0)
    return pl.pallas_call(
        functools.partial(_convproj_kernel, tm=tm, n_tiles=n_tiles),
        grid=(bsz, n_tiles),
        in_specs=[
            pl.BlockSpec((None, tm, D_MODEL), tile3),
            pl.BlockSpec((None, 6, 1, D_MODEL), lambda b, i: (b, 0, 0, 0)),
            pl.BlockSpec((1, D_MODEL), const2),
            pl.BlockSpec(wa_bf.shape, const2, pipeline_mode=pl.Buffered(1)),
            pl.BlockSpec(wg_bf.shape, const2, pipeline_mode=pl.Buffered(1)),
            pl.BlockSpec((CONV_K, D_MODEL), const2),
            pl.BlockSpec((D_MODEL, D_MODEL), const2, pipeline_mode=pl.Buffered(1)),
        ],
        out_specs=[pl.BlockSpec((None, tm, D_MODEL), tile3), pl.BlockSpec((None, tm, D_MODEL), tile3),
                   pl.BlockSpec((None, CONV_K - 1, D_MODEL), lambda b, i: (b, 0, 0))],
        out_shape=[jax.ShapeDtypeStruct((bsz, seq, D_MODEL), BF16), jax.ShapeDtypeStruct((bsz, seq, D_MODEL), BF16),
                   jax.ShapeDtypeStruct((bsz, CONV_K - 1, D_MODEL), F32)],
        scratch_shapes=[pltpu.VMEM((tm + SUBLANES, D_MODEL), F32)],
        compiler_params=_cparams(("arbitrary", "arbitrary")),
        name="convproj_prompt",
    )(x, mod4, g1, wa_bf, wg_bf, conv_w, wco_bf)


def _qkvproj_kernel(x_ref, mod_ref, g1_ref, w_ref, rq_ref, rk_ref, *rest, tm, n_tiles, seq):
    qkv_refs = rest[0:3 * N_DIL]
    st_refs = rest[3 * N_DIL:3 * N_DIL + 2 * N_DIL]
    d_ref = rest[-1]
    i = pl.program_id(1)
    hb = _modulated_norm_bf16(x_ref, mod_ref, g1_ref)

    def proj(lo, width):
        return jnp.dot(hb, w_ref[:, lo:lo + width], preferred_element_type=F32)

    zq = proj(0, ATTN_WIDTH)
    zk = proj(ATTN_WIDTH, ATTN_WIDTH)
    zv = proj(2 * ATTN_WIDTH, ATTN_WIDTH)
    aq, bmq, bpq = rq_ref[0], rq_ref[1], rq_ref[2]
    ak, bmk, bpk = rk_ref[0], rk_ref[1], rk_ref[2]
    n_chunks = ATTN_WIDTH // LANES
    q_chunks, k_chunks, v_chunks = [], [], []
    for c in range(n_chunks):
        sl = slice(c * LANES, (c + 1) * LANES)
        q_chunks.append(_rope_chunk(zq[:, sl], aq, bmq, bpq))
        k_chunks.append(_rope_chunk(zk[:, sl], ak, bmk, bpk))
        v_chunks.append(zv[:, sl])

    for which, chunks in enumerate((q_chunks, k_chunks, v_chunks)):
        for g in range(N_DIL):
            out_ref = qkv_refs[3 * g + which]
            dil = DIL_GROUPS[g][1]
            for cc in range(CHUNKS_PER_GROUP):
                c = g * CHUNKS_PER_GROUP + cc
                sl = slice(cc * LANES, (cc + 1) * LANES)
                if dil == 1:
                    out_ref[0, :, sl] = chunks[c].astype(BF16)
                else:
                    d_ref[c] = chunks[c]
                    for r in range(dil):
                        out_ref[r, :, sl] = d_ref[c, pl.ds(r, tm // dil, stride=dil), :].astype(BF16)

    for g in range(N_DIL):
        kst, vst = st_refs[2 * g], st_refs[2 * g + 1]
        win = min(DIL_GROUPS[g][0], seq)
        if win >= tm:
            cond, r0 = i >= (seq - win) // tm, 0
        else:
            cond, r0 = i == n_tiles - 1, tm - win

        @pl.when(cond)
        def _(g=g, kst=kst, vst=vst, r0=r0):
            for cc in range(CHUNKS_PER_GROUP):
                c = g * CHUNKS_PER_GROUP + cc
                kst[cc * LANES:(cc + 1) * LANES, :] = k_chunks[c][r0:, :].T
                vst[cc * LANES:(cc + 1) * LANES, :] = v_chunks[c][r0:, :].T


def _qkvproj_prompt(x, mod4, g1, wqkv_bf, rope_q, rope_k, *, tm):
    bsz, seq, _ = x.shape
    n_tiles = seq // tm
    const2 = lambda b, i: (0, 0)
    tile3 = lambda b, i: (b, i, 0)
    in_specs = [
        pl.BlockSpec((None, tm, D_MODEL), tile3),
        pl.BlockSpec((None, 6, 1, D_MODEL), lambda b, i: (b, 0, 0, 0)),
        pl.BlockSpec((1, D_MODEL), const2),
        pl.BlockSpec(wqkv_bf.shape, const2, pipeline_mode=pl.Buffered(1)),
        pl.BlockSpec((3, tm, LANES), lambda b, i: (0, i, 0)),
        pl.BlockSpec((3, tm, LANES), lambda b, i: (0, i, 0)),
    ]
    out_shape, out_specs = [], []
    for _, dil in DIL_GROUPS:
        assert tm % (dil * 16) == 0
        for _ in range(3):
            out_shape.append(jax.ShapeDtypeStruct((bsz, dil, seq // dil, GROUP_WIDTH), BF16))
            out_specs.append(pl.BlockSpec((None, dil, tm // dil, GROUP_WIDTH), lambda b, i: (b, 0, i, 0)))
    for win, _ in DIL_GROUPS:
        win = min(win, seq)
        cols = min(win, tm)
        if win >= tm:
            imap = lambda b, i, ft=(seq - win) // tm: (b, 0, jnp.maximum(i - ft, 0))
        else:
            imap = lambda b, i: (b, 0, 0)
        for _ in range(2):
            out_shape.append(jax.ShapeDtypeStruct((bsz, GROUP_WIDTH, win), F32))
            out_specs.append(pl.BlockSpec((None, GROUP_WIDTH, cols), imap))
    return pl.pallas_call(
        functools.partial(_qkvproj_kernel, tm=tm, n_tiles=n_tiles, seq=seq),
        grid=(bsz, n_tiles),
        in_specs=in_specs,
        out_specs=out_specs,
        out_shape=out_shape,
        scratch_shapes=[pltpu.VMEM((ATTN_WIDTH // LANES, tm, LANES), F32)],
        compiler_params=_cparams(("arbitrary", "arbitrary")),
        name="qkvproj_prompt",
    )(x, mod4, g1, wqkv_bf, rope_q, rope_k)


ATTN_UNROLL = 8


def _attn_kernel(*refs, chunk, has_halo, n_cls):
    if has_halo:
        q_ref, k_ref, v_ref, kh_ref, vh_ref, o_ref, lse_ref, kbuf, vbuf = refs
    else:
        q_ref, k_ref, v_ref, o_ref, lse_ref, kbuf, vbuf = refs
    c = pl.program_id(2)
    if has_halo:
        kbuf[:, 0:Q_BLOCK, :] = kh_ref[...]
        vbuf[:, 0:Q_BLOCK, :] = vh_ref[...]
    else:
        kbuf[:, 0:Q_BLOCK, :] = jnp.zeros((n_cls, Q_BLOCK, GROUP_WIDTH), BF16)
        vbuf[:, 0:Q_BLOCK, :] = jnp.zeros((n_cls, Q_BLOCK, GROUP_WIDTH), BF16)
    kbuf[:, Q_BLOCK:Q_BLOCK + chunk, :] = k_ref[...]
    vbuf[:, Q_BLOCK:Q_BLOCK + chunk, :] = v_ref[...]
    blocks_per_cls = chunk // Q_BLOCK

    row = lax.broadcasted_iota(jnp.int32, (Q_BLOCK, 2 * Q_BLOCK), 0)
    col = lax.broadcasted_iota(jnp.int32, (Q_BLOCK, 2 * Q_BLOCK), 1)
    bias_main = jnp.where(col >= row, jnp.where(col <= row + Q_BLOCK, 0.0, NEG_BIG), NEG_BIG)
    bias_first = jnp.where(col >= Q_BLOCK, bias_main, NEG_BIG)
    lane = lax.broadcasted_iota(jnp.int32, (1, LANES), 1)
    lo_half = lane < HEAD_DIM

    def body(u, carry):
        cls, qb = u // blocks_per_cls, u % blocks_per_cls
        r0 = pl.multiple_of(qb * Q_BLOCK, Q_BLOCK)
        kt = kbuf[cls, pl.ds(r0, 2 * Q_BLOCK), :]
        vt = vbuf[cls, pl.ds(r0, 2 * Q_BLOCK), :]
        qt = q_ref[cls, pl.ds(r0, Q_BLOCK), :]
        is_first = jnp.logical_and(qb == 0, c == 0)
        bias = jnp.where(is_first, bias_first, bias_main)
        bias2 = jnp.concatenate([bias, bias], axis=0)
        lse_tile = jnp.zeros((Q_BLOCK, LANES), F32)
        for p in range(CHUNKS_PER_GROUP):
            sl = slice(p * LANES, (p + 1) * LANES)
            qp, kp, vp = qt[:, sl], kt[:, sl], vt[:, sl]
            zero = jnp.zeros_like(qp)
            q2 = jnp.concatenate([jnp.where(lo_half, qp, zero), jnp.where(lo_half, zero, qp)], axis=0)
            s = lax.dot_general(q2, kp, (((1,), (1,)), ((), ())), preferred_element_type=F32) + bias2
            m = jnp.max(s, axis=1, keepdims=True)
            e = jnp.exp(s - m)
            l = jnp.sum(e, axis=1, keepdims=True)
            o = jnp.dot(e.astype(BF16), vp, preferred_element_type=F32) * (1.0 / l)
            lse = m + jnp.log(l)
            for hh in range(2):
                lse_tile = jnp.where(lane == 2 * p + hh, lse[hh * Q_BLOCK:(hh + 1) * Q_BLOCK], lse_tile)
            o_ref[cls, pl.ds(r0, Q_BLOCK), sl] = jnp.where(lo_half, o[:Q_BLOCK], o[Q_BLOCK:])
        lse_ref[cls, pl.ds(r0, Q_BLOCK), :] = lse_tile
        return carry

    n_units = n_cls * blocks_per_cls
    lax.fori_loop(0, n_units, body, 0, unroll=min(ATTN_UNROLL, n_units))


def _attn_prompt_group(q, k, v, g, *, max_rows):
    bsz, dil, cls_len, _ = q.shape
    chunk = min(max_rows, cls_len)
    n_chunks = cls_len // chunk
    n_cls = min(dil, max_rows // chunk)
    has_halo = n_chunks > 1
    main = pl.BlockSpec((None, n_cls, chunk, GROUP_WIDTH), lambda b, r, c: (b, r, c, 0))
    in_specs = [main, main, main]
    args = [q, k, v]
    if has_halo:
        per = chunk // Q_BLOCK
        halo = pl.BlockSpec((None, n_cls, Q_BLOCK, GROUP_WIDTH), lambda b, r, c: (b, r, jnp.maximum(c * per - 1, 0), 0))
        in_specs += [halo, halo]
        args += [k, v]
    return pl.pallas_call(
        functools.partial(_attn_kernel, chunk=chunk, has_halo=has_halo, n_cls=n_cls),
        grid=(bsz, dil // n_cls, n_chunks),
        in_specs=in_specs,
        out_specs=[pl.BlockSpec((None, n_cls, chunk, GROUP_WIDTH), lambda b, r, c: (b, r, c, 0)),
                   pl.BlockSpec((None, n_cls, chunk, LANES), lambda b, r, c: (b, r, c, 0))],
        out_shape=[jax.ShapeDtypeStruct((bsz, dil, cls_len, GROUP_WIDTH), F32),
                   jax.ShapeDtypeStruct((bsz, dil, cls_len, LANES), F32)],
        scratch_shapes=[pltpu.VMEM((n_cls, chunk + Q_BLOCK, GROUP_WIDTH), BF16),
                        pltpu.VMEM((n_cls, chunk + Q_BLOCK, GROUP_WIDTH), BF16)],
        compiler_params=_cparams(("arbitrary", "arbitrary", "arbitrary")),
        name=f"attn_prompt_g{g}",
    )(*args)


def _split_bf16(x, n):
    parts = []
    r = x
    for _ in range(n):
        p = r.astype(BF16)
        parts.append(p)
        r = r - p.astype(F32)
    return parts


def _store_token_rows(ref, value, n_tokens, first_chunk=0, rows_per_token=None):
    n_chunks = value.shape[1] // LANES
    rows_per_token = rows_per_token or n_chunks
    for c in range(n_chunks):
        ref[pl.ds(first_chunk + c, n_tokens, stride=rows_per_token), :] = value[:, c * LANES:(c + 1) * LANES]


def _load_token_rows(ref, n_tokens, n_chunks, first_chunk=0, rows_per_token=None):
    rows_per_token = rows_per_token or n_chunks
    return jnp.concatenate([ref[pl.ds(first_chunk + c, n_tokens, stride=rows_per_token), :] for c in range(n_chunks)],
                           axis=1)


H_ROWS = D_MODEL // LANES
Y_ROWS = 2 * D_MODEL // LANES

_PAIRS = ((0, 1), (0, 2), (0, 3), (1, 2), (1, 3), (2, 3))
N_CLASSES = N_EXPERT_GROUPS * len(_PAIRS)
META_CLASS, META_RANK, META_WA, META_WB = 0, 1, 2, 3


def _post_kernel(*refs, tm, dils, full_precision, routed):
    (o0_ref, o1_ref, o2_ref, l0_ref, l1_ref, l2_ref, pc_ref, sga_ref, x_ref, mod_ref,
     wao_ref, wo_ref, g2_ref, wr_ref, wrh_ref, wrl_ref, br_ref, exp_ref) = refs[:18]
    if routed:
        tri_ref, x1_ref, h2_ref, meta_ref, route_ref, cnt_ref, o_scr, l_scr, run_scr = refs[18:]

        @pl.when(jnp.logical_and(pl.program_id(0) == 0, pl.program_id(1) == 0))
        def _():
            run_scr[...] = jnp.zeros_like(run_scr)
    else:
        x1_ref, h2_ref, comb_ref, o_scr, l_scr = refs[18:]

    def mm(a, w_ref):
        if full_precision:
            return jnp.dot(a, w_ref[...], precision=HIGHEST, preferred_element_type=F32)
        return jnp.dot(a.astype(BF16), w_ref[...], preferred_element_type=F32)

    def natural_order(ref, scr, dil, n_chunks):
        if dil == 1:
            return [ref[0, :, c * LANES:(c + 1) * LANES] for c in range(n_chunks)]
        out = []
        for c in range(n_chunks):
            for r in range(dil):
                scr[c, pl.ds(r, tm // dil, stride=dil), :] = ref[r, :, c * LANES:(c + 1) * LANES]
            out.append(scr[c])
        return out

    lses = [natural_order(ref, l_scr.at[g], dils[g], 1)[0] for g, ref in enumerate((l0_ref, l1_ref, l2_ref))]
    mx = jnp.maximum(lses[0], jnp.maximum(lses[1], lses[2]))
    es = [jnp.exp(v - mx) for v in lses]
    inv = 1.0 / (es[0] + es[1] + es[2])
    expand = exp_ref[...]
    attn_o = None
    for g, o_ref in enumerate((o0_ref, o1_ref, o2_ref)):
        w = es[g] * inv
        we = None
        for part in _split_bf16(w, 3 if full_precision else 2):
            t = jnp.dot(part, expand, preferred_element_type=F32)
            we = t if we is None else we + t
        o_nat = jnp.concatenate(natural_order(o_ref, o_scr.at[g], dils[g], CHUNKS_PER_GROUP), axis=1)
        term = we * o_nat
        attn_o = term if attn_o is None else attn_o + term

    y_attn = mm(attn_o, wao_ref)
    mixed = mm(pc_ref[...] + sga_ref[...] * y_attn, wo_ref)
    x1 = x_ref[...] + mod_ref[2] * mixed
    x1_ref[...] = x1
    var = jnp.mean(x1 * x1, axis=-1, keepdims=True)
    h2 = (x1 * lax.rsqrt(var + RMS_EPS)) * (g2_ref[...] * (1.0 + mod_ref[4])) + mod_ref[3]
    if routed:
        _store_token_rows(h2_ref, h2, tm)
    else:
        h2_ref[...] = h2.astype(h2_ref.dtype)

    if full_precision:
        lg = jnp.dot(h2, wr_ref[...], precision=HIGHEST, preferred_element_type=F32) + br_ref[...]
    else:
        h_hi, h_lo = _split_bf16(h2, 2)
        lg = (jnp.dot(h_hi, wrh_ref[...], preferred_element_type=F32)
              + jnp.dot(h_lo, wrh_ref[...], preferred_element_type=F32)
              + jnp.dot(h_hi, wrl_ref[...], preferred_element_type=F32)) + br_ref[...]
    lane_i = lax.broadcasted_iota(jnp.int32, (1, LANES), 1)
    lane = lane_i.astype(F32)
    lane_group = ((lane_i - N_EXPERT_GROUPS) >> 2).astype(F32)
    big = jnp.float32(1e9)
    gl = jnp.where(lane_i < N_EXPERT_GROUPS, lg, NEG_BIG)
    gmax = jnp.max(gl, axis=1, keepdims=True)
    gidx = jnp.min(jnp.where(gl == gmax, lane, big), axis=1, keepdims=True)
    g_w = 1.0 / jnp.sum(jnp.exp(gl - gmax), axis=1, keepdims=True)
    el = jnp.where(lane_group == gidx, lg, NEG_BIG)
    v1 = jnp.max(el, axis=1, keepdims=True)
    i1 = jnp.min(jnp.where(el == v1, lane, big), axis=1, keepdims=True)
    el2 = jnp.where(lane == i1, NEG_BIG, el)
    v2 = jnp.max(el2, axis=1, keepdims=True)
    i2 = jnp.min(jnp.where(el2 == v2, lane, big), axis=1, keepdims=True)
    t = jnp.exp(v2 - v1)
    den = 1.0 / (1.0 + t)
    w_top1, w_top2 = g_w * den, g_w * (t * den)
    if not routed:
        comb_ref[...] = jnp.where(lane == i1, w_top1, jnp.where(lane == i2, w_top2, 0.0))
        return

    base = jnp.float32(N_EXPERT_GROUPS) + jnp.float32(EXPERTS_PER_GROUP) * gidx
    e1, e2 = i1 - base, i2 - base
    first_is_lower = e1 < e2
    ea = jnp.where(first_is_lower, e1, e2)
    eb = jnp.where(first_is_lower, e2, e1)
    pair = ea * (7.0 - ea) * 0.5 + (eb - ea - 1.0)
    cls = gidx * jnp.float32(len(_PAIRS)) + pair
    onehot = lane == cls
    earlier = jnp.dot(tri_ref[...], jnp.where(onehot, 1.0, 0.0).astype(BF16), preferred_element_type=F32)
    running = run_scr[...]
    rank = jnp.sum(jnp.where(onehot, earlier + running, 0.0), axis=1, keepdims=True)
    running = running + jnp.sum(jnp.where(onehot, 1.0, 0.0), axis=0, keepdims=True)
    run_scr[...] = running
    cnt_ref[...] = running
    w_a = jnp.where(first_is_lower, w_top1, w_top2)
    w_b = jnp.where(first_is_lower, w_top2, w_top1)
    meta = jnp.where(lane_i == META_CLASS, cls,
                     jnp.where(lane_i == META_RANK, rank,
                               jnp.where(lane_i == META_WA, w_a, jnp.where(lane_i == META_WB, w_b, 0.0))))
    meta_ref[...] = meta
    route_ref[...] = meta.T[0:SUBLANES, :].astype(jnp.int32)


def _post(o_list, lse_list, pc, sga, x, mod4, wao, wo, g2, wr, br, expand, *, tm, full_precision, routed):
    nb, rows, _ = x.shape
    tiles = rows // tm
    dils = tuple(o.shape[1] for o in o_list)
    tile3 = lambda b, i: (b, i, 0)
    const2 = lambda b, i: (0, 0)
    mod_rows = mod4.shape[2]
    if mod_rows == 1:
        mod_spec = pl.BlockSpec((None, 6, 1, D_MODEL), lambda b, i: (b, 0, 0, 0))
    else:
        mod_spec = pl.BlockSpec((None, 6, tm, D_MODEL), lambda b, i: (b, 0, i, 0))
    cls4 = lambda b, i: (b, 0, i, 0)
    in_specs = (
        [pl.BlockSpec((None, d, tm // d, GROUP_WIDTH), cls4) for d in dils]
        + [pl.BlockSpec((None, d, tm // d, LANES), cls4) for d in dils]
        + [pl.BlockSpec((None, tm, D_MODEL), tile3)] * 3 + [mod_spec]
        + [pl.BlockSpec(wao.shape, const2), pl.BlockSpec(wo.shape, const2), pl.BlockSpec((1, D_MODEL), const2)]
        + [pl.BlockSpec((D_MODEL, LANES), const2)] * 3
        + [pl.BlockSpec((1, LANES), const2), pl.BlockSpec((LANES, GROUP_WIDTH), const2)]
    )
    wr_hi = wr.astype(BF16)
    wr_lo = (wr - wr_hi.astype(F32)).astype(BF16)
    args = [*o_list, *lse_list, pc, sga, x, mod4, wao, wo, g2, wr, wr_hi, wr_lo, br, expand]
    scratch = [pltpu.VMEM((N_DIL, CHUNKS_PER_GROUP, tm, LANES), F32), pltpu.VMEM((N_DIL, 1, tm, LANES), F32)]
    if routed:
        tri = jnp.asarray(np.tril(np.ones((tm, tm), np.float32), -1), dtype=BF16)
        args.append(tri)
        in_specs = in_specs + [pl.BlockSpec((tm, tm), const2)]
        out_specs = [pl.BlockSpec((None, tm, D_MODEL), tile3), pl.BlockSpec((None, tm * H_ROWS, LANES), tile3),
                     pl.BlockSpec((None, tm, LANES), tile3),
                     pl.BlockSpec((None, None, SUBLANES, tm), lambda b, i: (b, i, 0, 0)),
                     pl.BlockSpec((1, LANES), const2)]
        out_shape = [jax.ShapeDtypeStruct((nb, rows, D_MODEL), F32),
                     jax.ShapeDtypeStruct((nb, rows * H_ROWS, LANES), F32),
                     jax.ShapeDtypeStruct((nb, rows, LANES), F32),
                     jax.ShapeDtypeStruct((nb, tiles, SUBLANES, tm), jnp.int32),
                     jax.ShapeDtypeStruct((1, LANES), F32)]
        scratch.append(pltpu.VMEM((1, LANES), F32))
    else:
        out_specs = [pl.BlockSpec((None, tm, D_MODEL), tile3), pl.BlockSpec((None, tm, D_MODEL), tile3),
                     pl.BlockSpec((None, tm, LANES), tile3)]
        out_shape = [jax.ShapeDtypeStruct((nb, rows, D_MODEL), F32), jax.ShapeDtypeStruct((nb, rows, D_MODEL), F32),
                     jax.ShapeDtypeStruct((nb, rows, LANES), F32)]
    return pl.pallas_call(
        functools.partial(_post_kernel, tm=tm, dils=dils, full_precision=full_precision, routed=routed),
        grid=(nb, tiles),
        in_specs=in_specs,
        out_specs=out_specs,
        out_shape=out_shape,
        scratch_shapes=scratch,
        compiler_params=_cparams(("arbitrary", "arbitrary")),
        name="post_sample" if full_precision else "post_prompt",
    )(*args)


DMA_LOOP_UNROLL = 8
DMA_PRIORITIES = 2


ROUTE_CLASS_ROW, ROUTE_RANK_ROW = META_CLASS, META_RANK


def _sorted_slot(starts_ref, route_ref, r):
    return starts_ref[route_ref[ROUTE_CLASS_ROW, r]] + route_ref[ROUTE_RANK_ROW, r]


def _token_rows(token, rows_per_token):
    return pl.ds(pl.multiple_of(token * rows_per_token, rows_per_token), rows_per_token)


def _dispatch_kernel(starts_ref, route_ref, src_ref, dst_hbm, buf, sems, *, rows_per_step, n_steps):
    i = pl.program_id(0)
    slot = i % 2

    def wait_slot(s):
        def body(r, carry):
            pltpu.make_async_copy(buf.at[s, _token_rows(0, H_ROWS)], dst_hbm.at[_token_rows(0, H_ROWS)],
                                  sems.at[s * rows_per_step + r]).wait()
            return carry
        lax.fori_loop(0, rows_per_step, body, 0, unroll=DMA_LOOP_UNROLL)

    @pl.when(i >= 1)
    def _():
        wait_slot(1 - slot)

    buf[slot] = src_ref[...]

    def issue(pair, carry):
        for prio in range(DMA_PRIORITIES):
            r = pair * DMA_PRIORITIES + prio
            pltpu.make_async_copy(buf.at[slot, _token_rows(r, H_ROWS)],
                                  dst_hbm.at[_token_rows(_sorted_slot(starts_ref, route_ref, r), H_ROWS)],
                                  sems.at[slot * rows_per_step + r]).start(priority=prio)
        return carry

    lax.fori_loop(0, rows_per_step // DMA_PRIORITIES, issue, 0, unroll=DMA_LOOP_UNROLL // DMA_PRIORITIES)

    @pl.when(i == n_steps - 1)
    def _():
        wait_slot(slot)


def _route_spec(rows_per_step, route_tile, shift=0, n_steps=None):
    per_tile = route_tile // rows_per_step

    def imap(i, starts):
        s = i + shift if n_steps is None else jnp.minimum(i + shift, n_steps - 1)
        return (s // per_tile, 0, s % per_tile)

    return pl.BlockSpec((None, SUBLANES, rows_per_step), imap, memory_space=pltpu.SMEM)


def _dispatch_rows(src, route, starts, *, rows_per_step):
    n = src.shape[0] // H_ROWS
    n_steps = n // rows_per_step
    grid_spec = pltpu.PrefetchScalarGridSpec(
        num_scalar_prefetch=1,
        grid=(n_steps,),
        in_specs=[_route_spec(rows_per_step, route.shape[2]),
                  pl.BlockSpec((rows_per_step * H_ROWS, LANES), lambda i, starts: (i, 0))],
        out_specs=pl.BlockSpec(memory_space=pl.ANY),
        scratch_shapes=[pltpu.VMEM((2, rows_per_step * H_ROWS, LANES), src.dtype),
                        pltpu.SemaphoreType.DMA((2 * rows_per_step,))],
    )
    return pl.pallas_call(
        functools.partial(_dispatch_kernel, rows_per_step=rows_per_step, n_steps=n_steps),
        grid_spec=grid_spec,
        out_shape=jax.ShapeDtypeStruct(src.shape, src.dtype),
        compiler_params=_cparams(("arbitrary",)),
        name="moe_dispatch",
    )(starts, route, src)


def _experts_kernel(tile_ref, ea_ref, eb_ref, lo_ref, hi_ref, first_ref, switch_ref, nv_ref,
                    x_ref, w1a_ref, w1b_ref, w3a_ref, w3b_ref, w2a_ref, w2b_ref, y_ref, w13_scr, w2_scr, *, tm):
    i = pl.program_id(0)

    @pl.when(i < nv_ref[0])
    def _():
        @pl.when(switch_ref[i] == 1)
        def _():
            for k, w_ref in enumerate((w1a_ref, w3a_ref, w1b_ref, w3b_ref)):
                w13_scr[k] = w_ref[...].astype(BF16)
            for k, w_ref in enumerate((w2a_ref, w2b_ref)):
                w2_scr[k] = w_ref[...].astype(BF16)

        def expert(x, k):
            a = jnp.dot(x, w13_scr[2 * k], preferred_element_type=F32)
            b = jnp.dot(x, w13_scr[2 * k + 1], preferred_element_type=F32)
            hid = (a * _sigmoid(a)) * b
            return jnp.dot(hid.astype(BF16), w2_scr[k], preferred_element_type=F32)

        @pl.when(first_ref[i] == 1)
        def _():
            x = _load_token_rows(x_ref, tm, H_ROWS).astype(BF16)
            _store_token_rows(y_ref, expert(x, 0), tm, first_chunk=0, rows_per_token=Y_ROWS)
            _store_token_rows(y_ref, expert(x, 1), tm, first_chunk=H_ROWS, rows_per_token=Y_ROWS)

        @pl.when(first_ref[i] == 0)
        def _():
            x = _load_token_rows(x_ref, tm, H_ROWS).astype(BF16)
            row = lax.broadcasted_iota(jnp.int32, (tm, 1), 0)
            for k in range(2):
                old = _load_token_rows(y_ref, tm, H_ROWS, first_chunk=k * H_ROWS, rows_per_token=Y_ROWS)
                new = jnp.where(row >= lo_ref[i], jnp.where(row < hi_ref[i], expert(x, k), old), old)
                _store_token_rows(y_ref, new, tm, first_chunk=k * H_ROWS, rows_per_token=Y_ROWS)


def _experts_routed(hs, tables, w1, w3, w2, *, tm):
    n = hs.shape[0] // H_ROWS
    n_items = tables[0].shape[0]
    x_map = lambda i, tile, ea, eb, lo, hi, first, switch, nv: (tile[i], 0)
    wa_map = lambda i, tile, ea, eb, lo, hi, first, switch, nv: (ea[i], 0, 0)
    wb_map = lambda i, tile, ea, eb, lo, hi, first, switch, nv: (eb[i], 0, 0)
    w13 = lambda m: pl.BlockSpec((None, D_MODEL, D_EXPERT), m)
    w2s = lambda m: pl.BlockSpec((None, D_EXPERT, D_MODEL), m)
    grid_spec = pltpu.PrefetchScalarGridSpec(
        num_scalar_prefetch=len(tables),
        grid=(n_items,),
        in_specs=[pl.BlockSpec((tm * H_ROWS, LANES), x_map),
                  w13(wa_map), w13(wb_map), w13(wa_map), w13(wb_map), w2s(wa_map), w2s(wb_map)],
        out_specs=pl.BlockSpec((tm * Y_ROWS, LANES), x_map),
        scratch_shapes=[pltpu.VMEM((4, D_MODEL, D_EXPERT), BF16), pltpu.VMEM((2, D_EXPERT, D_MODEL), BF16)],
    )
    return pl.pallas_call(
        functools.partial(_experts_kernel, tm=tm),
        grid_spec=grid_spec,
        out_shape=jax.ShapeDtypeStruct((n * Y_ROWS, LANES), F32),
        compiler_params=_cparams(("arbitrary",)),
        name="moe_experts",
    )(*tables, hs, w1, w1, w3, w3, w2, w2)


def _combine_kernel(starts_ref, route_ref, route_next_ref, yp_hbm, x1_ref, meta_ref, mod_ref, gf_ref, y_ref, ybuf, sems,
                    *, rows_per_step, n_steps):
    i = pl.program_id(0)
    slot = i % 2

    def issue(r_ref, s):
        def body(pair, carry):
            for prio in range(DMA_PRIORITIES):
                r = pair * DMA_PRIORITIES + prio
                pltpu.make_async_copy(yp_hbm.at[_token_rows(_sorted_slot(starts_ref, r_ref, r), Y_ROWS)],
                                      ybuf.at[s, :, pl.ds(r, 1), :],
                                      sems.at[s * rows_per_step + r]).start(priority=prio)
            return carry
        lax.fori_loop(0, rows_per_step // DMA_PRIORITIES, body, 0, unroll=DMA_LOOP_UNROLL // DMA_PRIORITIES)

    @pl.when(i == 0)
    def _():
        issue(route_ref, 0)

    @pl.when(i + 1 < n_steps)
    def _():
        issue(route_next_ref, 1 - slot)

    def wait_body(r, carry):
        pltpu.make_async_copy(yp_hbm.at[_token_rows(0, Y_ROWS)], ybuf.at[slot, :, pl.ds(r, 1), :],
                              sems.at[slot * rows_per_step + r]).wait()
        return carry

    lax.fori_loop(0, rows_per_step, wait_body, 0, unroll=DMA_LOOP_UNROLL)
    ya = jnp.concatenate([ybuf[slot, c] for c in range(H_ROWS)], axis=1)
    yb = jnp.concatenate([ybuf[slot, H_ROWS + c] for c in range(H_ROWS)], axis=1)
    meta = meta_ref[...]
    lane = lax.broadcasted_iota(jnp.int32, (1, LANES), 1)
    w_a = jnp.sum(jnp.where(lane == META_WA, meta, 0.0), axis=1, keepdims=True)
    w_b = jnp.sum(jnp.where(lane == META_WB, meta, 0.0), axis=1, keepdims=True)
    x2 = x1_ref[...] + mod_ref[5] * (w_a * ya + w_b * yb)
    var = jnp.mean(x2 * x2, axis=-1, keepdims=True)
    y_ref[...] = (x2 * lax.rsqrt(var + RMS_EPS)) * gf_ref[...]


def _combine_final(yp, route, starts, x1, meta, mod4, gf, *, rows_per_step, rows_per_mod):
    n = x1.shape[0]
    n_steps = n // rows_per_step
    steps_per_mod = rows_per_mod // rows_per_step
    row = lambda i, starts: (i, 0)
    grid_spec = pltpu.PrefetchScalarGridSpec(
        num_scalar_prefetch=1,
        grid=(n_steps,),
        in_specs=[_route_spec(rows_per_step, route.shape[2]),
                  _route_spec(rows_per_step, route.shape[2], shift=1, n_steps=n_steps),
                  pl.BlockSpec(memory_space=pl.ANY),
                  pl.BlockSpec((rows_per_step, D_MODEL), row),
                  pl.BlockSpec((rows_per_step, LANES), row),
                  pl.BlockSpec((None, 6, 1, D_MODEL), lambda i, starts: (i // steps_per_mod, 0, 0, 0)),
                  pl.BlockSpec((1, D_MODEL), lambda i, starts: (0, 0))],
        out_specs=pl.BlockSpec((rows_per_step, D_MODEL), row),
        scratch_shapes=[pltpu.VMEM((2, Y_ROWS, rows_per_step, LANES), F32),
                        pltpu.SemaphoreType.DMA((2 * rows_per_step,))],
    )
    yp_rows = yp.reshape(yp.shape[0], 1, LANES)
    return pl.pallas_call(
        functools.partial(_combine_kernel, rows_per_step=rows_per_step, n_steps=n_steps),
        grid_spec=grid_spec,
        out_shape=jax.ShapeDtypeStruct((n, D_MODEL), F32),
        compiler_params=_cparams(("arbitrary",)),
        name="moe_combine",
    )(starts, route, route, yp_rows, x1, meta, mod4, gf)


def _routing_tables(counts, *, n, tm):
    counts = counts.astype(jnp.int32)
    ends = jnp.cumsum(counts)
    starts = ends - counts
    n_items = n // tm + N_CLASSES
    first_tile = starts // tm
    last_tile = jnp.maximum(ends - 1, starts) // tm
    visits = jnp.where(counts > 0, last_tile - first_tile + 1, 0)
    item_end = jnp.cumsum(visits)
    item_start = item_end - visits
    n_valid = item_end[-1]
    item = jnp.arange(n_items, dtype=jnp.int32)
    idx = jnp.minimum(item, n_valid - 1)
    c = jnp.sum((idx[:, None] >= item_end[None, :]).astype(jnp.int32), axis=1)
    class_ids = jnp.arange(N_CLASSES, dtype=jnp.int32)

    def pick(table):
        return jnp.sum(jnp.where(c[:, None] == class_ids[None, :], table[None, :], 0), axis=1)

    tile = pick(first_tile) + (idx - pick(item_start))
    live = item < n_valid
    lo = jnp.where(live, jnp.clip(pick(starts) - tile * tm, 0, tm), 0)
    hi = jnp.where(live, jnp.clip(pick(ends) - tile * tm, 0, tm), 0)
    prev_tile = jnp.concatenate([jnp.full((1,), -1, jnp.int32), tile[:-1]])
    first = (tile != prev_tile).astype(jnp.int32)
    prev_c = jnp.concatenate([jnp.full((1,), -1, jnp.int32), c[:-1]])
    switch = (c != prev_c).astype(jnp.int32)
    group_of_class = np.arange(N_CLASSES) // len(_PAIRS)
    pair_of_class = np.arange(N_CLASSES) % len(_PAIRS)
    pairs = np.asarray(_PAIRS)
    ea = pick(jnp.asarray(group_of_class * EXPERTS_PER_GROUP + pairs[pair_of_class, 0], dtype=jnp.int32))
    eb = pick(jnp.asarray(group_of_class * EXPERTS_PER_GROUP + pairs[pair_of_class, 1], dtype=jnp.int32))
    as_i32 = lambda a: a.astype(jnp.int32)
    return as_i32(starts), tuple(as_i32(a) for a in (tile, ea, eb, lo, hi, first, switch, n_valid.reshape(1)))


def _moe_kernel(h_ref, comb_ref, x1_ref, mod_ref, gf_ref, w1_ref, w3_ref, w2_ref, y_ref, acc_ref):
    e = pl.program_id(2)

    @pl.when(e == 0)
    def _():
        acc_ref[...] = jnp.zeros_like(acc_ref)

    h = h_ref[...].astype(BF16)
    a = jnp.dot(h, w1_ref[...].astype(BF16), preferred_element_type=F32)
    b = jnp.dot(h, w3_ref[...].astype(BF16), preferred_element_type=F32)
    lane = lax.broadcasted_iota(jnp.int32, (1, LANES), 1)
    cw = jnp.sum(jnp.where(lane == e + N_EXPERT_GROUPS, comb_ref[...], 0.0), axis=1, keepdims=True)
    hid = (a * _sigmoid(a)) * b * cw
    acc_ref[...] += jnp.dot(hid.astype(BF16), w2_ref[...].astype(BF16), preferred_element_type=F32)

    @pl.when(e == N_EXPERTS - 1)
    def _():
        x2 = x1_ref[...] + mod_ref[5] * acc_ref[...]
        var = jnp.mean(x2 * x2, axis=-1, keepdims=True)
        y_ref[...] = (x2 * lax.rsqrt(var + RMS_EPS)) * gf_ref[...]


def _moe_dense(h2, comb, x1, mod4, gf, w1, w3, w2, *, tm):
    nb, rows, _ = x1.shape
    tiles = rows // tm
    tile3 = lambda b, i, e: (b, i, 0)
    if mod4.shape[2] == 1:
        mod_spec = pl.BlockSpec((None, 6, 1, D_MODEL), lambda b, i, e: (b, 0, 0, 0))
    else:
        mod_spec = pl.BlockSpec((None, 6, tm, D_MODEL), lambda b, i, e: (b, 0, i, 0))
    return pl.pallas_call(
        _moe_kernel,
        grid=(nb, tiles, N_EXPERTS),
        in_specs=[
            pl.BlockSpec((None, tm, D_MODEL), tile3),
            pl.BlockSpec((None, tm, LANES), tile3),
            pl.BlockSpec((None, tm, D_MODEL), tile3),
            mod_spec,
            pl.BlockSpec((1, D_MODEL), lambda b, i, e: (0, 0)),
            pl.BlockSpec((None, D_MODEL, D_EXPERT), lambda b, i, e: (e, 0, 0)),
            pl.BlockSpec((None, D_MODEL, D_EXPERT), lambda b, i, e: (e, 0, 0)),
            pl.BlockSpec((None, D_EXPERT, D_MODEL), lambda b, i, e: (e, 0, 0)),
        ],
        out_specs=pl.BlockSpec((None, tm, D_MODEL), tile3),
        out_shape=jax.ShapeDtypeStruct((nb, rows, D_MODEL), F32),
        scratch_shapes=[pltpu.VMEM((tm, D_MODEL), F32)],
        compiler_params=_cparams(("arbitrary", "arbitrary", "arbitrary")),
        name=f"moe_dense_{nb * rows}",
    )(h2, comb, x1, mod4, gf, w1, w3, w2)


def _s_inproj_kernel(x_ref, mod_ref, g1_ref, w_ref, z_ref):
    x = x_ref[...]
    var = jnp.mean(x * x, axis=-1, keepdims=True)
    h = (x * lax.rsqrt(var + RMS_EPS)) * (g1_ref[...] * (1.0 + mod_ref[1])) + mod_ref[0]
    h_hi, h_lo = _split_bf16(h, 2)
    w_hi, w_lo = _split_bf16(w_ref[...], 2)
    z_ref[...] = (jnp.dot(h_hi, w_hi, preferred_element_type=F32) + jnp.dot(h_lo, w_hi, preferred_element_type=F32)
                  + jnp.dot(h_hi, w_lo, preferred_element_type=F32))


def _s_inproj(x, mod_tok, g1, w_in, *, col_block):
    n = x.shape[0]
    return pl.pallas_call(
        _s_inproj_kernel,
        grid=(IN_COLS // col_block,),
        in_specs=[pl.BlockSpec((n, D_MODEL), lambda j: (0, 0)),
                  pl.BlockSpec((6, n, D_MODEL), lambda j: (0, 0, 0)),
                  pl.BlockSpec((1, D_MODEL), lambda j: (0, 0)),
                  pl.BlockSpec((D_MODEL, col_block), lambda j: (0, j))],
        out_specs=pl.BlockSpec((n, col_block), lambda j: (0, j)),
        out_shape=jax.ShapeDtypeStruct((n, IN_COLS), F32),
        compiler_params=_cparams(("arbitrary",)),
        name="inproj_sample",
    )(x, mod_tok, g1, w_in)


def _s_mid_kernel(z_ref, p0_ref, p1_ref, cw_ref, wco_ref, rq_ref, rk_ref,
                  cu_ref, pc_ref, sga_ref, q_ref, k_ref, v_ref, *, t_len):
    n = z_ref.shape[0]
    cu = z_ref[:, OFF_GC:OFF_GC + D_MODEL] * z_ref[:, OFF_U:OFF_U + D_MODEL]
    cu_ref[...] = cu
    t = lax.broadcasted_iota(jnp.int32, (n, D_MODEL), 0) & (t_len - 1)
    prev1 = jnp.where(t >= 1, pltpu.roll(cu, 1, 0), p1_ref[...])
    prev2 = jnp.where(t >= 2, pltpu.roll(cu, 2, 0), jnp.where(t == 0, p0_ref[...], p1_ref[...]))
    cw = cw_ref[...]
    conv = cw[0:1] * prev2 + cw[1:2] * prev1 + cw[2:3] * cu
    yc = jnp.dot(z_ref[:, OFF_GB:OFF_GB + D_MODEL] * conv, wco_ref[...], precision=HIGHEST,
                 preferred_element_type=F32)
    pc_ref[...] = _sigmoid(z_ref[:, OFF_GCONV:OFF_GCONV + D_MODEL]) * yc
    sga_ref[...] = _sigmoid(z_ref[:, OFF_GATTN:OFF_GATTN + D_MODEL])
    aq, bmq, bpq = rq_ref[0], rq_ref[1], rq_ref[2]
    ak, bmk, bpk = rk_ref[0], rk_ref[1], rk_ref[2]
    for c in range(ATTN_WIDTH // LANES):
        sl = slice(c * LANES, (c + 1) * LANES)
        q_ref[:, sl] = _rope_chunk(z_ref[:, OFF_Q + c * LANES:OFF_Q + (c + 1) * LANES], aq, bmq, bpq)
        k_ref[:, sl] = _rope_chunk(z_ref[:, OFF_K + c * LANES:OFF_K + (c + 1) * LANES], ak, bmk, bpk)
    v_ref[...] = z_ref[:, OFF_V:OFF_V + ATTN_WIDTH]


def _s_mid(z, p0e, p1e, conv_w, w_conv_out, rope_q, rope_k, *, t_len):
    n = z.shape[0]
    assert t_len & (t_len - 1) == 0
    full = lambda shape: pl.BlockSpec(shape, lambda i: (0,) * len(shape))
    out_shape = [jax.ShapeDtypeStruct((n, D_MODEL), F32)] * 3 + [jax.ShapeDtypeStruct((n, ATTN_WIDTH), F32)] * 3
    return pl.pallas_call(
        functools.partial(_s_mid_kernel, t_len=t_len),
        grid=(1,),
        in_specs=[full(z.shape), full(p0e.shape), full(p1e.shape), full(conv_w.shape), full(w_conv_out.shape),
                  full(rope_q.shape), full(rope_k.shape)],
        out_specs=[full((n, D_MODEL))] * 3 + [full((n, ATTN_WIDTH))] * 3,
        out_shape=out_shape,
        compiler_params=_cparams(("arbitrary",)),
        name="mid_sample",
    )(z, p0e, p1e, conv_w, w_conv_out, rope_q, rope_k)


def _head_sum(x):
    return jnp.sum(x.reshape(HEADS_PER_GROUP, HEAD_DIM, x.shape[-1]), axis=1)


def _head_expand(x):
    n = x.shape[-1]
    return jnp.broadcast_to(x[:, None, :], (HEADS_PER_GROUP, HEAD_DIM, n)).reshape(GROUP_WIDTH, n)


def _s_attn_disjoint(q_ref, kn, vn, ck_ref, cv_ref, o_ref, lse_ref, s_scr, lane, new_idx, *, n_tiles, dil, t_len):
    cls = lane & (dil - 1)
    q_all = q_ref[...]
    qsel = jnp.zeros((GROUP_WIDTH, LANES), F32)
    for t in range(t_len):
        qsel = jnp.where(cls == t, jnp.broadcast_to(q_all[:, t:t + 1], (GROUP_WIDTH, LANES)), qsel)
    smax = None
    for j in range(n_tiles):
        s = jnp.where(cls < t_len, _head_sum(ck_ref[:, j * LANES:(j + 1) * LANES] * qsel), NEG_BIG)
        s_scr[:, j * LANES:(j + 1) * LANES] = s
        smax = s if smax is None else jnp.maximum(smax, s)
    s_new = _head_sum(kn * q_all)
    m_cols = []
    m_lane = jnp.zeros((HEADS_PER_GROUP, LANES), F32)
    m_new = jnp.zeros((HEADS_PER_GROUP, t_len), F32)
    for t in range(t_len):
        mt = jnp.maximum(jnp.max(jnp.where(cls == t, smax, NEG_BIG), axis=1, keepdims=True), s_new[:, t:t + 1])
        m_cols.append(mt)
        m_lane = jnp.where(cls == t, mt, m_lane)
        m_new = jnp.where(new_idx == t, mt, m_new)
    esum = jnp.zeros((HEADS_PER_GROUP, LANES), F32)
    for j in range(n_tiles):
        e = jnp.exp(s_scr[:, j * LANES:(j + 1) * LANES] - m_lane)
        s_scr[:, j * LANES:(j + 1) * LANES] = e
        esum = esum + e
    e_new = jnp.exp(s_new - m_new)
    l_cols = []
    inv_lane = jnp.zeros((HEADS_PER_GROUP, LANES), F32)
    inv_new = jnp.zeros((HEADS_PER_GROUP, t_len), F32)
    for t in range(t_len):
        lt = jnp.sum(jnp.where(cls == t, esum, 0.0), axis=1, keepdims=True) + e_new[:, t:t + 1]
        l_cols.append(lt)
        inv_lane = jnp.where(cls == t, 1.0 / lt, inv_lane)
        inv_new = jnp.where(new_idx == t, 1.0 / lt, inv_new)
    acc = None
    for j in range(n_tiles):
        term = cv_ref[:, j * LANES:(j + 1) * LANES] * _head_expand(s_scr[:, j * LANES:(j + 1) * LANES] * inv_lane)
        acc = term if acc is None else acc + term
    o_new = vn * _head_expand(e_new * inv_new)
    for t in range(t_len):
        o_ref[:, t:t + 1] = jnp.sum(jnp.where(cls == t, acc, 0.0), axis=1, keepdims=True) + o_new[:, t:t + 1]
        lse_ref[:, t:t + 1] = m_cols[t] + jnp.log(l_cols[t])


def _s_attn_kernel(q_ref, kn_ref, vn_ref, ck_ref, cv_ref, o_ref, lse_ref, ko_ref, vo_ref, s_scr,
                   *, win, dil, t_len):
    n_tiles = win // LANES
    lane = lax.broadcasted_iota(jnp.int32, (1, LANES), 1)
    new_idx = lax.broadcasted_iota(jnp.int32, (1, t_len), 1)
    kn = kn_ref[...]
    vn = vn_ref[...]
    if dil >= t_len:
        _s_attn_disjoint(q_ref, kn, vn, ck_ref, cv_ref, o_ref, lse_ref, s_scr, lane, new_idx,
                         n_tiles=n_tiles, dil=dil, t_len=t_len)
    for t in range(t_len if dil < t_len else 0):
        qb = jnp.broadcast_to(q_ref[:, t:t + 1], (GROUP_WIDTH, LANES))
        m = None
        for j in range(n_tiles):
            pos = lane + j * LANES
            s = _head_sum(ck_ref[:, j * LANES:(j + 1) * LANES] * qb)
            s = jnp.where(pos >= t, jnp.where(((pos - t) & (dil - 1)) == 0, s, NEG_BIG), NEG_BIG)
            s_scr[:, j * LANES:(j + 1) * LANES] = s
            mj = jnp.max(s, axis=1, keepdims=True)
            m = mj if m is None else jnp.maximum(m, mj)
        s_new = _head_sum(kn * qb[:, 0:t_len])
        s_new = jnp.where(new_idx <= t, jnp.where(((t - new_idx) & (dil - 1)) == 0, s_new, NEG_BIG), NEG_BIG)
        m = jnp.maximum(m, jnp.max(s_new, axis=1, keepdims=True))
        e_new = jnp.exp(s_new - m)
        l = jnp.sum(e_new, axis=1, keepdims=True)
        acc = None
        for j in range(n_tiles):
            e = jnp.exp(s_scr[:, j * LANES:(j + 1) * LANES] - m)
            l = l + jnp.sum(e, axis=1, keepdims=True)
            term = cv_ref[:, j * LANES:(j + 1) * LANES] * _head_expand(e)
            acc = term if acc is None else acc + term
        o = jnp.sum(acc, axis=1, keepdims=True) + jnp.sum(vn * _head_expand(e_new), axis=1, keepdims=True)
        o_ref[:, t:t + 1] = o * _head_expand(1.0 / l)
        lse_ref[:, t:t + 1] = m + jnp.log(l)

    for c_ref, new, out_ref in ((ck_ref, kn, ko_ref), (cv_ref, vn, vo_ref)):
        out_ref[...] = pltpu.roll(c_ref[...], win - t_len, 1)
        out_ref[:, win - t_len:win] = new


def _s_attn_group(q_t, kn_t, vn_t, cache_k, cache_v, g):
    bsz, _, t_len = q_t.shape
    win_full, dil = DIL_GROUPS[g]
    win = cache_k.shape[2]
    assert win == win_full and win == (KEYS_PER_QUERY - 1) * dil and win % LANES == 0
    assert dil & (dil - 1) == 0 and LANES % dil == 0
    grp = lambda b: (b, g, 0)
    per_b = lambda b: (b, 0, 0)
    return pl.pallas_call(
        functools.partial(_s_attn_kernel, win=win, dil=dil, t_len=t_len),
        grid=(bsz,),
        in_specs=[pl.BlockSpec((None, GROUP_WIDTH, t_len), grp)] * 3
        + [pl.BlockSpec((None, GROUP_WIDTH, win), per_b)] * 2,
        out_specs=[pl.BlockSpec((None, GROUP_WIDTH, t_len), per_b),
                   pl.BlockSpec((None, HEADS_PER_GROUP, t_len), per_b),
                   pl.BlockSpec((None, GROUP_WIDTH, win), per_b),
                   pl.BlockSpec((None, GROUP_WIDTH, win), per_b)],
        out_shape=[jax.ShapeDtypeStruct((bsz, GROUP_WIDTH, t_len), F32),
                   jax.ShapeDtypeStruct((bsz, HEADS_PER_GROUP, t_len), F32),
                   jax.ShapeDtypeStruct((bsz, GROUP_WIDTH, win), F32),
                   jax.ShapeDtypeStruct((bsz, GROUP_WIDTH, win), F32)],
        scratch_shapes=[pltpu.VMEM((HEADS_PER_GROUP, win), F32)],
        compiler_params=_cparams(("arbitrary",)),
        name=f"attn_sample_g{g}",
    )(q_t, kn_t, vn_t, cache_k, cache_v)


TM_INPROJ = 512
TM_POST = 512
TM_EXPERT = 256
ROWS_PER_DMA_STEP = 256
ATTN_ROWS_PER_STEP = 1024
S_COL_BLOCK = 512


def _to_state(a_t):
    b, _, length = a_t.shape
    return jnp.transpose(a_t.reshape(b, HEADS_PER_GROUP, HEAD_DIM, length), (0, 3, 1, 2))[None]


def _from_state(a):
    b, length = a.shape[0], a.shape[1]
    return jnp.transpose(a, (0, 2, 3, 1)).reshape(b, GROUP_WIDTH, length)


def kernel(x_prompt, x_sample, cache_k1, cache_v1, cache_k2, cache_v2, cache_k3, cache_v3, state_conv,
           c_prompt, c_sample, norm1_g, norm2_g, normf_g, w_ada, b_ada, w_in, conv_w, w_conv_out,
           w_attn_out, w_o, w_rg, b_rg, w_re, b_re, w1, w3, w2):
    depth = w_in.shape[0]
    assert depth == 1
    bsz, seq, _ = x_prompt.shape
    dbsz, t_len, _ = x_sample.shape
    n_s = dbsz * t_len
    l = 0

    w_a_bf = w_in[l][:, :OFF_Q].astype(BF16)
    w_g_bf = w_in[l][:, OFF_GCONV:].astype(BF16)
    w_qkv_bf = w_in[l][:, OFF_Q:OFF_GCONV].astype(BF16)
    wco_bf = w_conv_out[l].astype(BF16)
    wao_bf = w_attn_out[l].astype(BF16)
    wo_bf = w_o[l].astype(BF16)
    w1_e = w1[l].reshape(N_EXPERTS, D_MODEL, D_EXPERT)
    w3_e = w3[l].reshape(N_EXPERTS, D_MODEL, D_EXPERT)
    w2_e = w2[l].reshape(N_EXPERTS, D_EXPERT, D_MODEL)
    g1 = norm1_g[l].reshape(1, D_MODEL)
    g2 = norm2_g[l].reshape(1, D_MODEL)
    gf = normf_g.reshape(1, D_MODEL)
    n_route = N_EXPERT_GROUPS + N_EXPERTS
    wr = jnp.pad(jnp.concatenate([w_rg[l], w_re[l]], axis=1), ((0, 0), (0, LANES - n_route)))
    br = jnp.pad(jnp.concatenate([b_rg[l], b_re[l]]), (0, LANES - n_route)).reshape(1, LANES)
    head_of_lane = np.arange(GROUP_WIDTH) // HEAD_DIM
    expand_bf = jnp.asarray((np.arange(LANES)[:, None] == head_of_lane[None, :]).astype(np.float32), dtype=BF16)

    mod = _adaln(jnp.concatenate([c_prompt, c_sample], axis=0), w_ada[l], b_ada[l])
    mod_p4 = mod[:bsz].reshape(bsz, 6, 1, D_MODEL)
    mod_tok = jnp.repeat(mod[bsz:].reshape(dbsz, 1, 6, D_MODEL), t_len, axis=1)
    mod_tok = jnp.transpose(mod_tok.reshape(n_s, 6, D_MODEL), (1, 0, 2))
    mod_s4 = mod_tok[None]

    pos_p = jnp.arange(seq, dtype=jnp.int32)
    rope_q_p = _rope_tables(pos_p, HEAD_DIM ** -0.5)
    rope_k_p = _rope_tables(pos_p, 1.0)
    pc_p, sga_p, conv_p = _convproj_prompt(x_prompt, mod_p4, g1, w_a_bf, w_g_bf,
                                           conv_w[l], wco_bf, tm=TM_INPROJ)
    outs = _qkvproj_prompt(x_prompt, mod_p4, g1, w_qkv_bf, rope_q_p, rope_k_p, tm=TM_INPROJ)
    qkv_p = outs[0:3 * N_DIL]
    states_p = outs[3 * N_DIL:]
    o_p, lse_p = [], []
    for g in range(N_DIL):
        o_g, lse_g = _attn_prompt_group(qkv_p[3 * g], qkv_p[3 * g + 1], qkv_p[3 * g + 2], g, max_rows=ATTN_ROWS_PER_STEP)
        o_p.append(o_g)
        lse_p.append(lse_g)
    x1_p, h2rows_p, meta_p, route_p, cnt_p = _post(o_p, lse_p, pc_p, sga_p, x_prompt, mod_p4, wao_bf, wo_bf, g2, wr, br,
                                                   expand_bf, tm=TM_POST, full_precision=False, routed=True)
    n_p = bsz * seq
    route = route_p.reshape(n_p // TM_POST, SUBLANES, TM_POST)
    starts, tables = _routing_tables(cnt_p[0, :N_CLASSES], n=n_p, tm=TM_EXPERT)
    hs = _dispatch_rows(h2rows_p.reshape(n_p * H_ROWS, LANES), route, starts, rows_per_step=ROWS_PER_DMA_STEP)
    yp = _experts_routed(hs, tables, w1_e, w3_e, w2_e, tm=TM_EXPERT)
    y_p = _combine_final(yp, route, starts, x1_p.reshape(n_p, D_MODEL), meta_p.reshape(n_p, LANES), mod_p4, gf,
                         rows_per_step=ROWS_PER_DMA_STEP, rows_per_mod=seq).reshape(bsz, seq, D_MODEL)

    pos_s = PAST_LEN + jnp.arange(t_len, dtype=jnp.int32)
    rope_q_s = jnp.tile(_rope_tables(pos_s, HEAD_DIM ** -0.5), (1, dbsz, 1))
    rope_k_s = jnp.tile(_rope_tables(pos_s, 1.0), (1, dbsz, 1))
    xs = x_sample.reshape(n_s, D_MODEL)
    z_s = _s_inproj(xs, mod_tok, g1, w_in[l], col_block=S_COL_BLOCK)
    past = state_conv[l]
    p0e = jnp.repeat(past[:, 0], t_len, axis=0)
    p1e = jnp.repeat(past[:, 1], t_len, axis=0)
    cu_s, pc_s, sga_s, q_s, k_s, v_s = _s_mid(z_s, p0e, p1e, conv_w[l], w_conv_out[l], rope_q_s, rope_k_s,
                                              t_len=t_len)
    to_cols = lambda a: jnp.transpose(a.reshape(dbsz, t_len, ATTN_WIDTH), (0, 2, 1))
    q_t, kn_t, vn_t = to_cols(q_s), to_cols(k_s), to_cols(v_s)
    caches = ((cache_k1, cache_v1), (cache_k2, cache_v2), (cache_k3, cache_v3))
    o_s, lse_s, kv_s = [], [], []
    for g, (ck, cv) in enumerate(caches):
        o_g, lse_g, ko, vo = _s_attn_group(q_t, kn_t, vn_t, _from_state(ck[l]), _from_state(cv[l]), g)
        o_s.append(jnp.transpose(o_g, (0, 2, 1)).reshape(1, 1, n_s, GROUP_WIDTH))
        lse_rows = jnp.transpose(lse_g, (0, 2, 1)).reshape(n_s, HEADS_PER_GROUP)
        lse_s.append(jnp.pad(lse_rows, ((0, 0), (0, LANES - HEADS_PER_GROUP))).reshape(1, 1, n_s, LANES))
        kv_s += [_to_state(ko), _to_state(vo)]
    x1_s, h2_s, comb_s = _post(o_s, lse_s, pc_s[None], sga_s[None], xs[None], mod_s4, w_attn_out[l], w_o[l], g2, wr, br,
                               expand_bf, tm=n_s, full_precision=True, routed=False)
    y_s = _moe_dense(h2_s, comb_s, x1_s, mod_s4, gf, w1_e, w3_e, w2_e, tm=n_s)

    conv_s = cu_s.reshape(dbsz, t_len, D_MODEL)[:, t_len - (CONV_K - 1):]
    return (y_p, y_s.reshape(dbsz, t_len, D_MODEL),
            *[_to_state(a) for a in states_p], conv_p.reshape(1, bsz, CONV_K - 1, D_MODEL),
            *kv_s, conv_s.reshape(1, dbsz, CONV_K - 1, D_MODEL))
```

```python
import functools

import numpy as np
import jax
import jax.numpy as jnp
from jax import lax
from jax.experimental import pallas as pl
from jax.experimental.pallas import tpu as pltpu

F32 = jnp.float32
BF16 = jnp.bfloat16
HIGHEST = lax.Precision.HIGHEST

D_MODEL = 1024
HEAD_DIM = 64
HEADS_PER_GROUP = 8
GROUP_WIDTH = HEADS_PER_GROUP * HEAD_DIM
DIL_GROUPS = ((128, 1), (512, 4), (2048, 16))
N_DIL = len(DIL_GROUPS)
ATTN_WIDTH = N_DIL * GROUP_WIDTH
ROT_DIM = HEAD_DIM // 4
ROPE_THETA = 500000.0
PAST_LEN = 16384
CONV_K = 3
N_EXPERT_GROUPS = 4
EXPERTS_PER_GROUP = 4
N_EXPERTS = N_EXPERT_GROUPS * EXPERTS_PER_GROUP
D_EXPERT = 512
RMS_EPS = 1e-6
IN_COLS = 3 * D_MODEL + 3 * ATTN_WIDTH + 2 * D_MODEL
OFF_U, OFF_GC, OFF_GB = 0, D_MODEL, 2 * D_MODEL
OFF_Q = 3 * D_MODEL
OFF_K = OFF_Q + ATTN_WIDTH
OFF_V = OFF_K + ATTN_WIDTH
OFF_GCONV = OFF_V + ATTN_WIDTH
OFF_GATTN = OFF_GCONV + D_MODEL

LANES = 128
SUBLANES = 8
CHUNKS_PER_GROUP = GROUP_WIDTH // LANES
KEYS_PER_QUERY = 129
Q_BLOCK = 128
NEG_BIG = -1e30

VMEM_LIMIT = 56 * 1024 * 1024


def _sigmoid(x):
    return 1.0 / (1.0 + jnp.exp(-x))


def _cparams(sem):
    return pltpu.CompilerParams(dimension_semantics=sem, vmem_limit_bytes=VMEM_LIMIT)


def _adaln_kernel(c_ref, w_ref, b_ref, o_ref):
    c = c_ref[...]
    s = c * _sigmoid(c)
    o_ref[...] = jnp.dot(s, w_ref[...], precision=HIGHEST, preferred_element_type=F32) + b_ref[...]


def _adaln(c_all, w_ada, b_ada):
    rows = c_all.shape[0]
    n_col = w_ada.shape[1] // D_MODEL
    return pl.pallas_call(
        _adaln_kernel,
        grid=(n_col,),
        in_specs=[
            pl.BlockSpec((rows, D_MODEL), lambda j: (0, 0)),
            pl.BlockSpec((D_MODEL, D_MODEL), lambda j: (0, j)),
            pl.BlockSpec((1, D_MODEL), lambda j: (0, j)),
        ],
        out_specs=pl.BlockSpec((rows, D_MODEL), lambda j: (0, j)),
        out_shape=jax.ShapeDtypeStruct((rows, w_ada.shape[1]), F32),
        compiler_params=_cparams(("arbitrary",)),
        name="adaln",
    )(c_all, w_ada, b_ada.reshape(1, -1))


def _rope_tables(pos, scale):
    half = ROT_DIM // 2
    inv_freq = jnp.power(jnp.float32(ROPE_THETA), -jnp.arange(half, dtype=F32) / half)
    ang = pos.astype(F32)[:, None] * inv_freq[None, :]
    cos, sin = jnp.cos(ang), jnp.sin(ang)
    lane_in_head = np.arange(LANES) % HEAD_DIM
    freq = lane_in_head % half
    first = lane_in_head < half
    second = (lane_in_head >= half) & (lane_in_head < ROT_DIM)
    a = jnp.where(first | second, cos[:, freq], 1.0)
    bm = jnp.where(first, -sin[:, freq], 0.0)
    bp = jnp.where(second, sin[:, freq], 0.0)
    return jnp.stack([a, bm, bp]) * scale


def _rope_chunk(zc, a, bm, bp):
    return zc * a + pltpu.roll(zc, LANES - ROT_DIM // 2, 1) * bm + pltpu.roll(zc, ROT_DIM // 2, 1) * bp


def _modulated_norm_bf16(x_ref, mod_ref, g1_ref):
    x = x_ref[...]
    var = jnp.mean(x * x, axis=-1, keepdims=True)
    h = (x * lax.rsqrt(var + RMS_EPS)) * (g1_ref[...] * (1.0 + mod_ref[1])) + mod_ref[0]
    return h.astype(BF16)


def _convproj_kernel(x_ref, mod_ref, g1_ref, wa_ref, wg_ref, cw_ref, wco_ref, pc_ref, sga_ref, cst_ref, s_ref,
                     *, tm, n_tiles):
    i = pl.program_id(1)

    @pl.when(i == 0)
    def _():
        s_ref[0:SUBLANES, :] = jnp.zeros((SUBLANES, D_MODEL), F32)

    hb = _modulated_norm_bf16(x_ref, mod_ref, g1_ref)

    def proj(w_ref, lo):
        return jnp.dot(hb, w_ref[:, lo:lo + D_MODEL], preferred_element_type=F32)

    cu = proj(wa_ref, OFF_GC) * proj(wa_ref, OFF_U)
    s_ref[SUBLANES:SUBLANES + tm, :] = cu
    cw = cw_ref[...]
    conv = (cw[0:1] * s_ref[SUBLANES - 2:SUBLANES - 2 + tm, :]
            + cw[1:2] * s_ref[SUBLANES - 1:SUBLANES - 1 + tm, :]
            + cw[2:3] * cu)
    yc = jnp.dot((proj(wa_ref, OFF_GB) * conv).astype(BF16), wco_ref[...], preferred_element_type=F32)
    pc_ref[...] = (_sigmoid(proj(wg_ref, 0)) * yc).astype(pc_ref.dtype)
    sga_ref[...] = _sigmoid(proj(wg_ref, D_MODEL)).astype(sga_ref.dtype)
    s_ref[0:SUBLANES, :] = s_ref[tm:tm + SUBLANES, :]

    @pl.when(i == n_tiles - 1)
    def _():
        cst_ref[...] = s_ref[tm + SUBLANES - 2:tm + SUBLANES, :]


def _convproj_prompt(x, mod4, g1, wa_bf, wg_bf, conv_w, wco_bf, *, tm):
    bsz, seq, _ = x.shape
    n_tiles = seq // tm
    const2 = lambda b, i: (0, 0)
    tile3 = lambda b, i: (b, i, 0)
    return pl.pallas_call(
        functools.partial(_convproj_kernel, tm=tm, n_tiles=n_tiles),
        grid=(bsz, n_tiles),
        in_specs=[
            pl.BlockSpec((None, tm, D_MODEL), tile3),
            pl.BlockSpec((None, 6, 1, D_MODEL), lambda b, i: (b, 0, 0, 0)),
            pl.BlockSpec((1, D_MODEL), const2),
            pl.BlockSpec(wa_bf.shape, const2, pipeline_mode=pl.Buffered(1)),
            pl.BlockSpec(wg_bf.shape, const2, pipeline_mode=pl.Buffered(1)),
            pl.BlockSpec((CONV_K, D_MODEL), const2),
            pl.BlockSpec((D_MODEL, D_MODEL), const2, pipeline_mode=pl.Buffered(1)),
        ],
        out_specs=[pl.BlockSpec((None, tm, D_MODEL), tile3), pl.BlockSpec((None, tm, D_MODEL), tile3),
                   pl.BlockSpec((None, CONV_K - 1, D_MODEL), lambda b, i: (b, 0, 0))],
        out_shape=[jax.ShapeDtypeStruct((bsz, seq, D_MODEL), BF16), jax.ShapeDtypeStruct((bsz, seq, D_MODEL), BF16),
                   jax.ShapeDtypeStruct((bsz, CONV_K - 1, D_MODEL), F32)],
        scratch_shapes=[pltpu.VMEM((tm + SUBLANES, D_MODEL), F32)],
        compiler_params=_cparams(("arbitrary", "arbitrary")),
        name="convproj_prompt",
    )(x, mod4, g1, wa_bf, wg_bf, conv_w, wco_bf)


def _qkvproj_kernel(x_ref, mod_ref, g1_ref, w_ref, rq_ref, rk_ref, *rest, tm, n_tiles, seq):
    qkv_refs = rest[0:3 * N_DIL]
    st_refs = rest[3 * N_DIL:3 * N_DIL + 2 * N_DIL]
    d_ref = rest[-1]
    i = pl.program_id(1)
    hb = _modulated_norm_bf16(x_ref, mod_ref, g1_ref)

    def proj(lo, width):
        return jnp.dot(hb, w_ref[:, lo:lo + width], preferred_element_type=F32)

    zq = proj(0, ATTN_WIDTH)
    zk = proj(ATTN_WIDTH, ATTN_WIDTH)
    zv = proj(2 * ATTN_WIDTH, ATTN_WIDTH)
    aq, bmq, bpq = rq_ref[0], rq_ref[1], rq_ref[2]
    ak, bmk, bpk = rk_ref[0], rk_ref[1], rk_ref[2]
    n_chunks = ATTN_WIDTH // LANES
    q_chunks, k_chunks, v_chunks = [], [], []
    for c in range(n_chunks):
        sl = slice(c * LANES, (c + 1) * LANES)
        q_chunks.append(_rope_chunk(zq[:, sl], aq, bmq, bpq))
        k_chunks.append(_rope_chunk(zk[:, sl], ak, bmk, bpk))
        v_chunks.append(zv[:, sl])

    for which, chunks in enumerate((q_chunks, k_chunks, v_chunks)):
        for g in range(N_DIL):
            out_ref = qkv_refs[3 * g + which]
            dil = DIL_GROUPS[g][1]
            for cc in range(CHUNKS_PER_GROUP):
                c = g * CHUNKS_PER_GROUP + cc
                sl = slice(cc * LANES, (cc + 1) * LANES)
                if dil == 1:
                    out_ref[0, :, sl] = chunks[c].astype(BF16)
                else:
                    d_ref[c] = chunks[c]
                    for r in range(dil):
                        out_ref[r, :, sl] = d_ref[c, pl.ds(r, tm // dil, stride=dil), :].astype(BF16)

    for g in range(N_DIL):
        kst, vst = st_refs[2 * g], st_refs[2 * g + 1]
        win = min(DIL_GROUPS[g][0], seq)
        if win >= tm:
            cond, r0 = i >= (seq - win) // tm, 0
        else:
            cond, r0 = i == n_tiles - 1, tm - win

        @pl.when(cond)
        def _(g=g, kst=kst, vst=vst, r0=r0):
            for cc in range(CHUNKS_PER_GROUP):
                c = g * CHUNKS_PER_GROUP + cc
                kst[cc * LANES:(cc + 1) * LANES, :] = k_chunks[c][r0:, :].T
                vst[cc * LANES:(cc + 1) * LANES, :] = v_chunks[c][r0:, :].T


def _qkvproj_prompt(x, mod4, g1, wqkv_bf, rope_q, rope_k, *, tm):
    bsz, seq, _ = x.shape
    n_tiles = seq // tm
    const2 = lambda b, i: (0, 0)
    tile3 = lambda b, i: (b, i, 0)
    in_specs = [
        pl.BlockSpec((None, tm, D_MODEL), tile3),
        pl.BlockSpec((None, 6, 1, D_MODEL), lambda b, i: (b, 0, 0, 0)),
        pl.BlockSpec((1, D_MODEL), const2),
        pl.BlockSpec(wqkv_bf.shape, const2, pipeline_mode=pl.Buffered(1)),
        pl.BlockSpec((3, tm, LANES), lambda b, i: (0, i, 0)),
        pl.BlockSpec((3, tm, LANES), lambda b, i: (0, i, 0)),
    ]
    out_shape, out_specs = [], []
    for _, dil in DIL_GROUPS:
        assert tm % (dil * 16) == 0
        for _ in range(3):
            out_shape.append(jax.ShapeDtypeStruct((bsz, dil, seq // dil, GROUP_WIDTH), BF16))
            out_specs.append(pl.BlockSpec((None, dil, tm // dil, GROUP_WIDTH), lambda b, i: (b, 0, i, 0)))
    for win, _ in DIL_GROUPS:
        win = min(win, seq)
        cols = min(win, tm)
        if win >= tm:
            imap = lambda b, i, ft=(seq - win) // tm: (b, 0, jnp.maximum(i - ft, 0))
        else:
            imap = lambda b, i: (b, 0, 0)
        for _ in range(2):
            out_shape.append(jax.ShapeDtypeStruct((bsz, GROUP_WIDTH, win), F32))
            out_specs.append(pl.BlockSpec((None, GROUP_WIDTH, cols), imap))
    return pl.pallas_call(
        functools.partial(_qkvproj_kernel, tm=tm, n_tiles=n_tiles, seq=seq),
        grid=(bsz, n_tiles),
        in_specs=in_specs,
        out_specs=out_specs,
        out_shape=out_shape,
        scratch_shapes=[pltpu.VMEM((ATTN_WIDTH // LANES, tm, LANES), F32)],
        compiler_params=_cparams(("arbitrary", "arbitrary")),
        name="qkvproj_prompt",
    )(x, mod4, g1, wqkv_bf, rope_q, rope_k)


ATTN_UNROLL = 8


def _attn_kernel(*refs, chunk, has_halo, n_cls):
    if has_halo:
        q_ref, k_ref, v_ref, kh_ref, vh_ref, o_ref, lse_ref, kbuf, vbuf = refs
    else:
        q_ref, k_ref, v_ref, o_ref, lse_ref, kbuf, vbuf = refs
    c = pl.program_id(2)
    if has_halo:
        kbuf[:, 0:Q_BLOCK, :] = kh_ref[...]
        vbuf[:, 0:Q_BLOCK, :] = vh_ref[...]
    else:
        kbuf[:, 0:Q_BLOCK, :] = jnp.zeros((n_cls, Q_BLOCK, GROUP_WIDTH), BF16)
        vbuf[:, 0:Q_BLOCK, :] = jnp.zeros((n_cls, Q_BLOCK, GROUP_WIDTH), BF16)
    kbuf[:, Q_BLOCK:Q_BLOCK + chunk, :] = k_ref[...]
    vbuf[:, Q_BLOCK:Q_BLOCK + chunk, :] = v_ref[...]
    blocks_per_cls = chunk // Q_BLOCK

    row = lax.broadcasted_iota(jnp.int32, (Q_BLOCK, 2 * Q_BLOCK), 0)
    col = lax.broadcasted_iota(jnp.int32, (Q_BLOCK, 2 * Q_BLOCK), 1)
    bias_main = jnp.where(col >= row, jnp.where(col <= row + Q_BLOCK, 0.0, NEG_BIG), NEG_BIG)
    bias_first = jnp.where(col >= Q_BLOCK, bias_main, NEG_BIG)
    lane = lax.broadcasted_iota(jnp.int32, (1, LANES), 1)
    lo_half = lane < HEAD_DIM

    def body(u, carry):
        cls, qb = u // blocks_per_cls, u % blocks_per_cls
        r0 = pl.multiple_of(qb * Q_BLOCK, Q_BLOCK)
        kt = kbuf[cls, pl.ds(r0, 2 * Q_BLOCK), :]
        vt = vbuf[cls, pl.ds(r0, 2 * Q_BLOCK), :]
        qt = q_ref[cls, pl.ds(r0, Q_BLOCK), :]
        is_first = jnp.logical_and(qb == 0, c == 0)
        bias = jnp.where(is_first, bias_first, bias_main)
        bias2 = jnp.concatenate([bias, bias], axis=0)
        lse_tile = jnp.zeros((Q_BLOCK, LANES), F32)
        for p in range(CHUNKS_PER_GROUP):
            sl = slice(p * LANES, (p + 1) * LANES)
            qp, kp, vp = qt[:, sl], kt[:, sl], vt[:, sl]
            zero = jnp.zeros_like(qp)
            q2 = jnp.concatenate([jnp.where(lo_half, qp, zero), jnp.where(lo_half, zero, qp)], axis=0)
            s = lax.dot_general(q2, kp, (((1,), (1,)), ((), ())), preferred_element_type=F32) + bias2
            m = jnp.max(s, axis=1, keepdims=True)
            e = jnp.exp(s - m)
            l = jnp.sum(e, axis=1, keepdims=True)
            o = jnp.dot(e.astype(BF16), vp, preferred_element_type=F32) * (1.0 / l)
            lse = m + jnp.log(l)
            for hh in range(2):
                lse_tile = jnp.where(lane == 2 * p + hh, lse[hh * Q_BLOCK:(hh + 1) * Q_BLOCK], lse_tile)
            o_ref[cls, pl.ds(r0, Q_BLOCK), sl] = jnp.where(lo_half, o[:Q_BLOCK], o[Q_BLOCK:])
        lse_ref[cls, pl.ds(r0, Q_BLOCK), :] = lse_tile
        return carry

    n_units = n_cls * blocks_per_cls
    lax.fori_loop(0, n_units, body, 0, unroll=min(ATTN_UNROLL, n_units))


def _attn_prompt_group(q, k, v, g, *, max_rows):
    bsz, dil, cls_len, _ = q.shape
    chunk = min(max_rows, cls_len)
    n_chunks = cls_len // chunk
    n_cls = min(dil, max_rows // chunk)
    has_halo = n_chunks > 1
    main = pl.BlockSpec((None, n_cls, chunk, GROUP_WIDTH), lambda b, r, c: (b, r, c, 0))
    in_specs = [main, main, main]
    args = [q, k, v]
    if has_halo:
        per = chunk // Q_BLOCK
        halo = pl.BlockSpec((None, n_cls, Q_BLOCK, GROUP_WIDTH), lambda b, r, c: (b, r, jnp.maximum(c * per - 1, 0), 0))
        in_specs += [halo, halo]
        args += [k, v]
    return pl.pallas_call(
        functools.partial(_attn_kernel, chunk=chunk, has_halo=has_halo, n_cls=n_cls),
        grid=(bsz, dil // n_cls, n_chunks),
        in_specs=in_specs,
        out_specs=[pl.BlockSpec((None, n_cls, chunk, GROUP_WIDTH), lambda b, r, c: (b, r, c, 0)),
                   pl.BlockSpec((None, n_cls, chunk, LANES), lambda b, r, c: (b, r, c, 0))],
        out_shape=[jax.ShapeDtypeStruct((bsz, dil, cls_len, GROUP_WIDTH), F32),
                   jax.ShapeDtypeStruct((bsz, dil, cls_len, LANES), F32)],
        scratch_shapes=[pltpu.VMEM((n_cls, chunk + Q_BLOCK, GROUP_WIDTH), BF16),
                        pltpu.VMEM((n_cls, chunk + Q_BLOCK, GROUP_WIDTH), BF16)],
        compiler_params=_cparams(("arbitrary", "arbitrary", "arbitrary")),
        name=f"attn_prompt_g{g}",
    )(*args)


def _split_bf16(x, n):
    parts = []
    r = x
    for _ in range(n):
        p = r.astype(BF16)
        parts.append(p)
        r = r - p.astype(F32)
    return parts


def _store_token_rows(ref, value, n_tokens, first_chunk=0, rows_per_token=None):
    n_chunks = value.shape[1] // LANES
    rows_per_token = rows_per_token or n_chunks
    for c in range(n_chunks):
        ref[pl.ds(first_chunk + c, n_tokens, stride=rows_per_token), :] = value[:, c * LANES:(c + 1) * LANES]


def _load_token_rows(ref, n_tokens, n_chunks, first_chunk=0, rows_per_token=None):
    rows_per_token = rows_per_token or n_chunks
    return jnp.concatenate([ref[pl.ds(first_chunk + c, n_tokens, stride=rows_per_token), :] for c in range(n_chunks)],
                           axis=1)


H_ROWS = D_MODEL // LANES
Y_ROWS = 2 * D_MODEL // LANES

_PAIRS = ((0, 1), (0, 2), (0, 3), (1, 2), (1, 3), (2, 3))
N_CLASSES = N_EXPERT_GROUPS * len(_PAIRS)
META_CLASS, META_RANK, META_WA, META_WB = 0, 1, 2, 3


def _post_kernel(*refs, tm, dils, full_precision, routed):
    (o0_ref, o1_ref, o2_ref, l0_ref, l1_ref, l2_ref, pc_ref, sga_ref, x_ref, mod_ref,
     wao_ref, wo_ref, g2_ref, wr_ref, wrh_ref, wrl_ref, br_ref, exp_ref) = refs[:18]
    if routed:
        tri_ref, x1_ref, h2_ref, meta_ref, route_ref, cnt_ref, o_scr, l_scr, run_scr = refs[18:]

        @pl.when(jnp.logical_and(pl.program_id(0) == 0, pl.program_id(1) == 0))
        def _():
            run_scr[...] = jnp.zeros_like(run_scr)
    else:
        x1_ref, h2_ref, comb_ref, o_scr, l_scr = refs[18:]

    def mm(a, w_ref):
        if full_precision:
            return jnp.dot(a, w_ref[...], precision=HIGHEST, preferred_element_type=F32)
        return jnp.dot(a.astype(BF16), w_ref[...], preferred_element_type=F32)

    def natural_order(ref, scr, dil, n_chunks):
        if dil == 1:
            return [ref[0, :, c * LANES:(c + 1) * LANES] for c in range(n_chunks)]
        out = []
        for c in range(n_chunks):
            for r in range(dil):
                scr[c, pl.ds(r, tm // dil, stride=dil), :] = ref[r, :, c * LANES:(c + 1) * LANES]
            out.append(scr[c])
        return out

    lses = [natural_order(ref, l_scr.at[g], dils[g], 1)[0] for g, ref in enumerate((l0_ref, l1_ref, l2_ref))]
    mx = jnp.maximum(lses[0], jnp.maximum(lses[1], lses[2]))
    es = [jnp.exp(v - mx) for v in lses]
    inv = 1.0 / (es[0] + es[1] + es[2])
    expand = exp_ref[...]
    attn_o = None
    for g, o_ref in enumerate((o0_ref, o1_ref, o2_ref)):
        w = es[g] * inv
        we = None
        for part in _split_bf16(w, 3 if full_precision else 2):
            t = jnp.dot(part, expand, preferred_element_type=F32)
            we = t if we is None else we + t
        o_nat = jnp.concatenate(natural_order(o_ref, o_scr.at[g], dils[g], CHUNKS_PER_GROUP), axis=1)
        term = we * o_nat
        attn_o = term if attn_o is None else attn_o + term

    y_attn = mm(attn_o, wao_ref)
    mixed = mm(pc_ref[...] + sga_ref[...] * y_attn, wo_ref)
    x1 = x_ref[...] + mod_ref[2] * mixed
    x1_ref[...] = x1
    var = jnp.mean(x1 * x1, axis=-1, keepdims=True)
    h2 = (x1 * lax.rsqrt(var + RMS_EPS)) * (g2_ref[...] * (1.0 + mod_ref[4])) + mod_ref[3]
    if routed:
        _store_token_rows(h2_ref, h2, tm)
    else:
        h2_ref[...] = h2.astype(h2_ref.dtype)

    if full_precision:
        lg = jnp.dot(h2, wr_ref[...], precision=HIGHEST, preferred_element_type=F32) + br_ref[...]
    else:
        h_hi, h_lo = _split_bf16(h2, 2)
        lg = (jnp.dot(h_hi, wrh_ref[...], preferred_element_type=F32)
              + jnp.dot(h_lo, wrh_ref[...], preferred_element_type=F32)
              + jnp.dot(h_hi, wrl_ref[...], preferred_element_type=F32)) + br_ref[...]
    lane_i = lax.broadcasted_iota(jnp.int32, (1, LANES), 1)
    lane = lane_i.astype(F32)
    lane_group = ((lane_i - N_EXPERT_GROUPS) >> 2).astype(F32)
    big = jnp.float32(1e9)
    gl = jnp.where(lane_i < N_EXPERT_GROUPS, lg, NEG_BIG)
    gmax = jnp.max(gl, axis=1, keepdims=True)
    gidx = jnp.min(jnp.where(gl == gmax, lane, big), axis=1, keepdims=True)
    g_w = 1.0 / jnp.sum(jnp.exp(gl - gmax), axis=1, keepdims=True)
    el = jnp.where(lane_group == gidx, lg, NEG_BIG)
    v1 = jnp.max(el, axis=1, keepdims=True)
    i1 = jnp.min(jnp.where(el == v1, lane, big), axis=1, keepdims=True)
    el2 = jnp.where(lane == i1, NEG_BIG, el)
    v2 = jnp.max(el2, axis=1, keepdims=True)
    i2 = jnp.min(jnp.where(el2 == v2, lane, big), axis=1, keepdims=True)
    t = jnp.exp(v2 - v1)
    den = 1.0 / (1.0 + t)
    w_top1, w_top2 = g_w * den, g_w * (t * den)
    if not routed:
        comb_ref[...] = jnp.where(lane == i1, w_top1, jnp.where(lane == i2, w_top2, 0.0))
        return

    base = jnp.float32(N_EXPERT_GROUPS) + jnp.float32(EXPERTS_PER_GROUP) * gidx
    e1, e2 = i1 - base, i2 - base
    first_is_lower = e1 < e2
    ea = jnp.where(first_is_lower, e1, e2)
    eb = jnp.where(first_is_lower, e2, e1)
    pair = ea * (7.0 - ea) * 0.5 + (eb - ea - 1.0)
    cls = gidx * jnp.float32(len(_PAIRS)) + pair
    onehot = lane == cls
    earlier = jnp.dot(tri_ref[...], jnp.where(onehot, 1.0, 0.0).astype(BF16), preferred_element_type=F32)
    running = run_scr[...]
    rank = jnp.sum(jnp.where(onehot, earlier + running, 0.0), axis=1, keepdims=True)
    running = running + jnp.sum(jnp.where(onehot, 1.0, 0.0), axis=0, keepdims=True)
    run_scr[...] = running
    cnt_ref[...] = running
    w_a = jnp.where(first_is_lower, w_top1, w_top2)
    w_b = jnp.where(first_is_lower, w_top2, w_top1)
    meta = jnp.where(lane_i == META_CLASS, cls,
                     jnp.where(lane_i == META_RANK, rank,
                               jnp.where(lane_i == META_WA, w_a, jnp.where(lane_i == META_WB, w_b, 0.0))))
    meta_ref[...] = meta
    route_ref[...] = meta.T[0:SUBLANES, :].astype(jnp.int32)


def _post(o_list, lse_list, pc, sga, x, mod4, wao, wo, g2, wr, br, expand, *, tm, full_precision, routed):
    nb, rows, _ = x.shape
    tiles = rows // tm
    dils = tuple(o.shape[1] for o in o_list)
    tile3 = lambda b, i: (b, i, 0)
    const2 = lambda b, i: (0, 0)
    mod_rows = mod4.shape[2]
    if mod_rows == 1:
        mod_spec = pl.BlockSpec((None, 6, 1, D_MODEL), lambda b, i: (b, 0, 0, 0))
    else:
        mod_spec = pl.BlockSpec((None, 6, tm, D_MODEL), lambda b, i: (b, 0, i, 0))
    cls4 = lambda b, i: (b, 0, i, 0)
    in_specs = (
        [pl.BlockSpec((None, d, tm // d, GROUP_WIDTH), cls4) for d in dils]
        + [pl.BlockSpec((None, d, tm // d, LANES), cls4) for d in dils]
        + [pl.BlockSpec((None, tm, D_MODEL), tile3)] * 3 + [mod_spec]
        + [pl.BlockSpec(wao.shape, const2), pl.BlockSpec(wo.shape, const2), pl.BlockSpec((1, D_MODEL), const2)]
        + [pl.BlockSpec((D_MODEL, LANES), const2)] * 3
        + [pl.BlockSpec((1, LANES), const2), pl.BlockSpec((LANES, GROUP_WIDTH), const2)]
    )
    wr_hi = wr.astype(BF16)
    wr_lo = (wr - wr_hi.astype(F32)).astype(BF16)
    args = [*o_list, *lse_list, pc, sga, x, mod4, wao, wo, g2, wr, wr_hi, wr_lo, br, expand]
    scratch = [pltpu.VMEM((N_DIL, CHUNKS_PER_GROUP, tm, LANES), F32), pltpu.VMEM((N_DIL, 1, tm, LANES), F32)]
    if routed:
        tri = jnp.asarray(np.tril(np.ones((tm, tm), np.float32), -1), dtype=BF16)
        args.append(tri)
        in_specs = in_specs + [pl.BlockSpec((tm, tm), const2)]
        out_specs = [pl.BlockSpec((None, tm, D_MODEL), tile3), pl.BlockSpec((None, tm * H_ROWS, LANES), tile3),
                     pl.BlockSpec((None, tm, LANES), tile3),
                     pl.BlockSpec((None, None, SUBLANES, tm), lambda b, i: (b, i, 0, 0)),
                     pl.BlockSpec((1, LANES), const2)]
        out_shape = [jax.ShapeDtypeStruct((nb, rows, D_MODEL), F32),
                     jax.ShapeDtypeStruct((nb, rows * H_ROWS, LANES), F32),
                     jax.ShapeDtypeStruct((nb, rows, LANES), F32),
                     jax.ShapeDtypeStruct((nb, tiles, SUBLANES, tm), jnp.int32),
                     jax.ShapeDtypeStruct((1, LANES), F32)]
        scratch.append(pltpu.VMEM((1, LANES), F32))
    else:
        out_specs = [pl.BlockSpec((None, tm, D_MODEL), tile3), pl.BlockSpec((None, tm, D_MODEL), tile3),
                     pl.BlockSpec((None, tm, LANES), tile3)]
        out_shape = [jax.ShapeDtypeStruct((nb, rows, D_MODEL), F32), jax.ShapeDtypeStruct((nb, rows, D_MODEL), F32),
                     jax.ShapeDtypeStruct((nb, rows, LANES), F32)]
    return pl.pallas_call(
        functools.partial(_post_kernel, tm=tm, dils=dils, full_precision=full_precision, routed=routed),
        grid=(nb, tiles),
        in_specs=in_specs,
        out_specs=out_specs,
        out_shape=out_shape,
        scratch_shapes=scratch,
        compiler_params=_cparams(("arbitrary", "arbitrary")),
        name="post_sample" if full_precision else "post_prompt",
    )(*args)


DMA_LOOP_UNROLL = 8
DMA_PRIORITIES = 2


ROUTE_CLASS_ROW, ROUTE_RANK_ROW = META_CLASS, META_RANK


def _sorted_slot(starts_ref, route_ref, r):
    return starts_ref[route_ref[ROUTE_CLASS_ROW, r]] + route_ref[ROUTE_RANK_ROW, r]


def _token_rows(token, rows_per_token):
    return pl.ds(pl.multiple_of(token * rows_per_token, rows_per_token), rows_per_token)


def _dispatch_kernel(starts_ref, route_ref, src_ref, dst_hbm, buf, sems, *, rows_per_step, n_steps):
    i = pl.program_id(0)
    slot = i % 2

    def wait_slot(s):
        def body(r, carry):
            pltpu.make_async_copy(buf.at[s, _token_rows(0, H_ROWS)], dst_hbm.at[_token_rows(0, H_ROWS)],
                                  sems.at[s * rows_per_step + r]).wait()
            return carry
        lax.fori_loop(0, rows_per_step, body, 0, unroll=DMA_LOOP_UNROLL)

    @pl.when(i >= 1)
    def _():
        wait_slot(1 - slot)

    buf[slot] = src_ref[...]

    def issue(pair, carry):
        for prio in range(DMA_PRIORITIES):
            r = pair * DMA_PRIORITIES + prio
            pltpu.make_async_copy(buf.at[slot, _token_rows(r, H_ROWS)],
                                  dst_hbm.at[_token_rows(_sorted_slot(starts_ref, route_ref, r), H_ROWS)],
                                  sems.at[slot * rows_per_step + r]).start(priority=prio)
        return carry

    lax.fori_loop(0, rows_per_step // DMA_PRIORITIES, issue, 0, unroll=DMA_LOOP_UNROLL // DMA_PRIORITIES)

    @pl.when(i == n_steps - 1)
    def _():
        wait_slot(slot)


def _route_spec(rows_per_step, route_tile, shift=0, n_steps=None):
    per_tile = route_tile // rows_per_step

    def imap(i, starts):
        s = i + shift if n_steps is None else jnp.minimum(i + shift, n_steps - 1)
        return (s // per_tile, 0, s % per_tile)

    return pl.BlockSpec((None, SUBLANES, rows_per_step), imap, memory_space=pltpu.SMEM)


def _dispatch_rows(src, route, starts, *, rows_per_step):
    n = src.shape[0] // H_ROWS
    n_steps = n // rows_per_step
    grid_spec = pltpu.PrefetchScalarGridSpec(
        num_scalar_prefetch=1,
        grid=(n_steps,),
        in_specs=[_route_spec(rows_per_step, route.shape[2]),
                  pl.BlockSpec((rows_per_step * H_ROWS, LANES), lambda i, starts: (i, 0))],
        out_specs=pl.BlockSpec(memory_space=pl.ANY),
        scratch_shapes=[pltpu.VMEM((2, rows_per_step * H_ROWS, LANES), src.dtype),
                        pltpu.SemaphoreType.DMA((2 * rows_per_step,))],
    )
    return pl.pallas_call(
        functools.partial(_dispatch_kernel, rows_per_step=rows_per_step, n_steps=n_steps),
        grid_spec=grid_spec,
        out_shape=jax.ShapeDtypeStruct(src.shape, src.dtype),
        compiler_params=_cparams(("arbitrary",)),
        name="moe_dispatch",
    )(starts, route, src)


def _experts_kernel(tile_ref, ea_ref, eb_ref, lo_ref, hi_ref, first_ref, switch_ref, nv_ref,
                    x_ref, w1a_ref, w1b_ref, w3a_ref, w3b_ref, w2a_ref, w2b_ref, y_ref, w13_scr, w2_scr, *, tm):
    i = pl.program_id(0)

    @pl.when(i < nv_ref[0])
    def _():
        @pl.when(switch_ref[i] == 1)
        def _():
            for k, w_ref in enumerate((w1a_ref, w3a_ref, w1b_ref, w3b_ref)):
                w13_scr[k] = w_ref[...].astype(BF16)
            for k, w_ref in enumerate((w2a_ref, w2b_ref)):
                w2_scr[k] = w_ref[...].astype(BF16)

        def expert(x, k):
            a = jnp.dot(x, w13_scr[2 * k], preferred_element_type=F32)
            b = jnp.dot(x, w13_scr[2 * k + 1], preferred_element_type=F32)
            hid = (a * _sigmoid(a)) * b
            return jnp.dot(hid.astype(BF16), w2_scr[k], preferred_element_type=F32)

        @pl.when(first_ref[i] == 1)
        def _():
            x = _load_token_rows(x_ref, tm, H_ROWS).astype(BF16)
            _store_token_rows(y_ref, expert(x, 0), tm, first_chunk=0, rows_per_token=Y_ROWS)
            _store_token_rows(y_ref, expert(x, 1), tm, first_chunk=H_ROWS, rows_per_token=Y_ROWS)

        @pl.when(first_ref[i] == 0)
        def _():
            x = _load_token_rows(x_ref, tm, H_ROWS).astype(BF16)
            row = lax.broadcasted_iota(jnp.int32, (tm, 1), 0)
            for k in range(2):
                old = _load_token_rows(y_ref, tm, H_ROWS, first_chunk=k * H_ROWS, rows_per_token=Y_ROWS)
                new = jnp.where(row >= lo_ref[i], jnp.where(row < hi_ref[i], expert(x, k), old), old)
                _store_token_rows(y_ref, new, tm, first_chunk=k * H_ROWS, rows_per_token=Y_ROWS)


def _experts_routed(hs, tables, w1, w3, w2, *, tm):
    n = hs.shape[0] // H_ROWS
    n_items = tables[0].shape[0]
    x_map = lambda i, tile, ea, eb, lo, hi, first, switch, nv: (tile[i], 0)
    wa_map = lambda i, tile, ea, eb, lo, hi, first, switch, nv: (ea[i], 0, 0)
    wb_map = lambda i, tile, ea, eb, lo, hi, first, switch, nv: (eb[i], 0, 0)
    w13 = lambda m: pl.BlockSpec((None, D_MODEL, D_EXPERT), m)
    w2s = lambda m: pl.BlockSpec((None, D_EXPERT, D_MODEL), m)
    grid_spec = pltpu.PrefetchScalarGridSpec(
        num_scalar_prefetch=len(tables),
        grid=(n_items,),
        in_specs=[pl.BlockSpec((tm * H_ROWS, LANES), x_map),
                  w13(wa_map), w13(wb_map), w13(wa_map), w13(wb_map), w2s(wa_map), w2s(wb_map)],
        out_specs=pl.BlockSpec((tm * Y_ROWS, LANES), x_map),
        scratch_shapes=[pltpu.VMEM((4, D_MODEL, D_EXPERT), BF16), pltpu.VMEM((2, D_EXPERT, D_MODEL), BF16)],
    )
    return pl.pallas_call(
        functools.partial(_experts_kernel, tm=tm),
        grid_spec=grid_spec,
        out_shape=jax.ShapeDtypeStruct((n * Y_ROWS, LANES), F32),
        compiler_params=_cparams(("arbitrary",)),
        name="moe_experts",
    )(*tables, hs, w1, w1, w3, w3, w2, w2)


def _combine_kernel(starts_ref, route_ref, route_next_ref, yp_hbm, x1_ref, meta_ref, mod_ref, gf_ref, y_ref, ybuf, sems,
                    *, rows_per_step, n_steps):
    i = pl.program_id(0)
    slot = i % 2

    def issue(r_ref, s):
        def body(pair, carry):
            for prio in range(DMA_PRIORITIES):
                r = pair * DMA_PRIORITIES + prio
                pltpu.make_async_copy(yp_hbm.at[_token_rows(_sorted_slot(starts_ref, r_ref, r), Y_ROWS)],
                                      ybuf.at[s, :, pl.ds(r, 1), :],
                                      sems.at[s * rows_per_step + r]).start(priority=prio)
            return carry
        lax.fori_loop(0, rows_per_step // DMA_PRIORITIES, body, 0, unroll=DMA_LOOP_UNROLL // DMA_PRIORITIES)

    @pl.when(i == 0)
    def _():
        issue(route_ref, 0)

    @pl.when(i + 1 < n_steps)
    def _():
        issue(route_next_ref, 1 - slot)

    def wait_body(r, carry):
        pltpu.make_async_copy(yp_hbm.at[_token_rows(0, Y_ROWS)], ybuf.at[slot, :, pl.ds(r, 1), :],
                              sems.at[slot * rows_per_step + r]).wait()
        return carry

    lax.fori_loop(0, rows_per_step, wait_body, 0, unroll=DMA_LOOP_UNROLL)
    ya = jnp.concatenate([ybuf[slot, c] for c in range(H_ROWS)], axis=1)
    yb = jnp.concatenate([ybuf[slot, H_ROWS + c] for c in range(H_ROWS)], axis=1)
    meta = meta_ref[...]
    lane = lax.broadcasted_iota(jnp.int32, (1, LANES), 1)
    w_a = jnp.sum(jnp.where(lane == META_WA, meta, 0.0), axis=1, keepdims=True)
    w_b = jnp.sum(jnp.where(lane == META_WB, meta, 0.0), axis=1, keepdims=True)
    x2 = x1_ref[...] + mod_ref[5] * (w_a * ya + w_b * yb)
    var = jnp.mean(x2 * x2, axis=-1, keepdims=True)
    y_ref[...] = (x2 * lax.rsqrt(var + RMS_EPS)) * gf_ref[...]


def _combine_final(yp, route, starts, x1, meta, mod4, gf, *, rows_per_step, rows_per_mod):
    n = x1.shape[0]
    n_steps = n // rows_per_step
    steps_per_mod = rows_per_mod // rows_per_step
    row = lambda i, starts: (i, 0)
    grid_spec = pltpu.PrefetchScalarGridSpec(
        num_scalar_prefetch=1,
        grid=(n_steps,),
        in_specs=[_route_spec(rows_per_step, route.shape[2]),
                  _route_spec(rows_per_step, route.shape[2], shift=1, n_steps=n_steps),
                  pl.BlockSpec(memory_space=pl.ANY),
                  pl.BlockSpec((rows_per_step, D_MODEL), row),
                  pl.BlockSpec((rows_per_step, LANES), row),
                  pl.BlockSpec((None, 6, 1, D_MODEL), lambda i, starts: (i // steps_per_mod, 0, 0, 0)),
                  pl.BlockSpec((1, D_MODEL), lambda i, starts: (0, 0))],
        out_specs=pl.BlockSpec((rows_per_step, D_MODEL), row),
        scratch_shapes=[pltpu.VMEM((2, Y_ROWS, rows_per_step, LANES), F32),
                        pltpu.SemaphoreType.DMA((2 * rows_per_step,))],
    )
    yp_rows = yp.reshape(yp.shape[0], 1, LANES)
    return pl.pallas_call(
        functools.partial(_combine_kernel, rows_per_step=rows_per_step, n_steps=n_steps),
        grid_spec=grid_spec,
        out_shape=jax.ShapeDtypeStruct((n, D_MODEL), F32),
        compiler_params=_cparams(("arbitrary",)),
        name="moe_combine",
    )(starts, route, route, yp_rows, x1, meta, mod4, gf)


def _routing_tables(counts, *, n, tm):
    counts = counts.astype(jnp.int32)
    ends = jnp.cumsum(counts)
    starts = ends - counts
    n_items = n // tm + N_CLASSES
    first_tile = starts // tm
    last_tile = jnp.maximum(ends - 1, starts) // tm
    visits = jnp.where(counts > 0, last_tile - first_tile + 1, 0)
    item_end = jnp.cumsum(visits)
    item_start = item_end - visits
    n_valid = item_end[-1]
    item = jnp.arange(n_items, dtype=jnp.int32)
    idx = jnp.minimum(item, n_valid - 1)
    c = jnp.sum((idx[:, None] >= item_end[None, :]).astype(jnp.int32), axis=1)
    class_ids = jnp.arange(N_CLASSES, dtype=jnp.int32)

    def pick(table):
        return jnp.sum(jnp.where(c[:, None] == class_ids[None, :], table[None, :], 0), axis=1)

    tile = pick(first_tile) + (idx - pick(item_start))
    live = item < n_valid
    lo = jnp.where(live, jnp.clip(pick(starts) - tile * tm, 0, tm), 0)
    hi = jnp.where(live, jnp.clip(pick(ends) - tile * tm, 0, tm), 0)
    prev_tile = jnp.concatenate([jnp.full((1,), -1, jnp.int32), tile[:-1]])
    first = (tile != prev_tile).astype(jnp.int32)
    prev_c = jnp.concatenate([jnp.full((1,), -1, jnp.int32), c[:-1]])
    switch = (c != prev_c).astype(jnp.int32)
    group_of_class = np.arange(N_CLASSES) // len(_PAIRS)
    pair_of_class = np.arange(N_CLASSES) % len(_PAIRS)
    pairs = np.asarray(_PAIRS)
    ea = pick(jnp.asarray(group_of_class * EXPERTS_PER_GROUP + pairs[pair_of_class, 0], dtype=jnp.int32))
    eb = pick(jnp.asarray(group_of_class * EXPERTS_PER_GROUP + pairs[pair_of_class, 1], dtype=jnp.int32))
    as_i32 = lambda a: a.astype(jnp.int32)
    return as_i32(starts), tuple(as_i32(a) for a in (tile, ea, eb, lo, hi, first, switch, n_valid.reshape(1)))


def _moe_kernel(h_ref, comb_ref, x1_ref, mod_ref, gf_ref, w1_ref, w3_ref, w2_ref, y_ref, acc_ref):
    e = pl.program_id(2)

    @pl.when(e == 0)
    def _():
        acc_ref[...] = jnp.zeros_like(acc_ref)

    h = h_ref[...].astype(BF16)
    a = jnp.dot(h, w1_ref[...].astype(BF16), preferred_element_type=F32)
    b = jnp.dot(h, w3_ref[...].astype(BF16), preferred_element_type=F32)
    lane = lax.broadcasted_iota(jnp.int32, (1, LANES), 1)
    cw = jnp.sum(jnp.where(lane == e + N_EXPERT_GROUPS, comb_ref[...], 0.0), axis=1, keepdims=True)
    hid = (a * _sigmoid(a)) * b * cw
    acc_ref[...] += jnp.dot(hid.astype(BF16), w2_ref[...].astype(BF16), preferred_element_type=F32)

    @pl.when(e == N_EXPERTS - 1)
    def _():
        x2 = x1_ref[...] + mod_ref[5] * acc_ref[...]
        var = jnp.mean(x2 * x2, axis=-1, keepdims=True)
        y_ref[...] = (x2 * lax.rsqrt(var + RMS_EPS)) * gf_ref[...]


def _moe_dense(h2, comb, x1, mod4, gf, w1, w3, w2, *, tm):
    nb, rows, _ = x1.shape
    tiles = rows // tm
    tile3 = lambda b, i, e: (b, i, 0)
    if mod4.shape[2] == 1:
        mod_spec = pl.BlockSpec((None, 6, 1, D_MODEL), lambda b, i, e: (b, 0, 0, 0))
    else:
        mod_spec = pl.BlockSpec((None, 6, tm, D_MODEL), lambda b, i, e: (b, 0, i, 0))
    return pl.pallas_call(
        _moe_kernel,
        grid=(nb, tiles, N_EXPERTS),
        in_specs=[
            pl.BlockSpec((None, tm, D_MODEL), tile3),
            pl.BlockSpec((None, tm, LANES), tile3),
            pl.BlockSpec((None, tm, D_MODEL), tile3),
            mod_spec,
            pl.BlockSpec((1, D_MODEL), lambda b, i, e: (0, 0)),
            pl.BlockSpec((None, D_MODEL, D_EXPERT), lambda b, i, e: (e, 0, 0)),
            pl.BlockSpec((None, D_MODEL, D_EXPERT), lambda b, i, e: (e, 0, 0)),
            pl.BlockSpec((None, D_EXPERT, D_MODEL), lambda b, i, e: (e, 0, 0)),
        ],
        out_specs=pl.BlockSpec((None, tm, D_MODEL), tile3),
        out_shape=jax.ShapeDtypeStruct((nb, rows, D_MODEL), F32),
        scratch_shapes=[pltpu.VMEM((tm, D_MODEL), F32)],
        compiler_params=_cparams(("arbitrary", "arbitrary", "arbitrary")),
        name=f"moe_dense_{nb * rows}",
    )(h2, comb, x1, mod4, gf, w1, w3, w2)


def _s_inproj_kernel(x_ref, mod_ref, g1_ref, w_ref, z_ref):
    x = x_ref[...]
    var = jnp.mean(x * x, axis=-1, keepdims=True)
    h = (x * lax.rsqrt(var + RMS_EPS)) * (g1_ref[...] * (1.0 + mod_ref[1])) + mod_ref[0]
    h_hi, h_lo = _split_bf16(h, 2)
    w_hi, w_lo = _split_bf16(w_ref[...], 2)
    z_ref[...] = (jnp.dot(h_hi, w_hi, preferred_element_type=F32) + jnp.dot(h_lo, w_hi, preferred_element_type=F32)
                  + jnp.dot(h_hi, w_lo, preferred_element_type=F32))


def _s_inproj(x, mod_tok, g1, w_in, *, col_block):
    n = x.shape[0]
    return pl.pallas_call(
        _s_inproj_kernel,
        grid=(IN_COLS // col_block,),
        in_specs=[pl.BlockSpec((n, D_MODEL), lambda j: (0, 0)),
                  pl.BlockSpec((6, n, D_MODEL), lambda j: (0, 0, 0)),
                  pl.BlockSpec((1, D_MODEL), lambda j: (0, 0)),
                  pl.BlockSpec((D_MODEL, col_block), lambda j: (0, j))],
        out_specs=pl.BlockSpec((n, col_block), lambda j: (0, j)),
        out_shape=jax.ShapeDtypeStruct((n, IN_COLS), F32),
        compiler_params=_cparams(("arbitrary",)),
        name="inproj_sample",
    )(x, mod_tok, g1, w_in)


def _s_mid_kernel(z_ref, p0_ref, p1_ref, cw_ref, wco_ref, rq_ref, rk_ref,
                  cu_ref, pc_ref, sga_ref, q_ref, k_ref, v_ref, *, t_len):
    n = z_ref.shape[0]
    cu = z_ref[:, OFF_GC:OFF_GC + D_MODEL] * z_ref[:, OFF_U:OFF_U + D_MODEL]
    cu_ref[...] = cu
    t = lax.broadcasted_iota(jnp.int32, (n, D_MODEL), 0) & (t_len - 1)
    prev1 = jnp.where(t >= 1, pltpu.roll(cu, 1, 0), p1_ref[...])
    prev2 = jnp.where(t >= 2, pltpu.roll(cu, 2, 0), jnp.where(t == 0, p0_ref[...], p1_ref[...]))
    cw = cw_ref[...]
    conv = cw[0:1] * prev2 + cw[1:2] * prev1 + cw[2:3] * cu
    yc = jnp.dot(z_ref[:, OFF_GB:OFF_GB + D_MODEL] * conv, wco_ref[...], precision=HIGHEST,
                 preferred_element_type=F32)
    pc_ref[...] = _sigmoid(z_ref[:, OFF_GCONV:OFF_GCONV + D_MODEL]) * yc
    sga_ref[...] = _sigmoid(z_ref[:, OFF_GATTN:OFF_GATTN + D_MODEL])
    aq, bmq, bpq = rq_ref[0], rq_ref[1], rq_ref[2]
    ak, bmk, bpk = rk_ref[0], rk_ref[1], rk_ref[2]
    for c in range(ATTN_WIDTH // LANES):
        sl = slice(c * LANES, (c + 1) * LANES)
        q_ref[:, sl] = _rope_chunk(z_ref[:, OFF_Q + c * LANES:OFF_Q + (c + 1) * LANES], aq, bmq, bpq)
        k_ref[:, sl] = _rope_chunk(z_ref[:, OFF_K + c * LANES:OFF_K + (c + 1) * LANES], ak, bmk, bpk)
    v_ref[...] = z_ref[:, OFF_V:OFF_V + ATTN_WIDTH]


def _s_mid(z, p0e, p1e, conv_w, w_conv_out, rope_q, rope_k, *, t_len):
    n = z.shape[0]
    assert t_len & (t_len - 1) == 0
    full = lambda shape: pl.BlockSpec(shape, lambda i: (0,) * len(shape))
    out_shape = [jax.ShapeDtypeStruct((n, D_MODEL), F32)] * 3 + [jax.ShapeDtypeStruct((n, ATTN_WIDTH), F32)] * 3
    return pl.pallas_call(
        functools.partial(_s_mid_kernel, t_len=t_len),
        grid=(1,),
        in_specs=[full(z.shape), full(p0e.shape), full(p1e.shape), full(conv_w.shape), full(w_conv_out.shape),
                  full(rope_q.shape), full(rope_k.shape)],
        out_specs=[full((n, D_MODEL))] * 3 + [full((n, ATTN_WIDTH))] * 3,
        out_shape=out_shape,
        compiler_params=_cparams(("arbitrary",)),
        name="mid_sample",
    )(z, p0e, p1e, conv_w, w_conv_out, rope_q, rope_k)


def _head_sum(x):
    return jnp.sum(x.reshape(HEADS_PER_GROUP, HEAD_DIM, x.shape[-1]), axis=1)


def _head_expand(x):
    n = x.shape[-1]
    return jnp.broadcast_to(x[:, None, :], (HEADS_PER_GROUP, HEAD_DIM, n)).reshape(GROUP_WIDTH, n)


def _s_attn_disjoint(q_ref, kn, vn, ck_ref, cv_ref, o_ref, lse_ref, s_scr, lane, new_idx, *, n_tiles, dil, t_len):
    cls = lane & (dil - 1)
    q_all = q_ref[...]
    qsel = jnp.zeros((GROUP_WIDTH, LANES), F32)
    for t in range(t_len):
        qsel = jnp.where(cls == t, jnp.broadcast_to(q_all[:, t:t + 1], (GROUP_WIDTH, LANES)), qsel)
    smax = None
    for j in range(n_tiles):
        s = jnp.where(cls < t_len, _head_sum(ck_ref[:, j * LANES:(j + 1) * LANES] * qsel), NEG_BIG)
        s_scr[:, j * LANES:(j + 1) * LANES] = s
        smax = s if smax is None else jnp.maximum(smax, s)
    s_new = _head_sum(kn * q_all)
    m_cols = []
    m_lane = jnp.zeros((HEADS_PER_GROUP, LANES), F32)
    m_new = jnp.zeros((HEADS_PER_GROUP, t_len), F32)
    for t in range(t_len):
        mt = jnp.maximum(jnp.max(jnp.where(cls == t, smax, NEG_BIG), axis=1, keepdims=True), s_new[:, t:t + 1])
        m_cols.append(mt)
        m_lane = jnp.where(cls == t, mt, m_lane)
        m_new = jnp.where(new_idx == t, mt, m_new)
    esum = jnp.zeros((HEADS_PER_GROUP, LANES), F32)
    for j in range(n_tiles):
        e = jnp.exp(s_scr[:, j * LANES:(j + 1) * LANES] - m_lane)
        s_scr[:, j * LANES:(j + 1) * LANES] = e
        esum = esum + e
    e_new = jnp.exp(s_new - m_new)
    l_cols = []
    inv_lane = jnp.zeros((HEADS_PER_GROUP, LANES), F32)
    inv_new = jnp.zeros((HEADS_PER_GROUP, t_len), F32)
    for t in range(t_len):
        lt = jnp.sum(jnp.where(cls == t, esum, 0.0), axis=1, keepdims=True) + e_new[:, t:t + 1]
        l_cols.append(lt)
        inv_lane = jnp.where(cls == t, 1.0 / lt, inv_lane)
        inv_new = jnp.where(new_idx == t, 1.0 / lt, inv_new)
    acc = None
    for j in range(n_tiles):
        term = cv_ref[:, j * LANES:(j + 1) * LANES] * _head_expand(s_scr[:, j * LANES:(j + 1) * LANES] * inv_lane)
        acc = term if acc is None else acc + term
    o_new = vn * _head_expand(e_new * inv_new)
    for t in range(t_len):
        o_ref[:, t:t + 1] = jnp.sum(jnp.where(cls == t, acc, 0.0), axis=1, keepdims=True) + o_new[:, t:t + 1]
        lse_ref[:, t:t + 1] = m_cols[t] + jnp.log(l_cols[t])


def _s_attn_kernel(q_ref, kn_ref, vn_ref, ck_ref, cv_ref, o_ref, lse_ref, ko_ref, vo_ref, s_scr,
                   *, win, dil, t_len):
    n_tiles = win // LANES
    lane = lax.broadcasted_iota(jnp.int32, (1, LANES), 1)
    new_idx = lax.broadcasted_iota(jnp.int32, (1, t_len), 1)
    kn = kn_ref[...]
    vn = vn_ref[...]
    if dil >= t_len:
        _s_attn_disjoint(q_ref, kn, vn, ck_ref, cv_ref, o_ref, lse_ref, s_scr, lane, new_idx,
                         n_tiles=n_tiles, dil=dil, t_len=t_len)
    for t in range(t_len if dil < t_len else 0):
        qb = jnp.broadcast_to(q_ref[:, t:t + 1], (GROUP_WIDTH, LANES))
        m = None
        for j in range(n_tiles):
            pos = lane + j * LANES
            s = _head_sum(ck_ref[:, j * LANES:(j + 1) * LANES] * qb)
            s = jnp.where(pos >= t, jnp.where(((pos - t) & (dil - 1)) == 0, s, NEG_BIG), NEG_BIG)
            s_scr[:, j * LANES:(j + 1) * LANES] = s
            mj = jnp.max(s, axis=1, keepdims=True)
            m = mj if m is None else jnp.maximum(m, mj)
        s_new = _head_sum(kn * qb[:, 0:t_len])
        s_new = jnp.where(new_idx <= t, jnp.where(((t - new_idx) & (dil - 1)) == 0, s_new, NEG_BIG), NEG_BIG)
        m = jnp.maximum(m, jnp.max(s_new, axis=1, keepdims=True))
        e_new = jnp.exp(s_new - m)
        l = jnp.sum(e_new, axis=1, keepdims=True)
        acc = None
        for j in range(n_tiles):
            e = jnp.exp(s_scr[:, j * LANES:(j + 1) * LANES] - m)
            l = l + jnp.sum(e, axis=1, keepdims=True)
            term = cv_ref[:, j * LANES:(j + 1) * LANES] * _head_expand(e)
            acc = term if acc is None else acc + term
        o = jnp.sum(acc, axis=1, keepdims=True) + jnp.sum(vn * _head_expand(e_new), axis=1, keepdims=True)
        o_ref[:, t:t + 1] = o * _head_expand(1.0 / l)
        lse_ref[:, t:t + 1] = m + jnp.log(l)

    for c_ref, new, out_ref in ((ck_ref, kn, ko_ref), (cv_ref, vn, vo_ref)):
        out_ref[...] = pltpu.roll(c_ref[...], win - t_len, 1)
        out_ref[:, win - t_len:win] = new


def _s_attn_group(q_t, kn_t, vn_t, cache_k, cache_v, g):
    bsz, _, t_len = q_t.shape
    win_full, dil = DIL_GROUPS[g]
    win = cache_k.shape[2]
    assert win == win_full and win == (KEYS_PER_QUERY - 1) * dil and win % LANES == 0
    assert dil & (dil - 1) == 0 and LANES % dil == 0
    grp = lambda b: (b, g, 0)
    per_b = lambda b: (b, 0, 0)
    return pl.pallas_call(
        functools.partial(_s_attn_kernel, win=win, dil=dil, t_len=t_len),
        grid=(bsz,),
        in_specs=[pl.BlockSpec((None, GROUP_WIDTH, t_len), grp)] * 3
        + [pl.BlockSpec((None, GROUP_WIDTH, win), per_b)] * 2,
        out_specs=[pl.BlockSpec((None, GROUP_WIDTH, t_len), per_b),
                   pl.BlockSpec((None, HEADS_PER_GROUP, t_len), per_b),
                   pl.BlockSpec((None, GROUP_WIDTH, win), per_b),
                   pl.BlockSpec((None, GROUP_WIDTH, win), per_b)],
        out_shape=[jax.ShapeDtypeStruct((bsz, GROUP_WIDTH, t_len), F32),
                   jax.ShapeDtypeStruct((bsz, HEADS_PER_GROUP, t_len), F32),
                   jax.ShapeDtypeStruct((bsz, GROUP_WIDTH, win), F32),
                   jax.ShapeDtypeStruct((bsz, GROUP_WIDTH, win), F32)],
        scratch_shapes=[pltpu.VMEM((HEADS_PER_GROUP, win), F32)],
        compiler_params=_cparams(("arbitrary",)),
        name=f"attn_sample_g{g}",
    )(q_t, kn_t, vn_t, cache_k, cache_v)


TM_INPROJ = 512
TM_CONVPROJ = 1024
TM_POST = 512
TM_EXPERT = 256
ROWS_PER_DMA_STEP = 256
ATTN_ROWS_PER_STEP = 1024
S_COL_BLOCK = 512


def _to_state(a_t):
    b, _, length = a_t.shape
    return jnp.transpose(a_t.reshape(b, HEADS_PER_GROUP, HEAD_DIM, length), (0, 3, 1, 2))[None]


def _from_state(a):
    b, length = a.shape[0], a.shape[1]
    return jnp.transpose(a, (0, 2, 3, 1)).reshape(b, GROUP_WIDTH, length)


def kernel(x_prompt, x_sample, cache_k1, cache_v1, cache_k2, cache_v2, cache_k3, cache_v3, state_conv,
           c_prompt, c_sample, norm1_g, norm2_g, normf_g, w_ada, b_ada, w_in, conv_w, w_conv_out,
           w_attn_out, w_o, w_rg, b_rg, w_re, b_re, w1, w3, w2):
    depth = w_in.shape[0]
    assert depth == 1
    bsz, seq, _ = x_prompt.shape
    dbsz, t_len, _ = x_sample.shape
    n_s = dbsz * t_len
    l = 0

    w_a_bf = w_in[l][:, :OFF_Q].astype(BF16)
    w_g_bf = w_in[l][:, OFF_GCONV:].astype(BF16)
    w_qkv_bf = w_in[l][:, OFF_Q:OFF_GCONV].astype(BF16)
    wco_bf = w_conv_out[l].astype(BF16)
    wao_bf = w_attn_out[l].astype(BF16)
    wo_bf = w_o[l].astype(BF16)
    w1_e = w1[l].reshape(N_EXPERTS, D_MODEL, D_EXPERT)
    w3_e = w3[l].reshape(N_EXPERTS, D_MODEL, D_EXPERT)
    w2_e = w2[l].reshape(N_EXPERTS, D_EXPERT, D_MODEL)
    g1 = norm1_g[l].reshape(1, D_MODEL)
    g2 = norm2_g[l].reshape(1, D_MODEL)
    gf = normf_g.reshape(1, D_MODEL)
    n_route = N_EXPERT_GROUPS + N_EXPERTS
    wr = jnp.pad(jnp.concatenate([w_rg[l], w_re[l]], axis=1), ((0, 0), (0, LANES - n_route)))
    br = jnp.pad(jnp.concatenate([b_rg[l], b_re[l]]), (0, LANES - n_route)).reshape(1, LANES)
    head_of_lane = np.arange(GROUP_WIDTH) // HEAD_DIM
    expand_bf = jnp.asarray((np.arange(LANES)[:, None] == head_of_lane[None, :]).astype(np.float32), dtype=BF16)

    mod = _adaln(jnp.concatenate([c_prompt, c_sample], axis=0), w_ada[l], b_ada[l])
    mod_p4 = mod[:bsz].reshape(bsz, 6, 1, D_MODEL)
    mod_tok = jnp.repeat(mod[bsz:].reshape(dbsz, 1, 6, D_MODEL), t_len, axis=1)
    mod_tok = jnp.transpose(mod_tok.reshape(n_s, 6, D_MODEL), (1, 0, 2))
    mod_s4 = mod_tok[None]

    pos_p = jnp.arange(seq, dtype=jnp.int32)
    rope_q_p = _rope_tables(pos_p, HEAD_DIM ** -0.5)
    rope_k_p = _rope_tables(pos_p, 1.0)
    pc_p, sga_p, conv_p = _convproj_prompt(x_prompt, mod_p4, g1, w_a_bf, w_g_bf,
                                           conv_w[l], wco_bf, tm=TM_CONVPROJ)
    outs = _qkvproj_prompt(x_prompt, mod_p4, g1, w_qkv_bf, rope_q_p, rope_k_p, tm=TM_INPROJ)
    qkv_p = outs[0:3 * N_DIL]
    states_p = outs[3 * N_DIL:]
    o_p, lse_p = [], []
    for g in range(N_DIL):
        o_g, lse_g = _attn_prompt_group(qkv_p[3 * g], qkv_p[3 * g + 1], qkv_p[3 * g + 2], g, max_rows=ATTN_ROWS_PER_STEP)
        o_p.append(o_g)
        lse_p.append(lse_g)
    x1_p, h2rows_p, meta_p, route_p, cnt_p = _post(o_p, lse_p, pc_p, sga_p, x_prompt, mod_p4, wao_bf, wo_bf, g2, wr, br,
                                                   expand_bf, tm=TM_POST, full_precision=False, routed=True)
    n_p = bsz * seq
    route = route_p.reshape(n_p // TM_POST, SUBLANES, TM_POST)
    starts, tables = _routing_tables(cnt_p[0, :N_CLASSES], n=n_p, tm=TM_EXPERT)
    hs = _dispatch_rows(h2rows_p.reshape(n_p * H_ROWS, LANES), route, starts, rows_per_step=ROWS_PER_DMA_STEP)
    yp = _experts_routed(hs, tables, w1_e, w3_e, w2_e, tm=TM_EXPERT)
    y_p = _combine_final(yp, route, starts, x1_p.reshape(n_p, D_MODEL), meta_p.reshape(n_p, LANES), mod_p4, gf,
                         rows_per_step=ROWS_PER_DMA_STEP, rows_per_mod=seq).reshape(bsz, seq, D_MODEL)

    pos_s = PAST_LEN + jnp.arange(t_len, dtype=jnp.int32)
    rope_q_s = jnp.tile(_rope_tables(pos_s, HEAD_DIM ** -0.5), (1, dbsz, 1))
    rope_k_s = jnp.tile(_rope_tables(pos_s, 1.0), (1, dbsz, 1))
    xs = x_sample.reshape(n_s, D_MODEL)
    z_s = _s_inproj(xs, mod_tok, g1, w_in[l], col_block=S_COL_BLOCK)
    past = state_conv[l]
    p0e = jnp.repeat(past[:, 0], t_len, axis=0)
    p1e = jnp.repeat(past[:, 1], t_len, axis=0)
    cu_s, pc_s, sga_s, q_s, k_s, v_s = _s_mid(z_s, p0e, p1e, conv_w[l], w_conv_out[l], rope_q_s, rope_k_s,
                                              t_len=t_len)
    to_cols = lambda a: jnp.transpose(a.reshape(dbsz, t_len, ATTN_WIDTH), (0, 2, 1))
    q_t, kn_t, vn_t = to_cols(q_s), to_cols(k_s), to_cols(v_s)
    caches = ((cache_k1, cache_v1), (cache_k2, cache_v2), (cache_k3, cache_v3))
    o_s, lse_s, kv_s = [], [], []
    for g, (ck, cv) in enumerate(caches):
        o_g, lse_g, ko, vo = _s_attn_group(q_t, kn_t, vn_t, _from_state(ck[l]), _from_state(cv[l]), g)
        o_s.append(jnp.transpose(o_g, (0, 2, 1)).reshape(1, 1, n_s, GROUP_WIDTH))
        lse_rows = jnp.transpose(lse_g, (0, 2, 1)).reshape(n_s, HEADS_PER_GROUP)
        lse_s.append(jnp.pad(lse_rows, ((0, 0), (0, LANES - HEADS_PER_GROUP))).reshape(1, 1, n_s, LANES))
        kv_s += [_to_state(ko), _to_state(vo)]
    x1_s, h2_s, comb_s = _post(o_s, lse_s, pc_s[None], sga_s[None], xs[None], mod_s4, w_attn_out[l], w_o[l], g2, wr, br,
                               expand_bf, tm=n_s, full_precision=True, routed=False)
    y_s = _moe_dense(h2_s, comb_s, x1_s, mod_s4, gf, w1_e, w3_e, w2_e, tm=n_s)

    conv_s = cu_s.reshape(dbsz, t_len, D_MODEL)[:, t_len - (CONV_K - 1):]
    return (y_p, y_s.reshape(dbsz, t_len, D_MODEL),
            *[_to_state(a) for a in states_p], conv_p.reshape(1, bsz, CONV_K - 1, D_MODEL),
            *kv_s, conv_s.reshape(1, dbsz, CONV_K - 1, D_MODEL))
```

```python
import functools

import numpy as np
import jax
import jax.numpy as jnp
from jax import lax
from jax.experimental import pallas as pl
from jax.experimental.pallas import tpu as pltpu

F32 = jnp.float32
BF16 = jnp.bfloat16
HIGHEST = lax.Precision.HIGHEST

D_MODEL = 1024
HEAD_DIM = 64
HEADS_PER_GROUP = 8
GROUP_WIDTH = HEADS_PER_GROUP * HEAD_DIM
DIL_GROUPS = ((128, 1), (512, 4), (2048, 16))
N_DIL = len(DIL_GROUPS)
ATTN_WIDTH = N_DIL * GROUP_WIDTH
ROT_DIM = HEAD_DIM // 4
ROPE_THETA = 500000.0
PAST_LEN = 16384
CONV_K = 3
N_EXPERT_GROUPS = 4
EXPERTS_PER_GROUP = 4
N_EXPERTS = N_EXPERT_GROUPS * EXPERTS_PER_GROUP
D_EXPERT = 512
RMS_EPS = 1e-6
IN_COLS = 3 * D_MODEL + 3 * ATTN_WIDTH + 2 * D_MODEL
OFF_U, OFF_GC, OFF_GB = 0, D_MODEL, 2 * D_MODEL
OFF_Q = 3 * D_MODEL
OFF_K = OFF_Q + ATTN_WIDTH
OFF_V = OFF_K + ATTN_WIDTH
OFF_GCONV = OFF_V + ATTN_WIDTH
OFF_GATTN = OFF_GCONV + D_MODEL

LANES = 128
SUBLANES = 8
CHUNKS_PER_GROUP = GROUP_WIDTH // LANES
KEYS_PER_QUERY = 129
Q_BLOCK = 128
NEG_BIG = -1e30

VMEM_LIMIT = 56 * 1024 * 1024


def _sigmoid(x):
    return 1.0 / (1.0 + jnp.exp(-x))


def _cparams(sem):
    return pltpu.CompilerParams(dimension_semantics=sem, vmem_limit_bytes=VMEM_LIMIT)


def _adaln_kernel(c_ref, w_ref, b_ref, o_ref):
    c = c_ref[...]
    s = c * _sigmoid(c)
    o_ref[...] = jnp.dot(s, w_ref[...], precision=HIGHEST, preferred_element_type=F32) + b_ref[...]


def _adaln(c_all, w_ada, b_ada):
    rows = c_all.shape[0]
    n_col = w_ada.shape[1] // D_MODEL
    return pl.pallas_call(
        _adaln_kernel,
        grid=(n_col,),
        in_specs=[
            pl.BlockSpec((rows, D_MODEL), lambda j: (0, 0)),
            pl.BlockSpec((D_MODEL, D_MODEL), lambda j: (0, j)),
            pl.BlockSpec((1, D_MODEL), lambda j: (0, j)),
        ],
        out_specs=pl.BlockSpec((rows, D_MODEL), lambda j: (0, j)),
        out_shape=jax.ShapeDtypeStruct((rows, w_ada.shape[1]), F32),
        compiler_params=_cparams(("arbitrary",)),
        name="adaln",
    )(c_all, w_ada, b_ada.reshape(1, -1))


def _rope_tables(pos, scale):
    half = ROT_DIM // 2
    inv_freq = jnp.power(jnp.float32(ROPE_THETA), -jnp.arange(half, dtype=F32) / half)
    ang = pos.astype(F32)[:, None] * inv_freq[None, :]
    cos, sin = jnp.cos(ang), jnp.sin(ang)
    lane_in_head = np.arange(LANES) % HEAD_DIM
    freq = lane_in_head % half
    first = lane_in_head < half
    second = (lane_in_head >= half) & (lane_in_head < ROT_DIM)
    a = jnp.where(first | second, cos[:, freq], 1.0)
    bm = jnp.where(first, -sin[:, freq], 0.0)
    bp = jnp.where(second, sin[:, freq], 0.0)
    return jnp.stack([a, bm, bp]) * scale


def _rope_chunk(zc, a, bm, bp):
    return zc * a + pltpu.roll(zc, LANES - ROT_DIM // 2, 1) * bm + pltpu.roll(zc, ROT_DIM // 2, 1) * bp


def _modulated_norm_bf16(x_ref, mod_ref, g1_ref):
    x = x_ref[...]
    var = jnp.mean(x * x, axis=-1, keepdims=True)
    h = (x * lax.rsqrt(var + RMS_EPS)) * (g1_ref[...] * (1.0 + mod_ref[1])) + mod_ref[0]
    return h.astype(BF16)


def _convproj_kernel(x_ref, mod_ref, g1_ref, wa_ref, wg_ref, cw_ref, wco_ref, pc_ref, sga_ref, cst_ref, s_ref,
                     *, tm, n_tiles):
    i = pl.program_id(1)

    @pl.when(i == 0)
    def _():
        s_ref[0:SUBLANES, :] = jnp.zeros((SUBLANES, D_MODEL), F32)

    hb = _modulated_norm_bf16(x_ref, mod_ref, g1_ref)

    def proj(w_ref, lo):
        return jnp.dot(hb, w_ref[:, lo:lo + D_MODEL], preferred_element_type=F32)

    cu = proj(wa_ref, OFF_GC) * proj(wa_ref, OFF_U)
    s_ref[SUBLANES:SUBLANES + tm, :] = cu
    cw = cw_ref[...]
    conv = (cw[0:1] * s_ref[SUBLANES - 2:SUBLANES - 2 + tm, :]
            + cw[1:2] * s_ref[SUBLANES - 1:SUBLANES - 1 + tm, :]
            + cw[2:3] * cu)
    yc = jnp.dot((proj(wa_ref, OFF_GB) * conv).astype(BF16), wco_ref[...], preferred_element_type=F32)
    pc_ref[...] = (_sigmoid(proj(wg_ref, 0)) * yc).astype(pc_ref.dtype)
    sga_ref[...] = _sigmoid(proj(wg_ref, D_MODEL)).astype(sga_ref.dtype)
    s_ref[0:SUBLANES, :] = s_ref[tm:tm + SUBLANES, :]

    @pl.when(i == n_tiles - 1)
    def _():
        cst_ref[...] = s_ref[tm + SUBLANES - 2:tm + SUBLANES, :]


def _convproj_prompt(x, mod4, g1, wa_bf, wg_bf, conv_w, wco_bf, *, tm):
    bsz, seq, _ = x.shape
    n_tiles = seq // tm
    const2 = lambda b, i: (0, 0)
    tile3 = lambda b, i: (b, i, 0)
    return pl.pallas_call(
        functools.partial(_convproj_kernel, tm=tm, n_tiles=n_tiles),
        grid=(bsz, n_tiles),
        in_specs=[
            pl.BlockSpec((None, tm, D_MODEL), tile3),
            pl.BlockSpec((None, 6, 1, D_MODEL), lambda b, i: (b, 0, 0, 0)),
            pl.BlockSpec((1, D_MODEL), const2),
            pl.BlockSpec(wa_bf.shape, const2, pipeline_mode=pl.Buffered(1)),
            pl.BlockSpec(wg_bf.shape, const2, pipeline_mode=pl.Buffered(1)),
            pl.BlockSpec((CONV_K, D_MODEL), const2),
            pl.BlockSpec((D_MODEL, D_MODEL), const2, pipeline_mode=pl.Buffered(1)),
        ],
        out_specs=[pl.BlockSpec((None, tm, D_MODEL), tile3), pl.BlockSpec((None, tm, D_MODEL), tile3),
                   pl.BlockSpec((None, CONV_K - 1, D_MODEL), lambda b, i: (b, 0, 0))],
        out_shape=[jax.ShapeDtypeStruct((bsz, seq, D_MODEL), BF16), jax.ShapeDtypeStruct((bsz, seq, D_MODEL), BF16),
                   jax.ShapeDtypeStruct((bsz, CONV_K - 1, D_MODEL), F32)],
        scratch_shapes=[pltpu.VMEM((tm + SUBLANES, D_MODEL), F32)],
        compiler_params=_cparams(("arbitrary", "arbitrary")),
        name="convproj_prompt",
    )(x, mod4, g1, wa_bf, wg_bf, conv_w, wco_bf)


def _qkvproj_kernel(x_ref, mod_ref, g1_ref, w_ref, rq_ref, rk_ref, *rest, tm, n_tiles, seq):
    qkv_refs = rest[0:3 * N_DIL]
    st_refs = rest[3 * N_DIL:3 * N_DIL + 2 * N_DIL]
    d_ref = rest[-1]
    i = pl.program_id(1)
    hb = _modulated_norm_bf16(x_ref, mod_ref, g1_ref)

    def proj(lo, width):
        return jnp.dot(hb, w_ref[:, lo:lo + width], preferred_element_type=F32)

    zq = proj(0, ATTN_WIDTH)
    zk = proj(ATTN_WIDTH, ATTN_WIDTH)
    zv = proj(2 * ATTN_WIDTH, ATTN_WIDTH)
    aq, bmq, bpq = rq_ref[0], rq_ref[1], rq_ref[2]
    ak, bmk, bpk = rk_ref[0], rk_ref[1], rk_ref[2]
    n_chunks = ATTN_WIDTH // LANES
    q_chunks, k_chunks, v_chunks = [], [], []
    for c in range(n_chunks):
        sl = slice(c * LANES, (c + 1) * LANES)
        q_chunks.append(_rope_chunk(zq[:, sl], aq, bmq, bpq))
        k_chunks.append(_rope_chunk(zk[:, sl], ak, bmk, bpk))
        v_chunks.append(zv[:, sl])

    for which, chunks in enumerate((q_chunks, k_chunks, v_chunks)):
        for g in range(N_DIL):
            out_ref = qkv_refs[3 * g + which]
            dil = DIL_GROUPS[g][1]
            for cc in range(CHUNKS_PER_GROUP):
                c = g * CHUNKS_PER_GROUP + cc
                sl = slice(cc * LANES, (cc + 1) * LANES)
                if dil == 1:
                    out_ref[0, :, sl] = chunks[c].astype(BF16)
                else:
                    d_ref[c] = chunks[c]
                    for r in range(dil):
                        out_ref[r, :, sl] = d_ref[c, pl.ds(r, tm // dil, stride=dil), :].astype(BF16)

    for g in range(N_DIL):
        kst, vst = st_refs[2 * g], st_refs[2 * g + 1]
        win = min(DIL_GROUPS[g][0], seq)
        if win >= tm:
            cond, r0 = i >= (seq - win) // tm, 0
        else:
            cond, r0 = i == n_tiles - 1, tm - win

        @pl.when(cond)
        def _(g=g, kst=kst, vst=vst, r0=r0):
            for cc in range(CHUNKS_PER_GROUP):
                c = g * CHUNKS_PER_GROUP + cc
                kst[cc * LANES:(cc + 1) * LANES, :] = k_chunks[c][r0:, :].T
                vst[cc * LANES:(cc + 1) * LANES, :] = v_chunks[c][r0:, :].T


def _qkvproj_prompt(x, mod4, g1, wqkv_bf, rope_q, rope_k, *, tm):
    bsz, seq, _ = x.shape
    n_tiles = seq // tm
    const2 = lambda b, i: (0, 0)
    tile3 = lambda b, i: (b, i, 0)
    in_specs = [
        pl.BlockSpec((None, tm, D_MODEL), tile3),
        pl.BlockSpec((None, 6, 1, D_MODEL), lambda b, i: (b, 0, 0, 0)),
        pl.BlockSpec((1, D_MODEL), const2),
        pl.BlockSpec(wqkv_bf.shape, const2, pipeline_mode=pl.Buffered(1)),
        pl.BlockSpec((3, tm, LANES), lambda b, i: (0, i, 0)),
        pl.BlockSpec((3, tm, LANES), lambda b, i: (0, i, 0)),
    ]
    out_shape, out_specs = [], []
    for _, dil in DIL_GROUPS:
        assert tm % (dil * 16) == 0
        for _ in range(3):
            out_shape.append(jax.ShapeDtypeStruct((bsz, dil, seq // dil, GROUP_WIDTH), BF16))
            out_specs.append(pl.BlockSpec((None, dil, tm // dil, GROUP_WIDTH), lambda b, i: (b, 0, i, 0)))
    for win, _ in DIL_GROUPS:
        win = min(win, seq)
        cols = min(win, tm)
        if win >= tm:
            imap = lambda b, i, ft=(seq - win) // tm: (b, 0, jnp.maximum(i - ft, 0))
        else:
            imap = lambda b, i: (b, 0, 0)
        for _ in range(2):
            out_shape.append(jax.ShapeDtypeStruct((bsz, GROUP_WIDTH, win), F32))
            out_specs.append(pl.BlockSpec((None, GROUP_WIDTH, cols), imap))
    return pl.pallas_call(
        functools.partial(_qkvproj_kernel, tm=tm, n_tiles=n_tiles, seq=seq),
        grid=(bsz, n_tiles),
        in_specs=in_specs,
        out_specs=out_specs,
        out_shape=out_shape,
        scratch_shapes=[pltpu.VMEM((ATTN_WIDTH // LANES, tm, LANES), F32)],
        compiler_params=_cparams(("arbitrary", "arbitrary")),
        name="qkvproj_prompt",
    )(x, mod4, g1, wqkv_bf, rope_q, rope_k)


ATTN_UNROLL = 8


def _attn_kernel(*refs, chunk, has_halo, n_cls):
    if has_halo:
        q_ref, k_ref, v_ref, kh_ref, vh_ref, o_ref, lse_ref, kbuf, vbuf = refs
    else:
        q_ref, k_ref, v_ref, o_ref, lse_ref, kbuf, vbuf = refs
    c = pl.program_id(2)
    if has_halo:
        kbuf[:, 0:Q_BLOCK, :] = kh_ref[...]
        vbuf[:, 0:Q_BLOCK, :] = vh_ref[...]
    else:
        kbuf[:, 0:Q_BLOCK, :] = jnp.zeros((n_cls, Q_BLOCK, GROUP_WIDTH), BF16)
        vbuf[:, 0:Q_BLOCK, :] = jnp.zeros((n_cls, Q_BLOCK, GROUP_WIDTH), BF16)
    kbuf[:, Q_BLOCK:Q_BLOCK + chunk, :] = k_ref[...]
    vbuf[:, Q_BLOCK:Q_BLOCK + chunk, :] = v_ref[...]
    blocks_per_cls = chunk // Q_BLOCK

    row = lax.broadcasted_iota(jnp.int32, (Q_BLOCK, 2 * Q_BLOCK), 0)
    col = lax.broadcasted_iota(jnp.int32, (Q_BLOCK, 2 * Q_BLOCK), 1)
    bias_main = jnp.where(col >= row, jnp.where(col <= row + Q_BLOCK, 0.0, NEG_BIG), NEG_BIG)
    bias_first = jnp.where(col >= Q_BLOCK, bias_main, NEG_BIG)
    lane = lax.broadcasted_iota(jnp.int32, (1, LANES), 1)
    lo_half = lane < HEAD_DIM

    def body(u, carry):
        cls, qb = u // blocks_per_cls, u % blocks_per_cls
        r0 = pl.multiple_of(qb * Q_BLOCK, Q_BLOCK)
        kt = kbuf[cls, pl.ds(r0, 2 * Q_BLOCK), :]
        vt = vbuf[cls, pl.ds(r0, 2 * Q_BLOCK), :]
        qt = q_ref[cls, pl.ds(r0, Q_BLOCK), :]
        is_first = jnp.logical_and(qb == 0, c == 0)
        bias = jnp.where(is_first, bias_first, bias_main)
        bias2 = jnp.concatenate([bias, bias], axis=0)
        lse_tile = jnp.zeros((Q_BLOCK, LANES), F32)
        for p in range(CHUNKS_PER_GROUP):
            sl = slice(p * LANES, (p + 1) * LANES)
            qp, kp, vp = qt[:, sl], kt[:, sl], vt[:, sl]
            zero = jnp.zeros_like(qp)
            q2 = jnp.concatenate([jnp.where(lo_half, qp, zero), jnp.where(lo_half, zero, qp)], axis=0)
            s = lax.dot_general(q2, kp, (((1,), (1,)), ((), ())), preferred_element_type=F32) + bias2
            m = jnp.max(s, axis=1, keepdims=True)
            e = jnp.exp(s - m)
            l = jnp.sum(e, axis=1, keepdims=True)
            o = jnp.dot(e.astype(BF16), vp, preferred_element_type=F32) * (1.0 / l)
            lse = m + jnp.log(l)
            for hh in range(2):
                lse_tile = jnp.where(lane == 2 * p + hh, lse[hh * Q_BLOCK:(hh + 1) * Q_BLOCK], lse_tile)
            o_ref[cls, pl.ds(r0, Q_BLOCK), sl] = jnp.where(lo_half, o[:Q_BLOCK], o[Q_BLOCK:])
        lse_ref[cls, pl.ds(r0, Q_BLOCK), :] = lse_tile
        return carry

    n_units = n_cls * blocks_per_cls
    lax.fori_loop(0, n_units, body, 0, unroll=min(ATTN_UNROLL, n_units))


def _attn_prompt_group(q, k, v, g, *, max_rows):
    bsz, dil, cls_len, _ = q.shape
    chunk = min(max_rows, cls_len)
    n_chunks = cls_len // chunk
    n_cls = min(dil, max_rows // chunk)
    has_halo = n_chunks > 1
    main = pl.BlockSpec((None, n_cls, chunk, GROUP_WIDTH), lambda b, r, c: (b, r, c, 0))
    in_specs = [main, main, main]
    args = [q, k, v]
    if has_halo:
        per = chunk // Q_BLOCK
        halo = pl.BlockSpec((None, n_cls, Q_BLOCK, GROUP_WIDTH), lambda b, r, c: (b, r, jnp.maximum(c * per - 1, 0), 0))
        in_specs += [halo, halo]
        args += [k, v]
    return pl.pallas_call(
        functools.partial(_attn_kernel, chunk=chunk, has_halo=has_halo, n_cls=n_cls),
        grid=(bsz, dil // n_cls, n_chunks),
        in_specs=in_specs,
        out_specs=[pl.BlockSpec((None, n_cls, chunk, GROUP_WIDTH), lambda b, r, c: (b, r, c, 0)),
                   pl.BlockSpec((None, n_cls, chunk, LANES), lambda b, r, c: (b, r, c, 0))],
        out_shape=[jax.ShapeDtypeStruct((bsz, dil, cls_len, GROUP_WIDTH), F32),
                   jax.ShapeDtypeStruct((bsz, dil, cls_len, LANES), F32)],
        scratch_shapes=[pltpu.VMEM((n_cls, chunk + Q_BLOCK, GROUP_WIDTH), BF16),
                        pltpu.VMEM((n_cls, chunk + Q_BLOCK, GROUP_WIDTH), BF16)],
        compiler_params=_cparams(("arbitrary", "arbitrary", "arbitrary")),
        name=f"attn_prompt_g{g}",
    )(*args)


def _split_bf16(x, n):
    parts = []
    r = x
    for _ in range(n):
        p = r.astype(BF16)
        parts.append(p)
        r = r - p.astype(F32)
    return parts


def _store_token_rows(ref, value, n_tokens, first_chunk=0, rows_per_token=None):
    n_chunks = value.shape[1] // LANES
    rows_per_token = rows_per_token or n_chunks
    for c in range(n_chunks):
        ref[pl.ds(first_chunk + c, n_tokens, stride=rows_per_token), :] = value[:, c * LANES:(c + 1) * LANES]


def _load_token_rows(ref, n_tokens, n_chunks, first_chunk=0, rows_per_token=None):
    rows_per_token = rows_per_token or n_chunks
    return jnp.concatenate([ref[pl.ds(first_chunk + c, n_tokens, stride=rows_per_token), :] for c in range(n_chunks)],
                           axis=1)


H_ROWS = D_MODEL // LANES
Y_ROWS = 2 * D_MODEL // LANES

_PAIRS = ((0, 1), (0, 2), (0, 3), (1, 2), (1, 3), (2, 3))
N_CLASSES = N_EXPERT_GROUPS * len(_PAIRS)
META_CLASS, META_RANK, META_WA, META_WB = 0, 1, 2, 3


def _post_kernel(*refs, tm, dils, full_precision, routed):
    (o0_ref, o1_ref, o2_ref, l0_ref, l1_ref, l2_ref, pc_ref, sga_ref, x_ref, mod_ref,
     wao_ref, wo_ref, g2_ref, wr_ref, wrh_ref, wrl_ref, br_ref, exp_ref) = refs[:18]
    if routed:
        tri_ref, x1_ref, h2_ref, meta_ref, route_ref, cnt_ref, o_scr, l_scr, run_scr = refs[18:]

        @pl.when(jnp.logical_and(pl.program_id(0) == 0, pl.program_id(1) == 0))
        def _():
            run_scr[...] = jnp.zeros_like(run_scr)
    else:
        x1_ref, h2_ref, comb_ref, o_scr, l_scr = refs[18:]

    def mm(a, w_ref):
        if full_precision:
            return jnp.dot(a, w_ref[...], precision=HIGHEST, preferred_element_type=F32)
        return jnp.dot(a.astype(BF16), w_ref[...], preferred_element_type=F32)

    def natural_order(ref, scr, dil, n_chunks):
        if dil == 1:
            return [ref[0, :, c * LANES:(c + 1) * LANES] for c in range(n_chunks)]
        out = []
        for c in range(n_chunks):
            for r in range(dil):
                scr[c, pl.ds(r, tm // dil, stride=dil), :] = ref[r, :, c * LANES:(c + 1) * LANES]
            out.append(scr[c])
        return out

    lses = [natural_order(ref, l_scr.at[g], dils[g], 1)[0] for g, ref in enumerate((l0_ref, l1_ref, l2_ref))]
    mx = jnp.maximum(lses[0], jnp.maximum(lses[1], lses[2]))
    es = [jnp.exp(v - mx) for v in lses]
    inv = 1.0 / (es[0] + es[1] + es[2])
    expand = exp_ref[...]
    attn_o = None
    for g, o_ref in enumerate((o0_ref, o1_ref, o2_ref)):
        w = es[g] * inv
        we = None
        for part in _split_bf16(w, 3 if full_precision else 2):
            t = jnp.dot(part, expand, preferred_element_type=F32)
            we = t if we is None else we + t
        o_nat = jnp.concatenate(natural_order(o_ref, o_scr.at[g], dils[g], CHUNKS_PER_GROUP), axis=1)
        term = we * o_nat
        attn_o = term if attn_o is None else attn_o + term

    y_attn = mm(attn_o, wao_ref)
    mixed = mm(pc_ref[...] + sga_ref[...] * y_attn, wo_ref)
    x1 = x_ref[...] + mod_ref[2] * mixed
    x1_ref[...] = x1
    var = jnp.mean(x1 * x1, axis=-1, keepdims=True)
    h2 = (x1 * lax.rsqrt(var + RMS_EPS)) * (g2_ref[...] * (1.0 + mod_ref[4])) + mod_ref[3]
    if routed:
        _store_token_rows(h2_ref, h2, tm)
    else:
        h2_ref[...] = h2.astype(h2_ref.dtype)

    if full_precision:
        lg = jnp.dot(h2, wr_ref[...], precision=HIGHEST, preferred_element_type=F32) + br_ref[...]
    else:
        h_hi, h_lo = _split_bf16(h2, 2)
        lg = (jnp.dot(h_hi, wrh_ref[...], preferred_element_type=F32)
              + jnp.dot(h_lo, wrh_ref[...], preferred_element_type=F32)
              + jnp.dot(h_hi, wrl_ref[...], preferred_element_type=F32)) + br_ref[...]
    lane_i = lax.broadcasted_iota(jnp.int32, (1, LANES), 1)
    lane = lane_i.astype(F32)
    lane_group = ((lane_i - N_EXPERT_GROUPS) >> 2).astype(F32)
    big = jnp.float32(1e9)
    gl = jnp.where(lane_i < N_EXPERT_GROUPS, lg, NEG_BIG)
    gmax = jnp.max(gl, axis=1, keepdims=True)
    gidx = jnp.min(jnp.where(gl == gmax, lane, big), axis=1, keepdims=True)
    g_w = 1.0 / jnp.sum(jnp.exp(gl - gmax), axis=1, keepdims=True)
    el = jnp.where(lane_group == gidx, lg, NEG_BIG)
    v1 = jnp.max(el, axis=1, keepdims=True)
    i1 = jnp.min(jnp.where(el == v1, lane, big), axis=1, keepdims=True)
    el2 = jnp.where(lane == i1, NEG_BIG, el)
    v2 = jnp.max(el2, axis=1, keepdims=True)
    i2 = jnp.min(jnp.where(el2 == v2, lane, big), axis=1, keepdims=True)
    t = jnp.exp(v2 - v1)
    den = 1.0 / (1.0 + t)
    w_top1, w_top2 = g_w * den, g_w * (t * den)
    if not routed:
        comb_ref[...] = jnp.where(lane == i1, w_top1, jnp.where(lane == i2, w_top2, 0.0))
        return

    base = jnp.float32(N_EXPERT_GROUPS) + jnp.float32(EXPERTS_PER_GROUP) * gidx
    e1, e2 = i1 - base, i2 - base
    first_is_lower = e1 < e2
    ea = jnp.where(first_is_lower, e1, e2)
    eb = jnp.where(first_is_lower, e2, e1)
    pair = ea * (7.0 - ea) * 0.5 + (eb - ea - 1.0)
    cls = gidx * jnp.float32(len(_PAIRS)) + pair
    onehot = lane == cls
    earlier = jnp.dot(tri_ref[...], jnp.where(onehot, 1.0, 0.0).astype(BF16), preferred_element_type=F32)
    running = run_scr[...]
    rank = jnp.sum(jnp.where(onehot, earlier + running, 0.0), axis=1, keepdims=True)
    running = running + jnp.sum(jnp.where(onehot, 1.0, 0.0), axis=0, keepdims=True)
    run_scr[...] = running
    cnt_ref[...] = running
    w_a = jnp.where(first_is_lower, w_top1, w_top2)
    w_b = jnp.where(first_is_lower, w_top2, w_top1)
    meta = jnp.where(lane_i == META_CLASS, cls,
                     jnp.where(lane_i == META_RANK, rank,
                               jnp.where(lane_i == META_WA, w_a, jnp.where(lane_i == META_WB, w_b, 0.0))))
    meta_ref[...] = meta
    route_ref[...] = meta.T[0:SUBLANES, :].astype(jnp.int32)


def _post(o_list, lse_list, pc, sga, x, mod4, wao, wo, g2, wr, br, expand, *, tm, full_precision, routed):
    nb, rows, _ = x.shape
    tiles = rows // tm
    dils = tuple(o.shape[1] for o in o_list)
    tile3 = lambda b, i: (b, i, 0)
    const2 = lambda b, i: (0, 0)
    mod_rows = mod4.shape[2]
    if mod_rows == 1:
        mod_spec = pl.BlockSpec((None, 6, 1, D_MODEL), lambda b, i: (b, 0, 0, 0))
    else:
        mod_spec = pl.BlockSpec((None, 6, tm, D_MODEL), lambda b, i: (b, 0, i, 0))
    cls4 = lambda b, i: (b, 0, i, 0)
    in_specs = (
        [pl.BlockSpec((None, d, tm // d, GROUP_WIDTH), cls4) for d in dils]
        + [pl.BlockSpec((None, d, tm // d, LANES), cls4) for d in dils]
        + [pl.BlockSpec((None, tm, D_MODEL), tile3)] * 3 + [mod_spec]
        + [pl.BlockSpec(wao.shape, const2), pl.BlockSpec(wo.shape, const2), pl.BlockSpec((1, D_MODEL), const2)]
        + [pl.BlockSpec((D_MODEL, LANES), const2)] * 3
        + [pl.BlockSpec((1, LANES), const2), pl.BlockSpec((LANES, GROUP_WIDTH), const2)]
    )
    wr_hi = wr.astype(BF16)
    wr_lo = (wr - wr_hi.astype(F32)).astype(BF16)
    args = [*o_list, *lse_list, pc, sga, x, mod4, wao, wo, g2, wr, wr_hi, wr_lo, br, expand]
    scratch = [pltpu.VMEM((N_DIL, CHUNKS_PER_GROUP, tm, LANES), F32), pltpu.VMEM((N_DIL, 1, tm, LANES), F32)]
    if routed:
        tri = jnp.asarray(np.tril(np.ones((tm, tm), np.float32), -1), dtype=BF16)
        args.append(tri)
        in_specs = in_specs + [pl.BlockSpec((tm, tm), const2)]
        out_specs = [pl.BlockSpec((None, tm, D_MODEL), tile3), pl.BlockSpec((None, tm * H_ROWS, LANES), tile3),
                     pl.BlockSpec((None, tm, LANES), tile3),
                     pl.BlockSpec((None, None, SUBLANES, tm), lambda b, i: (b, i, 0, 0)),
                     pl.BlockSpec((1, LANES), const2)]
        out_shape = [jax.ShapeDtypeStruct((nb, rows, D_MODEL), F32),
                     jax.ShapeDtypeStruct((nb, rows * H_ROWS, LANES), F32),
                     jax.ShapeDtypeStruct((nb, rows, LANES), F32),
                     jax.ShapeDtypeStruct((nb, tiles, SUBLANES, tm), jnp.int32),
                     jax.ShapeDtypeStruct((1, LANES), F32)]
        scratch.append(pltpu.VMEM((1, LANES), F32))
    else:
        out_specs = [pl.BlockSpec((None, tm, D_MODEL), tile3), pl.BlockSpec((None, tm, D_MODEL), tile3),
                     pl.BlockSpec((None, tm, LANES), tile3)]
        out_shape = [jax.ShapeDtypeStruct((nb, rows, D_MODEL), F32), jax.ShapeDtypeStruct((nb, rows, D_MODEL), F32),
                     jax.ShapeDtypeStruct((nb, rows, LANES), F32)]
    return pl.pallas_call(
        functools.partial(_post_kernel, tm=tm, dils=dils, full_precision=full_precision, routed=routed),
        grid=(nb, tiles),
        in_specs=in_specs,
        out_specs=out_specs,
        out_shape=out_shape,
        scratch_shapes=scratch,
        compiler_params=_cparams(("arbitrary", "arbitrary")),
        name="post_sample" if full_precision else "post_prompt",
    )(*args)


DMA_LOOP_UNROLL = 8
DMA_PRIORITIES = 2


ROUTE_CLASS_ROW, ROUTE_RANK_ROW = META_CLASS, META_RANK


def _sorted_slot(starts_ref, route_ref, r):
    return starts_ref[route_ref[ROUTE_CLASS_ROW, r]] + route_ref[ROUTE_RANK_ROW, r]


def _token_rows(token, rows_per_token):
    return pl.ds(pl.multiple_of(token * rows_per_token, rows_per_token), rows_per_token)


def _dispatch_kernel(starts_ref, route_ref, src_ref, dst_hbm, buf, sems, *, rows_per_step, n_steps):
    i = pl.program_id(0)
    slot = i % 2

    def wait_slot(s):
        def body(r, carry):
            pltpu.make_async_copy(buf.at[s, _token_rows(0, H_ROWS)], dst_hbm.at[_token_rows(0, H_ROWS)],
                                  sems.at[s * rows_per_step + r]).wait()
            return carry
        lax.fori_loop(0, rows_per_step, body, 0, unroll=DMA_LOOP_UNROLL)

    @pl.when(i >= 1)
    def _():
        wait_slot(1 - slot)

    buf[slot] = src_ref[...]

    def issue(pair, carry):
        for prio in range(DMA_PRIORITIES):
            r = pair * DMA_PRIORITIES + prio
            pltpu.make_async_copy(buf.at[slot, _token_rows(r, H_ROWS)],
                                  dst_hbm.at[_token_rows(_sorted_slot(starts_ref, route_ref, r), H_ROWS)],
                                  sems.at[slot * rows_per_step + r]).start(priority=prio)
        return carry

    lax.fori_loop(0, rows_per_step // DMA_PRIORITIES, issue, 0, unroll=DMA_LOOP_UNROLL // DMA_PRIORITIES)

    @pl.when(i == n_steps - 1)
    def _():
        wait_slot(slot)


def _route_spec(rows_per_step, route_tile, shift=0, n_steps=None):
    per_tile = route_tile // rows_per_step

    def imap(i, starts):
        s = i + shift if n_steps is None else jnp.minimum(i + shift, n_steps - 1)
        return (s // per_tile, 0, s % per_tile)

    return pl.BlockSpec((None, SUBLANES, rows_per_step), imap, memory_space=pltpu.SMEM)


def _dispatch_rows(src, route, starts, *, rows_per_step):
    n = src.shape[0] // H_ROWS
    n_steps = n // rows_per_step
    grid_spec = pltpu.PrefetchScalarGridSpec(
        num_scalar_prefetch=1,
        grid=(n_steps,),
        in_specs=[_route_spec(rows_per_step, route.shape[2]),
                  pl.BlockSpec((rows_per_step * H_ROWS, LANES), lambda i, starts: (i, 0))],
        out_specs=pl.BlockSpec(memory_space=pl.ANY),
        scratch_shapes=[pltpu.VMEM((2, rows_per_step * H_ROWS, LANES), src.dtype),
                        pltpu.SemaphoreType.DMA((2 * rows_per_step,))],
    )
    return pl.pallas_call(
        functools.partial(_dispatch_kernel, rows_per_step=rows_per_step, n_steps=n_steps),
        grid_spec=grid_spec,
        out_shape=jax.ShapeDtypeStruct(src.shape, src.dtype),
        compiler_params=_cparams(("arbitrary",)),
        name="moe_dispatch",
    )(starts, route, src)


def _experts_kernel(tile_ref, ea_ref, eb_ref, lo_ref, hi_ref, first_ref, switch_ref, nv_ref,
                    x_ref, w1a_ref, w1b_ref, w3a_ref, w3b_ref, w2a_ref, w2b_ref, y_ref, w13_scr, w2_scr, *, tm):
    i = pl.program_id(0)

    @pl.when(i < nv_ref[0])
    def _():
        @pl.when(switch_ref[i] == 1)
        def _():
            for k, w_ref in enumerate((w1a_ref, w3a_ref, w1b_ref, w3b_ref)):
                w13_scr[k] = w_ref[...].astype(BF16)
            for k, w_ref in enumerate((w2a_ref, w2b_ref)):
                w2_scr[k] = w_ref[...].astype(BF16)

        def expert(x, k):
            a = jnp.dot(x, w13_scr[2 * k], preferred_element_type=F32)
            b = jnp.dot(x, w13_scr[2 * k + 1], preferred_element_type=F32)
            hid = (a * _sigmoid(a)) * b
            return jnp.dot(hid.astype(BF16), w2_scr[k], preferred_element_type=F32)

        @pl.when(first_ref[i] == 1)
        def _():
            x = _load_token_rows(x_ref, tm, H_ROWS).astype(BF16)
            _store_token_rows(y_ref, expert(x, 0), tm, first_chunk=0, rows_per_token=Y_ROWS)
            _store_token_rows(y_ref, expert(x, 1), tm, first_chunk=H_ROWS, rows_per_token=Y_ROWS)

        @pl.when(first_ref[i] == 0)
        def _():
            x = _load_token_rows(x_ref, tm, H_ROWS).astype(BF16)
            row = lax.broadcasted_iota(jnp.int32, (tm, 1), 0)
            for k in range(2):
                old = _load_token_rows(y_ref, tm, H_ROWS, first_chunk=k * H_ROWS, rows_per_token=Y_ROWS)
                new = jnp.where(row >= lo_ref[i], jnp.where(row < hi_ref[i], expert(x, k), old), old)
                _store_token_rows(y_ref, new, tm, first_chunk=k * H_ROWS, rows_per_token=Y_ROWS)


def _experts_routed(hs, tables, w1, w3, w2, *, tm):
    n = hs.shape[0] // H_ROWS
    n_items = tables[0].shape[0]
    x_map = lambda i, tile, ea, eb, lo, hi, first, switch, nv: (tile[i], 0)
    wa_map = lambda i, tile, ea, eb, lo, hi, first, switch, nv: (ea[i], 0, 0)
    wb_map = lambda i, tile, ea, eb, lo, hi, first, switch, nv: (eb[i], 0, 0)
    w13 = lambda m: pl.BlockSpec((None, D_MODEL, D_EXPERT), m)
    w2s = lambda m: pl.BlockSpec((None, D_EXPERT, D_MODEL), m)
    grid_spec = pltpu.PrefetchScalarGridSpec(
        num_scalar_prefetch=len(tables),
        grid=(n_items,),
        in_specs=[pl.BlockSpec((tm * H_ROWS, LANES), x_map),
                  w13(wa_map), w13(wb_map), w13(wa_map), w13(wb_map), w2s(wa_map), w2s(wb_map)],
        out_specs=pl.BlockSpec((tm * Y_ROWS, LANES), x_map),
        scratch_shapes=[pltpu.VMEM((4, D_MODEL, D_EXPERT), BF16), pltpu.VMEM((2, D_EXPERT, D_MODEL), BF16)],
    )
    return pl.pallas_call(
        functools.partial(_experts_kernel, tm=tm),
        grid_spec=grid_spec,
        out_shape=jax.ShapeDtypeStruct((n * Y_ROWS, LANES), F32),
        compiler_params=_cparams(("arbitrary",)),
        name="moe_experts",
    )(*tables, hs, w1, w1, w3, w3, w2, w2)


def _combine_kernel(starts_ref, route_ref, route_next_ref, yp_hbm, x1_ref, meta_ref, mod_ref, gf_ref, y_ref, ybuf, sems,
                    *, rows_per_step, n_steps):
    i = pl.program_id(0)
    slot = i % 2

    def issue(r_ref, s):
        def body(pair, carry):
            for prio in range(DMA_PRIORITIES):
                r = pair * DMA_PRIORITIES + prio
                pltpu.make_async_copy(yp_hbm.at[_token_rows(_sorted_slot(starts_ref, r_ref, r), Y_ROWS)],
                                      ybuf.at[s, :, pl.ds(r, 1), :],
                                      sems.at[s * rows_per_step + r]).start(priority=prio)
            return carry
        lax.fori_loop(0, rows_per_step // DMA_PRIORITIES, body, 0, unroll=DMA_LOOP_UNROLL // DMA_PRIORITIES)

    @pl.when(i == 0)
    def _():
        issue(route_ref, 0)

    def wait_slot(s):
        def wait_body(r, carry):
            pltpu.make_async_copy(yp_hbm.at[_token_rows(0, Y_ROWS)], ybuf.at[s, :, pl.ds(r, 1), :],
                                  sems.at[s * rows_per_step + r]).wait()
            return carry
        lax.fori_loop(0, rows_per_step, wait_body, 0, unroll=DMA_LOOP_UNROLL)

    wait_slot(slot)
    ya = jnp.concatenate([ybuf[slot, c] for c in range(H_ROWS)], axis=1)
    yb = jnp.concatenate([ybuf[slot, H_ROWS + c] for c in range(H_ROWS)], axis=1)
    meta = meta_ref[...]
    lane = lax.broadcasted_iota(jnp.int32, (1, LANES), 1)
    w_a = jnp.sum(jnp.where(lane == META_WA, meta, 0.0), axis=1, keepdims=True)
    w_b = jnp.sum(jnp.where(lane == META_WB, meta, 0.0), axis=1, keepdims=True)
    x2 = x1_ref[...] + mod_ref[5] * (w_a * ya + w_b * yb)
    var = jnp.mean(x2 * x2, axis=-1, keepdims=True)
    y_ref[...] = (x2 * lax.rsqrt(var + RMS_EPS)) * gf_ref[...]

    for r in range(rows_per_step):
        pltpu.make_async_copy(yp_hbm.at[_token_rows(_sorted_slot(starts_ref, route_next_ref, r), Y_ROWS)],
                              ybuf.at[1 - slot, :, pl.ds(r, 1), :],
                              sems.at[(1 - slot) * rows_per_step + r]).start(priority=r % DMA_PRIORITIES)

    @pl.when(i == n_steps - 1)
    def _():
        wait_slot(1 - slot)


def _combine_final(yp, route, starts, x1, meta, mod4, gf, *, rows_per_step, rows_per_mod):
    n = x1.shape[0]
    n_steps = n // rows_per_step
    steps_per_mod = rows_per_mod // rows_per_step
    row = lambda i, starts: (i, 0)
    grid_spec = pltpu.PrefetchScalarGridSpec(
        num_scalar_prefetch=1,
        grid=(n_steps,),
        in_specs=[_route_spec(rows_per_step, route.shape[2]),
                  _route_spec(rows_per_step, route.shape[2], shift=1, n_steps=n_steps),
                  pl.BlockSpec(memory_space=pl.ANY),
                  pl.BlockSpec((rows_per_step, D_MODEL), row),
                  pl.BlockSpec((rows_per_step, LANES), row),
                  pl.BlockSpec((None, 6, 1, D_MODEL), lambda i, starts: (i // steps_per_mod, 0, 0, 0)),
                  pl.BlockSpec((1, D_MODEL), lambda i, starts: (0, 0))],
        out_specs=pl.BlockSpec((rows_per_step, D_MODEL), row),
        scratch_shapes=[pltpu.VMEM((2, Y_ROWS, rows_per_step, LANES), F32),
                        pltpu.SemaphoreType.DMA((2 * rows_per_step,))],
    )
    yp_rows = yp.reshape(yp.shape[0], 1, LANES)
    return pl.pallas_call(
        functools.partial(_combine_kernel, rows_per_step=rows_per_step, n_steps=n_steps),
        grid_spec=grid_spec,
        out_shape=jax.ShapeDtypeStruct((n, D_MODEL), F32),
        compiler_params=_cparams(("arbitrary",)),
        name="moe_combine",
    )(starts, route, route, yp_rows, x1, meta, mod4, gf)


def _routing_tables(counts, *, n, tm):
    counts = counts.astype(jnp.int32)
    ends = jnp.cumsum(counts)
    starts = ends - counts
    n_items = n // tm + N_CLASSES
    first_tile = starts // tm
    last_tile = jnp.maximum(ends - 1, starts) // tm
    visits = jnp.where(counts > 0, last_tile - first_tile + 1, 0)
    item_end = jnp.cumsum(visits)
    item_start = item_end - visits
    n_valid = item_end[-1]
    item = jnp.arange(n_items, dtype=jnp.int32)
    idx = jnp.minimum(item, n_valid - 1)
    c = jnp.sum((idx[:, None] >= item_end[None, :]).astype(jnp.int32), axis=1)
    class_ids = jnp.arange(N_CLASSES, dtype=jnp.int32)

    def pick(table):
        return jnp.sum(jnp.where(c[:, None] == class_ids[None, :], table[None, :], 0), axis=1)

    tile = pick(first_tile) + (idx - pick(item_start))
    live = item < n_valid
    lo = jnp.where(live, jnp.clip(pick(starts) - tile * tm, 0, tm), 0)
    hi = jnp.where(live, jnp.clip(pick(ends) - tile * tm, 0, tm), 0)
    prev_tile = jnp.concatenate([jnp.full((1,), -1, jnp.int32), tile[:-1]])
    first = (tile != prev_tile).astype(jnp.int32)
    prev_c = jnp.concatenate([jnp.full((1,), -1, jnp.int32), c[:-1]])
    switch = (c != prev_c).astype(jnp.int32)
    group_of_class = np.arange(N_CLASSES) // len(_PAIRS)
    pair_of_class = np.arange(N_CLASSES) % len(_PAIRS)
    pairs = np.asarray(_PAIRS)
    ea = pick(jnp.asarray(group_of_class * EXPERTS_PER_GROUP + pairs[pair_of_class, 0], dtype=jnp.int32))
    eb = pick(jnp.asarray(group_of_class * EXPERTS_PER_GROUP + pairs[pair_of_class, 1], dtype=jnp.int32))
    as_i32 = lambda a: a.astype(jnp.int32)
    return as_i32(starts), tuple(as_i32(a) for a in (tile, ea, eb, lo, hi, first, switch, n_valid.reshape(1)))


def _moe_kernel(h_ref, comb_ref, x1_ref, mod_ref, gf_ref, w1_ref, w3_ref, w2_ref, y_ref, acc_ref):
    e = pl.program_id(2)

    @pl.when(e == 0)
    def _():
        acc_ref[...] = jnp.zeros_like(acc_ref)

    h = h_ref[...].astype(BF16)
    a = jnp.dot(h, w1_ref[...].astype(BF16), preferred_element_type=F32)
    b = jnp.dot(h, w3_ref[...].astype(BF16), preferred_element_type=F32)
    lane = lax.broadcasted_iota(jnp.int32, (1, LANES), 1)
    cw = jnp.sum(jnp.where(lane == e + N_EXPERT_GROUPS, comb_ref[...], 0.0), axis=1, keepdims=True)
    hid = (a * _sigmoid(a)) * b * cw
    acc_ref[...] += jnp.dot(hid.astype(BF16), w2_ref[...].astype(BF16), preferred_element_type=F32)

    @pl.when(e == N_EXPERTS - 1)
    def _():
        x2 = x1_ref[...] + mod_ref[5] * acc_ref[...]
        var = jnp.mean(x2 * x2, axis=-1, keepdims=True)
        y_ref[...] = (x2 * lax.rsqrt(var + RMS_EPS)) * gf_ref[...]


def _moe_dense(h2, comb, x1, mod4, gf, w1, w3, w2, *, tm):
    nb, rows, _ = x1.shape
    tiles = rows // tm
    tile3 = lambda b, i, e: (b, i, 0)
    if mod4.shape[2] == 1:
        mod_spec = pl.BlockSpec((None, 6, 1, D_MODEL), lambda b, i, e: (b, 0, 0, 0))
    else:
        mod_spec = pl.BlockSpec((None, 6, tm, D_MODEL), lambda b, i, e: (b, 0, i, 0))
    return pl.pallas_call(
        _moe_kernel,
        grid=(nb, tiles, N_EXPERTS),
        in_specs=[
            pl.BlockSpec((None, tm, D_MODEL), tile3),
            pl.BlockSpec((None, tm, LANES), tile3),
            pl.BlockSpec((None, tm, D_MODEL), tile3),
            mod_spec,
            pl.BlockSpec((1, D_MODEL), lambda b, i, e: (0, 0)),
            pl.BlockSpec((None, D_MODEL, D_EXPERT), lambda b, i, e: (e, 0, 0)),
            pl.BlockSpec((None, D_MODEL, D_EXPERT), lambda b, i, e: (e, 0, 0)),
            pl.BlockSpec((None, D_EXPERT, D_MODEL), lambda b, i, e: (e, 0, 0)),
        ],
        out_specs=pl.BlockSpec((None, tm, D_MODEL), tile3),
        out_shape=jax.ShapeDtypeStruct((nb, rows, D_MODEL), F32),
        scratch_shapes=[pltpu.VMEM((tm, D_MODEL), F32)],
        compiler_params=_cparams(("arbitrary", "arbitrary", "arbitrary")),
        name=f"moe_dense_{nb * rows}",
    )(h2, comb, x1, mod4, gf, w1, w3, w2)


def _s_inproj_kernel(x_ref, mod_ref, g1_ref, w_ref, z_ref):
    x = x_ref[...]
    var = jnp.mean(x * x, axis=-1, keepdims=True)
    h = (x * lax.rsqrt(var + RMS_EPS)) * (g1_ref[...] * (1.0 + mod_ref[1])) + mod_ref[0]
    h_hi, h_lo = _split_bf16(h, 2)
    w_hi, w_lo = _split_bf16(w_ref[...], 2)
    z_ref[...] = (jnp.dot(h_hi, w_hi, preferred_element_type=F32) + jnp.dot(h_lo, w_hi, preferred_element_type=F32)
                  + jnp.dot(h_hi, w_lo, preferred_element_type=F32))


def _s_inproj(x, mod_tok, g1, w_in, *, col_block):
    n = x.shape[0]
    return pl.pallas_call(
        _s_inproj_kernel,
        grid=(IN_COLS // col_block,),
        in_specs=[pl.BlockSpec((n, D_MODEL), lambda j: (0, 0)),
                  pl.BlockSpec((6, n, D_MODEL), lambda j: (0, 0, 0)),
                  pl.BlockSpec((1, D_MODEL), lambda j: (0, 0)),
                  pl.BlockSpec((D_MODEL, col_block), lambda j: (0, j))],
        out_specs=pl.BlockSpec((n, col_block), lambda j: (0, j)),
        out_shape=jax.ShapeDtypeStruct((n, IN_COLS), F32),
        compiler_params=_cparams(("arbitrary",)),
        name="inproj_sample",
    )(x, mod_tok, g1, w_in)


def _s_mid_kernel(z_ref, p0_ref, p1_ref, cw_ref, wco_ref, rq_ref, rk_ref,
                  cu_ref, pc_ref, sga_ref, q_ref, k_ref, v_ref, *, t_len):
    n = z_ref.shape[0]
    cu = z_ref[:, OFF_GC:OFF_GC + D_MODEL] * z_ref[:, OFF_U:OFF_U + D_MODEL]
    cu_ref[...] = cu
    t = lax.broadcasted_iota(jnp.int32, (n, D_MODEL), 0) & (t_len - 1)
    prev1 = jnp.where(t >= 1, pltpu.roll(cu, 1, 0), p1_ref[...])
    prev2 = jnp.where(t >= 2, pltpu.roll(cu, 2, 0), jnp.where(t == 0, p0_ref[...], p1_ref[...]))
    cw = cw_ref[...]
    conv = cw[0:1] * prev2 + cw[1:2] * prev1 + cw[2:3] * cu
    yc = jnp.dot(z_ref[:, OFF_GB:OFF_GB + D_MODEL] * conv, wco_ref[...], precision=HIGHEST,
                 preferred_element_type=F32)
    pc_ref[...] = _sigmoid(z_ref[:, OFF_GCONV:OFF_GCONV + D_MODEL]) * yc
    sga_ref[...] = _sigmoid(z_ref[:, OFF_GATTN:OFF_GATTN + D_MODEL])
    aq, bmq, bpq = rq_ref[0], rq_ref[1], rq_ref[2]
    ak, bmk, bpk = rk_ref[0], rk_ref[1], rk_ref[2]
    for c in range(ATTN_WIDTH // LANES):
        sl = slice(c * LANES, (c + 1) * LANES)
        q_ref[:, sl] = _rope_chunk(z_ref[:, OFF_Q + c * LANES:OFF_Q + (c + 1) * LANES], aq, bmq, bpq)
        k_ref[:, sl] = _rope_chunk(z_ref[:, OFF_K + c * LANES:OFF_K + (c + 1) * LANES], ak, bmk, bpk)
    v_ref[...] = z_ref[:, OFF_V:OFF_V + ATTN_WIDTH]


def _s_mid(z, p0e, p1e, conv_w, w_conv_out, rope_q, rope_k, *, t_len):
    n = z.shape[0]
    assert t_len & (t_len - 1) == 0
    full = lambda shape: pl.BlockSpec(shape, lambda i: (0,) * len(shape))
    out_shape = [jax.ShapeDtypeStruct((n, D_MODEL), F32)] * 3 + [jax.ShapeDtypeStruct((n, ATTN_WIDTH), F32)] * 3
    return pl.pallas_call(
        functools.partial(_s_mid_kernel, t_len=t_len),
        grid=(1,),
        in_specs=[full(z.shape), full(p0e.shape), full(p1e.shape), full(conv_w.shape), full(w_conv_out.shape),
                  full(rope_q.shape), full(rope_k.shape)],
        out_specs=[full((n, D_MODEL))] * 3 + [full((n, ATTN_WIDTH))] * 3,
        out_shape=out_shape,
        compiler_params=_cparams(("arbitrary",)),
        name="mid_sample",
    )(z, p0e, p1e, conv_w, w_conv_out, rope_q, rope_k)


def _head_sum(x):
    return jnp.sum(x.reshape(HEADS_PER_GROUP, HEAD_DIM, x.shape[-1]), axis=1)


def _head_expand(x):
    n = x.shape[-1]
    return jnp.broadcast_to(x[:, None, :], (HEADS_PER_GROUP, HEAD_DIM, n)).reshape(GROUP_WIDTH, n)


def _s_attn_disjoint(q_ref, kn, vn, ck_ref, cv_ref, o_ref, lse_ref, s_scr, lane, new_idx, *, n_tiles, dil, t_len):
    cls = lane & (dil - 1)
    q_all = q_ref[...]
    qsel = jnp.zeros((GROUP_WIDTH, LANES), F32)
    for t in range(t_len):
        qsel = jnp.where(cls == t, jnp.broadcast_to(q_all[:, t:t + 1], (GROUP_WIDTH, LANES)), qsel)
    smax = None
    for j in range(n_tiles):
        s = jnp.where(cls < t_len, _head_sum(ck_ref[:, j * LANES:(j + 1) * LANES] * qsel), NEG_BIG)
        s_scr[:, j * LANES:(j + 1) * LANES] = s
        smax = s if smax is None else jnp.maximum(smax, s)
    s_new = _head_sum(kn * q_all)
    m_cols = []
    m_lane = jnp.zeros((HEADS_PER_GROUP, LANES), F32)
    m_new = jnp.zeros((HEADS_PER_GROUP, t_len), F32)
    for t in range(t_len):
        mt = jnp.maximum(jnp.max(jnp.where(cls == t, smax, NEG_BIG), axis=1, keepdims=True), s_new[:, t:t + 1])
        m_cols.append(mt)
        m_lane = jnp.where(cls == t, mt, m_lane)
        m_new = jnp.where(new_idx == t, mt, m_new)
    esum = jnp.zeros((HEADS_PER_GROUP, LANES), F32)
    for j in range(n_tiles):
        e = jnp.exp(s_scr[:, j * LANES:(j + 1) * LANES] - m_lane)
        s_scr[:, j * LANES:(j + 1) * LANES] = e
        esum = esum + e
    e_new = jnp.exp(s_new - m_new)
    l_cols = []
    inv_lane = jnp.zeros((HEADS_PER_GROUP, LANES), F32)
    inv_new = jnp.zeros((HEADS_PER_GROUP, t_len), F32)
    for t in range(t_len):
        lt = jnp.sum(jnp.where(cls == t, esum, 0.0), axis=1, keepdims=True) + e_new[:, t:t + 1]
        l_cols.append(lt)
        inv_lane = jnp.where(cls == t, 1.0 / lt, inv_lane)
        inv_new = jnp.where(new_idx == t, 1.0 / lt, inv_new)
    acc = None
    for j in range(n_tiles):
        term = cv_ref[:, j * LANES:(j + 1) * LANES] * _head_expand(s_scr[:, j * LANES:(j + 1) * LANES] * inv_lane)
        acc = term if acc is None else acc + term
    o_new = vn * _head_expand(e_new * inv_new)
    for t in range(t_len):
        o_ref[:, t:t + 1] = jnp.sum(jnp.where(cls == t, acc, 0.0), axis=1, keepdims=True) + o_new[:, t:t + 1]
        lse_ref[:, t:t + 1] = m_cols[t] + jnp.log(l_cols[t])


def _s_attn_kernel(q_ref, kn_ref, vn_ref, ck_ref, cv_ref, o_ref, lse_ref, ko_ref, vo_ref, s_scr,
                   *, win, dil, t_len):
    n_tiles = win // LANES
    lane = lax.broadcasted_iota(jnp.int32, (1, LANES), 1)
    new_idx = lax.broadcasted_iota(jnp.int32, (1, t_len), 1)
    kn = kn_ref[...]
    vn = vn_ref[...]
    if dil >= t_len:
        _s_attn_disjoint(q_ref, kn, vn, ck_ref, cv_ref, o_ref, lse_ref, s_scr, lane, new_idx,
                         n_tiles=n_tiles, dil=dil, t_len=t_len)
    for t in range(t_len if dil < t_len else 0):
        qb = jnp.broadcast_to(q_ref[:, t:t + 1], (GROUP_WIDTH, LANES))
        m = None
        for j in range(n_tiles):
            pos = lane + j * LANES
            s = _head_sum(ck_ref[:, j * LANES:(j + 1) * LANES] * qb)
            s = jnp.where(pos >= t, jnp.where(((pos - t) & (dil - 1)) == 0, s, NEG_BIG), NEG_BIG)
            s_scr[:, j * LANES:(j + 1) * LANES] = s
            mj = jnp.max(s, axis=1, keepdims=True)
            m = mj if m is None else jnp.maximum(m, mj)
        s_new = _head_sum(kn * qb[:, 0:t_len])
        s_new = jnp.where(new_idx <= t, jnp.where(((t - new_idx) & (dil - 1)) == 0, s_new, NEG_BIG), NEG_BIG)
        m = jnp.maximum(m, jnp.max(s_new, axis=1, keepdims=True))
        e_new = jnp.exp(s_new - m)
        l = jnp.sum(e_new, axis=1, keepdims=True)
        acc = None
        for j in range(n_tiles):
            e = jnp.exp(s_scr[:, j * LANES:(j + 1) * LANES] - m)
            l = l + jnp.sum(e, axis=1, keepdims=True)
            term = cv_ref[:, j * LANES:(j + 1) * LANES] * _head_expand(e)
            acc = term if acc is None else acc + term
        o = jnp.sum(acc, axis=1, keepdims=True) + jnp.sum(vn * _head_expand(e_new), axis=1, keepdims=True)
        o_ref[:, t:t + 1] = o * _head_expand(1.0 / l)
        lse_ref[:, t:t + 1] = m + jnp.log(l)

    for c_ref, new, out_ref in ((ck_ref, kn, ko_ref), (cv_ref, vn, vo_ref)):
        out_ref[...] = pltpu.roll(c_ref[...], win - t_len, 1)
        out_ref[:, win - t_len:win] = new


def _s_attn_group(q_t, kn_t, vn_t, cache_k, cache_v, g):
    bsz, _, t_len = q_t.shape
    win_full, dil = DIL_GROUPS[g]
    win = cache_k.shape[2]
    assert win == win_full and win == (KEYS_PER_QUERY - 1) * dil and win % LANES == 0
    assert dil & (dil - 1) == 0 and LANES % dil == 0
    grp = lambda b: (b, g, 0)
    per_b = lambda b: (b, 0, 0)
    return pl.pallas_call(
        functools.partial(_s_attn_kernel, win=win, dil=dil, t_len=t_len),
        grid=(bsz,),
        in_specs=[pl.BlockSpec((None, GROUP_WIDTH, t_len), grp)] * 3
        + [pl.BlockSpec((None, GROUP_WIDTH, win), per_b)] * 2,
        out_specs=[pl.BlockSpec((None, GROUP_WIDTH, t_len), per_b),
                   pl.BlockSpec((None, HEADS_PER_GROUP, t_len), per_b),
                   pl.BlockSpec((None, GROUP_WIDTH, win), per_b),
                   pl.BlockSpec((None, GROUP_WIDTH, win), per_b)],
        out_shape=[jax.ShapeDtypeStruct((bsz, GROUP_WIDTH, t_len), F32),
                   jax.ShapeDtypeStruct((bsz, HEADS_PER_GROUP, t_len), F32),
                   jax.ShapeDtypeStruct((bsz, GROUP_WIDTH, win), F32),
                   jax.ShapeDtypeStruct((bsz, GROUP_WIDTH, win), F32)],
        scratch_shapes=[pltpu.VMEM((HEADS_PER_GROUP, win), F32)],
        compiler_params=_cparams(("arbitrary",)),
        name=f"attn_sample_g{g}",
    )(q_t, kn_t, vn_t, cache_k, cache_v)


TM_INPROJ = 512
TM_CONVPROJ = 1024
TM_POST = 512
TM_EXPERT = 256
ROWS_PER_DMA_STEP = 256
ATTN_ROWS_PER_STEP = 1024
S_COL_BLOCK = 512


def _to_state(a_t):
    b, _, length = a_t.shape
    return jnp.transpose(a_t.reshape(b, HEADS_PER_GROUP, HEAD_DIM, length), (0, 3, 1, 2))[None]


def _from_state(a):
    b, length = a.shape[0], a.shape[1]
    return jnp.transpose(a, (0, 2, 3, 1)).reshape(b, GROUP_WIDTH, length)


def kernel(x_prompt, x_sample, cache_k1, cache_v1, cache_k2, cache_v2, cache_k3, cache_v3, state_conv,
           c_prompt, c_sample, norm1_g, norm2_g, normf_g, w_ada, b_ada, w_in, conv_w, w_conv_out,
           w_attn_out, w_o, w_rg, b_rg, w_re, b_re, w1, w3, w2):
    depth = w_in.shape[0]
    assert depth == 1
    bsz, seq, _ = x_prompt.shape
    dbsz, t_len, _ = x_sample.shape
    n_s = dbsz * t_len
    l = 0

    w_a_bf = w_in[l][:, :OFF_Q].astype(BF16)
    w_g_bf = w_in[l][:, OFF_GCONV:].astype(BF16)
    w_qkv_bf = w_in[l][:, OFF_Q:OFF_GCONV].astype(BF16)
    wco_bf = w_conv_out[l].astype(BF16)
    wao_bf = w_attn_out[l].astype(BF16)
    wo_bf = w_o[l].astype(BF16)
    w1_e = w1[l].reshape(N_EXPERTS, D_MODEL, D_EXPERT)
    w3_e = w3[l].reshape(N_EXPERTS, D_MODEL, D_EXPERT)
    w2_e = w2[l].reshape(N_EXPERTS, D_EXPERT, D_MODEL)
    g1 = norm1_g[l].reshape(1, D_MODEL)
    g2 = norm2_g[l].reshape(1, D_MODEL)
    gf = normf_g.reshape(1, D_MODEL)
    n_route = N_EXPERT_GROUPS + N_EXPERTS
    wr = jnp.pad(jnp.concatenate([w_rg[l], w_re[l]], axis=1), ((0, 0), (0, LANES - n_route)))
    br = jnp.pad(jnp.concatenate([b_rg[l], b_re[l]]), (0, LANES - n_route)).reshape(1, LANES)
    head_of_lane = np.arange(GROUP_WIDTH) // HEAD_DIM
    expand_bf = jnp.asarray((np.arange(LANES)[:, None] == head_of_lane[None, :]).astype(np.float32), dtype=BF16)

    mod = _adaln(jnp.concatenate([c_prompt, c_sample], axis=0), w_ada[l], b_ada[l])
    mod_p4 = mod[:bsz].reshape(bsz, 6, 1, D_MODEL)
    mod_tok = jnp.repeat(mod[bsz:].reshape(dbsz, 1, 6, D_MODEL), t_len, axis=1)
    mod_tok = jnp.transpose(mod_tok.reshape(n_s, 6, D_MODEL), (1, 0, 2))
    mod_s4 = mod_tok[None]

    pos_p = jnp.arange(seq, dtype=jnp.int32)
    rope_q_p = _rope_tables(pos_p, HEAD_DIM ** -0.5)
    rope_k_p = _rope_tables(pos_p, 1.0)
    pc_p, sga_p, conv_p = _convproj_prompt(x_prompt, mod_p4, g1, w_a_bf, w_g_bf,
                                           conv_w[l], wco_bf, tm=TM_CONVPROJ)
    outs = _qkvproj_prompt(x_prompt, mod_p4, g1, w_qkv_bf, rope_q_p, rope_k_p, tm=TM_INPROJ)
    qkv_p = outs[0:3 * N_DIL]
    states_p = outs[3 * N_DIL:]
    o_p, lse_p = [], []
    for g in range(N_DIL):
        o_g, lse_g = _attn_prompt_group(qkv_p[3 * g], qkv_p[3 * g + 1], qkv_p[3 * g + 2], g, max_rows=ATTN_ROWS_PER_STEP)
        o_p.append(o_g)
        lse_p.append(lse_g)
    x1_p, h2rows_p, meta_p, route_p, cnt_p = _post(o_p, lse_p, pc_p, sga_p, x_prompt, mod_p4, wao_bf, wo_bf, g2, wr, br,
                                                   expand_bf, tm=TM_POST, full_precision=False, routed=True)
    n_p = bsz * seq
    route = route_p.reshape(n_p // TM_POST, SUBLANES, TM_POST)
    starts, tables = _routing_tables(cnt_p[0, :N_CLASSES], n=n_p, tm=TM_EXPERT)
    hs = _dispatch_rows(h2rows_p.reshape(n_p * H_ROWS, LANES), route, starts, rows_per_step=ROWS_PER_DMA_STEP)
    yp = _experts_routed(hs, tables, w1_e, w3_e, w2_e, tm=TM_EXPERT)
    y_p = _combine_final(yp, route, starts, x1_p.reshape(n_p, D_MODEL), meta_p.reshape(n_p, LANES), mod_p4, gf,
                         rows_per_step=ROWS_PER_DMA_STEP, rows_per_mod=seq).reshape(bsz, seq, D_MODEL)

    pos_s = PAST_LEN + jnp.arange(t_len, dtype=jnp.int32)
    rope_q_s = jnp.tile(_rope_tables(pos_s, HEAD_DIM ** -0.5), (1, dbsz, 1))
    rope_k_s = jnp.tile(_rope_tables(pos_s, 1.0), (1, dbsz, 1))
    xs = x_sample.reshape(n_s, D_MODEL)
    z_s = _s_inproj(xs, mod_tok, g1, w_in[l], col_block=S_COL_BLOCK)
    past = state_conv[l]
    p0e = jnp.repeat(past[:, 0], t_len, axis=0)
    p1e = jnp.repeat(past[:, 1], t_len, axis=0)
    cu_s, pc_s, sga_s, q_s, k_s, v_s = _s_mid(z_s, p0e, p1e, conv_w[l], w_conv_out[l], rope_q_s, rope_k_s,
                                              t_len=t_len)
    to_cols = lambda a: jnp.transpose(a.reshape(dbsz, t_len, ATTN_WIDTH), (0, 2, 1))
    q_t, kn_t, vn_t = to_cols(q_s), to_cols(k_s), to_cols(v_s)
    caches = ((cache_k1, cache_v1), (cache_k2, cache_v2), (cache_k3, cache_v3))
    o_s, lse_s, kv_s = [], [], []
    for g, (ck, cv) in enumerate(caches):
        o_g, lse_g, ko, vo = _s_attn_group(q_t, kn_t, vn_t, _from_state(ck[l]), _from_state(cv[l]), g)
        o_s.append(jnp.transpose(o_g, (0, 2, 1)).reshape(1, 1, n_s, GROUP_WIDTH))
        lse_rows = jnp.transpose(lse_g, (0, 2, 1)).reshape(n_s, HEADS_PER_GROUP)
        lse_s.append(jnp.pad(lse_rows, ((0, 0), (0, LANES - HEADS_PER_GROUP))).reshape(1, 1, n_s, LANES))
        kv_s += [_to_state(ko), _to_state(vo)]
    x1_s, h2_s, comb_s = _post(o_s, lse_s, pc_s[None], sga_s[None], xs[None], mod_s4, w_attn_out[l], w_o[l], g2, wr, br,
                               expand_bf, tm=n_s, full_precision=True, routed=False)
    y_s = _moe_dense(h2_s, comb_s, x1_s, mod_s4, gf, w1_e, w3_e, w2_e, tm=n_s)

    conv_s = cu_s.reshape(dbsz, t_len, D_MODEL)[:, t_len - (CONV_K - 1):]
    return (y_p, y_s.reshape(dbsz, t_len, D_MODEL),
            *[_to_state(a) for a in states_p], conv_p.reshape(1, bsz, CONV_K - 1, D_MODEL),
            *kv_s, conv_s.reshape(1, dbsz, CONV_K - 1, D_MODEL))
```
